```python
import jax, jax.numpy as jnp
from jax import lax
import numpy as np

D_MODEL = 2048
BATCH = 8
SEQ = 4096
DEPTH = 2

MIX_WIDTH = D_MODEL
A_WIDTH = MIX_WIDTH // 2
A_HEAD_DIM = 128
A_HEADS = A_WIDTH // A_HEAD_DIM
B_WIDTH = MIX_WIDTH - A_WIDTH
B_CONV_WIDTH = 31
C_WIDTH = MIX_WIDTH // 2
C_CONV_WIDTH = 3
D_WIDTH = MIX_WIDTH - C_WIDTH
POOL_WINDOWS = (2, 4, 8, 16)
D_GROUPS = len(POOL_WINDOWS)
D_GROUP_DIM = D_WIDTH // D_GROUPS
D_FF = 5632
FFN_CONV_WIDTH = 3
QBLOCK = 128
LN_EPS = 1e-5
DEEPNORM_ALPHA = (2.0 * DEPTH) ** 0.25
DEEPNORM_BETA = (8.0 * DEPTH) ** -0.25
N_EVEN = (DEPTH + 1) // 2
N_ODD = DEPTH // 2

kernel_name = "hybrid_stickbreak_conformer_shortconv_pool_deepnorm"


def layer_norm(x, g, b):
    xf = x.astype(jnp.float32)
    mu = jnp.mean(xf, axis=-1, keepdims=True)
    var = jnp.mean(jnp.square(xf - mu), axis=-1, keepdims=True)
    y = (xf - mu) * lax.rsqrt(var + LN_EPS)
    return (y * g.astype(jnp.float32) + b.astype(jnp.float32)).astype(x.dtype)


def causal_dwconv(x, w):
    K, C = w.shape
    return lax.conv_general_dilated(
        x, w[:, None, :].astype(x.dtype), window_strides=(1,), padding=((K - 1, 0),),
        dimension_numbers=("NWC", "WIO", "NWC"), feature_group_count=C)


def stick_breaking_attention(q, k, v):
    Bn, S, H, dh = q.shape
    nb = S // QBLOCK
    scale = 1.0 / float(np.sqrt(dh))
    kh = k.transpose(0, 2, 1, 3)
    vh = v.transpose(0, 2, 1, 3)
    qb = q.transpose(0, 2, 1, 3).reshape(Bn, H, nb, QBLOCK, dh).transpose(2, 0, 1, 3, 4)
    key_pos = jnp.arange(S)
    starts = jnp.arange(nb) * QBLOCK

    def block(args):
        qblk, start = args
        z = jnp.einsum("bhqd,bhkd->bhqk", qblk, kh,
                       preferred_element_type=jnp.float32) * scale
        qpos = start + jnp.arange(QBLOCK)
        mask = key_pos[None, :] < qpos[:, None]
        log_skip = jnp.where(mask, jax.nn.log_sigmoid(-z), 0.0)
        excl = lax.cumsum(log_skip, axis=3, reverse=True) - log_skip
        w = jnp.where(mask, jnp.exp(jax.nn.log_sigmoid(z) + excl), 0.0)
        return jnp.einsum("bhqk,bhkd->bhqd", w.astype(vh.dtype), vh)

    o = lax.map(block, (qb, starts))
    return o.transpose(1, 0, 3, 2, 4).reshape(Bn, S, H * dh)


def causal_pool_diff(p, window):
    S = p.shape[1]
    cs = jnp.cumsum(p.astype(jnp.float32), axis=1)
    cs_pad = jnp.pad(cs, ((0, 0), (window, 0), (0, 0)))
    wsum = cs_pad[:, window:] - cs_pad[:, :S]
    count = jnp.minimum(jnp.arange(S) + 1, window).astype(jnp.float32)
    return (wsum / count[None, :, None] - p.astype(jnp.float32)).astype(p.dtype)


def even_mixer(x, w_in, dw_w, dw_b, bn_g, bn_b, w_out):
    Bn, S, _ = x.shape
    h = x @ w_in
    q, k, v, a, g = jnp.split(h, [A_WIDTH, 2 * A_WIDTH, 3 * A_WIDTH, 3 * A_WIDTH + B_WIDTH], axis=-1)
    q = q.reshape(Bn, S, A_HEADS, A_HEAD_DIM)
    k = k.reshape(Bn, S, A_HEADS, A_HEAD_DIM)
    v = v.reshape(Bn, S, A_HEADS, A_HEAD_DIM)
    o_a = stick_breaking_attention(q, k, v)
    u = a * jax.nn.sigmoid(g)
    u = causal_dwconv(u, dw_w) + dw_b
    u = jax.nn.silu(layer_norm(u, bn_g, bn_b))
    return jnp.concatenate([o_a, u], axis=-1) @ w_out


def odd_mixer(x, w_in, conv_w, pool_w, pool_scale, w_out):
    Bn, S, _ = x.shape
    h = x @ w_in
    cb, cc, ch, p = jnp.split(h, [C_WIDTH, 2 * C_WIDTH, 3 * C_WIDTH], axis=-1)
    y_c = cb * causal_dwconv(cc * ch, conv_w)
    pg = p.reshape(Bn, S, D_GROUPS, D_GROUP_DIM)
    diffs = jnp.stack([causal_pool_diff(pg[:, :, i], POOL_WINDOWS[i]) for i in range(D_GROUPS)], axis=2)
    y_d = jnp.einsum("bsgc,gcd->bsgd", diffs, pool_w).reshape(Bn, S, D_WIDTH) * pool_scale
    return jnp.concatenate([y_c, y_d], axis=-1) @ w_out


def conv_ffn(x, w_up, conv_w, conv_b, w_down):
    g, u = jnp.split(x @ w_up, 2, axis=-1)
    g = causal_dwconv(g, conv_w) + conv_b
    return (jax.nn.silu(g) * u) @ w_down


def _normal(key, shape, scale):
    return jax.random.normal(key, shape, jnp.float32) * scale


def _fwd_setup_inputs(seed: int = 0) -> dict:
    key = jax.random.key(seed)
    ks = jax.random.split(key, 20)
    D = D_MODEL
    ev_in_cols = 3 * A_WIDTH + 2 * B_WIDTH
    od_in_cols = 3 * C_WIDTH + D_WIDTH
    return {
        "x": _normal(ks[0], (BATCH, SEQ, D), 1.0),
        "ev_w_in": _normal(ks[1], (N_EVEN, D, ev_in_cols), D ** -0.5),
        "ev_dw_w": _normal(ks[2], (N_EVEN, B_CONV_WIDTH, B_WIDTH), B_CONV_WIDTH ** -0.5),
        "ev_dw_b": _normal(ks[3], (N_EVEN, B_WIDTH), 0.02),
        "ev_bn_g": 1.0 + _normal(ks[4], (N_EVEN, B_WIDTH), 0.02),
        "ev_bn_b": _normal(ks[5], (N_EVEN, B_WIDTH), 0.02),
        "ev_w_out": _normal(ks[6], (N_EVEN, MIX_WIDTH, D), MIX_WIDTH ** -0.5 * DEEPNORM_BETA),
        "od_w_in": _normal(ks[7], (N_ODD, D, od_in_cols), D ** -0.5),
        "od_conv_w": _normal(ks[8], (N_ODD, C_CONV_WIDTH, C_WIDTH), C_CONV_WIDTH ** -0.5),
        "od_pool_w": _normal(ks[9], (N_ODD, D_GROUPS, D_GROUP_DIM, D_GROUP_DIM), D_GROUP_DIM ** -0.5),
        "od_pool_scale": 1.0 + _normal(ks[10], (N_ODD, D_WIDTH), 0.02),
        "od_w_out": _normal(ks[11], (N_ODD, MIX_WIDTH, D), MIX_WIDTH ** -0.5 * DEEPNORM_BETA),
        "ffn_w_up": _normal(ks[12], (DEPTH, D, 2 * D_FF), D ** -0.5),
        "ffn_conv_w": _normal(ks[13], (DEPTH, FFN_CONV_WIDTH, D_FF), FFN_CONV_WIDTH ** -0.5),
        "ffn_conv_b": _normal(ks[14], (DEPTH, D_FF), 0.02),
        "ffn_w_down": _normal(ks[15], (DEPTH, D_FF, D), D_FF ** -0.5 * DEEPNORM_BETA),
        "ln_g": 1.0 + _normal(ks[16], (DEPTH, 2, D), 0.02),
        "ln_b": _normal(ks[17], (DEPTH, 2, D), 0.02),
    }


def _fwd_reference(x, ev_w_in, ev_dw_w, ev_dw_b, ev_bn_g, ev_bn_b, ev_w_out,
              od_w_in, od_conv_w, od_pool_w, od_pool_scale, od_w_out,
              ffn_w_up, ffn_conv_w, ffn_conv_b, ffn_w_down, ln_g, ln_b):
    for i in range(DEPTH):
        j = i // 2
        if i % 2 == 0:
            y = even_mixer(x, ev_w_in[j], ev_dw_w[j], ev_dw_b[j], ev_bn_g[j], ev_bn_b[j], ev_w_out[j])
        else:
            y = odd_mixer(x, od_w_in[j], od_conv_w[j], od_pool_w[j], od_pool_scale[j], od_w_out[j])
        x = layer_norm(DEEPNORM_ALPHA * x + y, ln_g[i, 0], ln_b[i, 0])
        y = conv_ffn(x, ffn_w_up[i], ffn_conv_w[i], ffn_conv_b[i], ffn_w_down[i])
        x = layer_norm(DEEPNORM_ALPHA * x + y, ln_g[i, 1], ln_b[i, 1])
    return x


import jax as _jax
import jax.numpy as _jnp

TWIN_FORMAT = 'train_step'
FWD_PARAMS = ['x', 'ev_w_in', 'ev_dw_w', 'ev_dw_b', 'ev_bn_g', 'ev_bn_b', 'ev_w_out', 'od_w_in', 'od_conv_w', 'od_pool_w', 'od_pool_scale', 'od_w_out', 'ffn_w_up', 'ffn_conv_w', 'ffn_conv_b', 'ffn_w_down', 'ln_g', 'ln_b']
TWIN_WEIGHTS = ['ev_w_in', 'ev_dw_w', 'ev_dw_b', 'ev_bn_g', 'ev_bn_b', 'ev_w_out', 'od_w_in', 'od_conv_w', 'od_pool_w', 'od_pool_scale', 'od_w_out', 'ffn_w_up', 'ffn_conv_w', 'ffn_conv_b', 'ffn_w_down', 'ln_g', 'ln_b']
TWIN_DIFF_INPUT = 'x'
TWIN_INPUTS = ['x', 'ev_w_in', 'ev_dw_w', 'ev_dw_b', 'ev_bn_g', 'ev_bn_b', 'ev_w_out', 'od_w_in', 'od_conv_w', 'od_pool_w', 'od_pool_scale', 'od_w_out', 'ffn_w_up', 'ffn_conv_w', 'ffn_conv_b', 'ffn_w_down', 'ln_g', 'ln_b', 'loss_target', 'm_ev_w_in', 'm_ev_dw_w', 'm_ev_dw_b', 'm_ev_bn_g', 'm_ev_bn_b', 'm_ev_w_out', 'm_od_w_in', 'm_od_conv_w', 'm_od_pool_w', 'm_od_pool_scale', 'm_od_w_out', 'm_ffn_w_up', 'm_ffn_conv_w', 'm_ffn_conv_b', 'm_ffn_w_down', 'm_ln_g', 'm_ln_b', 'v_ev_w_in', 'v_ev_dw_w', 'v_ev_dw_b', 'v_ev_bn_g', 'v_ev_bn_b', 'v_ev_w_out', 'v_od_w_in', 'v_od_conv_w', 'v_od_pool_w', 'v_od_pool_scale', 'v_od_w_out', 'v_ffn_w_up', 'v_ffn_conv_w', 'v_ffn_conv_b', 'v_ffn_w_down', 'v_ln_g', 'v_ln_b']
TWIN_OUTPUTS = ['loss', 'grad_x', 'grad_ev_w_in', 'grad_ev_dw_w', 'grad_ev_dw_b', 'grad_ev_bn_g', 'grad_ev_bn_b', 'grad_ev_w_out', 'grad_od_w_in', 'grad_od_conv_w', 'grad_od_pool_w', 'grad_od_pool_scale', 'grad_od_w_out', 'grad_ffn_w_up', 'grad_ffn_conv_w', 'grad_ffn_conv_b', 'grad_ffn_w_down', 'grad_ln_g', 'grad_ln_b', 'delta_ev_w_in', 'delta_ev_dw_w', 'delta_ev_dw_b', 'delta_ev_bn_g', 'delta_ev_bn_b', 'delta_ev_w_out', 'delta_od_w_in', 'delta_od_conv_w', 'delta_od_pool_w', 'delta_od_pool_scale', 'delta_od_w_out', 'delta_ffn_w_up', 'delta_ffn_conv_w', 'delta_ffn_conv_b', 'delta_ffn_w_down', 'delta_ln_g', 'delta_ln_b', 'new_m_ev_w_in', 'new_m_ev_dw_w', 'new_m_ev_dw_b', 'new_m_ev_bn_g', 'new_m_ev_bn_b', 'new_m_ev_w_out', 'new_m_od_w_in', 'new_m_od_conv_w', 'new_m_od_pool_w', 'new_m_od_pool_scale', 'new_m_od_w_out', 'new_m_ffn_w_up', 'new_m_ffn_conv_w', 'new_m_ffn_conv_b', 'new_m_ffn_w_down', 'new_m_ln_g', 'new_m_ln_b', 'new_v_ev_w_in', 'new_v_ev_dw_w', 'new_v_ev_dw_b', 'new_v_ev_bn_g', 'new_v_ev_bn_b', 'new_v_ev_w_out', 'new_v_od_w_in', 'new_v_od_conv_w', 'new_v_od_pool_w', 'new_v_od_pool_scale', 'new_v_od_w_out', 'new_v_ffn_w_up', 'new_v_ffn_conv_w', 'new_v_ffn_conv_b', 'new_v_ffn_w_down', 'new_v_ln_g', 'new_v_ln_b']
TWIN_LEAF_KINDS = {'loss': 'loss', 'grad_x': 'grad_x', 'grad_ev_w_in': 'grad_w', 'grad_ev_dw_w': 'grad_w', 'grad_ev_dw_b': 'grad_w', 'grad_ev_bn_g': 'grad_w', 'grad_ev_bn_b': 'grad_w', 'grad_ev_w_out': 'grad_w', 'grad_od_w_in': 'grad_w', 'grad_od_conv_w': 'grad_w', 'grad_od_pool_w': 'grad_w', 'grad_od_pool_scale': 'grad_w', 'grad_od_w_out': 'grad_w', 'grad_ffn_w_up': 'grad_w', 'grad_ffn_conv_w': 'grad_w', 'grad_ffn_conv_b': 'grad_w', 'grad_ffn_w_down': 'grad_w', 'grad_ln_g': 'grad_w', 'grad_ln_b': 'grad_w', 'delta_ev_w_in': 'delta_w', 'delta_ev_dw_w': 'delta_w', 'delta_ev_dw_b': 'delta_w', 'delta_ev_bn_g': 'delta_w', 'delta_ev_bn_b': 'delta_w', 'delta_ev_w_out': 'delta_w', 'delta_od_w_in': 'delta_w', 'delta_od_conv_w': 'delta_w', 'delta_od_pool_w': 'delta_w', 'delta_od_pool_scale': 'delta_w', 'delta_od_w_out': 'delta_w', 'delta_ffn_w_up': 'delta_w', 'delta_ffn_conv_w': 'delta_w', 'delta_ffn_conv_b': 'delta_w', 'delta_ffn_w_down': 'delta_w', 'delta_ln_g': 'delta_w', 'delta_ln_b': 'delta_w', 'new_m_ev_w_in': 'new_m', 'new_m_ev_dw_w': 'new_m', 'new_m_ev_dw_b': 'new_m', 'new_m_ev_bn_g': 'new_m', 'new_m_ev_bn_b': 'new_m', 'new_m_ev_w_out': 'new_m', 'new_m_od_w_in': 'new_m', 'new_m_od_conv_w': 'new_m', 'new_m_od_pool_w': 'new_m', 'new_m_od_pool_scale': 'new_m', 'new_m_od_w_out': 'new_m', 'new_m_ffn_w_up': 'new_m', 'new_m_ffn_conv_w': 'new_m', 'new_m_ffn_conv_b': 'new_m', 'new_m_ffn_w_down': 'new_m', 'new_m_ln_g': 'new_m', 'new_m_ln_b': 'new_m', 'new_v_ev_w_in': 'new_v', 'new_v_ev_dw_w': 'new_v', 'new_v_ev_dw_b': 'new_v', 'new_v_ev_bn_g': 'new_v', 'new_v_ev_bn_b': 'new_v', 'new_v_ev_w_out': 'new_v', 'new_v_od_w_in': 'new_v', 'new_v_od_conv_w': 'new_v', 'new_v_od_pool_w': 'new_v', 'new_v_od_pool_scale': 'new_v', 'new_v_od_w_out': 'new_v', 'new_v_ffn_w_up': 'new_v', 'new_v_ffn_conv_w': 'new_v', 'new_v_ffn_conv_b': 'new_v', 'new_v_ffn_w_down': 'new_v', 'new_v_ln_g': 'new_v', 'new_v_ln_b': 'new_v'}


def _forward(args):
    return _fwd_reference(*[args[k] for k in FWD_PARAMS])


def _output_shape():
    def fwd():
        inp = _fwd_setup_inputs(0)
        return _fwd_reference(*[inp[k] for k in FWD_PARAMS])
    out = _jax.eval_shape(fwd)
    return out.shape, out.dtype

N_MICROBATCH = 1
ADAM_LR = 0.001
ADAM_B1 = 0.9
ADAM_B2 = 0.999
ADAM_EPS = 1e-08
ADAM_WD = 0.01
ADAM_STEP = 10
PER_EXAMPLE_BATCH_AXIS = {'x': 0, 'loss_target': 0}
SHARED_INPUTS = []
_WEIGHT_DTYPES = {'ev_w_in': _jnp.float32, 'ev_dw_w': _jnp.float32, 'ev_dw_b': _jnp.float32, 'ev_bn_g': _jnp.float32, 'ev_bn_b': _jnp.float32, 'ev_w_out': _jnp.float32, 'od_w_in': _jnp.float32, 'od_conv_w': _jnp.float32, 'od_pool_w': _jnp.float32, 'od_pool_scale': _jnp.float32, 'od_w_out': _jnp.float32, 'ffn_w_up': _jnp.float32, 'ffn_conv_w': _jnp.float32, 'ffn_conv_b': _jnp.float32, 'ffn_w_down': _jnp.float32, 'ln_g': _jnp.float32, 'ln_b': _jnp.float32}
MOMENT_SCALE = {'ev_w_in': 1.536821e-02, 'ev_dw_w': 2.135014e-02, 'ev_dw_b': 7.183523e-02, 'ev_bn_g': 3.414005e-02, 'ev_bn_b': 4.230093e-02, 'ev_w_out': 4.649280e-02, 'od_w_in': 2.925875e-02, 'od_conv_w': 3.082926e-02, 'od_pool_w': 2.665192e-02, 'od_pool_scale': 2.752368e-02, 'od_w_out': 5.697569e-02, 'ffn_w_up': 1.208107e-02, 'ffn_conv_w': 1.231918e-02, 'ffn_conv_b': 1.176212e-02, 'ffn_w_down': 3.943017e-02, 'ln_g': 8.024295e+00, 'ln_b': 5.170855e-01}


def _to_microbatches(a, axis):
    t = _jnp.moveaxis(a, axis, 0)
    t = t.reshape((N_MICROBATCH, t.shape[0] // N_MICROBATCH) + t.shape[1:])
    return _jnp.moveaxis(t, 1, axis + 1)


def setup_inputs(seed: int = 0) -> dict:
    inp = _fwd_setup_inputs(seed)
    key = _jax.random.fold_in(_jax.random.key(seed), 7919)
    shape, _ = _output_shape()
    out = dict(inp)
    out["loss_target"] = _jax.random.normal(_jax.random.fold_in(key, 0), shape, _jnp.float32)
    for i, name in enumerate(TWIN_WEIGHTS):
        w = inp[name].astype(_jnp.float32)
        if MOMENT_SCALE is None:
            s = _jnp.sqrt(_jnp.mean(_jnp.square(w)) + 1e-30)
        else:
            s = MOMENT_SCALE[name]
        km, kv = _jax.random.split(_jax.random.fold_in(key, i + 1))
        out[name] = w
        out["m_" + name] = s * _jax.random.normal(km, w.shape, _jnp.float32)
        out["v_" + name] = (s * s) * _jax.random.uniform(kv, w.shape, _jnp.float32, 0.5, 1.5)
    if N_MICROBATCH > 1:
        for name, axis in PER_EXAMPLE_BATCH_AXIS.items():
            out[name] = _to_microbatches(out[name], axis)
    return {'x': out['x'], 'ev_w_in': out['ev_w_in'], 'ev_dw_w': out['ev_dw_w'], 'ev_dw_b': out['ev_dw_b'], 'ev_bn_g': out['ev_bn_g'], 'ev_bn_b': out['ev_bn_b'], 'ev_w_out': out['ev_w_out'], 'od_w_in': out['od_w_in'], 'od_conv_w': out['od_conv_w'], 'od_pool_w': out['od_pool_w'], 'od_pool_scale': out['od_pool_scale'], 'od_w_out': out['od_w_out'], 'ffn_w_up': out['ffn_w_up'], 'ffn_conv_w': out['ffn_conv_w'], 'ffn_conv_b': out['ffn_conv_b'], 'ffn_w_down': out['ffn_w_down'], 'ln_g': out['ln_g'], 'ln_b': out['ln_b'], 'loss_target': out['loss_target'], 'm_ev_w_in': out['m_ev_w_in'], 'm_ev_dw_w': out['m_ev_dw_w'], 'm_ev_dw_b': out['m_ev_dw_b'], 'm_ev_bn_g': out['m_ev_bn_g'], 'm_ev_bn_b': out['m_ev_bn_b'], 'm_ev_w_out': out['m_ev_w_out'], 'm_od_w_in': out['m_od_w_in'], 'm_od_conv_w': out['m_od_conv_w'], 'm_od_pool_w': out['m_od_pool_w'], 'm_od_pool_scale': out['m_od_pool_scale'], 'm_od_w_out': out['m_od_w_out'], 'm_ffn_w_up': out['m_ffn_w_up'], 'm_ffn_conv_w': out['m_ffn_conv_w'], 'm_ffn_conv_b': out['m_ffn_conv_b'], 'm_ffn_w_down': out['m_ffn_w_down'], 'm_ln_g': out['m_ln_g'], 'm_ln_b': out['m_ln_b'], 'v_ev_w_in': out['v_ev_w_in'], 'v_ev_dw_w': out['v_ev_dw_w'], 'v_ev_dw_b': out['v_ev_dw_b'], 'v_ev_bn_g': out['v_ev_bn_g'], 'v_ev_bn_b': out['v_ev_bn_b'], 'v_ev_w_out': out['v_ev_w_out'], 'v_od_w_in': out['v_od_w_in'], 'v_od_conv_w': out['v_od_conv_w'], 'v_od_pool_w': out['v_od_pool_w'], 'v_od_pool_scale': out['v_od_pool_scale'], 'v_od_w_out': out['v_od_w_out'], 'v_ffn_w_up': out['v_ffn_w_up'], 'v_ffn_conv_w': out['v_ffn_conv_w'], 'v_ffn_conv_b': out['v_ffn_conv_b'], 'v_ffn_w_down': out['v_ffn_w_down'], 'v_ln_g': out['v_ln_g'], 'v_ln_b': out['v_ln_b']}


def _loss(weights, diff, rest, loss_target):
    with _jax.named_scope("forward"):
        args = {**rest, TWIN_DIFF_INPUT: diff, **{k: w.astype(_WEIGHT_DTYPES[k]) for k, w in weights.items()}}
        y = _forward(args)
    with _jax.named_scope("loss_head"):
        err = _jnp.square(y.astype(_jnp.float32) - loss_target)
        return 0.5 * _jnp.sum(_jnp.mean(err, axis=-1)) if err.ndim else 0.5 * err


def _adamw(w, g, m, v):
    m = ADAM_B1 * m + (1.0 - ADAM_B1) * g
    v = ADAM_B2 * v + (1.0 - ADAM_B2) * _jnp.square(g)
    m_hat = m / (1.0 - ADAM_B1 ** ADAM_STEP)
    v_hat = v / (1.0 - ADAM_B2 ** ADAM_STEP)
    delta = -ADAM_LR * (m_hat / (_jnp.sqrt(v_hat) + ADAM_EPS) + ADAM_WD * w)
    return delta, m, v


def reference(x, ev_w_in, ev_dw_w, ev_dw_b, ev_bn_g, ev_bn_b, ev_w_out, od_w_in, od_conv_w, od_pool_w, od_pool_scale, od_w_out, ffn_w_up, ffn_conv_w, ffn_conv_b, ffn_w_down, ln_g, ln_b, loss_target, m_ev_w_in, m_ev_dw_w, m_ev_dw_b, m_ev_bn_g, m_ev_bn_b, m_ev_w_out, m_od_w_in, m_od_conv_w, m_od_pool_w, m_od_pool_scale, m_od_w_out, m_ffn_w_up, m_ffn_conv_w, m_ffn_conv_b, m_ffn_w_down, m_ln_g, m_ln_b, v_ev_w_in, v_ev_dw_w, v_ev_dw_b, v_ev_bn_g, v_ev_bn_b, v_ev_w_out, v_od_w_in, v_od_conv_w, v_od_pool_w, v_od_pool_scale, v_od_w_out, v_ffn_w_up, v_ffn_conv_w, v_ffn_conv_b, v_ffn_w_down, v_ln_g, v_ln_b):
    given = dict(x=x, ev_w_in=ev_w_in, ev_dw_w=ev_dw_w, ev_dw_b=ev_dw_b, ev_bn_g=ev_bn_g, ev_bn_b=ev_bn_b, ev_w_out=ev_w_out, od_w_in=od_w_in, od_conv_w=od_conv_w, od_pool_w=od_pool_w, od_pool_scale=od_pool_scale, od_w_out=od_w_out, ffn_w_up=ffn_w_up, ffn_conv_w=ffn_conv_w, ffn_conv_b=ffn_conv_b, ffn_w_down=ffn_w_down, ln_g=ln_g, ln_b=ln_b, loss_target=loss_target, m_ev_w_in=m_ev_w_in, m_ev_dw_w=m_ev_dw_w, m_ev_dw_b=m_ev_dw_b, m_ev_bn_g=m_ev_bn_g, m_ev_bn_b=m_ev_bn_b, m_ev_w_out=m_ev_w_out, m_od_w_in=m_od_w_in, m_od_conv_w=m_od_conv_w, m_od_pool_w=m_od_pool_w, m_od_pool_scale=m_od_pool_scale, m_od_w_out=m_od_w_out, m_ffn_w_up=m_ffn_w_up, m_ffn_conv_w=m_ffn_conv_w, m_ffn_conv_b=m_ffn_conv_b, m_ffn_w_down=m_ffn_w_down, m_ln_g=m_ln_g, m_ln_b=m_ln_b, v_ev_w_in=v_ev_w_in, v_ev_dw_w=v_ev_dw_w, v_ev_dw_b=v_ev_dw_b, v_ev_bn_g=v_ev_bn_g, v_ev_bn_b=v_ev_bn_b, v_ev_w_out=v_ev_w_out, v_od_w_in=v_od_w_in, v_od_conv_w=v_od_conv_w, v_od_pool_w=v_od_pool_w, v_od_pool_scale=v_od_pool_scale, v_od_w_out=v_od_w_out, v_ffn_w_up=v_ffn_w_up, v_ffn_conv_w=v_ffn_conv_w, v_ffn_conv_b=v_ffn_conv_b, v_ffn_w_down=v_ffn_w_down, v_ln_g=v_ln_g, v_ln_b=v_ln_b)
    weights = {n: given[n] for n in TWIN_WEIGHTS}
    shared = {n: given[n] for n in SHARED_INPUTS}
    per_example = {n: given[n] for n in ['x']}
    grad_fn = _jax.value_and_grad(_loss, argnums=(0, 1))

    def one_microbatch(ex, loss_target):
        ex = dict(ex)
        diff = ex.pop(TWIN_DIFF_INPUT)
        return grad_fn(weights, diff, {**shared, **ex}, loss_target)

    if N_MICROBATCH == 1:
        loss, (grad_w, grad_x) = one_microbatch(per_example, given["loss_target"])
    else:
        def body(carry, xs):
            loss_sum, grad_sum = carry
            l_k, (gw_k, gx_k) = one_microbatch(xs[0], xs[1])
            with _jax.named_scope("update"):
                return (loss_sum + l_k, _jax.tree.map(_jnp.add, grad_sum, gw_k)), gx_k

        init = (_jnp.zeros((), _jnp.float32), _jax.tree.map(_jnp.zeros_like, weights))
        (loss, grad_w), grad_x = _jax.lax.scan(body, init, (per_example, given["loss_target"]))
    with _jax.named_scope("update"):
        delta_w, new_m, new_v = {}, {}, {}
        for n in TWIN_WEIGHTS:
            delta_w[n], new_m[n], new_v[n] = _adamw(weights[n], grad_w[n], given["m_" + n], given["v_" + n])
    return (loss, grad_x, *[grad_w[n] for n in TWIN_WEIGHTS], *[delta_w[n] for n in TWIN_WEIGHTS],
            *[new_m[n] for n in TWIN_WEIGHTS], *[new_v[n] for n in TWIN_WEIGHTS])
```

```python
import functools

import jax
import jax.numpy as jnp
from jax import lax
from jax.experimental import pallas as pl
from jax.experimental.pallas import tpu as pltpu

F32 = jnp.float32
BF16 = jnp.bfloat16

HEAD_DIM = 128
POOL_WINDOWS = (2, 4, 8, 16)
LN_EPS = 1e-5
ADAM_LR = 0.001
ADAM_B1 = 0.9
ADAM_B2 = 0.999
ADAM_EPS = 1e-08
ADAM_WD = 0.01
ADAM_STEP = 10
N_CHIPS = 4
N_DEV = 8
MESH = pl.DeviceIdType.MESH
LANES = 128
HALO3 = 16
HALO31 = 32

ANY = pl.BlockSpec(memory_space=pl.ANY)


def _pick(n, pref, mult=LANES):
    if n <= pref:
        return n
    t = (pref // mult) * mult
    while t >= mult:
        if n % t == 0:
            return t
        t -= mult
    return n


def _params(*sem):
    return pltpu.CompilerParams(dimension_semantics=sem)


def _matmul(a, b, *, mode, out_dtype, name, b_lead=None, b_split=False, out_split=False, add=None,
            add_scale=1.0, tm=512, tn=1024, tk=None):
    ash, bsh = a.shape[-2:], b.shape[-2:]
    if mode == "nn":
        (M, K), (K2, N) = ash, bsh
        if b_split:
            N = N * N_CHIPS
    elif mode == "nt":
        (M, K), (N, K2) = ash, bsh
        if b_split:
            K2 = K2 * N_CHIPS
    else:
        (K, M), (K2, N) = ash, bsh
    assert K == K2, (ash, bsh, mode)
    tm = _pick(M, tm)
    tn = _pick(N // N_CHIPS if (out_split or (b_split and mode == "nn")) else N, tn)
    if tk is None:
        tk = K // N_CHIPS if (b_split and mode == "nt") else K
    else:
        tk = _pick(K // N_CHIPS if (b_split and mode == "nt") else K, tk)
    nk = K // tk
    n_per = (N // N_CHIPS) // tn
    k_per = (K // N_CHIPS) // tk

    def lead(shape, idx):
        if b_lead is None:
            return pl.BlockSpec(shape, idx)
        return pl.BlockSpec((None,) + shape, lambda i, j, k: (b_lead,) + idx(i, j, k))

    if mode == "nn":
        a_spec = pl.BlockSpec((tm, tk), lambda i, j, k: (i, k))
        if b_split:
            b_spec = lead((None, tk, tn), lambda i, j, k: (j // n_per, k, j % n_per))
        else:
            b_spec = lead((tk, tn), lambda i, j, k: (k, j))
        dims = (((1,), (0,)), ((), ()))
    elif mode == "nt":
        a_spec = pl.BlockSpec((tm, tk), lambda i, j, k: (i, k))
        if b_split:
            b_spec = lead((None, tn, tk), lambda i, j, k: (k // k_per, j, k % k_per))
        else:
            b_spec = lead((tn, tk), lambda i, j, k: (j, k))
        dims = (((1,), (1,)), ((), ()))
    else:
        a_spec = pl.BlockSpec((tk, tm), lambda i, j, k: (k, i))
        b_spec = pl.BlockSpec((tk, tn), lambda i, j, k: (k, j))
        dims = (((0,), (0,)), ((), ()))
    if out_split:
        out_shape = jax.ShapeDtypeStruct((N_CHIPS, M, N // N_CHIPS), out_dtype)
        out_spec = pl.BlockSpec((None, tm, tn), lambda i, j, k: (j // n_per, i, j % n_per))
    else:
        out_shape = jax.ShapeDtypeStruct((M, N), out_dtype)
        out_spec = pl.BlockSpec((tm, tn), lambda i, j, k: (i, j))
    in_specs = [a_spec, b_spec]
    args = [a, b]
    if add is not None:
        in_specs.append(pl.BlockSpec((tm, tn), lambda i, j, k: (i, j)))
        args.append(add)

    def body(*refs):
        if add is not None:
            a_ref, b_ref, add_ref, o_ref = refs[:4]
            scr = refs[4:]
        else:
            a_ref, b_ref, o_ref = refs[:3]
            add_ref = None
            scr = refs[3:]
        part = lax.dot_general(a_ref[...], b_ref[...], dims, preferred_element_type=F32)

        def finish(res):
            if add_ref is not None:
                res = res + add_scale * add_ref[...]
            o_ref[...] = res.astype(out_dtype)

        if nk == 1:
            finish(part)
        else:
            acc = scr[0]
            k = pl.program_id(2)

            @pl.when(k == 0)
            def _():
                acc[...] = part

            @pl.when(k > 0)
            def _():
                acc[...] += part

            @pl.when(k == nk - 1)
            def _():
                finish(acc[...])

    return pl.pallas_call(
        body, name=name,
        out_shape=out_shape,
        grid=(M // tm, N // tn, nk),
        in_specs=in_specs,
        out_specs=out_spec,
        scratch_shapes=[pltpu.VMEM((tm, tn), F32)] if nk > 1 else [],
        compiler_params=_params("parallel", "parallel", "arbitrary"),
    )(*args)


def _cast_bf16(w, name):
    L, R, C = w.shape
    tr, tc = _pick(R, 512, 16), _pick(C, 1408)

    def body(w_ref, o_ref):
        o_ref[...] = w_ref[...].astype(BF16)

    return pl.pallas_call(
        body, name=name, out_shape=jax.ShapeDtypeStruct(w.shape, BF16),
        grid=(L, R // tr, C // tc),
        in_specs=[pl.BlockSpec((None, tr, tc), lambda l, i, j: (l, i, j))],
        out_specs=pl.BlockSpec((None, tr, tc), lambda l, i, j: (l, i, j)),
        compiler_params=_params("parallel", "parallel", "parallel"),
    )(w)


def _sigmoid(v):
    return 1.0 / (1.0 + jnp.exp(-v))


def _ln_fwd(x, y, g, b, alpha, name):
    S, D = x.shape
    tr = _pick(S, 256, 8)

    def body(x_ref, y_ref, g_ref, b_ref, o_ref, ob_ref, xh_ref, rs_ref):
        r = alpha * x_ref[...] + y_ref[...]
        mu = jnp.mean(r, axis=-1, keepdims=True)
        d = r - mu
        var = jnp.mean(d * d, axis=-1, keepdims=True)
        rstd = lax.rsqrt(var + LN_EPS)
        xh = d * rstd
        o = xh * g_ref[...] + b_ref[...]
        o_ref[...] = o
        ob_ref[...] = o.astype(BF16)
        xh_ref[...] = xh
        rs_ref[...] = rstd

    row = pl.BlockSpec((tr, D), lambda i: (i, 0))
    vec = pl.BlockSpec((1, D), lambda i: (0, 0))
    return pl.pallas_call(
        body, name=name,
        out_shape=(jax.ShapeDtypeStruct((S, D), F32), jax.ShapeDtypeStruct((S, D), BF16),
                   jax.ShapeDtypeStruct((S, D), F32), jax.ShapeDtypeStruct((S, 1), F32)),
        grid=(S // tr,),
        in_specs=[row, row, vec, vec],
        out_specs=(row, row, row, pl.BlockSpec((tr, 1), lambda i: (i, 0))),
        compiler_params=_params("parallel"),
    )(x, y, g, b)


def _ln_bwd(dout, xhat, rstd, g, name):
    S, D = dout.shape
    tr = _pick(S, 256, 8)

    def body(do_ref, xh_ref, rs_ref, g_ref, dr_ref, drb_ref, dg_ref, db_ref):
        i = pl.program_id(0)
        do = do_ref[...]
        xh = xh_ref[...]
        dxh = do * g_ref[...]
        m1 = jnp.mean(dxh, axis=-1, keepdims=True)
        m2 = jnp.mean(dxh * xh, axis=-1, keepdims=True)
        dr = rs_ref[...] * (dxh - m1 - xh * m2)
        dr_ref[...] = dr
        drb_ref[...] = dr.astype(BF16)
        pg = jnp.sum(do * xh, axis=0, keepdims=True)
        pb = jnp.sum(do, axis=0, keepdims=True)

        @pl.when(i == 0)
        def _():
            dg_ref[...] = pg
            db_ref[...] = pb

        @pl.when(i > 0)
        def _():
            dg_ref[...] += pg
            db_ref[...] += pb

    row = pl.BlockSpec((tr, D), lambda i: (i, 0))
    vec = pl.BlockSpec((1, D), lambda i: (0, 0))
    return pl.pallas_call(
        body, name=name,
        out_shape=(jax.ShapeDtypeStruct((S, D), F32), jax.ShapeDtypeStruct((S, D), BF16),
                   jax.ShapeDtypeStruct((1, D), F32), jax.ShapeDtypeStruct((1, D), F32)),
        grid=(S // tr,),
        in_specs=[row, row, pl.BlockSpec((tr, 1), lambda i: (i, 0)), vec],
        out_specs=(row, row, vec, vec),
        compiler_params=_params("arbitrary"),
    )(dout, xhat, rstd, g)


def _loss_grad(y, target, name):
    S, D = y.shape
    tr = _pick(S, 256, 8)
    n = S // tr

    def body(y_ref, t_ref, dy_ref, l_ref, acc):
        i = pl.program_id(0)
        d = y_ref[...] - t_ref[...]
        dy_ref[...] = d * (1.0 / D)
        p = jnp.sum(d * d, axis=0, keepdims=True)

        @pl.when(i == 0)
        def _():
            acc[...] = p

        @pl.when(i > 0)
        def _():
            acc[...] += p

        @pl.when(i == n - 1)
        def _():
            l_ref[...] = (0.5 / D) * jnp.sum(acc[...], axis=1, keepdims=True)

    row = pl.BlockSpec((tr, D), lambda i: (i, 0))
    return pl.pallas_call(
        body, name=name,
        out_shape=(jax.ShapeDtypeStruct((S, D), F32), jax.ShapeDtypeStruct((1, 1), F32)),
        grid=(n,),
        in_specs=[row, row],
        out_specs=(row, pl.BlockSpec((1, 1), lambda i: (0, 0))),
        scratch_shapes=[pltpu.VMEM((1, D), F32)],
        compiler_params=_params("arbitrary"),
    )(y, target)


def _prev_spec(tr, halo, width, col):
    return pl.BlockSpec((halo, width), lambda c, i: (jnp.maximum(i * (tr // halo) - 1, 0), col(c)))


def _next_spec(tr, halo, width, col, nrows):
    last = nrows // halo - 1
    return pl.BlockSpec((halo, width), lambda c, i: (jnp.minimum((i + 1) * (tr // halo), last), col(c)))


def _cur_spec(tr, width, col):
    return pl.BlockSpec((tr, width), lambda c, i: (i, col(c)))


def _ffn_act_fwd(hu, conv_w, conv_b, name):
    S, F2 = hu.shape
    F = F2 // 2
    tr, tc, H = _pick(S, 512, 16), _pick(F, 512), HALO3
    nc, nr = F // tc, S // tr

    def body(gp_ref, g_ref, u_ref, w_ref, b_ref, z_ref, G):
        i = pl.program_id(1)
        G[0:H, :] = jnp.where(i > 0, gp_ref[...].astype(F32), 0.0)
        G[H:H + tr, :] = g_ref[...].astype(F32)
        gc = (b_ref[...] + w_ref[pl.ds(0, 1), :] * G[pl.ds(H - 2, tr), :]
              + w_ref[pl.ds(1, 1), :] * G[pl.ds(H - 1, tr), :] + w_ref[pl.ds(2, 1), :] * G[pl.ds(H, tr), :])
        z = gc * _sigmoid(gc) * u_ref[...].astype(F32)
        z_ref[...] = z.astype(BF16)

    gcol = lambda c: c
    ucol = lambda c: c + nc
    return pl.pallas_call(
        body, name=name, out_shape=jax.ShapeDtypeStruct((S, F), BF16),
        grid=(nc, nr),
        in_specs=[_prev_spec(tr, H, tc, gcol), _cur_spec(tr, tc, gcol), _cur_spec(tr, tc, ucol),
                  pl.BlockSpec((3, tc), lambda c, i: (0, c)), pl.BlockSpec((1, tc), lambda c, i: (0, c))],
        out_specs=pl.BlockSpec((tr, tc), lambda c, i: (i, c)),
        scratch_shapes=[pltpu.VMEM((H + tr, tc), F32)],
        compiler_params=_params("parallel", "parallel"),
    )(hu, hu, hu, conv_w, conv_b)


def _ffn_act_bwd(dz, hu, conv_w, conv_b, name):
    S, F = dz.shape
    tr, tc, H = _pick(S, 512, 16), _pick(F, 512), HALO3
    nc, nr = F // tc, S // tr
    n = tr + H

    def body(dz_ref, dzn_ref, gp_ref, g_ref, gn_ref, u_ref, un_ref, w_ref, b_ref,
             dg_ref, du_ref, dw_ref, db_ref, G, DG):
        i = pl.program_id(1)
        G[0:H, :] = jnp.where(i > 0, gp_ref[...].astype(F32), 0.0)
        G[H:H + tr, :] = g_ref[...].astype(F32)
        G[H + tr:H + tr + H, :] = gn_ref[...].astype(F32)
        w0, w1, w2 = w_ref[pl.ds(0, 1), :], w_ref[pl.ds(1, 1), :], w_ref[pl.ds(2, 1), :]
        gc = b_ref[...] + w0 * G[pl.ds(H - 2, n), :] + w1 * G[pl.ds(H - 1, n), :] + w2 * G[pl.ds(H, n), :]
        sg = _sigmoid(gc)
        dzf = jnp.concatenate([dz_ref[...], dzn_ref[...]], axis=0).astype(F32)
        uf = jnp.concatenate([u_ref[...], un_ref[...]], axis=0).astype(F32)
        rows = lax.broadcasted_iota(jnp.int32, (n, 1), 0)
        dzf = jnp.where((rows < tr) | (i < nr - 1), dzf, 0.0)
        dgc = dzf * uf * (sg * (1.0 + gc * (1.0 - sg)))
        du_ref[...] = (dzf[0:tr] * (gc[0:tr] * sg[0:tr])).astype(BF16)
        DG[...] = dgc
        dg = w2 * DG[pl.ds(0, tr), :] + w1 * DG[pl.ds(1, tr), :] + w0 * DG[pl.ds(2, tr), :]
        dg_ref[...] = dg.astype(BF16)
        dcur = dgc[0:tr]
        pw = [jnp.sum(dcur * G[pl.ds(H - 2 + k, tr), :], axis=0, keepdims=True) for k in range(3)]
        pb = jnp.sum(dcur, axis=0, keepdims=True)

        @pl.when(i == 0)
        def _():
            for k in range(3):
                dw_ref[pl.ds(k, 1), :] = pw[k]
            db_ref[...] = pb

        @pl.when(i > 0)
        def _():
            for k in range(3):
                dw_ref[pl.ds(k, 1), :] += pw[k]
            db_ref[...] += pb

    gcol = lambda c: c
    ucol = lambda c: c + nc
    blk = pl.BlockSpec((tr, tc), lambda c, i: (i, c))
    return pl.pallas_call(
        body, name=name,
        out_shape=(jax.ShapeDtypeStruct((S, F), BF16), jax.ShapeDtypeStruct((S, F), BF16),
                   jax.ShapeDtypeStruct((3, F), F32), jax.ShapeDtypeStruct((1, F), F32)),
        grid=(nc, nr),
        in_specs=[_cur_spec(tr, tc, gcol), _next_spec(tr, H, tc, gcol, S),
                  _prev_spec(tr, H, tc, gcol), _cur_spec(tr, tc, gcol), _next_spec(tr, H, tc, gcol, S),
                  _cur_spec(tr, tc, ucol), _next_spec(tr, H, tc, ucol, S),
                  pl.BlockSpec((3, tc), lambda c, i: (0, c)), pl.BlockSpec((1, tc), lambda c, i: (0, c))],
        out_specs=(blk, blk, pl.BlockSpec((3, tc), lambda c, i: (0, c)), pl.BlockSpec((1, tc), lambda c, i: (0, c))),
        scratch_shapes=[pltpu.VMEM((H + tr + H, tc), F32), pltpu.VMEM((n, tc), F32)],
        compiler_params=_params("parallel", "arbitrary"),
    )(dz, dz, hu, hu, hu, hu, hu, conv_w, conv_b)


def _softplus_neg(s):
    return jnp.minimum(-s, 0.0) - jnp.log(1.0 + jnp.exp(-jnp.abs(s)))


def _hilo_dot(v, m):
    hi = v.astype(BF16)
    lo = (v - hi.astype(F32)).astype(BF16)
    return (jnp.dot(hi, m, preferred_element_type=F32) + jnp.dot(lo, m, preferred_element_type=F32))


def _attn_fwd(h, n_heads, name):
    S = h.shape[0]
    dh = HEAD_DIM
    A = n_heads * dh
    tq = _pick(S, 256)
    nq = S // tq
    scale = 1.0 / float(dh) ** 0.5

    def body(q_ref, k_ref, v_ref, o_ref, tot_ref):
        i = pl.program_id(1)
        q = q_ref[...]
        r_io = lax.broadcasted_iota(jnp.int32, (tq, tq), 0)
        c_io = lax.broadcasted_iota(jnp.int32, (tq, tq), 1)
        later = (r_io > c_io).astype(BF16)
        ext = jnp.concatenate([later, jnp.ones((tq, LANES), BF16)], axis=1)
        causal = c_io < r_io

        def step(j, carry, diag):
            acc, run = carry
            start = pl.multiple_of(j * tq, tq)
            kj = k_ref[pl.ds(start, tq), :]
            vj = v_ref[pl.ds(start, tq), :]
            s = lax.dot_general(q, kj, (((1,), (1,)), ((), ())), preferred_element_type=F32) * scale
            ls = _softplus_neg(s)
            if diag:
                ls = jnp.where(causal, ls, 0.0)
            cs = _hilo_dot(ls, ext)
            excl, tot = cs[:, :tq], cs[:, tq:]
            lw = s + ls + excl + jnp.tile(run, (1, tq // LANES))
            w = jnp.exp(lw)
            if diag:
                w = jnp.where(causal, w, 0.0)
            acc = acc + jnp.dot(w.astype(BF16), vj, preferred_element_type=F32)
            return acc, run + tot

        carry = step(i, (jnp.zeros((tq, dh), F32), jnp.zeros((tq, LANES), F32)), True)
        acc, run = lax.fori_loop(0, i, lambda t, c: step(i - 1 - t, c, False), carry)
        o_ref[...] = acc.astype(BF16)
        tot_ref[...] = run

    return pl.pallas_call(
        body, name=name,
        out_shape=(jax.ShapeDtypeStruct((S, A), BF16), jax.ShapeDtypeStruct((n_heads, S, LANES), F32)),
        grid=(n_heads, nq),
        in_specs=[pl.BlockSpec((tq, dh), lambda hd, i: (i, hd)),
                  pl.BlockSpec((S, dh), lambda hd, i: (0, n_heads + hd)),
                  pl.BlockSpec((S, dh), lambda hd, i: (0, 2 * n_heads + hd))],
        out_specs=(pl.BlockSpec((tq, dh), lambda hd, i: (i, hd)),
                   pl.BlockSpec((None, tq, LANES), lambda hd, i: (hd, i, 0))),
        compiler_params=_params("parallel", "parallel"),
    )(h, h, h)


def _attn_bwd(h, do, tot, n_heads, name):
    S = h.shape[0]
    dh = HEAD_DIM
    A = n_heads * dh
    tq = _pick(S, 256)
    nq = S // tq
    scale = 1.0 / float(dh) ** 0.5
    tn_dims = (((0,), (0,)), ((), ()))

    def body(q_ref, k_ref, v_ref, do_ref, tot_ref, dq_ref, dk_ref, dv_ref, dk_acc, dv_acc):
        i = pl.program_id(1)

        @pl.when(i == 0)
        def _():
            dk_acc[...] = jnp.zeros_like(dk_acc)
            dv_acc[...] = jnp.zeros_like(dv_acc)

        q = q_ref[...]
        dob = do_ref[...]
        total = jnp.tile(tot_ref[...], (1, tq // LANES))
        r_io = lax.broadcasted_iota(jnp.int32, (tq, tq), 0)
        c_io = lax.broadcasted_iota(jnp.int32, (tq, tq), 1)
        ones = jnp.ones((tq, LANES), BF16)
        upto = jnp.concatenate([(r_io <= c_io).astype(BF16), ones], axis=1)
        before = jnp.concatenate([(r_io < c_io).astype(BF16), ones], axis=1)
        causal = c_io < r_io

        def step(j, carry, diag):
            dq, prun, erun = carry
            start = pl.multiple_of(j * tq, tq)
            kj = k_ref[pl.ds(start, tq), :]
            vj = v_ref[pl.ds(start, tq), :]
            s = lax.dot_general(q, kj, (((1,), (1,)), ((), ())), preferred_element_type=F32) * scale
            ls = _softplus_neg(s)
            if diag:
                ls = jnp.where(causal, ls, 0.0)
            cs = _hilo_dot(ls, upto)
            pin, ptot = cs[:, :tq], cs[:, tq:]
            excl = total - jnp.tile(prun, (1, tq // LANES)) - pin
            w = jnp.exp(s + ls + excl)
            if diag:
                w = jnp.where(causal, w, 0.0)
            dw = lax.dot_general(dob, vj, (((1,), (1,)), ((), ())), preferred_element_type=F32)
            e = dw * w
            ce = jnp.dot(e.astype(BF16), before, preferred_element_type=F32)
            ein, etot = ce[:, :tq], ce[:, tq:]
            ecum = ein + jnp.tile(erun, (1, tq // LANES))
            sn = jnp.exp(ls)
            dz = e * sn - (1.0 - sn) * ecum
            if diag:
                dz = jnp.where(causal, dz, 0.0)
            ds = (dz * scale).astype(BF16)
            wb = w.astype(BF16)
            dq = dq + jnp.dot(ds, kj, preferred_element_type=F32)
            dk_acc[pl.ds(start, tq), :] += lax.dot_general(ds, q, tn_dims, preferred_element_type=F32)
            dv_acc[pl.ds(start, tq), :] += lax.dot_general(wb, dob, tn_dims, preferred_element_type=F32)
            return dq, prun + ptot, erun + etot

        zero = jnp.zeros((tq, LANES), F32)
        carry = lax.fori_loop(0, i, lambda j, c: step(j, c, False), (jnp.zeros((tq, dh), F32), zero, zero))
        dq, _, _ = step(i, carry, True)
        dq_ref[...] = dq.astype(BF16)

        @pl.when(i == nq - 1)
        def _():
            dk_ref[...] = dk_acc[...].astype(BF16)
            dv_ref[...] = dv_acc[...].astype(BF16)

    qblk = pl.BlockSpec((tq, dh), lambda hd, i: (i, hd))
    full = pl.BlockSpec((S, dh), lambda hd, i: (0, hd))
    return pl.pallas_call(
        body, name=name,
        out_shape=tuple(jax.ShapeDtypeStruct((S, A), BF16) for _ in range(3)),
        grid=(n_heads, nq),
        in_specs=[qblk,
                  pl.BlockSpec((S, dh), lambda hd, i: (0, n_heads + hd)),
                  pl.BlockSpec((S, dh), lambda hd, i: (0, 2 * n_heads + hd)),
                  qblk,
                  pl.BlockSpec((None, tq, LANES), lambda hd, i: (hd, i, 0))],
        out_specs=(qblk, full, full),
        scratch_shapes=[pltpu.VMEM((S, dh), F32), pltpu.VMEM((S, dh), F32)],
        compiler_params=_params("parallel", "arbitrary"),
    )(h, h, h, do, tot)


def _evenconv_fwd(h, dw_w, dw_b, bn_g, bn_b, name):
    S = h.shape[0]
    KW, A = dw_w.shape
    H = HALO31
    tr = _pick(S, 256, H)
    first_tap = H - (KW - 1)

    def body(ap_ref, a_ref, gp_ref, g_ref, w_ref, b_ref, bg_ref, bb_ref, u1_ref, u3_ref, U):
        i = pl.program_id(1)
        glu_prev = ap_ref[...].astype(F32) * _sigmoid(gp_ref[...].astype(F32))
        U[0:H, :] = jnp.where(i > 0, glu_prev, 0.0)
        U[H:H + tr, :] = a_ref[...].astype(F32) * _sigmoid(g_ref[...].astype(F32))
        acc = b_ref[...] + w_ref[pl.ds(0, 1), :] * U[pl.ds(first_tap, tr), :]
        for k in range(1, KW):
            acc = acc + w_ref[pl.ds(k, 1), :] * U[pl.ds(first_tap + k, tr), :]
        u1_ref[...] = acc
        mu = jnp.mean(acc, axis=-1, keepdims=True)
        d = acc - mu
        var = jnp.mean(d * d, axis=-1, keepdims=True)
        u2 = d * lax.rsqrt(var + LN_EPS) * bg_ref[...] + bb_ref[...]
        u3_ref[...] = (u2 * _sigmoid(u2)).astype(BF16)

    acol = lambda c: 3
    gcol = lambda c: 4
    vec = pl.BlockSpec((1, A), lambda c, i: (0, 0))
    blk = pl.BlockSpec((tr, A), lambda c, i: (i, 0))
    return pl.pallas_call(
        body, name=name,
        out_shape=(jax.ShapeDtypeStruct((S, A), F32), jax.ShapeDtypeStruct((S, A), BF16)),
        grid=(1, S // tr),
        in_specs=[_prev_spec(tr, H, A, acol), _cur_spec(tr, A, acol),
                  _prev_spec(tr, H, A, gcol), _cur_spec(tr, A, gcol),
                  pl.BlockSpec((KW, A), lambda c, i: (0, 0)), vec, vec, vec],
        out_specs=(blk, blk),
        scratch_shapes=[pltpu.VMEM((H + tr, A), F32)],
        compiler_params=_params("parallel", "parallel"),
    )(h, h, h, h, dw_w, dw_b, bn_g, bn_b)


def _evenconv_bwd(du3, u1, h, dw_w, bn_g, bn_b, name):
    S = h.shape[0]
    KW, A = dw_w.shape
    H = HALO31
    tr = _pick(S, 256, H)
    nr = S // tr
    n = tr + H
    first_tap = H - (KW - 1)

    def body(d3_ref, d3n_ref, u1_ref, u1n_ref, ap_ref, a_ref, gp_ref, g_ref, w_ref, bg_ref, bb_ref,
             da_ref, dg_ref, dww_ref, dwb_ref, dbg_ref, dbb_ref, U0, DU):
        i = pl.program_id(1)
        u1 = jnp.concatenate([u1_ref[...], u1n_ref[...]], axis=0)
        d3 = jnp.concatenate([d3_ref[...], d3n_ref[...]], axis=0).astype(F32)
        rows = lax.broadcasted_iota(jnp.int32, (n, 1), 0)
        d3 = jnp.where((rows < tr) | (i < nr - 1), d3, 0.0)
        mu = jnp.mean(u1, axis=-1, keepdims=True)
        d = u1 - mu
        var = jnp.mean(d * d, axis=-1, keepdims=True)
        rstd = lax.rsqrt(var + LN_EPS)
        xh = d * rstd
        u2 = xh * bg_ref[...] + bb_ref[...]
        sg = _sigmoid(u2)
        du2 = d3 * (sg * (1.0 + u2 * (1.0 - sg)))
        dxh = du2 * bg_ref[...]
        m1 = jnp.mean(dxh, axis=-1, keepdims=True)
        m2 = jnp.mean(dxh * xh, axis=-1, keepdims=True)
        du1 = rstd * (dxh - m1 - xh * m2)
        DU[...] = du1
        pbg = jnp.sum(du2[0:tr] * xh[0:tr], axis=0, keepdims=True)
        pbb = jnp.sum(du2[0:tr], axis=0, keepdims=True)
        pwb = jnp.sum(du1[0:tr], axis=0, keepdims=True)

        glu_prev = ap_ref[...].astype(F32) * _sigmoid(gp_ref[...].astype(F32))
        U0[0:H, :] = jnp.where(i > 0, glu_prev, 0.0)
        a = a_ref[...].astype(F32)
        sgg = _sigmoid(g_ref[...].astype(F32))
        U0[H:H + tr, :] = a * sgg

        @pl.when(i == 0)
        def _():
            dbg_ref[...] = pbg
            dbb_ref[...] = pbb
            dwb_ref[...] = pwb
            dww_ref[...] = jnp.zeros_like(dww_ref)

        @pl.when(i > 0)
        def _():
            dbg_ref[...] += pbg
            dbb_ref[...] += pbb
            dwb_ref[...] += pwb

        du0 = w_ref[pl.ds(0, 1), :] * DU[pl.ds(KW - 1, tr), :]
        for k in range(1, KW):
            du0 = du0 + w_ref[pl.ds(k, 1), :] * DU[pl.ds(KW - 1 - k, tr), :]
        da_ref[...] = (du0 * sgg).astype(BF16)
        dg_ref[...] = (du0 * a * sgg * (1.0 - sgg)).astype(BF16)
        dcur = DU[pl.ds(0, tr), :]
        for k in range(KW):
            dww_ref[pl.ds(k, 1), :] += jnp.sum(dcur * U0[pl.ds(first_tap + k, tr), :], axis=0, keepdims=True)

    acol = lambda c: 3
    gcol = lambda c: 4
    one = lambda c: 1
    zero = lambda c: 0
    vec = pl.BlockSpec((1, A), lambda c, i: (0, 0))
    blk = pl.BlockSpec((tr, A), lambda c, i: (i, 0))
    return pl.pallas_call(
        body, name=name,
        out_shape=(jax.ShapeDtypeStruct((S, A), BF16), jax.ShapeDtypeStruct((S, A), BF16),
                   jax.ShapeDtypeStruct((KW, A), F32), jax.ShapeDtypeStruct((1, A), F32),
                   jax.ShapeDtypeStruct((1, A), F32), jax.ShapeDtypeStruct((1, A), F32)),
        grid=(1, nr),
        in_specs=[_cur_spec(tr, A, one), _next_spec(tr, H, A, one, S),
                  _cur_spec(tr, A, zero), _next_spec(tr, H, A, zero, S),
                  _prev_spec(tr, H, A, acol), _cur_spec(tr, A, acol),
                  _prev_spec(tr, H, A, gcol), _cur_spec(tr, A, gcol),
                  pl.BlockSpec((KW, A), lambda c, i: (0, 0)), vec, vec],
        out_specs=(blk, blk, pl.BlockSpec((KW, A), lambda c, i: (0, 0)), vec, vec, vec),
        scratch_shapes=[pltpu.VMEM((H + tr, A), F32), pltpu.VMEM((n, A), F32)],
        compiler_params=_params("arbitrary", "arbitrary"),
    )(du3, du3, u1, u1, h, h, h, h, dw_w, bn_g, bn_b)


def _pool_inv_count(row0, nrows, window):
    t = row0 + lax.broadcasted_iota(jnp.int32, (nrows, 1), 0)
    return 1.0 / jnp.minimum(t + 1, window).astype(F32)


def _odd_fwd(h, conv_w, pool_w, pool_scale, name):
    S = h.shape[0]
    C = conv_w.shape[1]
    G = len(POOL_WINDOWS)
    Dg = C // G
    H = HALO3
    tr = _pick(S, 256, H)

    def body(cb_ref, ccp_ref, cc_ref, chp_ref, ch_ref, pp_ref, p_ref, w_ref, pw_ref, sc_ref, mix_ref, M, P):
        i = pl.program_id(1)
        M[0:H, :] = jnp.where(i > 0, ccp_ref[...].astype(F32) * chp_ref[...].astype(F32), 0.0)
        M[H:H + tr, :] = cc_ref[...].astype(F32) * ch_ref[...].astype(F32)
        cm = (w_ref[pl.ds(0, 1), :] * M[pl.ds(H - 2, tr), :] + w_ref[pl.ds(1, 1), :] * M[pl.ds(H - 1, tr), :]
              + w_ref[pl.ds(2, 1), :] * M[pl.ds(H, tr), :])
        mix_ref[:, 0:C] = (cb_ref[...].astype(F32) * cm).astype(BF16)
        P[0:H, :] = jnp.where(i > 0, pp_ref[...].astype(F32), 0.0)
        P[H:H + tr, :] = p_ref[...].astype(F32)
        for gi, window in enumerate(POOL_WINDOWS):
            cols = pl.ds(gi * Dg, Dg)
            wsum = P[pl.ds(H, tr), cols]
            for dlt in range(1, window):
                wsum = wsum + P[pl.ds(H - dlt, tr), cols]
            diff = wsum * _pool_inv_count(i * tr, tr, window) - P[pl.ds(H, tr), cols]
            yd = jnp.dot(diff.astype(BF16), pw_ref[gi], preferred_element_type=F32) * sc_ref[:, cols]
            mix_ref[:, pl.ds(C + gi * Dg, Dg)] = yd.astype(BF16)

    col = lambda k: (lambda c: k)
    return pl.pallas_call(
        body, name=name, out_shape=jax.ShapeDtypeStruct((S, 2 * C), BF16),
        grid=(1, S // tr),
        in_specs=[_cur_spec(tr, C, col(0)),
                  _prev_spec(tr, H, C, col(1)), _cur_spec(tr, C, col(1)),
                  _prev_spec(tr, H, C, col(2)), _cur_spec(tr, C, col(2)),
                  _prev_spec(tr, H, C, col(3)), _cur_spec(tr, C, col(3)),
                  pl.BlockSpec((3, C), lambda c, i: (0, 0)),
                  pl.BlockSpec((G, Dg, Dg), lambda c, i: (0, 0, 0)),
                  pl.BlockSpec((1, C), lambda c, i: (0, 0))],
        out_specs=pl.BlockSpec((tr, 2 * C), lambda c, i: (i, 0)),
        scratch_shapes=[pltpu.VMEM((H + tr, C), F32), pltpu.VMEM((H + tr, C), F32)],
        compiler_params=_params("parallel", "parallel"),
    )(h, h, h, h, h, h, h, conv_w, pool_w, pool_scale)


def _odd_bwd(dmix, h, conv_w, pool_w, pool_scale, name):
    S = h.shape[0]
    C = conv_w.shape[1]
    G = len(POOL_WINDOWS)
    Dg = C // G
    H = HALO3
    tr = _pick(S, 256, H)
    nr = S // tr
    n = tr + H
    nt_dims = (((1,), (1,)), ((), ()))
    tn_dims = (((0,), (0,)), ((), ()))

    def body(dyc_ref, dycn_ref, dyd_ref, dydn_ref, cb_ref, cbn_ref, ccp_ref, cc_ref, ccn_ref,
             chp_ref, ch_ref, chn_ref, pp_ref, p_ref, w_ref, pw_ref, sc_ref,
             dh_ref, dw_ref, dpw_ref, dsc_ref, M, DCM, P, Q):
        i = pl.program_id(1)
        rows = lax.broadcasted_iota(jnp.int32, (n, 1), 0)
        valid = (rows < tr) | (i < nr - 1)

        @pl.when(i == 0)
        def _():
            dw_ref[...] = jnp.zeros_like(dw_ref)
            dpw_ref[...] = jnp.zeros_like(dpw_ref)
            dsc_ref[...] = jnp.zeros_like(dsc_ref)

        M[0:H, :] = jnp.where(i > 0, ccp_ref[...].astype(F32) * chp_ref[...].astype(F32), 0.0)
        cc = cc_ref[...].astype(F32)
        ch = ch_ref[...].astype(F32)
        M[H:H + tr, :] = cc * ch
        M[H + tr:H + tr + H, :] = ccn_ref[...].astype(F32) * chn_ref[...].astype(F32)
        w0, w1, w2 = w_ref[pl.ds(0, 1), :], w_ref[pl.ds(1, 1), :], w_ref[pl.ds(2, 1), :]
        cm = w0 * M[pl.ds(H - 2, tr), :] + w1 * M[pl.ds(H - 1, tr), :] + w2 * M[pl.ds(H, tr), :]
        dyc = jnp.concatenate([dyc_ref[...], dycn_ref[...]], axis=0).astype(F32)
        dyc = jnp.where(valid, dyc, 0.0)
        cbf = jnp.concatenate([cb_ref[...], cbn_ref[...]], axis=0).astype(F32)
        dh_ref[:, 0:C] = (dyc[0:tr] * cm).astype(BF16)
        DCM[...] = dyc * cbf
        dm = w2 * DCM[pl.ds(0, tr), :] + w1 * DCM[pl.ds(1, tr), :] + w0 * DCM[pl.ds(2, tr), :]
        dh_ref[:, C:2 * C] = (dm * ch).astype(BF16)
        dh_ref[:, 2 * C:3 * C] = (dm * cc).astype(BF16)
        dcur = DCM[pl.ds(0, tr), :]
        for k in range(3):
            dw_ref[pl.ds(k, 1), :] += jnp.sum(dcur * M[pl.ds(H - 2 + k, tr), :], axis=0, keepdims=True)

        P[0:H, :] = jnp.where(i > 0, pp_ref[...].astype(F32), 0.0)
        P[H:H + tr, :] = p_ref[...].astype(F32)
        dyd = jnp.concatenate([dyd_ref[...], dydn_ref[...]], axis=0).astype(F32)
        dyd = jnp.where(valid, dyd, 0.0)
        for gi, window in enumerate(POOL_WINDOWS):
            cols = pl.ds(gi * Dg, Dg)
            lo = gi * Dg
            wsum = P[pl.ds(H, tr), cols]
            for dlt in range(1, window):
                wsum = wsum + P[pl.ds(H - dlt, tr), cols]
            diff = (wsum * _pool_inv_count(i * tr, tr, window) - P[pl.ds(H, tr), cols]).astype(BF16)
            pw = pw_ref[gi]
            dyd_g = dyd[:, lo:lo + Dg]
            e = (dyd_g * sc_ref[:, cols]).astype(BF16)
            yraw = jnp.dot(diff, pw, preferred_element_type=F32)
            dsc_ref[:, cols] += jnp.sum(dyd_g[0:tr] * yraw, axis=0, keepdims=True)
            dpw_ref[gi] += lax.dot_general(diff, e[0:tr], tn_dims, preferred_element_type=F32)
            ddiff = lax.dot_general(e, pw, nt_dims, preferred_element_type=F32)
            Q[:, cols] = ddiff * _pool_inv_count(i * tr, n, window)
            acc = Q[pl.ds(0, tr), cols]
            for dlt in range(1, window):
                acc = acc + Q[pl.ds(dlt, tr), cols]
            dh_ref[:, pl.ds(3 * C + lo, Dg)] = (acc - ddiff[0:tr]).astype(BF16)

    col = lambda k: (lambda c: k)
    return pl.pallas_call(
        body, name=name,
        out_shape=(jax.ShapeDtypeStruct((S, 4 * C), BF16), jax.ShapeDtypeStruct((3, C), F32),
                   jax.ShapeDtypeStruct((G, Dg, Dg), F32), jax.ShapeDtypeStruct((1, C), F32)),
        grid=(1, nr),
        in_specs=[_cur_spec(tr, C, col(0)), _next_spec(tr, H, C, col(0), S),
                  _cur_spec(tr, C, col(1)), _next_spec(tr, H, C, col(1), S),
                  _cur_spec(tr, C, col(0)), _next_spec(tr, H, C, col(0), S),
                  _prev_spec(tr, H, C, col(1)), _cur_spec(tr, C, col(1)), _next_spec(tr, H, C, col(1), S),
                  _prev_spec(tr, H, C, col(2)), _cur_spec(tr, C, col(2)), _next_spec(tr, H, C, col(2), S),
                  _prev_spec(tr, H, C, col(3)), _cur_spec(tr, C, col(3)),
                  pl.BlockSpec((3, C), lambda c, i: (0, 0)),
                  pl.BlockSpec((G, Dg, Dg), lambda c, i: (0, 0, 0)),
                  pl.BlockSpec((1, C), lambda c, i: (0, 0))],
        out_specs=(pl.BlockSpec((tr, 4 * C), lambda c, i: (i, 0)),
                   pl.BlockSpec((3, C), lambda c, i: (0, 0)),
                   pl.BlockSpec((G, Dg, Dg), lambda c, i: (0, 0, 0)),
                   pl.BlockSpec((1, C), lambda c, i: (0, 0))),
        scratch_shapes=[pltpu.VMEM((H + tr + H, C), F32), pltpu.VMEM((n, C), F32),
                        pltpu.VMEM((H + tr, C), F32), pltpu.VMEM((n, C), F32)],
        compiler_params=_params("arbitrary", "arbitrary"),
    )(dmix, dmix, dmix, dmix, h, h, h, h, h, h, h, h, h, h, conv_w, pool_w, pool_scale)


def _adamw(w, g, m, v, name):
    L, R, C = w.shape
    tr, tc = _pick(R, 256, 8), _pick(C, 1408)
    c1 = 1.0 / (1.0 - ADAM_B1 ** ADAM_STEP)
    c2 = 1.0 / (1.0 - ADAM_B2 ** ADAM_STEP)

    def body(w_ref, g_ref, m_ref, v_ref, go_ref, d_ref, mo_ref, vo_ref):
        gg = g_ref[...]
        mn = ADAM_B1 * m_ref[...] + (1.0 - ADAM_B1) * gg
        vn = ADAM_B2 * v_ref[...] + (1.0 - ADAM_B2) * (gg * gg)
        d_ref[...] = -ADAM_LR * ((mn * c1) / (jnp.sqrt(vn * c2) + ADAM_EPS) + ADAM_WD * w_ref[...])
        go_ref[...] = gg
        mo_ref[...] = mn
        vo_ref[...] = vn

    blk = pl.BlockSpec((None, tr, tc), lambda l, i, j: (l, i, j))
    sds = jax.ShapeDtypeStruct(w.shape, F32)
    return pl.pallas_call(
        body, name=name, out_shape=(sds, sds, sds, sds),
        grid=(L, R // tr, C // tc),
        in_specs=[blk, blk, blk, blk], out_specs=(blk, blk, blk, blk),
        compiler_params=_params("parallel", "parallel", "parallel"),
    )(w, g, m, v)


def _sum_slots(buf, name):
    N, R, C = buf.shape
    tr = _pick(R, 512, 8)

    def body(b_ref, o_ref):
        acc = b_ref[0]
        for k in range(1, N):
            acc = acc + b_ref[k]
        o_ref[...] = acc

    return pl.pallas_call(
        body, name=name, out_shape=jax.ShapeDtypeStruct((R, C), F32),
        grid=(R // tr,),
        in_specs=[pl.BlockSpec((N, tr, C), lambda i: (0, i, 0))],
        out_specs=pl.BlockSpec((tr, C), lambda i: (i, 0)),
        compiler_params=_params("parallel"),
    )(buf)


def _pair_sum(pos, g, rsib, name):
    _, hr, hc = rsib.shape
    tr, tc = _pick(hr, 512, 16), _pick(hc, 1024)

    def body(p_ref, g_ref, r_ref, o_ref):
        o_ref[...] = (g_ref[...].astype(F32) + r_ref[...].astype(F32)).astype(BF16)

    blk = pl.BlockSpec((None, tr, tc), lambda s, i, j, p: (s, i, j))
    return pl.pallas_call(
        body, name=name, out_shape=jax.ShapeDtypeStruct(rsib.shape, BF16),
        grid_spec=pltpu.PrefetchScalarGridSpec(
            num_scalar_prefetch=1, grid=(N_CHIPS, hr // tr, hc // tc),
            in_specs=[pl.BlockSpec((None, None, tr, tc), lambda s, i, j, p: (s, p[1], i, j)), blk],
            out_specs=blk),
        compiler_params=_params("parallel", "parallel", "parallel"),
    )(pos, g, rsib)


def _chip_sum(pos, part, land, name):
    _, sr, sc = land.shape
    tr, tc = _pick(sr, 512, 16), _pick(sc, 1024)

    def body(p_ref, own_ref, l_ref, o_ref):
        acc = own_ref[...].astype(F32)
        for k in range(3):
            acc = acc + l_ref[k].astype(F32)
        o_ref[...] = acc

    return pl.pallas_call(
        body, name=name, out_shape=jax.ShapeDtypeStruct((sr, sc), F32),
        grid_spec=pltpu.PrefetchScalarGridSpec(
            num_scalar_prefetch=1, grid=(sr // tr, sc // tc),
            in_specs=[pl.BlockSpec((None, tr, tc), lambda i, j, p: (p[0], i, j)),
                      pl.BlockSpec((3, tr, tc), lambda i, j, p: (0, i, j))],
            out_specs=pl.BlockSpec((tr, tc), lambda i, j, p: (i, j))),
        compiler_params=_params("parallel", "parallel"),
    )(pos, part, land)


def _place():
    x, y, c = lax.axis_index("x"), lax.axis_index("y"), lax.axis_index("c")
    return x, y, c


def _other_chips(x, y):
    return [(1 - x, y), (x, 1 - y), (1 - x, 1 - y)]


def _rcopy(src, dst, ssem, rsem, dev):
    return pltpu.make_async_remote_copy(src_ref=src, dst_ref=dst, send_sem=ssem, recv_sem=rsem,
                                        device_id=dev, device_id_type=MESH)


def _allgather_big(shards, name):
    n = len(shards)
    items = [(a, l) for a in range(n) for l in range(shards[a].shape[0])]
    outs = [jax.ShapeDtypeStruct((s.shape[0], N_CHIPS) + s.shape[1:], BF16) for s in shards]
    T = len(items)

    def body(*refs):
        ins, full = refs[:n], refs[n:2 * n]
        ssem, rsem, lsem = refs[2 * n:]
        x, y, c = _place()
        j = 2 * x + y
        chips = _other_chips(x, y)
        sends, locs = [], []
        for t, (a, l) in enumerate(items):
            loc = pltpu.make_async_copy(ins[a].at[l], full[a].at[l, j], lsem.at[t])
            loc.start()
            locs.append(loc)
            for r, (px, py) in enumerate(chips):
                cp = _rcopy(ins[a].at[l, c], full[a].at[l, j, c],
                            ssem.at[6 * t + r], rsem.at[6 * t + r], (px, py, c))
                cp.start()
                sends.append(cp)
        for t, (a, l) in enumerate(items):
            for r, (px, py) in enumerate(chips):
                slab = full[a].at[l, 2 * px + py, c]
                _rcopy(slab, slab, ssem.at[6 * t + r], rsem.at[6 * t + r], (px, py, c)).wait_recv()
                fw = _rcopy(slab, slab, ssem.at[6 * t + 3 + r], rsem.at[6 * t + 3 + r], (x, y, 1 - c))
                fw.start()
                sends.append(fw)
        for t, (a, l) in enumerate(items):
            for r, (px, py) in enumerate(chips):
                slab = full[a].at[l, 2 * px + py, 1 - c]
                _rcopy(slab, slab, ssem.at[6 * t + 3 + r], rsem.at[6 * t + 3 + r], (x, y, 1 - c)).wait_recv()
        for cp in sends:
            cp.wait_send()
        for loc in locs:
            loc.wait()

    return pl.pallas_call(
        body, name=name, out_shape=tuple(outs),
        in_specs=[ANY] * n, out_specs=tuple([ANY] * n),
        scratch_shapes=[pltpu.SemaphoreType.DMA((6 * T,)), pltpu.SemaphoreType.DMA((6 * T,)),
                        pltpu.SemaphoreType.DMA((T,))],
    )(*shards)


def _allgather_small(shards, name):
    n = len(shards)
    outs = tuple(jax.ShapeDtypeStruct((N_CHIPS,) + s.shape, s.dtype) for s in shards)

    def body(*refs):
        ins, full = refs[:n], refs[n:2 * n]
        ssem, rsem, lsem = refs[2 * n:]
        x, y, c = _place()
        j = 2 * x + y
        chips = _other_chips(x, y)
        cps, locs = [], []
        for a in range(n):
            loc = pltpu.make_async_copy(ins[a], full[a].at[j], lsem.at[a])
            loc.start()
            locs.append(loc)
            for r, (px, py) in enumerate(chips):
                cp = _rcopy(ins[a], full[a].at[j], ssem.at[3 * a + r], rsem.at[3 * a + r], (px, py, c))
                cp.start()
                cps.append(cp)
        for a in range(n):
            for r, (px, py) in enumerate(chips):
                dst = full[a].at[2 * px + py]
                _rcopy(dst, dst, ssem.at[3 * a + r], rsem.at[3 * a + r], (px, py, c)).wait_recv()
        for cp in cps:
            cp.wait_send()
        for loc in locs:
            loc.wait()

    return pl.pallas_call(
        body, name=name, out_shape=outs,
        in_specs=[ANY] * n, out_specs=tuple([ANY] * n),
        scratch_shapes=[pltpu.SemaphoreType.DMA((3 * n,)), pltpu.SemaphoreType.DMA((3 * n,)),
                        pltpu.SemaphoreType.DMA((n,))],
    )(*shards)


def _pair_exchange(grads, name):
    n = len(grads)
    outs = [jax.ShapeDtypeStruct((N_CHIPS,) + g.shape[2:], BF16) for g in grads]

    def body(*refs):
        ins, got = refs[:n], refs[n:2 * n]
        ssem, rsem = refs[2 * n:]
        x, y, c = _place()
        cps = []
        for a in range(n):
            for s in range(N_CHIPS):
                cp = _rcopy(ins[a].at[s, 1 - c], got[a].at[s], ssem.at[N_CHIPS * a + s],
                            rsem.at[N_CHIPS * a + s], (x, y, 1 - c))
                cp.start()
                cps.append(cp)
        for cp in cps:
            cp.wait()

    return pl.pallas_call(
        body, name=name, out_shape=tuple(outs),
        in_specs=[ANY] * n, out_specs=tuple([ANY] * n),
        scratch_shapes=[pltpu.SemaphoreType.DMA((N_CHIPS * n,)), pltpu.SemaphoreType.DMA((N_CHIPS * n,))],
    )(*grads)


def _chip_exchange(parts, name):
    n = len(parts)
    outs = [jax.ShapeDtypeStruct((3,) + p.shape[1:], BF16) for p in parts]

    def body(*refs):
        ins, land = refs[:n], refs[n:2 * n]
        ssem, rsem = refs[2 * n:]
        x, y, c = _place()
        chips = _other_chips(x, y)
        cps = []
        for a in range(n):
            for r, (px, py) in enumerate(chips):
                cp = _rcopy(ins[a].at[2 * px + py], land[a].at[r],
                            ssem.at[3 * a + r], rsem.at[3 * a + r], (px, py, c))
                cp.start()
                cps.append(cp)
        for cp in cps:
            cp.wait()

    return pl.pallas_call(
        body, name=name, out_shape=tuple(outs),
        in_specs=[ANY] * n, out_specs=tuple([ANY] * n),
        scratch_shapes=[pltpu.SemaphoreType.DMA((3 * n,)), pltpu.SemaphoreType.DMA((3 * n,))],
    )(*parts)


def _half_swap(halves, groups, name):
    n = len(halves)
    outs = [jax.ShapeDtypeStruct((len(idxs), 2) + halves[idxs[0]].shape, F32) for idxs in groups]
    nw = len(groups)

    def body(*refs):
        ins, full = refs[:n], refs[n:n + nw]
        ssem, rsem, lsem = refs[n + nw:]
        x, y, c = _place()
        cps, locs = [], []
        for w, idxs in enumerate(groups):
            for l, t in enumerate(idxs):
                dst = full[w].at[l, c]
                loc = pltpu.make_async_copy(ins[t], dst, lsem.at[t])
                loc.start()
                locs.append(loc)
                cp = _rcopy(ins[t], dst, ssem.at[t], rsem.at[t], (x, y, 1 - c))
                cp.start()
                cps.append(cp)
        for w, idxs in enumerate(groups):
            for l, t in enumerate(idxs):
                dst = full[w].at[l, 1 - c]
                _rcopy(dst, dst, ssem.at[t], rsem.at[t], (x, y, 1 - c)).wait_recv()
        for cp in cps:
            cp.wait_send()
        for loc in locs:
            loc.wait()

    return pl.pallas_call(
        body, name=name, out_shape=tuple(outs),
        in_specs=[ANY] * n, out_specs=tuple([ANY] * nw),
        scratch_shapes=[pltpu.SemaphoreType.DMA((n,)), pltpu.SemaphoreType.DMA((n,)),
                        pltpu.SemaphoreType.DMA((n,))],
    )(*halves)


def _gather_all_devices(buf, name):
    R, C = buf.shape

    def body(b_ref, o_ref, ssem, rsem, lsem):
        x, y, c = _place()
        me = 4 * x + 2 * y + c
        loc = pltpu.make_async_copy(b_ref, o_ref.at[me], lsem)
        loc.start()
        cps = []
        for m in range(1, N_DEV):
            fx, fy, fc = (m >> 2) & 1, (m >> 1) & 1, m & 1
            px = x + fx - 2 * x * fx
            py = y + fy - 2 * y * fy
            pc = c + fc - 2 * c * fc
            cp = _rcopy(b_ref, o_ref.at[me], ssem.at[m - 1], rsem.at[m - 1], (px, py, pc))
            cp.start()
            cps.append((cp, 4 * px + 2 * py + pc))
        for m, (cp, peer) in enumerate(cps):
            dst = o_ref.at[peer]
            _rcopy(dst, dst, ssem.at[m], rsem.at[m], (x, y, c)).wait_recv()
        for cp, _ in cps:
            cp.wait_send()
        loc.wait()

    return pl.pallas_call(
        body, name=name, out_shape=jax.ShapeDtypeStruct((N_DEV, R, C), F32),
        in_specs=[ANY], out_specs=ANY,
        scratch_shapes=[pltpu.SemaphoreType.DMA((N_DEV - 1,)), pltpu.SemaphoreType.DMA((N_DEV - 1,)),
                        pltpu.SemaphoreType.DMA],
    )(buf)


def _pack(arrs):
    flat = jnp.concatenate([a.reshape(-1) for a in arrs])
    rows = -(-flat.shape[0] // (8 * LANES)) * 8
    flat = jnp.pad(flat, (0, rows * LANES - flat.shape[0]))
    return flat.reshape(rows, LANES)


def _unpack(buf, shapes):
    flat = buf.reshape(-1)
    out, off = [], 0
    for s in shapes:
        size = 1
        for d in s:
            size *= d
        out.append(flat[off:off + size].reshape(s))
        off += size
    return out


BIG = ("ev_w_in", "ev_w_out", "od_w_in", "od_w_out", "ffn_w_up", "ffn_w_down")
BIG_KIND = {"ev_w_in": "col", "ev_w_out": "row", "od_w_in": "col", "od_w_out": "row",
            "ffn_w_up": "col", "ffn_w_down": "row"}
SMALL_AXIS = {"ev_dw_w": 2, "ev_dw_b": None, "ev_bn_g": None, "ev_bn_b": None, "od_conv_w": 2,
              "od_pool_w": 2, "od_pool_scale": 1, "ffn_conv_w": 2, "ffn_conv_b": None, "ln_g": 2, "ln_b": 2}
WEIGHTS = ("ev_w_in", "ev_dw_w", "ev_dw_b", "ev_bn_g", "ev_bn_b", "ev_w_out", "od_w_in", "od_conv_w",
           "od_pool_w", "od_pool_scale", "od_w_out", "ffn_w_up", "ffn_conv_w", "ffn_conv_b", "ffn_w_down",
           "ln_g", "ln_b")


def _ffn_fwd(xb, w_up, w_down, l, conv_w, conv_b, tag):
    hu = _matmul(xb, w_up, mode="nn", b_lead=l, b_split=True, out_dtype=BF16, name=f"{tag}_up", tm=1024, tn=1408)
    z = _ffn_act_fwd(hu, conv_w, conv_b, name=f"{tag}_act")
    y = _matmul(z, w_down, mode="nn", b_lead=l, out_dtype=F32, name=f"{tag}_down", tm=512, tn=1024, tk=1408)
    return hu, z, y


def _ffn_bwd(drb, dr, alpha, xb, hu, z, w_up, w_down, l, conv_w, conv_b, tag):
    g_down = _matmul(z, drb, mode="tn", out_dtype=BF16, name=f"{tag}_dwdown", tm=512, tn=1024, tk=1024)
    dz = _matmul(drb, w_down, mode="nt", b_lead=l, out_dtype=BF16, name=f"{tag}_dz", tm=1024, tn=1408)
    dg, du, dcw, dcb = _ffn_act_bwd(dz, hu, conv_w, conv_b, name=f"{tag}_actbwd")
    dhu = jnp.concatenate([dg, du], axis=1)
    g_up = _matmul(xb, dhu, mode="tn", out_split=True, out_dtype=BF16, name=f"{tag}_dwup", tm=512, tn=1408, tk=1024)
    dx = _matmul(dhu, w_up, mode="nt", b_lead=l, b_split=True, out_dtype=F32, add=dr, add_scale=alpha,
                 name=f"{tag}_dx", tm=512, tn=1024, tk=1408)
    return dx, g_up, g_down, dcw, dcb


def kernel(x, ev_w_in, ev_dw_w, ev_dw_b, ev_bn_g, ev_bn_b, ev_w_out, od_w_in, od_conv_w, od_pool_w, od_pool_scale, od_w_out, ffn_w_up, ffn_conv_w, ffn_conv_b, ffn_w_down, ln_g, ln_b, loss_target, m_ev_w_in, m_ev_dw_w, m_ev_dw_b, m_ev_bn_g, m_ev_bn_b, m_ev_w_out, m_od_w_in, m_od_conv_w, m_od_pool_w, m_od_pool_scale, m_od_w_out, m_ffn_w_up, m_ffn_conv_w, m_ffn_conv_b, m_ffn_w_down, m_ln_g, m_ln_b, v_ev_w_in, v_ev_dw_w, v_ev_dw_b, v_ev_bn_g, v_ev_bn_b, v_ev_w_out, v_od_w_in, v_od_conv_w, v_od_pool_w, v_od_pool_scale, v_od_w_out, v_ffn_w_up, v_ffn_conv_w, v_ffn_conv_b, v_ffn_w_down, v_ln_g, v_ln_b):
    wts = dict(ev_w_in=ev_w_in, ev_dw_w=ev_dw_w, ev_dw_b=ev_dw_b, ev_bn_g=ev_bn_g, ev_bn_b=ev_bn_b,
               ev_w_out=ev_w_out, od_w_in=od_w_in, od_conv_w=od_conv_w, od_pool_w=od_pool_w,
               od_pool_scale=od_pool_scale, od_w_out=od_w_out, ffn_w_up=ffn_w_up, ffn_conv_w=ffn_conv_w,
               ffn_conv_b=ffn_conv_b, ffn_w_down=ffn_w_down, ln_g=ln_g, ln_b=ln_b)
    mom = dict(ev_w_in=m_ev_w_in, ev_dw_w=m_ev_dw_w, ev_dw_b=m_ev_dw_b, ev_bn_g=m_ev_bn_g, ev_bn_b=m_ev_bn_b,
               ev_w_out=m_ev_w_out, od_w_in=m_od_w_in, od_conv_w=m_od_conv_w, od_pool_w=m_od_pool_w,
               od_pool_scale=m_od_pool_scale, od_w_out=m_od_w_out, ffn_w_up=m_ffn_w_up, ffn_conv_w=m_ffn_conv_w,
               ffn_conv_b=m_ffn_conv_b, ffn_w_down=m_ffn_w_down, ln_g=m_ln_g, ln_b=m_ln_b)
    var = dict(ev_w_in=v_ev_w_in, ev_dw_w=v_ev_dw_w, ev_dw_b=v_ev_dw_b, ev_bn_g=v_ev_bn_g, ev_bn_b=v_ev_bn_b,
               ev_w_out=v_ev_w_out, od_w_in=v_od_w_in, od_conv_w=v_od_conv_w, od_pool_w=v_od_pool_w,
               od_pool_scale=v_od_pool_scale, od_w_out=v_od_w_out, ffn_w_up=v_ffn_w_up, ffn_conv_w=v_ffn_conv_w,
               ffn_conv_b=v_ffn_conv_b, ffn_w_down=v_ffn_w_down, ln_g=v_ln_g, ln_b=v_ln_b)

    S, D = x.shape[1], x.shape[2]
    depth = ln_g.shape[0]
    alpha = (2.0 * depth) ** 0.25
    A = ev_dw_b.shape[-1]
    n_heads = A // HEAD_DIM
    xi, yi, ci = _place()
    chip = 2 * xi + yi
    pos = jnp.stack([chip, ci]).astype(jnp.int32)

    shards_bf = []
    for k in BIG:
        L, r, c = wts[k].shape
        shards_bf.append(_cast_bf16(wts[k], name=f"cast_{k}").reshape(L, 2, r // 2, c))
    full = {}
    for k, g5 in zip(BIG, _allgather_big(shards_bf, name="gather_weights")):
        L, r, c = wts[k].shape
        full[k] = g5.reshape(L, N_CHIPS, r, c) if BIG_KIND[k] == "col" else g5.reshape(L, N_CHIPS * r, c)
    small_sharded = [k for k in WEIGHTS if k not in BIG and SMALL_AXIS[k] is not None]
    gathered = _allgather_small([wts[k] for k in small_sharded], name="gather_small")
    sm = {k: wts[k] for k in WEIGHTS if k not in BIG and SMALL_AXIS[k] is None}
    for k, g4 in zip(small_sharded, gathered):
        sm[k] = jnp.concatenate([g4[t] for t in range(N_CHIPS)], axis=SMALL_AXIS[k])
    pool_w_bf = sm["od_pool_w"][0].astype(BF16)

    x0 = x[0]
    x0b = _cast_bf16(x, name="cast_x")[0]
    h0 = _matmul(x0b, full["ev_w_in"], mode="nn", b_lead=0, b_split=True, out_dtype=BF16, name="ev_in",
                 tm=1024, tn=1280)
    o_a, tot = _attn_fwd(h0, n_heads, name="attn_fwd")
    u1, u3 = _evenconv_fwd(h0, sm["ev_dw_w"][0], sm["ev_dw_b"], sm["ev_bn_g"], sm["ev_bn_b"], name="evconv_fwd")
    mix0 = jnp.concatenate([o_a, u3], axis=1)
    y1 = _matmul(mix0, full["ev_w_out"], mode="nn", b_lead=0, out_dtype=F32, name="ev_out", tm=1024, tn=1024)
    x1, x1b, xh1, rs1 = _ln_fwd(x0, y1, sm["ln_g"][0, 0][None], sm["ln_b"][0, 0][None], alpha, name="ln00")
    hu0, z0, y2 = _ffn_fwd(x1b, full["ffn_w_up"], full["ffn_w_down"], 0, sm["ffn_conv_w"][0],
                           sm["ffn_conv_b"][0][None], "ffn0")
    x2, x2b, xh2, rs2 = _ln_fwd(x1, y2, sm["ln_g"][0, 1][None], sm["ln_b"][0, 1][None], alpha, name="ln01")
    h1 = _matmul(x2b, full["od_w_in"], mode="nn", b_lead=0, b_split=True, out_dtype=BF16, name="od_in",
                 tm=1024, tn=1024)
    mix1 = _odd_fwd(h1, sm["od_conv_w"][0], pool_w_bf, sm["od_pool_scale"], name="odd_fwd")
    y3 = _matmul(mix1, full["od_w_out"], mode="nn", b_lead=0, out_dtype=F32, name="od_out", tm=1024, tn=1024)
    x3, x3b, xh3, rs3 = _ln_fwd(x2, y3, sm["ln_g"][1, 0][None], sm["ln_b"][1, 0][None], alpha, name="ln10")
    hu1, z1, y4 = _ffn_fwd(x3b, full["ffn_w_up"], full["ffn_w_down"], 1, sm["ffn_conv_w"][1],
                           sm["ffn_conv_b"][1][None], "ffn1")
    x4, _, xh4, rs4 = _ln_fwd(x3, y4, sm["ln_g"][1, 1][None], sm["ln_b"][1, 1][None], alpha, name="ln11")

    dx4, loss_part = _loss_grad(x4, loss_target[0], name="loss")
    loss = lax.psum(loss_part[0, 0], ("x", "y", "c"))

    dr4, dr4b, dg11, db11 = _ln_bwd(dx4, xh4, rs4, sm["ln_g"][1, 1][None], name="ln11_bwd")
    dx3, g_up1, g_down1, dcw1, dcb1 = _ffn_bwd(dr4b, dr4, alpha, x3b, hu1, z1, full["ffn_w_up"],
                                               full["ffn_w_down"], 1, sm["ffn_conv_w"][1],
                                               sm["ffn_conv_b"][1][None], "ffn1")
    dr3, dr3b, dg10, db10 = _ln_bwd(dx3, xh3, rs3, sm["ln_g"][1, 0][None], name="ln10_bwd")
    g_odout = _matmul(mix1, dr3b, mode="tn", out_dtype=BF16, name="od_dwout", tm=512, tn=1024, tk=1024)
    dmix1 = _matmul(dr3b, full["od_w_out"], mode="nt", b_lead=0, out_dtype=BF16, name="od_dmix", tm=1024, tn=1024)
    dh1, d_odconv, d_pool, d_pscale = _odd_bwd(dmix1, h1, sm["od_conv_w"][0], pool_w_bf, sm["od_pool_scale"],
                                               name="odd_bwd")
    g_odin = _matmul(x2b, dh1, mode="tn", out_split=True, out_dtype=BF16, name="od_dwin", tm=512, tn=1024, tk=1024)
    dx2 = _matmul(dh1, full["od_w_in"], mode="nt", b_lead=0, b_split=True, out_dtype=F32, add=dr3, add_scale=alpha,
                  name="od_dx", tm=512, tn=1024, tk=1024)
    dr2, dr2b, dg01, db01 = _ln_bwd(dx2, xh2, rs2, sm["ln_g"][0, 1][None], name="ln01_bwd")
    dx1, g_up0, g_down0, dcw0, dcb0 = _ffn_bwd(dr2b, dr2, alpha, x1b, hu0, z0, full["ffn_w_up"],
                                               full["ffn_w_down"], 0, sm["ffn_conv_w"][0],
                                               sm["ffn_conv_b"][0][None], "ffn0")
    dr1, dr1b, dg00, db00 = _ln_bwd(dx1, xh1, rs1, sm["ln_g"][0, 0][None], name="ln00_bwd")
    g_evout = _matmul(mix0, dr1b, mode="tn", out_dtype=BF16, name="ev_dwout", tm=512, tn=1024, tk=1024)
    dmix0 = _matmul(dr1b, full["ev_w_out"], mode="nt", b_lead=0, out_dtype=BF16, name="ev_dmix", tm=1024, tn=1024)
    dq, dk, dv = _attn_bwd(h0, dmix0, tot, n_heads, name="attn_bwd")
    da, dgate, d_dww, d_dwb, d_bng, d_bnb = _evenconv_bwd(dmix0, u1, h0, sm["ev_dw_w"][0], sm["ev_bn_g"],
                                                          sm["ev_bn_b"], name="evconv_bwd")
    dh0 = jnp.concatenate([dq, dk, dv, da, dgate], axis=1)
    g_evin = _matmul(x0b, dh0, mode="tn", out_split=True, out_dtype=BF16, name="ev_dwin", tm=512, tn=1280, tk=1024)
    grad_x = _matmul(dh0, full["ev_w_in"], mode="nt", b_lead=0, b_split=True, out_dtype=F32, add=dr1, add_scale=alpha,
                     name="ev_dx", tm=512, tn=1024, tk=1280)

    items = [("ev_w_in", g_evin), ("ev_w_out", g_evout), ("od_w_in", g_odin), ("od_w_out", g_odout),
             ("ffn_w_up", g_up0), ("ffn_w_up", g_up1), ("ffn_w_down", g_down0), ("ffn_w_down", g_down1)]
    g4 = []
    for k, g in items:
        rows, cols = (g.shape[1], g.shape[2]) if BIG_KIND[k] == "col" else (g.shape[0] // N_CHIPS, g.shape[1])
        g4.append(g.reshape(N_CHIPS, 2, rows // 2, cols))
    sib = _pair_exchange(g4, name="grad_pair_exchange")
    parts = [_pair_sum(pos, g, r, name=f"grad_pair_sum{t}") for t, (g, r) in enumerate(zip(g4, sib))]
    land = _chip_exchange(parts, name="grad_chip_exchange")
    halves = [_chip_sum(pos, p, ld, name=f"grad_chip_sum{t}") for t, (p, ld) in enumerate(zip(parts, land))]
    groups = [[t for t, (k, _) in enumerate(items) if k == name] for name in BIG]
    big_grads = {k: g.reshape(wts[k].shape)
                 for k, g in zip(BIG, _half_swap(halves, groups, name="grad_half_swap"))}

    d_ln_g = jnp.stack([jnp.stack([dg00[0], dg01[0]]), jnp.stack([dg10[0], dg11[0]])])
    d_ln_b = jnp.stack([jnp.stack([db00[0], db01[0]]), jnp.stack([db10[0], db11[0]])])
    small_partial = {
        "ev_dw_w": d_dww[None], "ev_dw_b": d_dwb, "ev_bn_g": d_bng, "ev_bn_b": d_bnb,
        "od_conv_w": d_odconv[None], "od_pool_w": d_pool[None], "od_pool_scale": d_pscale,
        "ffn_conv_w": jnp.stack([dcw0, dcw1]), "ffn_conv_b": jnp.concatenate([dcb0, dcb1], axis=0),
        "ln_g": d_ln_g, "ln_b": d_ln_b}
    small_names = [k for k in WEIGHTS if k not in BIG]
    packed = _pack([small_partial[k] for k in small_names])
    summed = _sum_slots(_gather_all_devices(packed, name="gather_small_grads"), name="sum_small_grads")
    small_full = dict(zip(small_names, _unpack(summed, [small_partial[k].shape for k in small_names])))
    small_grads = {}
    for k in small_names:
        ax = SMALL_AXIS[k]
        if ax is None:
            small_grads[k] = small_full[k]
        else:
            size = wts[k].shape[ax]
            small_grads[k] = lax.dynamic_slice_in_dim(small_full[k], chip * size, size, axis=ax)

    grads, delta, new_m, new_v = {}, {}, {}, {}
    for k in BIG:
        grads[k], delta[k], new_m[k], new_v[k] = _adamw(wts[k], big_grads[k], mom[k], var[k], name=f"adamw_{k}")
    shapes = [wts[k].shape for k in small_names]
    pw, pg, pm, pv = (_pack([d[k] for k in small_names]) for d in (wts, small_grads, mom, var))
    sg, sd, smn, svn = _adamw(pw[None], pg[None], pm[None], pv[None], name="adamw_small")
    for dst, buf in ((grads, sg), (delta, sd), (new_m, smn), (new_v, svn)):
        for k, a in zip(small_names, _unpack(buf[0], shapes)):
            dst[k] = a

    return (loss, grad_x[None], *[grads[k] for k in WEIGHTS], *[delta[k] for k in WEIGHTS],
            *[new_m[k] for k in WEIGHTS], *[new_v[k] for k in WEIGHTS])
```

```python
import functools

import jax
import jax.numpy as jnp
from jax import lax
from jax.experimental import pallas as pl
from jax.experimental.pallas import tpu as pltpu

F32 = jnp.float32
BF16 = jnp.bfloat16

HEAD_DIM = 128
POOL_WINDOWS = (2, 4, 8, 16)
LN_EPS = 1e-5
ADAM_LR = 0.001
ADAM_B1 = 0.9
ADAM_B2 = 0.999
ADAM_EPS = 1e-08
ADAM_WD = 0.01
ADAM_STEP = 10
N_CHIPS = 4
N_DEV = 8
MESH = pl.DeviceIdType.MESH
LANES = 128
HALO3 = 16
HALO31 = 32

ANY = pl.BlockSpec(memory_space=pl.ANY)


def _pick(n, pref, mult=LANES):
    if n <= pref:
        return n
    t = (pref // mult) * mult
    while t >= mult:
        if n % t == 0:
            return t
        t -= mult
    return n


def _params(*sem):
    return pltpu.CompilerParams(dimension_semantics=sem)


def _matmul(a, b, *, mode, out_dtype, name, b_lead=None, b_split=False, out_split=False, add=None,
            add_scale=1.0, tm=512, tn=1024, tk=None, exchange=None):
    ash, bsh = a.shape[-2:], b.shape[-2:]
    if mode == "nn":
        (M, K), (K2, N) = ash, bsh
        if b_split:
            N = N * N_CHIPS
    elif mode == "nt":
        (M, K), (N, K2) = ash, bsh
        if b_split:
            K2 = K2 * N_CHIPS
    else:
        (K, M), (K2, N) = ash, bsh
    assert K == K2, (ash, bsh, mode)
    tm = _pick(M, tm)
    tn = _pick(N // N_CHIPS if (out_split or (b_split and mode == "nn")) else N, tn)
    if tk is None:
        tk = K // N_CHIPS if (b_split and mode == "nt") else K
    else:
        tk = _pick(K // N_CHIPS if (b_split and mode == "nt") else K, tk)
    nk = K // tk
    n_per = (N // N_CHIPS) // tn
    k_per = (K // N_CHIPS) // tk

    def lead(shape, idx):
        if b_lead is None:
            return pl.BlockSpec(shape, idx)
        return pl.BlockSpec((None,) + shape, lambda i, j, k: (b_lead,) + idx(i, j, k))

    if mode == "nn":
        a_spec = pl.BlockSpec((tm, tk), lambda i, j, k: (i, k))
        if b_split:
            b_spec = lead((None, tk, tn), lambda i, j, k: (j // n_per, k, j % n_per))
        else:
            b_spec = lead((tk, tn), lambda i, j, k: (k, j))
        dims = (((1,), (0,)), ((), ()))
    elif mode == "nt":
        a_spec = pl.BlockSpec((tm, tk), lambda i, j, k: (i, k))
        if b_split:
            b_spec = lead((None, tn, tk), lambda i, j, k: (k // k_per, j, k % k_per))
        else:
            b_spec = lead((tn, tk), lambda i, j, k: (j, k))
        dims = (((1,), (1,)), ((), ()))
    else:
        a_spec = pl.BlockSpec((tk, tm), lambda i, j, k: (k, i))
        b_spec = pl.BlockSpec((tk, tn), lambda i, j, k: (k, j))
        dims = (((0,), (0,)), ((), ()))
    if out_split:
        out_shape = jax.ShapeDtypeStruct((N_CHIPS, M, N // N_CHIPS), out_dtype)
        out_spec = pl.BlockSpec((None, tm, tn), lambda i, j, k: (j // n_per, i, j % n_per))
    else:
        out_shape = jax.ShapeDtypeStruct((M, N), out_dtype)
        out_spec = pl.BlockSpec((tm, tn), lambda i, j, k: (i, j))
    in_specs = [a_spec, b_spec]
    args = [a, b]
    if add is not None:
        in_specs.append(pl.BlockSpec((tm, tn), lambda i, j, k: (i, j)))
        args.append(add)

    n_in = len(args)
    n_ex = 0 if exchange is None else len(exchange)
    grid = (M // tm, N // tn, nk)

    def body(*refs):
        a_ref, b_ref = refs[:2]
        add_ref = refs[2] if add is not None else None
        parts = refs[n_in:n_in + n_ex]
        o_ref = refs[n_in + n_ex]
        land = refs[n_in + n_ex + 1:n_in + 2 * n_ex + 1]
        scr = refs[n_in + 2 * n_ex + 1:]
        i, j, k = pl.program_id(0), pl.program_id(1), pl.program_id(2)
        if n_ex:
            ssem, rsem = scr[-2:]

            @pl.when((i == 0) & (j == 0) & (k == 0))
            def _():
                _exchange_start(parts, land, ssem, rsem)

        part = lax.dot_general(a_ref[...], b_ref[...], dims, preferred_element_type=F32)

        def finish(res):
            if add_ref is not None:
                res = res + add_scale * add_ref[...]
            o_ref[...] = res.astype(out_dtype)

        if nk == 1:
            finish(part)
        else:
            acc = scr[0]

            @pl.when(k == 0)
            def _():
                acc[...] = part

            @pl.when(k > 0)
            def _():
                acc[...] += part

            @pl.when(k == nk - 1)
            def _():
                finish(acc[...])

        if n_ex:
            @pl.when((i == grid[0] - 1) & (j == grid[1] - 1) & (k == nk - 1))
            def _():
                _exchange_finish(parts, land, ssem, rsem)

    scratch = [pltpu.VMEM((tm, tn), F32)] if nk > 1 else []
    if n_ex:
        scratch += [pltpu.SemaphoreType.DMA((3 * n_ex,)), pltpu.SemaphoreType.DMA((3 * n_ex,))]
        res = pl.pallas_call(
            body, name=name,
            out_shape=(out_shape,) + tuple(_land_shape(p) for p in exchange),
            grid=grid,
            in_specs=in_specs + [ANY] * n_ex,
            out_specs=(out_spec,) + (ANY,) * n_ex,
            scratch_shapes=scratch,
            compiler_params=_params("arbitrary", "arbitrary", "arbitrary"),
        )(*args, *exchange)
        return res[0], list(res[1:])
    return pl.pallas_call(
        body, name=name,
        out_shape=out_shape,
        grid=grid,
        in_specs=in_specs,
        out_specs=out_spec,
        scratch_shapes=scratch,
        compiler_params=_params("parallel", "parallel", "arbitrary"),
    )(*args)


def _cast_bf16(w, name):
    L, R, C = w.shape
    tr, tc = _pick(R, 512, 16), _pick(C, 1408)

    def body(w_ref, o_ref):
        o_ref[...] = w_ref[...].astype(BF16)

    return pl.pallas_call(
        body, name=name, out_shape=jax.ShapeDtypeStruct(w.shape, BF16),
        grid=(L, R // tr, C // tc),
        in_specs=[pl.BlockSpec((None, tr, tc), lambda l, i, j: (l, i, j))],
        out_specs=pl.BlockSpec((None, tr, tc), lambda l, i, j: (l, i, j)),
        compiler_params=_params("parallel", "parallel", "parallel"),
    )(w)


def _cast_into_gather(pos, w, name):
    L, R, C = w.shape
    r2 = R // 2
    tr, tc = _pick(r2, 512, 16), _pick(C, 1408)

    def body(p_ref, w_ref, o_ref):
        o_ref[...] = w_ref[...].astype(BF16)

    return pl.pallas_call(
        body, name=name, out_shape=jax.ShapeDtypeStruct((L, N_CHIPS, 2, r2, C), BF16),
        grid_spec=pltpu.PrefetchScalarGridSpec(
            num_scalar_prefetch=1, grid=(L, 2, r2 // tr, C // tc),
            in_specs=[pl.BlockSpec((None, None, tr, tc), lambda l, h, i, j, p: (l, h, i, j))],
            out_specs=pl.BlockSpec((None, None, None, tr, tc), lambda l, h, i, j, p: (l, p[0], h, i, j))),
        compiler_params=_params("parallel", "parallel", "parallel", "parallel"),
    )(pos, w.reshape(L, 2, r2, C))


def _sigmoid(v):
    return 1.0 / (1.0 + jnp.exp(-v))


def _ln_fwd(x, y, g, b, alpha, name):
    S, D = x.shape
    tr = _pick(S, 256, 8)

    def body(x_ref, y_ref, g_ref, b_ref, o_ref, ob_ref, xh_ref, rs_ref):
        r = alpha * x_ref[...] + y_ref[...]
        mu = jnp.mean(r, axis=-1, keepdims=True)
        d = r - mu
        var = jnp.mean(d * d, axis=-1, keepdims=True)
        rstd = lax.rsqrt(var + LN_EPS)
        xh = d * rstd
        o = xh * g_ref[...] + b_ref[...]
        o_ref[...] = o
        ob_ref[...] = o.astype(BF16)
        xh_ref[...] = xh
        rs_ref[...] = rstd

    row = pl.BlockSpec((tr, D), lambda i: (i, 0))
    vec = pl.BlockSpec((1, D), lambda i: (0, 0))
    return pl.pallas_call(
        body, name=name,
        out_shape=(jax.ShapeDtypeStruct((S, D), F32), jax.ShapeDtypeStruct((S, D), BF16),
                   jax.ShapeDtypeStruct((S, D), F32), jax.ShapeDtypeStruct((S, 1), F32)),
        grid=(S // tr,),
        in_specs=[row, row, vec, vec],
        out_specs=(row, row, row, pl.BlockSpec((tr, 1), lambda i: (i, 0))),
        compiler_params=_params("parallel"),
    )(x, y, g, b)


def _ln_bwd(dout, xhat, rstd, g, name):
    S, D = dout.shape
    tr = _pick(S, 256, 8)

    def body(do_ref, xh_ref, rs_ref, g_ref, dr_ref, drb_ref, dg_ref, db_ref):
        i = pl.program_id(0)
        do = do_ref[...]
        xh = xh_ref[...]
        dxh = do * g_ref[...]
        m1 = jnp.mean(dxh, axis=-1, keepdims=True)
        m2 = jnp.mean(dxh * xh, axis=-1, keepdims=True)
        dr = rs_ref[...] * (dxh - m1 - xh * m2)
        dr_ref[...] = dr
        drb_ref[...] = dr.astype(BF16)
        pg = jnp.sum(do * xh, axis=0, keepdims=True)
        pb = jnp.sum(do, axis=0, keepdims=True)

        @pl.when(i == 0)
        def _():
            dg_ref[...] = pg
            db_ref[...] = pb

        @pl.when(i > 0)
        def _():
            dg_ref[...] += pg
            db_ref[...] += pb

    row = pl.BlockSpec((tr, D), lambda i: (i, 0))
    vec = pl.BlockSpec((1, D), lambda i: (0, 0))
    return pl.pallas_call(
        body, name=name,
        out_shape=(jax.ShapeDtypeStruct((S, D), F32), jax.ShapeDtypeStruct((S, D), BF16),
                   jax.ShapeDtypeStruct((1, D), F32), jax.ShapeDtypeStruct((1, D), F32)),
        grid=(S // tr,),
        in_specs=[row, row, pl.BlockSpec((tr, 1), lambda i: (i, 0)), vec],
        out_specs=(row, row, vec, vec),
        compiler_params=_params("arbitrary"),
    )(dout, xhat, rstd, g)


def _loss_grad(y, target, name):
    S, D = y.shape
    tr = _pick(S, 256, 8)
    n = S // tr

    def body(y_ref, t_ref, dy_ref, l_ref, acc):
        i = pl.program_id(0)
        d = y_ref[...] - t_ref[...]
        dy_ref[...] = d * (1.0 / D)
        p = jnp.sum(d * d, axis=0, keepdims=True)

        @pl.when(i == 0)
        def _():
            acc[...] = p

        @pl.when(i > 0)
        def _():
            acc[...] += p

        @pl.when(i == n - 1)
        def _():
            l_ref[...] = (0.5 / D) * jnp.sum(acc[...], axis=1, keepdims=True)

    row = pl.BlockSpec((tr, D), lambda i: (i, 0))
    return pl.pallas_call(
        body, name=name,
        out_shape=(jax.ShapeDtypeStruct((S, D), F32), jax.ShapeDtypeStruct((1, 1), F32)),
        grid=(n,),
        in_specs=[row, row],
        out_specs=(row, pl.BlockSpec((1, 1), lambda i: (0, 0))),
        scratch_shapes=[pltpu.VMEM((1, D), F32)],
        compiler_params=_params("arbitrary"),
    )(y, target)


def _prev_spec(tr, halo, width, col):
    return pl.BlockSpec((halo, width), lambda c, i: (jnp.maximum(i * (tr // halo) - 1, 0), col(c)))


def _next_spec(tr, halo, width, col, nrows):
    last = nrows // halo - 1
    return pl.BlockSpec((halo, width), lambda c, i: (jnp.minimum((i + 1) * (tr // halo), last), col(c)))


def _cur_spec(tr, width, col):
    return pl.BlockSpec((tr, width), lambda c, i: (i, col(c)))


def _ffn_act_fwd(hu, conv_w, conv_b, name):
    S, F2 = hu.shape
    F = F2 // 2
    tr, tc, H = _pick(S, 512, 16), _pick(F, 512), HALO3
    nc, nr = F // tc, S // tr

    def body(gp_ref, g_ref, u_ref, w_ref, b_ref, z_ref, G):
        i = pl.program_id(1)
        G[0:H, :] = jnp.where(i > 0, gp_ref[...].astype(F32), 0.0)
        G[H:H + tr, :] = g_ref[...].astype(F32)
        gc = (b_ref[...] + w_ref[pl.ds(0, 1), :] * G[pl.ds(H - 2, tr), :]
              + w_ref[pl.ds(1, 1), :] * G[pl.ds(H - 1, tr), :] + w_ref[pl.ds(2, 1), :] * G[pl.ds(H, tr), :])
        z = gc * _sigmoid(gc) * u_ref[...].astype(F32)
        z_ref[...] = z.astype(BF16)

    gcol = lambda c: c
    ucol = lambda c: c + nc
    return pl.pallas_call(
        body, name=name, out_shape=jax.ShapeDtypeStruct((S, F), BF16),
        grid=(nc, nr),
        in_specs=[_prev_spec(tr, H, tc, gcol), _cur_spec(tr, tc, gcol), _cur_spec(tr, tc, ucol),
                  pl.BlockSpec((3, tc), lambda c, i: (0, c)), pl.BlockSpec((1, tc), lambda c, i: (0, c))],
        out_specs=pl.BlockSpec((tr, tc), lambda c, i: (i, c)),
        scratch_shapes=[pltpu.VMEM((H + tr, tc), F32)],
        compiler_params=_params("parallel", "parallel"),
    )(hu, hu, hu, conv_w, conv_b)


def _ffn_act_bwd(dz, hu, conv_w, conv_b, name):
    S, F = dz.shape
    tr, tc, H = _pick(S, 512, 16), _pick(F, 512), HALO3
    nc, nr = F // tc, S // tr
    n = tr + H

    def body(dz_ref, dzn_ref, gp_ref, g_ref, gn_ref, u_ref, un_ref, w_ref, b_ref,
             dg_ref, du_ref, dw_ref, db_ref, G, DG):
        i = pl.program_id(1)
        G[0:H, :] = jnp.where(i > 0, gp_ref[...].astype(F32), 0.0)
        G[H:H + tr, :] = g_ref[...].astype(F32)
        G[H + tr:H + tr + H, :] = gn_ref[...].astype(F32)
        w0, w1, w2 = w_ref[pl.ds(0, 1), :], w_ref[pl.ds(1, 1), :], w_ref[pl.ds(2, 1), :]
        gc = b_ref[...] + w0 * G[pl.ds(H - 2, n), :] + w1 * G[pl.ds(H - 1, n), :] + w2 * G[pl.ds(H, n), :]
        sg = _sigmoid(gc)
        dzf = jnp.concatenate([dz_ref[...], dzn_ref[...]], axis=0).astype(F32)
        uf = jnp.concatenate([u_ref[...], un_ref[...]], axis=0).astype(F32)
        rows = lax.broadcasted_iota(jnp.int32, (n, 1), 0)
        dzf = jnp.where((rows < tr) | (i < nr - 1), dzf, 0.0)
        dgc = dzf * uf * (sg * (1.0 + gc * (1.0 - sg)))
        du_ref[...] = (dzf[0:tr] * (gc[0:tr] * sg[0:tr])).astype(BF16)
        DG[...] = dgc
        dg = w2 * DG[pl.ds(0, tr), :] + w1 * DG[pl.ds(1, tr), :] + w0 * DG[pl.ds(2, tr), :]
        dg_ref[...] = dg.astype(BF16)
        dcur = dgc[0:tr]
        pw = [jnp.sum(dcur * G[pl.ds(H - 2 + k, tr), :], axis=0, keepdims=True) for k in range(3)]
        pb = jnp.sum(dcur, axis=0, keepdims=True)

        @pl.when(i == 0)
        def _():
            for k in range(3):
                dw_ref[pl.ds(k, 1), :] = pw[k]
            db_ref[...] = pb

        @pl.when(i > 0)
        def _():
            for k in range(3):
                dw_ref[pl.ds(k, 1), :] += pw[k]
            db_ref[...] += pb

    gcol = lambda c: c
    ucol = lambda c: c + nc
    blk = pl.BlockSpec((tr, tc), lambda c, i: (i, c))
    return pl.pallas_call(
        body, name=name,
        out_shape=(jax.ShapeDtypeStruct((S, F), BF16), jax.ShapeDtypeStruct((S, F), BF16),
                   jax.ShapeDtypeStruct((3, F), F32), jax.ShapeDtypeStruct((1, F), F32)),
        grid=(nc, nr),
        in_specs=[_cur_spec(tr, tc, gcol), _next_spec(tr, H, tc, gcol, S),
                  _prev_spec(tr, H, tc, gcol), _cur_spec(tr, tc, gcol), _next_spec(tr, H, tc, gcol, S),
                  _cur_spec(tr, tc, ucol), _next_spec(tr, H, tc, ucol, S),
                  pl.BlockSpec((3, tc), lambda c, i: (0, c)), pl.BlockSpec((1, tc), lambda c, i: (0, c))],
        out_specs=(blk, blk, pl.BlockSpec((3, tc), lambda c, i: (0, c)), pl.BlockSpec((1, tc), lambda c, i: (0, c))),
        scratch_shapes=[pltpu.VMEM((H + tr + H, tc), F32), pltpu.VMEM((n, tc), F32)],
        compiler_params=_params("parallel", "arbitrary"),
    )(dz, dz, hu, hu, hu, hu, hu, conv_w, conv_b)


def _softplus_neg(s):
    return jnp.minimum(-s, 0.0) - jnp.log(1.0 + jnp.exp(-jnp.abs(s)))


def _hilo_dot(v, m):
    hi = v.astype(BF16)
    lo = (v - hi.astype(F32)).astype(BF16)
    return (jnp.dot(hi, m, preferred_element_type=F32) + jnp.dot(lo, m, preferred_element_type=F32))


def _attn_fwd(h, n_heads, name, gather=()):
    S = h.shape[0]
    dh = HEAD_DIM
    A = n_heads * dh
    tq = _pick(S, 256)
    nq = S // tq
    scale = 1.0 / float(dh) ** 0.5
    ng = len(gather)

    def body(*refs):
        q_ref, k_ref, v_ref = refs[:3]
        o_ref, tot_ref = refs[3 + ng:5 + ng]
        full = refs[5 + ng:5 + 2 * ng]
        hd = pl.program_id(0)
        i = pl.program_id(1)
        if ng:
            ssem, rsem = refs[5 + 2 * ng:]

            @pl.when((hd == 0) & (i == 0))
            def _():
                _gather_start(full, ssem, rsem)

            @pl.when((hd == n_heads - 1) & (i == 0))
            def _():
                _gather_forward(full, ssem, rsem)

        q = q_ref[...]
        r_io = lax.broadcasted_iota(jnp.int32, (tq, tq), 0)
        c_io = lax.broadcasted_iota(jnp.int32, (tq, tq), 1)
        later = (r_io > c_io).astype(BF16)
        ext = jnp.concatenate([later, jnp.ones((tq, LANES), BF16)], axis=1)
        causal = c_io < r_io

        def step(j, carry, diag):
            acc, run = carry
            start = pl.multiple_of(j * tq, tq)
            kj = k_ref[pl.ds(start, tq), :]
            vj = v_ref[pl.ds(start, tq), :]
            s = lax.dot_general(q, kj, (((1,), (1,)), ((), ())), preferred_element_type=F32) * scale
            ls = _softplus_neg(s)
            if diag:
                ls = jnp.where(causal, ls, 0.0)
            cs = _hilo_dot(ls, ext)
            excl, tot = cs[:, :tq], cs[:, tq:]
            lw = s + ls + excl + jnp.tile(run, (1, tq // LANES))
            w = jnp.exp(lw)
            if diag:
                w = jnp.where(causal, w, 0.0)
            acc = acc + jnp.dot(w.astype(BF16), vj, preferred_element_type=F32)
            return acc, run + tot

        carry = step(i, (jnp.zeros((tq, dh), F32), jnp.zeros((tq, LANES), F32)), True)
        acc, run = lax.fori_loop(0, i, lambda t, c: step(i - 1 - t, c, False), carry)
        o_ref[...] = acc.astype(BF16)
        tot_ref[...] = run
        if ng:
            @pl.when((hd == n_heads - 1) & (i == nq - 1))
            def _():
                _gather_finish(full, ssem, rsem)

    T = _gather_items(gather) if ng else 0
    return pl.pallas_call(
        body, name=name,
        out_shape=(jax.ShapeDtypeStruct((S, A), BF16), jax.ShapeDtypeStruct((n_heads, S, LANES), F32))
        + tuple(jax.ShapeDtypeStruct(b.shape, b.dtype) for b in gather),
        grid=(n_heads, nq),
        in_specs=[pl.BlockSpec((tq, dh), lambda hd, i: (i, hd)),
                  pl.BlockSpec((S, dh), lambda hd, i: (0, n_heads + hd)),
                  pl.BlockSpec((S, dh), lambda hd, i: (0, 2 * n_heads + hd))] + [ANY] * ng,
        out_specs=(pl.BlockSpec((tq, dh), lambda hd, i: (i, hd)),
                   pl.BlockSpec((None, tq, LANES), lambda hd, i: (hd, i, 0))) + (ANY,) * ng,
        input_output_aliases={3 + a: 2 + a for a in range(ng)},
        scratch_shapes=[pltpu.SemaphoreType.DMA((6 * T,)), pltpu.SemaphoreType.DMA((6 * T,))] if ng else [],
        compiler_params=_params("arbitrary", "arbitrary") if ng else _params("parallel", "parallel"),
    )(h, h, h, *gather)


def _attn_bwd(h, do, tot, n_heads, name, exchange=()):
    S = h.shape[0]
    dh = HEAD_DIM
    A = n_heads * dh
    tq = _pick(S, 256)
    nq = S // tq
    scale = 1.0 / float(dh) ** 0.5
    tn_dims = (((0,), (0,)), ((), ()))
    ne = len(exchange)

    def body(*refs):
        q_ref, k_ref, v_ref, do_ref, tot_ref = refs[:5]
        parts = refs[5:5 + ne]
        dq_ref, dk_ref, dv_ref = refs[5 + ne:8 + ne]
        land = refs[8 + ne:8 + 2 * ne]
        dk_acc, dv_acc = refs[8 + 2 * ne:10 + 2 * ne]
        hd = pl.program_id(0)
        i = pl.program_id(1)
        if ne:
            ssem, rsem = refs[10 + 2 * ne:]

            @pl.when((hd == 0) & (i == 0))
            def _():
                _exchange_start(parts, land, ssem, rsem)

        @pl.when(i == 0)
        def _():
            dk_acc[...] = jnp.zeros_like(dk_acc)
            dv_acc[...] = jnp.zeros_like(dv_acc)

        q = q_ref[...]
        dob = do_ref[...]
        total = jnp.tile(tot_ref[...], (1, tq // LANES))
        r_io = lax.broadcasted_iota(jnp.int32, (tq, tq), 0)
        c_io = lax.broadcasted_iota(jnp.int32, (tq, tq), 1)
        ones = jnp.ones((tq, LANES), BF16)
        upto = jnp.concatenate([(r_io <= c_io).astype(BF16), ones], axis=1)
        before = jnp.concatenate([(r_io < c_io).astype(BF16), ones], axis=1)
        causal = c_io < r_io

        def step(j, carry, diag):
            dq, prun, erun = carry
            start = pl.multiple_of(j * tq, tq)
            kj = k_ref[pl.ds(start, tq), :]
            vj = v_ref[pl.ds(start, tq), :]
            s = lax.dot_general(q, kj, (((1,), (1,)), ((), ())), preferred_element_type=F32) * scale
            ls = _softplus_neg(s)
            if diag:
                ls = jnp.where(causal, ls, 0.0)
            cs = _hilo_dot(ls, upto)
            pin, ptot = cs[:, :tq], cs[:, tq:]
            excl = total - jnp.tile(prun, (1, tq // LANES)) - pin
            w = jnp.exp(s + ls + excl)
            if diag:
                w = jnp.where(causal, w, 0.0)
            dw = lax.dot_general(dob, vj, (((1,), (1,)), ((), ())), preferred_element_type=F32)
            e = dw * w
            ce = jnp.dot(e.astype(BF16), before, preferred_element_type=F32)
            ein, etot = ce[:, :tq], ce[:, tq:]
            ecum = ein + jnp.tile(erun, (1, tq // LANES))
            sn = jnp.exp(ls)
            dz = e * sn - (1.0 - sn) * ecum
            if diag:
                dz = jnp.where(causal, dz, 0.0)
            ds = (dz * scale).astype(BF16)
            wb = w.astype(BF16)
            dq = dq + jnp.dot(ds, kj, preferred_element_type=F32)
            dk_acc[pl.ds(start, tq), :] += lax.dot_general(ds, q, tn_dims, preferred_element_type=F32)
            dv_acc[pl.ds(start, tq), :] += lax.dot_general(wb, dob, tn_dims, preferred_element_type=F32)
            return dq, prun + ptot, erun + etot

        zero = jnp.zeros((tq, LANES), F32)
        carry = lax.fori_loop(0, i, lambda j, c: step(j, c, False), (jnp.zeros((tq, dh), F32), zero, zero))
        dq, _, _ = step(i, carry, True)
        dq_ref[...] = dq.astype(BF16)

        @pl.when(i == nq - 1)
        def _():
            dk_ref[...] = dk_acc[...].astype(BF16)
            dv_ref[...] = dv_acc[...].astype(BF16)

        if ne:
            @pl.when((hd == n_heads - 1) & (i == nq - 1))
            def _():
                _exchange_finish(parts, land, ssem, rsem)

    qblk = pl.BlockSpec((tq, dh), lambda hd, i: (i, hd))
    full = pl.BlockSpec((S, dh), lambda hd, i: (0, hd))
    scratch = [pltpu.VMEM((S, dh), F32), pltpu.VMEM((S, dh), F32)]
    if ne:
        scratch += [pltpu.SemaphoreType.DMA((3 * ne,)), pltpu.SemaphoreType.DMA((3 * ne,))]
    return pl.pallas_call(
        body, name=name,
        out_shape=tuple(jax.ShapeDtypeStruct((S, A), BF16) for _ in range(3)) + tuple(_land_shape(p) for p in exchange),
        grid=(n_heads, nq),
        in_specs=[qblk,
                  pl.BlockSpec((S, dh), lambda hd, i: (0, n_heads + hd)),
                  pl.BlockSpec((S, dh), lambda hd, i: (0, 2 * n_heads + hd)),
                  qblk,
                  pl.BlockSpec((None, tq, LANES), lambda hd, i: (hd, i, 0))] + [ANY] * ne,
        out_specs=(qblk, full, full) + (ANY,) * ne,
        scratch_shapes=scratch,
        compiler_params=_params("arbitrary", "arbitrary") if ne else _params("parallel", "arbitrary"),
    )(h, h, h, do, tot, *exchange)


def _evenconv_fwd(h, dw_w, dw_b, bn_g, bn_b, name):
    S = h.shape[0]
    KW, A = dw_w.shape
    H = HALO31
    tr = _pick(S, 256, H)
    first_tap = H - (KW - 1)

    def body(ap_ref, a_ref, gp_ref, g_ref, w_ref, b_ref, bg_ref, bb_ref, u1_ref, u3_ref, U):
        i = pl.program_id(1)
        glu_prev = ap_ref[...].astype(F32) * _sigmoid(gp_ref[...].astype(F32))
        U[0:H, :] = jnp.where(i > 0, glu_prev, 0.0)
        U[H:H + tr, :] = a_ref[...].astype(F32) * _sigmoid(g_ref[...].astype(F32))
        acc = b_ref[...] + w_ref[pl.ds(0, 1), :] * U[pl.ds(first_tap, tr), :]
        for k in range(1, KW):
            acc = acc + w_ref[pl.ds(k, 1), :] * U[pl.ds(first_tap + k, tr), :]
        u1_ref[...] = acc
        mu = jnp.mean(acc, axis=-1, keepdims=True)
        d = acc - mu
        var = jnp.mean(d * d, axis=-1, keepdims=True)
        u2 = d * lax.rsqrt(var + LN_EPS) * bg_ref[...] + bb_ref[...]
        u3_ref[...] = (u2 * _sigmoid(u2)).astype(BF16)

    acol = lambda c: 3
    gcol = lambda c: 4
    vec = pl.BlockSpec((1, A), lambda c, i: (0, 0))
    blk = pl.BlockSpec((tr, A), lambda c, i: (i, 0))
    return pl.pallas_call(
        body, name=name,
        out_shape=(jax.ShapeDtypeStruct((S, A), F32), jax.ShapeDtypeStruct((S, A), BF16)),
        grid=(1, S // tr),
        in_specs=[_prev_spec(tr, H, A, acol), _cur_spec(tr, A, acol),
                  _prev_spec(tr, H, A, gcol), _cur_spec(tr, A, gcol),
                  pl.BlockSpec((KW, A), lambda c, i: (0, 0)), vec, vec, vec],
        out_specs=(blk, blk),
        scratch_shapes=[pltpu.VMEM((H + tr, A), F32)],
        compiler_params=_params("parallel", "parallel"),
    )(h, h, h, h, dw_w, dw_b, bn_g, bn_b)


def _evenconv_bwd(du3, u1, h, dw_w, bn_g, bn_b, name):
    S = h.shape[0]
    KW, A = dw_w.shape
    H = HALO31
    tr = _pick(S, 256, H)
    nr = S // tr
    n = tr + H
    first_tap = H - (KW - 1)

    def body(d3_ref, d3n_ref, u1_ref, u1n_ref, ap_ref, a_ref, gp_ref, g_ref, w_ref, bg_ref, bb_ref,
             da_ref, dg_ref, dww_ref, dwb_ref, dbg_ref, dbb_ref, U0, DU):
        i = pl.program_id(1)
        u1 = jnp.concatenate([u1_ref[...], u1n_ref[...]], axis=0)
        d3 = jnp.concatenate([d3_ref[...], d3n_ref[...]], axis=0).astype(F32)
        rows = lax.broadcasted_iota(jnp.int32, (n, 1), 0)
        d3 = jnp.where((rows < tr) | (i < nr - 1), d3, 0.0)
        mu = jnp.mean(u1, axis=-1, keepdims=True)
        d = u1 - mu
        var = jnp.mean(d * d, axis=-1, keepdims=True)
        rstd = lax.rsqrt(var + LN_EPS)
        xh = d * rstd
        u2 = xh * bg_ref[...] + bb_ref[...]
        sg = _sigmoid(u2)
        du2 = d3 * (sg * (1.0 + u2 * (1.0 - sg)))
        dxh = du2 * bg_ref[...]
        m1 = jnp.mean(dxh, axis=-1, keepdims=True)
        m2 = jnp.mean(dxh * xh, axis=-1, keepdims=True)
        du1 = rstd * (dxh - m1 - xh * m2)
        DU[...] = du1
        pbg = jnp.sum(du2[0:tr] * xh[0:tr], axis=0, keepdims=True)
        pbb = jnp.sum(du2[0:tr], axis=0, keepdims=True)
        pwb = jnp.sum(du1[0:tr], axis=0, keepdims=True)

        glu_prev = ap_ref[...].astype(F32) * _sigmoid(gp_ref[...].astype(F32))
        U0[0:H, :] = jnp.where(i > 0, glu_prev, 0.0)
        a = a_ref[...].astype(F32)
        sgg = _sigmoid(g_ref[...].astype(F32))
        U0[H:H + tr, :] = a * sgg

        @pl.when(i == 0)
        def _():
            dbg_ref[...] = pbg
            dbb_ref[...] = pbb
            dwb_ref[...] = pwb
            dww_ref[...] = jnp.zeros_like(dww_ref)

        @pl.when(i > 0)
        def _():
            dbg_ref[...] += pbg
            dbb_ref[...] += pbb
            dwb_ref[...] += pwb

        du0 = w_ref[pl.ds(0, 1), :] * DU[pl.ds(KW - 1, tr), :]
        for k in range(1, KW):
            du0 = du0 + w_ref[pl.ds(k, 1), :] * DU[pl.ds(KW - 1 - k, tr), :]
        da_ref[...] = (du0 * sgg).astype(BF16)
        dg_ref[...] = (du0 * a * sgg * (1.0 - sgg)).astype(BF16)
        dcur = DU[pl.ds(0, tr), :]
        for k in range(KW):
            dww_ref[pl.ds(k, 1), :] += jnp.sum(dcur * U0[pl.ds(first_tap + k, tr), :], axis=0, keepdims=True)

    acol = lambda c: 3
    gcol = lambda c: 4
    one = lambda c: 1
    zero = lambda c: 0
    vec = pl.BlockSpec((1, A), lambda c, i: (0, 0))
    blk = pl.BlockSpec((tr, A), lambda c, i: (i, 0))
    return pl.pallas_call(
        body, name=name,
        out_shape=(jax.ShapeDtypeStruct((S, A), BF16), jax.ShapeDtypeStruct((S, A), BF16),
                   jax.ShapeDtypeStruct((KW, A), F32), jax.ShapeDtypeStruct((1, A), F32),
                   jax.ShapeDtypeStruct((1, A), F32), jax.ShapeDtypeStruct((1, A), F32)),
        grid=(1, nr),
        in_specs=[_cur_spec(tr, A, one), _next_spec(tr, H, A, one, S),
                  _cur_spec(tr, A, zero), _next_spec(tr, H, A, zero, S),
                  _prev_spec(tr, H, A, acol), _cur_spec(tr, A, acol),
                  _prev_spec(tr, H, A, gcol), _cur_spec(tr, A, gcol),
                  pl.BlockSpec((KW, A), lambda c, i: (0, 0)), vec, vec],
        out_specs=(blk, blk, pl.BlockSpec((KW, A), lambda c, i: (0, 0)), vec, vec, vec),
        scratch_shapes=[pltpu.VMEM((H + tr, A), F32), pltpu.VMEM((n, A), F32)],
        compiler_params=_params("arbitrary", "arbitrary"),
    )(du3, du3, u1, u1, h, h, h, h, dw_w, bn_g, bn_b)


def _pool_inv_count(row0, nrows, window):
    t = row0 + lax.broadcasted_iota(jnp.int32, (nrows, 1), 0)
    return 1.0 / jnp.minimum(t + 1, window).astype(F32)


def _odd_fwd(h, conv_w, pool_w, pool_scale, name):
    S = h.shape[0]
    C = conv_w.shape[1]
    G = len(POOL_WINDOWS)
    Dg = C // G
    H = HALO3
    tr = _pick(S, 256, H)

    def body(cb_ref, ccp_ref, cc_ref, chp_ref, ch_ref, pp_ref, p_ref, w_ref, pw_ref, sc_ref, mix_ref, M, P):
        i = pl.program_id(1)
        M[0:H, :] = jnp.where(i > 0, ccp_ref[...].astype(F32) * chp_ref[...].astype(F32), 0.0)
        M[H:H + tr, :] = cc_ref[...].astype(F32) * ch_ref[...].astype(F32)
        cm = (w_ref[pl.ds(0, 1), :] * M[pl.ds(H - 2, tr), :] + w_ref[pl.ds(1, 1), :] * M[pl.ds(H - 1, tr), :]
              + w_ref[pl.ds(2, 1), :] * M[pl.ds(H, tr), :])
        mix_ref[:, 0:C] = (cb_ref[...].astype(F32) * cm).astype(BF16)
        P[0:H, :] = jnp.where(i > 0, pp_ref[...].astype(F32), 0.0)
        P[H:H + tr, :] = p_ref[...].astype(F32)
        for gi, window in enumerate(POOL_WINDOWS):
            cols = pl.ds(gi * Dg, Dg)
            wsum = P[pl.ds(H, tr), cols]
            for dlt in range(1, window):
                wsum = wsum + P[pl.ds(H - dlt, tr), cols]
            diff = wsum * _pool_inv_count(i * tr, tr, window) - P[pl.ds(H, tr), cols]
            yd = jnp.dot(diff.astype(BF16), pw_ref[gi], preferred_element_type=F32) * sc_ref[:, cols]
            mix_ref[:, pl.ds(C + gi * Dg, Dg)] = yd.astype(BF16)

    col = lambda k: (lambda c: k)
    return pl.pallas_call(
        body, name=name, out_shape=jax.ShapeDtypeStruct((S, 2 * C), BF16),
        grid=(1, S // tr),
        in_specs=[_cur_spec(tr, C, col(0)),
                  _prev_spec(tr, H, C, col(1)), _cur_spec(tr, C, col(1)),
                  _prev_spec(tr, H, C, col(2)), _cur_spec(tr, C, col(2)),
                  _prev_spec(tr, H, C, col(3)), _cur_spec(tr, C, col(3)),
                  pl.BlockSpec((3, C), lambda c, i: (0, 0)),
                  pl.BlockSpec((G, Dg, Dg), lambda c, i: (0, 0, 0)),
                  pl.BlockSpec((1, C), lambda c, i: (0, 0))],
        out_specs=pl.BlockSpec((tr, 2 * C), lambda c, i: (i, 0)),
        scratch_shapes=[pltpu.VMEM((H + tr, C), F32), pltpu.VMEM((H + tr, C), F32)],
        compiler_params=_params("parallel", "parallel"),
    )(h, h, h, h, h, h, h, conv_w, pool_w, pool_scale)


def _odd_bwd(dmix, h, conv_w, pool_w, pool_scale, name):
    S = h.shape[0]
    C = conv_w.shape[1]
    G = len(POOL_WINDOWS)
    Dg = C // G
    H = HALO3
    tr = _pick(S, 256, H)
    nr = S // tr
    n = tr + H
    nt_dims = (((1,), (1,)), ((), ()))
    tn_dims = (((0,), (0,)), ((), ()))

    def body(dyc_ref, dycn_ref, dyd_ref, dydn_ref, cb_ref, cbn_ref, ccp_ref, cc_ref, ccn_ref,
             chp_ref, ch_ref, chn_ref, pp_ref, p_ref, w_ref, pw_ref, sc_ref,
             dh_ref, dw_ref, dpw_ref, dsc_ref, M, DCM, P, Q):
        i = pl.program_id(1)
        rows = lax.broadcasted_iota(jnp.int32, (n, 1), 0)
        valid = (rows < tr) | (i < nr - 1)

        @pl.when(i == 0)
        def _():
            dw_ref[...] = jnp.zeros_like(dw_ref)
            dpw_ref[...] = jnp.zeros_like(dpw_ref)
            dsc_ref[...] = jnp.zeros_like(dsc_ref)

        M[0:H, :] = jnp.where(i > 0, ccp_ref[...].astype(F32) * chp_ref[...].astype(F32), 0.0)
        cc = cc_ref[...].astype(F32)
        ch = ch_ref[...].astype(F32)
        M[H:H + tr, :] = cc * ch
        M[H + tr:H + tr + H, :] = ccn_ref[...].astype(F32) * chn_ref[...].astype(F32)
        w0, w1, w2 = w_ref[pl.ds(0, 1), :], w_ref[pl.ds(1, 1), :], w_ref[pl.ds(2, 1), :]
        cm = w0 * M[pl.ds(H - 2, tr), :] + w1 * M[pl.ds(H - 1, tr), :] + w2 * M[pl.ds(H, tr), :]
        dyc = jnp.concatenate([dyc_ref[...], dycn_ref[...]], axis=0).astype(F32)
        dyc = jnp.where(valid, dyc, 0.0)
        cbf = jnp.concatenate([cb_ref[...], cbn_ref[...]], axis=0).astype(F32)
        dh_ref[:, 0:C] = (dyc[0:tr] * cm).astype(BF16)
        DCM[...] = dyc * cbf
        dm = w2 * DCM[pl.ds(0, tr), :] + w1 * DCM[pl.ds(1, tr), :] + w0 * DCM[pl.ds(2, tr), :]
        dh_ref[:, C:2 * C] = (dm * ch).astype(BF16)
        dh_ref[:, 2 * C:3 * C] = (dm * cc).astype(BF16)
        dcur = DCM[pl.ds(0, tr), :]
        for k in range(3):
            dw_ref[pl.ds(k, 1), :] += jnp.sum(dcur * M[pl.ds(H - 2 + k, tr), :], axis=0, keepdims=True)

        P[0:H, :] = jnp.where(i > 0, pp_ref[...].astype(F32), 0.0)
        P[H:H + tr, :] = p_ref[...].astype(F32)
        dyd = jnp.concatenate([dyd_ref[...], dydn_ref[...]], axis=0).astype(F32)
        dyd = jnp.where(valid, dyd, 0.0)
        for gi, window in enumerate(POOL_WINDOWS):
            cols = pl.ds(gi * Dg, Dg)
            lo = gi * Dg
            wsum = P[pl.ds(H, tr), cols]
            for dlt in range(1, window):
                wsum = wsum + P[pl.ds(H - dlt, tr), cols]
            diff = (wsum * _pool_inv_count(i * tr, tr, window) - P[pl.ds(H, tr), cols]).astype(BF16)
            pw = pw_ref[gi]
            dyd_g = dyd[:, lo:lo + Dg]
            e = (dyd_g * sc_ref[:, cols]).astype(BF16)
            yraw = jnp.dot(diff, pw, preferred_element_type=F32)
            dsc_ref[:, cols] += jnp.sum(dyd_g[0:tr] * yraw, axis=0, keepdims=True)
            dpw_ref[gi] += lax.dot_general(diff, e[0:tr], tn_dims, preferred_element_type=F32)
            ddiff = lax.dot_general(e, pw, nt_dims, preferred_element_type=F32)
            Q[:, cols] = ddiff * _pool_inv_count(i * tr, n, window)
            acc = Q[pl.ds(0, tr), cols]
            for dlt in range(1, window):
                acc = acc + Q[pl.ds(dlt, tr), cols]
            dh_ref[:, pl.ds(3 * C + lo, Dg)] = (acc - ddiff[0:tr]).astype(BF16)

    col = lambda k: (lambda c: k)
    return pl.pallas_call(
        body, name=name,
        out_shape=(jax.ShapeDtypeStruct((S, 4 * C), BF16), jax.ShapeDtypeStruct((3, C), F32),
                   jax.ShapeDtypeStruct((G, Dg, Dg), F32), jax.ShapeDtypeStruct((1, C), F32)),
        grid=(1, nr),
        in_specs=[_cur_spec(tr, C, col(0)), _next_spec(tr, H, C, col(0), S),
                  _cur_spec(tr, C, col(1)), _next_spec(tr, H, C, col(1), S),
                  _cur_spec(tr, C, col(0)), _next_spec(tr, H, C, col(0), S),
                  _prev_spec(tr, H, C, col(1)), _cur_spec(tr, C, col(1)), _next_spec(tr, H, C, col(1), S),
                  _prev_spec(tr, H, C, col(2)), _cur_spec(tr, C, col(2)), _next_spec(tr, H, C, col(2), S),
                  _prev_spec(tr, H, C, col(3)), _cur_spec(tr, C, col(3)),
                  pl.BlockSpec((3, C), lambda c, i: (0, 0)),
                  pl.BlockSpec((G, Dg, Dg), lambda c, i: (0, 0, 0)),
                  pl.BlockSpec((1, C), lambda c, i: (0, 0))],
        out_specs=(pl.BlockSpec((tr, 4 * C), lambda c, i: (i, 0)),
                   pl.BlockSpec((3, C), lambda c, i: (0, 0)),
                   pl.BlockSpec((G, Dg, Dg), lambda c, i: (0, 0, 0)),
                   pl.BlockSpec((1, C), lambda c, i: (0, 0))),
        scratch_shapes=[pltpu.VMEM((H + tr + H, C), F32), pltpu.VMEM((n, C), F32),
                        pltpu.VMEM((H + tr, C), F32), pltpu.VMEM((n, C), F32)],
        compiler_params=_params("arbitrary", "arbitrary"),
    )(dmix, dmix, dmix, dmix, h, h, h, h, h, h, h, h, h, h, conv_w, pool_w, pool_scale)


def _adamw(w, g, m, v, name):
    L, R, C = w.shape
    tr, tc = _pick(R, 256, 8), _pick(C, 1408)
    c1 = 1.0 / (1.0 - ADAM_B1 ** ADAM_STEP)
    c2 = 1.0 / (1.0 - ADAM_B2 ** ADAM_STEP)

    def body(w_ref, g_ref, m_ref, v_ref, go_ref, d_ref, mo_ref, vo_ref):
        gg = g_ref[...]
        mn = ADAM_B1 * m_ref[...] + (1.0 - ADAM_B1) * gg
        vn = ADAM_B2 * v_ref[...] + (1.0 - ADAM_B2) * (gg * gg)
        d_ref[...] = -ADAM_LR * ((mn * c1) / (jnp.sqrt(vn * c2) + ADAM_EPS) + ADAM_WD * w_ref[...])
        go_ref[...] = gg
        mo_ref[...] = mn
        vo_ref[...] = vn

    blk = pl.BlockSpec((None, tr, tc), lambda l, i, j: (l, i, j))
    sds = jax.ShapeDtypeStruct(w.shape, F32)
    return pl.pallas_call(
        body, name=name, out_shape=(sds, sds, sds, sds),
        grid=(L, R // tr, C // tc),
        in_specs=[blk, blk, blk, blk], out_specs=(blk, blk, blk, blk),
        compiler_params=_params("parallel", "parallel", "parallel"),
    )(w, g, m, v)


def _sum_slots(buf, name):
    N, R, C = buf.shape
    tr = _pick(R, 512, 8)

    def body(b_ref, o_ref):
        acc = b_ref[0]
        for k in range(1, N):
            acc = acc + b_ref[k]
        o_ref[...] = acc

    return pl.pallas_call(
        body, name=name, out_shape=jax.ShapeDtypeStruct((R, C), F32),
        grid=(R // tr,),
        in_specs=[pl.BlockSpec((N, tr, C), lambda i: (0, i, 0))],
        out_specs=pl.BlockSpec((tr, C), lambda i: (i, 0)),
        compiler_params=_params("parallel"),
    )(buf)


def _pair_sum(pos, g, rsib, name):
    _, hr, hc = rsib.shape
    tr, tc = _pick(hr, 512, 16), _pick(hc, 1024)

    def body(p_ref, g_ref, r_ref, o_ref):
        o_ref[...] = (g_ref[...].astype(F32) + r_ref[...].astype(F32)).astype(BF16)

    blk = pl.BlockSpec((None, tr, tc), lambda s, i, j, p: (s, i, j))
    return pl.pallas_call(
        body, name=name, out_shape=jax.ShapeDtypeStruct(rsib.shape, BF16),
        grid_spec=pltpu.PrefetchScalarGridSpec(
            num_scalar_prefetch=1, grid=(N_CHIPS, hr // tr, hc // tc),
            in_specs=[pl.BlockSpec((None, None, tr, tc), lambda s, i, j, p: (s, p[1], i, j)), blk],
            out_specs=blk),
        compiler_params=_params("parallel", "parallel", "parallel"),
    )(pos, g, rsib)


def _chip_sum(pos, part, land, name):
    _, sr, sc = land.shape
    tr, tc = _pick(sr, 512, 16), _pick(sc, 1024)

    def body(p_ref, own_ref, l_ref, o_ref):
        acc = own_ref[...].astype(F32)
        for k in range(3):
            acc = acc + l_ref[k].astype(F32)
        o_ref[...] = acc

    return pl.pallas_call(
        body, name=name, out_shape=jax.ShapeDtypeStruct((2, sr, sc), F32),
        grid_spec=pltpu.PrefetchScalarGridSpec(
            num_scalar_prefetch=1, grid=(sr // tr, sc // tc),
            in_specs=[pl.BlockSpec((None, tr, tc), lambda i, j, p: (p[0], i, j)),
                      pl.BlockSpec((3, tr, tc), lambda i, j, p: (0, i, j))],
            out_specs=pl.BlockSpec((None, tr, tc), lambda i, j, p: (p[1], i, j))),
        compiler_params=_params("parallel", "parallel"),
    )(pos, part, land)


def _place():
    x, y, c = lax.axis_index("x"), lax.axis_index("y"), lax.axis_index("c")
    return x, y, c


def _other_chips(x, y):
    return [(1 - x, y), (x, 1 - y), (1 - x, 1 - y)]


def _rcopy(src, dst, ssem, rsem, dev):
    return pltpu.make_async_remote_copy(src_ref=src, dst_ref=dst, send_sem=ssem, recv_sem=rsem,
                                        device_id=dev, device_id_type=MESH)


def _gather_items(bufs):
    return sum(b.shape[0] for b in bufs)


def _gather_walk(full):
    t = 0
    for ref in full:
        for l in range(ref.shape[0]):
            yield t, ref, l
            t += 1


def _gather_start(full, ssem, rsem):
    x, y, c = _place()
    j = 2 * x + y
    for t, ref, l in _gather_walk(full):
        own = ref.at[l, j, c]
        for r, (px, py) in enumerate(_other_chips(x, y)):
            _rcopy(own, own, ssem.at[6 * t + r], rsem.at[6 * t + r], (px, py, c)).start()


def _gather_forward(full, ssem, rsem):
    x, y, c = _place()
    for t, ref, l in _gather_walk(full):
        for r, (px, py) in enumerate(_other_chips(x, y)):
            slab = ref.at[l, 2 * px + py, c]
            _rcopy(slab, slab, ssem.at[6 * t + r], rsem.at[6 * t + r], (px, py, c)).wait_recv()
            _rcopy(slab, slab, ssem.at[6 * t + 3 + r], rsem.at[6 * t + 3 + r], (x, y, 1 - c)).start()


def _gather_finish(full, ssem, rsem):
    x, y, c = _place()
    j = 2 * x + y
    for t, ref, l in _gather_walk(full):
        for r, (px, py) in enumerate(_other_chips(x, y)):
            got = ref.at[l, 2 * px + py, 1 - c]
            _rcopy(got, got, ssem.at[6 * t + 3 + r], rsem.at[6 * t + 3 + r], (x, y, 1 - c)).wait_recv()
    for t, ref, l in _gather_walk(full):
        own = ref.at[l, j, c]
        for r, (px, py) in enumerate(_other_chips(x, y)):
            _rcopy(own, own, ssem.at[6 * t + r], rsem.at[6 * t + r], (px, py, c)).wait_send()
            slab = ref.at[l, 2 * px + py, c]
            _rcopy(slab, slab, ssem.at[6 * t + 3 + r], rsem.at[6 * t + 3 + r], (x, y, 1 - c)).wait_send()


def _land_shape(part):
    return jax.ShapeDtypeStruct((3,) + part.shape[1:], part.dtype)


def _exchange_start(parts, land, ssem, rsem):
    x, y, c = _place()
    for a in range(len(parts)):
        for r, (px, py) in enumerate(_other_chips(x, y)):
            _rcopy(parts[a].at[2 * px + py], land[a].at[r], ssem.at[3 * a + r], rsem.at[3 * a + r],
                   (px, py, c)).start()


def _exchange_finish(parts, land, ssem, rsem):
    x, y, c = _place()
    for a in range(len(parts)):
        for r, (px, py) in enumerate(_other_chips(x, y)):
            _rcopy(parts[a].at[2 * px + py], land[a].at[r], ssem.at[3 * a + r], rsem.at[3 * a + r],
                   (px, py, c)).wait()


def _allgather_big(bufs, name):
    n = len(bufs)
    T = _gather_items(bufs)

    def body(*refs):
        full = refs[n:2 * n]
        ssem, rsem = refs[2 * n:]
        _gather_start(full, ssem, rsem)
        _gather_forward(full, ssem, rsem)
        _gather_finish(full, ssem, rsem)

    return pl.pallas_call(
        body, name=name, out_shape=tuple(jax.ShapeDtypeStruct(b.shape, BF16) for b in bufs),
        in_specs=[ANY] * n, out_specs=tuple([ANY] * n),
        input_output_aliases={a: a for a in range(n)},
        scratch_shapes=[pltpu.SemaphoreType.DMA((6 * T,)), pltpu.SemaphoreType.DMA((6 * T,))],
    )(*bufs)


def _allgather_small(shards, name):
    n = len(shards)
    outs = tuple(jax.ShapeDtypeStruct((N_CHIPS,) + s.shape, s.dtype) for s in shards)

    def body(*refs):
        ins, full = refs[:n], refs[n:2 * n]
        ssem, rsem, lsem = refs[2 * n:]
        x, y, c = _place()
        j = 2 * x + y
        chips = _other_chips(x, y)
        cps, locs = [], []
        for a in range(n):
            loc = pltpu.make_async_copy(ins[a], full[a].at[j], lsem.at[a])
            loc.start()
            locs.append(loc)
            for r, (px, py) in enumerate(chips):
                cp = _rcopy(ins[a], full[a].at[j], ssem.at[3 * a + r], rsem.at[3 * a + r], (px, py, c))
                cp.start()
                cps.append(cp)
        for a in range(n):
            for r, (px, py) in enumerate(chips):
                dst = full[a].at[2 * px + py]
                _rcopy(dst, dst, ssem.at[3 * a + r], rsem.at[3 * a + r], (px, py, c)).wait_recv()
        for cp in cps:
            cp.wait_send()
        for loc in locs:
            loc.wait()

    return pl.pallas_call(
        body, name=name, out_shape=outs,
        in_specs=[ANY] * n, out_specs=tuple([ANY] * n),
        scratch_shapes=[pltpu.SemaphoreType.DMA((3 * n,)), pltpu.SemaphoreType.DMA((3 * n,)),
                        pltpu.SemaphoreType.DMA((n,))],
    )(*shards)


def _pair_exchange(grads, name):
    n = len(grads)
    outs = [jax.ShapeDtypeStruct((N_CHIPS,) + g.shape[2:], BF16) for g in grads]

    def body(*refs):
        ins, got = refs[:n], refs[n:2 * n]
        ssem, rsem = refs[2 * n:]
        x, y, c = _place()
        cps = []
        for a in range(n):
            for s in range(N_CHIPS):
                cp = _rcopy(ins[a].at[s, 1 - c], got[a].at[s], ssem.at[N_CHIPS * a + s],
                            rsem.at[N_CHIPS * a + s], (x, y, 1 - c))
                cp.start()
                cps.append(cp)
        for cp in cps:
            cp.wait()

    return pl.pallas_call(
        body, name=name, out_shape=tuple(outs),
        in_specs=[ANY] * n, out_specs=tuple([ANY] * n),
        scratch_shapes=[pltpu.SemaphoreType.DMA((N_CHIPS * n,)), pltpu.SemaphoreType.DMA((N_CHIPS * n,))],
    )(*grads)


def _chip_exchange(parts, name):
    n = len(parts)

    def body(*refs):
        ins, land = refs[:n], refs[n:2 * n]
        ssem, rsem = refs[2 * n:]
        _exchange_start(ins, land, ssem, rsem)
        _exchange_finish(ins, land, ssem, rsem)

    return pl.pallas_call(
        body, name=name, out_shape=tuple(_land_shape(p) for p in parts),
        in_specs=[ANY] * n, out_specs=tuple([ANY] * n),
        scratch_shapes=[pltpu.SemaphoreType.DMA((3 * n,)), pltpu.SemaphoreType.DMA((3 * n,))],
    )(*parts)


def _half_swap(bufs, name):
    n = len(bufs)

    def body(*refs):
        full = refs[n:2 * n]
        ssem, rsem = refs[2 * n:]
        x, y, c = _place()
        cps = []
        for t in range(n):
            mine = full[t].at[c]
            cp = _rcopy(mine, mine, ssem.at[t], rsem.at[t], (x, y, 1 - c))
            cp.start()
            cps.append(cp)
        for t in range(n):
            got = full[t].at[1 - c]
            _rcopy(got, got, ssem.at[t], rsem.at[t], (x, y, 1 - c)).wait_recv()
        for cp in cps:
            cp.wait_send()

    return pl.pallas_call(
        body, name=name, out_shape=tuple(jax.ShapeDtypeStruct(b.shape, F32) for b in bufs),
        in_specs=[ANY] * n, out_specs=tuple([ANY] * n),
        input_output_aliases={a: a for a in range(n)},
        scratch_shapes=[pltpu.SemaphoreType.DMA((n,)), pltpu.SemaphoreType.DMA((n,))],
    )(*bufs)


def _gather_all_devices(buf, name):
    R, C = buf.shape

    def body(b_ref, o_ref, ssem, rsem, lsem):
        x, y, c = _place()
        me = 4 * x + 2 * y + c
        loc = pltpu.make_async_copy(b_ref, o_ref.at[me], lsem)
        loc.start()
        cps = []
        for m in range(1, N_DEV):
            fx, fy, fc = (m >> 2) & 1, (m >> 1) & 1, m & 1
            px = x + fx - 2 * x * fx
            py = y + fy - 2 * y * fy
            pc = c + fc - 2 * c * fc
            cp = _rcopy(b_ref, o_ref.at[me], ssem.at[m - 1], rsem.at[m - 1], (px, py, pc))
            cp.start()
            cps.append((cp, 4 * px + 2 * py + pc))
        for m, (cp, peer) in enumerate(cps):
            dst = o_ref.at[peer]
            _rcopy(dst, dst, ssem.at[m], rsem.at[m], (x, y, c)).wait_recv()
        for cp, _ in cps:
            cp.wait_send()
        loc.wait()

    return pl.pallas_call(
        body, name=name, out_shape=jax.ShapeDtypeStruct((N_DEV, R, C), F32),
        in_specs=[ANY], out_specs=ANY,
        scratch_shapes=[pltpu.SemaphoreType.DMA((N_DEV - 1,)), pltpu.SemaphoreType.DMA((N_DEV - 1,)),
                        pltpu.SemaphoreType.DMA],
    )(buf)


def _pack(arrs):
    flat = jnp.concatenate([a.reshape(-1) for a in arrs])
    rows = -(-flat.shape[0] // (8 * LANES)) * 8
    flat = jnp.pad(flat, (0, rows * LANES - flat.shape[0]))
    return flat.reshape(rows, LANES)


def _unpack(buf, shapes):
    flat = buf.reshape(-1)
    out, off = [], 0
    for s in shapes:
        size = 1
        for d in s:
            size *= d
        out.append(flat[off:off + size].reshape(s))
        off += size
    return out


BIG = ("ev_w_in", "ev_w_out", "od_w_in", "od_w_out", "ffn_w_up", "ffn_w_down")
BIG_KIND = {"ev_w_in": "col", "ev_w_out": "row", "od_w_in": "col", "od_w_out": "row",
            "ffn_w_up": "col", "ffn_w_down": "row"}
SMALL_AXIS = {"ev_dw_w": 2, "ev_dw_b": None, "ev_bn_g": None, "ev_bn_b": None, "od_conv_w": 2,
              "od_pool_w": 2, "od_pool_scale": 1, "ffn_conv_w": 2, "ffn_conv_b": None, "ln_g": 2, "ln_b": 2}
WEIGHTS = ("ev_w_in", "ev_dw_w", "ev_dw_b", "ev_bn_g", "ev_bn_b", "ev_w_out", "od_w_in", "od_conv_w",
           "od_pool_w", "od_pool_scale", "od_w_out", "ffn_w_up", "ffn_conv_w", "ffn_conv_b", "ffn_w_down",
           "ln_g", "ln_b")


def _ffn_fwd(xb, w_up, w_down, l, conv_w, conv_b, tag):
    hu = _matmul(xb, w_up, mode="nn", b_lead=l, b_split=True, out_dtype=BF16, name=f"{tag}_up", tm=1024, tn=1408)
    z = _ffn_act_fwd(hu, conv_w, conv_b, name=f"{tag}_act")
    y = _matmul(z, w_down, mode="nn", b_lead=l, out_dtype=F32, name=f"{tag}_down", tm=512, tn=1024, tk=1408)
    return hu, z, y


def _ffn_bwd(drb, dr, alpha, xb, hu, z, w_up, w_down, l, conv_w, conv_b, tag, exchange=None):
    g_down = _matmul(z, drb, mode="tn", out_dtype=BF16, name=f"{tag}_dwdown", tm=512, tn=1024, tk=1024)
    dz = _matmul(drb, w_down, mode="nt", b_lead=l, out_dtype=BF16, name=f"{tag}_dz", tm=1024, tn=1408)
    dg, du, dcw, dcb = _ffn_act_bwd(dz, hu, conv_w, conv_b, name=f"{tag}_actbwd")
    dhu = jnp.concatenate([dg, du], axis=1)
    g_up = _matmul(xb, dhu, mode="tn", out_split=True, out_dtype=BF16, name=f"{tag}_dwup", tm=512, tn=1408, tk=1024)
    dx = _matmul(dhu, w_up, mode="nt", b_lead=l, b_split=True, out_dtype=F32, add=dr, add_scale=alpha,
                 name=f"{tag}_dx", tm=512, tn=1024, tk=1408, exchange=exchange)
    dx, land = dx if exchange is not None else (dx, None)
    return dx, g_up, g_down, dcw, dcb, land


def kernel(x, ev_w_in, ev_dw_w, ev_dw_b, ev_bn_g, ev_bn_b, ev_w_out, od_w_in, od_conv_w, od_pool_w, od_pool_scale, od_w_out, ffn_w_up, ffn_conv_w, ffn_conv_b, ffn_w_down, ln_g, ln_b, loss_target, m_ev_w_in, m_ev_dw_w, m_ev_dw_b, m_ev_bn_g, m_ev_bn_b, m_ev_w_out, m_od_w_in, m_od_conv_w, m_od_pool_w, m_od_pool_scale, m_od_w_out, m_ffn_w_up, m_ffn_conv_w, m_ffn_conv_b, m_ffn_w_down, m_ln_g, m_ln_b, v_ev_w_in, v_ev_dw_w, v_ev_dw_b, v_ev_bn_g, v_ev_bn_b, v_ev_w_out, v_od_w_in, v_od_conv_w, v_od_pool_w, v_od_pool_scale, v_od_w_out, v_ffn_w_up, v_ffn_conv_w, v_ffn_conv_b, v_ffn_w_down, v_ln_g, v_ln_b):
    wts = dict(ev_w_in=ev_w_in, ev_dw_w=ev_dw_w, ev_dw_b=ev_dw_b, ev_bn_g=ev_bn_g, ev_bn_b=ev_bn_b,
               ev_w_out=ev_w_out, od_w_in=od_w_in, od_conv_w=od_conv_w, od_pool_w=od_pool_w,
               od_pool_scale=od_pool_scale, od_w_out=od_w_out, ffn_w_up=ffn_w_up, ffn_conv_w=ffn_conv_w,
               ffn_conv_b=ffn_conv_b, ffn_w_down=ffn_w_down, ln_g=ln_g, ln_b=ln_b)
    mom = dict(ev_w_in=m_ev_w_in, ev_dw_w=m_ev_dw_w, ev_dw_b=m_ev_dw_b, ev_bn_g=m_ev_bn_g, ev_bn_b=m_ev_bn_b,
               ev_w_out=m_ev_w_out, od_w_in=m_od_w_in, od_conv_w=m_od_conv_w, od_pool_w=m_od_pool_w,
               od_pool_scale=m_od_pool_scale, od_w_out=m_od_w_out, ffn_w_up=m_ffn_w_up, ffn_conv_w=m_ffn_conv_w,
               ffn_conv_b=m_ffn_conv_b, ffn_w_down=m_ffn_w_down, ln_g=m_ln_g, ln_b=m_ln_b)
    var = dict(ev_w_in=v_ev_w_in, ev_dw_w=v_ev_dw_w, ev_dw_b=v_ev_dw_b, ev_bn_g=v_ev_bn_g, ev_bn_b=v_ev_bn_b,
               ev_w_out=v_ev_w_out, od_w_in=v_od_w_in, od_conv_w=v_od_conv_w, od_pool_w=v_od_pool_w,
               od_pool_scale=v_od_pool_scale, od_w_out=v_od_w_out, ffn_w_up=v_ffn_w_up, ffn_conv_w=v_ffn_conv_w,
               ffn_conv_b=v_ffn_conv_b, ffn_w_down=v_ffn_w_down, ln_g=v_ln_g, ln_b=v_ln_b)

    S, D = x.shape[1], x.shape[2]
    depth = ln_g.shape[0]
    alpha = (2.0 * depth) ** 0.25
    A = ev_dw_b.shape[-1]
    n_heads = A // HEAD_DIM
    xi, yi, ci = _place()
    chip = 2 * xi + yi
    pos = jnp.stack([chip, ci]).astype(jnp.int32)

    bufs = {k: _cast_into_gather(pos, wts[k], name=f"cast_{k}") for k in BIG}
    early = ("ev_w_in", "ev_w_out")
    late = tuple(k for k in BIG if k not in early)
    bufs.update(zip(early, _allgather_big([bufs[k] for k in early], name="gather_first")))

    def whole(k):
        L, r, c = wts[k].shape
        return bufs[k].reshape(L, N_CHIPS, r, c) if BIG_KIND[k] == "col" else bufs[k].reshape(L, N_CHIPS * r, c)

    full = {k: whole(k) for k in early}
    small_sharded = [k for k in WEIGHTS if k not in BIG and SMALL_AXIS[k] is not None]
    gathered = _allgather_small([wts[k] for k in small_sharded], name="gather_small")
    sm = {k: wts[k] for k in WEIGHTS if k not in BIG and SMALL_AXIS[k] is None}
    for k, g4 in zip(small_sharded, gathered):
        sm[k] = jnp.concatenate([g4[t] for t in range(N_CHIPS)], axis=SMALL_AXIS[k])
    pool_w_bf = sm["od_pool_w"][0].astype(BF16)

    x0 = x[0]
    x0b = _cast_bf16(x, name="cast_x")[0]
    h0 = _matmul(x0b, full["ev_w_in"], mode="nn", b_lead=0, b_split=True, out_dtype=BF16, name="ev_in",
                 tm=1024, tn=1280)
    o_a, tot, *rest = _attn_fwd(h0, n_heads, name="attn_fwd", gather=[bufs[k] for k in late])
    bufs.update(zip(late, rest))
    full.update({k: whole(k) for k in late})
    u1, u3 = _evenconv_fwd(h0, sm["ev_dw_w"][0], sm["ev_dw_b"], sm["ev_bn_g"], sm["ev_bn_b"], name="evconv_fwd")
    mix0 = jnp.concatenate([o_a, u3], axis=1)
    y1 = _matmul(mix0, full["ev_w_out"], mode="nn", b_lead=0, out_dtype=F32, name="ev_out", tm=1024, tn=1024)
    x1, x1b, xh1, rs1 = _ln_fwd(x0, y1, sm["ln_g"][0, 0][None], sm["ln_b"][0, 0][None], alpha, name="ln00")
    hu0, z0, y2 = _ffn_fwd(x1b, full["ffn_w_up"], full["ffn_w_down"], 0, sm["ffn_conv_w"][0],
                           sm["ffn_conv_b"][0][None], "ffn0")
    x2, x2b, xh2, rs2 = _ln_fwd(x1, y2, sm["ln_g"][0, 1][None], sm["ln_b"][0, 1][None], alpha, name="ln01")
    h1 = _matmul(x2b, full["od_w_in"], mode="nn", b_lead=0, b_split=True, out_dtype=BF16, name="od_in",
                 tm=1024, tn=1024)
    mix1 = _odd_fwd(h1, sm["od_conv_w"][0], pool_w_bf, sm["od_pool_scale"], name="odd_fwd")
    y3 = _matmul(mix1, full["od_w_out"], mode="nn", b_lead=0, out_dtype=F32, name="od_out", tm=1024, tn=1024)
    x3, x3b, xh3, rs3 = _ln_fwd(x2, y3, sm["ln_g"][1, 0][None], sm["ln_b"][1, 0][None], alpha, name="ln10")
    hu1, z1, y4 = _ffn_fwd(x3b, full["ffn_w_up"], full["ffn_w_down"], 1, sm["ffn_conv_w"][1],
                           sm["ffn_conv_b"][1][None], "ffn1")
    x4, _, xh4, rs4 = _ln_fwd(x3, y4, sm["ln_g"][1, 1][None], sm["ln_b"][1, 1][None], alpha, name="ln11")

    dx4, loss_part = _loss_grad(x4, loss_target[0], name="loss")
    loss = lax.psum(loss_part[0, 0], ("x", "y", "c"))

    def pair_reduce(named, tag):
        g4 = []
        for k, g in named:
            rows, cols = (g.shape[1], g.shape[2]) if BIG_KIND[k] == "col" else (g.shape[0] // N_CHIPS, g.shape[1])
            g4.append(g.reshape(N_CHIPS, 2, rows // 2, cols))
        sib = _pair_exchange(g4, name=f"grad_pair_exchange_{tag}")
        return [_pair_sum(pos, g, r, name=f"grad_pair_sum_{tag}{t}") for t, (g, r) in enumerate(zip(g4, sib))]

    dr4, dr4b, dg11, db11 = _ln_bwd(dx4, xh4, rs4, sm["ln_g"][1, 1][None], name="ln11_bwd")
    dx3, g_up1, g_down1, dcw1, dcb1, _ = _ffn_bwd(dr4b, dr4, alpha, x3b, hu1, z1, full["ffn_w_up"],
                                                  full["ffn_w_down"], 1, sm["ffn_conv_w"][1],
                                                  sm["ffn_conv_b"][1][None], "ffn1")
    parts_f1 = pair_reduce([("ffn_w_up", g_up1), ("ffn_w_down", g_down1)], "f1")
    dr3, dr3b, dg10, db10 = _ln_bwd(dx3, xh3, rs3, sm["ln_g"][1, 0][None], name="ln10_bwd")
    g_odout = _matmul(mix1, dr3b, mode="tn", out_dtype=BF16, name="od_dwout", tm=512, tn=1024, tk=1024)
    dmix1 = _matmul(dr3b, full["od_w_out"], mode="nt", b_lead=0, out_dtype=BF16, name="od_dmix", tm=1024, tn=1024)
    dh1, d_odconv, d_pool, d_pscale = _odd_bwd(dmix1, h1, sm["od_conv_w"][0], pool_w_bf, sm["od_pool_scale"],
                                               name="odd_bwd")
    g_odin = _matmul(x2b, dh1, mode="tn", out_split=True, out_dtype=BF16, name="od_dwin", tm=512, tn=1024, tk=1024)
    dx2 = _matmul(dh1, full["od_w_in"], mode="nt", b_lead=0, b_split=True, out_dtype=F32, add=dr3, add_scale=alpha,
                  name="od_dx", tm=512, tn=1024, tk=1024)
    dr2, dr2b, dg01, db01 = _ln_bwd(dx2, xh2, rs2, sm["ln_g"][0, 1][None], name="ln01_bwd")
    dx1, g_up0, g_down0, dcw0, dcb0, land_f1 = _ffn_bwd(dr2b, dr2, alpha, x1b, hu0, z0, full["ffn_w_up"],
                                                        full["ffn_w_down"], 0, sm["ffn_conv_w"][0],
                                                        sm["ffn_conv_b"][0][None], "ffn0", exchange=parts_f1)
    parts_b = pair_reduce([("od_w_in", g_odin), ("od_w_out", g_odout), ("ffn_w_up", g_up0),
                           ("ffn_w_down", g_down0)], "b")
    dr1, dr1b, dg00, db00 = _ln_bwd(dx1, xh1, rs1, sm["ln_g"][0, 0][None], name="ln00_bwd")
    g_evout = _matmul(mix0, dr1b, mode="tn", out_dtype=BF16, name="ev_dwout", tm=512, tn=1024, tk=1024)
    dmix0 = _matmul(dr1b, full["ev_w_out"], mode="nt", b_lead=0, out_dtype=BF16, name="ev_dmix", tm=1024, tn=1024)
    dq, dk, dv, *land_b = _attn_bwd(h0, dmix0, tot, n_heads, name="attn_bwd", exchange=parts_b)
    da, dgate, d_dww, d_dwb, d_bng, d_bnb = _evenconv_bwd(dmix0, u1, h0, sm["ev_dw_w"][0], sm["ev_bn_g"],
                                                          sm["ev_bn_b"], name="evconv_bwd")
    dh0 = jnp.concatenate([dq, dk, dv, da, dgate], axis=1)
    g_evin = _matmul(x0b, dh0, mode="tn", out_split=True, out_dtype=BF16, name="ev_dwin", tm=512, tn=1280, tk=1024)
    grad_x = _matmul(dh0, full["ev_w_in"], mode="nt", b_lead=0, b_split=True, out_dtype=F32, add=dr1, add_scale=alpha,
                     name="ev_dx", tm=512, tn=1024, tk=1280)

    parts_e = pair_reduce([("ev_w_in", g_evin), ("ev_w_out", g_evout)], "e")
    land_e = _chip_exchange(parts_e, name="grad_chip_exchange_e")
    order = ["ffn_w_up1", "ffn_w_down1", "od_w_in0", "od_w_out0", "ffn_w_up0", "ffn_w_down0", "ev_w_in0", "ev_w_out0"]
    parts = parts_f1 + parts_b + parts_e
    land = list(land_f1) + list(land_b) + list(land_e)
    halves = [_chip_sum(pos, p, ld, name=f"grad_chip_sum_{tag}") for tag, p, ld in zip(order, parts, land)]
    reduced = dict(zip(order, _half_swap(halves, name="grad_half_swap")))
    big_grads = {}
    for k in BIG:
        L = wts[k].shape[0]
        per_layer = [reduced[f"{k}{l}"].reshape(wts[k].shape[1:]) for l in range(L)]
        big_grads[k] = per_layer[0][None] if L == 1 else jnp.stack(per_layer)

    d_ln_g = jnp.stack([jnp.stack([dg00[0], dg01[0]]), jnp.stack([dg10[0], dg11[0]])])
    d_ln_b = jnp.stack([jnp.stack([db00[0], db01[0]]), jnp.stack([db10[0], db11[0]])])
    small_partial = {
        "ev_dw_w": d_dww[None], "ev_dw_b": d_dwb, "ev_bn_g": d_bng, "ev_bn_b": d_bnb,
        "od_conv_w": d_odconv[None], "od_pool_w": d_pool[None], "od_pool_scale": d_pscale,
        "ffn_conv_w": jnp.stack([dcw0, dcw1]), "ffn_conv_b": jnp.concatenate([dcb0, dcb1], axis=0),
        "ln_g": d_ln_g, "ln_b": d_ln_b}
    small_names = [k for k in WEIGHTS if k not in BIG]
    packed = _pack([small_partial[k] for k in small_names])
    summed = _sum_slots(_gather_all_devices(packed, name="gather_small_grads"), name="sum_small_grads")
    small_full = dict(zip(small_names, _unpack(summed, [small_partial[k].shape for k in small_names])))
    small_grads = {}
    for k in small_names:
        ax = SMALL_AXIS[k]
        if ax is None:
            small_grads[k] = small_full[k]
        else:
            size = wts[k].shape[ax]
            small_grads[k] = lax.dynamic_slice_in_dim(small_full[k], chip * size, size, axis=ax)

    grads, delta, new_m, new_v = {}, {}, {}, {}
    for k in BIG:
        grads[k], delta[k], new_m[k], new_v[k] = _adamw(wts[k], big_grads[k], mom[k], var[k], name=f"adamw_{k}")
    shapes = [wts[k].shape for k in small_names]
    pw, pg, pm, pv = (_pack([d[k] for k in small_names]) for d in (wts, small_grads, mom, var))
    sg, sd, smn, svn = _adamw(pw[None], pg[None], pm[None], pv[None], name="adamw_small")
    for dst, buf in ((grads, sg), (delta, sd), (new_m, smn), (new_v, svn)):
        for k, a in zip(small_names, _unpack(buf[0], shapes)):
            dst[k] = a

    return (loss, grad_x[None], *[grads[k] for k in WEIGHTS], *[delta[k] for k in WEIGHTS],
            *[new_m[k] for k in WEIGHTS], *[new_v[k] for k in WEIGHTS])
```

```python
import functools

import jax
import jax.numpy as jnp
from jax import lax
from jax.experimental import pallas as pl
from jax.experimental.pallas import tpu as pltpu

F32 = jnp.float32
BF16 = jnp.bfloat16

HEAD_DIM = 128
POOL_WINDOWS = (2, 4, 8, 16)
LN_EPS = 1e-5
ADAM_LR = 0.001
ADAM_B1 = 0.9
ADAM_B2 = 0.999
ADAM_EPS = 1e-08
ADAM_WD = 0.01
ADAM_STEP = 10
N_CHIPS = 4
N_DEV = 8
MESH = pl.DeviceIdType.MESH
LANES = 128
HALO3 = 16
HALO31 = 32

ANY = pl.BlockSpec(memory_space=pl.ANY)


def _pick(n, pref, mult=LANES):
    if n <= pref:
        return n
    t = (pref // mult) * mult
    while t >= mult:
        if n % t == 0:
            return t
        t -= mult
    return n


def _params(*sem):
    return pltpu.CompilerParams(dimension_semantics=sem)


def _matmul(a, b, *, mode, out_dtype, name, b_lead=None, b_split=False, out_split=False, add=None,
            add_scale=1.0, tm=512, tn=1024, tk=None, exchange=None):
    ash, bsh = a.shape[-2:], b.shape[-2:]
    if mode == "nn":
        (M, K), (K2, N) = ash, bsh
        if b_split:
            N = N * N_CHIPS
    elif mode == "nt":
        (M, K), (N, K2) = ash, bsh
        if b_split:
            K2 = K2 * N_CHIPS
    else:
        (K, M), (K2, N) = ash, bsh
    assert K == K2, (ash, bsh, mode)
    tm = _pick(M, tm)
    tn = _pick(N // N_CHIPS if (out_split or (b_split and mode == "nn")) else N, tn)
    whole_split_k = b_split and mode == "nt" and tk is None
    if tk is None:
        tk = K
    else:
        tk = _pick(K // N_CHIPS if (b_split and mode == "nt") else K, tk)
    nk = K // tk
    kq = K // N_CHIPS
    n_per = (N // N_CHIPS) // tn
    k_per = (K // N_CHIPS) // tk

    def lead(shape, idx):
        if b_lead is None:
            return pl.BlockSpec(shape, idx)
        return pl.BlockSpec((None,) + shape, lambda i, j, k: (b_lead,) + idx(i, j, k))

    if mode == "nn":
        a_spec = pl.BlockSpec((tm, tk), lambda i, j, k: (i, k))
        if b_split:
            b_spec = lead((None, tk, tn), lambda i, j, k: (lax.div(j, n_per), k, lax.rem(j, n_per)))
        else:
            b_spec = lead((tk, tn), lambda i, j, k: (k, j))
        dims = (((1,), (0,)), ((), ()))
    elif mode == "nt":
        a_spec = pl.BlockSpec((tm, tk), lambda i, j, k: (i, k))
        if whole_split_k:
            b_spec = lead((N_CHIPS, tn, kq), lambda i, j, k: (0, j, 0))
        elif b_split:
            b_spec = lead((None, tn, tk), lambda i, j, k: (lax.div(k, k_per), j, lax.rem(k, k_per)))
        else:
            b_spec = lead((tn, tk), lambda i, j, k: (j, k))
        dims = (((1,), (1,)), ((), ()))
    else:
        a_spec = pl.BlockSpec((tk, tm), lambda i, j, k: (k, i))
        b_spec = pl.BlockSpec((tk, tn), lambda i, j, k: (k, j))
        dims = (((0,), (0,)), ((), ()))
    if out_split:
        out_shape = jax.ShapeDtypeStruct((N_CHIPS, M, N // N_CHIPS), out_dtype)
        out_spec = pl.BlockSpec((None, tm, tn), lambda i, j, k: (lax.div(j, n_per), i, lax.rem(j, n_per)))
    else:
        out_shape = jax.ShapeDtypeStruct((M, N), out_dtype)
        out_spec = pl.BlockSpec((tm, tn), lambda i, j, k: (i, j))
    in_specs = [a_spec, b_spec]
    args = [a, b]
    if add is not None:
        in_specs.append(pl.BlockSpec((tm, tn), lambda i, j, k: (i, j)))
        args.append(add)

    n_in = len(args)
    n_ex = 0 if exchange is None else len(exchange)
    grid = (M // tm, N // tn, nk)

    def body(*refs):
        a_ref, b_ref = refs[:2]
        add_ref = refs[2] if add is not None else None
        parts = refs[n_in:n_in + n_ex]
        o_ref = refs[n_in + n_ex]
        land = refs[n_in + n_ex + 1:n_in + 2 * n_ex + 1]
        scr = refs[n_in + 2 * n_ex + 1:]
        i, j, k = pl.program_id(0), pl.program_id(1), pl.program_id(2)
        if n_ex:
            ssem, rsem = scr[-2:]

            @pl.when((i == 0) & (j == 0) & (k == 0))
            def _():
                _exchange_start(parts, land, ssem, rsem)

        if whole_split_k:
            part = lax.dot_general(a_ref[:, 0:kq], b_ref[0], dims, preferred_element_type=F32)
            for s in range(1, N_CHIPS):
                part = part + lax.dot_general(a_ref[:, s * kq:(s + 1) * kq], b_ref[s], dims,
                                              preferred_element_type=F32)
        else:
            part = lax.dot_general(a_ref[...], b_ref[...], dims, preferred_element_type=F32)

        def finish(res):
            if add_ref is not None:
                res = res + add_scale * add_ref[...]
            o_ref[...] = res.astype(out_dtype)

        if nk == 1:
            finish(part)
        else:
            acc = scr[0]

            @pl.when(k == 0)
            def _():
                acc[...] = part

            @pl.when(k > 0)
            def _():
                acc[...] += part

            @pl.when(k == nk - 1)
            def _():
                finish(acc[...])

        if n_ex:
            @pl.when((i == grid[0] - 1) & (j == grid[1] - 1) & (k == nk - 1))
            def _():
                _exchange_finish(parts, land, ssem, rsem)

    scratch = [pltpu.VMEM((tm, tn), F32)] if nk > 1 else []
    if n_ex:
        scratch += [pltpu.SemaphoreType.DMA((3 * n_ex,)), pltpu.SemaphoreType.DMA((3 * n_ex,))]
        res = pl.pallas_call(
            body, name=name,
            out_shape=(out_shape,) + tuple(_land_shape(p) for p in exchange),
            grid=grid,
            in_specs=in_specs + [ANY] * n_ex,
            out_specs=(out_spec,) + (ANY,) * n_ex,
            scratch_shapes=scratch,
            compiler_params=_params("arbitrary", "arbitrary", "arbitrary"),
        )(*args, *exchange)
        return res[0], list(res[1:])
    return pl.pallas_call(
        body, name=name,
        out_shape=out_shape,
        grid=grid,
        in_specs=in_specs,
        out_specs=out_spec,
        scratch_shapes=scratch,
        compiler_params=_params("parallel", "parallel", "arbitrary"),
    )(*args)


def _cast_bf16(w, name):
    L, R, C = w.shape
    tr, tc = _pick(R, 512, 16), _pick(C, 1408)

    def body(w_ref, o_ref):
        o_ref[...] = w_ref[...].astype(BF16)

    return pl.pallas_call(
        body, name=name, out_shape=jax.ShapeDtypeStruct(w.shape, BF16),
        grid=(L, R // tr, C // tc),
        in_specs=[pl.BlockSpec((None, tr, tc), lambda l, i, j: (l, i, j))],
        out_specs=pl.BlockSpec((None, tr, tc), lambda l, i, j: (l, i, j)),
        compiler_params=_params("parallel", "parallel", "parallel"),
    )(w)


def _cast_into_gather(pos, w, name):
    L, R, C = w.shape
    r2 = R // 2
    tr, tc = _pick(r2, 512, 16), _pick(C, 1408)

    def body(p_ref, w_ref, o_ref):
        o_ref[...] = w_ref[...].astype(BF16)

    return pl.pallas_call(
        body, name=name, out_shape=jax.ShapeDtypeStruct((L, N_CHIPS, 2, r2, C), BF16),
        grid_spec=pltpu.PrefetchScalarGridSpec(
            num_scalar_prefetch=1, grid=(L, 2, r2 // tr, C // tc),
            in_specs=[pl.BlockSpec((None, None, tr, tc), lambda l, h, i, j, p: (l, h, i, j))],
            out_specs=pl.BlockSpec((None, None, None, tr, tc), lambda l, h, i, j, p: (l, p[0], h, i, j))),
        compiler_params=_params("parallel", "parallel", "parallel", "parallel"),
    )(pos, w.reshape(L, 2, r2, C))


def _sigmoid(v):
    return 1.0 / (1.0 + jnp.exp(-v))


def _ln_fwd(x, y, g, b, alpha, name):
    S, D = x.shape
    tr = _pick(S, 256, 8)

    def body(x_ref, y_ref, g_ref, b_ref, o_ref, ob_ref, xh_ref, rs_ref):
        r = alpha * x_ref[...] + y_ref[...]
        mu = jnp.mean(r, axis=-1, keepdims=True)
        d = r - mu
        var = jnp.mean(d * d, axis=-1, keepdims=True)
        rstd = lax.rsqrt(var + LN_EPS)
        xh = d * rstd
        o = xh * g_ref[...] + b_ref[...]
        o_ref[...] = o
        ob_ref[...] = o.astype(BF16)
        xh_ref[...] = xh
        rs_ref[...] = rstd

    row = pl.BlockSpec((tr, D), lambda i: (i, 0))
    vec = pl.BlockSpec((1, D), lambda i: (0, 0))
    return pl.pallas_call(
        body, name=name,
        out_shape=(jax.ShapeDtypeStruct((S, D), F32), jax.ShapeDtypeStruct((S, D), BF16),
                   jax.ShapeDtypeStruct((S, D), F32), jax.ShapeDtypeStruct((S, 1), F32)),
        grid=(S // tr,),
        in_specs=[row, row, vec, vec],
        out_specs=(row, row, row, pl.BlockSpec((tr, 1), lambda i: (i, 0))),
        compiler_params=_params("parallel"),
    )(x, y, g, b)


def _ln_bwd(dout, xhat, rstd, g, name):
    S, D = dout.shape
    tr = _pick(S, 256, 8)

    def body(do_ref, xh_ref, rs_ref, g_ref, dr_ref, drb_ref, dg_ref, db_ref):
        i = pl.program_id(0)
        do = do_ref[...]
        xh = xh_ref[...]
        dxh = do * g_ref[...]
        m1 = jnp.mean(dxh, axis=-1, keepdims=True)
        m2 = jnp.mean(dxh * xh, axis=-1, keepdims=True)
        dr = rs_ref[...] * (dxh - m1 - xh * m2)
        dr_ref[...] = dr
        drb_ref[...] = dr.astype(BF16)
        pg = jnp.sum(do * xh, axis=0, keepdims=True)
        pb = jnp.sum(do, axis=0, keepdims=True)

        @pl.when(i == 0)
        def _():
            dg_ref[...] = pg
            db_ref[...] = pb

        @pl.when(i > 0)
        def _():
            dg_ref[...] += pg
            db_ref[...] += pb

    row = pl.BlockSpec((tr, D), lambda i: (i, 0))
    vec = pl.BlockSpec((1, D), lambda i: (0, 0))
    return pl.pallas_call(
        body, name=name,
        out_shape=(jax.ShapeDtypeStruct((S, D), F32), jax.ShapeDtypeStruct((S, D), BF16),
                   jax.ShapeDtypeStruct((1, D), F32), jax.ShapeDtypeStruct((1, D), F32)),
        grid=(S // tr,),
        in_specs=[row, row, pl.BlockSpec((tr, 1), lambda i: (i, 0)), vec],
        out_specs=(row, row, vec, vec),
        compiler_params=_params("arbitrary"),
    )(dout, xhat, rstd, g)


def _loss_grad(y, target, name):
    S, D = y.shape
    tr = _pick(S, 256, 8)
    n = S // tr

    def body(y_ref, t_ref, dy_ref, l_ref, acc):
        i = pl.program_id(0)
        d = y_ref[...] - t_ref[...]
        dy_ref[...] = d * (1.0 / D)
        p = jnp.sum(d * d, axis=0, keepdims=True)

        @pl.when(i == 0)
        def _():
            acc[...] = p

        @pl.when(i > 0)
        def _():
            acc[...] += p

        @pl.when(i == n - 1)
        def _():
            l_ref[...] = (0.5 / D) * jnp.sum(acc[...], axis=1, keepdims=True)

    row = pl.BlockSpec((tr, D), lambda i: (i, 0))
    return pl.pallas_call(
        body, name=name,
        out_shape=(jax.ShapeDtypeStruct((S, D), F32), jax.ShapeDtypeStruct((1, 1), F32)),
        grid=(n,),
        in_specs=[row, row],
        out_specs=(row, pl.BlockSpec((1, 1), lambda i: (0, 0))),
        scratch_shapes=[pltpu.VMEM((1, D), F32)],
        compiler_params=_params("arbitrary"),
    )(y, target)


def _prev_spec(tr, halo, width, col):
    return pl.BlockSpec((halo, width), lambda c, i: (jnp.maximum(i * (tr // halo) - 1, 0), col(c)))


def _next_spec(tr, halo, width, col, nrows):
    last = nrows // halo - 1
    return pl.BlockSpec((halo, width), lambda c, i: (jnp.minimum((i + 1) * (tr // halo), last), col(c)))


def _cur_spec(tr, width, col):
    return pl.BlockSpec((tr, width), lambda c, i: (i, col(c)))


def _ffn_act_fwd(hu, conv_w, conv_b, name):
    S, F2 = hu.shape
    F = F2 // 2
    tr, tc, H = _pick(S, 512, 16), _pick(F, 512), HALO3
    nc, nr = F // tc, S // tr

    def body(gp_ref, g_ref, u_ref, w_ref, b_ref, z_ref, G):
        i = pl.program_id(1)
        G[0:H, :] = jnp.where(i > 0, gp_ref[...].astype(F32), 0.0)
        G[H:H + tr, :] = g_ref[...].astype(F32)
        gc = (b_ref[...] + w_ref[pl.ds(0, 1), :] * G[pl.ds(H - 2, tr), :]
              + w_ref[pl.ds(1, 1), :] * G[pl.ds(H - 1, tr), :] + w_ref[pl.ds(2, 1), :] * G[pl.ds(H, tr), :])
        z = gc * _sigmoid(gc) * u_ref[...].astype(F32)
        z_ref[...] = z.astype(BF16)

    gcol = lambda c: c
    ucol = lambda c: c + nc
    return pl.pallas_call(
        body, name=name, out_shape=jax.ShapeDtypeStruct((S, F), BF16),
        grid=(nc, nr),
        in_specs=[_prev_spec(tr, H, tc, gcol), _cur_spec(tr, tc, gcol), _cur_spec(tr, tc, ucol),
                  pl.BlockSpec((3, tc), lambda c, i: (0, c)), pl.BlockSpec((1, tc), lambda c, i: (0, c))],
        out_specs=pl.BlockSpec((tr, tc), lambda c, i: (i, c)),
        scratch_shapes=[pltpu.VMEM((H + tr, tc), F32)],
        compiler_params=_params("parallel", "parallel"),
    )(hu, hu, hu, conv_w, conv_b)


def _ffn_act_bwd(dz, hu, conv_w, conv_b, name):
    S, F = dz.shape
    tr, tc, H = _pick(S, 512, 16), _pick(F, 512), HALO3
    nc, nr = F // tc, S // tr
    n = tr + H

    def body(dz_ref, dzn_ref, gp_ref, g_ref, gn_ref, u_ref, un_ref, w_ref, b_ref,
             dg_ref, du_ref, dw_ref, db_ref, G, DG):
        i = pl.program_id(1)
        G[0:H, :] = jnp.where(i > 0, gp_ref[...].astype(F32), 0.0)
        G[H:H + tr, :] = g_ref[...].astype(F32)
        G[H + tr:H + tr + H, :] = gn_ref[...].astype(F32)
        w0, w1, w2 = w_ref[pl.ds(0, 1), :], w_ref[pl.ds(1, 1), :], w_ref[pl.ds(2, 1), :]
        gc = b_ref[...] + w0 * G[pl.ds(H - 2, n), :] + w1 * G[pl.ds(H - 1, n), :] + w2 * G[pl.ds(H, n), :]
        sg = _sigmoid(gc)
        dzf = jnp.concatenate([dz_ref[...], dzn_ref[...]], axis=0).astype(F32)
        uf = jnp.concatenate([u_ref[...], un_ref[...]], axis=0).astype(F32)
        rows = lax.broadcasted_iota(jnp.int32, (n, 1), 0)
        dzf = jnp.where((rows < tr) | (i < nr - 1), dzf, 0.0)
        dgc = dzf * uf * (sg * (1.0 + gc * (1.0 - sg)))
        du_ref[...] = (dzf[0:tr] * (gc[0:tr] * sg[0:tr])).astype(BF16)
        DG[...] = dgc
        dg = w2 * DG[pl.ds(0, tr), :] + w1 * DG[pl.ds(1, tr), :] + w0 * DG[pl.ds(2, tr), :]
        dg_ref[...] = dg.astype(BF16)
        dcur = dgc[0:tr]
        pw = [jnp.sum(dcur * G[pl.ds(H - 2 + k, tr), :], axis=0, keepdims=True) for k in range(3)]
        pb = jnp.sum(dcur, axis=0, keepdims=True)

        @pl.when(i == 0)
        def _():
            for k in range(3):
                dw_ref[pl.ds(k, 1), :] = pw[k]
            db_ref[...] = pb

        @pl.when(i > 0)
        def _():
            for k in range(3):
                dw_ref[pl.ds(k, 1), :] += pw[k]
            db_ref[...] += pb

    gcol = lambda c: c
    ucol = lambda c: c + nc
    blk = pl.BlockSpec((tr, tc), lambda c, i: (i, c))
    return pl.pallas_call(
        body, name=name,
        out_shape=(jax.ShapeDtypeStruct((S, F), BF16), jax.ShapeDtypeStruct((S, F), BF16),
                   jax.ShapeDtypeStruct((3, F), F32), jax.ShapeDtypeStruct((1, F), F32)),
        grid=(nc, nr),
        in_specs=[_cur_spec(tr, tc, gcol), _next_spec(tr, H, tc, gcol, S),
                  _prev_spec(tr, H, tc, gcol), _cur_spec(tr, tc, gcol), _next_spec(tr, H, tc, gcol, S),
                  _cur_spec(tr, tc, ucol), _next_spec(tr, H, tc, ucol, S),
                  pl.BlockSpec((3, tc), lambda c, i: (0, c)), pl.BlockSpec((1, tc), lambda c, i: (0, c))],
        out_specs=(blk, blk, pl.BlockSpec((3, tc), lambda c, i: (0, c)), pl.BlockSpec((1, tc), lambda c, i: (0, c))),
        scratch_shapes=[pltpu.VMEM((H + tr + H, tc), F32), pltpu.VMEM((n, tc), F32)],
        compiler_params=_params("parallel", "arbitrary"),
    )(dz, dz, hu, hu, hu, hu, hu, conv_w, conv_b)


def _softplus_neg(s):
    return jnp.minimum(-s, 0.0) - jnp.log(1.0 + jnp.exp(-jnp.abs(s)))


def _hilo_dot(v, m):
    hi = v.astype(BF16)
    lo = (v - hi.astype(F32)).astype(BF16)
    return (jnp.dot(hi, m, preferred_element_type=F32) + jnp.dot(lo, m, preferred_element_type=F32))


def _attn_fwd(h, n_heads, name, gather=()):
    S = h.shape[0]
    dh = HEAD_DIM
    A = n_heads * dh
    tq = _pick(S, 256)
    nq = S // tq
    scale = 1.0 / float(dh) ** 0.5
    ng = len(gather)

    def body(*refs):
        q_ref, k_ref, v_ref = refs[:3]
        o_ref, tot_ref = refs[3 + ng:5 + ng]
        full = refs[5 + ng:5 + 2 * ng]
        hd = pl.program_id(0)
        i = pl.program_id(1)
        if ng:
            ssem, rsem = refs[5 + 2 * ng:]

            @pl.when((hd == 0) & (i == 0))
            def _():
                _gather_start(full, ssem, rsem)

            @pl.when((hd == n_heads - 1) & (i == 0))
            def _():
                _gather_forward(full, ssem, rsem)

        q = q_ref[...]
        r_io = lax.broadcasted_iota(jnp.int32, (tq, tq), 0)
        c_io = lax.broadcasted_iota(jnp.int32, (tq, tq), 1)
        later = (r_io > c_io).astype(BF16)
        ext = jnp.concatenate([later, jnp.ones((tq, LANES), BF16)], axis=1)
        causal = c_io < r_io

        def scores(j, diag):
            kj = k_ref[pl.ds(pl.multiple_of(j * tq, tq), tq), :]
            s = lax.dot_general(q, kj, (((1,), (1,)), ((), ())), preferred_element_type=F32) * scale
            ls = _softplus_neg(s)
            if diag:
                ls = jnp.where(causal, ls, 0.0)
            cs = _hilo_dot(ls, ext)
            base = s + ls + cs[:, :tq]
            if diag:
                base = jnp.where(causal, base, -1e30)
            return base, cs[:, tq:]

        def weigh(j, acc, run, base):
            vj = v_ref[pl.ds(pl.multiple_of(j * tq, tq), tq), :]
            w = jnp.exp(base + jnp.tile(run, (1, tq // LANES)))
            return acc + jnp.dot(w.astype(BF16), vj, preferred_element_type=F32)

        def trip(t, carry):
            acc, run, base, tot = carry
            j = i - 1 - t
            base_n, tot_n = scores(j, False)
            acc = weigh(j + 1, acc, run, base)
            return acc, run + tot, base_n, tot_n

        base, tot = scores(i, True)
        carry = (jnp.zeros((tq, dh), F32), jnp.zeros((tq, LANES), F32), base, tot)
        acc, run, base, tot = lax.fori_loop(0, i, trip, carry)
        acc = weigh(0, acc, run, base)
        o_ref[...] = acc.astype(BF16)
        tot_ref[...] = run + tot
        if ng:
            @pl.when((hd == n_heads - 1) & (i == nq - 1))
            def _():
                _gather_finish(full, ssem, rsem)

    T = _gather_items(gather) if ng else 0
    return pl.pallas_call(
        body, name=name,
        out_shape=(jax.ShapeDtypeStruct((S, A), BF16), jax.ShapeDtypeStruct((n_heads, S, LANES), F32))
        + tuple(jax.ShapeDtypeStruct(b.shape, b.dtype) for b in gather),
        grid=(n_heads, nq),
        in_specs=[pl.BlockSpec((tq, dh), lambda hd, i: (i, hd)),
                  pl.BlockSpec((S, dh), lambda hd, i: (0, n_heads + hd)),
                  pl.BlockSpec((S, dh), lambda hd, i: (0, 2 * n_heads + hd))] + [ANY] * ng,
        out_specs=(pl.BlockSpec((tq, dh), lambda hd, i: (i, hd)),
                   pl.BlockSpec((None, tq, LANES), lambda hd, i: (hd, i, 0))) + (ANY,) * ng,
        input_output_aliases={3 + a: 2 + a for a in range(ng)},
        scratch_shapes=[pltpu.SemaphoreType.DMA((6 * T,)), pltpu.SemaphoreType.DMA((6 * T,))] if ng else [],
        compiler_params=_params("arbitrary", "arbitrary") if ng else _params("parallel", "parallel"),
    )(h, h, h, *gather)


def _attn_bwd(h, do, tot, n_heads, name, exchange=()):
    S = h.shape[0]
    dh = HEAD_DIM
    A = n_heads * dh
    tq = _pick(S, 256)
    nq = S // tq
    scale = 1.0 / float(dh) ** 0.5
    tn_dims = (((0,), (0,)), ((), ()))
    ne = len(exchange)

    def body(*refs):
        q_ref, k_ref, v_ref, do_ref, tot_ref = refs[:5]
        parts = refs[5:5 + ne]
        dq_ref, dk_ref, dv_ref = refs[5 + ne:8 + ne]
        land = refs[8 + ne:8 + 2 * ne]
        dk_acc, dv_acc = refs[8 + 2 * ne:10 + 2 * ne]
        hd = pl.program_id(0)
        i = pl.program_id(1)
        if ne:
            ssem, rsem = refs[10 + 2 * ne:]

            @pl.when((hd == 0) & (i == 0))
            def _():
                _exchange_start(parts, land, ssem, rsem)

        @pl.when(i == 0)
        def _():
            dk_acc[...] = jnp.zeros_like(dk_acc)
            dv_acc[...] = jnp.zeros_like(dv_acc)

        q = q_ref[...]
        dob = do_ref[...]
        total = jnp.tile(tot_ref[...], (1, tq // LANES))
        r_io = lax.broadcasted_iota(jnp.int32, (tq, tq), 0)
        c_io = lax.broadcasted_iota(jnp.int32, (tq, tq), 1)
        ones = jnp.ones((tq, LANES), BF16)
        upto = jnp.concatenate([(r_io <= c_io).astype(BF16), ones], axis=1)
        before = jnp.concatenate([(r_io < c_io).astype(BF16), ones], axis=1)
        causal = c_io < r_io

        def scores(j):
            start = pl.multiple_of(j * tq, tq)
            kj = k_ref[pl.ds(start, tq), :]
            vj = v_ref[pl.ds(start, tq), :]
            keep = jnp.logical_or(causal, j != i)
            s = lax.dot_general(q, kj, (((1,), (1,)), ((), ())), preferred_element_type=F32) * scale
            ls = jnp.where(keep, _softplus_neg(s), 0.0)
            cs = _hilo_dot(ls, upto)
            base = jnp.where(keep, s + ls - cs[:, :tq], -1e30)
            dw = lax.dot_general(dob, vj, (((1,), (1,)), ((), ())), preferred_element_type=F32)
            return base, jnp.exp(ls), dw, cs[:, tq:]

        def grads(j, dq, prun, erun, base, sn, dw):
            start = pl.multiple_of(j * tq, tq)
            kj = k_ref[pl.ds(start, tq), :]
            w = jnp.exp(base + (total - jnp.tile(prun, (1, tq // LANES))))
            e = dw * w
            ce = jnp.dot(e.astype(BF16), before, preferred_element_type=F32)
            ecum = ce[:, :tq] + jnp.tile(erun, (1, tq // LANES))
            dz = e * sn - (1.0 - sn) * ecum
            ds = (dz * scale).astype(BF16)
            dq = dq + jnp.dot(ds, kj, preferred_element_type=F32)
            dk_acc[pl.ds(start, tq), :] += lax.dot_general(ds, q, tn_dims, preferred_element_type=F32)
            dv_acc[pl.ds(start, tq), :] += lax.dot_general(w.astype(BF16), dob, tn_dims, preferred_element_type=F32)
            return dq, erun + ce[:, tq:]

        def trip(j, carry):
            dq, prun, erun, base, sn, dw, ptot = carry
            nxt = scores(j + 1)
            dq, erun = grads(j, dq, prun, erun, base, sn, dw)
            return (dq, prun + ptot, erun) + nxt

        zero = jnp.zeros((tq, LANES), F32)
        carry = lax.fori_loop(0, i, trip, (jnp.zeros((tq, dh), F32), zero, zero) + scores(0))
        dq, prun, erun, base, sn, dw, _ = carry
        dq, _ = grads(i, dq, prun, erun, base, sn, dw)
        dq_ref[...] = dq.astype(BF16)

        @pl.when(i == nq - 1)
        def _():
            dk_ref[...] = dk_acc[...].astype(BF16)
            dv_ref[...] = dv_acc[...].astype(BF16)

        if ne:
            @pl.when((hd == n_heads - 1) & (i == nq - 1))
            def _():
                _exchange_finish(parts, land, ssem, rsem)

    qblk = pl.BlockSpec((tq, dh), lambda hd, i: (i, hd))
    full = pl.BlockSpec((S, dh), lambda hd, i: (0, hd))
    scratch = [pltpu.VMEM((S, dh), F32), pltpu.VMEM((S, dh), F32)]
    if ne:
        scratch += [pltpu.SemaphoreType.DMA((3 * ne,)), pltpu.SemaphoreType.DMA((3 * ne,))]
    return pl.pallas_call(
        body, name=name,
        out_shape=tuple(jax.ShapeDtypeStruct((S, A), BF16) for _ in range(3)) + tuple(_land_shape(p) for p in exchange),
        grid=(n_heads, nq),
        in_specs=[qblk,
                  pl.BlockSpec((S, dh), lambda hd, i: (0, n_heads + hd)),
                  pl.BlockSpec((S, dh), lambda hd, i: (0, 2 * n_heads + hd)),
                  qblk,
                  pl.BlockSpec((None, tq, LANES), lambda hd, i: (hd, i, 0))] + [ANY] * ne,
        out_specs=(qblk, full, full) + (ANY,) * ne,
        scratch_shapes=scratch,
        compiler_params=_params("arbitrary", "arbitrary") if ne else _params("parallel", "arbitrary"),
    )(h, h, h, do, tot, *exchange)


def _evenconv_fwd(h, dw_w, dw_b, bn_g, bn_b, name):
    S = h.shape[0]
    KW, A = dw_w.shape
    H = HALO31
    tr = _pick(S, 256, H)
    first_tap = H - (KW - 1)

    def body(ap_ref, a_ref, gp_ref, g_ref, w_ref, b_ref, bg_ref, bb_ref, u1_ref, u3_ref, U):
        i = pl.program_id(1)
        glu_prev = ap_ref[...].astype(F32) * _sigmoid(gp_ref[...].astype(F32))
        U[0:H, :] = jnp.where(i > 0, glu_prev, 0.0)
        U[H:H + tr, :] = a_ref[...].astype(F32) * _sigmoid(g_ref[...].astype(F32))
        acc = b_ref[...] + w_ref[pl.ds(0, 1), :] * U[pl.ds(first_tap, tr), :]
        for k in range(1, KW):
            acc = acc + w_ref[pl.ds(k, 1), :] * U[pl.ds(first_tap + k, tr), :]
        u1_ref[...] = acc
        mu = jnp.mean(acc, axis=-1, keepdims=True)
        d = acc - mu
        var = jnp.mean(d * d, axis=-1, keepdims=True)
        u2 = d * lax.rsqrt(var + LN_EPS) * bg_ref[...] + bb_ref[...]
        u3_ref[...] = (u2 * _sigmoid(u2)).astype(BF16)

    acol = lambda c: 3
    gcol = lambda c: 4
    vec = pl.BlockSpec((1, A), lambda c, i: (0, 0))
    blk = pl.BlockSpec((tr, A), lambda c, i: (i, 0))
    return pl.pallas_call(
        body, name=name,
        out_shape=(jax.ShapeDtypeStruct((S, A), F32), jax.ShapeDtypeStruct((S, A), BF16)),
        grid=(1, S // tr),
        in_specs=[_prev_spec(tr, H, A, acol), _cur_spec(tr, A, acol),
                  _prev_spec(tr, H, A, gcol), _cur_spec(tr, A, gcol),
                  pl.BlockSpec((KW, A), lambda c, i: (0, 0)), vec, vec, vec],
        out_specs=(blk, blk),
        scratch_shapes=[pltpu.VMEM((H + tr, A), F32)],
        compiler_params=_params("parallel", "parallel"),
    )(h, h, h, h, dw_w, dw_b, bn_g, bn_b)


def _evenconv_bwd(du3, u1, h, dw_w, bn_g, bn_b, name):
    S = h.shape[0]
    KW, A = dw_w.shape
    H = HALO31
    tr = _pick(S, 256, H)
    nr = S // tr
    n = tr + H
    first_tap = H - (KW - 1)

    def body(d3_ref, d3n_ref, u1_ref, u1n_ref, ap_ref, a_ref, gp_ref, g_ref, w_ref, bg_ref, bb_ref,
             da_ref, dg_ref, dww_ref, dwb_ref, dbg_ref, dbb_ref, U0, DU):
        i = pl.program_id(1)
        u1 = jnp.concatenate([u1_ref[...], u1n_ref[...]], axis=0)
        d3 = jnp.concatenate([d3_ref[...], d3n_ref[...]], axis=0).astype(F32)
        rows = lax.broadcasted_iota(jnp.int32, (n, 1), 0)
        d3 = jnp.where((rows < tr) | (i < nr - 1), d3, 0.0)
        mu = jnp.mean(u1, axis=-1, keepdims=True)
        d = u1 - mu
        var = jnp.mean(d * d, axis=-1, keepdims=True)
        rstd = lax.rsqrt(var + LN_EPS)
        xh = d * rstd
        u2 = xh * bg_ref[...] + bb_ref[...]
        sg = _sigmoid(u2)
        du2 = d3 * (sg * (1.0 + u2 * (1.0 - sg)))
        dxh = du2 * bg_ref[...]
        m1 = jnp.mean(dxh, axis=-1, keepdims=True)
        m2 = jnp.mean(dxh * xh, axis=-1, keepdims=True)
        du1 = rstd * (dxh - m1 - xh * m2)
        DU[...] = du1
        pbg = jnp.sum(du2[0:tr] * xh[0:tr], axis=0, keepdims=True)
        pbb = jnp.sum(du2[0:tr], axis=0, keepdims=True)
        pwb = jnp.sum(du1[0:tr], axis=0, keepdims=True)

        glu_prev = ap_ref[...].astype(F32) * _sigmoid(gp_ref[...].astype(F32))
        U0[0:H, :] = jnp.where(i > 0, glu_prev, 0.0)
        a = a_ref[...].astype(F32)
        sgg = _sigmoid(g_ref[...].astype(F32))
        U0[H:H + tr, :] = a * sgg

        @pl.when(i == 0)
        def _():
            dbg_ref[...] = pbg
            dbb_ref[...] = pbb
            dwb_ref[...] = pwb
            dww_ref[...] = jnp.zeros_like(dww_ref)

        @pl.when(i > 0)
        def _():
            dbg_ref[...] += pbg
            dbb_ref[...] += pbb
            dwb_ref[...] += pwb

        du0 = w_ref[pl.ds(0, 1), :] * DU[pl.ds(KW - 1, tr), :]
        for k in range(1, KW):
            du0 = du0 + w_ref[pl.ds(k, 1), :] * DU[pl.ds(KW - 1 - k, tr), :]
        da_ref[...] = (du0 * sgg).astype(BF16)
        dg_ref[...] = (du0 * a * sgg * (1.0 - sgg)).astype(BF16)
        dcur = DU[pl.ds(0, tr), :]
        for k in range(KW):
            dww_ref[pl.ds(k, 1), :] += jnp.sum(dcur * U0[pl.ds(first_tap + k, tr), :], axis=0, keepdims=True)

    acol = lambda c: 3
    gcol = lambda c: 4
    one = lambda c: 1
    zero = lambda c: 0
    vec = pl.BlockSpec((1, A), lambda c, i: (0, 0))
    blk = pl.BlockSpec((tr, A), lambda c, i: (i, 0))
    return pl.pallas_call(
        body, name=name,
        out_shape=(jax.ShapeDtypeStruct((S, A), BF16), jax.ShapeDtypeStruct((S, A), BF16),
                   jax.ShapeDtypeStruct((KW, A), F32), jax.ShapeDtypeStruct((1, A), F32),
                   jax.ShapeDtypeStruct((1, A), F32), jax.ShapeDtypeStruct((1, A), F32)),
        grid=(1, nr),
        in_specs=[_cur_spec(tr, A, one), _next_spec(tr, H, A, one, S),
                  _cur_spec(tr, A, zero), _next_spec(tr, H, A, zero, S),
                  _prev_spec(tr, H, A, acol), _cur_spec(tr, A, acol),
                  _prev_spec(tr, H, A, gcol), _cur_spec(tr, A, gcol),
                  pl.BlockSpec((KW, A), lambda c, i: (0, 0)), vec, vec],
        out_specs=(blk, blk, pl.BlockSpec((KW, A), lambda c, i: (0, 0)), vec, vec, vec),
        scratch_shapes=[pltpu.VMEM((H + tr, A), F32), pltpu.VMEM((n, A), F32)],
        compiler_params=_params("arbitrary", "arbitrary"),
    )(du3, du3, u1, u1, h, h, h, h, dw_w, bn_g, bn_b)


def _pool_inv_count(row0, nrows, window):
    t = row0 + lax.broadcasted_iota(jnp.int32, (nrows, 1), 0)
    return 1.0 / jnp.minimum(t + 1, window).astype(F32)


def _odd_fwd(h, conv_w, pool_w, pool_scale, name):
    S = h.shape[0]
    C = conv_w.shape[1]
    G = len(POOL_WINDOWS)
    Dg = C // G
    H = HALO3
    tr = _pick(S, 256, H)

    def body(cb_ref, ccp_ref, cc_ref, chp_ref, ch_ref, pp_ref, p_ref, w_ref, pw_ref, sc_ref, mix_ref, M, P):
        i = pl.program_id(1)
        M[0:H, :] = jnp.where(i > 0, ccp_ref[...].astype(F32) * chp_ref[...].astype(F32), 0.0)
        M[H:H + tr, :] = cc_ref[...].astype(F32) * ch_ref[...].astype(F32)
        cm = (w_ref[pl.ds(0, 1), :] * M[pl.ds(H - 2, tr), :] + w_ref[pl.ds(1, 1), :] * M[pl.ds(H - 1, tr), :]
              + w_ref[pl.ds(2, 1), :] * M[pl.ds(H, tr), :])
        mix_ref[:, 0:C] = (cb_ref[...].astype(F32) * cm).astype(BF16)
        P[0:H, :] = jnp.where(i > 0, pp_ref[...].astype(F32), 0.0)
        P[H:H + tr, :] = p_ref[...].astype(F32)
        for gi, window in enumerate(POOL_WINDOWS):
            cols = pl.ds(gi * Dg, Dg)
            wsum = P[pl.ds(H, tr), cols]
            for dlt in range(1, window):
                wsum = wsum + P[pl.ds(H - dlt, tr), cols]
            diff = wsum * _pool_inv_count(i * tr, tr, window) - P[pl.ds(H, tr), cols]
            yd = jnp.dot(diff.astype(BF16), pw_ref[gi], preferred_element_type=F32) * sc_ref[:, cols]
            mix_ref[:, pl.ds(C + gi * Dg, Dg)] = yd.astype(BF16)

    col = lambda k: (lambda c: k)
    return pl.pallas_call(
        body, name=name, out_shape=jax.ShapeDtypeStruct((S, 2 * C), BF16),
        grid=(1, S // tr),
        in_specs=[_cur_spec(tr, C, col(0)),
                  _prev_spec(tr, H, C, col(1)), _cur_spec(tr, C, col(1)),
                  _prev_spec(tr, H, C, col(2)), _cur_spec(tr, C, col(2)),
                  _prev_spec(tr, H, C, col(3)), _cur_spec(tr, C, col(3)),
                  pl.BlockSpec((3, C), lambda c, i: (0, 0)),
                  pl.BlockSpec((G, Dg, Dg), lambda c, i: (0, 0, 0)),
                  pl.BlockSpec((1, C), lambda c, i: (0, 0))],
        out_specs=pl.BlockSpec((tr, 2 * C), lambda c, i: (i, 0)),
        scratch_shapes=[pltpu.VMEM((H + tr, C), F32), pltpu.VMEM((H + tr, C), F32)],
        compiler_params=_params("parallel", "parallel"),
    )(h, h, h, h, h, h, h, conv_w, pool_w, pool_scale)


def _odd_bwd(dmix, h, conv_w, pool_w, pool_scale, name):
    S = h.shape[0]
    C = conv_w.shape[1]
    G = len(POOL_WINDOWS)
    Dg = C // G
    H = HALO3
    tr = _pick(S, 256, H)
    nr = S // tr
    n = tr + H
    nt_dims = (((1,), (1,)), ((), ()))
    tn_dims = (((0,), (0,)), ((), ()))

    def body(dyc_ref, dycn_ref, dyd_ref, dydn_ref, cb_ref, cbn_ref, ccp_ref, cc_ref, ccn_ref,
             chp_ref, ch_ref, chn_ref, pp_ref, p_ref, w_ref, pw_ref, sc_ref,
             dh_ref, dw_ref, dpw_ref, dsc_ref, M, DCM, P, Q):
        i = pl.program_id(1)
        rows = lax.broadcasted_iota(jnp.int32, (n, 1), 0)
        valid = (rows < tr) | (i < nr - 1)

        @pl.when(i == 0)
        def _():
            dw_ref[...] = jnp.zeros_like(dw_ref)
            dpw_ref[...] = jnp.zeros_like(dpw_ref)
            dsc_ref[...] = jnp.zeros_like(dsc_ref)

        M[0:H, :] = jnp.where(i > 0, ccp_ref[...].astype(F32) * chp_ref[...].astype(F32), 0.0)
        cc = cc_ref[...].astype(F32)
        ch = ch_ref[...].astype(F32)
        M[H:H + tr, :] = cc * ch
        M[H + tr:H + tr + H, :] = ccn_ref[...].astype(F32) * chn_ref[...].astype(F32)
        w0, w1, w2 = w_ref[pl.ds(0, 1), :], w_ref[pl.ds(1, 1), :], w_ref[pl.ds(2, 1), :]
        cm = w0 * M[pl.ds(H - 2, tr), :] + w1 * M[pl.ds(H - 1, tr), :] + w2 * M[pl.ds(H, tr), :]
        dyc = jnp.concatenate([dyc_ref[...], dycn_ref[...]], axis=0).astype(F32)
        dyc = jnp.where(valid, dyc, 0.0)
        cbf = jnp.concatenate([cb_ref[...], cbn_ref[...]], axis=0).astype(F32)
        dh_ref[:, 0:C] = (dyc[0:tr] * cm).astype(BF16)
        DCM[...] = dyc * cbf
        dm = w2 * DCM[pl.ds(0, tr), :] + w1 * DCM[pl.ds(1, tr), :] + w0 * DCM[pl.ds(2, tr), :]
        dh_ref[:, C:2 * C] = (dm * ch).astype(BF16)
        dh_ref[:, 2 * C:3 * C] = (dm * cc).astype(BF16)
        dcur = DCM[pl.ds(0, tr), :]
        for k in range(3):
            dw_ref[pl.ds(k, 1), :] += jnp.sum(dcur * M[pl.ds(H - 2 + k, tr), :], axis=0, keepdims=True)

        P[0:H, :] = jnp.where(i > 0, pp_ref[...].astype(F32), 0.0)
        P[H:H + tr, :] = p_ref[...].astype(F32)
        dyd = jnp.concatenate([dyd_ref[...], dydn_ref[...]], axis=0).astype(F32)
        dyd = jnp.where(valid, dyd, 0.0)
        for gi, window in enumerate(POOL_WINDOWS):
            cols = pl.ds(gi * Dg, Dg)
            lo = gi * Dg
            wsum = P[pl.ds(H, tr), cols]
            for dlt in range(1, window):
                wsum = wsum + P[pl.ds(H - dlt, tr), cols]
            diff = (wsum * _pool_inv_count(i * tr, tr, window) - P[pl.ds(H, tr), cols]).astype(BF16)
            pw = pw_ref[gi]
            dyd_g = dyd[:, lo:lo + Dg]
            e = (dyd_g * sc_ref[:, cols]).astype(BF16)
            yraw = jnp.dot(diff, pw, preferred_element_type=F32)
            dsc_ref[:, cols] += jnp.sum(dyd_g[0:tr] * yraw, axis=0, keepdims=True)
            dpw_ref[gi] += lax.dot_general(diff, e[0:tr], tn_dims, preferred_element_type=F32)
            ddiff = lax.dot_general(e, pw, nt_dims, preferred_element_type=F32)
            Q[:, cols] = ddiff * _pool_inv_count(i * tr, n, window)
            acc = Q[pl.ds(0, tr), cols]
            for dlt in range(1, window):
                acc = acc + Q[pl.ds(dlt, tr), cols]
            dh_ref[:, pl.ds(3 * C + lo, Dg)] = (acc - ddiff[0:tr]).astype(BF16)

    col = lambda k: (lambda c: k)
    return pl.pallas_call(
        body, name=name,
        out_shape=(jax.ShapeDtypeStruct((S, 4 * C), BF16), jax.ShapeDtypeStruct((3, C), F32),
                   jax.ShapeDtypeStruct((G, Dg, Dg), F32), jax.ShapeDtypeStruct((1, C), F32)),
        grid=(1, nr),
        in_specs=[_cur_spec(tr, C, col(0)), _next_spec(tr, H, C, col(0), S),
                  _cur_spec(tr, C, col(1)), _next_spec(tr, H, C, col(1), S),
                  _cur_spec(tr, C, col(0)), _next_spec(tr, H, C, col(0), S),
                  _prev_spec(tr, H, C, col(1)), _cur_spec(tr, C, col(1)), _next_spec(tr, H, C, col(1), S),
                  _prev_spec(tr, H, C, col(2)), _cur_spec(tr, C, col(2)), _next_spec(tr, H, C, col(2), S),
                  _prev_spec(tr, H, C, col(3)), _cur_spec(tr, C, col(3)),
                  pl.BlockSpec((3, C), lambda c, i: (0, 0)),
                  pl.BlockSpec((G, Dg, Dg), lambda c, i: (0, 0, 0)),
                  pl.BlockSpec((1, C), lambda c, i: (0, 0))],
        out_specs=(pl.BlockSpec((tr, 4 * C), lambda c, i: (i, 0)),
                   pl.BlockSpec((3, C), lambda c, i: (0, 0)),
                   pl.BlockSpec((G, Dg, Dg), lambda c, i: (0, 0, 0)),
                   pl.BlockSpec((1, C), lambda c, i: (0, 0))),
        scratch_shapes=[pltpu.VMEM((H + tr + H, C), F32), pltpu.VMEM((n, C), F32),
                        pltpu.VMEM((H + tr, C), F32), pltpu.VMEM((n, C), F32)],
        compiler_params=_params("arbitrary", "arbitrary"),
    )(dmix, dmix, dmix, dmix, h, h, h, h, h, h, h, h, h, h, conv_w, pool_w, pool_scale)


def _adamw(w, g, m, v, name):
    L, R, C = w.shape
    tr, tc = _pick(R, 256, 8), _pick(C, 1408)
    c1 = 1.0 / (1.0 - ADAM_B1 ** ADAM_STEP)
    c2 = 1.0 / (1.0 - ADAM_B2 ** ADAM_STEP)

    def body(w_ref, g_ref, m_ref, v_ref, go_ref, d_ref, mo_ref, vo_ref):
        gg = g_ref[...]
        mn = ADAM_B1 * m_ref[...] + (1.0 - ADAM_B1) * gg
        vn = ADAM_B2 * v_ref[...] + (1.0 - ADAM_B2) * (gg * gg)
        d_ref[...] = -ADAM_LR * ((mn * c1) / (jnp.sqrt(vn * c2) + ADAM_EPS) + ADAM_WD * w_ref[...])
        go_ref[...] = gg
        mo_ref[...] = mn
        vo_ref[...] = vn

    blk = pl.BlockSpec((None, tr, tc), lambda l, i, j: (l, i, j))
    sds = jax.ShapeDtypeStruct(w.shape, F32)
    return pl.pallas_call(
        body, name=name, out_shape=(sds, sds, sds, sds),
        grid=(L, R // tr, C // tc),
        in_specs=[blk, blk, blk, blk], out_specs=(blk, blk, blk, blk),
        compiler_params=_params("parallel", "parallel", "parallel"),
    )(w, g, m, v)


def _sum_slots(buf, name):
    N, R, C = buf.shape
    tr = _pick(R, 512, 8)

    def body(b_ref, o_ref):
        acc = b_ref[0]
        for k in range(1, N):
            acc = acc + b_ref[k]
        o_ref[...] = acc

    return pl.pallas_call(
        body, name=name, out_shape=jax.ShapeDtypeStruct((R, C), F32),
        grid=(R // tr,),
        in_specs=[pl.BlockSpec((N, tr, C), lambda i: (0, i, 0))],
        out_specs=pl.BlockSpec((tr, C), lambda i: (i, 0)),
        compiler_params=_params("parallel"),
    )(buf)


def _pair_sum(pos, g, rsib, name):
    _, hr, hc = rsib.shape
    tr, tc = _pick(hr, 512, 16), _pick(hc, 1024)

    def body(p_ref, g_ref, r_ref, o_ref):
        o_ref[...] = (g_ref[...].astype(F32) + r_ref[...].astype(F32)).astype(BF16)

    blk = pl.BlockSpec((None, tr, tc), lambda s, i, j, p: (s, i, j))
    return pl.pallas_call(
        body, name=name, out_shape=jax.ShapeDtypeStruct(rsib.shape, BF16),
        grid_spec=pltpu.PrefetchScalarGridSpec(
            num_scalar_prefetch=1, grid=(N_CHIPS, hr // tr, hc // tc),
            in_specs=[pl.BlockSpec((None, None, tr, tc), lambda s, i, j, p: (s, p[1], i, j)), blk],
            out_specs=blk),
        compiler_params=_params("parallel", "parallel", "parallel"),
    )(pos, g, rsib)


def _chip_sum(pos, part, land, name):
    _, sr, sc = land.shape
    tr, tc = _pick(sr, 512, 16), _pick(sc, 1024)

    def body(p_ref, own_ref, l_ref, o_ref):
        acc = own_ref[...].astype(F32)
        for k in range(3):
            acc = acc + l_ref[k].astype(F32)
        o_ref[...] = acc

    return pl.pallas_call(
        body, name=name, out_shape=jax.ShapeDtypeStruct((2, sr, sc), F32),
        grid_spec=pltpu.PrefetchScalarGridSpec(
            num_scalar_prefetch=1, grid=(sr // tr, sc // tc),
            in_specs=[pl.BlockSpec((None, tr, tc), lambda i, j, p: (p[0], i, j)),
                      pl.BlockSpec((3, tr, tc), lambda i, j, p: (0, i, j))],
            out_specs=pl.BlockSpec((None, tr, tc), lambda i, j, p: (p[1], i, j))),
        compiler_params=_params("parallel", "parallel"),
    )(pos, part, land)


def _place():
    x, y, c = lax.axis_index("x"), lax.axis_index("y"), lax.axis_index("c")
    return x, y, c


def _other_chips(x, y):
    return [(1 - x, y), (x, 1 - y), (1 - x, 1 - y)]


def _rcopy(src, dst, ssem, rsem, dev):
    return pltpu.make_async_remote_copy(src_ref=src, dst_ref=dst, send_sem=ssem, recv_sem=rsem,
                                        device_id=dev, device_id_type=MESH)


def _gather_items(bufs):
    return sum(b.shape[0] for b in bufs)


def _gather_walk(full):
    t = 0
    for ref in full:
        for l in range(ref.shape[0]):
            yield t, ref, l
            t += 1


def _gather_start(full, ssem, rsem):
    x, y, c = _place()
    j = 2 * x + y
    for t, ref, l in _gather_walk(full):
        own = ref.at[l, j, c]
        for r, (px, py) in enumerate(_other_chips(x, y)):
            _rcopy(own, own, ssem.at[6 * t + r], rsem.at[6 * t + r], (px, py, c)).start()


def _gather_forward(full, ssem, rsem):
    x, y, c = _place()
    for t, ref, l in _gather_walk(full):
        for r, (px, py) in enumerate(_other_chips(x, y)):
            slab = ref.at[l, 2 * px + py, c]
            _rcopy(slab, slab, ssem.at[6 * t + r], rsem.at[6 * t + r], (px, py, c)).wait_recv()
            _rcopy(slab, slab, ssem.at[6 * t + 3 + r], rsem.at[6 * t + 3 + r], (x, y, 1 - c)).start()


def _gather_finish(full, ssem, rsem):
    x, y, c = _place()
    j = 2 * x + y
    for t, ref, l in _gather_walk(full):
        for r, (px, py) in enumerate(_other_chips(x, y)):
            got = ref.at[l, 2 * px + py, 1 - c]
            _rcopy(got, got, ssem.at[6 * t + 3 + r], rsem.at[6 * t + 3 + r], (x, y, 1 - c)).wait_recv()
    for t, ref, l in _gather_walk(full):
        own = ref.at[l, j, c]
        for r, (px, py) in enumerate(_other_chips(x, y)):
            _rcopy(own, own, ssem.at[6 * t + r], rsem.at[6 * t + r], (px, py, c)).wait_send()
            slab = ref.at[l, 2 * px + py, c]
            _rcopy(slab, slab, ssem.at[6 * t + 3 + r], rsem.at[6 * t + 3 + r], (x, y, 1 - c)).wait_send()


def _land_shape(part):
    return jax.ShapeDtypeStruct((3,) + part.shape[1:], part.dtype)


def _exchange_start(parts, land, ssem, rsem):
    x, y, c = _place()
    for a in range(len(parts)):
        for r, (px, py) in enumerate(_other_chips(x, y)):
            _rcopy(parts[a].at[2 * px + py], land[a].at[r], ssem.at[3 * a + r], rsem.at[3 * a + r],
                   (px, py, c)).start()


def _exchange_finish(parts, land, ssem, rsem):
    x, y, c = _place()
    for a in range(len(parts)):
        for r, (px, py) in enumerate(_other_chips(x, y)):
            _rcopy(parts[a].at[2 * px + py], land[a].at[r], ssem.at[3 * a + r], rsem.at[3 * a + r],
                   (px, py, c)).wait()


def _allgather_big(bufs, name):
    n = len(bufs)
    T = _gather_items(bufs)

    def body(*refs):
        full = refs[n:2 * n]
        ssem, rsem = refs[2 * n:]
        _gather_start(full, ssem, rsem)
        _gather_forward(full, ssem, rsem)
        _gather_finish(full, ssem, rsem)

    return pl.pallas_call(
        body, name=name, out_shape=tuple(jax.ShapeDtypeStruct(b.shape, BF16) for b in bufs),
        in_specs=[ANY] * n, out_specs=tuple([ANY] * n),
        input_output_aliases={a: a for a in range(n)},
        scratch_shapes=[pltpu.SemaphoreType.DMA((6 * T,)), pltpu.SemaphoreType.DMA((6 * T,))],
    )(*bufs)


def _allgather_small(shards, name):
    n = len(shards)
    outs = tuple(jax.ShapeDtypeStruct((N_CHIPS,) + s.shape, s.dtype) for s in shards)

    def body(*refs):
        ins, full = refs[:n], refs[n:2 * n]
        ssem, rsem, lsem = refs[2 * n:]
        x, y, c = _place()
        j = 2 * x + y
        chips = _other_chips(x, y)
        cps, locs = [], []
        for a in range(n):
            loc = pltpu.make_async_copy(ins[a], full[a].at[j], lsem.at[a])
            loc.start()
            locs.append(loc)
            for r, (px, py) in enumerate(chips):
                cp = _rcopy(ins[a], full[a].at[j], ssem.at[3 * a + r], rsem.at[3 * a + r], (px, py, c))
                cp.start()
                cps.append(cp)
        for a in range(n):
            for r, (px, py) in enumerate(chips):
                dst = full[a].at[2 * px + py]
                _rcopy(dst, dst, ssem.at[3 * a + r], rsem.at[3 * a + r], (px, py, c)).wait_recv()
        for cp in cps:
            cp.wait_send()
        for loc in locs:
            loc.wait()

    return pl.pallas_call(
        body, name=name, out_shape=outs,
        in_specs=[ANY] * n, out_specs=tuple([ANY] * n),
        scratch_shapes=[pltpu.SemaphoreType.DMA((3 * n,)), pltpu.SemaphoreType.DMA((3 * n,)),
                        pltpu.SemaphoreType.DMA((n,))],
    )(*shards)


def _pair_exchange(grads, name):
    n = len(grads)
    outs = [jax.ShapeDtypeStruct((N_CHIPS,) + g.shape[2:], BF16) for g in grads]

    def body(*refs):
        ins, got = refs[:n], refs[n:2 * n]
        ssem, rsem = refs[2 * n:]
        x, y, c = _place()
        cps = []
        for a in range(n):
            for s in range(N_CHIPS):
                cp = _rcopy(ins[a].at[s, 1 - c], got[a].at[s], ssem.at[N_CHIPS * a + s],
                            rsem.at[N_CHIPS * a + s], (x, y, 1 - c))
                cp.start()
                cps.append(cp)
        for cp in cps:
            cp.wait()

    return pl.pallas_call(
        body, name=name, out_shape=tuple(outs),
        in_specs=[ANY] * n, out_specs=tuple([ANY] * n),
        scratch_shapes=[pltpu.SemaphoreType.DMA((N_CHIPS * n,)), pltpu.SemaphoreType.DMA((N_CHIPS * n,))],
    )(*grads)


def _chip_exchange(parts, name):
    n = len(parts)

    def body(*refs):
        ins, land = refs[:n], refs[n:2 * n]
        ssem, rsem = refs[2 * n:]
        _exchange_start(ins, land, ssem, rsem)
        _exchange_finish(ins, land, ssem, rsem)

    return pl.pallas_call(
        body, name=name, out_shape=tuple(_land_shape(p) for p in parts),
        in_specs=[ANY] * n, out_specs=tuple([ANY] * n),
        scratch_shapes=[pltpu.SemaphoreType.DMA((3 * n,)), pltpu.SemaphoreType.DMA((3 * n,))],
    )(*parts)


def _half_swap(bufs, name):
    n = len(bufs)

    def body(*refs):
        full = refs[n:2 * n]
        ssem, rsem = refs[2 * n:]
        x, y, c = _place()
        cps = []
        for t in range(n):
            mine = full[t].at[c]
            cp = _rcopy(mine, mine, ssem.at[t], rsem.at[t], (x, y, 1 - c))
            cp.start()
            cps.append(cp)
        for t in range(n):
            got = full[t].at[1 - c]
            _rcopy(got, got, ssem.at[t], rsem.at[t], (x, y, 1 - c)).wait_recv()
        for cp in cps:
            cp.wait_send()

    return pl.pallas_call(
        body, name=name, out_shape=tuple(jax.ShapeDtypeStruct(b.shape, F32) for b in bufs),
        in_specs=[ANY] * n, out_specs=tuple([ANY] * n),
        input_output_aliases={a: a for a in range(n)},
        scratch_shapes=[pltpu.SemaphoreType.DMA((n,)), pltpu.SemaphoreType.DMA((n,))],
    )(*bufs)


def _gather_all_devices(buf, name):
    R, C = buf.shape

    def body(b_ref, o_ref, ssem, rsem, lsem):
        x, y, c = _place()
        me = 4 * x + 2 * y + c
        loc = pltpu.make_async_copy(b_ref, o_ref.at[me], lsem)
        loc.start()
        cps = []
        for m in range(1, N_DEV):
            fx, fy, fc = (m >> 2) & 1, (m >> 1) & 1, m & 1
            px = x + fx - 2 * x * fx
            py = y + fy - 2 * y * fy
            pc = c + fc - 2 * c * fc
            cp = _rcopy(b_ref, o_ref.at[me], ssem.at[m - 1], rsem.at[m - 1], (px, py, pc))
            cp.start()
            cps.append((cp, 4 * px + 2 * py + pc))
        for m, (cp, peer) in enumerate(cps):
            dst = o_ref.at[peer]
            _rcopy(dst, dst, ssem.at[m], rsem.at[m], (x, y, c)).wait_recv()
        for cp, _ in cps:
            cp.wait_send()
        loc.wait()

    return pl.pallas_call(
        body, name=name, out_shape=jax.ShapeDtypeStruct((N_DEV, R, C), F32),
        in_specs=[ANY], out_specs=ANY,
        scratch_shapes=[pltpu.SemaphoreType.DMA((N_DEV - 1,)), pltpu.SemaphoreType.DMA((N_DEV - 1,)),
                        pltpu.SemaphoreType.DMA],
    )(buf)


def _pack(arrs):
    flat = jnp.concatenate([a.reshape(-1) for a in arrs])
    rows = -(-flat.shape[0] // (8 * LANES)) * 8
    flat = jnp.pad(flat, (0, rows * LANES - flat.shape[0]))
    return flat.reshape(rows, LANES)


def _unpack(buf, shapes):
    flat = buf.reshape(-1)
    out, off = [], 0
    for s in shapes:
        size = 1
        for d in s:
            size *= d
        out.append(flat[off:off + size].reshape(s))
        off += size
    return out


BIG = ("ev_w_in", "ev_w_out", "od_w_in", "od_w_out", "ffn_w_up", "ffn_w_down")
BIG_KIND = {"ev_w_in": "col", "ev_w_out": "row", "od_w_in": "col", "od_w_out": "row",
            "ffn_w_up": "col", "ffn_w_down": "row"}
SMALL_AXIS = {"ev_dw_w": 2, "ev_dw_b": None, "ev_bn_g": None, "ev_bn_b": None, "od_conv_w": 2,
              "od_pool_w": 2, "od_pool_scale": 1, "ffn_conv_w": 2, "ffn_conv_b": None, "ln_g": 2, "ln_b": 2}
WEIGHTS = ("ev_w_in", "ev_dw_w", "ev_dw_b", "ev_bn_g", "ev_bn_b", "ev_w_out", "od_w_in", "od_conv_w",
           "od_pool_w", "od_pool_scale", "od_w_out", "ffn_w_up", "ffn_conv_w", "ffn_conv_b", "ffn_w_down",
           "ln_g", "ln_b")


def _ffn_fwd(xb, w_up, w_down, l, conv_w, conv_b, tag):
    hu = _matmul(xb, w_up, mode="nn", b_lead=l, b_split=True, out_dtype=BF16, name=f"{tag}_up", tm=1024, tn=1408)
    z = _ffn_act_fwd(hu, conv_w, conv_b, name=f"{tag}_act")
    y = _matmul(z, w_down, mode="nn", b_lead=l, out_dtype=F32, name=f"{tag}_down", tm=512, tn=1024)
    return hu, z, y


def _ffn_bwd(drb, dr, alpha, xb, hu, z, w_up, w_down, l, conv_w, conv_b, tag, exchange=None):
    g_down = _matmul(z, drb, mode="tn", out_dtype=BF16, name=f"{tag}_dwdown", tm=512, tn=1024)
    dz = _matmul(drb, w_down, mode="nt", b_lead=l, out_dtype=BF16, name=f"{tag}_dz", tm=1024, tn=1408)
    dg, du, dcw, dcb = _ffn_act_bwd(dz, hu, conv_w, conv_b, name=f"{tag}_actbwd")
    dhu = jnp.concatenate([dg, du], axis=1)
    g_up = _matmul(xb, dhu, mode="tn", out_split=True, out_dtype=BF16, name=f"{tag}_dwup", tm=512, tn=1408)
    dx = _matmul(dhu, w_up, mode="nt", b_lead=l, b_split=True, out_dtype=F32, add=dr, add_scale=alpha,
                 name=f"{tag}_dx", tm=512, tn=512, exchange=exchange)
    dx, land = dx if exchange is not None else (dx, None)
    return dx, g_up, g_down, dcw, dcb, land


def kernel(x, ev_w_in, ev_dw_w, ev_dw_b, ev_bn_g, ev_bn_b, ev_w_out, od_w_in, od_conv_w, od_pool_w, od_pool_scale, od_w_out, ffn_w_up, ffn_conv_w, ffn_conv_b, ffn_w_down, ln_g, ln_b, loss_target, m_ev_w_in, m_ev_dw_w, m_ev_dw_b, m_ev_bn_g, m_ev_bn_b, m_ev_w_out, m_od_w_in, m_od_conv_w, m_od_pool_w, m_od_pool_scale, m_od_w_out, m_ffn_w_up, m_ffn_conv_w, m_ffn_conv_b, m_ffn_w_down, m_ln_g, m_ln_b, v_ev_w_in, v_ev_dw_w, v_ev_dw_b, v_ev_bn_g, v_ev_bn_b, v_ev_w_out, v_od_w_in, v_od_conv_w, v_od_pool_w, v_od_pool_scale, v_od_w_out, v_ffn_w_up, v_ffn_conv_w, v_ffn_conv_b, v_ffn_w_down, v_ln_g, v_ln_b):
    wts = dict(ev_w_in=ev_w_in, ev_dw_w=ev_dw_w, ev_dw_b=ev_dw_b, ev_bn_g=ev_bn_g, ev_bn_b=ev_bn_b,
               ev_w_out=ev_w_out, od_w_in=od_w_in, od_conv_w=od_conv_w, od_pool_w=od_pool_w,
               od_pool_scale=od_pool_scale, od_w_out=od_w_out, ffn_w_up=ffn_w_up, ffn_conv_w=ffn_conv_w,
               ffn_conv_b=ffn_conv_b, ffn_w_down=ffn_w_down, ln_g=ln_g, ln_b=ln_b)
    mom = dict(ev_w_in=m_ev_w_in, ev_dw_w=m_ev_dw_w, ev_dw_b=m_ev_dw_b, ev_bn_g=m_ev_bn_g, ev_bn_b=m_ev_bn_b,
               ev_w_out=m_ev_w_out, od_w_in=m_od_w_in, od_conv_w=m_od_conv_w, od_pool_w=m_od_pool_w,
               od_pool_scale=m_od_pool_scale, od_w_out=m_od_w_out, ffn_w_up=m_ffn_w_up, ffn_conv_w=m_ffn_conv_w,
               ffn_conv_b=m_ffn_conv_b, ffn_w_down=m_ffn_w_down, ln_g=m_ln_g, ln_b=m_ln_b)
    var = dict(ev_w_in=v_ev_w_in, ev_dw_w=v_ev_dw_w, ev_dw_b=v_ev_dw_b, ev_bn_g=v_ev_bn_g, ev_bn_b=v_ev_bn_b,
               ev_w_out=v_ev_w_out, od_w_in=v_od_w_in, od_conv_w=v_od_conv_w, od_pool_w=v_od_pool_w,
               od_pool_scale=v_od_pool_scale, od_w_out=v_od_w_out, ffn_w_up=v_ffn_w_up, ffn_conv_w=v_ffn_conv_w,
               ffn_conv_b=v_ffn_conv_b, ffn_w_down=v_ffn_w_down, ln_g=v_ln_g, ln_b=v_ln_b)

    S, D = x.shape[1], x.shape[2]
    depth = ln_g.shape[0]
    alpha = (2.0 * depth) ** 0.25
    A = ev_dw_b.shape[-1]
    n_heads = A // HEAD_DIM
    xi, yi, ci = _place()
    chip = 2 * xi + yi
    pos = jnp.stack([chip, ci]).astype(jnp.int32)

    bufs = {k: _cast_into_gather(pos, wts[k], name=f"cast_{k}") for k in BIG}
    early = ("ev_w_in", "ev_w_out")
    late = tuple(k for k in BIG if k not in early)
    bufs.update(zip(early, _allgather_big([bufs[k] for k in early], name="gather_first")))

    def whole(k):
        L, r, c = wts[k].shape
        return bufs[k].reshape(L, N_CHIPS, r, c) if BIG_KIND[k] == "col" else bufs[k].reshape(L, N_CHIPS * r, c)

    full = {k: whole(k) for k in early}
    small_sharded = [k for k in WEIGHTS if k not in BIG and SMALL_AXIS[k] is not None]
    gathered = _allgather_small([wts[k] for k in small_sharded], name="gather_small")
    sm = {k: wts[k] for k in WEIGHTS if k not in BIG and SMALL_AXIS[k] is None}
    for k, g4 in zip(small_sharded, gathered):
        sm[k] = jnp.concatenate([g4[t] for t in range(N_CHIPS)], axis=SMALL_AXIS[k])
    pool_w_bf = sm["od_pool_w"][0].astype(BF16)

    x0 = x[0]
    x0b = _cast_bf16(x, name="cast_x")[0]
    h0 = _matmul(x0b, full["ev_w_in"], mode="nn", b_lead=0, b_split=True, out_dtype=BF16, name="ev_in",
                 tm=1024, tn=1280)
    o_a, tot, *rest = _attn_fwd(h0, n_heads, name="attn_fwd", gather=[bufs[k] for k in late])
    bufs.update(zip(late, rest))
    full.update({k: whole(k) for k in late})
    u1, u3 = _evenconv_fwd(h0, sm["ev_dw_w"][0], sm["ev_dw_b"], sm["ev_bn_g"], sm["ev_bn_b"], name="evconv_fwd")
    mix0 = jnp.concatenate([o_a, u3], axis=1)
    y1 = _matmul(mix0, full["ev_w_out"], mode="nn", b_lead=0, out_dtype=F32, name="ev_out", tm=1024, tn=1024)
    x1, x1b, xh1, rs1 = _ln_fwd(x0, y1, sm["ln_g"][0, 0][None], sm["ln_b"][0, 0][None], alpha, name="ln00")
    hu0, z0, y2 = _ffn_fwd(x1b, full["ffn_w_up"], full["ffn_w_down"], 0, sm["ffn_conv_w"][0],
                           sm["ffn_conv_b"][0][None], "ffn0")
    x2, x2b, xh2, rs2 = _ln_fwd(x1, y2, sm["ln_g"][0, 1][None], sm["ln_b"][0, 1][None], alpha, name="ln01")
    h1 = _matmul(x2b, full["od_w_in"], mode="nn", b_lead=0, b_split=True, out_dtype=BF16, name="od_in",
                 tm=1024, tn=1024)
    mix1 = _odd_fwd(h1, sm["od_conv_w"][0], pool_w_bf, sm["od_pool_scale"], name="odd_fwd")
    y3 = _matmul(mix1, full["od_w_out"], mode="nn", b_lead=0, out_dtype=F32, name="od_out", tm=1024, tn=1024)
    x3, x3b, xh3, rs3 = _ln_fwd(x2, y3, sm["ln_g"][1, 0][None], sm["ln_b"][1, 0][None], alpha, name="ln10")
    hu1, z1, y4 = _ffn_fwd(x3b, full["ffn_w_up"], full["ffn_w_down"], 1, sm["ffn_conv_w"][1],
                           sm["ffn_conv_b"][1][None], "ffn1")
    x4, _, xh4, rs4 = _ln_fwd(x3, y4, sm["ln_g"][1, 1][None], sm["ln_b"][1, 1][None], alpha, name="ln11")

    dx4, loss_part = _loss_grad(x4, loss_target[0], name="loss")
    loss = lax.psum(loss_part[0, 0], ("x", "y", "c"))

    def pair_reduce(named, tag):
        g4 = []
        for k, g in named:
            rows, cols = (g.shape[1], g.shape[2]) if BIG_KIND[k] == "col" else (g.shape[0] // N_CHIPS, g.shape[1])
            g4.append(g.reshape(N_CHIPS, 2, rows // 2, cols))
        sib = _pair_exchange(g4, name=f"grad_pair_exchange_{tag}")
        return [_pair_sum(pos, g, r, name=f"grad_pair_sum_{tag}{t}") for t, (g, r) in enumerate(zip(g4, sib))]

    dr4, dr4b, dg11, db11 = _ln_bwd(dx4, xh4, rs4, sm["ln_g"][1, 1][None], name="ln11_bwd")
    dx3, g_up1, g_down1, dcw1, dcb1, _ = _ffn_bwd(dr4b, dr4, alpha, x3b, hu1, z1, full["ffn_w_up"],
                                                  full["ffn_w_down"], 1, sm["ffn_conv_w"][1],
                                                  sm["ffn_conv_b"][1][None], "ffn1")
    parts_f1 = pair_reduce([("ffn_w_up", g_up1), ("ffn_w_down", g_down1)], "f1")
    dr3, dr3b, dg10, db10 = _ln_bwd(dx3, xh3, rs3, sm["ln_g"][1, 0][None], name="ln10_bwd")
    g_odout = _matmul(mix1, dr3b, mode="tn", out_dtype=BF16, name="od_dwout", tm=512, tn=1024)
    dmix1 = _matmul(dr3b, full["od_w_out"], mode="nt", b_lead=0, out_dtype=BF16, name="od_dmix", tm=1024, tn=1024)
    dh1, d_odconv, d_pool, d_pscale = _odd_bwd(dmix1, h1, sm["od_conv_w"][0], pool_w_bf, sm["od_pool_scale"],
                                               name="odd_bwd")
    g_odin = _matmul(x2b, dh1, mode="tn", out_split=True, out_dtype=BF16, name="od_dwin", tm=512, tn=1024)
    dx2 = _matmul(dh1, full["od_w_in"], mode="nt", b_lead=0, b_split=True, out_dtype=F32, add=dr3, add_scale=alpha,
                  name="od_dx", tm=1024, tn=512)
    dr2, dr2b, dg01, db01 = _ln_bwd(dx2, xh2, rs2, sm["ln_g"][0, 1][None], name="ln01_bwd")
    dx1, g_up0, g_down0, dcw0, dcb0, land_f1 = _ffn_bwd(dr2b, dr2, alpha, x1b, hu0, z0, full["ffn_w_up"],
                                                        full["ffn_w_down"], 0, sm["ffn_conv_w"][0],
                                                        sm["ffn_conv_b"][0][None], "ffn0", exchange=parts_f1)
    parts_b = pair_reduce([("od_w_in", g_odin), ("od_w_out", g_odout), ("ffn_w_up", g_up0),
                           ("ffn_w_down", g_down0)], "b")
    dr1, dr1b, dg00, db00 = _ln_bwd(dx1, xh1, rs1, sm["ln_g"][0, 0][None], name="ln00_bwd")
    g_evout = _matmul(mix0, dr1b, mode="tn", out_dtype=BF16, name="ev_dwout", tm=512, tn=1024)
    dmix0 = _matmul(dr1b, full["ev_w_out"], mode="nt", b_lead=0, out_dtype=BF16, name="ev_dmix", tm=1024, tn=1024)
    dq, dk, dv, *land_b = _attn_bwd(h0, dmix0, tot, n_heads, name="attn_bwd", exchange=parts_b)
    da, dgate, d_dww, d_dwb, d_bng, d_bnb = _evenconv_bwd(dmix0, u1, h0, sm["ev_dw_w"][0], sm["ev_bn_g"],
                                                          sm["ev_bn_b"], name="evconv_bwd")
    dh0 = jnp.concatenate([dq, dk, dv, da, dgate], axis=1)
    g_evin = _matmul(x0b, dh0, mode="tn", out_split=True, out_dtype=BF16, name="ev_dwin", tm=512, tn=1280)
    grad_x = _matmul(dh0, full["ev_w_in"], mode="nt", b_lead=0, b_split=True, out_dtype=F32, add=dr1, add_scale=alpha,
                     name="ev_dx", tm=1024, tn=512)

    parts_e = pair_reduce([("ev_w_in", g_evin), ("ev_w_out", g_evout)], "e")
    land_e = _chip_exchange(parts_e, name="grad_chip_exchange_e")
    order = ["ffn_w_up1", "ffn_w_down1", "od_w_in0", "od_w_out0", "ffn_w_up0", "ffn_w_down0", "ev_w_in0", "ev_w_out0"]
    parts = parts_f1 + parts_b + parts_e
    land = list(land_f1) + list(land_b) + list(land_e)
    halves = [_chip_sum(pos, p, ld, name=f"grad_chip_sum_{tag}") for tag, p, ld in zip(order, parts, land)]
    reduced = dict(zip(order, _half_swap(halves, name="grad_half_swap")))
    big_grads = {}
    for k in BIG:
        L = wts[k].shape[0]
        per_layer = [reduced[f"{k}{l}"].reshape(wts[k].shape[1:]) for l in range(L)]
        big_grads[k] = per_layer[0][None] if L == 1 else jnp.stack(per_layer)

    d_ln_g = jnp.stack([jnp.stack([dg00[0], dg01[0]]), jnp.stack([dg10[0], dg11[0]])])
    d_ln_b = jnp.stack([jnp.stack([db00[0], db01[0]]), jnp.stack([db10[0], db11[0]])])
    small_partial = {
        "ev_dw_w": d_dww[None], "ev_dw_b": d_dwb, "ev_bn_g": d_bng, "ev_bn_b": d_bnb,
        "od_conv_w": d_odconv[None], "od_pool_w": d_pool[None], "od_pool_scale": d_pscale,
        "ffn_conv_w": jnp.stack([dcw0, dcw1]), "ffn_conv_b": jnp.concatenate([dcb0, dcb1], axis=0),
        "ln_g": d_ln_g, "ln_b": d_ln_b}
    small_names = [k for k in WEIGHTS if k not in BIG]
    packed = _pack([small_partial[k] for k in small_names])
    summed = _sum_slots(_gather_all_devices(packed, name="gather_small_grads"), name="sum_small_grads")
    small_full = dict(zip(small_names, _unpack(summed, [small_partial[k].shape for k in small_names])))
    small_grads = {}
    for k in small_names:
        ax = SMALL_AXIS[k]
        if ax is None:
            small_grads[k] = small_full[k]
        else:
            size = wts[k].shape[ax]
            small_grads[k] = lax.dynamic_slice_in_dim(small_full[k], chip * size, size, axis=ax)

    grads, delta, new_m, new_v = {}, {}, {}, {}
    for k in BIG:
        grads[k], delta[k], new_m[k], new_v[k] = _adamw(wts[k], big_grads[k], mom[k], var[k], name=f"adamw_{k}")
    shapes = [wts[k].shape for k in small_names]
    pw, pg, pm, pv = (_pack([d[k] for k in small_names]) for d in (wts, small_grads, mom, var))
    sg, sd, smn, svn = _adamw(pw[None], pg[None], pm[None], pv[None], name="adamw_small")
    for dst, buf in ((grads, sg), (delta, sd), (new_m, smn), (new_v, svn)):
        for k, a in zip(small_names, _unpack(buf[0], shapes)):
            dst[k] = a

    return (loss, grad_x[None], *[grads[k] for k in WEIGHTS], *[delta[k] for k in WEIGHTS],
            *[new_m[k] for k in WEIGHTS], *[new_v[k] for k in WEIGHTS])
```

```python
import collections

import jax
import jax.numpy as jnp
from jax import lax
from jax.experimental import pallas as pl
from jax.experimental.pallas import tpu as pltpu

F32 = jnp.float32
BF16 = jnp.bfloat16

HEAD_DIM = 128
POOL_WINDOWS = (2, 4, 8, 16)
LN_EPS = 1e-5
ADAM_LR = 0.001
ADAM_B1 = 0.9
ADAM_B2 = 0.999
ADAM_EPS = 1e-08
ADAM_WD = 0.01
ADAM_STEP = 10
N_CHIPS = 4
N_DEV = 8
MESH = pl.DeviceIdType.MESH
LANES = 128
HALO3 = 16
HALO31 = 32

ANY = pl.BlockSpec(memory_space=pl.ANY)


def _pick(n, pref, mult=LANES):
    if n <= pref:
        return n
    t = (pref // mult) * mult
    while t >= mult:
        if n % t == 0:
            return t
        t -= mult
    return n


def _params(*sem):
    return pltpu.CompilerParams(dimension_semantics=sem)


def _matmul(a, b, *, mode, out_dtype, name, b_lead=None, b_split=False, out_split=False, add=None,
            add_scale=1.0, tm=512, tn=1024, tk=None, hosted=None):
    halves = isinstance(a, tuple) or isinstance(b, tuple)
    if isinstance(a, tuple):
        assert mode == "nt" and b_split and tk is None
        ash = (a[0].shape[0], 2 * a[0].shape[1])
    else:
        ash = a.shape[-2:]
    if isinstance(b, tuple):
        assert mode == "tn" and tk is None
        bsh = (b[0].shape[0], 2 * b[0].shape[1])
    else:
        bsh = b.shape[-2:]
    if mode == "nn":
        (M, K), (K2, N) = ash, bsh
        if b_split:
            N = N * N_CHIPS
    elif mode == "nt":
        (M, K), (N, K2) = ash, bsh
        if b_split:
            K2 = K2 * N_CHIPS
    else:
        (K, M), (K2, N) = ash, bsh
    assert K == K2, (ash, bsh, mode)
    tm = _pick(M, tm)
    tn = _pick(N // N_CHIPS if (out_split or (b_split and mode == "nn")) else N, tn)
    whole_split_k = b_split and mode == "nt" and tk is None
    if tk is None:
        tk = K
    else:
        tk = _pick(K // N_CHIPS if (b_split and mode == "nt") else K, tk)
    nk = K // tk
    kq = K // N_CHIPS
    n_per = (N // N_CHIPS) // tn
    k_per = (K // N_CHIPS) // tk

    def lead(shape, idx):
        if b_lead is None:
            return pl.BlockSpec(shape, idx)
        return pl.BlockSpec((None,) + shape, lambda i, j, k: (b_lead,) + idx(i, j, k))

    if mode == "nn":
        a_spec = pl.BlockSpec((tm, tk), lambda i, j, k: (i, k))
        if b_split:
            b_spec = lead((None, tk, tn), lambda i, j, k: (lax.div(j, n_per), k, lax.rem(j, n_per)))
        else:
            b_spec = lead((tk, tn), lambda i, j, k: (k, j))
        dims = (((1,), (0,)), ((), ()))
    elif mode == "nt":
        a_spec = pl.BlockSpec((tm, tk), lambda i, j, k: (i, k))
        if whole_split_k:
            b_spec = lead((N_CHIPS, tn, kq), lambda i, j, k: (0, j, 0))
        elif b_split:
            b_spec = lead((None, tn, tk), lambda i, j, k: (lax.div(k, k_per), j, lax.rem(k, k_per)))
        else:
            b_spec = lead((tn, tk), lambda i, j, k: (j, k))
        dims = (((1,), (1,)), ((), ()))
    else:
        a_spec = pl.BlockSpec((tk, tm), lambda i, j, k: (k, i))
        b_spec = pl.BlockSpec((tk, tn), lambda i, j, k: (k, j))
        dims = (((0,), (0,)), ((), ()))
    if out_split:
        out_shape = jax.ShapeDtypeStruct((N_CHIPS, M, N // N_CHIPS), out_dtype)
        out_spec = pl.BlockSpec((None, tm, tn), lambda i, j, k: (lax.div(j, n_per), i, lax.rem(j, n_per)))
    else:
        out_shape = jax.ShapeDtypeStruct((M, N), out_dtype)
        out_spec = pl.BlockSpec((tm, tn), lambda i, j, k: (i, j))
    grid = (M // tm, N // tn, nk)
    nj_half = grid[1] // 2
    if isinstance(a, tuple):
        in_specs = [pl.BlockSpec((tm, K // 2), lambda i, j, k: (i, 0))] * 2 + [b_spec]
        args = [a[0], a[1], b]
    elif isinstance(b, tuple):
        in_specs = [a_spec,
                    pl.BlockSpec((tk, tn), lambda i, j, k: (k, jnp.minimum(j, nj_half - 1))),
                    pl.BlockSpec((tk, tn), lambda i, j, k: (k, jnp.maximum(j - nj_half, 0)))]
        args = [a, b[0], b[1]]
    else:
        in_specs = [a_spec, b_spec]
        args = [a, b]
    n_op = len(args)
    if add is not None:
        in_specs.append(pl.BlockSpec((tm, tn), lambda i, j, k: (i, j)))
        args.append(add)

    n_in = len(args)
    h_in = 0 if hosted is None else len(hosted.ins)
    h_out = 0 if hosted is None else len(hosted.outs)

    def body(*refs):
        ops = refs[:n_op]
        add_ref = refs[n_op] if add is not None else None
        h_ins = refs[n_in:n_in + h_in]
        o_ref = refs[n_in + h_in]
        h_outs = refs[n_in + h_in + 1:n_in + h_in + 1 + h_out]
        scr = refs[n_in + h_in + 1 + h_out:]
        i, j, k = pl.program_id(0), pl.program_id(1), pl.program_id(2)
        if hosted is not None:
            sems = scr[len(scr) - len(hosted.sems):]

            @pl.when((i == 0) & (j == 0) & (k == 0))
            def _():
                hosted.start(h_ins, h_outs, sems)

        def finish(res):
            if add_ref is not None:
                res = res + add_scale * add_ref[...]
            o_ref[...] = res.astype(out_dtype)

        def dot(x, y):
            return lax.dot_general(x, y, dims, preferred_element_type=F32)

        if isinstance(b, tuple):
            @pl.when(j < nj_half)
            def _():
                finish(dot(ops[0][...], ops[1][...]))

            @pl.when(j >= nj_half)
            def _():
                finish(dot(ops[0][...], ops[2][...]))
            part = None
        elif whole_split_k:
            srcs = [(ops[0], s) for s in range(N_CHIPS)] if not isinstance(a, tuple) else \
                   [(ops[s // 2], s % 2) for s in range(N_CHIPS)]
            b_ref = ops[-1]
            part = None
            for s, (src, off) in enumerate(srcs):
                term = dot(src[:, off * kq:(off + 1) * kq], b_ref[s])
                part = term if part is None else part + term
        else:
            part = dot(ops[0][...], ops[1][...])

        if part is None:
            pass
        elif nk == 1:
            finish(part)
        else:
            acc = scr[0]

            @pl.when(k == 0)
            def _():
                acc[...] = part

            @pl.when(k > 0)
            def _():
                acc[...] += part

            @pl.when(k == nk - 1)
            def _():
                finish(acc[...])

        if hosted is not None:
            @pl.when((i == grid[0] - 1) & (j == grid[1] - 1) & (k == nk - 1))
            def _():
                hosted.finish(h_ins, h_outs, sems)

    scratch = [pltpu.VMEM((tm, tn), F32)] if nk > 1 else []
    if hosted is not None:
        res = pl.pallas_call(
            body, name=name,
            out_shape=(out_shape,) + tuple(hosted.outs),
            grid=grid,
            in_specs=in_specs + [ANY] * h_in,
            out_specs=(out_spec,) + (ANY,) * h_out,
            input_output_aliases={n_in + src: 1 + dst for src, dst in hosted.alias.items()},
            scratch_shapes=scratch + [pltpu.SemaphoreType.DMA((n,)) for n in hosted.sems],
            compiler_params=_params("arbitrary", "arbitrary", "arbitrary"),
        )(*args, *hosted.ins)
        return res[0], list(res[1:])
    return pl.pallas_call(
        body, name=name,
        out_shape=out_shape,
        grid=grid,
        in_specs=in_specs,
        out_specs=out_spec,
        scratch_shapes=scratch,
        compiler_params=_params("parallel", "parallel", "arbitrary"),
    )(*args)


def _cast_bf16(w, name):
    L, R, C = w.shape
    tr, tc = _pick(R, 512, 16), _pick(C, 1408)

    def body(w_ref, o_ref):
        o_ref[...] = w_ref[...].astype(BF16)

    return pl.pallas_call(
        body, name=name, out_shape=jax.ShapeDtypeStruct(w.shape, BF16),
        grid=(L, R // tr, C // tc),
        in_specs=[pl.BlockSpec((None, tr, tc), lambda l, i, j: (l, i, j))],
        out_specs=pl.BlockSpec((None, tr, tc), lambda l, i, j: (l, i, j)),
        compiler_params=_params("parallel", "parallel", "parallel"),
    )(w)


def _cast_into_gather(pos, w, layer, name):
    L, R, C = w.shape
    r2 = R // 2
    tr, tc = _pick(r2, 512, 16), _pick(C, 1408)

    def body(p_ref, w_ref, o_ref):
        o_ref[...] = w_ref[...].astype(BF16)

    return pl.pallas_call(
        body, name=name, out_shape=jax.ShapeDtypeStruct((1, N_CHIPS, 2, r2, C), BF16),
        grid_spec=pltpu.PrefetchScalarGridSpec(
            num_scalar_prefetch=1, grid=(2, r2 // tr, C // tc),
            in_specs=[pl.BlockSpec((None, None, tr, tc), lambda h, i, j, p: (layer, h, i, j))],
            out_specs=pl.BlockSpec((None, None, None, tr, tc), lambda h, i, j, p: (0, p[0], h, i, j))),
        compiler_params=_params("parallel", "parallel", "parallel"),
    )(pos, w.reshape(L, 2, r2, C))


def _sigmoid(v):
    return 1.0 / (1.0 + jnp.exp(-v))


def _ln_fwd(x, y, g, b, alpha, name):
    S, D = x.shape
    tr = _pick(S, 256, 8)

    def body(x_ref, y_ref, g_ref, b_ref, o_ref, ob_ref, xh_ref, rs_ref):
        r = alpha * x_ref[...] + y_ref[...]
        mu = jnp.mean(r, axis=-1, keepdims=True)
        d = r - mu
        var = jnp.mean(d * d, axis=-1, keepdims=True)
        rstd = lax.rsqrt(var + LN_EPS)
        xh = d * rstd
        o = xh * g_ref[...] + b_ref[...]
        o_ref[...] = o
        ob_ref[...] = o.astype(BF16)
        xh_ref[...] = xh
        rs_ref[...] = rstd

    row = pl.BlockSpec((tr, D), lambda i: (i, 0))
    vec = pl.BlockSpec((1, D), lambda i: (0, 0))
    return pl.pallas_call(
        body, name=name,
        out_shape=(jax.ShapeDtypeStruct((S, D), F32), jax.ShapeDtypeStruct((S, D), BF16),
                   jax.ShapeDtypeStruct((S, D), F32), jax.ShapeDtypeStruct((S, 1), F32)),
        grid=(S // tr,),
        in_specs=[row, row, vec, vec],
        out_specs=(row, row, row, pl.BlockSpec((tr, 1), lambda i: (i, 0))),
        compiler_params=_params("parallel"),
    )(x, y, g, b)


def _ln_bwd(dout, xhat, rstd, g, name):
    S, D = dout.shape
    tr = _pick(S, 256, 8)

    def body(do_ref, xh_ref, rs_ref, g_ref, dr_ref, drb_ref, dg_ref, db_ref):
        i = pl.program_id(0)
        do = do_ref[...]
        xh = xh_ref[...]
        dxh = do * g_ref[...]
        m1 = jnp.mean(dxh, axis=-1, keepdims=True)
        m2 = jnp.mean(dxh * xh, axis=-1, keepdims=True)
        dr = rs_ref[...] * (dxh - m1 - xh * m2)
        dr_ref[...] = dr
        drb_ref[...] = dr.astype(BF16)
        pg = jnp.sum(do * xh, axis=0, keepdims=True)
        pb = jnp.sum(do, axis=0, keepdims=True)

        @pl.when(i == 0)
        def _():
            dg_ref[...] = pg
            db_ref[...] = pb

        @pl.when(i > 0)
        def _():
            dg_ref[...] += pg
            db_ref[...] += pb

    row = pl.BlockSpec((tr, D), lambda i: (i, 0))
    vec = pl.BlockSpec((1, D), lambda i: (0, 0))
    return pl.pallas_call(
        body, name=name,
        out_shape=(jax.ShapeDtypeStruct((S, D), F32), jax.ShapeDtypeStruct((S, D), BF16),
                   jax.ShapeDtypeStruct((1, D), F32), jax.ShapeDtypeStruct((1, D), F32)),
        grid=(S // tr,),
        in_specs=[row, row, pl.BlockSpec((tr, 1), lambda i: (i, 0)), vec],
        out_specs=(row, row, vec, vec),
        compiler_params=_params("arbitrary"),
    )(dout, xhat, rstd, g)


def _loss_grad(y, target, name):
    S, D = y.shape
    tr = _pick(S, 256, 8)
    n = S // tr

    def body(y_ref, t_ref, dy_ref, l_ref, acc):
        i = pl.program_id(0)
        d = y_ref[...] - t_ref[...]
        dy_ref[...] = d * (1.0 / D)
        p = jnp.sum(d * d, axis=0, keepdims=True)

        @pl.when(i == 0)
        def _():
            acc[...] = p

        @pl.when(i > 0)
        def _():
            acc[...] += p

        @pl.when(i == n - 1)
        def _():
            l_ref[...] = (0.5 / D) * jnp.sum(acc[...], axis=1, keepdims=True)

    row = pl.BlockSpec((tr, D), lambda i: (i, 0))
    return pl.pallas_call(
        body, name=name,
        out_shape=(jax.ShapeDtypeStruct((S, D), F32), jax.ShapeDtypeStruct((1, 1), F32)),
        grid=(n,),
        in_specs=[row, row],
        out_specs=(row, pl.BlockSpec((1, 1), lambda i: (0, 0))),
        scratch_shapes=[pltpu.VMEM((1, D), F32)],
        compiler_params=_params("arbitrary"),
    )(y, target)


def _prev_spec(tr, halo, width, col):
    return pl.BlockSpec((halo, width), lambda c, i: (jnp.maximum(i * (tr // halo) - 1, 0), col(c)))


def _next_spec(tr, halo, width, col, nrows):
    last = nrows // halo - 1
    return pl.BlockSpec((halo, width), lambda c, i: (jnp.minimum((i + 1) * (tr // halo), last), col(c)))


def _cur_spec(tr, width, col):
    return pl.BlockSpec((tr, width), lambda c, i: (i, col(c)))


def _ffn_act_fwd(hu, conv_w, conv_b, name):
    S, F2 = hu.shape
    F = F2 // 2
    tr, tc, H = _pick(S, 512, 16), _pick(F, 512), HALO3
    nc, nr = F // tc, S // tr

    def body(gp_ref, g_ref, u_ref, w_ref, b_ref, z_ref, G):
        i = pl.program_id(1)
        G[0:H, :] = jnp.where(i > 0, gp_ref[...].astype(F32), 0.0)
        G[H:H + tr, :] = g_ref[...].astype(F32)
        gc = (b_ref[...] + w_ref[pl.ds(0, 1), :] * G[pl.ds(H - 2, tr), :]
              + w_ref[pl.ds(1, 1), :] * G[pl.ds(H - 1, tr), :] + w_ref[pl.ds(2, 1), :] * G[pl.ds(H, tr), :])
        z = gc * _sigmoid(gc) * u_ref[...].astype(F32)
        z_ref[...] = z.astype(BF16)

    gcol = lambda c: c
    ucol = lambda c: c + nc
    return pl.pallas_call(
        body, name=name, out_shape=jax.ShapeDtypeStruct((S, F), BF16),
        grid=(nc, nr),
        in_specs=[_prev_spec(tr, H, tc, gcol), _cur_spec(tr, tc, gcol), _cur_spec(tr, tc, ucol),
                  pl.BlockSpec((3, tc), lambda c, i: (0, c)), pl.BlockSpec((1, tc), lambda c, i: (0, c))],
        out_specs=pl.BlockSpec((tr, tc), lambda c, i: (i, c)),
        scratch_shapes=[pltpu.VMEM((H + tr, tc), F32)],
        compiler_params=_params("parallel", "parallel"),
    )(hu, hu, hu, conv_w, conv_b)


def _ffn_act_bwd(dz, hu, conv_w, conv_b, name):
    S, F = dz.shape
    tr, tc, H = _pick(S, 512, 16), _pick(F, 512), HALO3
    nc, nr = F // tc, S // tr
    n = tr + H

    def body(dz_ref, dzn_ref, gp_ref, g_ref, gn_ref, u_ref, un_ref, w_ref, b_ref,
             dg_ref, du_ref, dw_ref, db_ref, G, DG):
        i = pl.program_id(1)
        G[0:H, :] = jnp.where(i > 0, gp_ref[...].astype(F32), 0.0)
        G[H:H + tr, :] = g_ref[...].astype(F32)
        G[H + tr:H + tr + H, :] = gn_ref[...].astype(F32)
        w0, w1, w2 = w_ref[pl.ds(0, 1), :], w_ref[pl.ds(1, 1), :], w_ref[pl.ds(2, 1), :]
        gc = b_ref[...] + w0 * G[pl.ds(H - 2, n), :] + w1 * G[pl.ds(H - 1, n), :] + w2 * G[pl.ds(H, n), :]
        sg = _sigmoid(gc)
        dzf = jnp.concatenate([dz_ref[...], dzn_ref[...]], axis=0).astype(F32)
        uf = jnp.concatenate([u_ref[...], un_ref[...]], axis=0).astype(F32)
        rows = lax.broadcasted_iota(jnp.int32, (n, 1), 0)
        dzf = jnp.where((rows < tr) | (i < nr - 1), dzf, 0.0)
        dgc = dzf * uf * (sg * (1.0 + gc * (1.0 - sg)))
        du_ref[...] = (dzf[0:tr] * (gc[0:tr] * sg[0:tr])).astype(BF16)
        DG[...] = dgc
        dg = w2 * DG[pl.ds(0, tr), :] + w1 * DG[pl.ds(1, tr), :] + w0 * DG[pl.ds(2, tr), :]
        dg_ref[...] = dg.astype(BF16)
        dcur = dgc[0:tr]
        pw = [jnp.sum(dcur * G[pl.ds(H - 2 + k, tr), :], axis=0, keepdims=True) for k in range(3)]
        pb = jnp.sum(dcur, axis=0, keepdims=True)

        @pl.when(i == 0)
        def _():
            for k in range(3):
                dw_ref[pl.ds(k, 1), :] = pw[k]
            db_ref[...] = pb

        @pl.when(i > 0)
        def _():
            for k in range(3):
                dw_ref[pl.ds(k, 1), :] += pw[k]
            db_ref[...] += pb

    gcol = lambda c: c
    ucol = lambda c: c + nc
    blk = pl.BlockSpec((tr, tc), lambda c, i: (i, c))
    return pl.pallas_call(
        body, name=name,
        out_shape=(jax.ShapeDtypeStruct((S, F), BF16), jax.ShapeDtypeStruct((S, F), BF16),
                   jax.ShapeDtypeStruct((3, F), F32), jax.ShapeDtypeStruct((1, F), F32)),
        grid=(nc, nr),
        in_specs=[_cur_spec(tr, tc, gcol), _next_spec(tr, H, tc, gcol, S),
                  _prev_spec(tr, H, tc, gcol), _cur_spec(tr, tc, gcol), _next_spec(tr, H, tc, gcol, S),
                  _cur_spec(tr, tc, ucol), _next_spec(tr, H, tc, ucol, S),
                  pl.BlockSpec((3, tc), lambda c, i: (0, c)), pl.BlockSpec((1, tc), lambda c, i: (0, c))],
        out_specs=(blk, blk, pl.BlockSpec((3, tc), lambda c, i: (0, c)), pl.BlockSpec((1, tc), lambda c, i: (0, c))),
        scratch_shapes=[pltpu.VMEM((H + tr + H, tc), F32), pltpu.VMEM((n, tc), F32)],
        compiler_params=_params("parallel", "arbitrary"),
    )(dz, dz, hu, hu, hu, hu, hu, conv_w, conv_b)


def _softplus_neg(s):
    return jnp.minimum(-s, 0.0) - jnp.log(1.0 + jnp.exp(-jnp.abs(s)))


def _hilo_dot(v, m):
    hi = v.astype(BF16)
    lo = (v - hi.astype(F32)).astype(BF16)
    return (jnp.dot(hi, m, preferred_element_type=F32) + jnp.dot(lo, m, preferred_element_type=F32))


def _attn_fwd(h, n_heads, name, gather=()):
    S = h.shape[0]
    dh = HEAD_DIM
    A = n_heads * dh
    tq = _pick(S, 256)
    nq = S // tq
    scale = 1.0 / float(dh) ** 0.5
    ng = len(gather)

    def body(*refs):
        q_ref, k_ref, v_ref = refs[:3]
        o_ref, tot_ref = refs[3 + ng:5 + ng]
        full = refs[5 + ng:5 + 2 * ng]
        hd = pl.program_id(0)
        i = pl.program_id(1)
        if ng:
            ssem, rsem = refs[5 + 2 * ng:]

            @pl.when((hd == 0) & (i == 0))
            def _():
                _gather_start(full, ssem, rsem)

            @pl.when((hd == n_heads - 1) & (i == 0))
            def _():
                _gather_forward(full, ssem, rsem)

        q = q_ref[...]
        r_io = lax.broadcasted_iota(jnp.int32, (tq, tq), 0)
        c_io = lax.broadcasted_iota(jnp.int32, (tq, tq), 1)
        later = (r_io > c_io).astype(BF16)
        ext = jnp.concatenate([later, jnp.ones((tq, LANES), BF16)], axis=1)
        causal = c_io < r_io

        def scores(j, diag):
            kj = k_ref[pl.ds(pl.multiple_of(j * tq, tq), tq), :]
            s = lax.dot_general(q, kj, (((1,), (1,)), ((), ())), preferred_element_type=F32) * scale
            ls = _softplus_neg(s)
            if diag:
                ls = jnp.where(causal, ls, 0.0)
            cs = _hilo_dot(ls, ext)
            base = s + ls + cs[:, :tq]
            if diag:
                base = jnp.where(causal, base, -1e30)
            return base, cs[:, tq:]

        def weigh(j, acc, run, base):
            vj = v_ref[pl.ds(pl.multiple_of(j * tq, tq), tq), :]
            w = jnp.exp(base + jnp.tile(run, (1, tq // LANES)))
            return acc + jnp.dot(w.astype(BF16), vj, preferred_element_type=F32)

        def trip(t, carry):
            acc, run, base, tot = carry
            j = i - 1 - t
            kj = k_ref[pl.ds(pl.multiple_of(j * tq, tq), tq), :]
            s = lax.dot_general(q, kj, (((1,), (1,)), ((), ())), preferred_element_type=F32) * scale
            acc = weigh(j + 1, acc, run, base)
            ls = _softplus_neg(s)
            cs = _hilo_dot(ls, ext)
            return acc, run + tot, s + ls + cs[:, :tq], cs[:, tq:]

        base, tot = scores(i, True)
        carry = (jnp.zeros((tq, dh), F32), jnp.zeros((tq, LANES), F32), base, tot)
        acc, run, base, tot = lax.fori_loop(0, i, trip, carry)
        acc = weigh(0, acc, run, base)
        o_ref[...] = acc.astype(BF16)
        tot_ref[...] = run + tot
        if ng:
            @pl.when((hd == n_heads - 1) & (i == nq - 1))
            def _():
                _gather_finish(full, ssem, rsem)

    T = _gather_items(gather) if ng else 0
    return pl.pallas_call(
        body, name=name,
        out_shape=(jax.ShapeDtypeStruct((S, A), BF16), jax.ShapeDtypeStruct((n_heads, S, LANES), F32))
        + tuple(jax.ShapeDtypeStruct(b.shape, b.dtype) for b in gather),
        grid=(n_heads, nq),
        in_specs=[pl.BlockSpec((tq, dh), lambda hd, i: (i, hd)),
                  pl.BlockSpec((S, dh), lambda hd, i: (0, n_heads + hd)),
                  pl.BlockSpec((S, dh), lambda hd, i: (0, 2 * n_heads + hd))] + [ANY] * ng,
        out_specs=(pl.BlockSpec((tq, dh), lambda hd, i: (i, hd)),
                   pl.BlockSpec((None, tq, LANES), lambda hd, i: (hd, i, 0))) + (ANY,) * ng,
        input_output_aliases={3 + a: 2 + a for a in range(ng)},
        scratch_shapes=[pltpu.SemaphoreType.DMA((6 * T,)), pltpu.SemaphoreType.DMA((6 * T,))] if ng else [],
        compiler_params=_params("arbitrary", "arbitrary") if ng else _params("parallel", "parallel"),
    )(h, h, h, *gather)


def _attn_bwd(h, do, tot, n_heads, name, exchange=()):
    S = h.shape[0]
    dh = HEAD_DIM
    A = n_heads * dh
    tq = _pick(S, 256)
    nq = S // tq
    scale = 1.0 / float(dh) ** 0.5
    tn_dims = (((0,), (0,)), ((), ()))
    ne = len(exchange)

    def body(*refs):
        q_ref, k_ref, v_ref, do_ref, tot_ref = refs[:5]
        parts = refs[5:5 + ne]
        dq_ref, dk_ref, dv_ref = refs[5 + ne:8 + ne]
        land = refs[8 + ne:8 + 2 * ne]
        dk_acc, dv_acc = refs[8 + 2 * ne:10 + 2 * ne]
        hd = pl.program_id(0)
        i = pl.program_id(1)
        if ne:
            ssem, rsem = refs[10 + 2 * ne:]

            @pl.when((hd == 0) & (i == 0))
            def _():
                _exchange_start(parts, land, ssem, rsem)

        @pl.when(i == 0)
        def _():
            dk_acc[...] = jnp.zeros_like(dk_acc)
            dv_acc[...] = jnp.zeros_like(dv_acc)

        q = q_ref[...]
        dob = do_ref[...]
        total = jnp.tile(tot_ref[...], (1, tq // LANES))
        r_io = lax.broadcasted_iota(jnp.int32, (tq, tq), 0)
        c_io = lax.broadcasted_iota(jnp.int32, (tq, tq), 1)
        ones = jnp.ones((tq, LANES), BF16)
        upto = jnp.concatenate([(r_io <= c_io).astype(BF16), ones], axis=1)
        before = jnp.concatenate([(r_io < c_io).astype(BF16), ones], axis=1)
        causal = c_io < r_io

        def scores(j):
            start = pl.multiple_of(j * tq, tq)
            kj = k_ref[pl.ds(start, tq), :]
            vj = v_ref[pl.ds(start, tq), :]
            keep = jnp.logical_or(causal, j != i)
            s = lax.dot_general(q, kj, (((1,), (1,)), ((), ())), preferred_element_type=F32) * scale
            ls = jnp.where(keep, _softplus_neg(s), 0.0)
            cs = _hilo_dot(ls, upto)
            base = jnp.where(keep, s + ls - cs[:, :tq], -1e30)
            dw = lax.dot_general(dob, vj, (((1,), (1,)), ((), ())), preferred_element_type=F32)
            return base, jnp.exp(ls), dw, cs[:, tq:]

        def grads(j, dq, prun, erun, base, sn, dw):
            start = pl.multiple_of(j * tq, tq)
            kj = k_ref[pl.ds(start, tq), :]
            w = jnp.exp(base + (total - jnp.tile(prun, (1, tq // LANES))))
            e = dw * w
            ce = jnp.dot(e.astype(BF16), before, preferred_element_type=F32)
            ecum = ce[:, :tq] + jnp.tile(erun, (1, tq // LANES))
            dz = e * sn - (1.0 - sn) * ecum
            ds = (dz * scale).astype(BF16)
            dq = dq + jnp.dot(ds, kj, preferred_element_type=F32)
            dk_acc[pl.ds(start, tq), :] += lax.dot_general(ds, q, tn_dims, preferred_element_type=F32)
            dv_acc[pl.ds(start, tq), :] += lax.dot_general(w.astype(BF16), dob, tn_dims, preferred_element_type=F32)
            return dq, erun + ce[:, tq:]

        def trip(j, carry):
            dq, prun, erun, base, sn, dw, ptot = carry
            start = pl.multiple_of(j * tq, tq)
            nstart = pl.multiple_of((j + 1) * tq, tq)
            kj = k_ref[pl.ds(start, tq), :]
            kn = k_ref[pl.ds(nstart, tq), :]
            vn = v_ref[pl.ds(nstart, tq), :]
            keep = jnp.logical_or(causal, j + 1 != i)
            s_n = lax.dot_general(q, kn, (((1,), (1,)), ((), ())), preferred_element_type=F32) * scale
            w = jnp.exp(base + (total - jnp.tile(prun, (1, tq // LANES))))
            e = dw * w
            ce = jnp.dot(e.astype(BF16), before, preferred_element_type=F32)
            ls_n = jnp.where(keep, _softplus_neg(s_n), 0.0)
            cs = _hilo_dot(ls_n, upto)
            ecum = ce[:, :tq] + jnp.tile(erun, (1, tq // LANES))
            dz = e * sn - (1.0 - sn) * ecum
            ds = (dz * scale).astype(BF16)
            dv_acc[pl.ds(start, tq), :] += lax.dot_general(w.astype(BF16), dob, tn_dims, preferred_element_type=F32)
            dq = dq + jnp.dot(ds, kj, preferred_element_type=F32)
            dk_acc[pl.ds(start, tq), :] += lax.dot_general(ds, q, tn_dims, preferred_element_type=F32)
            base_n = jnp.where(keep, s_n + ls_n - cs[:, :tq], -1e30)
            dw_n = lax.dot_general(dob, vn, (((1,), (1,)), ((), ())), preferred_element_type=F32)
            return dq, prun + ptot, erun + ce[:, tq:], base_n, jnp.exp(ls_n), dw_n, cs[:, tq:]

        zero = jnp.zeros((tq, LANES), F32)
        carry = lax.fori_loop(0, i, trip, (jnp.zeros((tq, dh), F32), zero, zero) + scores(0))
        dq, prun, erun, base, sn, dw, _ = carry
        dq, _ = grads(i, dq, prun, erun, base, sn, dw)
        dq_ref[...] = dq.astype(BF16)

        @pl.when(i == nq - 1)
        def _():
            dk_ref[...] = dk_acc[...].astype(BF16)
            dv_ref[...] = dv_acc[...].astype(BF16)

        if ne:
            @pl.when((hd == n_heads - 1) & (i == nq - 1))
            def _():
                _exchange_finish(parts, land, ssem, rsem)

    qblk = pl.BlockSpec((tq, dh), lambda hd, i: (i, hd))
    full = pl.BlockSpec((S, dh), lambda hd, i: (0, hd))
    scratch = [pltpu.VMEM((S, dh), F32), pltpu.VMEM((S, dh), F32)]
    if ne:
        scratch += [pltpu.SemaphoreType.DMA((3 * ne,)), pltpu.SemaphoreType.DMA((3 * ne,))]
    return pl.pallas_call(
        body, name=name,
        out_shape=tuple(jax.ShapeDtypeStruct((S, A), BF16) for _ in range(3)) + tuple(_land_shape(p) for p in exchange),
        grid=(n_heads, nq),
        in_specs=[qblk,
                  pl.BlockSpec((S, dh), lambda hd, i: (0, n_heads + hd)),
                  pl.BlockSpec((S, dh), lambda hd, i: (0, 2 * n_heads + hd)),
                  qblk,
                  pl.BlockSpec((None, tq, LANES), lambda hd, i: (hd, i, 0))] + [ANY] * ne,
        out_specs=(qblk, full, full) + (ANY,) * ne,
        scratch_shapes=scratch,
        compiler_params=_params("arbitrary", "arbitrary") if ne else _params("parallel", "arbitrary"),
    )(h, h, h, do, tot, *exchange)


def _evenconv_fwd(h, dw_w, dw_b, bn_g, bn_b, name):
    S = h.shape[0]
    KW, A = dw_w.shape
    H = HALO31
    tr = _pick(S, 256, H)
    first_tap = H - (KW - 1)

    def body(ap_ref, a_ref, gp_ref, g_ref, w_ref, b_ref, bg_ref, bb_ref, u1_ref, u3_ref, U):
        i = pl.program_id(1)
        glu_prev = ap_ref[...].astype(F32) * _sigmoid(gp_ref[...].astype(F32))
        U[0:H, :] = jnp.where(i > 0, glu_prev, 0.0)
        U[H:H + tr, :] = a_ref[...].astype(F32) * _sigmoid(g_ref[...].astype(F32))
        acc = b_ref[...] + w_ref[pl.ds(0, 1), :] * U[pl.ds(first_tap, tr), :]
        for k in range(1, KW):
            acc = acc + w_ref[pl.ds(k, 1), :] * U[pl.ds(first_tap + k, tr), :]
        u1_ref[...] = acc
        mu = jnp.mean(acc, axis=-1, keepdims=True)
        d = acc - mu
        var = jnp.mean(d * d, axis=-1, keepdims=True)
        u2 = d * lax.rsqrt(var + LN_EPS) * bg_ref[...] + bb_ref[...]
        u3_ref[...] = (u2 * _sigmoid(u2)).astype(BF16)

    acol = lambda c: 3
    gcol = lambda c: 4
    vec = pl.BlockSpec((1, A), lambda c, i: (0, 0))
    blk = pl.BlockSpec((tr, A), lambda c, i: (i, 0))
    return pl.pallas_call(
        body, name=name,
        out_shape=(jax.ShapeDtypeStruct((S, A), F32), jax.ShapeDtypeStruct((S, A), BF16)),
        grid=(1, S // tr),
        in_specs=[_prev_spec(tr, H, A, acol), _cur_spec(tr, A, acol),
                  _prev_spec(tr, H, A, gcol), _cur_spec(tr, A, gcol),
                  pl.BlockSpec((KW, A), lambda c, i: (0, 0)), vec, vec, vec],
        out_specs=(blk, blk),
        scratch_shapes=[pltpu.VMEM((H + tr, A), F32)],
        compiler_params=_params("parallel", "parallel"),
    )(h, h, h, h, dw_w, dw_b, bn_g, bn_b)


def _evenconv_bwd(du3, u1, h, dw_w, bn_g, bn_b, name):
    S = h.shape[0]
    KW, A = dw_w.shape
    H = HALO31
    tr = _pick(S, 256, H)
    nr = S // tr
    n = tr + H
    first_tap = H - (KW - 1)

    def body(d3_ref, d3n_ref, u1_ref, u1n_ref, ap_ref, a_ref, gp_ref, g_ref, w_ref, bg_ref, bb_ref,
             da_ref, dg_ref, dww_ref, dwb_ref, dbg_ref, dbb_ref, U0, DU):
        i = pl.program_id(1)
        u1 = jnp.concatenate([u1_ref[...], u1n_ref[...]], axis=0)
        d3 = jnp.concatenate([d3_ref[...], d3n_ref[...]], axis=0).astype(F32)
        rows = lax.broadcasted_iota(jnp.int32, (n, 1), 0)
        d3 = jnp.where((rows < tr) | (i < nr - 1), d3, 0.0)
        mu = jnp.mean(u1, axis=-1, keepdims=True)
        d = u1 - mu
        var = jnp.mean(d * d, axis=-1, keepdims=True)
        rstd = lax.rsqrt(var + LN_EPS)
        xh = d * rstd
        u2 = xh * bg_ref[...] + bb_ref[...]
        sg = _sigmoid(u2)
        du2 = d3 * (sg * (1.0 + u2 * (1.0 - sg)))
        dxh = du2 * bg_ref[...]
        m1 = jnp.mean(dxh, axis=-1, keepdims=True)
        m2 = jnp.mean(dxh * xh, axis=-1, keepdims=True)
        du1 = rstd * (dxh - m1 - xh * m2)
        DU[...] = du1
        pbg = jnp.sum(du2[0:tr] * xh[0:tr], axis=0, keepdims=True)
        pbb = jnp.sum(du2[0:tr], axis=0, keepdims=True)
        pwb = jnp.sum(du1[0:tr], axis=0, keepdims=True)

        glu_prev = ap_ref[...].astype(F32) * _sigmoid(gp_ref[...].astype(F32))
        U0[0:H, :] = jnp.where(i > 0, glu_prev, 0.0)
        a = a_ref[...].astype(F32)
        sgg = _sigmoid(g_ref[...].astype(F32))
        U0[H:H + tr, :] = a * sgg

        @pl.when(i == 0)
        def _():
            dbg_ref[...] = pbg
            dbb_ref[...] = pbb
            dwb_ref[...] = pwb
            dww_ref[...] = jnp.zeros_like(dww_ref)

        @pl.when(i > 0)
        def _():
            dbg_ref[...] += pbg
            dbb_ref[...] += pbb
            dwb_ref[...] += pwb

        du0 = w_ref[pl.ds(0, 1), :] * DU[pl.ds(KW - 1, tr), :]
        for k in range(1, KW):
            du0 = du0 + w_ref[pl.ds(k, 1), :] * DU[pl.ds(KW - 1 - k, tr), :]
        da_ref[...] = (du0 * sgg).astype(BF16)
        dg_ref[...] = (du0 * a * sgg * (1.0 - sgg)).astype(BF16)
        dcur = DU[pl.ds(0, tr), :]
        for k in range(KW):
            dww_ref[pl.ds(k, 1), :] += jnp.sum(dcur * U0[pl.ds(first_tap + k, tr), :], axis=0, keepdims=True)

    acol = lambda c: 3
    gcol = lambda c: 4
    one = lambda c: 1
    zero = lambda c: 0
    vec = pl.BlockSpec((1, A), lambda c, i: (0, 0))
    blk = pl.BlockSpec((tr, A), lambda c, i: (i, 0))
    return pl.pallas_call(
        body, name=name,
        out_shape=(jax.ShapeDtypeStruct((S, A), BF16), jax.ShapeDtypeStruct((S, A), BF16),
                   jax.ShapeDtypeStruct((KW, A), F32), jax.ShapeDtypeStruct((1, A), F32),
                   jax.ShapeDtypeStruct((1, A), F32), jax.ShapeDtypeStruct((1, A), F32)),
        grid=(1, nr),
        in_specs=[_cur_spec(tr, A, one), _next_spec(tr, H, A, one, S),
                  _cur_spec(tr, A, zero), _next_spec(tr, H, A, zero, S),
                  _prev_spec(tr, H, A, acol), _cur_spec(tr, A, acol),
                  _prev_spec(tr, H, A, gcol), _cur_spec(tr, A, gcol),
                  pl.BlockSpec((KW, A), lambda c, i: (0, 0)), vec, vec],
        out_specs=(blk, blk, pl.BlockSpec((KW, A), lambda c, i: (0, 0)), vec, vec, vec),
        scratch_shapes=[pltpu.VMEM((H + tr, A), F32), pltpu.VMEM((n, A), F32)],
        compiler_params=_params("arbitrary", "arbitrary"),
    )(du3, du3, u1, u1, h, h, h, h, dw_w, bn_g, bn_b)


def _pool_inv_count(row0, nrows, window):
    t = row0 + lax.broadcasted_iota(jnp.int32, (nrows, 1), 0)
    return 1.0 / jnp.minimum(t + 1, window).astype(F32)


def _odd_fwd(h, conv_w, pool_w, pool_scale, name):
    S = h.shape[0]
    C = conv_w.shape[1]
    G = len(POOL_WINDOWS)
    Dg = C // G
    H = HALO3
    tr = _pick(S, 256, H)

    def body(cb_ref, ccp_ref, cc_ref, chp_ref, ch_ref, pp_ref, p_ref, w_ref, pw_ref, sc_ref, mix_ref, M, P):
        i = pl.program_id(1)
        M[0:H, :] = jnp.where(i > 0, ccp_ref[...].astype(F32) * chp_ref[...].astype(F32), 0.0)
        M[H:H + tr, :] = cc_ref[...].astype(F32) * ch_ref[...].astype(F32)
        cm = (w_ref[pl.ds(0, 1), :] * M[pl.ds(H - 2, tr), :] + w_ref[pl.ds(1, 1), :] * M[pl.ds(H - 1, tr), :]
              + w_ref[pl.ds(2, 1), :] * M[pl.ds(H, tr), :])
        mix_ref[:, 0:C] = (cb_ref[...].astype(F32) * cm).astype(BF16)
        P[0:H, :] = jnp.where(i > 0, pp_ref[...].astype(F32), 0.0)
        P[H:H + tr, :] = p_ref[...].astype(F32)
        for gi, window in enumerate(POOL_WINDOWS):
            cols = pl.ds(gi * Dg, Dg)
            wsum = P[pl.ds(H, tr), cols]
            for dlt in range(1, window):
                wsum = wsum + P[pl.ds(H - dlt, tr), cols]
            diff = wsum * _pool_inv_count(i * tr, tr, window) - P[pl.ds(H, tr), cols]
            yd = jnp.dot(diff.astype(BF16), pw_ref[gi], preferred_element_type=F32) * sc_ref[:, cols]
            mix_ref[:, pl.ds(C + gi * Dg, Dg)] = yd.astype(BF16)

    col = lambda k: (lambda c: k)
    return pl.pallas_call(
        body, name=name, out_shape=jax.ShapeDtypeStruct((S, 2 * C), BF16),
        grid=(1, S // tr),
        in_specs=[_cur_spec(tr, C, col(0)),
                  _prev_spec(tr, H, C, col(1)), _cur_spec(tr, C, col(1)),
                  _prev_spec(tr, H, C, col(2)), _cur_spec(tr, C, col(2)),
                  _prev_spec(tr, H, C, col(3)), _cur_spec(tr, C, col(3)),
                  pl.BlockSpec((3, C), lambda c, i: (0, 0)),
                  pl.BlockSpec((G, Dg, Dg), lambda c, i: (0, 0, 0)),
                  pl.BlockSpec((1, C), lambda c, i: (0, 0))],
        out_specs=pl.BlockSpec((tr, 2 * C), lambda c, i: (i, 0)),
        scratch_shapes=[pltpu.VMEM((H + tr, C), F32), pltpu.VMEM((H + tr, C), F32)],
        compiler_params=_params("parallel", "parallel"),
    )(h, h, h, h, h, h, h, conv_w, pool_w, pool_scale)


def _odd_bwd(dmix, h, conv_w, pool_w, pool_scale, name):
    S = h.shape[0]
    C = conv_w.shape[1]
    G = len(POOL_WINDOWS)
    Dg = C // G
    H = HALO3
    tr = _pick(S, 256, H)
    nr = S // tr
    n = tr + H
    nt_dims = (((1,), (1,)), ((), ()))
    tn_dims = (((0,), (0,)), ((), ()))

    def body(dyc_ref, dycn_ref, dyd_ref, dydn_ref, cb_ref, cbn_ref, ccp_ref, cc_ref, ccn_ref,
             chp_ref, ch_ref, chn_ref, pp_ref, p_ref, w_ref, pw_ref, sc_ref,
             dh_ref, dw_ref, dpw_ref, dsc_ref, M, DCM, P, Q):
        i = pl.program_id(1)
        rows = lax.broadcasted_iota(jnp.int32, (n, 1), 0)
        valid = (rows < tr) | (i < nr - 1)

        @pl.when(i == 0)
        def _():
            dw_ref[...] = jnp.zeros_like(dw_ref)
            dpw_ref[...] = jnp.zeros_like(dpw_ref)
            dsc_ref[...] = jnp.zeros_like(dsc_ref)

        M[0:H, :] = jnp.where(i > 0, ccp_ref[...].astype(F32) * chp_ref[...].astype(F32), 0.0)
        cc = cc_ref[...].astype(F32)
        ch = ch_ref[...].astype(F32)
        M[H:H + tr, :] = cc * ch
        M[H + tr:H + tr + H, :] = ccn_ref[...].astype(F32) * chn_ref[...].astype(F32)
        w0, w1, w2 = w_ref[pl.ds(0, 1), :], w_ref[pl.ds(1, 1), :], w_ref[pl.ds(2, 1), :]
        cm = w0 * M[pl.ds(H - 2, tr), :] + w1 * M[pl.ds(H - 1, tr), :] + w2 * M[pl.ds(H, tr), :]
        dyc = jnp.concatenate([dyc_ref[...], dycn_ref[...]], axis=0).astype(F32)
        dyc = jnp.where(valid, dyc, 0.0)
        cbf = jnp.concatenate([cb_ref[...], cbn_ref[...]], axis=0).astype(F32)
        dh_ref[:, 0:C] = (dyc[0:tr] * cm).astype(BF16)
        DCM[...] = dyc * cbf
        dm = w2 * DCM[pl.ds(0, tr), :] + w1 * DCM[pl.ds(1, tr), :] + w0 * DCM[pl.ds(2, tr), :]
        dh_ref[:, C:2 * C] = (dm * ch).astype(BF16)
        dh_ref[:, 2 * C:3 * C] = (dm * cc).astype(BF16)
        dcur = DCM[pl.ds(0, tr), :]
        for k in range(3):
            dw_ref[pl.ds(k, 1), :] += jnp.sum(dcur * M[pl.ds(H - 2 + k, tr), :], axis=0, keepdims=True)

        P[0:H, :] = jnp.where(i > 0, pp_ref[...].astype(F32), 0.0)
        P[H:H + tr, :] = p_ref[...].astype(F32)
        dyd = jnp.concatenate([dyd_ref[...], dydn_ref[...]], axis=0).astype(F32)
        dyd = jnp.where(valid, dyd, 0.0)
        for gi, window in enumerate(POOL_WINDOWS):
            cols = pl.ds(gi * Dg, Dg)
            lo = gi * Dg
            wsum = P[pl.ds(H, tr), cols]
            for dlt in range(1, window):
                wsum = wsum + P[pl.ds(H - dlt, tr), cols]
            diff = (wsum * _pool_inv_count(i * tr, tr, window) - P[pl.ds(H, tr), cols]).astype(BF16)
            pw = pw_ref[gi]
            dyd_g = dyd[:, lo:lo + Dg]
            e = (dyd_g * sc_ref[:, cols]).astype(BF16)
            yraw = jnp.dot(diff, pw, preferred_element_type=F32)
            dsc_ref[:, cols] += jnp.sum(dyd_g[0:tr] * yraw, axis=0, keepdims=True)
            dpw_ref[gi] += lax.dot_general(diff, e[0:tr], tn_dims, preferred_element_type=F32)
            ddiff = lax.dot_general(e, pw, nt_dims, preferred_element_type=F32)
            Q[:, cols] = ddiff * _pool_inv_count(i * tr, n, window)
            acc = Q[pl.ds(0, tr), cols]
            for dlt in range(1, window):
                acc = acc + Q[pl.ds(dlt, tr), cols]
            dh_ref[:, pl.ds(3 * C + lo, Dg)] = (acc - ddiff[0:tr]).astype(BF16)

    col = lambda k: (lambda c: k)
    return pl.pallas_call(
        body, name=name,
        out_shape=(jax.ShapeDtypeStruct((S, 4 * C), BF16), jax.ShapeDtypeStruct((3, C), F32),
                   jax.ShapeDtypeStruct((G, Dg, Dg), F32), jax.ShapeDtypeStruct((1, C), F32)),
        grid=(1, nr),
        in_specs=[_cur_spec(tr, C, col(0)), _next_spec(tr, H, C, col(0), S),
                  _cur_spec(tr, C, col(1)), _next_spec(tr, H, C, col(1), S),
                  _cur_spec(tr, C, col(0)), _next_spec(tr, H, C, col(0), S),
                  _prev_spec(tr, H, C, col(1)), _cur_spec(tr, C, col(1)), _next_spec(tr, H, C, col(1), S),
                  _prev_spec(tr, H, C, col(2)), _cur_spec(tr, C, col(2)), _next_spec(tr, H, C, col(2), S),
                  _prev_spec(tr, H, C, col(3)), _cur_spec(tr, C, col(3)),
                  pl.BlockSpec((3, C), lambda c, i: (0, 0)),
                  pl.BlockSpec((G, Dg, Dg), lambda c, i: (0, 0, 0)),
                  pl.BlockSpec((1, C), lambda c, i: (0, 0))],
        out_specs=(pl.BlockSpec((tr, 4 * C), lambda c, i: (i, 0)),
                   pl.BlockSpec((3, C), lambda c, i: (0, 0)),
                   pl.BlockSpec((G, Dg, Dg), lambda c, i: (0, 0, 0)),
                   pl.BlockSpec((1, C), lambda c, i: (0, 0))),
        scratch_shapes=[pltpu.VMEM((H + tr + H, C), F32), pltpu.VMEM((n, C), F32),
                        pltpu.VMEM((H + tr, C), F32), pltpu.VMEM((n, C), F32)],
        compiler_params=_params("arbitrary", "arbitrary"),
    )(dmix, dmix, dmix, dmix, h, h, h, h, h, h, h, h, h, h, conv_w, pool_w, pool_scale)


def _adamw(w, grads, m, v, name):
    L, R, C = w.shape
    assert len(grads) == L
    tr, tc = _pick(R, 256, 8), _pick(C, 1408)
    ni, nj = R // tr, C // tc
    c1 = 1.0 / (1.0 - ADAM_B1 ** ADAM_STEP)
    c2 = 1.0 / (1.0 - ADAM_B2 ** ADAM_STEP)

    def g_spec(layer):
        def idx(l, i, j):
            before, after = l < layer, l > layer
            return (jnp.where(before, 0, jnp.where(after, ni - 1, i)),
                    jnp.where(before, 0, jnp.where(after, nj - 1, j)))
        return pl.BlockSpec((tr, tc), idx)

    def body(w_ref, *rest):
        g_refs = rest[:L]
        m_ref, v_ref, go_ref, d_ref, mo_ref, vo_ref = rest[L:]
        l = pl.program_id(0)
        gg = g_refs[0][...]
        for k in range(1, L):
            gg = jnp.where(l == k, g_refs[k][...], gg)
        mn = ADAM_B1 * m_ref[...] + (1.0 - ADAM_B1) * gg
        vn = ADAM_B2 * v_ref[...] + (1.0 - ADAM_B2) * (gg * gg)
        d_ref[...] = -ADAM_LR * ((mn * c1) / (jnp.sqrt(vn * c2) + ADAM_EPS) + ADAM_WD * w_ref[...])
        go_ref[...] = gg
        mo_ref[...] = mn
        vo_ref[...] = vn

    blk = pl.BlockSpec((None, tr, tc), lambda l, i, j: (l, i, j))
    sds = jax.ShapeDtypeStruct(w.shape, F32)
    return pl.pallas_call(
        body, name=name, out_shape=(sds, sds, sds, sds),
        grid=(L, R // tr, C // tc),
        in_specs=[blk] + [g_spec(k) for k in range(L)] + [blk, blk], out_specs=(blk, blk, blk, blk),
        compiler_params=_params("arbitrary", "arbitrary", "arbitrary"),
    )(w, *grads, m, v)


def _sum_slots(buf, name):
    N, R, C = buf.shape
    tr = _pick(R, 512, 8)

    def body(b_ref, o_ref):
        acc = b_ref[0]
        for k in range(1, N):
            acc = acc + b_ref[k]
        o_ref[...] = acc

    return pl.pallas_call(
        body, name=name, out_shape=jax.ShapeDtypeStruct((R, C), F32),
        grid=(R // tr,),
        in_specs=[pl.BlockSpec((N, tr, C), lambda i: (0, i, 0))],
        out_specs=pl.BlockSpec((tr, C), lambda i: (i, 0)),
        compiler_params=_params("parallel"),
    )(buf)


def _pair_sum(pos, g, rsib, name):
    _, hr, hc = rsib.shape
    tr, tc = _pick(hr, 512, 16), _pick(hc, 1024)

    def body(p_ref, g_ref, r_ref, o_ref):
        o_ref[...] = (g_ref[...].astype(F32) + r_ref[...].astype(F32)).astype(BF16)

    blk = pl.BlockSpec((None, tr, tc), lambda s, i, j, p: (s, i, j))
    return pl.pallas_call(
        body, name=name, out_shape=jax.ShapeDtypeStruct(rsib.shape, BF16),
        grid_spec=pltpu.PrefetchScalarGridSpec(
            num_scalar_prefetch=1, grid=(N_CHIPS, hr // tr, hc // tc),
            in_specs=[pl.BlockSpec((None, None, tr, tc), lambda s, i, j, p: (s, p[1], i, j)), blk],
            out_specs=blk),
        compiler_params=_params("parallel", "parallel", "parallel"),
    )(pos, g, rsib)


def _chip_sum(pos, part, land, name):
    _, sr, sc = land.shape
    tr, tc = _pick(sr, 512, 16), _pick(sc, 1024)

    def body(p_ref, own_ref, l_ref, o_ref):
        acc = own_ref[...].astype(F32)
        for k in range(3):
            acc = acc + l_ref[k].astype(F32)
        o_ref[...] = acc

    return pl.pallas_call(
        body, name=name, out_shape=jax.ShapeDtypeStruct((2, sr, sc), F32),
        grid_spec=pltpu.PrefetchScalarGridSpec(
            num_scalar_prefetch=1, grid=(sr // tr, sc // tc),
            in_specs=[pl.BlockSpec((None, tr, tc), lambda i, j, p: (p[0], i, j)),
                      pl.BlockSpec((3, tr, tc), lambda i, j, p: (0, i, j))],
            out_specs=pl.BlockSpec((None, tr, tc), lambda i, j, p: (p[1], i, j))),
        compiler_params=_params("parallel", "parallel"),
    )(pos, part, land)


def _place():
    x, y, c = lax.axis_index("x"), lax.axis_index("y"), lax.axis_index("c")
    return x, y, c


def _other_chips(x, y):
    return [(1 - x, y), (x, 1 - y), (1 - x, 1 - y)]


def _rcopy(src, dst, ssem, rsem, dev):
    return pltpu.make_async_remote_copy(src_ref=src, dst_ref=dst, send_sem=ssem, recv_sem=rsem,
                                        device_id=dev, device_id_type=MESH)


def _gather_items(bufs):
    return sum(b.shape[0] for b in bufs)


def _gather_walk(full):
    t = 0
    for ref in full:
        for l in range(ref.shape[0]):
            yield t, ref, l
            t += 1


def _gather_start(full, ssem, rsem):
    x, y, c = _place()
    j = 2 * x + y
    for t, ref, l in _gather_walk(full):
        own = ref.at[l, j, c]
        for r, (px, py) in enumerate(_other_chips(x, y)):
            _rcopy(own, own, ssem.at[6 * t + r], rsem.at[6 * t + r], (px, py, c)).start()


def _gather_forward(full, ssem, rsem):
    x, y, c = _place()
    for t, ref, l in _gather_walk(full):
        for r, (px, py) in enumerate(_other_chips(x, y)):
            slab = ref.at[l, 2 * px + py, c]
            _rcopy(slab, slab, ssem.at[6 * t + r], rsem.at[6 * t + r], (px, py, c)).wait_recv()
            _rcopy(slab, slab, ssem.at[6 * t + 3 + r], rsem.at[6 * t + 3 + r], (x, y, 1 - c)).start()


def _gather_finish(full, ssem, rsem):
    x, y, c = _place()
    j = 2 * x + y
    for t, ref, l in _gather_walk(full):
        for r, (px, py) in enumerate(_other_chips(x, y)):
            got = ref.at[l, 2 * px + py, 1 - c]
            _rcopy(got, got, ssem.at[6 * t + 3 + r], rsem.at[6 * t + 3 + r], (x, y, 1 - c)).wait_recv()
    for t, ref, l in _gather_walk(full):
        own = ref.at[l, j, c]
        for r, (px, py) in enumerate(_other_chips(x, y)):
            _rcopy(own, own, ssem.at[6 * t + r], rsem.at[6 * t + r], (px, py, c)).wait_send()
            slab = ref.at[l, 2 * px + py, c]
            _rcopy(slab, slab, ssem.at[6 * t + 3 + r], rsem.at[6 * t + 3 + r], (x, y, 1 - c)).wait_send()


def _land_shape(part):
    return jax.ShapeDtypeStruct((3,) + part.shape[1:], part.dtype)


def _exchange_start(parts, land, ssem, rsem):
    x, y, c = _place()
    for a in range(len(parts)):
        for r, (px, py) in enumerate(_other_chips(x, y)):
            _rcopy(parts[a].at[2 * px + py], land[a].at[r], ssem.at[3 * a + r], rsem.at[3 * a + r],
                   (px, py, c)).start()


def _exchange_finish(parts, land, ssem, rsem):
    x, y, c = _place()
    for a in range(len(parts)):
        for r, (px, py) in enumerate(_other_chips(x, y)):
            _rcopy(parts[a].at[2 * px + py], land[a].at[r], ssem.at[3 * a + r], rsem.at[3 * a + r],
                   (px, py, c)).wait()


_Hosted = collections.namedtuple("_Hosted", "ins outs alias sems start finish")


def _host_exchange(parts):
    n = len(parts)
    return _Hosted(list(parts), [_land_shape(p) for p in parts], {}, [3 * n, 3 * n],
                   lambda ins, outs, sems: _exchange_start(ins, outs, *sems),
                   lambda ins, outs, sems: _exchange_finish(ins, outs, *sems))


def _host_gather(bufs):
    T = _gather_items(bufs)

    def finish(ins, outs, sems):
        _gather_forward(outs, *sems)
        _gather_finish(outs, *sems)

    return _Hosted(list(bufs), [jax.ShapeDtypeStruct(b.shape, b.dtype) for b in bufs],
                   {a: a for a in range(len(bufs))}, [6 * T, 6 * T],
                   lambda ins, outs, sems: _gather_start(outs, *sems), finish)


def _host_all_devices(buf):
    return _Hosted([buf], [jax.ShapeDtypeStruct((N_DEV,) + buf.shape, buf.dtype)], {}, [N_DEV - 1, N_DEV - 1, 1],
                   lambda ins, outs, sems: _all_devices_start(ins[0], outs[0], *sems),
                   lambda ins, outs, sems: _all_devices_finish(ins[0], outs[0], *sems))


def _allgather_big(bufs, name):
    n = len(bufs)
    T = _gather_items(bufs)

    def body(*refs):
        full = refs[n:2 * n]
        ssem, rsem = refs[2 * n:]
        _gather_start(full, ssem, rsem)
        _gather_forward(full, ssem, rsem)
        _gather_finish(full, ssem, rsem)

    return pl.pallas_call(
        body, name=name, out_shape=tuple(jax.ShapeDtypeStruct(b.shape, BF16) for b in bufs),
        in_specs=[ANY] * n, out_specs=tuple([ANY] * n),
        input_output_aliases={a: a for a in range(n)},
        scratch_shapes=[pltpu.SemaphoreType.DMA((6 * T,)), pltpu.SemaphoreType.DMA((6 * T,))],
    )(*bufs)


def _allgather_small(shards, name):
    n = len(shards)
    outs = tuple(jax.ShapeDtypeStruct((N_CHIPS,) + s.shape, s.dtype) for s in shards)

    def body(*refs):
        ins, full = refs[:n], refs[n:2 * n]
        ssem, rsem, lsem = refs[2 * n:]
        x, y, c = _place()
        j = 2 * x + y
        chips = _other_chips(x, y)
        cps, locs = [], []
        for a in range(n):
            loc = pltpu.make_async_copy(ins[a], full[a].at[j], lsem.at[a])
            loc.start()
            locs.append(loc)
            for r, (px, py) in enumerate(chips):
                cp = _rcopy(ins[a], full[a].at[j], ssem.at[3 * a + r], rsem.at[3 * a + r], (px, py, c))
                cp.start()
                cps.append(cp)
        for a in range(n):
            for r, (px, py) in enumerate(chips):
                dst = full[a].at[2 * px + py]
                _rcopy(dst, dst, ssem.at[3 * a + r], rsem.at[3 * a + r], (px, py, c)).wait_recv()
        for cp in cps:
            cp.wait_send()
        for loc in locs:
            loc.wait()

    return pl.pallas_call(
        body, name=name, out_shape=outs,
        in_specs=[ANY] * n, out_specs=tuple([ANY] * n),
        scratch_shapes=[pltpu.SemaphoreType.DMA((3 * n,)), pltpu.SemaphoreType.DMA((3 * n,)),
                        pltpu.SemaphoreType.DMA((n,))],
    )(*shards)


def _pair_exchange(grads, name):
    n = len(grads)
    outs = [jax.ShapeDtypeStruct((N_CHIPS,) + g.shape[2:], BF16) for g in grads]

    def body(*refs):
        ins, got = refs[:n], refs[n:2 * n]
        ssem, rsem = refs[2 * n:]
        x, y, c = _place()
        cps = []
        for a in range(n):
            for s in range(N_CHIPS):
                cp = _rcopy(ins[a].at[s, 1 - c], got[a].at[s], ssem.at[N_CHIPS * a + s],
                            rsem.at[N_CHIPS * a + s], (x, y, 1 - c))
                cp.start()
                cps.append(cp)
        for cp in cps:
            cp.wait()

    return pl.pallas_call(
        body, name=name, out_shape=tuple(outs),
        in_specs=[ANY] * n, out_specs=tuple([ANY] * n),
        scratch_shapes=[pltpu.SemaphoreType.DMA((N_CHIPS * n,)), pltpu.SemaphoreType.DMA((N_CHIPS * n,))],
    )(*grads)


def _chip_exchange(parts, name):
    n = len(parts)

    def body(*refs):
        ins, land = refs[:n], refs[n:2 * n]
        ssem, rsem = refs[2 * n:]
        _exchange_start(ins, land, ssem, rsem)
        _exchange_finish(ins, land, ssem, rsem)

    return pl.pallas_call(
        body, name=name, out_shape=tuple(_land_shape(p) for p in parts),
        in_specs=[ANY] * n, out_specs=tuple([ANY] * n),
        scratch_shapes=[pltpu.SemaphoreType.DMA((3 * n,)), pltpu.SemaphoreType.DMA((3 * n,))],
    )(*parts)


def _half_swap(bufs, name):
    n = len(bufs)

    def body(*refs):
        full = refs[n:2 * n]
        ssem, rsem = refs[2 * n:]
        x, y, c = _place()
        cps = []
        for t in range(n):
            mine = full[t].at[c]
            cp = _rcopy(mine, mine, ssem.at[t], rsem.at[t], (x, y, 1 - c))
            cp.start()
            cps.append(cp)
        for t in range(n):
            got = full[t].at[1 - c]
            _rcopy(got, got, ssem.at[t], rsem.at[t], (x, y, 1 - c)).wait_recv()
        for cp in cps:
            cp.wait_send()

    return pl.pallas_call(
        body, name=name, out_shape=tuple(jax.ShapeDtypeStruct(b.shape, F32) for b in bufs),
        in_specs=[ANY] * n, out_specs=tuple([ANY] * n),
        input_output_aliases={a: a for a in range(n)},
        scratch_shapes=[pltpu.SemaphoreType.DMA((n,)), pltpu.SemaphoreType.DMA((n,))],
    )(*bufs)


def _flipped(x, y, c, m):
    fx, fy, fc = (m >> 2) & 1, (m >> 1) & 1, m & 1
    return x + fx - 2 * x * fx, y + fy - 2 * y * fy, c + fc - 2 * c * fc


def _all_devices_start(b_ref, o_ref, ssem, rsem, lsem):
    x, y, c = _place()
    me = 4 * x + 2 * y + c
    pltpu.make_async_copy(b_ref, o_ref.at[me], lsem.at[0]).start()
    for m in range(1, N_DEV):
        _rcopy(b_ref, o_ref.at[me], ssem.at[m - 1], rsem.at[m - 1], _flipped(x, y, c, m)).start()


def _all_devices_finish(b_ref, o_ref, ssem, rsem, lsem):
    x, y, c = _place()
    me = 4 * x + 2 * y + c
    for m in range(1, N_DEV):
        px, py, pc = _flipped(x, y, c, m)
        got = o_ref.at[4 * px + 2 * py + pc]
        _rcopy(got, got, ssem.at[m - 1], rsem.at[m - 1], (px, py, pc)).wait_recv()
    for m in range(1, N_DEV):
        _rcopy(b_ref, o_ref.at[me], ssem.at[m - 1], rsem.at[m - 1], _flipped(x, y, c, m)).wait_send()
    pltpu.make_async_copy(b_ref, o_ref.at[me], lsem.at[0]).wait()


def _pack(arrs):
    flat = jnp.concatenate([a.reshape(-1) for a in arrs])
    rows = -(-flat.shape[0] // (8 * LANES)) * 8
    flat = jnp.pad(flat, (0, rows * LANES - flat.shape[0]))
    return flat.reshape(rows, LANES)


def _unpack(buf, shapes):
    flat = buf.reshape(-1)
    out, off = [], 0
    for s in shapes:
        size = 1
        for d in s:
            size *= d
        out.append(flat[off:off + size].reshape(s))
        off += size
    return out


BIG = ("ev_w_in", "ev_w_out", "od_w_in", "od_w_out", "ffn_w_up", "ffn_w_down")
BIG_KIND = {"ev_w_in": "col", "ev_w_out": "row", "od_w_in": "col", "od_w_out": "row",
            "ffn_w_up": "col", "ffn_w_down": "row"}
SMALL_AXIS = {"ev_dw_w": 2, "ev_dw_b": None, "ev_bn_g": None, "ev_bn_b": None, "od_conv_w": 2,
              "od_pool_w": 2, "od_pool_scale": 1, "ffn_conv_w": 2, "ffn_conv_b": None, "ln_g": 2, "ln_b": 2}
WEIGHTS = ("ev_w_in", "ev_dw_w", "ev_dw_b", "ev_bn_g", "ev_bn_b", "ev_w_out", "od_w_in", "od_conv_w",
           "od_pool_w", "od_pool_scale", "od_w_out", "ffn_w_up", "ffn_conv_w", "ffn_conv_b", "ffn_w_down",
           "ln_g", "ln_b")


def _ffn_fwd(xb, w_up, w_down, conv_w, conv_b, tag, host_up=None, host_down=None):
    hu = _matmul(xb, w_up, mode="nn", b_lead=0, b_split=True, out_dtype=BF16, name=f"{tag}_up", tm=1024, tn=1408,
                 hosted=host_up)
    hu, up_outs = hu if host_up is not None else (hu, None)
    z = _ffn_act_fwd(hu, conv_w, conv_b, name=f"{tag}_act")
    y = _matmul(z, w_down, mode="nn", b_lead=0, out_dtype=F32, name=f"{tag}_down", tm=512, tn=1024,
                hosted=host_down)
    y, down_outs = y if host_down is not None else (y, None)
    return hu, z, y, up_outs, down_outs


def _ffn_bwd(drb, dr, alpha, xb, hu, z, w_up, w_down, conv_w, conv_b, tag, hosted=None):
    g_down = _matmul(z, drb, mode="tn", out_dtype=BF16, name=f"{tag}_dwdown", tm=512, tn=1024)
    dz = _matmul(drb, w_down, mode="nt", b_lead=0, out_dtype=BF16, name=f"{tag}_dz", tm=1024, tn=1408)
    dg, du, dcw, dcb = _ffn_act_bwd(dz, hu, conv_w, conv_b, name=f"{tag}_actbwd")
    g_up = _matmul(xb, (dg, du), mode="tn", out_split=True, out_dtype=BF16, name=f"{tag}_dwup", tm=512, tn=1408)
    dx = _matmul((dg, du), w_up, mode="nt", b_lead=0, b_split=True, out_dtype=F32, add=dr, add_scale=alpha,
                 name=f"{tag}_dx", tm=512, tn=512, hosted=hosted)
    dx, outs = dx if hosted is not None else (dx, None)
    return dx, g_up, g_down, dcw, dcb, outs


def kernel(x, ev_w_in, ev_dw_w, ev_dw_b, ev_bn_g, ev_bn_b, ev_w_out, od_w_in, od_conv_w, od_pool_w, od_pool_scale, od_w_out, ffn_w_up, ffn_conv_w, ffn_conv_b, ffn_w_down, ln_g, ln_b, loss_target, m_ev_w_in, m_ev_dw_w, m_ev_dw_b, m_ev_bn_g, m_ev_bn_b, m_ev_w_out, m_od_w_in, m_od_conv_w, m_od_pool_w, m_od_pool_scale, m_od_w_out, m_ffn_w_up, m_ffn_conv_w, m_ffn_conv_b, m_ffn_w_down, m_ln_g, m_ln_b, v_ev_w_in, v_ev_dw_w, v_ev_dw_b, v_ev_bn_g, v_ev_bn_b, v_ev_w_out, v_od_w_in, v_od_conv_w, v_od_pool_w, v_od_pool_scale, v_od_w_out, v_ffn_w_up, v_ffn_conv_w, v_ffn_conv_b, v_ffn_w_down, v_ln_g, v_ln_b):
    wts = dict(ev_w_in=ev_w_in, ev_dw_w=ev_dw_w, ev_dw_b=ev_dw_b, ev_bn_g=ev_bn_g, ev_bn_b=ev_bn_b,
               ev_w_out=ev_w_out, od_w_in=od_w_in, od_conv_w=od_conv_w, od_pool_w=od_pool_w,
               od_pool_scale=od_pool_scale, od_w_out=od_w_out, ffn_w_up=ffn_w_up, ffn_conv_w=ffn_conv_w,
               ffn_conv_b=ffn_conv_b, ffn_w_down=ffn_w_down, ln_g=ln_g, ln_b=ln_b)
    mom = dict(ev_w_in=m_ev_w_in, ev_dw_w=m_ev_dw_w, ev_dw_b=m_ev_dw_b, ev_bn_g=m_ev_bn_g, ev_bn_b=m_ev_bn_b,
               ev_w_out=m_ev_w_out, od_w_in=m_od_w_in, od_conv_w=m_od_conv_w, od_pool_w=m_od_pool_w,
               od_pool_scale=m_od_pool_scale, od_w_out=m_od_w_out, ffn_w_up=m_ffn_w_up, ffn_conv_w=m_ffn_conv_w,
               ffn_conv_b=m_ffn_conv_b, ffn_w_down=m_ffn_w_down, ln_g=m_ln_g, ln_b=m_ln_b)
    var = dict(ev_w_in=v_ev_w_in, ev_dw_w=v_ev_dw_w, ev_dw_b=v_ev_dw_b, ev_bn_g=v_ev_bn_g, ev_bn_b=v_ev_bn_b,
               ev_w_out=v_ev_w_out, od_w_in=v_od_w_in, od_conv_w=v_od_conv_w, od_pool_w=v_od_pool_w,
               od_pool_scale=v_od_pool_scale, od_w_out=v_od_w_out, ffn_w_up=v_ffn_w_up, ffn_conv_w=v_ffn_conv_w,
               ffn_conv_b=v_ffn_conv_b, ffn_w_down=v_ffn_w_down, ln_g=v_ln_g, ln_b=v_ln_b)

    S, D = x.shape[1], x.shape[2]
    depth = ln_g.shape[0]
    alpha = (2.0 * depth) ** 0.25
    A = ev_dw_b.shape[-1]
    n_heads = A // HEAD_DIM
    xi, yi, ci = _place()
    chip = 2 * xi + yi
    pos = jnp.stack([chip, ci]).astype(jnp.int32)

    bufs = {f"{k}{l}": _cast_into_gather(pos, wts[k], l, name=f"cast_{k}{l}")
            for k in BIG for l in range(wts[k].shape[0])}

    def whole(key):
        _, r, c = wts[key[:-1]].shape
        col = BIG_KIND[key[:-1]] == "col"
        return bufs[key].reshape(1, N_CHIPS, r, c) if col else bufs[key].reshape(1, N_CHIPS * r, c)

    full = {}

    def gathered_now(keys, arrays):
        bufs.update(zip(keys, arrays))
        full.update({key: whole(key) for key in keys})

    early = ("ev_w_in0", "ev_w_out0")
    under_attn = ("ffn_w_up0", "ffn_w_down0", "od_w_in0", "od_w_out0")
    gathered_now(early, _allgather_big([bufs[k] for k in early], name="gather_first"))
    small_sharded = [k for k in WEIGHTS if k not in BIG and SMALL_AXIS[k] is not None]
    gathered = _allgather_small([wts[k] for k in small_sharded], name="gather_small")
    sm = {k: wts[k] for k in WEIGHTS if k not in BIG and SMALL_AXIS[k] is None}
    for k, g4 in zip(small_sharded, gathered):
        sm[k] = jnp.concatenate([g4[t] for t in range(N_CHIPS)], axis=SMALL_AXIS[k])
    pool_w_bf = sm["od_pool_w"][0].astype(BF16)

    x0 = x[0]
    x0b = _cast_bf16(x, name="cast_x")[0]
    h0 = _matmul(x0b, full["ev_w_in0"], mode="nn", b_lead=0, b_split=True, out_dtype=BF16, name="ev_in",
                 tm=1024, tn=1280)
    o_a, tot, *rest = _attn_fwd(h0, n_heads, name="attn_fwd", gather=[bufs[k] for k in under_attn])
    gathered_now(under_attn, rest)
    u1, u3 = _evenconv_fwd(h0, sm["ev_dw_w"][0], sm["ev_dw_b"], sm["ev_bn_g"], sm["ev_bn_b"], name="evconv_fwd")
    mix0 = jnp.concatenate([o_a, u3], axis=1)
    y1 = _matmul(mix0, full["ev_w_out0"], mode="nn", b_lead=0, out_dtype=F32, name="ev_out", tm=1024, tn=1024)
    x1, x1b, xh1, rs1 = _ln_fwd(x0, y1, sm["ln_g"][0, 0][None], sm["ln_b"][0, 0][None], alpha, name="ln00")
    hu0, z0, y2, got_up, got_down = _ffn_fwd(
        x1b, full["ffn_w_up0"], full["ffn_w_down0"], sm["ffn_conv_w"][0], sm["ffn_conv_b"][0][None], "ffn0",
        host_up=_host_gather([bufs["ffn_w_up1"]]), host_down=_host_gather([bufs["ffn_w_down1"]]))
    gathered_now(("ffn_w_up1",), got_up)
    gathered_now(("ffn_w_down1",), got_down)
    x2, x2b, xh2, rs2 = _ln_fwd(x1, y2, sm["ln_g"][0, 1][None], sm["ln_b"][0, 1][None], alpha, name="ln01")
    h1 = _matmul(x2b, full["od_w_in0"], mode="nn", b_lead=0, b_split=True, out_dtype=BF16, name="od_in",
                 tm=1024, tn=1024)
    mix1 = _odd_fwd(h1, sm["od_conv_w"][0], pool_w_bf, sm["od_pool_scale"], name="odd_fwd")
    y3 = _matmul(mix1, full["od_w_out0"], mode="nn", b_lead=0, out_dtype=F32, name="od_out", tm=1024, tn=1024)
    x3, x3b, xh3, rs3 = _ln_fwd(x2, y3, sm["ln_g"][1, 0][None], sm["ln_b"][1, 0][None], alpha, name="ln10")
    hu1, z1, y4, _, _ = _ffn_fwd(x3b, full["ffn_w_up1"], full["ffn_w_down1"], sm["ffn_conv_w"][1],
                                 sm["ffn_conv_b"][1][None], "ffn1")
    x4, _, xh4, rs4 = _ln_fwd(x3, y4, sm["ln_g"][1, 1][None], sm["ln_b"][1, 1][None], alpha, name="ln11")

    dx4, loss_part = _loss_grad(x4, loss_target[0], name="loss")
    loss = lax.psum(loss_part[0, 0], ("x", "y", "c"))

    def pair_reduce(named, tag):
        g4 = []
        for k, g in named:
            rows, cols = (g.shape[1], g.shape[2]) if BIG_KIND[k] == "col" else (g.shape[0] // N_CHIPS, g.shape[1])
            g4.append(g.reshape(N_CHIPS, 2, rows // 2, cols))
        sib = _pair_exchange(g4, name=f"grad_pair_exchange_{tag}")
        return [_pair_sum(pos, g, r, name=f"grad_pair_sum_{tag}{t}") for t, (g, r) in enumerate(zip(g4, sib))]

    dr4, dr4b, dg11, db11 = _ln_bwd(dx4, xh4, rs4, sm["ln_g"][1, 1][None], name="ln11_bwd")
    dx3, g_up1, g_down1, dcw1, dcb1, _ = _ffn_bwd(dr4b, dr4, alpha, x3b, hu1, z1, full["ffn_w_up1"],
                                                  full["ffn_w_down1"], sm["ffn_conv_w"][1],
                                                  sm["ffn_conv_b"][1][None], "ffn1")
    parts_f1 = pair_reduce([("ffn_w_up", g_up1), ("ffn_w_down", g_down1)], "f1")
    dr3, dr3b, dg10, db10 = _ln_bwd(dx3, xh3, rs3, sm["ln_g"][1, 0][None], name="ln10_bwd")
    g_odout = _matmul(mix1, dr3b, mode="tn", out_dtype=BF16, name="od_dwout", tm=512, tn=1024)
    dmix1 = _matmul(dr3b, full["od_w_out0"], mode="nt", b_lead=0, out_dtype=BF16, name="od_dmix", tm=1024, tn=1024)
    dh1, d_odconv, d_pool, d_pscale = _odd_bwd(dmix1, h1, sm["od_conv_w"][0], pool_w_bf, sm["od_pool_scale"],
                                               name="odd_bwd")
    g_odin = _matmul(x2b, dh1, mode="tn", out_split=True, out_dtype=BF16, name="od_dwin", tm=512, tn=1024)
    dx2 = _matmul(dh1, full["od_w_in0"], mode="nt", b_lead=0, b_split=True, out_dtype=F32, add=dr3, add_scale=alpha,
                  name="od_dx", tm=1024, tn=512)
    dr2, dr2b, dg01, db01 = _ln_bwd(dx2, xh2, rs2, sm["ln_g"][0, 1][None], name="ln01_bwd")
    dx1, g_up0, g_down0, dcw0, dcb0, land_f1 = _ffn_bwd(dr2b, dr2, alpha, x1b, hu0, z0, full["ffn_w_up0"],
                                                        full["ffn_w_down0"], sm["ffn_conv_w"][0],
                                                        sm["ffn_conv_b"][0][None], "ffn0",
                                                        hosted=_host_exchange(parts_f1))
    parts_b = pair_reduce([("od_w_in", g_odin), ("od_w_out", g_odout), ("ffn_w_up", g_up0),
                           ("ffn_w_down", g_down0)], "b")
    dr1, dr1b, dg00, db00 = _ln_bwd(dx1, xh1, rs1, sm["ln_g"][0, 0][None], name="ln00_bwd")
    g_evout = _matmul(mix0, dr1b, mode="tn", out_dtype=BF16, name="ev_dwout", tm=512, tn=1024)
    dmix0 = _matmul(dr1b, full["ev_w_out0"], mode="nt", b_lead=0, out_dtype=BF16, name="ev_dmix", tm=1024, tn=1024)
    dq, dk, dv, *land_b = _attn_bwd(h0, dmix0, tot, n_heads, name="attn_bwd", exchange=parts_b)
    da, dgate, d_dww, d_dwb, d_bng, d_bnb = _evenconv_bwd(dmix0, u1, h0, sm["ev_dw_w"][0], sm["ev_bn_g"],
                                                          sm["ev_bn_b"], name="evconv_bwd")
    dh0 = jnp.concatenate([dq, dk, dv, da, dgate], axis=1)

    d_ln_g = jnp.stack([jnp.stack([dg00[0], dg01[0]]), jnp.stack([dg10[0], dg11[0]])])
    d_ln_b = jnp.stack([jnp.stack([db00[0], db01[0]]), jnp.stack([db10[0], db11[0]])])
    small_partial = {
        "ev_dw_w": d_dww[None], "ev_dw_b": d_dwb, "ev_bn_g": d_bng, "ev_bn_b": d_bnb,
        "od_conv_w": d_odconv[None], "od_pool_w": d_pool[None], "od_pool_scale": d_pscale,
        "ffn_conv_w": jnp.stack([dcw0, dcw1]), "ffn_conv_b": jnp.concatenate([dcb0, dcb1], axis=0),
        "ln_g": d_ln_g, "ln_b": d_ln_b}
    small_names = [k for k in WEIGHTS if k not in BIG]
    packed = _pack([small_partial[k] for k in small_names])
    g_evin, (all_small,) = _matmul(x0b, dh0, mode="tn", out_split=True, out_dtype=BF16, name="ev_dwin", tm=512,
                                   tn=1280, hosted=_host_all_devices(packed))
    parts_e = pair_reduce([("ev_w_in", g_evin), ("ev_w_out", g_evout)], "e")
    grad_x, land_e = _matmul(dh0, full["ev_w_in0"], mode="nt", b_lead=0, b_split=True, out_dtype=F32, add=dr1,
                             add_scale=alpha, name="ev_dx", tm=1024, tn=512, hosted=_host_exchange(parts_e))

    order = ["ffn_w_up1", "ffn_w_down1", "od_w_in0", "od_w_out0", "ffn_w_up0", "ffn_w_down0", "ev_w_in0", "ev_w_out0"]
    parts = parts_f1 + parts_b + parts_e
    land = list(land_f1) + list(land_b) + list(land_e)
    halves = [_chip_sum(pos, p, ld, name=f"grad_chip_sum_{tag}") for tag, p, ld in zip(order, parts, land)]
    reduced = dict(zip(order, _half_swap(halves, name="grad_half_swap")))
    big_grads = {k: [reduced[f"{k}{l}"].reshape(wts[k].shape[1:]) for l in range(wts[k].shape[0])] for k in BIG}

    summed = _sum_slots(all_small, name="sum_small_grads")
    small_full = dict(zip(small_names, _unpack(summed, [small_partial[k].shape for k in small_names])))
    small_grads = {}
    for k in small_names:
        ax = SMALL_AXIS[k]
        if ax is None:
            small_grads[k] = small_full[k]
        else:
            size = wts[k].shape[ax]
            small_grads[k] = lax.dynamic_slice_in_dim(small_full[k], chip * size, size, axis=ax)

    grads, delta, new_m, new_v = {}, {}, {}, {}
    for k in BIG:
        grads[k], delta[k], new_m[k], new_v[k] = _adamw(wts[k], big_grads[k], mom[k], var[k], name=f"adamw_{k}")
    shapes = [wts[k].shape for k in small_names]
    pw, pg, pm, pv = (_pack([d[k] for k in small_names]) for d in (wts, small_grads, mom, var))
    sg, sd, smn, svn = _adamw(pw[None], [pg], pm[None], pv[None], name="adamw_small")
    for dst, buf in ((grads, sg), (delta, sd), (new_m, smn), (new_v, svn)):
        for k, a in zip(small_names, _unpack(buf[0], shapes)):
            dst[k] = a

    return (loss, grad_x[None], *[grads[k] for k in WEIGHTS], *[delta[k] for k in WEIGHTS],
            *[new_m[k] for k in WEIGHTS], *[new_v[k] for k in WEIGHTS])
```

```python
import collections

import jax
import jax.numpy as jnp
from jax import lax
from jax.experimental import pallas as pl
from jax.experimental.pallas import tpu as pltpu

F32 = jnp.float32
BF16 = jnp.bfloat16

HEAD_DIM = 128
POOL_WINDOWS = (2, 4, 8, 16)
LN_EPS = 1e-5
ADAM_LR = 0.001
ADAM_B1 = 0.9
ADAM_B2 = 0.999
ADAM_EPS = 1e-08
ADAM_WD = 0.01
ADAM_STEP = 10
N_CHIPS = 4
N_DEV = 8
MESH = pl.DeviceIdType.MESH
LANES = 128
HALO3 = 16
HALO31 = 32

ANY = pl.BlockSpec(memory_space=pl.ANY)


def _pick(n, pref, mult=LANES):
    if n <= pref:
        return n
    t = (pref // mult) * mult
    while t >= mult:
        if n % t == 0:
            return t
        t -= mult
    return n


def _params(*sem):
    return pltpu.CompilerParams(dimension_semantics=sem)


def _matmul(a, b, *, mode, out_dtype, name, b_lead=None, b_split=False, out_split=False, add=None,
            add_scale=1.0, tm=512, tn=1024, tk=None, hosted=None):
    halves = isinstance(a, tuple) or isinstance(b, tuple)
    if isinstance(a, tuple):
        assert mode == "nt" and b_split and tk is None
        ash = (a[0].shape[0], 2 * a[0].shape[1])
    else:
        ash = a.shape[-2:]
    if isinstance(b, tuple):
        assert mode == "tn" and tk is None
        bsh = (b[0].shape[0], 2 * b[0].shape[1])
    else:
        bsh = b.shape[-2:]
    if mode == "nn":
        (M, K), (K2, N) = ash, bsh
        if b_split:
            N = N * N_CHIPS
    elif mode == "nt":
        (M, K), (N, K2) = ash, bsh
        if b_split:
            K2 = K2 * N_CHIPS
    else:
        (K, M), (K2, N) = ash, bsh
    assert K == K2, (ash, bsh, mode)
    tm = _pick(M, tm)
    tn = _pick(N // N_CHIPS if (out_split or (b_split and mode == "nn")) else N, tn)
    whole_split_k = b_split and mode == "nt" and tk is None
    if tk is None:
        tk = K
    else:
        tk = _pick(K // N_CHIPS if (b_split and mode == "nt") else K, tk)
    nk = K // tk
    kq = K // N_CHIPS
    n_per = (N // N_CHIPS) // tn
    k_per = (K // N_CHIPS) // tk

    def lead(shape, idx):
        if b_lead is None:
            return pl.BlockSpec(shape, idx)
        return pl.BlockSpec((None,) + shape, lambda i, j, k: (b_lead,) + idx(i, j, k))

    if mode == "nn":
        a_spec = pl.BlockSpec((tm, tk), lambda i, j, k: (i, k))
        if b_split:
            b_spec = lead((None, tk, tn), lambda i, j, k: (lax.div(j, n_per), k, lax.rem(j, n_per)))
        else:
            b_spec = lead((tk, tn), lambda i, j, k: (k, j))
        dims = (((1,), (0,)), ((), ()))
    elif mode == "nt":
        a_spec = pl.BlockSpec((tm, tk), lambda i, j, k: (i, k))
        if whole_split_k:
            b_spec = lead((N_CHIPS, tn, kq), lambda i, j, k: (0, j, 0))
        elif b_split:
            b_spec = lead((None, tn, tk), lambda i, j, k: (lax.div(k, k_per), j, lax.rem(k, k_per)))
        else:
            b_spec = lead((tn, tk), lambda i, j, k: (j, k))
        dims = (((1,), (1,)), ((), ()))
    else:
        a_spec = pl.BlockSpec((tk, tm), lambda i, j, k: (k, i))
        b_spec = pl.BlockSpec((tk, tn), lambda i, j, k: (k, j))
        dims = (((0,), (0,)), ((), ()))
    if out_split:
        out_shape = jax.ShapeDtypeStruct((N_CHIPS, M, N // N_CHIPS), out_dtype)
        out_spec = pl.BlockSpec((None, tm, tn), lambda i, j, k: (lax.div(j, n_per), i, lax.rem(j, n_per)))
    else:
        out_shape = jax.ShapeDtypeStruct((M, N), out_dtype)
        out_spec = pl.BlockSpec((tm, tn), lambda i, j, k: (i, j))
    grid = (M // tm, N // tn, nk)
    nj_half = grid[1] // 2
    if isinstance(a, tuple):
        in_specs = [pl.BlockSpec((tm, K // 2), lambda i, j, k: (i, 0))] * 2 + [b_spec]
        args = [a[0], a[1], b]
    elif isinstance(b, tuple):
        in_specs = [a_spec,
                    pl.BlockSpec((tk, tn), lambda i, j, k: (k, jnp.minimum(j, nj_half - 1))),
                    pl.BlockSpec((tk, tn), lambda i, j, k: (k, jnp.maximum(j - nj_half, 0)))]
        args = [a, b[0], b[1]]
    else:
        in_specs = [a_spec, b_spec]
        args = [a, b]
    n_op = len(args)
    if add is not None:
        in_specs.append(pl.BlockSpec((tm, tn), lambda i, j, k: (i, j)))
        args.append(add)

    n_in = len(args)
    h_in = 0 if hosted is None else len(hosted.ins)
    h_out = 0 if hosted is None else len(hosted.outs)

    def body(*refs):
        ops = refs[:n_op]
        add_ref = refs[n_op] if add is not None else None
        h_ins = refs[n_in:n_in + h_in]
        o_ref = refs[n_in + h_in]
        h_outs = refs[n_in + h_in + 1:n_in + h_in + 1 + h_out]
        scr = refs[n_in + h_in + 1 + h_out:]
        i, j, k = pl.program_id(0), pl.program_id(1), pl.program_id(2)
        if hosted is not None:
            sems = scr[len(scr) - len(hosted.sems):]

            @pl.when((i == 0) & (j == 0) & (k == 0))
            def _():
                hosted.start(h_ins, h_outs, sems)

        def finish(res):
            if add_ref is not None:
                res = res + add_scale * add_ref[...]
            o_ref[...] = res.astype(out_dtype)

        def dot(x, y):
            return lax.dot_general(x, y, dims, preferred_element_type=F32)

        if isinstance(b, tuple):
            @pl.when(j < nj_half)
            def _():
                finish(dot(ops[0][...], ops[1][...]))

            @pl.when(j >= nj_half)
            def _():
                finish(dot(ops[0][...], ops[2][...]))
            part = None
        elif whole_split_k:
            srcs = [(ops[0], s) for s in range(N_CHIPS)] if not isinstance(a, tuple) else \
                   [(ops[s // 2], s % 2) for s in range(N_CHIPS)]
            b_ref = ops[-1]
            part = None
            for s, (src, off) in enumerate(srcs):
                term = dot(src[:, off * kq:(off + 1) * kq], b_ref[s])
                part = term if part is None else part + term
        else:
            part = dot(ops[0][...], ops[1][...])

        if part is None:
            pass
        elif nk == 1:
            finish(part)
        else:
            acc = scr[0]

            @pl.when(k == 0)
            def _():
                acc[...] = part

            @pl.when(k > 0)
            def _():
                acc[...] += part

            @pl.when(k == nk - 1)
            def _():
                finish(acc[...])

        if hosted is not None:
            @pl.when((i == grid[0] - 1) & (j == grid[1] - 1) & (k == nk - 1))
            def _():
                hosted.finish(h_ins, h_outs, sems)

    scratch = [pltpu.VMEM((tm, tn), F32)] if nk > 1 else []
    if hosted is not None:
        res = pl.pallas_call(
            body, name=name,
            out_shape=(out_shape,) + tuple(hosted.outs),
            grid=grid,
            in_specs=in_specs + [ANY] * h_in,
            out_specs=(out_spec,) + (ANY,) * h_out,
            input_output_aliases={n_in + src: 1 + dst for src, dst in hosted.alias.items()},
            scratch_shapes=scratch + [pltpu.SemaphoreType.DMA((n,)) for n in hosted.sems],
            compiler_params=_params("arbitrary", "arbitrary", "arbitrary"),
        )(*args, *hosted.ins)
        return res[0], list(res[1:])
    return pl.pallas_call(
        body, name=name,
        out_shape=out_shape,
        grid=grid,
        in_specs=in_specs,
        out_specs=out_spec,
        scratch_shapes=scratch,
        compiler_params=_params("parallel", "parallel", "arbitrary"),
    )(*args)


def _cast_bf16(w, name):
    L, R, C = w.shape
    tr, tc = _pick(R, 512, 16), _pick(C, 1408)

    def body(w_ref, o_ref):
        o_ref[...] = w_ref[...].astype(BF16)

    return pl.pallas_call(
        body, name=name, out_shape=jax.ShapeDtypeStruct(w.shape, BF16),
        grid=(L, R // tr, C // tc),
        in_specs=[pl.BlockSpec((None, tr, tc), lambda l, i, j: (l, i, j))],
        out_specs=pl.BlockSpec((None, tr, tc), lambda l, i, j: (l, i, j)),
        compiler_params=_params("parallel", "parallel", "parallel"),
    )(w)


def _cast_into_gather(pos, w, layer, name):
    L, R, C = w.shape
    r2 = R // 2
    tr, tc = _pick(r2, 512, 16), _pick(C, 1408)

    def body(p_ref, w_ref, o_ref):
        o_ref[...] = w_ref[...].astype(BF16)

    return pl.pallas_call(
        body, name=name, out_shape=jax.ShapeDtypeStruct((1, N_CHIPS, 2, r2, C), BF16),
        grid_spec=pltpu.PrefetchScalarGridSpec(
            num_scalar_prefetch=1, grid=(2, r2 // tr, C // tc),
            in_specs=[pl.BlockSpec((None, None, tr, tc), lambda h, i, j, p: (layer, h, i, j))],
            out_specs=pl.BlockSpec((None, None, None, tr, tc), lambda h, i, j, p: (0, p[0], h, i, j))),
        compiler_params=_params("parallel", "parallel", "parallel"),
    )(pos, w.reshape(L, 2, r2, C))


def _sigmoid(v):
    return 1.0 / (1.0 + jnp.exp(-v))


def _ln_fwd(x, y, g, b, alpha, name):
    S, D = x.shape
    tr = _pick(S, 256, 8)

    def body(x_ref, y_ref, g_ref, b_ref, o_ref, ob_ref, xh_ref, rs_ref):
        r = alpha * x_ref[...] + y_ref[...]
        mu = jnp.mean(r, axis=-1, keepdims=True)
        d = r - mu
        var = jnp.mean(d * d, axis=-1, keepdims=True)
        rstd = lax.rsqrt(var + LN_EPS)
        xh = d * rstd
        o = xh * g_ref[...] + b_ref[...]
        o_ref[...] = o
        ob_ref[...] = o.astype(BF16)
        xh_ref[...] = xh
        rs_ref[...] = rstd

    row = pl.BlockSpec((tr, D), lambda i: (i, 0))
    vec = pl.BlockSpec((1, D), lambda i: (0, 0))
    return pl.pallas_call(
        body, name=name,
        out_shape=(jax.ShapeDtypeStruct((S, D), F32), jax.ShapeDtypeStruct((S, D), BF16),
                   jax.ShapeDtypeStruct((S, D), F32), jax.ShapeDtypeStruct((S, 1), F32)),
        grid=(S // tr,),
        in_specs=[row, row, vec, vec],
        out_specs=(row, row, row, pl.BlockSpec((tr, 1), lambda i: (i, 0))),
        compiler_params=_params("parallel"),
    )(x, y, g, b)


def _ln_bwd(dout, xhat, rstd, g, name):
    S, D = dout.shape
    tr = _pick(S, 256, 8)

    def body(do_ref, xh_ref, rs_ref, g_ref, dr_ref, drb_ref, dg_ref, db_ref):
        i = pl.program_id(0)
        do = do_ref[...]
        xh = xh_ref[...]
        dxh = do * g_ref[...]
        m1 = jnp.mean(dxh, axis=-1, keepdims=True)
        m2 = jnp.mean(dxh * xh, axis=-1, keepdims=True)
        dr = rs_ref[...] * (dxh - m1 - xh * m2)
        dr_ref[...] = dr
        drb_ref[...] = dr.astype(BF16)
        pg = jnp.sum(do * xh, axis=0, keepdims=True)
        pb = jnp.sum(do, axis=0, keepdims=True)

        @pl.when(i == 0)
        def _():
            dg_ref[...] = pg
            db_ref[...] = pb

        @pl.when(i > 0)
        def _():
            dg_ref[...] += pg
            db_ref[...] += pb

    row = pl.BlockSpec((tr, D), lambda i: (i, 0))
    vec = pl.BlockSpec((1, D), lambda i: (0, 0))
    return pl.pallas_call(
        body, name=name,
        out_shape=(jax.ShapeDtypeStruct((S, D), F32), jax.ShapeDtypeStruct((S, D), BF16),
                   jax.ShapeDtypeStruct((1, D), F32), jax.ShapeDtypeStruct((1, D), F32)),
        grid=(S // tr,),
        in_specs=[row, row, pl.BlockSpec((tr, 1), lambda i: (i, 0)), vec],
        out_specs=(row, row, vec, vec),
        compiler_params=_params("arbitrary"),
    )(dout, xhat, rstd, g)


def _loss_grad(y, target, name):
    S, D = y.shape
    tr = _pick(S, 256, 8)
    n = S // tr

    def body(y_ref, t_ref, dy_ref, l_ref, acc):
        i = pl.program_id(0)
        d = y_ref[...] - t_ref[...]
        dy_ref[...] = d * (1.0 / D)
        p = jnp.sum(d * d, axis=0, keepdims=True)

        @pl.when(i == 0)
        def _():
            acc[...] = p

        @pl.when(i > 0)
        def _():
            acc[...] += p

        @pl.when(i == n - 1)
        def _():
            l_ref[...] = (0.5 / D) * jnp.sum(acc[...], axis=1, keepdims=True)

    row = pl.BlockSpec((tr, D), lambda i: (i, 0))
    return pl.pallas_call(
        body, name=name,
        out_shape=(jax.ShapeDtypeStruct((S, D), F32), jax.ShapeDtypeStruct((1, 1), F32)),
        grid=(n,),
        in_specs=[row, row],
        out_specs=(row, pl.BlockSpec((1, 1), lambda i: (0, 0))),
        scratch_shapes=[pltpu.VMEM((1, D), F32)],
        compiler_params=_params("arbitrary"),
    )(y, target)


def _prev_spec(tr, halo, width, col):
    return pl.BlockSpec((halo, width), lambda c, i: (jnp.maximum(i * (tr // halo) - 1, 0), col(c)))


def _next_spec(tr, halo, width, col, nrows):
    last = nrows // halo - 1
    return pl.BlockSpec((halo, width), lambda c, i: (jnp.minimum((i + 1) * (tr // halo), last), col(c)))


def _cur_spec(tr, width, col):
    return pl.BlockSpec((tr, width), lambda c, i: (i, col(c)))


def _ffn_act_fwd(hu, conv_w, conv_b, name):
    S, F2 = hu.shape
    F = F2 // 2
    tr, tc, H = _pick(S, 512, 16), _pick(F, 512), HALO3
    nc, nr = F // tc, S // tr

    def body(gp_ref, g_ref, u_ref, w_ref, b_ref, z_ref, G):
        i = pl.program_id(1)
        G[0:H, :] = jnp.where(i > 0, gp_ref[...].astype(F32), 0.0)
        G[H:H + tr, :] = g_ref[...].astype(F32)
        gc = (b_ref[...] + w_ref[pl.ds(0, 1), :] * G[pl.ds(H - 2, tr), :]
              + w_ref[pl.ds(1, 1), :] * G[pl.ds(H - 1, tr), :] + w_ref[pl.ds(2, 1), :] * G[pl.ds(H, tr), :])
        z = gc * _sigmoid(gc) * u_ref[...].astype(F32)
        z_ref[...] = z.astype(BF16)

    gcol = lambda c: c
    ucol = lambda c: c + nc
    return pl.pallas_call(
        body, name=name, out_shape=jax.ShapeDtypeStruct((S, F), BF16),
        grid=(nc, nr),
        in_specs=[_prev_spec(tr, H, tc, gcol), _cur_spec(tr, tc, gcol), _cur_spec(tr, tc, ucol),
                  pl.BlockSpec((3, tc), lambda c, i: (0, c)), pl.BlockSpec((1, tc), lambda c, i: (0, c))],
        out_specs=pl.BlockSpec((tr, tc), lambda c, i: (i, c)),
        scratch_shapes=[pltpu.VMEM((H + tr, tc), F32)],
        compiler_params=_params("parallel", "parallel"),
    )(hu, hu, hu, conv_w, conv_b)


def _ffn_act_bwd(dz, hu, conv_w, conv_b, name):
    S, F = dz.shape
    tr, tc, H = _pick(S, 512, 16), _pick(F, 512), HALO3
    nc, nr = F // tc, S // tr
    n = tr + H

    def body(dz_ref, dzn_ref, gp_ref, g_ref, gn_ref, u_ref, un_ref, w_ref, b_ref,
             dg_ref, du_ref, dw_ref, db_ref, G, DG):
        i = pl.program_id(1)
        G[0:H, :] = jnp.where(i > 0, gp_ref[...].astype(F32), 0.0)
        G[H:H + tr, :] = g_ref[...].astype(F32)
        G[H + tr:H + tr + H, :] = gn_ref[...].astype(F32)
        w0, w1, w2 = w_ref[pl.ds(0, 1), :], w_ref[pl.ds(1, 1), :], w_ref[pl.ds(2, 1), :]
        gc = b_ref[...] + w0 * G[pl.ds(H - 2, n), :] + w1 * G[pl.ds(H - 1, n), :] + w2 * G[pl.ds(H, n), :]
        sg = _sigmoid(gc)
        dzf = jnp.concatenate([dz_ref[...], dzn_ref[...]], axis=0).astype(F32)
        uf = jnp.concatenate([u_ref[...], un_ref[...]], axis=0).astype(F32)
        rows = lax.broadcasted_iota(jnp.int32, (n, 1), 0)
        dzf = jnp.where((rows < tr) | (i < nr - 1), dzf, 0.0)
        dgc = dzf * uf * (sg * (1.0 + gc * (1.0 - sg)))
        du_ref[...] = (dzf[0:tr] * (gc[0:tr] * sg[0:tr])).astype(BF16)
        DG[...] = dgc
        dg = w2 * DG[pl.ds(0, tr), :] + w1 * DG[pl.ds(1, tr), :] + w0 * DG[pl.ds(2, tr), :]
        dg_ref[...] = dg.astype(BF16)
        dcur = dgc[0:tr]
        pw = [jnp.sum(dcur * G[pl.ds(H - 2 + k, tr), :], axis=0, keepdims=True) for k in range(3)]
        pb = jnp.sum(dcur, axis=0, keepdims=True)

        @pl.when(i == 0)
        def _():
            for k in range(3):
                dw_ref[pl.ds(k, 1), :] = pw[k]
            db_ref[...] = pb

        @pl.when(i > 0)
        def _():
            for k in range(3):
                dw_ref[pl.ds(k, 1), :] += pw[k]
            db_ref[...] += pb

    gcol = lambda c: c
    ucol = lambda c: c + nc
    blk = pl.BlockSpec((tr, tc), lambda c, i: (i, c))
    return pl.pallas_call(
        body, name=name,
        out_shape=(jax.ShapeDtypeStruct((S, F), BF16), jax.ShapeDtypeStruct((S, F), BF16),
                   jax.ShapeDtypeStruct((3, F), F32), jax.ShapeDtypeStruct((1, F), F32)),
        grid=(nc, nr),
        in_specs=[_cur_spec(tr, tc, gcol), _next_spec(tr, H, tc, gcol, S),
                  _prev_spec(tr, H, tc, gcol), _cur_spec(tr, tc, gcol), _next_spec(tr, H, tc, gcol, S),
                  _cur_spec(tr, tc, ucol), _next_spec(tr, H, tc, ucol, S),
                  pl.BlockSpec((3, tc), lambda c, i: (0, c)), pl.BlockSpec((1, tc), lambda c, i: (0, c))],
        out_specs=(blk, blk, pl.BlockSpec((3, tc), lambda c, i: (0, c)), pl.BlockSpec((1, tc), lambda c, i: (0, c))),
        scratch_shapes=[pltpu.VMEM((H + tr + H, tc), F32), pltpu.VMEM((n, tc), F32)],
        compiler_params=_params("parallel", "arbitrary"),
    )(dz, dz, hu, hu, hu, hu, hu, conv_w, conv_b)


def _softplus_neg(s):
    return jnp.minimum(-s, 0.0) - jnp.log(1.0 + jnp.exp(-jnp.abs(s)))


def _hilo_dot(v, m):
    hi = v.astype(BF16)
    lo = (v - hi.astype(F32)).astype(BF16)
    return (jnp.dot(hi, m, preferred_element_type=F32) + jnp.dot(lo, m, preferred_element_type=F32))


def _attn_fwd(h, n_heads, name, gather=()):
    S = h.shape[0]
    dh = HEAD_DIM
    A = n_heads * dh
    tq = _pick(S, 256)
    nq = S // tq
    scale = 1.0 / float(dh) ** 0.5
    ng = len(gather)
    hp = 2 if n_heads % 2 == 0 else 1
    n_grp, hw = n_heads // hp, hp * dh

    def body(*refs):
        q_ref, k_ref, v_ref = refs[:3]
        o_ref, tot_ref = refs[3 + ng:5 + ng]
        full = refs[5 + ng:5 + 2 * ng]
        hd = pl.program_id(0)
        i = pl.program_id(1)
        if ng:
            ssem, rsem = refs[5 + 2 * ng:]

            @pl.when((hd == 0) & (i == 0))
            def _():
                _gather_start(full, ssem, rsem)

            @pl.when((hd == n_grp - 1) & (i == 0))
            def _():
                _gather_forward(full, ssem, rsem)

        heads = range(hp)
        qs = [q_ref[:, h * dh:(h + 1) * dh] for h in heads]
        r_io = lax.broadcasted_iota(jnp.int32, (tq, tq), 0)
        c_io = lax.broadcasted_iota(jnp.int32, (tq, tq), 1)
        later = (r_io > c_io).astype(BF16)
        ext = jnp.concatenate([later, jnp.ones((tq, LANES), BF16)], axis=1)
        causal = c_io < r_io

        def rows(ref, j):
            blk = ref[pl.ds(pl.multiple_of(j * tq, tq), tq), :]
            return [blk[:, h * dh:(h + 1) * dh] for h in heads]

        def qk(kj):
            return [lax.dot_general(qs[h], kj[h], (((1,), (1,)), ((), ())), preferred_element_type=F32) * scale
                    for h in heads]

        def log_weights(s, diag):
            base, tot = [], []
            for h in heads:
                ls = _softplus_neg(s[h])
                if diag:
                    ls = jnp.where(causal, ls, 0.0)
                cs = _hilo_dot(ls, ext)
                b = s[h] + ls + cs[:, :tq]
                base.append(jnp.where(causal, b, -1e30) if diag else b)
                tot.append(cs[:, tq:])
            return base, tot

        def weigh(vj, acc, run, base):
            out = []
            for h in heads:
                w = jnp.exp(base[h] + jnp.tile(run[h], (1, tq // LANES)))
                out.append(acc[h] + jnp.dot(w.astype(BF16), vj[h], preferred_element_type=F32))
            return out

        def trip(t, carry):
            acc, run, base, tot = carry
            j = i - 1 - t
            s = qk(rows(k_ref, j))
            acc = weigh(rows(v_ref, j + 1), acc, run, base)
            base_n, tot_n = log_weights(s, False)
            return acc, [run[h] + tot[h] for h in heads], base_n, tot_n

        base, tot = log_weights(qk(rows(k_ref, i)), True)
        carry = ([jnp.zeros((tq, dh), F32) for _ in heads], [jnp.zeros((tq, LANES), F32) for _ in heads], base, tot)
        acc, run, base, tot = lax.fori_loop(0, i, trip, carry)
        acc = weigh(rows(v_ref, 0), acc, run, base)
        for h in heads:
            o_ref[:, h * dh:(h + 1) * dh] = acc[h].astype(BF16)
            tot_ref[h] = run[h] + tot[h]
        if ng:
            @pl.when((hd == n_grp - 1) & (i == nq - 1))
            def _():
                _gather_finish(full, ssem, rsem)

    T = _gather_items(gather) if ng else 0
    return pl.pallas_call(
        body, name=name,
        out_shape=(jax.ShapeDtypeStruct((S, A), BF16), jax.ShapeDtypeStruct((n_heads, S, LANES), F32))
        + tuple(jax.ShapeDtypeStruct(b.shape, b.dtype) for b in gather),
        grid=(n_grp, nq),
        in_specs=[pl.BlockSpec((tq, hw), lambda hd, i: (i, hd)),
                  pl.BlockSpec((S, hw), lambda hd, i: (0, n_grp + hd)),
                  pl.BlockSpec((S, hw), lambda hd, i: (0, 2 * n_grp + hd))] + [ANY] * ng,
        out_specs=(pl.BlockSpec((tq, hw), lambda hd, i: (i, hd)),
                   pl.BlockSpec((hp, tq, LANES), lambda hd, i: (hd, i, 0))) + (ANY,) * ng,
        input_output_aliases={3 + a: 2 + a for a in range(ng)},
        scratch_shapes=[pltpu.SemaphoreType.DMA((6 * T,)), pltpu.SemaphoreType.DMA((6 * T,))] if ng else [],
        compiler_params=_params("arbitrary", "arbitrary") if ng else _params("parallel", "parallel"),
    )(h, h, h, *gather)


def _attn_bwd(h, do, tot, n_heads, name, hosted=None):
    S = h.shape[0]
    dh = HEAD_DIM
    A = n_heads * dh
    tq = _pick(S, 256)
    nq = S // tq
    scale = 1.0 / float(dh) ** 0.5
    nt_dims = (((1,), (1,)), ((), ()))
    tn_dims = (((0,), (0,)), ((), ()))
    hp = 2 if n_heads % 2 == 0 else 1
    n_grp, hw = n_heads // hp, hp * dh
    h_in = 0 if hosted is None else len(hosted.ins)
    h_out = 0 if hosted is None else len(hosted.outs)

    def body(*refs):
        q_ref, k_ref, v_ref, do_ref, tot_ref = refs[:5]
        h_ins = refs[5:5 + h_in]
        dq_ref, dk_ref, dv_ref = refs[5 + h_in:8 + h_in]
        h_outs = refs[8 + h_in:8 + h_in + h_out]
        dk_acc, dv_acc = refs[8 + h_in + h_out:10 + h_in + h_out]
        sems = refs[10 + h_in + h_out:]
        hd = pl.program_id(0)
        i = pl.program_id(1)
        if hosted is not None:
            @pl.when((hd == 0) & (i == 0))
            def _():
                hosted.start(h_ins, h_outs, sems)

        @pl.when(i == 0)
        def _():
            dk_acc[...] = jnp.zeros_like(dk_acc)
            dv_acc[...] = jnp.zeros_like(dv_acc)

        heads = range(hp)
        qs = [q_ref[:, h * dh:(h + 1) * dh] for h in heads]
        dos = [do_ref[:, h * dh:(h + 1) * dh] for h in heads]
        total = [jnp.tile(tot_ref[h], (1, tq // LANES)) for h in heads]
        r_io = lax.broadcasted_iota(jnp.int32, (tq, tq), 0)
        c_io = lax.broadcasted_iota(jnp.int32, (tq, tq), 1)
        ones = jnp.ones((tq, LANES), BF16)
        upto = jnp.concatenate([(r_io <= c_io).astype(BF16), ones], axis=1)
        before = jnp.concatenate([(r_io < c_io).astype(BF16), ones], axis=1)
        causal = c_io < r_io

        def rows(ref, j):
            blk = ref[pl.ds(pl.multiple_of(j * tq, tq), tq), :]
            return [blk[:, h * dh:(h + 1) * dh] for h in heads]

        def qk(kj):
            return [lax.dot_general(qs[h], kj[h], nt_dims, preferred_element_type=F32) * scale for h in heads]

        def weights(base, prun, vj):
            dw = [lax.dot_general(dos[h], vj[h], nt_dims, preferred_element_type=F32) for h in heads]
            w, e, ce = [], [], []
            for h in heads:
                w.append(jnp.exp(base[h] + (total[h] - jnp.tile(prun[h], (1, tq // LANES)))))
                e.append(dw[h] * w[h])
                ce.append(jnp.dot(e[h].astype(BF16), before, preferred_element_type=F32))
            return w, e, ce

        def prefix(s, j):
            keep = jnp.logical_or(causal, j != i)
            ls = [jnp.where(keep, _softplus_neg(s[h]), 0.0) for h in heads]
            return keep, ls, [_hilo_dot(ls[h], upto) for h in heads]

        def grads(j, kj, dq, erun, w, e, ce, sn):
            start = pl.multiple_of(j * tq, tq)
            out = []
            for h in heads:
                ecum = ce[h][:, :tq] + jnp.tile(erun[h], (1, tq // LANES))
                dz = e[h] * sn[h] - (1.0 - sn[h]) * ecum
                ds = (dz * scale).astype(BF16)
                cols = slice(h * dh, (h + 1) * dh)
                dv_acc[pl.ds(start, tq), cols] += lax.dot_general(w[h].astype(BF16), dos[h], tn_dims,
                                                                  preferred_element_type=F32)
                out.append(dq[h] + jnp.dot(ds, kj[h], preferred_element_type=F32))
                dk_acc[pl.ds(start, tq), cols] += lax.dot_general(ds, qs[h], tn_dims, preferred_element_type=F32)
            return out, [erun[h] + ce[h][:, tq:] for h in heads]

        def carried(s, keep, ls, cs):
            base = [jnp.where(keep, s[h] + ls[h] - cs[h][:, :tq], -1e30) for h in heads]
            return base, [jnp.exp(ls[h]) for h in heads], [cs[h][:, tq:] for h in heads]

        def trip(j, carry):
            dq, prun, erun, base, sn, ptot = carry
            s_n = qk(rows(k_ref, j + 1))
            w, e, ce = weights(base, prun, rows(v_ref, j))
            keep, ls_n, cs = prefix(s_n, j + 1)
            dq, erun = grads(j, rows(k_ref, j), dq, erun, w, e, ce, sn)
            return (dq, [prun[h] + ptot[h] for h in heads], erun) + carried(s_n, keep, ls_n, cs)

        zeros = [jnp.zeros((tq, LANES), F32) for _ in heads]
        s0 = qk(rows(k_ref, 0))
        first = carried(s0, *prefix(s0, 0))
        carry = lax.fori_loop(0, i, trip, ([jnp.zeros((tq, dh), F32) for _ in heads], zeros, zeros) + first)
        dq, prun, erun, base, sn, _ = carry
        dq, _ = grads(i, rows(k_ref, i), dq, erun, *weights(base, prun, rows(v_ref, i)), sn)
        for h in heads:
            dq_ref[:, h * dh:(h + 1) * dh] = dq[h].astype(BF16)

        @pl.when(i == nq - 1)
        def _():
            dk_ref[...] = dk_acc[...].astype(BF16)
            dv_ref[...] = dv_acc[...].astype(BF16)

        if hosted is not None:
            @pl.when((hd == n_grp - 1) & (i == nq - 1))
            def _():
                hosted.finish(h_ins, h_outs, sems)

    qblk = pl.BlockSpec((tq, hw), lambda hd, i: (i, hd))
    full = pl.BlockSpec((S, hw), lambda hd, i: (0, hd))
    scratch = [pltpu.VMEM((S, hw), F32), pltpu.VMEM((S, hw), F32)]
    if hosted is not None:
        scratch += [pltpu.SemaphoreType.DMA((n,)) for n in hosted.sems]
    res = pl.pallas_call(
        body, name=name,
        out_shape=tuple(jax.ShapeDtypeStruct((S, A), BF16) for _ in range(3))
        + (tuple(hosted.outs) if hosted is not None else ()),
        grid=(n_grp, nq),
        in_specs=[qblk,
                  pl.BlockSpec((S, hw), lambda hd, i: (0, n_grp + hd)),
                  pl.BlockSpec((S, hw), lambda hd, i: (0, 2 * n_grp + hd)),
                  qblk,
                  pl.BlockSpec((hp, tq, LANES), lambda hd, i: (hd, i, 0))] + [ANY] * h_in,
        out_specs=(qblk, full, full) + (ANY,) * h_out,
        input_output_aliases={} if hosted is None else {5 + a: 3 + b for a, b in hosted.alias.items()},
        scratch_shapes=scratch,
        compiler_params=_params("arbitrary", "arbitrary") if hosted is not None else _params("parallel", "arbitrary"),
    )(h, h, h, do, tot, *(hosted.ins if hosted is not None else ()))
    return res[0], res[1], res[2], list(res[3:])


def _evenconv_fwd(h, dw_w, dw_b, bn_g, bn_b, name):
    S = h.shape[0]
    KW, A = dw_w.shape
    H = HALO31
    tr = _pick(S, 256, H)
    first_tap = H - (KW - 1)

    def body(ap_ref, a_ref, gp_ref, g_ref, w_ref, b_ref, bg_ref, bb_ref, u1_ref, u3_ref, U):
        i = pl.program_id(1)
        glu_prev = ap_ref[...].astype(F32) * _sigmoid(gp_ref[...].astype(F32))
        U[0:H, :] = jnp.where(i > 0, glu_prev, 0.0)
        U[H:H + tr, :] = a_ref[...].astype(F32) * _sigmoid(g_ref[...].astype(F32))
        acc = b_ref[...] + w_ref[pl.ds(0, 1), :] * U[pl.ds(first_tap, tr), :]
        for k in range(1, KW):
            acc = acc + w_ref[pl.ds(k, 1), :] * U[pl.ds(first_tap + k, tr), :]
        u1_ref[...] = acc
        mu = jnp.mean(acc, axis=-1, keepdims=True)
        d = acc - mu
        var = jnp.mean(d * d, axis=-1, keepdims=True)
        u2 = d * lax.rsqrt(var + LN_EPS) * bg_ref[...] + bb_ref[...]
        u3_ref[...] = (u2 * _sigmoid(u2)).astype(BF16)

    acol = lambda c: 3
    gcol = lambda c: 4
    vec = pl.BlockSpec((1, A), lambda c, i: (0, 0))
    blk = pl.BlockSpec((tr, A), lambda c, i: (i, 0))
    return pl.pallas_call(
        body, name=name,
        out_shape=(jax.ShapeDtypeStruct((S, A), F32), jax.ShapeDtypeStruct((S, A), BF16)),
        grid=(1, S // tr),
        in_specs=[_prev_spec(tr, H, A, acol), _cur_spec(tr, A, acol),
                  _prev_spec(tr, H, A, gcol), _cur_spec(tr, A, gcol),
                  pl.BlockSpec((KW, A), lambda c, i: (0, 0)), vec, vec, vec],
        out_specs=(blk, blk),
        scratch_shapes=[pltpu.VMEM((H + tr, A), F32)],
        compiler_params=_params("parallel", "parallel"),
    )(h, h, h, h, dw_w, dw_b, bn_g, bn_b)


def _evenconv_bwd(du3, u1, h, dw_w, bn_g, bn_b, name):
    S = h.shape[0]
    KW, A = dw_w.shape
    H = HALO31
    tr = _pick(S, 256, H)
    nr = S // tr
    n = tr + H
    first_tap = H - (KW - 1)

    def body(d3_ref, d3n_ref, u1_ref, u1n_ref, ap_ref, a_ref, gp_ref, g_ref, w_ref, bg_ref, bb_ref,
             da_ref, dg_ref, dww_ref, dwb_ref, dbg_ref, dbb_ref, U0, DU):
        i = pl.program_id(1)
        u1 = jnp.concatenate([u1_ref[...], u1n_ref[...]], axis=0)
        d3 = jnp.concatenate([d3_ref[...], d3n_ref[...]], axis=0).astype(F32)
        rows = lax.broadcasted_iota(jnp.int32, (n, 1), 0)
        d3 = jnp.where((rows < tr) | (i < nr - 1), d3, 0.0)
        mu = jnp.mean(u1, axis=-1, keepdims=True)
        d = u1 - mu
        var = jnp.mean(d * d, axis=-1, keepdims=True)
        rstd = lax.rsqrt(var + LN_EPS)
        xh = d * rstd
        u2 = xh * bg_ref[...] + bb_ref[...]
        sg = _sigmoid(u2)
        du2 = d3 * (sg * (1.0 + u2 * (1.0 - sg)))
        dxh = du2 * bg_ref[...]
        m1 = jnp.mean(dxh, axis=-1, keepdims=True)
        m2 = jnp.mean(dxh * xh, axis=-1, keepdims=True)
        du1 = rstd * (dxh - m1 - xh * m2)
        DU[...] = du1
        pbg = jnp.sum(du2[0:tr] * xh[0:tr], axis=0, keepdims=True)
        pbb = jnp.sum(du2[0:tr], axis=0, keepdims=True)
        pwb = jnp.sum(du1[0:tr], axis=0, keepdims=True)

        glu_prev = ap_ref[...].astype(F32) * _sigmoid(gp_ref[...].astype(F32))
        U0[0:H, :] = jnp.where(i > 0, glu_prev, 0.0)
        a = a_ref[...].astype(F32)
        sgg = _sigmoid(g_ref[...].astype(F32))
        U0[H:H + tr, :] = a * sgg

        @pl.when(i == 0)
        def _():
            dbg_ref[...] = pbg
            dbb_ref[...] = pbb
            dwb_ref[...] = pwb
            dww_ref[...] = jnp.zeros_like(dww_ref)

        @pl.when(i > 0)
        def _():
            dbg_ref[...] += pbg
            dbb_ref[...] += pbb
            dwb_ref[...] += pwb

        du0 = w_ref[pl.ds(0, 1), :] * DU[pl.ds(KW - 1, tr), :]
        for k in range(1, KW):
            du0 = du0 + w_ref[pl.ds(k, 1), :] * DU[pl.ds(KW - 1 - k, tr), :]
        da_ref[...] = (du0 * sgg).astype(BF16)
        dg_ref[...] = (du0 * a * sgg * (1.0 - sgg)).astype(BF16)
        dcur = DU[pl.ds(0, tr), :]
        for k in range(KW):
            dww_ref[pl.ds(k, 1), :] += jnp.sum(dcur * U0[pl.ds(first_tap + k, tr), :], axis=0, keepdims=True)

    acol = lambda c: 3
    gcol = lambda c: 4
    one = lambda c: 1
    zero = lambda c: 0
    vec = pl.BlockSpec((1, A), lambda c, i: (0, 0))
    blk = pl.BlockSpec((tr, A), lambda c, i: (i, 0))
    return pl.pallas_call(
        body, name=name,
        out_shape=(jax.ShapeDtypeStruct((S, A), BF16), jax.ShapeDtypeStruct((S, A), BF16),
                   jax.ShapeDtypeStruct((KW, A), F32), jax.ShapeDtypeStruct((1, A), F32),
                   jax.ShapeDtypeStruct((1, A), F32), jax.ShapeDtypeStruct((1, A), F32)),
        grid=(1, nr),
        in_specs=[_cur_spec(tr, A, one), _next_spec(tr, H, A, one, S),
                  _cur_spec(tr, A, zero), _next_spec(tr, H, A, zero, S),
                  _prev_spec(tr, H, A, acol), _cur_spec(tr, A, acol),
                  _prev_spec(tr, H, A, gcol), _cur_spec(tr, A, gcol),
                  pl.BlockSpec((KW, A), lambda c, i: (0, 0)), vec, vec],
        out_specs=(blk, blk, pl.BlockSpec((KW, A), lambda c, i: (0, 0)), vec, vec, vec),
        scratch_shapes=[pltpu.VMEM((H + tr, A), F32), pltpu.VMEM((n, A), F32)],
        compiler_params=_params("arbitrary", "arbitrary"),
    )(du3, du3, u1, u1, h, h, h, h, dw_w, bn_g, bn_b)


def _pool_inv_count(row0, nrows, window):
    t = row0 + lax.broadcasted_iota(jnp.int32, (nrows, 1), 0)
    return 1.0 / jnp.minimum(t + 1, window).astype(F32)


def _odd_fwd(h, conv_w, pool_w, pool_scale, name):
    S = h.shape[0]
    C = conv_w.shape[1]
    G = len(POOL_WINDOWS)
    Dg = C // G
    H = HALO3
    tr = _pick(S, 256, H)

    def body(cb_ref, ccp_ref, cc_ref, chp_ref, ch_ref, pp_ref, p_ref, w_ref, pw_ref, sc_ref, mix_ref, M, P):
        i = pl.program_id(1)
        M[0:H, :] = jnp.where(i > 0, ccp_ref[...].astype(F32) * chp_ref[...].astype(F32), 0.0)
        M[H:H + tr, :] = cc_ref[...].astype(F32) * ch_ref[...].astype(F32)
        cm = (w_ref[pl.ds(0, 1), :] * M[pl.ds(H - 2, tr), :] + w_ref[pl.ds(1, 1), :] * M[pl.ds(H - 1, tr), :]
              + w_ref[pl.ds(2, 1), :] * M[pl.ds(H, tr), :])
        mix_ref[:, 0:C] = (cb_ref[...].astype(F32) * cm).astype(BF16)
        P[0:H, :] = jnp.where(i > 0, pp_ref[...].astype(F32), 0.0)
        P[H:H + tr, :] = p_ref[...].astype(F32)
        for gi, window in enumerate(POOL_WINDOWS):
            cols = pl.ds(gi * Dg, Dg)
            wsum = P[pl.ds(H, tr), cols]
            for dlt in range(1, window):
                wsum = wsum + P[pl.ds(H - dlt, tr), cols]
            diff = wsum * _pool_inv_count(i * tr, tr, window) - P[pl.ds(H, tr), cols]
            yd = jnp.dot(diff.astype(BF16), pw_ref[gi], preferred_element_type=F32) * sc_ref[:, cols]
            mix_ref[:, pl.ds(C + gi * Dg, Dg)] = yd.astype(BF16)

    col = lambda k: (lambda c: k)
    return pl.pallas_call(
        body, name=name, out_shape=jax.ShapeDtypeStruct((S, 2 * C), BF16),
        grid=(1, S // tr),
        in_specs=[_cur_spec(tr, C, col(0)),
                  _prev_spec(tr, H, C, col(1)), _cur_spec(tr, C, col(1)),
                  _prev_spec(tr, H, C, col(2)), _cur_spec(tr, C, col(2)),
                  _prev_spec(tr, H, C, col(3)), _cur_spec(tr, C, col(3)),
                  pl.BlockSpec((3, C), lambda c, i: (0, 0)),
                  pl.BlockSpec((G, Dg, Dg), lambda c, i: (0, 0, 0)),
                  pl.BlockSpec((1, C), lambda c, i: (0, 0))],
        out_specs=pl.BlockSpec((tr, 2 * C), lambda c, i: (i, 0)),
        scratch_shapes=[pltpu.VMEM((H + tr, C), F32), pltpu.VMEM((H + tr, C), F32)],
        compiler_params=_params("parallel", "parallel"),
    )(h, h, h, h, h, h, h, conv_w, pool_w, pool_scale)


def _odd_bwd(dmix, h, conv_w, pool_w, pool_scale, name):
    S = h.shape[0]
    C = conv_w.shape[1]
    G = len(POOL_WINDOWS)
    Dg = C // G
    H = HALO3
    tr = _pick(S, 256, H)
    nr = S // tr
    n = tr + H
    nt_dims = (((1,), (1,)), ((), ()))
    tn_dims = (((0,), (0,)), ((), ()))

    def body(dyc_ref, dycn_ref, dyd_ref, dydn_ref, cb_ref, cbn_ref, ccp_ref, cc_ref, ccn_ref,
             chp_ref, ch_ref, chn_ref, pp_ref, p_ref, w_ref, pw_ref, sc_ref,
             dh_ref, dw_ref, dpw_ref, dsc_ref, M, DCM, P, Q):
        i = pl.program_id(1)
        rows = lax.broadcasted_iota(jnp.int32, (n, 1), 0)
        valid = (rows < tr) | (i < nr - 1)

        @pl.when(i == 0)
        def _():
            dw_ref[...] = jnp.zeros_like(dw_ref)
            dpw_ref[...] = jnp.zeros_like(dpw_ref)
            dsc_ref[...] = jnp.zeros_like(dsc_ref)

        M[0:H, :] = jnp.where(i > 0, ccp_ref[...].astype(F32) * chp_ref[...].astype(F32), 0.0)
        cc = cc_ref[...].astype(F32)
        ch = ch_ref[...].astype(F32)
        M[H:H + tr, :] = cc * ch
        M[H + tr:H + tr + H, :] = ccn_ref[...].astype(F32) * chn_ref[...].astype(F32)
        w0, w1, w2 = w_ref[pl.ds(0, 1), :], w_ref[pl.ds(1, 1), :], w_ref[pl.ds(2, 1), :]
        cm = w0 * M[pl.ds(H - 2, tr), :] + w1 * M[pl.ds(H - 1, tr), :] + w2 * M[pl.ds(H, tr), :]
        dyc = jnp.concatenate([dyc_ref[...], dycn_ref[...]], axis=0).astype(F32)
        dyc = jnp.where(valid, dyc, 0.0)
        cbf = jnp.concatenate([cb_ref[...], cbn_ref[...]], axis=0).astype(F32)
        dh_ref[:, 0:C] = (dyc[0:tr] * cm).astype(BF16)
        DCM[...] = dyc * cbf
        dm = w2 * DCM[pl.ds(0, tr), :] + w1 * DCM[pl.ds(1, tr), :] + w0 * DCM[pl.ds(2, tr), :]
        dh_ref[:, C:2 * C] = (dm * ch).astype(BF16)
        dh_ref[:, 2 * C:3 * C] = (dm * cc).astype(BF16)
        dcur = DCM[pl.ds(0, tr), :]
        for k in range(3):
            dw_ref[pl.ds(k, 1), :] += jnp.sum(dcur * M[pl.ds(H - 2 + k, tr), :], axis=0, keepdims=True)

        P[0:H, :] = jnp.where(i > 0, pp_ref[...].astype(F32), 0.0)
        P[H:H + tr, :] = p_ref[...].astype(F32)
        dyd = jnp.concatenate([dyd_ref[...], dydn_ref[...]], axis=0).astype(F32)
        dyd = jnp.where(valid, dyd, 0.0)
        for gi, window in enumerate(POOL_WINDOWS):
            cols = pl.ds(gi * Dg, Dg)
            lo = gi * Dg
            wsum = P[pl.ds(H, tr), cols]
            for dlt in range(1, window):
                wsum = wsum + P[pl.ds(H - dlt, tr), cols]
            diff = (wsum * _pool_inv_count(i * tr, tr, window) - P[pl.ds(H, tr), cols]).astype(BF16)
            pw = pw_ref[gi]
            dyd_g = dyd[:, lo:lo + Dg]
            e = (dyd_g * sc_ref[:, cols]).astype(BF16)
            yraw = jnp.dot(diff, pw, preferred_element_type=F32)
            dsc_ref[:, cols] += jnp.sum(dyd_g[0:tr] * yraw, axis=0, keepdims=True)
            dpw_ref[gi] += lax.dot_general(diff, e[0:tr], tn_dims, preferred_element_type=F32)
            ddiff = lax.dot_general(e, pw, nt_dims, preferred_element_type=F32)
            Q[:, cols] = ddiff * _pool_inv_count(i * tr, n, window)
            acc = Q[pl.ds(0, tr), cols]
            for dlt in range(1, window):
                acc = acc + Q[pl.ds(dlt, tr), cols]
            dh_ref[:, pl.ds(3 * C + lo, Dg)] = (acc - ddiff[0:tr]).astype(BF16)

    col = lambda k: (lambda c: k)
    return pl.pallas_call(
        body, name=name,
        out_shape=(jax.ShapeDtypeStruct((S, 4 * C), BF16), jax.ShapeDtypeStruct((3, C), F32),
                   jax.ShapeDtypeStruct((G, Dg, Dg), F32), jax.ShapeDtypeStruct((1, C), F32)),
        grid=(1, nr),
        in_specs=[_cur_spec(tr, C, col(0)), _next_spec(tr, H, C, col(0), S),
                  _cur_spec(tr, C, col(1)), _next_spec(tr, H, C, col(1), S),
                  _cur_spec(tr, C, col(0)), _next_spec(tr, H, C, col(0), S),
                  _prev_spec(tr, H, C, col(1)), _cur_spec(tr, C, col(1)), _next_spec(tr, H, C, col(1), S),
                  _prev_spec(tr, H, C, col(2)), _cur_spec(tr, C, col(2)), _next_spec(tr, H, C, col(2), S),
                  _prev_spec(tr, H, C, col(3)), _cur_spec(tr, C, col(3)),
                  pl.BlockSpec((3, C), lambda c, i: (0, 0)),
                  pl.BlockSpec((G, Dg, Dg), lambda c, i: (0, 0, 0)),
                  pl.BlockSpec((1, C), lambda c, i: (0, 0))],
        out_specs=(pl.BlockSpec((tr, 4 * C), lambda c, i: (i, 0)),
                   pl.BlockSpec((3, C), lambda c, i: (0, 0)),
                   pl.BlockSpec((G, Dg, Dg), lambda c, i: (0, 0, 0)),
                   pl.BlockSpec((1, C), lambda c, i: (0, 0))),
        scratch_shapes=[pltpu.VMEM((H + tr + H, C), F32), pltpu.VMEM((n, C), F32),
                        pltpu.VMEM((H + tr, C), F32), pltpu.VMEM((n, C), F32)],
        compiler_params=_params("arbitrary", "arbitrary"),
    )(dmix, dmix, dmix, dmix, h, h, h, h, h, h, h, h, h, h, conv_w, pool_w, pool_scale)


def _adamw(w, grads, m, v, name):
    L, R, C = w.shape
    assert len(grads) == L
    tr, tc = _pick(R, 256, 8), _pick(C, 1408)
    ni, nj = R // tr, C // tc
    c1 = 1.0 / (1.0 - ADAM_B1 ** ADAM_STEP)
    c2 = 1.0 / (1.0 - ADAM_B2 ** ADAM_STEP)

    def g_spec(layer):
        def idx(l, i, j):
            before, after = l < layer, l > layer
            return (jnp.where(before, 0, jnp.where(after, ni - 1, i)),
                    jnp.where(before, 0, jnp.where(after, nj - 1, j)))
        return pl.BlockSpec((tr, tc), idx)

    def body(w_ref, *rest):
        g_refs = rest[:L]
        m_ref, v_ref, go_ref, d_ref, mo_ref, vo_ref = rest[L:]
        l = pl.program_id(0)
        gg = g_refs[0][...]
        for k in range(1, L):
            gg = jnp.where(l == k, g_refs[k][...], gg)
        mn = ADAM_B1 * m_ref[...] + (1.0 - ADAM_B1) * gg
        vn = ADAM_B2 * v_ref[...] + (1.0 - ADAM_B2) * (gg * gg)
        d_ref[...] = -ADAM_LR * ((mn * c1) / (jnp.sqrt(vn * c2) + ADAM_EPS) + ADAM_WD * w_ref[...])
        go_ref[...] = gg
        mo_ref[...] = mn
        vo_ref[...] = vn

    blk = pl.BlockSpec((None, tr, tc), lambda l, i, j: (l, i, j))
    sds = jax.ShapeDtypeStruct(w.shape, F32)
    return pl.pallas_call(
        body, name=name, out_shape=(sds, sds, sds, sds),
        grid=(L, R // tr, C // tc),
        in_specs=[blk] + [g_spec(k) for k in range(L)] + [blk, blk], out_specs=(blk, blk, blk, blk),
        compiler_params=_params("arbitrary", "arbitrary", "arbitrary"),
    )(w, *grads, m, v)


def _sum_slots(buf, name):
    N, R, C = buf.shape
    tr = _pick(R, 512, 8)

    def body(b_ref, o_ref):
        acc = b_ref[0]
        for k in range(1, N):
            acc = acc + b_ref[k]
        o_ref[...] = acc

    return pl.pallas_call(
        body, name=name, out_shape=jax.ShapeDtypeStruct((R, C), F32),
        grid=(R // tr,),
        in_specs=[pl.BlockSpec((N, tr, C), lambda i: (0, i, 0))],
        out_specs=pl.BlockSpec((tr, C), lambda i: (i, 0)),
        compiler_params=_params("parallel"),
    )(buf)


def _pair_sum(pos, g, rsib, name):
    _, hr, hc = rsib.shape
    tr, tc = _pick(hr, 512, 16), _pick(hc, 1024)

    def body(p_ref, g_ref, r_ref, o_ref):
        o_ref[...] = (g_ref[...].astype(F32) + r_ref[...].astype(F32)).astype(BF16)

    blk = pl.BlockSpec((None, tr, tc), lambda s, i, j, p: (s, i, j))
    return pl.pallas_call(
        body, name=name, out_shape=jax.ShapeDtypeStruct(rsib.shape, BF16),
        grid_spec=pltpu.PrefetchScalarGridSpec(
            num_scalar_prefetch=1, grid=(N_CHIPS, hr // tr, hc // tc),
            in_specs=[pl.BlockSpec((None, None, tr, tc), lambda s, i, j, p: (s, p[1], i, j)), blk],
            out_specs=blk),
        compiler_params=_params("parallel", "parallel", "parallel"),
    )(pos, g, rsib)


def _chip_sum(pos, part, land, name):
    _, sr, sc = land.shape
    tr, tc = _pick(sr, 512, 16), _pick(sc, 1024)

    def body(p_ref, own_ref, l_ref, o_ref):
        acc = own_ref[...].astype(F32)
        for k in range(3):
            acc = acc + l_ref[k].astype(F32)
        o_ref[...] = acc

    return pl.pallas_call(
        body, name=name, out_shape=jax.ShapeDtypeStruct((2, sr, sc), F32),
        grid_spec=pltpu.PrefetchScalarGridSpec(
            num_scalar_prefetch=1, grid=(sr // tr, sc // tc),
            in_specs=[pl.BlockSpec((None, tr, tc), lambda i, j, p: (p[0], i, j)),
                      pl.BlockSpec((3, tr, tc), lambda i, j, p: (0, i, j))],
            out_specs=pl.BlockSpec((None, tr, tc), lambda i, j, p: (p[1], i, j))),
        compiler_params=_params("parallel", "parallel"),
    )(pos, part, land)


def _place():
    x, y, c = lax.axis_index("x"), lax.axis_index("y"), lax.axis_index("c")
    return x, y, c


def _other_chips(x, y):
    return [(1 - x, y), (x, 1 - y), (1 - x, 1 - y)]


def _rcopy(src, dst, ssem, rsem, dev):
    return pltpu.make_async_remote_copy(src_ref=src, dst_ref=dst, send_sem=ssem, recv_sem=rsem,
                                        device_id=dev, device_id_type=MESH)


def _gather_items(bufs):
    return sum(b.shape[0] for b in bufs)


def _gather_walk(full):
    t = 0
    for ref in full:
        for l in range(ref.shape[0]):
            yield t, ref, l
            t += 1


def _gather_start(full, ssem, rsem):
    x, y, c = _place()
    j = 2 * x + y
    for t, ref, l in _gather_walk(full):
        own = ref.at[l, j, c]
        for r, (px, py) in enumerate(_other_chips(x, y)):
            _rcopy(own, own, ssem.at[6 * t + r], rsem.at[6 * t + r], (px, py, c)).start()


def _gather_forward(full, ssem, rsem):
    x, y, c = _place()
    for t, ref, l in _gather_walk(full):
        for r, (px, py) in enumerate(_other_chips(x, y)):
            slab = ref.at[l, 2 * px + py, c]
            _rcopy(slab, slab, ssem.at[6 * t + r], rsem.at[6 * t + r], (px, py, c)).wait_recv()
            _rcopy(slab, slab, ssem.at[6 * t + 3 + r], rsem.at[6 * t + 3 + r], (x, y, 1 - c)).start()


def _gather_finish(full, ssem, rsem):
    x, y, c = _place()
    j = 2 * x + y
    for t, ref, l in _gather_walk(full):
        for r, (px, py) in enumerate(_other_chips(x, y)):
            got = ref.at[l, 2 * px + py, 1 - c]
            _rcopy(got, got, ssem.at[6 * t + 3 + r], rsem.at[6 * t + 3 + r], (x, y, 1 - c)).wait_recv()
    for t, ref, l in _gather_walk(full):
        own = ref.at[l, j, c]
        for r, (px, py) in enumerate(_other_chips(x, y)):
            _rcopy(own, own, ssem.at[6 * t + r], rsem.at[6 * t + r], (px, py, c)).wait_send()
            slab = ref.at[l, 2 * px + py, c]
            _rcopy(slab, slab, ssem.at[6 * t + 3 + r], rsem.at[6 * t + 3 + r], (x, y, 1 - c)).wait_send()


def _land_shape(part):
    return jax.ShapeDtypeStruct((3,) + part.shape[1:], part.dtype)


def _exchange_start(parts, land, ssem, rsem):
    x, y, c = _place()
    for a in range(len(parts)):
        for r, (px, py) in enumerate(_other_chips(x, y)):
            _rcopy(parts[a].at[2 * px + py], land[a].at[r], ssem.at[3 * a + r], rsem.at[3 * a + r],
                   (px, py, c)).start()


def _exchange_finish(parts, land, ssem, rsem):
    x, y, c = _place()
    for a in range(len(parts)):
        for r, (px, py) in enumerate(_other_chips(x, y)):
            _rcopy(parts[a].at[2 * px + py], land[a].at[r], ssem.at[3 * a + r], rsem.at[3 * a + r],
                   (px, py, c)).wait()


_Hosted = collections.namedtuple("_Hosted", "ins outs alias sems start finish")


def _host_join(*hosts):
    ins, outs, alias, sems, spans = [], [], {}, [], []
    for h in hosts:
        spans.append((h, len(ins), len(outs), len(sems)))
        alias.update({len(ins) + a: len(outs) + b for a, b in h.alias.items()})
        ins, outs, sems = ins + list(h.ins), outs + list(h.outs), sems + list(h.sems)

    def each(step):
        def run(i, o, s):
            for h, a, b, c in spans:
                getattr(h, step)(i[a:a + len(h.ins)], o[b:b + len(h.outs)], s[c:c + len(h.sems)])
        return run

    return _Hosted(ins, outs, alias, sems, each("start"), each("finish"))


def _host_exchange(parts):
    n = len(parts)
    return _Hosted(list(parts), [_land_shape(p) for p in parts], {}, [3 * n, 3 * n],
                   lambda ins, outs, sems: _exchange_start(ins, outs, *sems),
                   lambda ins, outs, sems: _exchange_finish(ins, outs, *sems))


def _host_gather(bufs):
    T = _gather_items(bufs)

    def finish(ins, outs, sems):
        _gather_forward(outs, *sems)
        _gather_finish(outs, *sems)

    return _Hosted(list(bufs), [jax.ShapeDtypeStruct(b.shape, b.dtype) for b in bufs],
                   {a: a for a in range(len(bufs))}, [6 * T, 6 * T],
                   lambda ins, outs, sems: _gather_start(outs, *sems), finish)


def _host_all_devices(buf):
    return _Hosted([buf], [jax.ShapeDtypeStruct((N_DEV,) + buf.shape, buf.dtype)], {}, [N_DEV - 1, N_DEV - 1, 1],
                   lambda ins, outs, sems: _all_devices_start(ins[0], outs[0], *sems),
                   lambda ins, outs, sems: _all_devices_finish(ins[0], outs[0], *sems))


def _allgather_big(bufs, name):
    n = len(bufs)
    T = _gather_items(bufs)

    def body(*refs):
        full = refs[n:2 * n]
        ssem, rsem = refs[2 * n:]
        _gather_start(full, ssem, rsem)
        _gather_forward(full, ssem, rsem)
        _gather_finish(full, ssem, rsem)

    return pl.pallas_call(
        body, name=name, out_shape=tuple(jax.ShapeDtypeStruct(b.shape, BF16) for b in bufs),
        in_specs=[ANY] * n, out_specs=tuple([ANY] * n),
        input_output_aliases={a: a for a in range(n)},
        scratch_shapes=[pltpu.SemaphoreType.DMA((6 * T,)), pltpu.SemaphoreType.DMA((6 * T,))],
    )(*bufs)


def _allgather_small(shards, name):
    n = len(shards)
    outs = tuple(jax.ShapeDtypeStruct((N_CHIPS,) + s.shape, s.dtype) for s in shards)

    def body(*refs):
        ins, full = refs[:n], refs[n:2 * n]
        ssem, rsem, lsem = refs[2 * n:]
        x, y, c = _place()
        j = 2 * x + y
        chips = _other_chips(x, y)
        cps, locs = [], []
        for a in range(n):
            loc = pltpu.make_async_copy(ins[a], full[a].at[j], lsem.at[a])
            loc.start()
            locs.append(loc)
            for r, (px, py) in enumerate(chips):
                cp = _rcopy(ins[a], full[a].at[j], ssem.at[3 * a + r], rsem.at[3 * a + r], (px, py, c))
                cp.start()
                cps.append(cp)
        for a in range(n):
            for r, (px, py) in enumerate(chips):
                dst = full[a].at[2 * px + py]
                _rcopy(dst, dst, ssem.at[3 * a + r], rsem.at[3 * a + r], (px, py, c)).wait_recv()
        for cp in cps:
            cp.wait_send()
        for loc in locs:
            loc.wait()

    return pl.pallas_call(
        body, name=name, out_shape=outs,
        in_specs=[ANY] * n, out_specs=tuple([ANY] * n),
        scratch_shapes=[pltpu.SemaphoreType.DMA((3 * n,)), pltpu.SemaphoreType.DMA((3 * n,)),
                        pltpu.SemaphoreType.DMA((n,))],
    )(*shards)


def _pair_exchange(grads, name):
    n = len(grads)
    outs = [jax.ShapeDtypeStruct((N_CHIPS,) + g.shape[2:], BF16) for g in grads]

    def body(*refs):
        ins, got = refs[:n], refs[n:2 * n]
        ssem, rsem = refs[2 * n:]
        x, y, c = _place()
        cps = []
        for a in range(n):
            for s in range(N_CHIPS):
                cp = _rcopy(ins[a].at[s, 1 - c], got[a].at[s], ssem.at[N_CHIPS * a + s],
                            rsem.at[N_CHIPS * a + s], (x, y, 1 - c))
                cp.start()
                cps.append(cp)
        for cp in cps:
            cp.wait()

    return pl.pallas_call(
        body, name=name, out_shape=tuple(outs),
        in_specs=[ANY] * n, out_specs=tuple([ANY] * n),
        scratch_shapes=[pltpu.SemaphoreType.DMA((N_CHIPS * n,)), pltpu.SemaphoreType.DMA((N_CHIPS * n,))],
    )(*grads)


def _chip_exchange(parts, name):
    n = len(parts)

    def body(*refs):
        ins, land = refs[:n], refs[n:2 * n]
        ssem, rsem = refs[2 * n:]
        _exchange_start(ins, land, ssem, rsem)
        _exchange_finish(ins, land, ssem, rsem)

    return pl.pallas_call(
        body, name=name, out_shape=tuple(_land_shape(p) for p in parts),
        in_specs=[ANY] * n, out_specs=tuple([ANY] * n),
        scratch_shapes=[pltpu.SemaphoreType.DMA((3 * n,)), pltpu.SemaphoreType.DMA((3 * n,))],
    )(*parts)


def _half_swap(bufs, name):
    n = len(bufs)

    def body(*refs):
        full = refs[n:2 * n]
        ssem, rsem = refs[2 * n:]
        x, y, c = _place()
        cps = []
        for t in range(n):
            mine = full[t].at[c]
            cp = _rcopy(mine, mine, ssem.at[t], rsem.at[t], (x, y, 1 - c))
            cp.start()
            cps.append(cp)
        for t in range(n):
            got = full[t].at[1 - c]
            _rcopy(got, got, ssem.at[t], rsem.at[t], (x, y, 1 - c)).wait_recv()
        for cp in cps:
            cp.wait_send()

    return pl.pallas_call(
        body, name=name, out_shape=tuple(jax.ShapeDtypeStruct(b.shape, F32) for b in bufs),
        in_specs=[ANY] * n, out_specs=tuple([ANY] * n),
        input_output_aliases={a: a for a in range(n)},
        scratch_shapes=[pltpu.SemaphoreType.DMA((n,)), pltpu.SemaphoreType.DMA((n,))],
    )(*bufs)


def _flipped(x, y, c, m):
    fx, fy, fc = (m >> 2) & 1, (m >> 1) & 1, m & 1
    return x + fx - 2 * x * fx, y + fy - 2 * y * fy, c + fc - 2 * c * fc


def _all_devices_start(b_ref, o_ref, ssem, rsem, lsem):
    x, y, c = _place()
    me = 4 * x + 2 * y + c
    pltpu.make_async_copy(b_ref, o_ref.at[me], lsem.at[0]).start()
    for m in range(1, N_DEV):
        _rcopy(b_ref, o_ref.at[me], ssem.at[m - 1], rsem.at[m - 1], _flipped(x, y, c, m)).start()


def _all_devices_finish(b_ref, o_ref, ssem, rsem, lsem):
    x, y, c = _place()
    me = 4 * x + 2 * y + c
    for m in range(1, N_DEV):
        px, py, pc = _flipped(x, y, c, m)
        got = o_ref.at[4 * px + 2 * py + pc]
        _rcopy(got, got, ssem.at[m - 1], rsem.at[m - 1], (px, py, pc)).wait_recv()
    for m in range(1, N_DEV):
        _rcopy(b_ref, o_ref.at[me], ssem.at[m - 1], rsem.at[m - 1], _flipped(x, y, c, m)).wait_send()
    pltpu.make_async_copy(b_ref, o_ref.at[me], lsem.at[0]).wait()


def _pack(arrs):
    flat = jnp.concatenate([a.reshape(-1) for a in arrs])
    rows = -(-flat.shape[0] // (8 * LANES)) * 8
    flat = jnp.pad(flat, (0, rows * LANES - flat.shape[0]))
    return flat.reshape(rows, LANES)


def _unpack(buf, shapes):
    flat = buf.reshape(-1)
    out, off = [], 0
    for s in shapes:
        size = 1
        for d in s:
            size *= d
        out.append(flat[off:off + size].reshape(s))
        off += size
    return out


BIG = ("ev_w_in", "ev_w_out", "od_w_in", "od_w_out", "ffn_w_up", "ffn_w_down")
BIG_KIND = {"ev_w_in": "col", "ev_w_out": "row", "od_w_in": "col", "od_w_out": "row",
            "ffn_w_up": "col", "ffn_w_down": "row"}
SMALL_AXIS = {"ev_dw_w": 2, "ev_dw_b": None, "ev_bn_g": None, "ev_bn_b": None, "od_conv_w": 2,
              "od_pool_w": 2, "od_pool_scale": 1, "ffn_conv_w": 2, "ffn_conv_b": None, "ln_g": 2, "ln_b": 2}
WEIGHTS = ("ev_w_in", "ev_dw_w", "ev_dw_b", "ev_bn_g", "ev_bn_b", "ev_w_out", "od_w_in", "od_conv_w",
           "od_pool_w", "od_pool_scale", "od_w_out", "ffn_w_up", "ffn_conv_w", "ffn_conv_b", "ffn_w_down",
           "ln_g", "ln_b")


def _ffn_fwd(xb, w_up, w_down, conv_w, conv_b, tag, host_up=None, host_down=None):
    hu = _matmul(xb, w_up, mode="nn", b_lead=0, b_split=True, out_dtype=BF16, name=f"{tag}_up", tm=1024, tn=1408,
                 hosted=host_up)
    hu, up_outs = hu if host_up is not None else (hu, None)
    z = _ffn_act_fwd(hu, conv_w, conv_b, name=f"{tag}_act")
    y = _matmul(z, w_down, mode="nn", b_lead=0, out_dtype=F32, name=f"{tag}_down", tm=512, tn=1024,
                hosted=host_down)
    y, down_outs = y if host_down is not None else (y, None)
    return hu, z, y, up_outs, down_outs


def _ffn_bwd(drb, dr, alpha, xb, hu, z, w_up, w_down, conv_w, conv_b, tag, host_dwup=None, host_dx=None):
    g_down = _matmul(z, drb, mode="tn", out_dtype=BF16, name=f"{tag}_dwdown", tm=512, tn=1024)
    dz = _matmul(drb, w_down, mode="nt", b_lead=0, out_dtype=BF16, name=f"{tag}_dz", tm=1024, tn=1408)
    dg, du, dcw, dcb = _ffn_act_bwd(dz, hu, conv_w, conv_b, name=f"{tag}_actbwd")
    g_up = _matmul(xb, (dg, du), mode="tn", out_split=True, out_dtype=BF16, name=f"{tag}_dwup", tm=512, tn=1408,
                   hosted=host_dwup)
    g_up, dwup_outs = g_up if host_dwup is not None else (g_up, None)
    dx = _matmul((dg, du), w_up, mode="nt", b_lead=0, b_split=True, out_dtype=F32, add=dr, add_scale=alpha,
                 name=f"{tag}_dx", tm=512, tn=512, hosted=host_dx)
    dx, dx_outs = dx if host_dx is not None else (dx, None)
    return dx, g_up, g_down, dcw, dcb, dwup_outs, dx_outs


def kernel(x, ev_w_in, ev_dw_w, ev_dw_b, ev_bn_g, ev_bn_b, ev_w_out, od_w_in, od_conv_w, od_pool_w, od_pool_scale, od_w_out, ffn_w_up, ffn_conv_w, ffn_conv_b, ffn_w_down, ln_g, ln_b, loss_target, m_ev_w_in, m_ev_dw_w, m_ev_dw_b, m_ev_bn_g, m_ev_bn_b, m_ev_w_out, m_od_w_in, m_od_conv_w, m_od_pool_w, m_od_pool_scale, m_od_w_out, m_ffn_w_up, m_ffn_conv_w, m_ffn_conv_b, m_ffn_w_down, m_ln_g, m_ln_b, v_ev_w_in, v_ev_dw_w, v_ev_dw_b, v_ev_bn_g, v_ev_bn_b, v_ev_w_out, v_od_w_in, v_od_conv_w, v_od_pool_w, v_od_pool_scale, v_od_w_out, v_ffn_w_up, v_ffn_conv_w, v_ffn_conv_b, v_ffn_w_down, v_ln_g, v_ln_b):
    wts = dict(ev_w_in=ev_w_in, ev_dw_w=ev_dw_w, ev_dw_b=ev_dw_b, ev_bn_g=ev_bn_g, ev_bn_b=ev_bn_b,
               ev_w_out=ev_w_out, od_w_in=od_w_in, od_conv_w=od_conv_w, od_pool_w=od_pool_w,
               od_pool_scale=od_pool_scale, od_w_out=od_w_out, ffn_w_up=ffn_w_up, ffn_conv_w=ffn_conv_w,
               ffn_conv_b=ffn_conv_b, ffn_w_down=ffn_w_down, ln_g=ln_g, ln_b=ln_b)
    mom = dict(ev_w_in=m_ev_w_in, ev_dw_w=m_ev_dw_w, ev_dw_b=m_ev_dw_b, ev_bn_g=m_ev_bn_g, ev_bn_b=m_ev_bn_b,
               ev_w_out=m_ev_w_out, od_w_in=m_od_w_in, od_conv_w=m_od_conv_w, od_pool_w=m_od_pool_w,
               od_pool_scale=m_od_pool_scale, od_w_out=m_od_w_out, ffn_w_up=m_ffn_w_up, ffn_conv_w=m_ffn_conv_w,
               ffn_conv_b=m_ffn_conv_b, ffn_w_down=m_ffn_w_down, ln_g=m_ln_g, ln_b=m_ln_b)
    var = dict(ev_w_in=v_ev_w_in, ev_dw_w=v_ev_dw_w, ev_dw_b=v_ev_dw_b, ev_bn_g=v_ev_bn_g, ev_bn_b=v_ev_bn_b,
               ev_w_out=v_ev_w_out, od_w_in=v_od_w_in, od_conv_w=v_od_conv_w, od_pool_w=v_od_pool_w,
               od_pool_scale=v_od_pool_scale, od_w_out=v_od_w_out, ffn_w_up=v_ffn_w_up, ffn_conv_w=v_ffn_conv_w,
               ffn_conv_b=v_ffn_conv_b, ffn_w_down=v_ffn_w_down, ln_g=v_ln_g, ln_b=v_ln_b)

    S, D = x.shape[1], x.shape[2]
    depth = ln_g.shape[0]
    alpha = (2.0 * depth) ** 0.25
    A = ev_dw_b.shape[-1]
    n_heads = A // HEAD_DIM
    xi, yi, ci = _place()
    chip = 2 * xi + yi
    pos = jnp.stack([chip, ci]).astype(jnp.int32)

    bufs = {f"{k}{l}": _cast_into_gather(pos, wts[k], l, name=f"cast_{k}{l}")
            for k in BIG for l in range(wts[k].shape[0])}

    def whole(key):
        _, r, c = wts[key[:-1]].shape
        col = BIG_KIND[key[:-1]] == "col"
        return bufs[key].reshape(1, N_CHIPS, r, c) if col else bufs[key].reshape(1, N_CHIPS * r, c)

    full = {}

    def gathered_now(keys, arrays):
        bufs.update(zip(keys, arrays))
        full.update({key: whole(key) for key in keys})

    early = ("ev_w_in0", "ev_w_out0")
    under_attn = ("ffn_w_up0", "ffn_w_down0", "od_w_in0", "od_w_out0")
    gathered_now(early, _allgather_big([bufs[k] for k in early], name="gather_first"))
    small_sharded = [k for k in WEIGHTS if k not in BIG and SMALL_AXIS[k] is not None]
    gathered = _allgather_small([wts[k] for k in small_sharded], name="gather_small")
    sm = {k: wts[k] for k in WEIGHTS if k not in BIG and SMALL_AXIS[k] is None}
    for k, g4 in zip(small_sharded, gathered):
        sm[k] = jnp.concatenate([g4[t] for t in range(N_CHIPS)], axis=SMALL_AXIS[k])
    pool_w_bf = sm["od_pool_w"][0].astype(BF16)

    x0 = x[0]
    x0b = _cast_bf16(x, name="cast_x")[0]
    h0 = _matmul(x0b, full["ev_w_in0"], mode="nn", b_lead=0, b_split=True, out_dtype=BF16, name="ev_in",
                 tm=1024, tn=1280)
    o_a, tot, *rest = _attn_fwd(h0, n_heads, name="attn_fwd", gather=[bufs[k] for k in under_attn])
    gathered_now(under_attn, rest)
    u1, u3 = _evenconv_fwd(h0, sm["ev_dw_w"][0], sm["ev_dw_b"], sm["ev_bn_g"], sm["ev_bn_b"], name="evconv_fwd")
    mix0 = jnp.concatenate([o_a, u3], axis=1)
    y1 = _matmul(mix0, full["ev_w_out0"], mode="nn", b_lead=0, out_dtype=F32, name="ev_out", tm=1024, tn=1024)
    x1, x1b, xh1, rs1 = _ln_fwd(x0, y1, sm["ln_g"][0, 0][None], sm["ln_b"][0, 0][None], alpha, name="ln00")
    hu0, z0, y2, got_up, got_down = _ffn_fwd(
        x1b, full["ffn_w_up0"], full["ffn_w_down0"], sm["ffn_conv_w"][0], sm["ffn_conv_b"][0][None], "ffn0",
        host_up=_host_gather([bufs["ffn_w_up1"]]), host_down=_host_gather([bufs["ffn_w_down1"]]))
    gathered_now(("ffn_w_up1",), got_up)
    gathered_now(("ffn_w_down1",), got_down)
    x2, x2b, xh2, rs2 = _ln_fwd(x1, y2, sm["ln_g"][0, 1][None], sm["ln_b"][0, 1][None], alpha, name="ln01")
    h1 = _matmul(x2b, full["od_w_in0"], mode="nn", b_lead=0, b_split=True, out_dtype=BF16, name="od_in",
                 tm=1024, tn=1024)
    mix1 = _odd_fwd(h1, sm["od_conv_w"][0], pool_w_bf, sm["od_pool_scale"], name="odd_fwd")
    y3 = _matmul(mix1, full["od_w_out0"], mode="nn", b_lead=0, out_dtype=F32, name="od_out", tm=1024, tn=1024)
    x3, x3b, xh3, rs3 = _ln_fwd(x2, y3, sm["ln_g"][1, 0][None], sm["ln_b"][1, 0][None], alpha, name="ln10")
    hu1, z1, y4, _, _ = _ffn_fwd(x3b, full["ffn_w_up1"], full["ffn_w_down1"], sm["ffn_conv_w"][1],
                                 sm["ffn_conv_b"][1][None], "ffn1")
    x4, _, xh4, rs4 = _ln_fwd(x3, y4, sm["ln_g"][1, 1][None], sm["ln_b"][1, 1][None], alpha, name="ln11")

    dx4, loss_part = _loss_grad(x4, loss_target[0], name="loss")
    loss = lax.psum(loss_part[0, 0], ("x", "y", "c"))

    def pair_reduce(named, tag):
        g4 = []
        for k, g in named:
            rows, cols = (g.shape[1], g.shape[2]) if BIG_KIND[k] == "col" else (g.shape[0] // N_CHIPS, g.shape[1])
            g4.append(g.reshape(N_CHIPS, 2, rows // 2, cols))
        sib = _pair_exchange(g4, name=f"grad_pair_exchange_{tag}")
        return [_pair_sum(pos, g, r, name=f"grad_pair_sum_{tag}{t}") for t, (g, r) in enumerate(zip(g4, sib))]

    dr4, dr4b, dg11, db11 = _ln_bwd(dx4, xh4, rs4, sm["ln_g"][1, 1][None], name="ln11_bwd")
    dx3, g_up1, g_down1, dcw1, dcb1, _, _ = _ffn_bwd(dr4b, dr4, alpha, x3b, hu1, z1, full["ffn_w_up1"],
                                                     full["ffn_w_down1"], sm["ffn_conv_w"][1],
                                                     sm["ffn_conv_b"][1][None], "ffn1")
    parts_f1 = pair_reduce([("ffn_w_up", g_up1), ("ffn_w_down", g_down1)], "f1")
    dr3, dr3b, dg10, db10 = _ln_bwd(dx3, xh3, rs3, sm["ln_g"][1, 0][None], name="ln10_bwd")
    g_odout = _matmul(mix1, dr3b, mode="tn", out_dtype=BF16, name="od_dwout", tm=512, tn=1024)
    dmix1 = _matmul(dr3b, full["od_w_out0"], mode="nt", b_lead=0, out_dtype=BF16, name="od_dmix", tm=1024, tn=1024)
    dh1, d_odconv, d_pool, d_pscale = _odd_bwd(dmix1, h1, sm["od_conv_w"][0], pool_w_bf, sm["od_pool_scale"],
                                               name="odd_bwd")
    g_odin = _matmul(x2b, dh1, mode="tn", out_split=True, out_dtype=BF16, name="od_dwin", tm=512, tn=1024)
    dx2 = _matmul(dh1, full["od_w_in0"], mode="nt", b_lead=0, b_split=True, out_dtype=F32, add=dr3, add_scale=alpha,
                  name="od_dx", tm=1024, tn=512)
    dr2, dr2b, dg01, db01 = _ln_bwd(dx2, xh2, rs2, sm["ln_g"][0, 1][None], name="ln01_bwd")
    dx1, g_up0, g_down0, dcw0, dcb0, land_up1, land_down1 = _ffn_bwd(
        dr2b, dr2, alpha, x1b, hu0, z0, full["ffn_w_up0"], full["ffn_w_down0"], sm["ffn_conv_w"][0],
        sm["ffn_conv_b"][0][None], "ffn0",
        host_dwup=_host_exchange(parts_f1[:1]), host_dx=_host_exchange(parts_f1[1:]))
    dr1, dr1b, dg00, db00 = _ln_bwd(dx1, xh1, rs1, sm["ln_g"][0, 0][None], name="ln00_bwd")
    g_evout = _matmul(mix0, dr1b, mode="tn", out_dtype=BF16, name="ev_dwout", tm=512, tn=1024)
    dmix0 = _matmul(dr1b, full["ev_w_out0"], mode="nt", b_lead=0, out_dtype=BF16, name="ev_dmix", tm=1024, tn=1024)
    da, dgate, d_dww, d_dwb, d_bng, d_bnb = _evenconv_bwd(dmix0, u1, h0, sm["ev_dw_w"][0], sm["ev_bn_g"],
                                                          sm["ev_bn_b"], name="evconv_bwd")
    parts_b = pair_reduce([("od_w_in", g_odin), ("od_w_out", g_odout), ("ffn_w_up", g_up0),
                           ("ffn_w_down", g_down0), ("ev_w_out", g_evout)], "b")

    d_ln_g = jnp.stack([jnp.stack([dg00[0], dg01[0]]), jnp.stack([dg10[0], dg11[0]])])
    d_ln_b = jnp.stack([jnp.stack([db00[0], db01[0]]), jnp.stack([db10[0], db11[0]])])
    small_partial = {
        "ev_dw_w": d_dww[None], "ev_dw_b": d_dwb, "ev_bn_g": d_bng, "ev_bn_b": d_bnb,
        "od_conv_w": d_odconv[None], "od_pool_w": d_pool[None], "od_pool_scale": d_pscale,
        "ffn_conv_w": jnp.stack([dcw0, dcw1]), "ffn_conv_b": jnp.concatenate([dcb0, dcb1], axis=0),
        "ln_g": d_ln_g, "ln_b": d_ln_b}
    small_names = [k for k in WEIGHTS if k not in BIG]
    packed = _pack([small_partial[k] for k in small_names])
    dq, dk, dv, under_bwd = _attn_bwd(h0, dmix0, tot, n_heads, name="attn_bwd",
                                      hosted=_host_join(_host_exchange(parts_b), _host_all_devices(packed)))
    land_b, all_small = under_bwd[:-1], under_bwd[-1]
    dh0 = jnp.concatenate([dq, dk, dv, da, dgate], axis=1)
    g_evin = _matmul(x0b, dh0, mode="tn", out_split=True, out_dtype=BF16, name="ev_dwin", tm=512, tn=1280)
    parts_e = pair_reduce([("ev_w_in", g_evin)], "e")
    grad_x, land_e = _matmul(dh0, full["ev_w_in0"], mode="nt", b_lead=0, b_split=True, out_dtype=F32, add=dr1,
                             add_scale=alpha, name="ev_dx", tm=1024, tn=512, hosted=_host_exchange(parts_e))

    order = ["ffn_w_up1", "ffn_w_down1", "od_w_in0", "od_w_out0", "ffn_w_up0", "ffn_w_down0", "ev_w_out0", "ev_w_in0"]
    parts = parts_f1 + parts_b + parts_e
    land = list(land_up1) + list(land_down1) + list(land_b) + list(land_e)
    halves = [_chip_sum(pos, p, ld, name=f"grad_chip_sum_{tag}") for tag, p, ld in zip(order, parts, land)]
    reduced = dict(zip(order, _half_swap(halves, name="grad_half_swap")))
    big_grads = {k: [reduced[f"{k}{l}"].reshape(wts[k].shape[1:]) for l in range(wts[k].shape[0])] for k in BIG}

    summed = _sum_slots(all_small, name="sum_small_grads")
    small_full = dict(zip(small_names, _unpack(summed, [small_partial[k].shape for k in small_names])))
    small_grads = {}
    for k in small_names:
        ax = SMALL_AXIS[k]
        if ax is None:
            small_grads[k] = small_full[k]
        else:
            size = wts[k].shape[ax]
            small_grads[k] = lax.dynamic_slice_in_dim(small_full[k], chip * size, size, axis=ax)

    grads, delta, new_m, new_v = {}, {}, {}, {}
    for k in BIG:
        grads[k], delta[k], new_m[k], new_v[k] = _adamw(wts[k], big_grads[k], mom[k], var[k], name=f"adamw_{k}")
    shapes = [wts[k].shape for k in small_names]
    pw, pg, pm, pv = (_pack([d[k] for k in small_names]) for d in (wts, small_grads, mom, var))
    sg, sd, smn, svn = _adamw(pw[None], [pg], pm[None], pv[None], name="adamw_small")
    for dst, buf in ((grads, sg), (delta, sd), (new_m, smn), (new_v, svn)):
        for k, a in zip(small_names, _unpack(buf[0], shapes)):
            dst[k] = a

    return (loss, grad_x[None], *[grads[k] for k in WEIGHTS], *[delta[k] for k in WEIGHTS],
            *[new_m[k] for k in WEIGHTS], *[new_v[k] for k in WEIGHTS])
```

```python
import collections

import jax
import jax.numpy as jnp
from jax import lax
from jax.experimental import pallas as pl
from jax.experimental.pallas import tpu as pltpu

F32 = jnp.float32
BF16 = jnp.bfloat16

HEAD_DIM = 128
POOL_WINDOWS = (2, 4, 8, 16)
LN_EPS = 1e-5
ADAM_LR = 0.001
ADAM_B1 = 0.9
ADAM_B2 = 0.999
ADAM_EPS = 1e-08
ADAM_WD = 0.01
ADAM_STEP = 10
N_CHIPS = 4
N_DEV = 8
MESH = pl.DeviceIdType.MESH
LANES = 128
HALO3 = 16
HALO31 = 32

ANY = pl.BlockSpec(memory_space=pl.ANY)


def _pick(n, pref, mult=LANES):
    if n <= pref:
        return n
    t = (pref // mult) * mult
    while t >= mult:
        if n % t == 0:
            return t
        t -= mult
    return n


def _params(*sem):
    return pltpu.CompilerParams(dimension_semantics=sem)


def _matmul(a, b, *, mode, out_dtype, name, b_lead=None, b_split=False, out_split=False, add=None,
            add_scale=1.0, tm=512, tn=1024, tk=None, hosted=None):
    halves = isinstance(a, tuple) or isinstance(b, tuple)
    if isinstance(a, tuple):
        assert mode == "nt" and b_split and tk is None
        ash = (a[0].shape[0], 2 * a[0].shape[1])
    else:
        ash = a.shape[-2:]
    if isinstance(b, tuple):
        assert mode == "tn" and tk is None
        bsh = (b[0].shape[0], 2 * b[0].shape[1])
    else:
        bsh = b.shape[-2:]
    if mode == "nn":
        (M, K), (K2, N) = ash, bsh
        if b_split:
            N = N * N_CHIPS
    elif mode == "nt":
        (M, K), (N, K2) = ash, bsh
        if b_split:
            K2 = K2 * N_CHIPS
    else:
        (K, M), (K2, N) = ash, bsh
    assert K == K2, (ash, bsh, mode)
    tm = _pick(M, tm)
    tn = _pick(N // N_CHIPS if (out_split or (b_split and mode == "nn")) else N, tn)
    whole_split_k = b_split and mode == "nt" and tk is None
    if tk is None:
        tk = K
    else:
        tk = _pick(K // N_CHIPS if (b_split and mode == "nt") else K, tk)
    nk = K // tk
    kq = K // N_CHIPS
    n_per = (N // N_CHIPS) // tn
    k_per = (K // N_CHIPS) // tk

    def lead(shape, idx):
        if b_lead is None:
            return pl.BlockSpec(shape, idx)
        return pl.BlockSpec((None,) + shape, lambda i, j, k: (b_lead,) + idx(i, j, k))

    if mode == "nn":
        a_spec = pl.BlockSpec((tm, tk), lambda i, j, k: (i, k))
        if b_split:
            b_spec = lead((None, tk, tn), lambda i, j, k: (lax.div(j, n_per), k, lax.rem(j, n_per)))
        else:
            b_spec = lead((tk, tn), lambda i, j, k: (k, j))
        dims = (((1,), (0,)), ((), ()))
    elif mode == "nt":
        a_spec = pl.BlockSpec((tm, tk), lambda i, j, k: (i, k))
        if whole_split_k:
            b_spec = lead((N_CHIPS, tn, kq), lambda i, j, k: (0, j, 0))
        elif b_split:
            b_spec = lead((None, tn, tk), lambda i, j, k: (lax.div(k, k_per), j, lax.rem(k, k_per)))
        else:
            b_spec = lead((tn, tk), lambda i, j, k: (j, k))
        dims = (((1,), (1,)), ((), ()))
    else:
        a_spec = pl.BlockSpec((tk, tm), lambda i, j, k: (k, i))
        b_spec = pl.BlockSpec((tk, tn), lambda i, j, k: (k, j))
        dims = (((0,), (0,)), ((), ()))
    if out_split:
        out_shape = jax.ShapeDtypeStruct((N_CHIPS, M, N // N_CHIPS), out_dtype)
        out_spec = pl.BlockSpec((None, tm, tn), lambda i, j, k: (lax.div(j, n_per), i, lax.rem(j, n_per)))
    else:
        out_shape = jax.ShapeDtypeStruct((M, N), out_dtype)
        out_spec = pl.BlockSpec((tm, tn), lambda i, j, k: (i, j))
    grid = (M // tm, N // tn, nk)
    nj_half = grid[1] // 2
    if isinstance(a, tuple):
        in_specs = [pl.BlockSpec((tm, K // 2), lambda i, j, k: (i, 0))] * 2 + [b_spec]
        args = [a[0], a[1], b]
    elif isinstance(b, tuple):
        in_specs = [a_spec,
                    pl.BlockSpec((tk, tn), lambda i, j, k: (k, jnp.minimum(j, nj_half - 1))),
                    pl.BlockSpec((tk, tn), lambda i, j, k: (k, jnp.maximum(j - nj_half, 0)))]
        args = [a, b[0], b[1]]
    else:
        in_specs = [a_spec, b_spec]
        args = [a, b]
    n_op = len(args)
    if add is not None:
        in_specs.append(pl.BlockSpec((tm, tn), lambda i, j, k: (i, j)))
        args.append(add)

    n_in = len(args)
    h_in = 0 if hosted is None else len(hosted.ins)
    h_out = 0 if hosted is None else len(hosted.outs)

    def body(*refs):
        ops = refs[:n_op]
        add_ref = refs[n_op] if add is not None else None
        h_ins = refs[n_in:n_in + h_in]
        o_ref = refs[n_in + h_in]
        h_outs = refs[n_in + h_in + 1:n_in + h_in + 1 + h_out]
        scr = refs[n_in + h_in + 1 + h_out:]
        i, j, k = pl.program_id(0), pl.program_id(1), pl.program_id(2)
        if hosted is not None:
            sems = scr[len(scr) - len(hosted.sems):]

            @pl.when((i == 0) & (j == 0) & (k == 0))
            def _():
                hosted.start(h_ins, h_outs, sems)

        def finish(res):
            if add_ref is not None:
                res = res + add_scale * add_ref[...]
            o_ref[...] = res.astype(out_dtype)

        def dot(x, y):
            return lax.dot_general(x, y, dims, preferred_element_type=F32)

        if isinstance(b, tuple):
            @pl.when(j < nj_half)
            def _():
                finish(dot(ops[0][...], ops[1][...]))

            @pl.when(j >= nj_half)
            def _():
                finish(dot(ops[0][...], ops[2][...]))
            part = None
        elif whole_split_k:
            srcs = [(ops[0], s) for s in range(N_CHIPS)] if not isinstance(a, tuple) else \
                   [(ops[s // 2], s % 2) for s in range(N_CHIPS)]
            b_ref = ops[-1]
            part = None
            for s, (src, off) in enumerate(srcs):
                term = dot(src[:, off * kq:(off + 1) * kq], b_ref[s])
                part = term if part is None else part + term
        else:
            part = dot(ops[0][...], ops[1][...])

        if part is None:
            pass
        elif nk == 1:
            finish(part)
        else:
            acc = scr[0]

            @pl.when(k == 0)
            def _():
                acc[...] = part

            @pl.when(k > 0)
            def _():
                acc[...] += part

            @pl.when(k == nk - 1)
            def _():
                finish(acc[...])

        if hosted is not None:
            @pl.when((i == grid[0] - 1) & (j == grid[1] - 1) & (k == nk - 1))
            def _():
                hosted.finish(h_ins, h_outs, sems)

    scratch = [pltpu.VMEM((tm, tn), F32)] if nk > 1 else []
    if hosted is not None:
        res = pl.pallas_call(
            body, name=name,
            out_shape=(out_shape,) + tuple(hosted.outs),
            grid=grid,
            in_specs=in_specs + [ANY] * h_in,
            out_specs=(out_spec,) + (ANY,) * h_out,
            input_output_aliases={n_in + src: 1 + dst for src, dst in hosted.alias.items()},
            scratch_shapes=scratch + [pltpu.SemaphoreType.DMA((n,)) for n in hosted.sems],
            compiler_params=_params("arbitrary", "arbitrary", "arbitrary"),
        )(*args, *hosted.ins)
        return res[0], list(res[1:])
    return pl.pallas_call(
        body, name=name,
        out_shape=out_shape,
        grid=grid,
        in_specs=in_specs,
        out_specs=out_spec,
        scratch_shapes=scratch,
        compiler_params=_params("parallel", "parallel", "arbitrary"),
    )(*args)


def _cast_bf16(w, name):
    L, R, C = w.shape
    tr, tc = _pick(R, 512, 16), _pick(C, 1408)

    def body(w_ref, o_ref):
        o_ref[...] = w_ref[...].astype(BF16)

    return pl.pallas_call(
        body, name=name, out_shape=jax.ShapeDtypeStruct(w.shape, BF16),
        grid=(L, R // tr, C // tc),
        in_specs=[pl.BlockSpec((None, tr, tc), lambda l, i, j: (l, i, j))],
        out_specs=pl.BlockSpec((None, tr, tc), lambda l, i, j: (l, i, j)),
        compiler_params=_params("parallel", "parallel", "parallel"),
    )(w)


def _cast_into_gather(pos, w, layer, name):
    L, R, C = w.shape
    r2 = R // 2
    tr, tc = _pick(r2, 512, 16), _pick(C, 1408)

    def body(p_ref, w_ref, o_ref):
        o_ref[...] = w_ref[...].astype(BF16)

    return pl.pallas_call(
        body, name=name, out_shape=jax.ShapeDtypeStruct((1, N_CHIPS, 2, r2, C), BF16),
        grid_spec=pltpu.PrefetchScalarGridSpec(
            num_scalar_prefetch=1, grid=(2, r2 // tr, C // tc),
            in_specs=[pl.BlockSpec((None, None, tr, tc), lambda h, i, j, p: (layer, h, i, j))],
            out_specs=pl.BlockSpec((None, None, None, tr, tc), lambda h, i, j, p: (0, p[0], h, i, j))),
        compiler_params=_params("parallel", "parallel", "parallel"),
    )(pos, w.reshape(L, 2, r2, C))


def _sigmoid(v):
    return 0.5 * jnp.tanh(0.5 * v) + 0.5


def _ln_fwd(x, y, g, b, alpha, name):
    S, D = x.shape
    tr = _pick(S, 256, 8)

    def body(x_ref, y_ref, g_ref, b_ref, o_ref, ob_ref, xh_ref, rs_ref):
        r = alpha * x_ref[...] + y_ref[...]
        mu = jnp.mean(r, axis=-1, keepdims=True)
        d = r - mu
        var = jnp.mean(d * d, axis=-1, keepdims=True)
        rstd = lax.rsqrt(var + LN_EPS)
        xh = d * rstd
        o = xh * g_ref[...] + b_ref[...]
        o_ref[...] = o
        ob_ref[...] = o.astype(BF16)
        xh_ref[...] = xh
        rs_ref[...] = rstd

    row = pl.BlockSpec((tr, D), lambda i: (i, 0))
    vec = pl.BlockSpec((1, D), lambda i: (0, 0))
    return pl.pallas_call(
        body, name=name,
        out_shape=(jax.ShapeDtypeStruct((S, D), F32), jax.ShapeDtypeStruct((S, D), BF16),
                   jax.ShapeDtypeStruct((S, D), F32), jax.ShapeDtypeStruct((S, 1), F32)),
        grid=(S // tr,),
        in_specs=[row, row, vec, vec],
        out_specs=(row, row, row, pl.BlockSpec((tr, 1), lambda i: (i, 0))),
        compiler_params=_params("parallel"),
    )(x, y, g, b)


def _ln_bwd(dout, xhat, rstd, g, name):
    S, D = dout.shape
    tr = _pick(S, 256, 8)

    def body(do_ref, xh_ref, rs_ref, g_ref, dr_ref, drb_ref, dg_ref, db_ref):
        i = pl.program_id(0)
        do = do_ref[...]
        xh = xh_ref[...]
        dxh = do * g_ref[...]
        m1 = jnp.mean(dxh, axis=-1, keepdims=True)
        m2 = jnp.mean(dxh * xh, axis=-1, keepdims=True)
        dr = rs_ref[...] * (dxh - m1 - xh * m2)
        dr_ref[...] = dr
        drb_ref[...] = dr.astype(BF16)
        pg = jnp.sum(do * xh, axis=0, keepdims=True)
        pb = jnp.sum(do, axis=0, keepdims=True)

        @pl.when(i == 0)
        def _():
            dg_ref[...] = pg
            db_ref[...] = pb

        @pl.when(i > 0)
        def _():
            dg_ref[...] += pg
            db_ref[...] += pb

    row = pl.BlockSpec((tr, D), lambda i: (i, 0))
    vec = pl.BlockSpec((1, D), lambda i: (0, 0))
    return pl.pallas_call(
        body, name=name,
        out_shape=(jax.ShapeDtypeStruct((S, D), F32), jax.ShapeDtypeStruct((S, D), BF16),
                   jax.ShapeDtypeStruct((1, D), F32), jax.ShapeDtypeStruct((1, D), F32)),
        grid=(S // tr,),
        in_specs=[row, row, pl.BlockSpec((tr, 1), lambda i: (i, 0)), vec],
        out_specs=(row, row, vec, vec),
        compiler_params=_params("arbitrary"),
    )(dout, xhat, rstd, g)


def _loss_grad(y, target, name):
    S, D = y.shape
    tr = _pick(S, 256, 8)
    n = S // tr

    def body(y_ref, t_ref, dy_ref, l_ref, acc):
        i = pl.program_id(0)
        d = y_ref[...] - t_ref[...]
        dy_ref[...] = d * (1.0 / D)
        p = jnp.sum(d * d, axis=0, keepdims=True)

        @pl.when(i == 0)
        def _():
            acc[...] = p

        @pl.when(i > 0)
        def _():
            acc[...] += p

        @pl.when(i == n - 1)
        def _():
            l_ref[...] = (0.5 / D) * jnp.sum(acc[...], axis=1, keepdims=True)

    row = pl.BlockSpec((tr, D), lambda i: (i, 0))
    return pl.pallas_call(
        body, name=name,
        out_shape=(jax.ShapeDtypeStruct((S, D), F32), jax.ShapeDtypeStruct((1, 1), F32)),
        grid=(n,),
        in_specs=[row, row],
        out_specs=(row, pl.BlockSpec((1, 1), lambda i: (0, 0))),
        scratch_shapes=[pltpu.VMEM((1, D), F32)],
        compiler_params=_params("arbitrary"),
    )(y, target)


def _prev_spec(tr, halo, width, col):
    return pl.BlockSpec((halo, width), lambda c, i: (jnp.maximum(i * (tr // halo) - 1, 0), col(c)))


def _next_spec(tr, halo, width, col, nrows):
    last = nrows // halo - 1
    return pl.BlockSpec((halo, width), lambda c, i: (jnp.minimum((i + 1) * (tr // halo), last), col(c)))


def _cur_spec(tr, width, col):
    return pl.BlockSpec((tr, width), lambda c, i: (i, col(c)))


def _ffn_act_fwd(hu, conv_w, conv_b, name):
    S, F2 = hu.shape
    F = F2 // 2
    tr, tc, H = _pick(S, 512, 16), _pick(F, 512), HALO3
    nc, nr = F // tc, S // tr

    def body(gp_ref, g_ref, u_ref, w_ref, b_ref, z_ref, G):
        i = pl.program_id(1)
        G[0:H, :] = jnp.where(i > 0, gp_ref[...].astype(F32), 0.0)
        G[H:H + tr, :] = g_ref[...].astype(F32)
        gc = (b_ref[...] + w_ref[pl.ds(0, 1), :] * G[pl.ds(H - 2, tr), :]
              + w_ref[pl.ds(1, 1), :] * G[pl.ds(H - 1, tr), :] + w_ref[pl.ds(2, 1), :] * G[pl.ds(H, tr), :])
        z = gc * _sigmoid(gc) * u_ref[...].astype(F32)
        z_ref[...] = z.astype(BF16)

    gcol = lambda c: c
    ucol = lambda c: c + nc
    return pl.pallas_call(
        body, name=name, out_shape=jax.ShapeDtypeStruct((S, F), BF16),
        grid=(nc, nr),
        in_specs=[_prev_spec(tr, H, tc, gcol), _cur_spec(tr, tc, gcol), _cur_spec(tr, tc, ucol),
                  pl.BlockSpec((3, tc), lambda c, i: (0, c)), pl.BlockSpec((1, tc), lambda c, i: (0, c))],
        out_specs=pl.BlockSpec((tr, tc), lambda c, i: (i, c)),
        scratch_shapes=[pltpu.VMEM((H + tr, tc), F32)],
        compiler_params=_params("parallel", "parallel"),
    )(hu, hu, hu, conv_w, conv_b)


def _ffn_act_bwd(dz, hu, conv_w, conv_b, name):
    S, F = dz.shape
    tr, tc, H = _pick(S, 512, 16), _pick(F, 512), HALO3
    nc, nr = F // tc, S // tr
    n = tr + H

    def body(dz_ref, dzn_ref, gp_ref, g_ref, gn_ref, u_ref, un_ref, w_ref, b_ref,
             dg_ref, du_ref, dw_ref, db_ref, G, DG):
        i = pl.program_id(1)
        G[0:H, :] = jnp.where(i > 0, gp_ref[...].astype(F32), 0.0)
        G[H:H + tr, :] = g_ref[...].astype(F32)
        G[H + tr:H + tr + H, :] = gn_ref[...].astype(F32)
        w0, w1, w2 = w_ref[pl.ds(0, 1), :], w_ref[pl.ds(1, 1), :], w_ref[pl.ds(2, 1), :]
        gc = b_ref[...] + w0 * G[pl.ds(H - 2, n), :] + w1 * G[pl.ds(H - 1, n), :] + w2 * G[pl.ds(H, n), :]
        sg = _sigmoid(gc)
        dzf = jnp.concatenate([dz_ref[...], dzn_ref[...]], axis=0).astype(F32)
        uf = jnp.concatenate([u_ref[...], un_ref[...]], axis=0).astype(F32)
        rows = lax.broadcasted_iota(jnp.int32, (n, 1), 0)
        dzf = jnp.where((rows < tr) | (i < nr - 1), dzf, 0.0)
        dgc = dzf * uf * (sg * (1.0 + gc * (1.0 - sg)))
        du_ref[...] = (dzf[0:tr] * (gc[0:tr] * sg[0:tr])).astype(BF16)
        DG[...] = dgc
        dg = w2 * DG[pl.ds(0, tr), :] + w1 * DG[pl.ds(1, tr), :] + w0 * DG[pl.ds(2, tr), :]
        dg_ref[...] = dg.astype(BF16)
        dcur = dgc[0:tr]
        pw = [jnp.sum(dcur * G[pl.ds(H - 2 + k, tr), :], axis=0, keepdims=True) for k in range(3)]
        pb = jnp.sum(dcur, axis=0, keepdims=True)

        @pl.when(i == 0)
        def _():
            for k in range(3):
                dw_ref[pl.ds(k, 1), :] = pw[k]
            db_ref[...] = pb

        @pl.when(i > 0)
        def _():
            for k in range(3):
                dw_ref[pl.ds(k, 1), :] += pw[k]
            db_ref[...] += pb

    gcol = lambda c: c
    ucol = lambda c: c + nc
    blk = pl.BlockSpec((tr, tc), lambda c, i: (i, c))
    return pl.pallas_call(
        body, name=name,
        out_shape=(jax.ShapeDtypeStruct((S, F), BF16), jax.ShapeDtypeStruct((S, F), BF16),
                   jax.ShapeDtypeStruct((3, F), F32), jax.ShapeDtypeStruct((1, F), F32)),
        grid=(nc, nr),
        in_specs=[_cur_spec(tr, tc, gcol), _next_spec(tr, H, tc, gcol, S),
                  _prev_spec(tr, H, tc, gcol), _cur_spec(tr, tc, gcol), _next_spec(tr, H, tc, gcol, S),
                  _cur_spec(tr, tc, ucol), _next_spec(tr, H, tc, ucol, S),
                  pl.BlockSpec((3, tc), lambda c, i: (0, c)), pl.BlockSpec((1, tc), lambda c, i: (0, c))],
        out_specs=(blk, blk, pl.BlockSpec((3, tc), lambda c, i: (0, c)), pl.BlockSpec((1, tc), lambda c, i: (0, c))),
        scratch_shapes=[pltpu.VMEM((H + tr + H, tc), F32), pltpu.VMEM((n, tc), F32)],
        compiler_params=_params("parallel", "arbitrary"),
    )(dz, dz, hu, hu, hu, hu, hu, conv_w, conv_b)


def _softplus_neg(s):
    return jnp.minimum(-s, 0.0) - jnp.log(1.0 + jnp.exp(-jnp.abs(s)))


def _hilo_dot(v, m):
    hi = v.astype(BF16)
    lo = (v - hi.astype(F32)).astype(BF16)
    return (jnp.dot(hi, m, preferred_element_type=F32) + jnp.dot(lo, m, preferred_element_type=F32))


def _attn_fwd(h, n_heads, name, gather=()):
    S = h.shape[0]
    dh = HEAD_DIM
    A = n_heads * dh
    tq = _pick(S, 256)
    nq = S // tq
    scale = 1.0 / float(dh) ** 0.5
    ng = len(gather)
    hp = 2 if n_heads % 2 == 0 else 1
    n_grp, hw = n_heads // hp, hp * dh

    def body(*refs):
        q_ref, k_ref, v_ref = refs[:3]
        o_ref, tot_ref = refs[3 + ng:5 + ng]
        full = refs[5 + ng:5 + 2 * ng]
        hd = pl.program_id(0)
        i = pl.program_id(1)
        if ng:
            ssem, rsem = refs[5 + 2 * ng:]

            @pl.when((hd == 0) & (i == 0))
            def _():
                _gather_start(full, ssem, rsem)

            @pl.when((hd == n_grp - 1) & (i == 0))
            def _():
                _gather_forward(full, ssem, rsem)

        heads = range(hp)
        qs = [q_ref[:, h * dh:(h + 1) * dh] for h in heads]
        r_io = lax.broadcasted_iota(jnp.int32, (tq, tq), 0)
        c_io = lax.broadcasted_iota(jnp.int32, (tq, tq), 1)
        later = (r_io > c_io).astype(BF16)
        causal = c_io < r_io

        def rows(ref, j):
            blk = ref[pl.ds(pl.multiple_of(j * tq, tq), tq), :]
            return [blk[:, h * dh:(h + 1) * dh] for h in heads]

        def qk(kj):
            return [lax.dot_general(qs[h], kj[h], (((1,), (1,)), ((), ())), preferred_element_type=F32) * scale
                    for h in heads]

        def log_weights(s, diag):
            base, tot = [], []
            for h in heads:
                ls = _softplus_neg(s[h])
                if diag:
                    ls = jnp.where(causal, ls, 0.0)
                cs = _hilo_dot(ls, later)
                b = s[h] + ls + cs
                base.append(jnp.where(causal, b, -1e30) if diag else b)
                tot.append(cs[:, 0:1] + ls[:, 0:1])
            return base, tot

        def weigh(vj, acc, run, base):
            out = []
            for h in heads:
                w = jnp.exp(base[h] + run[h])
                out.append(acc[h] + jnp.dot(w.astype(BF16), vj[h], preferred_element_type=F32))
            return out

        def trip(t, carry):
            acc, run, base, tot = carry
            j = i - 1 - t
            s = qk(rows(k_ref, j))
            acc = weigh(rows(v_ref, j + 1), acc, run, base)
            base_n, tot_n = log_weights(s, False)
            return acc, [run[h] + tot[h] for h in heads], base_n, tot_n

        base, tot = log_weights(qk(rows(k_ref, i)), True)
        carry = ([jnp.zeros((tq, dh), F32) for _ in heads], [jnp.zeros((tq, 1), F32) for _ in heads], base, tot)
        acc, run, base, tot = lax.fori_loop(0, i, trip, carry)
        acc = weigh(rows(v_ref, 0), acc, run, base)
        for h in heads:
            o_ref[:, h * dh:(h + 1) * dh] = acc[h].astype(BF16)
            tot_ref[h] = jnp.broadcast_to(run[h] + tot[h], (tq, LANES))
        if ng:
            @pl.when((hd == n_grp - 1) & (i == nq - 1))
            def _():
                _gather_finish(full, ssem, rsem)

    T = _gather_items(gather) if ng else 0
    return pl.pallas_call(
        body, name=name,
        out_shape=(jax.ShapeDtypeStruct((S, A), BF16), jax.ShapeDtypeStruct((n_heads, S, LANES), F32))
        + tuple(jax.ShapeDtypeStruct(b.shape, b.dtype) for b in gather),
        grid=(n_grp, nq),
        in_specs=[pl.BlockSpec((tq, hw), lambda hd, i: (i, hd)),
                  pl.BlockSpec((S, hw), lambda hd, i: (0, n_grp + hd)),
                  pl.BlockSpec((S, hw), lambda hd, i: (0, 2 * n_grp + hd))] + [ANY] * ng,
        out_specs=(pl.BlockSpec((tq, hw), lambda hd, i: (i, hd)),
                   pl.BlockSpec((hp, tq, LANES), lambda hd, i: (hd, i, 0))) + (ANY,) * ng,
        input_output_aliases={3 + a: 2 + a for a in range(ng)},
        scratch_shapes=[pltpu.SemaphoreType.DMA((6 * T,)), pltpu.SemaphoreType.DMA((6 * T,))] if ng else [],
        compiler_params=_params("arbitrary", "arbitrary") if ng else _params("parallel", "parallel"),
    )(h, h, h, *gather)


def _attn_bwd(h, do, tot, n_heads, name, hosted=None):
    S = h.shape[0]
    dh = HEAD_DIM
    A = n_heads * dh
    tq = _pick(S, 256)
    nq = S // tq
    scale = 1.0 / float(dh) ** 0.5
    nt_dims = (((1,), (1,)), ((), ()))
    tn_dims = (((0,), (0,)), ((), ()))
    hp = 2 if n_heads % 2 == 0 else 1
    n_grp, hw = n_heads // hp, hp * dh
    h_in = 0 if hosted is None else len(hosted.ins)
    h_out = 0 if hosted is None else len(hosted.outs)

    def body(*refs):
        q_ref, k_ref, v_ref, do_ref, tot_ref = refs[:5]
        h_ins = refs[5:5 + h_in]
        dq_ref, dk_ref, dv_ref = refs[5 + h_in:8 + h_in]
        h_outs = refs[8 + h_in:8 + h_in + h_out]
        dk_acc, dv_acc = refs[8 + h_in + h_out:10 + h_in + h_out]
        sems = refs[10 + h_in + h_out:]
        hd = pl.program_id(0)
        i = pl.program_id(1)
        if hosted is not None:
            @pl.when((hd == 0) & (i == 0))
            def _():
                hosted.start(h_ins, h_outs, sems)

        @pl.when(i == 0)
        def _():
            dk_acc[...] = jnp.zeros_like(dk_acc)
            dv_acc[...] = jnp.zeros_like(dv_acc)

        heads = range(hp)
        qs = [q_ref[:, h * dh:(h + 1) * dh] for h in heads]
        dos = [do_ref[:, h * dh:(h + 1) * dh] for h in heads]
        total = [tot_ref[h][:, 0:1] for h in heads]
        r_io = lax.broadcasted_iota(jnp.int32, (tq, tq), 0)
        c_io = lax.broadcasted_iota(jnp.int32, (tq, tq), 1)
        upto = (r_io <= c_io).astype(BF16)
        before = (r_io < c_io).astype(BF16)
        causal = c_io < r_io

        def rows(ref, j):
            blk = ref[pl.ds(pl.multiple_of(j * tq, tq), tq), :]
            return [blk[:, h * dh:(h + 1) * dh] for h in heads]

        def qk(kj):
            return [lax.dot_general(qs[h], kj[h], nt_dims, preferred_element_type=F32) * scale for h in heads]

        def weights(base, prun, vj):
            dw = [lax.dot_general(dos[h], vj[h], nt_dims, preferred_element_type=F32) for h in heads]
            w, e, ce = [], [], []
            for h in heads:
                w.append(jnp.exp(base[h] + (total[h] - prun[h])))
                e.append(dw[h] * w[h])
                ce.append(jnp.dot(e[h].astype(BF16), before, preferred_element_type=F32))
            return w, e, ce

        def prefix(s, j):
            keep = jnp.logical_or(causal, j != i)
            ls = [jnp.where(keep, _softplus_neg(s[h]), 0.0) for h in heads]
            return keep, ls, [_hilo_dot(ls[h], upto) for h in heads]

        def grads(j, kj, dq, erun, w, e, ce, sn):
            start = pl.multiple_of(j * tq, tq)
            out = []
            for h in heads:
                ecum = ce[h] + erun[h]
                dz = e[h] * sn[h] - (1.0 - sn[h]) * ecum
                ds = (dz * scale).astype(BF16)
                cols = slice(h * dh, (h + 1) * dh)
                dv_acc[pl.ds(start, tq), cols] += lax.dot_general(w[h].astype(BF16), dos[h], tn_dims,
                                                                  preferred_element_type=F32)
                out.append(dq[h] + jnp.dot(ds, kj[h], preferred_element_type=F32))
                dk_acc[pl.ds(start, tq), cols] += lax.dot_general(ds, qs[h], tn_dims, preferred_element_type=F32)
            return out, [erun[h] + ce[h][:, tq - 1:tq] + e[h][:, tq - 1:tq] for h in heads]

        def carried(s, keep, ls, cs):
            base = [jnp.where(keep, s[h] + ls[h] - cs[h], -1e30) for h in heads]
            return base, [jnp.exp(ls[h]) for h in heads], [cs[h][:, tq - 1:tq] for h in heads]

        def trip(j, carry):
            dq, prun, erun, base, sn, ptot = carry
            s_n = qk(rows(k_ref, j + 1))
            w, e, ce = weights(base, prun, rows(v_ref, j))
            keep, ls_n, cs = prefix(s_n, j + 1)
            dq, erun = grads(j, rows(k_ref, j), dq, erun, w, e, ce, sn)
            return (dq, [prun[h] + ptot[h] for h in heads], erun) + carried(s_n, keep, ls_n, cs)

        zeros = [jnp.zeros((tq, 1), F32) for _ in heads]
        s0 = qk(rows(k_ref, 0))
        first = carried(s0, *prefix(s0, 0))
        carry = lax.fori_loop(0, i, trip, ([jnp.zeros((tq, dh), F32) for _ in heads], zeros, zeros) + first)
        dq, prun, erun, base, sn, _ = carry
        dq, _ = grads(i, rows(k_ref, i), dq, erun, *weights(base, prun, rows(v_ref, i)), sn)
        for h in heads:
            dq_ref[:, h * dh:(h + 1) * dh] = dq[h].astype(BF16)

        @pl.when(i == nq - 1)
        def _():
            dk_ref[...] = dk_acc[...].astype(BF16)
            dv_ref[...] = dv_acc[...].astype(BF16)

        if hosted is not None:
            @pl.when((hd == n_grp - 1) & (i == nq - 1))
            def _():
                hosted.finish(h_ins, h_outs, sems)

    qblk = pl.BlockSpec((tq, hw), lambda hd, i: (i, hd))
    full = pl.BlockSpec((S, hw), lambda hd, i: (0, hd))
    scratch = [pltpu.VMEM((S, hw), F32), pltpu.VMEM((S, hw), F32)]
    if hosted is not None:
        scratch += [pltpu.SemaphoreType.DMA((n,)) for n in hosted.sems]
    res = pl.pallas_call(
        body, name=name,
        out_shape=tuple(jax.ShapeDtypeStruct((S, A), BF16) for _ in range(3))
        + (tuple(hosted.outs) if hosted is not None else ()),
        grid=(n_grp, nq),
        in_specs=[qblk,
                  pl.BlockSpec((S, hw), lambda hd, i: (0, n_grp + hd)),
                  pl.BlockSpec((S, hw), lambda hd, i: (0, 2 * n_grp + hd)),
                  qblk,
                  pl.BlockSpec((hp, tq, LANES), lambda hd, i: (hd, i, 0))] + [ANY] * h_in,
        out_specs=(qblk, full, full) + (ANY,) * h_out,
        input_output_aliases={} if hosted is None else {5 + a: 3 + b for a, b in hosted.alias.items()},
        scratch_shapes=scratch,
        compiler_params=_params("arbitrary", "arbitrary") if hosted is not None else _params("parallel", "arbitrary"),
    )(h, h, h, do, tot, *(hosted.ins if hosted is not None else ()))
    return res[0], res[1], res[2], list(res[3:])


def _evenconv_fwd(h, dw_w, dw_b, bn_g, bn_b, name):
    S = h.shape[0]
    KW, A = dw_w.shape
    H = HALO31
    tr = _pick(S, 256, H)
    first_tap = H - (KW - 1)

    def body(ap_ref, a_ref, gp_ref, g_ref, w_ref, b_ref, bg_ref, bb_ref, u1_ref, u3_ref, U):
        i = pl.program_id(1)
        glu_prev = ap_ref[...].astype(F32) * _sigmoid(gp_ref[...].astype(F32))
        U[0:H, :] = jnp.where(i > 0, glu_prev, 0.0)
        U[H:H + tr, :] = a_ref[...].astype(F32) * _sigmoid(g_ref[...].astype(F32))
        acc = b_ref[...] + w_ref[pl.ds(0, 1), :] * U[pl.ds(first_tap, tr), :]
        for k in range(1, KW):
            acc = acc + w_ref[pl.ds(k, 1), :] * U[pl.ds(first_tap + k, tr), :]
        u1_ref[...] = acc
        mu = jnp.mean(acc, axis=-1, keepdims=True)
        d = acc - mu
        var = jnp.mean(d * d, axis=-1, keepdims=True)
        u2 = d * lax.rsqrt(var + LN_EPS) * bg_ref[...] + bb_ref[...]
        u3_ref[...] = (u2 * _sigmoid(u2)).astype(BF16)

    acol = lambda c: 3
    gcol = lambda c: 4
    vec = pl.BlockSpec((1, A), lambda c, i: (0, 0))
    blk = pl.BlockSpec((tr, A), lambda c, i: (i, 0))
    return pl.pallas_call(
        body, name=name,
        out_shape=(jax.ShapeDtypeStruct((S, A), F32), jax.ShapeDtypeStruct((S, A), BF16)),
        grid=(1, S // tr),
        in_specs=[_prev_spec(tr, H, A, acol), _cur_spec(tr, A, acol),
                  _prev_spec(tr, H, A, gcol), _cur_spec(tr, A, gcol),
                  pl.BlockSpec((KW, A), lambda c, i: (0, 0)), vec, vec, vec],
        out_specs=(blk, blk),
        scratch_shapes=[pltpu.VMEM((H + tr, A), F32)],
        compiler_params=_params("parallel", "parallel"),
    )(h, h, h, h, dw_w, dw_b, bn_g, bn_b)


def _evenconv_bwd(du3, u1, h, dw_w, bn_g, bn_b, name):
    S = h.shape[0]
    KW, A = dw_w.shape
    H = HALO31
    tr = _pick(S, 256, H)
    nr = S // tr
    n = tr + H
    first_tap = H - (KW - 1)

    def body(d3_ref, d3n_ref, u1_ref, u1n_ref, ap_ref, a_ref, gp_ref, g_ref, w_ref, bg_ref, bb_ref,
             da_ref, dg_ref, dww_ref, dwb_ref, dbg_ref, dbb_ref, U0, DU):
        i = pl.program_id(1)
        u1 = jnp.concatenate([u1_ref[...], u1n_ref[...]], axis=0)
        d3 = jnp.concatenate([d3_ref[...], d3n_ref[...]], axis=0).astype(F32)
        rows = lax.broadcasted_iota(jnp.int32, (n, 1), 0)
        d3 = jnp.where((rows < tr) | (i < nr - 1), d3, 0.0)
        mu = jnp.mean(u1, axis=-1, keepdims=True)
        d = u1 - mu
        var = jnp.mean(d * d, axis=-1, keepdims=True)
        rstd = lax.rsqrt(var + LN_EPS)
        xh = d * rstd
        u2 = xh * bg_ref[...] + bb_ref[...]
        sg = _sigmoid(u2)
        du2 = d3 * (sg * (1.0 + u2 * (1.0 - sg)))
        dxh = du2 * bg_ref[...]
        m1 = jnp.mean(dxh, axis=-1, keepdims=True)
        m2 = jnp.mean(dxh * xh, axis=-1, keepdims=True)
        du1 = rstd * (dxh - m1 - xh * m2)
        DU[...] = du1
        pbg = jnp.sum(du2[0:tr] * xh[0:tr], axis=0, keepdims=True)
        pbb = jnp.sum(du2[0:tr], axis=0, keepdims=True)
        pwb = jnp.sum(du1[0:tr], axis=0, keepdims=True)

        glu_prev = ap_ref[...].astype(F32) * _sigmoid(gp_ref[...].astype(F32))
        U0[0:H, :] = jnp.where(i > 0, glu_prev, 0.0)
        a = a_ref[...].astype(F32)
        sgg = _sigmoid(g_ref[...].astype(F32))
        U0[H:H + tr, :] = a * sgg

        @pl.when(i == 0)
        def _():
            dbg_ref[...] = pbg
            dbb_ref[...] = pbb
            dwb_ref[...] = pwb
            dww_ref[...] = jnp.zeros_like(dww_ref)

        @pl.when(i > 0)
        def _():
            dbg_ref[...] += pbg
            dbb_ref[...] += pbb
            dwb_ref[...] += pwb

        du0 = w_ref[pl.ds(0, 1), :] * DU[pl.ds(KW - 1, tr), :]
        for k in range(1, KW):
            du0 = du0 + w_ref[pl.ds(k, 1), :] * DU[pl.ds(KW - 1 - k, tr), :]
        da_ref[...] = (du0 * sgg).astype(BF16)
        dg_ref[...] = (du0 * a * sgg * (1.0 - sgg)).astype(BF16)
        dcur = DU[pl.ds(0, tr), :]
        for k in range(KW):
            dww_ref[pl.ds(k, 1), :] += jnp.sum(dcur * U0[pl.ds(first_tap + k, tr), :], axis=0, keepdims=True)

    acol = lambda c: 3
    gcol = lambda c: 4
    one = lambda c: 1
    zero = lambda c: 0
    vec = pl.BlockSpec((1, A), lambda c, i: (0, 0))
    blk = pl.BlockSpec((tr, A), lambda c, i: (i, 0))
    return pl.pallas_call(
        body, name=name,
        out_shape=(jax.ShapeDtypeStruct((S, A), BF16), jax.ShapeDtypeStruct((S, A), BF16),
                   jax.ShapeDtypeStruct((KW, A), F32), jax.ShapeDtypeStruct((1, A), F32),
                   jax.ShapeDtypeStruct((1, A), F32), jax.ShapeDtypeStruct((1, A), F32)),
        grid=(1, nr),
        in_specs=[_cur_spec(tr, A, one), _next_spec(tr, H, A, one, S),
                  _cur_spec(tr, A, zero), _next_spec(tr, H, A, zero, S),
                  _prev_spec(tr, H, A, acol), _cur_spec(tr, A, acol),
                  _prev_spec(tr, H, A, gcol), _cur_spec(tr, A, gcol),
                  pl.BlockSpec((KW, A), lambda c, i: (0, 0)), vec, vec],
        out_specs=(blk, blk, pl.BlockSpec((KW, A), lambda c, i: (0, 0)), vec, vec, vec),
        scratch_shapes=[pltpu.VMEM((H + tr, A), F32), pltpu.VMEM((n, A), F32)],
        compiler_params=_params("arbitrary", "arbitrary"),
    )(du3, du3, u1, u1, h, h, h, h, dw_w, bn_g, bn_b)


def _pool_inv_count(row0, nrows, window):
    t = row0 + lax.broadcasted_iota(jnp.int32, (nrows, 1), 0)
    return 1.0 / jnp.minimum(t + 1, window).astype(F32)


def _odd_fwd(h, conv_w, pool_w, pool_scale, name):
    S = h.shape[0]
    C = conv_w.shape[1]
    G = len(POOL_WINDOWS)
    Dg = C // G
    H = HALO3
    tr = _pick(S, 256, H)

    def body(cb_ref, ccp_ref, cc_ref, chp_ref, ch_ref, pp_ref, p_ref, w_ref, pw_ref, sc_ref, mix_ref, M, P):
        i = pl.program_id(1)
        M[0:H, :] = jnp.where(i > 0, ccp_ref[...].astype(F32) * chp_ref[...].astype(F32), 0.0)
        M[H:H + tr, :] = cc_ref[...].astype(F32) * ch_ref[...].astype(F32)
        cm = (w_ref[pl.ds(0, 1), :] * M[pl.ds(H - 2, tr), :] + w_ref[pl.ds(1, 1), :] * M[pl.ds(H - 1, tr), :]
              + w_ref[pl.ds(2, 1), :] * M[pl.ds(H, tr), :])
        mix_ref[:, 0:C] = (cb_ref[...].astype(F32) * cm).astype(BF16)
        P[0:H, :] = jnp.where(i > 0, pp_ref[...].astype(F32), 0.0)
        P[H:H + tr, :] = p_ref[...].astype(F32)
        for gi, window in enumerate(POOL_WINDOWS):
            cols = pl.ds(gi * Dg, Dg)
            wsum = P[pl.ds(H, tr), cols]
            for dlt in range(1, window):
                wsum = wsum + P[pl.ds(H - dlt, tr), cols]
            diff = wsum * _pool_inv_count(i * tr, tr, window) - P[pl.ds(H, tr), cols]
            yd = jnp.dot(diff.astype(BF16), pw_ref[gi], preferred_element_type=F32) * sc_ref[:, cols]
            mix_ref[:, pl.ds(C + gi * Dg, Dg)] = yd.astype(BF16)

    col = lambda k: (lambda c: k)
    return pl.pallas_call(
        body, name=name, out_shape=jax.ShapeDtypeStruct((S, 2 * C), BF16),
        grid=(1, S // tr),
        in_specs=[_cur_spec(tr, C, col(0)),
                  _prev_spec(tr, H, C, col(1)), _cur_spec(tr, C, col(1)),
                  _prev_spec(tr, H, C, col(2)), _cur_spec(tr, C, col(2)),
                  _prev_spec(tr, H, C, col(3)), _cur_spec(tr, C, col(3)),
                  pl.BlockSpec((3, C), lambda c, i: (0, 0)),
                  pl.BlockSpec((G, Dg, Dg), lambda c, i: (0, 0, 0)),
                  pl.BlockSpec((1, C), lambda c, i: (0, 0))],
        out_specs=pl.BlockSpec((tr, 2 * C), lambda c, i: (i, 0)),
        scratch_shapes=[pltpu.VMEM((H + tr, C), F32), pltpu.VMEM((H + tr, C), F32)],
        compiler_params=_params("parallel", "parallel"),
    )(h, h, h, h, h, h, h, conv_w, pool_w, pool_scale)


def _odd_bwd(dmix, h, conv_w, pool_w, pool_scale, name):
    S = h.shape[0]
    C = conv_w.shape[1]
    G = len(POOL_WINDOWS)
    Dg = C // G
    H = HALO3
    tr = _pick(S, 256, H)
    nr = S // tr
    n = tr + H
    nt_dims = (((1,), (1,)), ((), ()))
    tn_dims = (((0,), (0,)), ((), ()))

    def body(dyc_ref, dycn_ref, dyd_ref, dydn_ref, cb_ref, cbn_ref, ccp_ref, cc_ref, ccn_ref,
             chp_ref, ch_ref, chn_ref, pp_ref, p_ref, w_ref, pw_ref, sc_ref,
             dh_ref, dw_ref, dpw_ref, dsc_ref, M, DCM, P, Q):
        i = pl.program_id(1)
        rows = lax.broadcasted_iota(jnp.int32, (n, 1), 0)
        valid = (rows < tr) | (i < nr - 1)

        @pl.when(i == 0)
        def _():
            dw_ref[...] = jnp.zeros_like(dw_ref)
            dpw_ref[...] = jnp.zeros_like(dpw_ref)
            dsc_ref[...] = jnp.zeros_like(dsc_ref)

        M[0:H, :] = jnp.where(i > 0, ccp_ref[...].astype(F32) * chp_ref[...].astype(F32), 0.0)
        cc = cc_ref[...].astype(F32)
        ch = ch_ref[...].astype(F32)
        M[H:H + tr, :] = cc * ch
        M[H + tr:H + tr + H, :] = ccn_ref[...].astype(F32) * chn_ref[...].astype(F32)
        w0, w1, w2 = w_ref[pl.ds(0, 1), :], w_ref[pl.ds(1, 1), :], w_ref[pl.ds(2, 1), :]
        cm = w0 * M[pl.ds(H - 2, tr), :] + w1 * M[pl.ds(H - 1, tr), :] + w2 * M[pl.ds(H, tr), :]
        dyc = jnp.concatenate([dyc_ref[...], dycn_ref[...]], axis=0).astype(F32)
        dyc = jnp.where(valid, dyc, 0.0)
        cbf = jnp.concatenate([cb_ref[...], cbn_ref[...]], axis=0).astype(F32)
        dh_ref[:, 0:C] = (dyc[0:tr] * cm).astype(BF16)
        DCM[...] = dyc * cbf
        dm = w2 * DCM[pl.ds(0, tr), :] + w1 * DCM[pl.ds(1, tr), :] + w0 * DCM[pl.ds(2, tr), :]
        dh_ref[:, C:2 * C] = (dm * ch).astype(BF16)
        dh_ref[:, 2 * C:3 * C] = (dm * cc).astype(BF16)
        dcur = DCM[pl.ds(0, tr), :]
        for k in range(3):
            dw_ref[pl.ds(k, 1), :] += jnp.sum(dcur * M[pl.ds(H - 2 + k, tr), :], axis=0, keepdims=True)

        P[0:H, :] = jnp.where(i > 0, pp_ref[...].astype(F32), 0.0)
        P[H:H + tr, :] = p_ref[...].astype(F32)
        dyd = jnp.concatenate([dyd_ref[...], dydn_ref[...]], axis=0).astype(F32)
        dyd = jnp.where(valid, dyd, 0.0)
        for gi, window in enumerate(POOL_WINDOWS):
            cols = pl.ds(gi * Dg, Dg)
            lo = gi * Dg
            wsum = P[pl.ds(H, tr), cols]
            for dlt in range(1, window):
                wsum = wsum + P[pl.ds(H - dlt, tr), cols]
            diff = (wsum * _pool_inv_count(i * tr, tr, window) - P[pl.ds(H, tr), cols]).astype(BF16)
            pw = pw_ref[gi]
            dyd_g = dyd[:, lo:lo + Dg]
            e = (dyd_g * sc_ref[:, cols]).astype(BF16)
            yraw = jnp.dot(diff, pw, preferred_element_type=F32)
            dsc_ref[:, cols] += jnp.sum(dyd_g[0:tr] * yraw, axis=0, keepdims=True)
            dpw_ref[gi] += lax.dot_general(diff, e[0:tr], tn_dims, preferred_element_type=F32)
            ddiff = lax.dot_general(e, pw, nt_dims, preferred_element_type=F32)
            Q[:, cols] = ddiff * _pool_inv_count(i * tr, n, window)
            acc = Q[pl.ds(0, tr), cols]
            for dlt in range(1, window):
                acc = acc + Q[pl.ds(dlt, tr), cols]
            dh_ref[:, pl.ds(3 * C + lo, Dg)] = (acc - ddiff[0:tr]).astype(BF16)

    col = lambda k: (lambda c: k)
    return pl.pallas_call(
        body, name=name,
        out_shape=(jax.ShapeDtypeStruct((S, 4 * C), BF16), jax.ShapeDtypeStruct((3, C), F32),
                   jax.ShapeDtypeStruct((G, Dg, Dg), F32), jax.ShapeDtypeStruct((1, C), F32)),
        grid=(1, nr),
        in_specs=[_cur_spec(tr, C, col(0)), _next_spec(tr, H, C, col(0), S),
                  _cur_spec(tr, C, col(1)), _next_spec(tr, H, C, col(1), S),
                  _cur_spec(tr, C, col(0)), _next_spec(tr, H, C, col(0), S),
                  _prev_spec(tr, H, C, col(1)), _cur_spec(tr, C, col(1)), _next_spec(tr, H, C, col(1), S),
                  _prev_spec(tr, H, C, col(2)), _cur_spec(tr, C, col(2)), _next_spec(tr, H, C, col(2), S),
                  _prev_spec(tr, H, C, col(3)), _cur_spec(tr, C, col(3)),
                  pl.BlockSpec((3, C), lambda c, i: (0, 0)),
                  pl.BlockSpec((G, Dg, Dg), lambda c, i: (0, 0, 0)),
                  pl.BlockSpec((1, C), lambda c, i: (0, 0))],
        out_specs=(pl.BlockSpec((tr, 4 * C), lambda c, i: (i, 0)),
                   pl.BlockSpec((3, C), lambda c, i: (0, 0)),
                   pl.BlockSpec((G, Dg, Dg), lambda c, i: (0, 0, 0)),
                   pl.BlockSpec((1, C), lambda c, i: (0, 0))),
        scratch_shapes=[pltpu.VMEM((H + tr + H, C), F32), pltpu.VMEM((n, C), F32),
                        pltpu.VMEM((H + tr, C), F32), pltpu.VMEM((n, C), F32)],
        compiler_params=_params("arbitrary", "arbitrary"),
    )(dmix, dmix, dmix, dmix, h, h, h, h, h, h, h, h, h, h, conv_w, pool_w, pool_scale)


def _adamw(w, grads, m, v, name):
    L, R, C = w.shape
    assert len(grads) == L
    tr, tc = _pick(R, 256, 8), _pick(C, 1408)
    ni, nj = R // tr, C // tc
    c1 = 1.0 / (1.0 - ADAM_B1 ** ADAM_STEP)
    c2 = 1.0 / (1.0 - ADAM_B2 ** ADAM_STEP)

    def g_spec(layer):
        def idx(l, i, j):
            before, after = l < layer, l > layer
            return (jnp.where(before, 0, jnp.where(after, ni - 1, i)),
                    jnp.where(before, 0, jnp.where(after, nj - 1, j)))
        return pl.BlockSpec((tr, tc), idx)

    def body(w_ref, *rest):
        g_refs = rest[:L]
        m_ref, v_ref, go_ref, d_ref, mo_ref, vo_ref = rest[L:]
        l = pl.program_id(0)
        gg = g_refs[0][...]
        for k in range(1, L):
            gg = jnp.where(l == k, g_refs[k][...], gg)
        mn = ADAM_B1 * m_ref[...] + (1.0 - ADAM_B1) * gg
        vn = ADAM_B2 * v_ref[...] + (1.0 - ADAM_B2) * (gg * gg)
        d_ref[...] = -ADAM_LR * ((mn * c1) / (jnp.sqrt(vn * c2) + ADAM_EPS) + ADAM_WD * w_ref[...])
        go_ref[...] = gg
        mo_ref[...] = mn
        vo_ref[...] = vn

    blk = pl.BlockSpec((None, tr, tc), lambda l, i, j: (l, i, j))
    sds = jax.ShapeDtypeStruct(w.shape, F32)
    return pl.pallas_call(
        body, name=name, out_shape=(sds, sds, sds, sds),
        grid=(L, R // tr, C // tc),
        in_specs=[blk] + [g_spec(k) for k in range(L)] + [blk, blk], out_specs=(blk, blk, blk, blk),
        compiler_params=_params("arbitrary", "arbitrary", "arbitrary"),
    )(w, *grads, m, v)


def _sum_slots(buf, name):
    N, R, C = buf.shape
    tr = _pick(R, 512, 8)

    def body(b_ref, o_ref):
        acc = b_ref[0]
        for k in range(1, N):
            acc = acc + b_ref[k]
        o_ref[...] = acc

    return pl.pallas_call(
        body, name=name, out_shape=jax.ShapeDtypeStruct((R, C), F32),
        grid=(R // tr,),
        in_specs=[pl.BlockSpec((N, tr, C), lambda i: (0, i, 0))],
        out_specs=pl.BlockSpec((tr, C), lambda i: (i, 0)),
        compiler_params=_params("parallel"),
    )(buf)


def _pair_sum(pos, g, rsib, name):
    _, hr, hc = rsib.shape
    tr, tc = _pick(hr, 512, 16), _pick(hc, 2816)

    def body(p_ref, g_ref, r_ref, o_ref):
        o_ref[...] = (g_ref[...].astype(F32) + r_ref[...].astype(F32)).astype(BF16)

    blk = pl.BlockSpec((None, tr, tc), lambda s, i, j, p: (s, i, j))
    return pl.pallas_call(
        body, name=name, out_shape=jax.ShapeDtypeStruct(rsib.shape, BF16),
        grid_spec=pltpu.PrefetchScalarGridSpec(
            num_scalar_prefetch=1, grid=(N_CHIPS, hr // tr, hc // tc),
            in_specs=[pl.BlockSpec((None, None, tr, tc), lambda s, i, j, p: (s, p[1], i, j)), blk],
            out_specs=blk),
        compiler_params=_params("parallel", "parallel", "parallel"),
    )(pos, g, rsib)


def _chip_sum(pos, part, land, name):
    _, sr, sc = land.shape
    tr, tc = _pick(sr, 256, 16), _pick(sc, 2816)

    def body(p_ref, own_ref, l_ref, o_ref):
        acc = own_ref[...].astype(F32)
        for k in range(3):
            acc = acc + l_ref[k].astype(F32)
        o_ref[...] = acc

    return pl.pallas_call(
        body, name=name, out_shape=jax.ShapeDtypeStruct((2, sr, sc), F32),
        grid_spec=pltpu.PrefetchScalarGridSpec(
            num_scalar_prefetch=1, grid=(sr // tr, sc // tc),
            in_specs=[pl.BlockSpec((None, tr, tc), lambda i, j, p: (p[0], i, j)),
                      pl.BlockSpec((3, tr, tc), lambda i, j, p: (0, i, j))],
            out_specs=pl.BlockSpec((None, tr, tc), lambda i, j, p: (p[1], i, j))),
        compiler_params=_params("parallel", "parallel"),
    )(pos, part, land)


def _place():
    x, y, c = lax.axis_index("x"), lax.axis_index("y"), lax.axis_index("c")
    return x, y, c


def _other_chips(x, y):
    return [(1 - x, y), (x, 1 - y), (1 - x, 1 - y)]


def _rcopy(src, dst, ssem, rsem, dev):
    return pltpu.make_async_remote_copy(src_ref=src, dst_ref=dst, send_sem=ssem, recv_sem=rsem,
                                        device_id=dev, device_id_type=MESH)


def _gather_items(bufs):
    return sum(b.shape[0] for b in bufs)


def _gather_walk(full):
    t = 0
    for ref in full:
        for l in range(ref.shape[0]):
            yield t, ref, l
            t += 1


def _gather_start(full, ssem, rsem):
    x, y, c = _place()
    j = 2 * x + y
    for t, ref, l in _gather_walk(full):
        own = ref.at[l, j, c]
        for r, (px, py) in enumerate(_other_chips(x, y)):
            _rcopy(own, own, ssem.at[6 * t + r], rsem.at[6 * t + r], (px, py, c)).start()


def _gather_forward(full, ssem, rsem):
    x, y, c = _place()
    for t, ref, l in _gather_walk(full):
        for r, (px, py) in enumerate(_other_chips(x, y)):
            slab = ref.at[l, 2 * px + py, c]
            _rcopy(slab, slab, ssem.at[6 * t + r], rsem.at[6 * t + r], (px, py, c)).wait_recv()
            _rcopy(slab, slab, ssem.at[6 * t + 3 + r], rsem.at[6 * t + 3 + r], (x, y, 1 - c)).start()


def _gather_finish(full, ssem, rsem):
    x, y, c = _place()
    j = 2 * x + y
    for t, ref, l in _gather_walk(full):
        for r, (px, py) in enumerate(_other_chips(x, y)):
            got = ref.at[l, 2 * px + py, 1 - c]
            _rcopy(got, got, ssem.at[6 * t + 3 + r], rsem.at[6 * t + 3 + r], (x, y, 1 - c)).wait_recv()
    for t, ref, l in _gather_walk(full):
        own = ref.at[l, j, c]
        for r, (px, py) in enumerate(_other_chips(x, y)):
            _rcopy(own, own, ssem.at[6 * t + r], rsem.at[6 * t + r], (px, py, c)).wait_send()
            slab = ref.at[l, 2 * px + py, c]
            _rcopy(slab, slab, ssem.at[6 * t + 3 + r], rsem.at[6 * t + 3 + r], (x, y, 1 - c)).wait_send()


def _land_shape(part):
    return jax.ShapeDtypeStruct((3,) + part.shape[1:], part.dtype)


def _exchange_start(parts, land, ssem, rsem):
    x, y, c = _place()
    for a in range(len(parts)):
        for r, (px, py) in enumerate(_other_chips(x, y)):
            _rcopy(parts[a].at[2 * px + py], land[a].at[r], ssem.at[3 * a + r], rsem.at[3 * a + r],
                   (px, py, c)).start()


def _exchange_finish(parts, land, ssem, rsem):
    x, y, c = _place()
    for a in range(len(parts)):
        for r, (px, py) in enumerate(_other_chips(x, y)):
            _rcopy(parts[a].at[2 * px + py], land[a].at[r], ssem.at[3 * a + r], rsem.at[3 * a + r],
                   (px, py, c)).wait()


_Hosted = collections.namedtuple("_Hosted", "ins outs alias sems start finish")


def _host_join(*hosts):
    ins, outs, alias, sems, spans = [], [], {}, [], []
    for h in hosts:
        spans.append((h, len(ins), len(outs), len(sems)))
        alias.update({len(ins) + a: len(outs) + b for a, b in h.alias.items()})
        ins, outs, sems = ins + list(h.ins), outs + list(h.outs), sems + list(h.sems)

    def each(step):
        def run(i, o, s):
            for h, a, b, c in spans:
                getattr(h, step)(i[a:a + len(h.ins)], o[b:b + len(h.outs)], s[c:c + len(h.sems)])
        return run

    return _Hosted(ins, outs, alias, sems, each("start"), each("finish"))


def _host_exchange(parts):
    n = len(parts)
    return _Hosted(list(parts), [_land_shape(p) for p in parts], {}, [3 * n, 3 * n],
                   lambda ins, outs, sems: _exchange_start(ins, outs, *sems),
                   lambda ins, outs, sems: _exchange_finish(ins, outs, *sems))


def _host_gather(bufs):
    T = _gather_items(bufs)

    def finish(ins, outs, sems):
        _gather_forward(outs, *sems)
        _gather_finish(outs, *sems)

    return _Hosted(list(bufs), [jax.ShapeDtypeStruct(b.shape, b.dtype) for b in bufs],
                   {a: a for a in range(len(bufs))}, [6 * T, 6 * T],
                   lambda ins, outs, sems: _gather_start(outs, *sems), finish)


def _host_all_devices(buf):
    return _Hosted([buf], [jax.ShapeDtypeStruct((N_DEV,) + buf.shape, buf.dtype)], {}, [N_DEV - 1, N_DEV - 1, 1],
                   lambda ins, outs, sems: _all_devices_start(ins[0], outs[0], *sems),
                   lambda ins, outs, sems: _all_devices_finish(ins[0], outs[0], *sems))


def _allgather_big(bufs, name):
    n = len(bufs)
    T = _gather_items(bufs)

    def body(*refs):
        full = refs[n:2 * n]
        ssem, rsem = refs[2 * n:]
        _gather_start(full, ssem, rsem)
        _gather_forward(full, ssem, rsem)
        _gather_finish(full, ssem, rsem)

    return pl.pallas_call(
        body, name=name, out_shape=tuple(jax.ShapeDtypeStruct(b.shape, BF16) for b in bufs),
        in_specs=[ANY] * n, out_specs=tuple([ANY] * n),
        input_output_aliases={a: a for a in range(n)},
        scratch_shapes=[pltpu.SemaphoreType.DMA((6 * T,)), pltpu.SemaphoreType.DMA((6 * T,))],
    )(*bufs)


def _allgather_small(shards, name):
    n = len(shards)
    outs = tuple(jax.ShapeDtypeStruct((N_CHIPS,) + s.shape, s.dtype) for s in shards)

    def body(*refs):
        ins, full = refs[:n], refs[n:2 * n]
        ssem, rsem, lsem = refs[2 * n:]
        x, y, c = _place()
        j = 2 * x + y
        chips = _other_chips(x, y)
        cps, locs = [], []
        for a in range(n):
            loc = pltpu.make_async_copy(ins[a], full[a].at[j], lsem.at[a])
            loc.start()
            locs.append(loc)
            for r, (px, py) in enumerate(chips):
                cp = _rcopy(ins[a], full[a].at[j], ssem.at[3 * a + r], rsem.at[3 * a + r], (px, py, c))
                cp.start()
                cps.append(cp)
        for a in range(n):
            for r, (px, py) in enumerate(chips):
                dst = full[a].at[2 * px + py]
                _rcopy(dst, dst, ssem.at[3 * a + r], rsem.at[3 * a + r], (px, py, c)).wait_recv()
        for cp in cps:
            cp.wait_send()
        for loc in locs:
            loc.wait()

    return pl.pallas_call(
        body, name=name, out_shape=outs,
        in_specs=[ANY] * n, out_specs=tuple([ANY] * n),
        scratch_shapes=[pltpu.SemaphoreType.DMA((3 * n,)), pltpu.SemaphoreType.DMA((3 * n,)),
                        pltpu.SemaphoreType.DMA((n,))],
    )(*shards)


def _pair_exchange(grads, name):
    n = len(grads)
    outs = [jax.ShapeDtypeStruct((N_CHIPS,) + g.shape[2:], BF16) for g in grads]

    def body(*refs):
        ins, got = refs[:n], refs[n:2 * n]
        ssem, rsem = refs[2 * n:]
        x, y, c = _place()
        cps = []
        for a in range(n):
            for s in range(N_CHIPS):
                cp = _rcopy(ins[a].at[s, 1 - c], got[a].at[s], ssem.at[N_CHIPS * a + s],
                            rsem.at[N_CHIPS * a + s], (x, y, 1 - c))
                cp.start()
                cps.append(cp)
        for cp in cps:
            cp.wait()

    return pl.pallas_call(
        body, name=name, out_shape=tuple(outs),
        in_specs=[ANY] * n, out_specs=tuple([ANY] * n),
        scratch_shapes=[pltpu.SemaphoreType.DMA((N_CHIPS * n,)), pltpu.SemaphoreType.DMA((N_CHIPS * n,))],
    )(*grads)


def _chip_exchange(parts, name):
    n = len(parts)

    def body(*refs):
        ins, land = refs[:n], refs[n:2 * n]
        ssem, rsem = refs[2 * n:]
        _exchange_start(ins, land, ssem, rsem)
        _exchange_finish(ins, land, ssem, rsem)

    return pl.pallas_call(
        body, name=name, out_shape=tuple(_land_shape(p) for p in parts),
        in_specs=[ANY] * n, out_specs=tuple([ANY] * n),
        scratch_shapes=[pltpu.SemaphoreType.DMA((3 * n,)), pltpu.SemaphoreType.DMA((3 * n,))],
    )(*parts)


def _half_swap(bufs, name):
    n = len(bufs)

    def body(*refs):
        full = refs[n:2 * n]
        ssem, rsem = refs[2 * n:]
        x, y, c = _place()
        cps = []
        for t in range(n):
            mine = full[t].at[c]
            cp = _rcopy(mine, mine, ssem.at[t], rsem.at[t], (x, y, 1 - c))
            cp.start()
            cps.append(cp)
        for t in range(n):
            got = full[t].at[1 - c]
            _rcopy(got, got, ssem.at[t], rsem.at[t], (x, y, 1 - c)).wait_recv()
        for cp in cps:
            cp.wait_send()

    return pl.pallas_call(
        body, name=name, out_shape=tuple(jax.ShapeDtypeStruct(b.shape, F32) for b in bufs),
        in_specs=[ANY] * n, out_specs=tuple([ANY] * n),
        input_output_aliases={a: a for a in range(n)},
        scratch_shapes=[pltpu.SemaphoreType.DMA((n,)), pltpu.SemaphoreType.DMA((n,))],
    )(*bufs)


def _flipped(x, y, c, m):
    fx, fy, fc = (m >> 2) & 1, (m >> 1) & 1, m & 1
    return x + fx - 2 * x * fx, y + fy - 2 * y * fy, c + fc - 2 * c * fc


def _all_devices_start(b_ref, o_ref, ssem, rsem, lsem):
    x, y, c = _place()
    me = 4 * x + 2 * y + c
    pltpu.make_async_copy(b_ref, o_ref.at[me], lsem.at[0]).start()
    for m in range(1, N_DEV):
        _rcopy(b_ref, o_ref.at[me], ssem.at[m - 1], rsem.at[m - 1], _flipped(x, y, c, m)).start()


def _all_devices_finish(b_ref, o_ref, ssem, rsem, lsem):
    x, y, c = _place()
    me = 4 * x + 2 * y + c
    for m in range(1, N_DEV):
        px, py, pc = _flipped(x, y, c, m)
        got = o_ref.at[4 * px + 2 * py + pc]
        _rcopy(got, got, ssem.at[m - 1], rsem.at[m - 1], (px, py, pc)).wait_recv()
    for m in range(1, N_DEV):
        _rcopy(b_ref, o_ref.at[me], ssem.at[m - 1], rsem.at[m - 1], _flipped(x, y, c, m)).wait_send()
    pltpu.make_async_copy(b_ref, o_ref.at[me], lsem.at[0]).wait()


def _pack(arrs):
    flat = jnp.concatenate([a.reshape(-1) for a in arrs])
    rows = -(-flat.shape[0] // (8 * LANES)) * 8
    flat = jnp.pad(flat, (0, rows * LANES - flat.shape[0]))
    return flat.reshape(rows, LANES)


def _unpack(buf, shapes):
    flat = buf.reshape(-1)
    out, off = [], 0
    for s in shapes:
        size = 1
        for d in s:
            size *= d
        out.append(flat[off:off + size].reshape(s))
        off += size
    return out


BIG = ("ev_w_in", "ev_w_out", "od_w_in", "od_w_out", "ffn_w_up", "ffn_w_down")
BIG_KIND = {"ev_w_in": "col", "ev_w_out": "row", "od_w_in": "col", "od_w_out": "row",
            "ffn_w_up": "col", "ffn_w_down": "row"}
SMALL_AXIS = {"ev_dw_w": 2, "ev_dw_b": None, "ev_bn_g": None, "ev_bn_b": None, "od_conv_w": 2,
              "od_pool_w": 2, "od_pool_scale": 1, "ffn_conv_w": 2, "ffn_conv_b": None, "ln_g": 2, "ln_b": 2}
WEIGHTS = ("ev_w_in", "ev_dw_w", "ev_dw_b", "ev_bn_g", "ev_bn_b", "ev_w_out", "od_w_in", "od_conv_w",
           "od_pool_w", "od_pool_scale", "od_w_out", "ffn_w_up", "ffn_conv_w", "ffn_conv_b", "ffn_w_down",
           "ln_g", "ln_b")


def _ffn_fwd(xb, w_up, w_down, conv_w, conv_b, tag, host_up=None, host_down=None):
    hu = _matmul(xb, w_up, mode="nn", b_lead=0, b_split=True, out_dtype=BF16, name=f"{tag}_up", tm=1024, tn=1408,
                 hosted=host_up)
    hu, up_outs = hu if host_up is not None else (hu, None)
    z = _ffn_act_fwd(hu, conv_w, conv_b, name=f"{tag}_act")
    y = _matmul(z, w_down, mode="nn", b_lead=0, out_dtype=F32, name=f"{tag}_down", tm=512, tn=1024,
                hosted=host_down)
    y, down_outs = y if host_down is not None else (y, None)
    return hu, z, y, up_outs, down_outs


def _ffn_bwd(drb, dr, alpha, xb, hu, z, w_up, w_down, conv_w, conv_b, tag, host_dwup=None, host_dx=None):
    g_down = _matmul(z, drb, mode="tn", out_dtype=BF16, name=f"{tag}_dwdown", tm=512, tn=1024)
    dz = _matmul(drb, w_down, mode="nt", b_lead=0, out_dtype=BF16, name=f"{tag}_dz", tm=1024, tn=1408)
    dg, du, dcw, dcb = _ffn_act_bwd(dz, hu, conv_w, conv_b, name=f"{tag}_actbwd")
    g_up = _matmul(xb, (dg, du), mode="tn", out_split=True, out_dtype=BF16, name=f"{tag}_dwup", tm=512, tn=1408,
                   hosted=host_dwup)
    g_up, dwup_outs = g_up if host_dwup is not None else (g_up, None)
    dx = _matmul((dg, du), w_up, mode="nt", b_lead=0, b_split=True, out_dtype=F32, add=dr, add_scale=alpha,
                 name=f"{tag}_dx", tm=512, tn=512, hosted=host_dx)
    dx, dx_outs = dx if host_dx is not None else (dx, None)
    return dx, g_up, g_down, dcw, dcb, dwup_outs, dx_outs


def kernel(x, ev_w_in, ev_dw_w, ev_dw_b, ev_bn_g, ev_bn_b, ev_w_out, od_w_in, od_conv_w, od_pool_w, od_pool_scale, od_w_out, ffn_w_up, ffn_conv_w, ffn_conv_b, ffn_w_down, ln_g, ln_b, loss_target, m_ev_w_in, m_ev_dw_w, m_ev_dw_b, m_ev_bn_g, m_ev_bn_b, m_ev_w_out, m_od_w_in, m_od_conv_w, m_od_pool_w, m_od_pool_scale, m_od_w_out, m_ffn_w_up, m_ffn_conv_w, m_ffn_conv_b, m_ffn_w_down, m_ln_g, m_ln_b, v_ev_w_in, v_ev_dw_w, v_ev_dw_b, v_ev_bn_g, v_ev_bn_b, v_ev_w_out, v_od_w_in, v_od_conv_w, v_od_pool_w, v_od_pool_scale, v_od_w_out, v_ffn_w_up, v_ffn_conv_w, v_ffn_conv_b, v_ffn_w_down, v_ln_g, v_ln_b):
    wts = dict(ev_w_in=ev_w_in, ev_dw_w=ev_dw_w, ev_dw_b=ev_dw_b, ev_bn_g=ev_bn_g, ev_bn_b=ev_bn_b,
               ev_w_out=ev_w_out, od_w_in=od_w_in, od_conv_w=od_conv_w, od_pool_w=od_pool_w,
               od_pool_scale=od_pool_scale, od_w_out=od_w_out, ffn_w_up=ffn_w_up, ffn_conv_w=ffn_conv_w,
               ffn_conv_b=ffn_conv_b, ffn_w_down=ffn_w_down, ln_g=ln_g, ln_b=ln_b)
    mom = dict(ev_w_in=m_ev_w_in, ev_dw_w=m_ev_dw_w, ev_dw_b=m_ev_dw_b, ev_bn_g=m_ev_bn_g, ev_bn_b=m_ev_bn_b,
               ev_w_out=m_ev_w_out, od_w_in=m_od_w_in, od_conv_w=m_od_conv_w, od_pool_w=m_od_pool_w,
               od_pool_scale=m_od_pool_scale, od_w_out=m_od_w_out, ffn_w_up=m_ffn_w_up, ffn_conv_w=m_ffn_conv_w,
               ffn_conv_b=m_ffn_conv_b, ffn_w_down=m_ffn_w_down, ln_g=m_ln_g, ln_b=m_ln_b)
    var = dict(ev_w_in=v_ev_w_in, ev_dw_w=v_ev_dw_w, ev_dw_b=v_ev_dw_b, ev_bn_g=v_ev_bn_g, ev_bn_b=v_ev_bn_b,
               ev_w_out=v_ev_w_out, od_w_in=v_od_w_in, od_conv_w=v_od_conv_w, od_pool_w=v_od_pool_w,
               od_pool_scale=v_od_pool_scale, od_w_out=v_od_w_out, ffn_w_up=v_ffn_w_up, ffn_conv_w=v_ffn_conv_w,
               ffn_conv_b=v_ffn_conv_b, ffn_w_down=v_ffn_w_down, ln_g=v_ln_g, ln_b=v_ln_b)

    S, D = x.shape[1], x.shape[2]
    depth = ln_g.shape[0]
    alpha = (2.0 * depth) ** 0.25
    A = ev_dw_b.shape[-1]
    n_heads = A // HEAD_DIM
    xi, yi, ci = _place()
    chip = 2 * xi + yi
    pos = jnp.stack([chip, ci]).astype(jnp.int32)

    bufs = {f"{k}{l}": _cast_into_gather(pos, wts[k], l, name=f"cast_{k}{l}")
            for k in BIG for l in range(wts[k].shape[0])}

    def whole(key):
        _, r, c = wts[key[:-1]].shape
        col = BIG_KIND[key[:-1]] == "col"
        return bufs[key].reshape(1, N_CHIPS, r, c) if col else bufs[key].reshape(1, N_CHIPS * r, c)

    full = {}

    def gathered_now(keys, arrays):
        bufs.update(zip(keys, arrays))
        full.update({key: whole(key) for key in keys})

    early = ("ev_w_in0", "ev_w_out0")
    under_attn = ("ffn_w_up0", "ffn_w_down0", "od_w_in0", "od_w_out0")
    gathered_now(early, _allgather_big([bufs[k] for k in early], name="gather_first"))
    small_sharded = [k for k in WEIGHTS if k not in BIG and SMALL_AXIS[k] is not None]
    gathered = _allgather_small([wts[k] for k in small_sharded], name="gather_small")
    sm = {k: wts[k] for k in WEIGHTS if k not in BIG and SMALL_AXIS[k] is None}
    for k, g4 in zip(small_sharded, gathered):
        sm[k] = jnp.concatenate([g4[t] for t in range(N_CHIPS)], axis=SMALL_AXIS[k])
    pool_w_bf = sm["od_pool_w"][0].astype(BF16)

    x0 = x[0]
    x0b = _cast_bf16(x, name="cast_x")[0]
    h0 = _matmul(x0b, full["ev_w_in0"], mode="nn", b_lead=0, b_split=True, out_dtype=BF16, name="ev_in",
                 tm=1024, tn=1280)
    o_a, tot, *rest = _attn_fwd(h0, n_heads, name="attn_fwd", gather=[bufs[k] for k in under_attn])
    gathered_now(under_attn, rest)
    u1, u3 = _evenconv_fwd(h0, sm["ev_dw_w"][0], sm["ev_dw_b"], sm["ev_bn_g"], sm["ev_bn_b"], name="evconv_fwd")
    mix0 = jnp.concatenate([o_a, u3], axis=1)
    y1 = _matmul(mix0, full["ev_w_out0"], mode="nn", b_lead=0, out_dtype=F32, name="ev_out", tm=1024, tn=1024)
    x1, x1b, xh1, rs1 = _ln_fwd(x0, y1, sm["ln_g"][0, 0][None], sm["ln_b"][0, 0][None], alpha, name="ln00")
    hu0, z0, y2, got_up, got_down = _ffn_fwd(
        x1b, full["ffn_w_up0"], full["ffn_w_down0"], sm["ffn_conv_w"][0], sm["ffn_conv_b"][0][None], "ffn0",
        host_up=_host_gather([bufs["ffn_w_up1"]]), host_down=_host_gather([bufs["ffn_w_down1"]]))
    gathered_now(("ffn_w_up1",), got_up)
    gathered_now(("ffn_w_down1",), got_down)
    x2, x2b, xh2, rs2 = _ln_fwd(x1, y2, sm["ln_g"][0, 1][None], sm["ln_b"][0, 1][None], alpha, name="ln01")
    h1 = _matmul(x2b, full["od_w_in0"], mode="nn", b_lead=0, b_split=True, out_dtype=BF16, name="od_in",
                 tm=1024, tn=1024)
    mix1 = _odd_fwd(h1, sm["od_conv_w"][0], pool_w_bf, sm["od_pool_scale"], name="odd_fwd")
    y3 = _matmul(mix1, full["od_w_out0"], mode="nn", b_lead=0, out_dtype=F32, name="od_out", tm=1024, tn=1024)
    x3, x3b, xh3, rs3 = _ln_fwd(x2, y3, sm["ln_g"][1, 0][None], sm["ln_b"][1, 0][None], alpha, name="ln10")
    hu1, z1, y4, _, _ = _ffn_fwd(x3b, full["ffn_w_up1"], full["ffn_w_down1"], sm["ffn_conv_w"][1],
                                 sm["ffn_conv_b"][1][None], "ffn1")
    x4, _, xh4, rs4 = _ln_fwd(x3, y4, sm["ln_g"][1, 1][None], sm["ln_b"][1, 1][None], alpha, name="ln11")

    dx4, loss_part = _loss_grad(x4, loss_target[0], name="loss")
    loss = lax.psum(loss_part[0, 0], ("x", "y", "c"))

    def pair_reduce(named, tag):
        g4 = []
        for k, g in named:
            rows, cols = (g.shape[1], g.shape[2]) if BIG_KIND[k] == "col" else (g.shape[0] // N_CHIPS, g.shape[1])
            g4.append(g.reshape(N_CHIPS, 2, rows // 2, cols))
        sib = _pair_exchange(g4, name=f"grad_pair_exchange_{tag}")
        return [_pair_sum(pos, g, r, name=f"grad_pair_sum_{tag}{t}") for t, (g, r) in enumerate(zip(g4, sib))]

    dr4, dr4b, dg11, db11 = _ln_bwd(dx4, xh4, rs4, sm["ln_g"][1, 1][None], name="ln11_bwd")
    dx3, g_up1, g_down1, dcw1, dcb1, _, _ = _ffn_bwd(dr4b, dr4, alpha, x3b, hu1, z1, full["ffn_w_up1"],
                                                     full["ffn_w_down1"], sm["ffn_conv_w"][1],
                                                     sm["ffn_conv_b"][1][None], "ffn1")
    parts_f1 = pair_reduce([("ffn_w_up", g_up1), ("ffn_w_down", g_down1)], "f1")
    dr3, dr3b, dg10, db10 = _ln_bwd(dx3, xh3, rs3, sm["ln_g"][1, 0][None], name="ln10_bwd")
    g_odout = _matmul(mix1, dr3b, mode="tn", out_dtype=BF16, name="od_dwout", tm=512, tn=1024)
    dmix1 = _matmul(dr3b, full["od_w_out0"], mode="nt", b_lead=0, out_dtype=BF16, name="od_dmix", tm=1024, tn=1024)
    dh1, d_odconv, d_pool, d_pscale = _odd_bwd(dmix1, h1, sm["od_conv_w"][0], pool_w_bf, sm["od_pool_scale"],
                                               name="odd_bwd")
    g_odin = _matmul(x2b, dh1, mode="tn", out_split=True, out_dtype=BF16, name="od_dwin", tm=512, tn=1024)
    dx2 = _matmul(dh1, full["od_w_in0"], mode="nt", b_lead=0, b_split=True, out_dtype=F32, add=dr3, add_scale=alpha,
                  name="od_dx", tm=1024, tn=512)
    dr2, dr2b, dg01, db01 = _ln_bwd(dx2, xh2, rs2, sm["ln_g"][0, 1][None], name="ln01_bwd")
    dx1, g_up0, g_down0, dcw0, dcb0, land_up1, land_down1 = _ffn_bwd(
        dr2b, dr2, alpha, x1b, hu0, z0, full["ffn_w_up0"], full["ffn_w_down0"], sm["ffn_conv_w"][0],
        sm["ffn_conv_b"][0][None], "ffn0",
        host_dwup=_host_exchange(parts_f1[:1]), host_dx=_host_exchange(parts_f1[1:]))
    dr1, dr1b, dg00, db00 = _ln_bwd(dx1, xh1, rs1, sm["ln_g"][0, 0][None], name="ln00_bwd")
    g_evout = _matmul(mix0, dr1b, mode="tn", out_dtype=BF16, name="ev_dwout", tm=512, tn=1024)
    dmix0 = _matmul(dr1b, full["ev_w_out0"], mode="nt", b_lead=0, out_dtype=BF16, name="ev_dmix", tm=1024, tn=1024)
    da, dgate, d_dww, d_dwb, d_bng, d_bnb = _evenconv_bwd(dmix0, u1, h0, sm["ev_dw_w"][0], sm["ev_bn_g"],
                                                          sm["ev_bn_b"], name="evconv_bwd")
    parts_b = pair_reduce([("od_w_in", g_odin), ("od_w_out", g_odout), ("ffn_w_up", g_up0),
                           ("ffn_w_down", g_down0), ("ev_w_out", g_evout)], "b")

    d_ln_g = jnp.stack([jnp.stack([dg00[0], dg01[0]]), jnp.stack([dg10[0], dg11[0]])])
    d_ln_b = jnp.stack([jnp.stack([db00[0], db01[0]]), jnp.stack([db10[0], db11[0]])])
    small_partial = {
        "ev_dw_w": d_dww[None], "ev_dw_b": d_dwb, "ev_bn_g": d_bng, "ev_bn_b": d_bnb,
        "od_conv_w": d_odconv[None], "od_pool_w": d_pool[None], "od_pool_scale": d_pscale,
        "ffn_conv_w": jnp.stack([dcw0, dcw1]), "ffn_conv_b": jnp.concatenate([dcb0, dcb1], axis=0),
        "ln_g": d_ln_g, "ln_b": d_ln_b}
    small_names = [k for k in WEIGHTS if k not in BIG]
    packed = _pack([small_partial[k] for k in small_names])
    dq, dk, dv, under_bwd = _attn_bwd(h0, dmix0, tot, n_heads, name="attn_bwd",
                                      hosted=_host_join(_host_exchange(parts_b), _host_all_devices(packed)))
    land_b, all_small = under_bwd[:-1], under_bwd[-1]
    dh0 = jnp.concatenate([dq, dk, dv, da, dgate], axis=1)
    g_evin = _matmul(x0b, dh0, mode="tn", out_split=True, out_dtype=BF16, name="ev_dwin", tm=512, tn=1280)
    parts_e = pair_reduce([("ev_w_in", g_evin)], "e")
    grad_x, land_e = _matmul(dh0, full["ev_w_in0"], mode="nt", b_lead=0, b_split=True, out_dtype=F32, add=dr1,
                             add_scale=alpha, name="ev_dx", tm=1024, tn=512, hosted=_host_exchange(parts_e))

    order = ["ffn_w_up1", "ffn_w_down1", "od_w_in0", "od_w_out0", "ffn_w_up0", "ffn_w_down0", "ev_w_out0", "ev_w_in0"]
    parts = parts_f1 + parts_b + parts_e
    land = list(land_up1) + list(land_down1) + list(land_b) + list(land_e)
    halves = [_chip_sum(pos, p, ld, name=f"grad_chip_sum_{tag}") for tag, p, ld in zip(order, parts, land)]
    reduced = dict(zip(order, _half_swap(halves, name="grad_half_swap")))
    big_grads = {k: [reduced[f"{k}{l}"].reshape(wts[k].shape[1:]) for l in range(wts[k].shape[0])] for k in BIG}

    summed = _sum_slots(all_small, name="sum_small_grads")
    small_full = dict(zip(small_names, _unpack(summed, [small_partial[k].shape for k in small_names])))
    small_grads = {}
    for k in small_names:
        ax = SMALL_AXIS[k]
        if ax is None:
            small_grads[k] = small_full[k]
        else:
            size = wts[k].shape[ax]
            small_grads[k] = lax.dynamic_slice_in_dim(small_full[k], chip * size, size, axis=ax)

    grads, delta, new_m, new_v = {}, {}, {}, {}
    for k in BIG:
        grads[k], delta[k], new_m[k], new_v[k] = _adamw(wts[k], big_grads[k], mom[k], var[k], name=f"adamw_{k}")
    shapes = [wts[k].shape for k in small_names]
    pw, pg, pm, pv = (_pack([d[k] for k in small_names]) for d in (wts, small_grads, mom, var))
    sg, sd, smn, svn = _adamw(pw[None], [pg], pm[None], pv[None], name="adamw_small")
    for dst, buf in ((grads, sg), (delta, sd), (new_m, smn), (new_v, svn)):
        for k, a in zip(small_names, _unpack(buf[0], shapes)):
            dst[k] = a

    return (loss, grad_x[None], *[grads[k] for k in WEIGHTS], *[delta[k] for k in WEIGHTS],
            *[new_m[k] for k in WEIGHTS], *[new_v[k] for k in WEIGHTS])
```

```python
import collections

import jax
import jax.numpy as jnp
from jax import lax
from jax.experimental import pallas as pl
from jax.experimental.pallas import tpu as pltpu

F32 = jnp.float32
BF16 = jnp.bfloat16

HEAD_DIM = 128
POOL_WINDOWS = (2, 4, 8, 16)
LN_EPS = 1e-5
ADAM_LR = 0.001
ADAM_B1 = 0.9
ADAM_B2 = 0.999
ADAM_EPS = 1e-08
ADAM_WD = 0.01
ADAM_STEP = 10
N_CHIPS = 4
N_DEV = 8
MESH = pl.DeviceIdType.MESH
LANES = 128
HALO3 = 16
HALO31 = 32
ROW_CHUNK = 32

ANY = pl.BlockSpec(memory_space=pl.ANY)


def _pick(n, pref, mult=LANES):
    if n <= pref:
        return n
    t = (pref // mult) * mult
    while t >= mult:
        if n % t == 0:
            return t
        t -= mult
    return n


def _params(*sem):
    return pltpu.CompilerParams(dimension_semantics=sem)


def _matmul(a, b, *, mode, out_dtype, name, b_lead=None, b_split=False, out_split=False, add=None,
            add_scale=1.0, tm=512, tn=1024, tk=None, hosted=None):
    halves = isinstance(a, tuple) or isinstance(b, tuple)
    if isinstance(a, tuple):
        assert mode == "nt" and b_split and tk is None
        ash = (a[0].shape[0], 2 * a[0].shape[1])
    else:
        ash = a.shape[-2:]
    if isinstance(b, tuple):
        assert mode == "tn" and tk is None
        bsh = (b[0].shape[0], 2 * b[0].shape[1])
    else:
        bsh = b.shape[-2:]
    if mode == "nn":
        (M, K), (K2, N) = ash, bsh
        if b_split:
            N = N * N_CHIPS
    elif mode == "nt":
        (M, K), (N, K2) = ash, bsh
        if b_split:
            K2 = K2 * N_CHIPS
    else:
        (K, M), (K2, N) = ash, bsh
    assert K == K2, (ash, bsh, mode)
    tm = _pick(M, tm)
    tn = _pick(N // N_CHIPS if (out_split or (b_split and mode == "nn")) else N, tn)
    whole_split_k = b_split and mode == "nt" and tk is None
    if tk is None:
        tk = K
    else:
        tk = _pick(K // N_CHIPS if (b_split and mode == "nt") else K, tk)
    nk = K // tk
    kq = K // N_CHIPS
    n_per = (N // N_CHIPS) // tn
    k_per = (K // N_CHIPS) // tk

    def lead(shape, idx):
        if b_lead is None:
            return pl.BlockSpec(shape, idx)
        return pl.BlockSpec((None,) + shape, lambda i, j, k: (b_lead,) + idx(i, j, k))

    if mode == "nn":
        a_spec = pl.BlockSpec((tm, tk), lambda i, j, k: (i, k))
        if b_split:
            b_spec = lead((None, tk, tn), lambda i, j, k: (lax.div(j, n_per), k, lax.rem(j, n_per)))
        else:
            b_spec = lead((tk, tn), lambda i, j, k: (k, j))
        dims = (((1,), (0,)), ((), ()))
    elif mode == "nt":
        a_spec = pl.BlockSpec((tm, tk), lambda i, j, k: (i, k))
        if whole_split_k:
            b_spec = lead((N_CHIPS, tn, kq), lambda i, j, k: (0, j, 0))
        elif b_split:
            b_spec = lead((None, tn, tk), lambda i, j, k: (lax.div(k, k_per), j, lax.rem(k, k_per)))
        else:
            b_spec = lead((tn, tk), lambda i, j, k: (j, k))
        dims = (((1,), (1,)), ((), ()))
    else:
        a_spec = pl.BlockSpec((tk, tm), lambda i, j, k: (k, i))
        b_spec = pl.BlockSpec((tk, tn), lambda i, j, k: (k, j))
        dims = (((0,), (0,)), ((), ()))
    if out_split:
        out_shape = jax.ShapeDtypeStruct((N_CHIPS, M, N // N_CHIPS), out_dtype)
        out_spec = pl.BlockSpec((None, tm, tn), lambda i, j, k: (lax.div(j, n_per), i, lax.rem(j, n_per)))
    else:
        out_shape = jax.ShapeDtypeStruct((M, N), out_dtype)
        out_spec = pl.BlockSpec((tm, tn), lambda i, j, k: (i, j))
    grid = (M // tm, N // tn, nk)
    nj_half = grid[1] // 2
    if isinstance(a, tuple):
        in_specs = [pl.BlockSpec((tm, K // 2), lambda i, j, k: (i, 0))] * 2 + [b_spec]
        args = [a[0], a[1], b]
    elif isinstance(b, tuple):
        in_specs = [a_spec,
                    pl.BlockSpec((tk, tn), lambda i, j, k: (k, jnp.minimum(j, nj_half - 1))),
                    pl.BlockSpec((tk, tn), lambda i, j, k: (k, jnp.maximum(j - nj_half, 0)))]
        args = [a, b[0], b[1]]
    else:
        in_specs = [a_spec, b_spec]
        args = [a, b]
    n_op = len(args)
    if add is not None:
        in_specs.append(pl.BlockSpec((tm, tn), lambda i, j, k: (i, j)))
        args.append(add)

    n_in = len(args)
    h_in = 0 if hosted is None else len(hosted.ins)
    h_out = 0 if hosted is None else len(hosted.outs)

    def body(*refs):
        ops = refs[:n_op]
        add_ref = refs[n_op] if add is not None else None
        h_ins = refs[n_in:n_in + h_in]
        o_ref = refs[n_in + h_in]
        h_outs = refs[n_in + h_in + 1:n_in + h_in + 1 + h_out]
        scr = refs[n_in + h_in + 1 + h_out:]
        i, j, k = pl.program_id(0), pl.program_id(1), pl.program_id(2)
        if hosted is not None:
            sems = scr[len(scr) - len(hosted.sems):]

            @pl.when((i == 0) & (j == 0) & (k == 0))
            def _():
                hosted.start(h_ins, h_outs, sems)

        def finish(res):
            if add_ref is not None:
                res = res + add_scale * add_ref[...]
            o_ref[...] = res.astype(out_dtype)

        def dot(x, y):
            return lax.dot_general(x, y, dims, preferred_element_type=F32)

        if isinstance(b, tuple):
            @pl.when(j < nj_half)
            def _():
                finish(dot(ops[0][...], ops[1][...]))

            @pl.when(j >= nj_half)
            def _():
                finish(dot(ops[0][...], ops[2][...]))
            part = None
        elif whole_split_k:
            srcs = [(ops[0], s) for s in range(N_CHIPS)] if not isinstance(a, tuple) else \
                   [(ops[s // 2], s % 2) for s in range(N_CHIPS)]
            b_ref = ops[-1]
            part = None
            for s, (src, off) in enumerate(srcs):
                term = dot(src[:, off * kq:(off + 1) * kq], b_ref[s])
                part = term if part is None else part + term
        else:
            part = dot(ops[0][...], ops[1][...])

        if part is None:
            pass
        elif nk == 1:
            finish(part)
        else:
            acc = scr[0]

            @pl.when(k == 0)
            def _():
                acc[...] = part

            @pl.when(k > 0)
            def _():
                acc[...] += part

            @pl.when(k == nk - 1)
            def _():
                finish(acc[...])

        if hosted is not None:
            @pl.when((i == grid[0] - 1) & (j == grid[1] - 1) & (k == nk - 1))
            def _():
                hosted.finish(h_ins, h_outs, sems)

    scratch = [pltpu.VMEM((tm, tn), F32)] if nk > 1 else []
    if hosted is not None:
        res = pl.pallas_call(
            body, name=name,
            out_shape=(out_shape,) + tuple(hosted.outs),
            grid=grid,
            in_specs=in_specs + [ANY] * h_in,
            out_specs=(out_spec,) + (ANY,) * h_out,
            input_output_aliases={n_in + src: 1 + dst for src, dst in hosted.alias.items()},
            scratch_shapes=scratch + [pltpu.SemaphoreType.DMA((n,)) for n in hosted.sems],
            compiler_params=_params("arbitrary", "arbitrary", "arbitrary"),
        )(*args, *hosted.ins)
        return res[0], list(res[1:])
    return pl.pallas_call(
        body, name=name,
        out_shape=out_shape,
        grid=grid,
        in_specs=in_specs,
        out_specs=out_spec,
        scratch_shapes=scratch,
        compiler_params=_params("parallel", "parallel", "arbitrary"),
    )(*args)


def _cast_bf16(w, name):
    L, R, C = w.shape
    tr, tc = _pick(R, 512, 16), _pick(C, 1408)

    def body(w_ref, o_ref):
        o_ref[...] = w_ref[...].astype(BF16)

    return pl.pallas_call(
        body, name=name, out_shape=jax.ShapeDtypeStruct(w.shape, BF16),
        grid=(L, R // tr, C // tc),
        in_specs=[pl.BlockSpec((None, tr, tc), lambda l, i, j: (l, i, j))],
        out_specs=pl.BlockSpec((None, tr, tc), lambda l, i, j: (l, i, j)),
        compiler_params=_params("parallel", "parallel", "parallel"),
    )(w)


def _cast_into_gather(pos, w, layer, name):
    L, R, C = w.shape
    r2 = R // 2
    tr, tc = _pick(r2, 512, 16), _pick(C, 1408)

    def body(p_ref, w_ref, o_ref):
        o_ref[...] = w_ref[...].astype(BF16)

    return pl.pallas_call(
        body, name=name, out_shape=jax.ShapeDtypeStruct((1, N_CHIPS, 2, r2, C), BF16),
        grid_spec=pltpu.PrefetchScalarGridSpec(
            num_scalar_prefetch=1, grid=(2, r2 // tr, C // tc),
            in_specs=[pl.BlockSpec((None, None, tr, tc), lambda h, i, j, p: (layer, h, i, j))],
            out_specs=pl.BlockSpec((None, None, None, tr, tc), lambda h, i, j, p: (0, p[0], h, i, j))),
        compiler_params=_params("parallel", "parallel", "parallel"),
    )(pos, w.reshape(L, 2, r2, C))


def _sigmoid(v):
    return 0.5 * jnp.tanh(0.5 * v) + 0.5


def _ln_fwd(x, y, g, b, alpha, name):
    S, D = x.shape
    tr = _pick(S, 256, 8)

    def body(x_ref, y_ref, g_ref, b_ref, o_ref, ob_ref, xh_ref, rs_ref):
        r = alpha * x_ref[...] + y_ref[...]
        mu = jnp.mean(r, axis=-1, keepdims=True)
        d = r - mu
        var = jnp.mean(d * d, axis=-1, keepdims=True)
        rstd = lax.rsqrt(var + LN_EPS)
        xh = d * rstd
        o = xh * g_ref[...] + b_ref[...]
        o_ref[...] = o
        ob_ref[...] = o.astype(BF16)
        xh_ref[...] = xh
        rs_ref[...] = rstd

    row = pl.BlockSpec((tr, D), lambda i: (i, 0))
    vec = pl.BlockSpec((1, D), lambda i: (0, 0))
    return pl.pallas_call(
        body, name=name,
        out_shape=(jax.ShapeDtypeStruct((S, D), F32), jax.ShapeDtypeStruct((S, D), BF16),
                   jax.ShapeDtypeStruct((S, D), F32), jax.ShapeDtypeStruct((S, 1), F32)),
        grid=(S // tr,),
        in_specs=[row, row, vec, vec],
        out_specs=(row, row, row, pl.BlockSpec((tr, 1), lambda i: (i, 0))),
        compiler_params=_params("parallel"),
    )(x, y, g, b)


def _ln_bwd(dout, xhat, rstd, g, name):
    S, D = dout.shape
    tr = _pick(S, 256, 8)

    def body(do_ref, xh_ref, rs_ref, g_ref, dr_ref, drb_ref, dg_ref, db_ref):
        i = pl.program_id(0)
        do = do_ref[...]
        xh = xh_ref[...]
        dxh = do * g_ref[...]
        m1 = jnp.mean(dxh, axis=-1, keepdims=True)
        m2 = jnp.mean(dxh * xh, axis=-1, keepdims=True)
        dr = rs_ref[...] * (dxh - m1 - xh * m2)
        dr_ref[...] = dr
        drb_ref[...] = dr.astype(BF16)
        pg = jnp.sum(do * xh, axis=0, keepdims=True)
        pb = jnp.sum(do, axis=0, keepdims=True)

        @pl.when(i == 0)
        def _():
            dg_ref[...] = pg
            db_ref[...] = pb

        @pl.when(i > 0)
        def _():
            dg_ref[...] += pg
            db_ref[...] += pb

    row = pl.BlockSpec((tr, D), lambda i: (i, 0))
    vec = pl.BlockSpec((1, D), lambda i: (0, 0))
    return pl.pallas_call(
        body, name=name,
        out_shape=(jax.ShapeDtypeStruct((S, D), F32), jax.ShapeDtypeStruct((S, D), BF16),
                   jax.ShapeDtypeStruct((1, D), F32), jax.ShapeDtypeStruct((1, D), F32)),
        grid=(S // tr,),
        in_specs=[row, row, pl.BlockSpec((tr, 1), lambda i: (i, 0)), vec],
        out_specs=(row, row, vec, vec),
        compiler_params=_params("arbitrary"),
    )(dout, xhat, rstd, g)


def _loss_grad(y, target, name):
    S, D = y.shape
    tr = _pick(S, 256, 8)
    n = S // tr

    def body(y_ref, t_ref, dy_ref, l_ref, acc):
        i = pl.program_id(0)
        d = y_ref[...] - t_ref[...]
        dy_ref[...] = d * (1.0 / D)
        p = jnp.sum(d * d, axis=0, keepdims=True)

        @pl.when(i == 0)
        def _():
            acc[...] = p

        @pl.when(i > 0)
        def _():
            acc[...] += p

        @pl.when(i == n - 1)
        def _():
            l_ref[...] = (0.5 / D) * jnp.sum(acc[...], axis=1, keepdims=True)

    row = pl.BlockSpec((tr, D), lambda i: (i, 0))
    return pl.pallas_call(
        body, name=name,
        out_shape=(jax.ShapeDtypeStruct((S, D), F32), jax.ShapeDtypeStruct((1, 1), F32)),
        grid=(n,),
        in_specs=[row, row],
        out_specs=(row, pl.BlockSpec((1, 1), lambda i: (0, 0))),
        scratch_shapes=[pltpu.VMEM((1, D), F32)],
        compiler_params=_params("arbitrary"),
    )(y, target)


def _prev_spec(tr, halo, width, col):
    return pl.BlockSpec((halo, width), lambda c, i: (jnp.maximum(i * (tr // halo) - 1, 0), col(c)))


def _next_spec(tr, halo, width, col, nrows):
    last = nrows // halo - 1
    return pl.BlockSpec((halo, width), lambda c, i: (jnp.minimum((i + 1) * (tr // halo), last), col(c)))


def _cur_spec(tr, width, col):
    return pl.BlockSpec((tr, width), lambda c, i: (i, col(c)))


def _ffn_act_fwd(hu, conv_w, conv_b, name):
    S, F2 = hu.shape
    F = F2 // 2
    tr, tc, H = _pick(S, 512, 16), _pick(F, 512), HALO3
    nc, nr = F // tc, S // tr
    rc = min(ROW_CHUNK, tr)

    def body(gp_ref, g_ref, u_ref, w_ref, b_ref, z_ref, G):
        i = pl.program_id(1)
        G[0:H, :] = jnp.where(i > 0, gp_ref[...].astype(F32), 0.0)
        G[H:H + tr, :] = g_ref[...].astype(F32)
        w0, w1, w2, b = w_ref[pl.ds(0, 1), :], w_ref[pl.ds(1, 1), :], w_ref[pl.ds(2, 1), :], b_ref[...]
        for r0 in range(0, tr, rc):
            gc = b + w0 * G[pl.ds(H - 2 + r0, rc), :] + w1 * G[pl.ds(H - 1 + r0, rc), :] + w2 * G[pl.ds(H + r0, rc), :]
            z = gc * _sigmoid(gc) * u_ref[pl.ds(r0, rc), :].astype(F32)
            z_ref[pl.ds(r0, rc), :] = z.astype(BF16)

    gcol = lambda c: c
    ucol = lambda c: c + nc
    return pl.pallas_call(
        body, name=name, out_shape=jax.ShapeDtypeStruct((S, F), BF16),
        grid=(nc, nr),
        in_specs=[_prev_spec(tr, H, tc, gcol), _cur_spec(tr, tc, gcol), _cur_spec(tr, tc, ucol),
                  pl.BlockSpec((3, tc), lambda c, i: (0, c)), pl.BlockSpec((1, tc), lambda c, i: (0, c))],
        out_specs=pl.BlockSpec((tr, tc), lambda c, i: (i, c)),
        scratch_shapes=[pltpu.VMEM((H + tr, tc), F32)],
        compiler_params=_params("parallel", "parallel"),
    )(hu, hu, hu, conv_w, conv_b)


def _ffn_act_bwd(dz, hu, conv_w, conv_b, name):
    S, F = dz.shape
    tr, tc, H = _pick(S, 512, 16), _pick(F, 512), HALO3
    nc, nr = F // tc, S // tr
    n = tr + H
    rc = min(ROW_CHUNK, tr)

    def body(dz_ref, dzn_ref, gp_ref, g_ref, gn_ref, u_ref, un_ref, w_ref, b_ref,
             dg_ref, du_ref, dw_ref, db_ref, G, DG):
        i = pl.program_id(1)
        G[0:H, :] = jnp.where(i > 0, gp_ref[...].astype(F32), 0.0)
        G[H:H + tr, :] = g_ref[...].astype(F32)
        G[H + tr:H + tr + H, :] = gn_ref[...].astype(F32)
        w0, w1, w2, b = w_ref[pl.ds(0, 1), :], w_ref[pl.ds(1, 1), :], w_ref[pl.ds(2, 1), :], b_ref[...]

        def fold(v):
            return jnp.sum(v.reshape(v.shape[0] // 8, 8, tc), axis=0)

        def d_gate(r0, rows, dzf, uf):
            taps = [G[pl.ds(H - 2 + k + r0, rows), :] for k in range(3)]
            gc = b + w0 * taps[0] + w1 * taps[1] + w2 * taps[2]
            sg = _sigmoid(gc)
            return dzf * uf * (sg * (1.0 + gc * (1.0 - sg))), gc * sg, taps

        acc_w = [jnp.zeros((8, tc), F32) for _ in range(3)]
        acc_b = jnp.zeros((8, tc), F32)
        for r0 in range(0, tr, rc):
            dzf = dz_ref[pl.ds(r0, rc), :].astype(F32)
            dgc, silu, taps = d_gate(r0, rc, dzf, u_ref[pl.ds(r0, rc), :].astype(F32))
            du_ref[pl.ds(r0, rc), :] = (dzf * silu).astype(BF16)
            DG[pl.ds(r0, rc), :] = dgc
            acc_w = [acc_w[k] + fold(dgc * taps[k]) for k in range(3)]
            acc_b = acc_b + fold(dgc)
        dzn = jnp.where(i < nr - 1, dzn_ref[...].astype(F32), 0.0)
        DG[pl.ds(tr, H), :] = d_gate(tr, H, dzn, un_ref[...].astype(F32))[0]
        for r0 in range(0, tr, rc):
            dg = w2 * DG[pl.ds(r0, rc), :] + w1 * DG[pl.ds(r0 + 1, rc), :] + w0 * DG[pl.ds(r0 + 2, rc), :]
            dg_ref[pl.ds(r0, rc), :] = dg.astype(BF16)
        pw = [jnp.sum(a, axis=0, keepdims=True) for a in acc_w]
        pb = jnp.sum(acc_b, axis=0, keepdims=True)

        @pl.when(i == 0)
        def _():
            for k in range(3):
                dw_ref[pl.ds(k, 1), :] = pw[k]
            db_ref[...] = pb

        @pl.when(i > 0)
        def _():
            for k in range(3):
                dw_ref[pl.ds(k, 1), :] += pw[k]
            db_ref[...] += pb

    gcol = lambda c: c
    ucol = lambda c: c + nc
    blk = pl.BlockSpec((tr, tc), lambda c, i: (i, c))
    return pl.pallas_call(
        body, name=name,
        out_shape=(jax.ShapeDtypeStruct((S, F), BF16), jax.ShapeDtypeStruct((S, F), BF16),
                   jax.ShapeDtypeStruct((3, F), F32), jax.ShapeDtypeStruct((1, F), F32)),
        grid=(nc, nr),
        in_specs=[_cur_spec(tr, tc, gcol), _next_spec(tr, H, tc, gcol, S),
                  _prev_spec(tr, H, tc, gcol), _cur_spec(tr, tc, gcol), _next_spec(tr, H, tc, gcol, S),
                  _cur_spec(tr, tc, ucol), _next_spec(tr, H, tc, ucol, S),
                  pl.BlockSpec((3, tc), lambda c, i: (0, c)), pl.BlockSpec((1, tc), lambda c, i: (0, c))],
        out_specs=(blk, blk, pl.BlockSpec((3, tc), lambda c, i: (0, c)), pl.BlockSpec((1, tc), lambda c, i: (0, c))),
        scratch_shapes=[pltpu.VMEM((H + tr + H, tc), F32), pltpu.VMEM((n, tc), F32)],
        compiler_params=_params("parallel", "arbitrary"),
    )(dz, dz, hu, hu, hu, hu, hu, conv_w, conv_b)


def _softplus_neg(s):
    return jnp.minimum(-s, 0.0) - jnp.log(1.0 + jnp.exp(-jnp.abs(s)))


def _hilo_dot(v, m):
    hi = v.astype(BF16)
    lo = (v - hi.astype(F32)).astype(BF16)
    return (jnp.dot(hi, m, preferred_element_type=F32) + jnp.dot(lo, m, preferred_element_type=F32))


def _attn_fwd(h, n_heads, name, gather=()):
    S = h.shape[0]
    dh = HEAD_DIM
    A = n_heads * dh
    tq = _pick(S, 256)
    nq = S // tq
    scale = 1.0 / float(dh) ** 0.5
    ng = len(gather)
    hp = 2 if n_heads % 2 == 0 else 1
    n_grp, hw = n_heads // hp, hp * dh

    def body(*refs):
        q_ref, k_ref, v_ref = refs[:3]
        o_ref, tot_ref = refs[3 + ng:5 + ng]
        full = refs[5 + ng:5 + 2 * ng]
        hd = pl.program_id(0)
        i = pl.program_id(1)
        if ng:
            ssem, rsem = refs[5 + 2 * ng:]

            @pl.when((hd == 0) & (i == 0))
            def _():
                _gather_start(full, ssem, rsem)

            @pl.when((hd == n_grp - 1) & (i == 0))
            def _():
                _gather_forward(full, ssem, rsem)

        heads = range(hp)
        qs = [q_ref[:, h * dh:(h + 1) * dh] for h in heads]
        r_io = lax.broadcasted_iota(jnp.int32, (tq, tq), 0)
        c_io = lax.broadcasted_iota(jnp.int32, (tq, tq), 1)
        later = (r_io > c_io).astype(BF16)
        causal = c_io < r_io

        def rows(ref, j):
            blk = ref[pl.ds(pl.multiple_of(j * tq, tq), tq), :]
            return [blk[:, h * dh:(h + 1) * dh] for h in heads]

        def qk(kj):
            return [lax.dot_general(qs[h], kj[h], (((1,), (1,)), ((), ())), preferred_element_type=F32) * scale
                    for h in heads]

        def log_weights(s, diag):
            base, tot = [], []
            for h in heads:
                ls = _softplus_neg(s[h])
                if diag:
                    ls = jnp.where(causal, ls, 0.0)
                cs = _hilo_dot(ls, later)
                b = s[h] + ls + cs
                base.append(jnp.where(causal, b, -1e30) if diag else b)
                tot.append(cs[:, 0:1] + ls[:, 0:1])
            return base, tot

        def weigh(vj, acc, run, base):
            out = []
            for h in heads:
                w = jnp.exp(base[h] + run[h])
                out.append(acc[h] + jnp.dot(w.astype(BF16), vj[h], preferred_element_type=F32))
            return out

        def trip(t, carry):
            acc, run, base, tot = carry
            j = i - 1 - t
            s = qk(rows(k_ref, j))
            acc = weigh(rows(v_ref, j + 1), acc, run, base)
            base_n, tot_n = log_weights(s, False)
            return acc, [run[h] + tot[h] for h in heads], base_n, tot_n

        base, tot = log_weights(qk(rows(k_ref, i)), True)
        carry = ([jnp.zeros((tq, dh), F32) for _ in heads], [jnp.zeros((tq, 1), F32) for _ in heads], base, tot)
        acc, run, base, tot = lax.fori_loop(0, i, trip, carry)
        acc = weigh(rows(v_ref, 0), acc, run, base)
        for h in heads:
            o_ref[:, h * dh:(h + 1) * dh] = acc[h].astype(BF16)
            tot_ref[h] = jnp.broadcast_to(run[h] + tot[h], (tq, LANES))
        if ng:
            @pl.when((hd == n_grp - 1) & (i == nq - 1))
            def _():
                _gather_finish(full, ssem, rsem)

    T = _gather_items(gather) if ng else 0
    return pl.pallas_call(
        body, name=name,
        out_shape=(jax.ShapeDtypeStruct((S, A), BF16), jax.ShapeDtypeStruct((n_heads, S, LANES), F32))
        + tuple(jax.ShapeDtypeStruct(b.shape, b.dtype) for b in gather),
        grid=(n_grp, nq),
        in_specs=[pl.BlockSpec((tq, hw), lambda hd, i: (i, hd)),
                  pl.BlockSpec((S, hw), lambda hd, i: (0, n_grp + hd)),
                  pl.BlockSpec((S, hw), lambda hd, i: (0, 2 * n_grp + hd))] + [ANY] * ng,
        out_specs=(pl.BlockSpec((tq, hw), lambda hd, i: (i, hd)),
                   pl.BlockSpec((hp, tq, LANES), lambda hd, i: (hd, i, 0))) + (ANY,) * ng,
        input_output_aliases={3 + a: 2 + a for a in range(ng)},
        scratch_shapes=[pltpu.SemaphoreType.DMA((6 * T,)), pltpu.SemaphoreType.DMA((6 * T,))] if ng else [],
        compiler_params=_params("arbitrary", "arbitrary") if ng else _params("parallel", "parallel"),
    )(h, h, h, *gather)


def _attn_bwd(h, do, tot, n_heads, name, hosted=None):
    S = h.shape[0]
    dh = HEAD_DIM
    A = n_heads * dh
    tq = _pick(S, 256)
    nq = S // tq
    scale = 1.0 / float(dh) ** 0.5
    nt_dims = (((1,), (1,)), ((), ()))
    tn_dims = (((0,), (0,)), ((), ()))
    hp = 2 if n_heads % 2 == 0 else 1
    n_grp, hw = n_heads // hp, hp * dh
    h_in = 0 if hosted is None else len(hosted.ins)
    h_out = 0 if hosted is None else len(hosted.outs)

    def body(*refs):
        q_ref, k_ref, v_ref, do_ref, tot_ref = refs[:5]
        h_ins = refs[5:5 + h_in]
        dq_ref, dk_ref, dv_ref = refs[5 + h_in:8 + h_in]
        h_outs = refs[8 + h_in:8 + h_in + h_out]
        dk_acc, dv_acc = refs[8 + h_in + h_out:10 + h_in + h_out]
        sems = refs[10 + h_in + h_out:]
        hd = pl.program_id(0)
        i = pl.program_id(1)
        if hosted is not None:
            @pl.when((hd == 0) & (i == 0))
            def _():
                hosted.start(h_ins, h_outs, sems)

        @pl.when(i == 0)
        def _():
            dk_acc[...] = jnp.zeros_like(dk_acc)
            dv_acc[...] = jnp.zeros_like(dv_acc)

        heads = range(hp)
        qs = [q_ref[:, h * dh:(h + 1) * dh] for h in heads]
        dos = [do_ref[:, h * dh:(h + 1) * dh] for h in heads]
        total = [tot_ref[h][:, 0:1] for h in heads]
        r_io = lax.broadcasted_iota(jnp.int32, (tq, tq), 0)
        c_io = lax.broadcasted_iota(jnp.int32, (tq, tq), 1)
        upto = (r_io <= c_io).astype(BF16)
        before = (r_io < c_io).astype(BF16)
        causal = c_io < r_io

        def rows(ref, j):
            blk = ref[pl.ds(pl.multiple_of(j * tq, tq), tq), :]
            return [blk[:, h * dh:(h + 1) * dh] for h in heads]

        def qk(kj):
            return [lax.dot_general(qs[h], kj[h], nt_dims, preferred_element_type=F32) * scale for h in heads]

        def weights(base, prun, vj):
            dw = [lax.dot_general(dos[h], vj[h], nt_dims, preferred_element_type=F32) for h in heads]
            w, e, ce = [], [], []
            for h in heads:
                w.append(jnp.exp(base[h] + (total[h] - prun[h])))
                e.append(dw[h] * w[h])
                ce.append(jnp.dot(e[h].astype(BF16), before, preferred_element_type=F32))
            return w, e, ce

        def prefix(s, j):
            keep = jnp.logical_or(causal, j != i)
            ls = [jnp.where(keep, _softplus_neg(s[h]), 0.0) for h in heads]
            return keep, ls, [_hilo_dot(ls[h], upto) for h in heads]

        def grads(j, kj, dq, erun, w, e, ce, sn):
            start = pl.multiple_of(j * tq, tq)
            out = []
            for h in heads:
                ecum = ce[h] + erun[h]
                dz = e[h] * sn[h] - (1.0 - sn[h]) * ecum
                ds = (dz * scale).astype(BF16)
                cols = slice(h * dh, (h + 1) * dh)
                dv_acc[pl.ds(start, tq), cols] += lax.dot_general(w[h].astype(BF16), dos[h], tn_dims,
                                                                  preferred_element_type=F32)
                out.append(dq[h] + jnp.dot(ds, kj[h], preferred_element_type=F32))
                dk_acc[pl.ds(start, tq), cols] += lax.dot_general(ds, qs[h], tn_dims, preferred_element_type=F32)
            return out, [erun[h] + ce[h][:, tq - 1:tq] + e[h][:, tq - 1:tq] for h in heads]

        def carried(s, keep, ls, cs):
            base = [jnp.where(keep, s[h] + ls[h] - cs[h], -1e30) for h in heads]
            return base, [jnp.exp(ls[h]) for h in heads], [cs[h][:, tq - 1:tq] for h in heads]

        def trip(j, carry):
            dq, prun, erun, base, sn, ptot = carry
            s_n = qk(rows(k_ref, j + 1))
            w, e, ce = weights(base, prun, rows(v_ref, j))
            keep, ls_n, cs = prefix(s_n, j + 1)
            dq, erun = grads(j, rows(k_ref, j), dq, erun, w, e, ce, sn)
            return (dq, [prun[h] + ptot[h] for h in heads], erun) + carried(s_n, keep, ls_n, cs)

        zeros = [jnp.zeros((tq, 1), F32) for _ in heads]
        s0 = qk(rows(k_ref, 0))
        first = carried(s0, *prefix(s0, 0))
        carry = lax.fori_loop(0, i, trip, ([jnp.zeros((tq, dh), F32) for _ in heads], zeros, zeros) + first)
        dq, prun, erun, base, sn, _ = carry
        dq, _ = grads(i, rows(k_ref, i), dq, erun, *weights(base, prun, rows(v_ref, i)), sn)
        for h in heads:
            dq_ref[:, h * dh:(h + 1) * dh] = dq[h].astype(BF16)

        @pl.when(i == nq - 1)
        def _():
            dk_ref[...] = dk_acc[...].astype(BF16)
            dv_ref[...] = dv_acc[...].astype(BF16)

        if hosted is not None:
            @pl.when((hd == n_grp - 1) & (i == nq - 1))
            def _():
                hosted.finish(h_ins, h_outs, sems)

    qblk = pl.BlockSpec((tq, hw), lambda hd, i: (i, hd))
    full = pl.BlockSpec((S, hw), lambda hd, i: (0, hd))
    scratch = [pltpu.VMEM((S, hw), F32), pltpu.VMEM((S, hw), F32)]
    if hosted is not None:
        scratch += [pltpu.SemaphoreType.DMA((n,)) for n in hosted.sems]
    res = pl.pallas_call(
        body, name=name,
        out_shape=tuple(jax.ShapeDtypeStruct((S, A), BF16) for _ in range(3))
        + (tuple(hosted.outs) if hosted is not None else ()),
        grid=(n_grp, nq),
        in_specs=[qblk,
                  pl.BlockSpec((S, hw), lambda hd, i: (0, n_grp + hd)),
                  pl.BlockSpec((S, hw), lambda hd, i: (0, 2 * n_grp + hd)),
                  qblk,
                  pl.BlockSpec((hp, tq, LANES), lambda hd, i: (hd, i, 0))] + [ANY] * h_in,
        out_specs=(qblk, full, full) + (ANY,) * h_out,
        input_output_aliases={} if hosted is None else {5 + a: 3 + b for a, b in hosted.alias.items()},
        scratch_shapes=scratch,
        compiler_params=_params("arbitrary", "arbitrary") if hosted is not None else _params("parallel", "arbitrary"),
    )(h, h, h, do, tot, *(hosted.ins if hosted is not None else ()))
    return res[0], res[1], res[2], list(res[3:])


def _evenconv_fwd(h, dw_w, dw_b, bn_g, bn_b, name):
    S = h.shape[0]
    KW, A = dw_w.shape
    H = HALO31
    tr = _pick(S, 256, H)
    first_tap = H - (KW - 1)

    def body(ap_ref, a_ref, gp_ref, g_ref, w_ref, b_ref, bg_ref, bb_ref, u1_ref, u3_ref, U):
        i = pl.program_id(1)
        glu_prev = ap_ref[...].astype(F32) * _sigmoid(gp_ref[...].astype(F32))
        U[0:H, :] = jnp.where(i > 0, glu_prev, 0.0)
        U[H:H + tr, :] = a_ref[...].astype(F32) * _sigmoid(g_ref[...].astype(F32))
        acc = b_ref[...] + w_ref[pl.ds(0, 1), :] * U[pl.ds(first_tap, tr), :]
        for k in range(1, KW):
            acc = acc + w_ref[pl.ds(k, 1), :] * U[pl.ds(first_tap + k, tr), :]
        u1_ref[...] = acc
        mu = jnp.mean(acc, axis=-1, keepdims=True)
        d = acc - mu
        var = jnp.mean(d * d, axis=-1, keepdims=True)
        u2 = d * lax.rsqrt(var + LN_EPS) * bg_ref[...] + bb_ref[...]
        u3_ref[...] = (u2 * _sigmoid(u2)).astype(BF16)

    acol = lambda c: 3
    gcol = lambda c: 4
    vec = pl.BlockSpec((1, A), lambda c, i: (0, 0))
    blk = pl.BlockSpec((tr, A), lambda c, i: (i, 0))
    return pl.pallas_call(
        body, name=name,
        out_shape=(jax.ShapeDtypeStruct((S, A), F32), jax.ShapeDtypeStruct((S, A), BF16)),
        grid=(1, S // tr),
        in_specs=[_prev_spec(tr, H, A, acol), _cur_spec(tr, A, acol),
                  _prev_spec(tr, H, A, gcol), _cur_spec(tr, A, gcol),
                  pl.BlockSpec((KW, A), lambda c, i: (0, 0)), vec, vec, vec],
        out_specs=(blk, blk),
        scratch_shapes=[pltpu.VMEM((H + tr, A), F32)],
        compiler_params=_params("parallel", "parallel"),
    )(h, h, h, h, dw_w, dw_b, bn_g, bn_b)


def _evenconv_bwd(du3, u1, h, dw_w, bn_g, bn_b, name):
    S = h.shape[0]
    KW, A = dw_w.shape
    H = HALO31
    tr = _pick(S, 256, H)
    nr = S // tr
    n = tr + H
    first_tap = H - (KW - 1)

    def body(d3_ref, d3n_ref, u1_ref, u1n_ref, ap_ref, a_ref, gp_ref, g_ref, w_ref, bg_ref, bb_ref,
             da_ref, dg_ref, dww_ref, dwb_ref, dbg_ref, dbb_ref, U0, DU):
        i = pl.program_id(1)
        u1 = jnp.concatenate([u1_ref[...], u1n_ref[...]], axis=0)
        d3 = jnp.concatenate([d3_ref[...], d3n_ref[...]], axis=0).astype(F32)
        rows = lax.broadcasted_iota(jnp.int32, (n, 1), 0)
        d3 = jnp.where((rows < tr) | (i < nr - 1), d3, 0.0)
        mu = jnp.mean(u1, axis=-1, keepdims=True)
        d = u1 - mu
        var = jnp.mean(d * d, axis=-1, keepdims=True)
        rstd = lax.rsqrt(var + LN_EPS)
        xh = d * rstd
        u2 = xh * bg_ref[...] + bb_ref[...]
        sg = _sigmoid(u2)
        du2 = d3 * (sg * (1.0 + u2 * (1.0 - sg)))
        dxh = du2 * bg_ref[...]
        m1 = jnp.mean(dxh, axis=-1, keepdims=True)
        m2 = jnp.mean(dxh * xh, axis=-1, keepdims=True)
        du1 = rstd * (dxh - m1 - xh * m2)
        DU[...] = du1
        pbg = jnp.sum(du2[0:tr] * xh[0:tr], axis=0, keepdims=True)
        pbb = jnp.sum(du2[0:tr], axis=0, keepdims=True)
        pwb = jnp.sum(du1[0:tr], axis=0, keepdims=True)

        glu_prev = ap_ref[...].astype(F32) * _sigmoid(gp_ref[...].astype(F32))
        U0[0:H, :] = jnp.where(i > 0, glu_prev, 0.0)
        a = a_ref[...].astype(F32)
        sgg = _sigmoid(g_ref[...].astype(F32))
        U0[H:H + tr, :] = a * sgg

        @pl.when(i == 0)
        def _():
            dbg_ref[...] = pbg
            dbb_ref[...] = pbb
            dwb_ref[...] = pwb
            dww_ref[...] = jnp.zeros_like(dww_ref)

        @pl.when(i > 0)
        def _():
            dbg_ref[...] += pbg
            dbb_ref[...] += pbb
            dwb_ref[...] += pwb

        du0 = w_ref[pl.ds(0, 1), :] * DU[pl.ds(KW - 1, tr), :]
        for k in range(1, KW):
            du0 = du0 + w_ref[pl.ds(k, 1), :] * DU[pl.ds(KW - 1 - k, tr), :]
        da_ref[...] = (du0 * sgg).astype(BF16)
        dg_ref[...] = (du0 * a * sgg * (1.0 - sgg)).astype(BF16)
        dcur = DU[pl.ds(0, tr), :]
        for k in range(KW):
            dww_ref[pl.ds(k, 1), :] += jnp.sum(dcur * U0[pl.ds(first_tap + k, tr), :], axis=0, keepdims=True)

    acol = lambda c: 3
    gcol = lambda c: 4
    one = lambda c: 1
    zero = lambda c: 0
    vec = pl.BlockSpec((1, A), lambda c, i: (0, 0))
    blk = pl.BlockSpec((tr, A), lambda c, i: (i, 0))
    return pl.pallas_call(
        body, name=name,
        out_shape=(jax.ShapeDtypeStruct((S, A), BF16), jax.ShapeDtypeStruct((S, A), BF16),
                   jax.ShapeDtypeStruct((KW, A), F32), jax.ShapeDtypeStruct((1, A), F32),
                   jax.ShapeDtypeStruct((1, A), F32), jax.ShapeDtypeStruct((1, A), F32)),
        grid=(1, nr),
        in_specs=[_cur_spec(tr, A, one), _next_spec(tr, H, A, one, S),
                  _cur_spec(tr, A, zero), _next_spec(tr, H, A, zero, S),
                  _prev_spec(tr, H, A, acol), _cur_spec(tr, A, acol),
                  _prev_spec(tr, H, A, gcol), _cur_spec(tr, A, gcol),
                  pl.BlockSpec((KW, A), lambda c, i: (0, 0)), vec, vec],
        out_specs=(blk, blk, pl.BlockSpec((KW, A), lambda c, i: (0, 0)), vec, vec, vec),
        scratch_shapes=[pltpu.VMEM((H + tr, A), F32), pltpu.VMEM((n, A), F32)],
        compiler_params=_params("arbitrary", "arbitrary"),
    )(du3, du3, u1, u1, h, h, h, h, dw_w, bn_g, bn_b)


def _pool_inv_count(row0, nrows, window):
    t = row0 + lax.broadcasted_iota(jnp.int32, (nrows, 1), 0)
    return 1.0 / jnp.minimum(t + 1, window).astype(F32)


def _odd_fwd(h, conv_w, pool_w, pool_scale, name):
    S = h.shape[0]
    C = conv_w.shape[1]
    G = len(POOL_WINDOWS)
    Dg = C // G
    H = HALO3
    tr = _pick(S, 256, H)

    def body(cb_ref, ccp_ref, cc_ref, chp_ref, ch_ref, pp_ref, p_ref, w_ref, pw_ref, sc_ref, mix_ref, M, P):
        i = pl.program_id(1)
        M[0:H, :] = jnp.where(i > 0, ccp_ref[...].astype(F32) * chp_ref[...].astype(F32), 0.0)
        M[H:H + tr, :] = cc_ref[...].astype(F32) * ch_ref[...].astype(F32)
        cm = (w_ref[pl.ds(0, 1), :] * M[pl.ds(H - 2, tr), :] + w_ref[pl.ds(1, 1), :] * M[pl.ds(H - 1, tr), :]
              + w_ref[pl.ds(2, 1), :] * M[pl.ds(H, tr), :])
        mix_ref[:, 0:C] = (cb_ref[...].astype(F32) * cm).astype(BF16)
        P[0:H, :] = jnp.where(i > 0, pp_ref[...].astype(F32), 0.0)
        P[H:H + tr, :] = p_ref[...].astype(F32)
        for gi, window in enumerate(POOL_WINDOWS):
            cols = pl.ds(gi * Dg, Dg)
            wsum = P[pl.ds(H, tr), cols]
            for dlt in range(1, window):
                wsum = wsum + P[pl.ds(H - dlt, tr), cols]
            diff = wsum * _pool_inv_count(i * tr, tr, window) - P[pl.ds(H, tr), cols]
            yd = jnp.dot(diff.astype(BF16), pw_ref[gi], preferred_element_type=F32) * sc_ref[:, cols]
            mix_ref[:, pl.ds(C + gi * Dg, Dg)] = yd.astype(BF16)

    col = lambda k: (lambda c: k)
    return pl.pallas_call(
        body, name=name, out_shape=jax.ShapeDtypeStruct((S, 2 * C), BF16),
        grid=(1, S // tr),
        in_specs=[_cur_spec(tr, C, col(0)),
                  _prev_spec(tr, H, C, col(1)), _cur_spec(tr, C, col(1)),
                  _prev_spec(tr, H, C, col(2)), _cur_spec(tr, C, col(2)),
                  _prev_spec(tr, H, C, col(3)), _cur_spec(tr, C, col(3)),
                  pl.BlockSpec((3, C), lambda c, i: (0, 0)),
                  pl.BlockSpec((G, Dg, Dg), lambda c, i: (0, 0, 0)),
                  pl.BlockSpec((1, C), lambda c, i: (0, 0))],
        out_specs=pl.BlockSpec((tr, 2 * C), lambda c, i: (i, 0)),
        scratch_shapes=[pltpu.VMEM((H + tr, C), F32), pltpu.VMEM((H + tr, C), F32)],
        compiler_params=_params("parallel", "parallel"),
    )(h, h, h, h, h, h, h, conv_w, pool_w, pool_scale)


def _odd_bwd(dmix, h, conv_w, pool_w, pool_scale, name):
    S = h.shape[0]
    C = conv_w.shape[1]
    G = len(POOL_WINDOWS)
    Dg = C // G
    H = HALO3
    tr = _pick(S, 256, H)
    nr = S // tr
    n = tr + H
    nt_dims = (((1,), (1,)), ((), ()))
    tn_dims = (((0,), (0,)), ((), ()))

    def body(dyc_ref, dycn_ref, dyd_ref, dydn_ref, cb_ref, cbn_ref, ccp_ref, cc_ref, ccn_ref,
             chp_ref, ch_ref, chn_ref, pp_ref, p_ref, w_ref, pw_ref, sc_ref,
             dh_ref, dw_ref, dpw_ref, dsc_ref, M, DCM, P, Q):
        i = pl.program_id(1)
        rows = lax.broadcasted_iota(jnp.int32, (n, 1), 0)
        valid = (rows < tr) | (i < nr - 1)

        @pl.when(i == 0)
        def _():
            dw_ref[...] = jnp.zeros_like(dw_ref)
            dpw_ref[...] = jnp.zeros_like(dpw_ref)
            dsc_ref[...] = jnp.zeros_like(dsc_ref)

        M[0:H, :] = jnp.where(i > 0, ccp_ref[...].astype(F32) * chp_ref[...].astype(F32), 0.0)
        cc = cc_ref[...].astype(F32)
        ch = ch_ref[...].astype(F32)
        M[H:H + tr, :] = cc * ch
        M[H + tr:H + tr + H, :] = ccn_ref[...].astype(F32) * chn_ref[...].astype(F32)
        w0, w1, w2 = w_ref[pl.ds(0, 1), :], w_ref[pl.ds(1, 1), :], w_ref[pl.ds(2, 1), :]
        cm = w0 * M[pl.ds(H - 2, tr), :] + w1 * M[pl.ds(H - 1, tr), :] + w2 * M[pl.ds(H, tr), :]
        dyc = jnp.concatenate([dyc_ref[...], dycn_ref[...]], axis=0).astype(F32)
        dyc = jnp.where(valid, dyc, 0.0)
        cbf = jnp.concatenate([cb_ref[...], cbn_ref[...]], axis=0).astype(F32)
        dh_ref[:, 0:C] = (dyc[0:tr] * cm).astype(BF16)
        DCM[...] = dyc * cbf
        dm = w2 * DCM[pl.ds(0, tr), :] + w1 * DCM[pl.ds(1, tr), :] + w0 * DCM[pl.ds(2, tr), :]
        dh_ref[:, C:2 * C] = (dm * ch).astype(BF16)
        dh_ref[:, 2 * C:3 * C] = (dm * cc).astype(BF16)
        dcur = DCM[pl.ds(0, tr), :]
        for k in range(3):
            dw_ref[pl.ds(k, 1), :] += jnp.sum(dcur * M[pl.ds(H - 2 + k, tr), :], axis=0, keepdims=True)

        P[0:H, :] = jnp.where(i > 0, pp_ref[...].astype(F32), 0.0)
        P[H:H + tr, :] = p_ref[...].astype(F32)
        dyd = jnp.concatenate([dyd_ref[...], dydn_ref[...]], axis=0).astype(F32)
        dyd = jnp.where(valid, dyd, 0.0)
        for gi, window in enumerate(POOL_WINDOWS):
            cols = pl.ds(gi * Dg, Dg)
            lo = gi * Dg
            wsum = P[pl.ds(H, tr), cols]
            for dlt in range(1, window):
                wsum = wsum + P[pl.ds(H - dlt, tr), cols]
            diff = (wsum * _pool_inv_count(i * tr, tr, window) - P[pl.ds(H, tr), cols]).astype(BF16)
            pw = pw_ref[gi]
            dyd_g = dyd[:, lo:lo + Dg]
            e = (dyd_g * sc_ref[:, cols]).astype(BF16)
            yraw = jnp.dot(diff, pw, preferred_element_type=F32)
            dsc_ref[:, cols] += jnp.sum(dyd_g[0:tr] * yraw, axis=0, keepdims=True)
            dpw_ref[gi] += lax.dot_general(diff, e[0:tr], tn_dims, preferred_element_type=F32)
            ddiff = lax.dot_general(e, pw, nt_dims, preferred_element_type=F32)
            Q[:, cols] = ddiff * _pool_inv_count(i * tr, n, window)
            acc = Q[pl.ds(0, tr), cols]
            for dlt in range(1, window):
                acc = acc + Q[pl.ds(dlt, tr), cols]
            dh_ref[:, pl.ds(3 * C + lo, Dg)] = (acc - ddiff[0:tr]).astype(BF16)

    col = lambda k: (lambda c: k)
    return pl.pallas_call(
        body, name=name,
        out_shape=(jax.ShapeDtypeStruct((S, 4 * C), BF16), jax.ShapeDtypeStruct((3, C), F32),
                   jax.ShapeDtypeStruct((G, Dg, Dg), F32), jax.ShapeDtypeStruct((1, C), F32)),
        grid=(1, nr),
        in_specs=[_cur_spec(tr, C, col(0)), _next_spec(tr, H, C, col(0), S),
                  _cur_spec(tr, C, col(1)), _next_spec(tr, H, C, col(1), S),
                  _cur_spec(tr, C, col(0)), _next_spec(tr, H, C, col(0), S),
                  _prev_spec(tr, H, C, col(1)), _cur_spec(tr, C, col(1)), _next_spec(tr, H, C, col(1), S),
                  _prev_spec(tr, H, C, col(2)), _cur_spec(tr, C, col(2)), _next_spec(tr, H, C, col(2), S),
                  _prev_spec(tr, H, C, col(3)), _cur_spec(tr, C, col(3)),
                  pl.BlockSpec((3, C), lambda c, i: (0, 0)),
                  pl.BlockSpec((G, Dg, Dg), lambda c, i: (0, 0, 0)),
                  pl.BlockSpec((1, C), lambda c, i: (0, 0))],
        out_specs=(pl.BlockSpec((tr, 4 * C), lambda c, i: (i, 0)),
                   pl.BlockSpec((3, C), lambda c, i: (0, 0)),
                   pl.BlockSpec((G, Dg, Dg), lambda c, i: (0, 0, 0)),
                   pl.BlockSpec((1, C), lambda c, i: (0, 0))),
        scratch_shapes=[pltpu.VMEM((H + tr + H, C), F32), pltpu.VMEM((n, C), F32),
                        pltpu.VMEM((H + tr, C), F32), pltpu.VMEM((n, C), F32)],
        compiler_params=_params("arbitrary", "arbitrary"),
    )(dmix, dmix, dmix, dmix, h, h, h, h, h, h, h, h, h, h, conv_w, pool_w, pool_scale)


def _adamw(w, grads, m, v, name):
    L, R, C = w.shape
    assert len(grads) == L
    tr, tc = _pick(R, 256, 8), _pick(C, 1408)
    ni, nj = R // tr, C // tc
    c1 = 1.0 / (1.0 - ADAM_B1 ** ADAM_STEP)
    c2 = 1.0 / (1.0 - ADAM_B2 ** ADAM_STEP)

    def g_spec(layer):
        def idx(l, i, j):
            before, after = l < layer, l > layer
            return (jnp.where(before, 0, jnp.where(after, ni - 1, i)),
                    jnp.where(before, 0, jnp.where(after, nj - 1, j)))
        return pl.BlockSpec((tr, tc), idx)

    def body(w_ref, *rest):
        g_refs = rest[:L]
        m_ref, v_ref, go_ref, d_ref, mo_ref, vo_ref = rest[L:]
        l = pl.program_id(0)
        gg = g_refs[0][...]
        for k in range(1, L):
            gg = jnp.where(l == k, g_refs[k][...], gg)
        mn = ADAM_B1 * m_ref[...] + (1.0 - ADAM_B1) * gg
        vn = ADAM_B2 * v_ref[...] + (1.0 - ADAM_B2) * (gg * gg)
        d_ref[...] = -ADAM_LR * ((mn * c1) / (jnp.sqrt(vn * c2) + ADAM_EPS) + ADAM_WD * w_ref[...])
        go_ref[...] = gg
        mo_ref[...] = mn
        vo_ref[...] = vn

    blk = pl.BlockSpec((None, tr, tc), lambda l, i, j: (l, i, j))
    sds = jax.ShapeDtypeStruct(w.shape, F32)
    return pl.pallas_call(
        body, name=name, out_shape=(sds, sds, sds, sds),
        grid=(L, R // tr, C // tc),
        in_specs=[blk] + [g_spec(k) for k in range(L)] + [blk, blk], out_specs=(blk, blk, blk, blk),
        compiler_params=_params("arbitrary", "arbitrary", "arbitrary"),
    )(w, *grads, m, v)


def _sum_slots(buf, name):
    N, R, C = buf.shape
    tr = _pick(R, 512, 8)

    def body(b_ref, o_ref):
        acc = b_ref[0]
        for k in range(1, N):
            acc = acc + b_ref[k]
        o_ref[...] = acc

    return pl.pallas_call(
        body, name=name, out_shape=jax.ShapeDtypeStruct((R, C), F32),
        grid=(R // tr,),
        in_specs=[pl.BlockSpec((N, tr, C), lambda i: (0, i, 0))],
        out_specs=pl.BlockSpec((tr, C), lambda i: (i, 0)),
        compiler_params=_params("parallel"),
    )(buf)


def _pair_sum(pos, g, rsib, name):
    _, hr, hc = rsib.shape
    tr, tc = _pick(hr, 512, 16), _pick(hc, 2816)

    def body(p_ref, g_ref, r_ref, o_ref):
        o_ref[...] = (g_ref[...].astype(F32) + r_ref[...].astype(F32)).astype(BF16)

    blk = pl.BlockSpec((None, tr, tc), lambda s, i, j, p: (s, i, j))
    return pl.pallas_call(
        body, name=name, out_shape=jax.ShapeDtypeStruct(rsib.shape, BF16),
        grid_spec=pltpu.PrefetchScalarGridSpec(
            num_scalar_prefetch=1, grid=(N_CHIPS, hr // tr, hc // tc),
            in_specs=[pl.BlockSpec((None, None, tr, tc), lambda s, i, j, p: (s, p[1], i, j)), blk],
            out_specs=blk),
        compiler_params=_params("parallel", "parallel", "parallel"),
    )(pos, g, rsib)


def _chip_sum(pos, part, land, name):
    _, sr, sc = land.shape
    tr, tc = _pick(sr, 256, 16), _pick(sc, 2816)

    def body(p_ref, own_ref, l_ref, o_ref):
        acc = own_ref[...].astype(F32)
        for k in range(3):
            acc = acc + l_ref[k].astype(F32)
        o_ref[...] = acc

    return pl.pallas_call(
        body, name=name, out_shape=jax.ShapeDtypeStruct((2, sr, sc), F32),
        grid_spec=pltpu.PrefetchScalarGridSpec(
            num_scalar_prefetch=1, grid=(sr // tr, sc // tc),
            in_specs=[pl.BlockSpec((None, tr, tc), lambda i, j, p: (p[0], i, j)),
                      pl.BlockSpec((3, tr, tc), lambda i, j, p: (0, i, j))],
            out_specs=pl.BlockSpec((None, tr, tc), lambda i, j, p: (p[1], i, j))),
        compiler_params=_params("parallel", "parallel"),
    )(pos, part, land)


def _place():
    x, y, c = lax.axis_index("x"), lax.axis_index("y"), lax.axis_index("c")
    return x, y, c


def _other_chips(x, y):
    return [(1 - x, y), (x, 1 - y), (1 - x, 1 - y)]


def _rcopy(src, dst, ssem, rsem, dev):
    return pltpu.make_async_remote_copy(src_ref=src, dst_ref=dst, send_sem=ssem, recv_sem=rsem,
                                        device_id=dev, device_id_type=MESH)


def _gather_items(bufs):
    return sum(b.shape[0] for b in bufs)


def _gather_walk(full):
    t = 0
    for ref in full:
        for l in range(ref.shape[0]):
            yield t, ref, l
            t += 1


def _gather_start(full, ssem, rsem):
    x, y, c = _place()
    j = 2 * x + y
    for t, ref, l in _gather_walk(full):
        own = ref.at[l, j, c]
        for r, (px, py) in enumerate(_other_chips(x, y)):
            _rcopy(own, own, ssem.at[6 * t + r], rsem.at[6 * t + r], (px, py, c)).start()


def _gather_forward(full, ssem, rsem):
    x, y, c = _place()
    for t, ref, l in _gather_walk(full):
        for r, (px, py) in enumerate(_other_chips(x, y)):
            slab = ref.at[l, 2 * px + py, c]
            _rcopy(slab, slab, ssem.at[6 * t + r], rsem.at[6 * t + r], (px, py, c)).wait_recv()
            _rcopy(slab, slab, ssem.at[6 * t + 3 + r], rsem.at[6 * t + 3 + r], (x, y, 1 - c)).start()


def _gather_finish(full, ssem, rsem):
    x, y, c = _place()
    j = 2 * x + y
    for t, ref, l in _gather_walk(full):
        for r, (px, py) in enumerate(_other_chips(x, y)):
            got = ref.at[l, 2 * px + py, 1 - c]
            _rcopy(got, got, ssem.at[6 * t + 3 + r], rsem.at[6 * t + 3 + r], (x, y, 1 - c)).wait_recv()
    for t, ref, l in _gather_walk(full):
        own = ref.at[l, j, c]
        for r, (px, py) in enumerate(_other_chips(x, y)):
            _rcopy(own, own, ssem.at[6 * t + r], rsem.at[6 * t + r], (px, py, c)).wait_send()
            slab = ref.at[l, 2 * px + py, c]
            _rcopy(slab, slab, ssem.at[6 * t + 3 + r], rsem.at[6 * t + 3 + r], (x, y, 1 - c)).wait_send()


def _land_shape(part):
    return jax.ShapeDtypeStruct((3,) + part.shape[1:], part.dtype)


def _exchange_start(parts, land, ssem, rsem):
    x, y, c = _place()
    for a in range(len(parts)):
        for r, (px, py) in enumerate(_other_chips(x, y)):
            _rcopy(parts[a].at[2 * px + py], land[a].at[r], ssem.at[3 * a + r], rsem.at[3 * a + r],
                   (px, py, c)).start()


def _exchange_finish(parts, land, ssem, rsem):
    x, y, c = _place()
    for a in range(len(parts)):
        for r, (px, py) in enumerate(_other_chips(x, y)):
            _rcopy(parts[a].at[2 * px + py], land[a].at[r], ssem.at[3 * a + r], rsem.at[3 * a + r],
                   (px, py, c)).wait()


_Hosted = collections.namedtuple("_Hosted", "ins outs alias sems start finish")


def _host_join(*hosts):
    ins, outs, alias, sems, spans = [], [], {}, [], []
    for h in hosts:
        spans.append((h, len(ins), len(outs), len(sems)))
        alias.update({len(ins) + a: len(outs) + b for a, b in h.alias.items()})
        ins, outs, sems = ins + list(h.ins), outs + list(h.outs), sems + list(h.sems)

    def each(step):
        def run(i, o, s):
            for h, a, b, c in spans:
                getattr(h, step)(i[a:a + len(h.ins)], o[b:b + len(h.outs)], s[c:c + len(h.sems)])
        return run

    return _Hosted(ins, outs, alias, sems, each("start"), each("finish"))


def _host_exchange(parts):
    n = len(parts)
    return _Hosted(list(parts), [_land_shape(p) for p in parts], {}, [3 * n, 3 * n],
                   lambda ins, outs, sems: _exchange_start(ins, outs, *sems),
                   lambda ins, outs, sems: _exchange_finish(ins, outs, *sems))


def _host_gather(bufs):
    T = _gather_items(bufs)

    def finish(ins, outs, sems):
        _gather_forward(outs, *sems)
        _gather_finish(outs, *sems)

    return _Hosted(list(bufs), [jax.ShapeDtypeStruct(b.shape, b.dtype) for b in bufs],
                   {a: a for a in range(len(bufs))}, [6 * T, 6 * T],
                   lambda ins, outs, sems: _gather_start(outs, *sems), finish)


def _host_all_devices(buf):
    return _Hosted([buf], [jax.ShapeDtypeStruct((N_DEV,) + buf.shape, buf.dtype)], {}, [N_DEV - 1, N_DEV - 1, 1],
                   lambda ins, outs, sems: _all_devices_start(ins[0], outs[0], *sems),
                   lambda ins, outs, sems: _all_devices_finish(ins[0], outs[0], *sems))


def _allgather_big(bufs, name):
    n = len(bufs)
    T = _gather_items(bufs)

    def body(*refs):
        full = refs[n:2 * n]
        ssem, rsem = refs[2 * n:]
        _gather_start(full, ssem, rsem)
        _gather_forward(full, ssem, rsem)
        _gather_finish(full, ssem, rsem)

    return pl.pallas_call(
        body, name=name, out_shape=tuple(jax.ShapeDtypeStruct(b.shape, BF16) for b in bufs),
        in_specs=[ANY] * n, out_specs=tuple([ANY] * n),
        input_output_aliases={a: a for a in range(n)},
        scratch_shapes=[pltpu.SemaphoreType.DMA((6 * T,)), pltpu.SemaphoreType.DMA((6 * T,))],
    )(*bufs)


def _allgather_small(shards, name):
    n = len(shards)
    outs = tuple(jax.ShapeDtypeStruct((N_CHIPS,) + s.shape, s.dtype) for s in shards)

    def body(*refs):
        ins, full = refs[:n], refs[n:2 * n]
        ssem, rsem, lsem = refs[2 * n:]
        x, y, c = _place()
        j = 2 * x + y
        chips = _other_chips(x, y)
        cps, locs = [], []
        for a in range(n):
            loc = pltpu.make_async_copy(ins[a], full[a].at[j], lsem.at[a])
            loc.start()
            locs.append(loc)
            for r, (px, py) in enumerate(chips):
                cp = _rcopy(ins[a], full[a].at[j], ssem.at[3 * a + r], rsem.at[3 * a + r], (px, py, c))
                cp.start()
                cps.append(cp)
        for a in range(n):
            for r, (px, py) in enumerate(chips):
                dst = full[a].at[2 * px + py]
                _rcopy(dst, dst, ssem.at[3 * a + r], rsem.at[3 * a + r], (px, py, c)).wait_recv()
        for cp in cps:
            cp.wait_send()
        for loc in locs:
            loc.wait()

    return pl.pallas_call(
        body, name=name, out_shape=outs,
        in_specs=[ANY] * n, out_specs=tuple([ANY] * n),
        scratch_shapes=[pltpu.SemaphoreType.DMA((3 * n,)), pltpu.SemaphoreType.DMA((3 * n,)),
                        pltpu.SemaphoreType.DMA((n,))],
    )(*shards)


def _pair_exchange(grads, name):
    n = len(grads)
    outs = [jax.ShapeDtypeStruct((N_CHIPS,) + g.shape[2:], BF16) for g in grads]

    def body(*refs):
        ins, got = refs[:n], refs[n:2 * n]
        ssem, rsem = refs[2 * n:]
        x, y, c = _place()
        cps = []
        for a in range(n):
            for s in range(N_CHIPS):
                cp = _rcopy(ins[a].at[s, 1 - c], got[a].at[s], ssem.at[N_CHIPS * a + s],
                            rsem.at[N_CHIPS * a + s], (x, y, 1 - c))
                cp.start()
                cps.append(cp)
        for cp in cps:
            cp.wait()

    return pl.pallas_call(
        body, name=name, out_shape=tuple(outs),
        in_specs=[ANY] * n, out_specs=tuple([ANY] * n),
        scratch_shapes=[pltpu.SemaphoreType.DMA((N_CHIPS * n,)), pltpu.SemaphoreType.DMA((N_CHIPS * n,))],
    )(*grads)


def _chip_exchange(parts, name):
    n = len(parts)

    def body(*refs):
        ins, land = refs[:n], refs[n:2 * n]
        ssem, rsem = refs[2 * n:]
        _exchange_start(ins, land, ssem, rsem)
        _exchange_finish(ins, land, ssem, rsem)

    return pl.pallas_call(
        body, name=name, out_shape=tuple(_land_shape(p) for p in parts),
        in_specs=[ANY] * n, out_specs=tuple([ANY] * n),
        scratch_shapes=[pltpu.SemaphoreType.DMA((3 * n,)), pltpu.SemaphoreType.DMA((3 * n,))],
    )(*parts)


def _half_swap(bufs, name):
    n = len(bufs)

    def body(*refs):
        full = refs[n:2 * n]
        ssem, rsem = refs[2 * n:]
        x, y, c = _place()
        cps = []
        for t in range(n):
            mine = full[t].at[c]
            cp = _rcopy(mine, mine, ssem.at[t], rsem.at[t], (x, y, 1 - c))
            cp.start()
            cps.append(cp)
        for t in range(n):
            got = full[t].at[1 - c]
            _rcopy(got, got, ssem.at[t], rsem.at[t], (x, y, 1 - c)).wait_recv()
        for cp in cps:
            cp.wait_send()

    return pl.pallas_call(
        body, name=name, out_shape=tuple(jax.ShapeDtypeStruct(b.shape, F32) for b in bufs),
        in_specs=[ANY] * n, out_specs=tuple([ANY] * n),
        input_output_aliases={a: a for a in range(n)},
        scratch_shapes=[pltpu.SemaphoreType.DMA((n,)), pltpu.SemaphoreType.DMA((n,))],
    )(*bufs)


def _flipped(x, y, c, m):
    fx, fy, fc = (m >> 2) & 1, (m >> 1) & 1, m & 1
    return x + fx - 2 * x * fx, y + fy - 2 * y * fy, c + fc - 2 * c * fc


def _all_devices_start(b_ref, o_ref, ssem, rsem, lsem):
    x, y, c = _place()
    me = 4 * x + 2 * y + c
    pltpu.make_async_copy(b_ref, o_ref.at[me], lsem.at[0]).start()
    for m in range(1, N_DEV):
        _rcopy(b_ref, o_ref.at[me], ssem.at[m - 1], rsem.at[m - 1], _flipped(x, y, c, m)).start()


def _all_devices_finish(b_ref, o_ref, ssem, rsem, lsem):
    x, y, c = _place()
    me = 4 * x + 2 * y + c
    for m in range(1, N_DEV):
        px, py, pc = _flipped(x, y, c, m)
        got = o_ref.at[4 * px + 2 * py + pc]
        _rcopy(got, got, ssem.at[m - 1], rsem.at[m - 1], (px, py, pc)).wait_recv()
    for m in range(1, N_DEV):
        _rcopy(b_ref, o_ref.at[me], ssem.at[m - 1], rsem.at[m - 1], _flipped(x, y, c, m)).wait_send()
    pltpu.make_async_copy(b_ref, o_ref.at[me], lsem.at[0]).wait()


def _pack(arrs):
    flat = jnp.concatenate([a.reshape(-1) for a in arrs])
    rows = -(-flat.shape[0] // (8 * LANES)) * 8
    flat = jnp.pad(flat, (0, rows * LANES - flat.shape[0]))
    return flat.reshape(rows, LANES)


def _unpack(buf, shapes):
    flat = buf.reshape(-1)
    out, off = [], 0
    for s in shapes:
        size = 1
        for d in s:
            size *= d
        out.append(flat[off:off + size].reshape(s))
        off += size
    return out


BIG = ("ev_w_in", "ev_w_out", "od_w_in", "od_w_out", "ffn_w_up", "ffn_w_down")
BIG_KIND = {"ev_w_in": "col", "ev_w_out": "row", "od_w_in": "col", "od_w_out": "row",
            "ffn_w_up": "col", "ffn_w_down": "row"}
SMALL_AXIS = {"ev_dw_w": 2, "ev_dw_b": None, "ev_bn_g": None, "ev_bn_b": None, "od_conv_w": 2,
              "od_pool_w": 2, "od_pool_scale": 1, "ffn_conv_w": 2, "ffn_conv_b": None, "ln_g": 2, "ln_b": 2}
WEIGHTS = ("ev_w_in", "ev_dw_w", "ev_dw_b", "ev_bn_g", "ev_bn_b", "ev_w_out", "od_w_in", "od_conv_w",
           "od_pool_w", "od_pool_scale", "od_w_out", "ffn_w_up", "ffn_conv_w", "ffn_conv_b", "ffn_w_down",
           "ln_g", "ln_b")


def _ffn_fwd(xb, w_up, w_down, conv_w, conv_b, tag, host_up=None, host_down=None):
    hu = _matmul(xb, w_up, mode="nn", b_lead=0, b_split=True, out_dtype=BF16, name=f"{tag}_up", tm=1024, tn=1408,
                 hosted=host_up)
    hu, up_outs = hu if host_up is not None else (hu, None)
    z = _ffn_act_fwd(hu, conv_w, conv_b, name=f"{tag}_act")
    y = _matmul(z, w_down, mode="nn", b_lead=0, out_dtype=F32, name=f"{tag}_down", tm=512, tn=1024,
                hosted=host_down)
    y, down_outs = y if host_down is not None else (y, None)
    return hu, z, y, up_outs, down_outs


def _ffn_bwd(drb, dr, alpha, xb, hu, z, w_up, w_down, conv_w, conv_b, tag, host_dwup=None, host_dx=None):
    g_down = _matmul(z, drb, mode="tn", out_dtype=BF16, name=f"{tag}_dwdown", tm=512, tn=1024)
    dz = _matmul(drb, w_down, mode="nt", b_lead=0, out_dtype=BF16, name=f"{tag}_dz", tm=1024, tn=1408)
    dg, du, dcw, dcb = _ffn_act_bwd(dz, hu, conv_w, conv_b, name=f"{tag}_actbwd")
    g_up = _matmul(xb, (dg, du), mode="tn", out_split=True, out_dtype=BF16, name=f"{tag}_dwup", tm=512, tn=1408,
                   hosted=host_dwup)
    g_up, dwup_outs = g_up if host_dwup is not None else (g_up, None)
    dx = _matmul((dg, du), w_up, mode="nt", b_lead=0, b_split=True, out_dtype=F32, add=dr, add_scale=alpha,
                 name=f"{tag}_dx", tm=512, tn=512, hosted=host_dx)
    dx, dx_outs = dx if host_dx is not None else (dx, None)
    return dx, g_up, g_down, dcw, dcb, dwup_outs, dx_outs


def kernel(x, ev_w_in, ev_dw_w, ev_dw_b, ev_bn_g, ev_bn_b, ev_w_out, od_w_in, od_conv_w, od_pool_w, od_pool_scale, od_w_out, ffn_w_up, ffn_conv_w, ffn_conv_b, ffn_w_down, ln_g, ln_b, loss_target, m_ev_w_in, m_ev_dw_w, m_ev_dw_b, m_ev_bn_g, m_ev_bn_b, m_ev_w_out, m_od_w_in, m_od_conv_w, m_od_pool_w, m_od_pool_scale, m_od_w_out, m_ffn_w_up, m_ffn_conv_w, m_ffn_conv_b, m_ffn_w_down, m_ln_g, m_ln_b, v_ev_w_in, v_ev_dw_w, v_ev_dw_b, v_ev_bn_g, v_ev_bn_b, v_ev_w_out, v_od_w_in, v_od_conv_w, v_od_pool_w, v_od_pool_scale, v_od_w_out, v_ffn_w_up, v_ffn_conv_w, v_ffn_conv_b, v_ffn_w_down, v_ln_g, v_ln_b):
    wts = dict(ev_w_in=ev_w_in, ev_dw_w=ev_dw_w, ev_dw_b=ev_dw_b, ev_bn_g=ev_bn_g, ev_bn_b=ev_bn_b,
               ev_w_out=ev_w_out, od_w_in=od_w_in, od_conv_w=od_conv_w, od_pool_w=od_pool_w,
               od_pool_scale=od_pool_scale, od_w_out=od_w_out, ffn_w_up=ffn_w_up, ffn_conv_w=ffn_conv_w,
               ffn_conv_b=ffn_conv_b, ffn_w_down=ffn_w_down, ln_g=ln_g, ln_b=ln_b)
    mom = dict(ev_w_in=m_ev_w_in, ev_dw_w=m_ev_dw_w, ev_dw_b=m_ev_dw_b, ev_bn_g=m_ev_bn_g, ev_bn_b=m_ev_bn_b,
               ev_w_out=m_ev_w_out, od_w_in=m_od_w_in, od_conv_w=m_od_conv_w, od_pool_w=m_od_pool_w,
               od_pool_scale=m_od_pool_scale, od_w_out=m_od_w_out, ffn_w_up=m_ffn_w_up, ffn_conv_w=m_ffn_conv_w,
               ffn_conv_b=m_ffn_conv_b, ffn_w_down=m_ffn_w_down, ln_g=m_ln_g, ln_b=m_ln_b)
    var = dict(ev_w_in=v_ev_w_in, ev_dw_w=v_ev_dw_w, ev_dw_b=v_ev_dw_b, ev_bn_g=v_ev_bn_g, ev_bn_b=v_ev_bn_b,
               ev_w_out=v_ev_w_out, od_w_in=v_od_w_in, od_conv_w=v_od_conv_w, od_pool_w=v_od_pool_w,
               od_pool_scale=v_od_pool_scale, od_w_out=v_od_w_out, ffn_w_up=v_ffn_w_up, ffn_conv_w=v_ffn_conv_w,
               ffn_conv_b=v_ffn_conv_b, ffn_w_down=v_ffn_w_down, ln_g=v_ln_g, ln_b=v_ln_b)

    S, D = x.shape[1], x.shape[2]
    depth = ln_g.shape[0]
    alpha = (2.0 * depth) ** 0.25
    A = ev_dw_b.shape[-1]
    n_heads = A // HEAD_DIM
    xi, yi, ci = _place()
    chip = 2 * xi + yi
    pos = jnp.stack([chip, ci]).astype(jnp.int32)

    bufs = {f"{k}{l}": _cast_into_gather(pos, wts[k], l, name=f"cast_{k}{l}")
            for k in BIG for l in range(wts[k].shape[0])}

    def whole(key):
        _, r, c = wts[key[:-1]].shape
        col = BIG_KIND[key[:-1]] == "col"
        return bufs[key].reshape(1, N_CHIPS, r, c) if col else bufs[key].reshape(1, N_CHIPS * r, c)

    full = {}

    def gathered_now(keys, arrays):
        bufs.update(zip(keys, arrays))
        full.update({key: whole(key) for key in keys})

    early = ("ev_w_in0", "ev_w_out0")
    under_attn = ("ffn_w_up0", "ffn_w_down0", "od_w_in0", "od_w_out0")
    gathered_now(early, _allgather_big([bufs[k] for k in early], name="gather_first"))
    small_sharded = [k for k in WEIGHTS if k not in BIG and SMALL_AXIS[k] is not None]
    gathered = _allgather_small([wts[k] for k in small_sharded], name="gather_small")
    sm = {k: wts[k] for k in WEIGHTS if k not in BIG and SMALL_AXIS[k] is None}
    for k, g4 in zip(small_sharded, gathered):
        sm[k] = jnp.concatenate([g4[t] for t in range(N_CHIPS)], axis=SMALL_AXIS[k])
    pool_w_bf = sm["od_pool_w"][0].astype(BF16)

    x0 = x[0]
    x0b = _cast_bf16(x, name="cast_x")[0]
    h0 = _matmul(x0b, full["ev_w_in0"], mode="nn", b_lead=0, b_split=True, out_dtype=BF16, name="ev_in",
                 tm=1024, tn=1280)
    o_a, tot, *rest = _attn_fwd(h0, n_heads, name="attn_fwd", gather=[bufs[k] for k in under_attn])
    gathered_now(under_attn, rest)
    u1, u3 = _evenconv_fwd(h0, sm["ev_dw_w"][0], sm["ev_dw_b"], sm["ev_bn_g"], sm["ev_bn_b"], name="evconv_fwd")
    mix0 = jnp.concatenate([o_a, u3], axis=1)
    y1 = _matmul(mix0, full["ev_w_out0"], mode="nn", b_lead=0, out_dtype=F32, name="ev_out", tm=1024, tn=1024)
    x1, x1b, xh1, rs1 = _ln_fwd(x0, y1, sm["ln_g"][0, 0][None], sm["ln_b"][0, 0][None], alpha, name="ln00")
    hu0, z0, y2, got_up, got_down = _ffn_fwd(
        x1b, full["ffn_w_up0"], full["ffn_w_down0"], sm["ffn_conv_w"][0], sm["ffn_conv_b"][0][None], "ffn0",
        host_up=_host_gather([bufs["ffn_w_up1"]]), host_down=_host_gather([bufs["ffn_w_down1"]]))
    gathered_now(("ffn_w_up1",), got_up)
    gathered_now(("ffn_w_down1",), got_down)
    x2, x2b, xh2, rs2 = _ln_fwd(x1, y2, sm["ln_g"][0, 1][None], sm["ln_b"][0, 1][None], alpha, name="ln01")
    h1 = _matmul(x2b, full["od_w_in0"], mode="nn", b_lead=0, b_split=True, out_dtype=BF16, name="od_in",
                 tm=1024, tn=1024)
    mix1 = _odd_fwd(h1, sm["od_conv_w"][0], pool_w_bf, sm["od_pool_scale"], name="odd_fwd")
    y3 = _matmul(mix1, full["od_w_out0"], mode="nn", b_lead=0, out_dtype=F32, name="od_out", tm=1024, tn=1024)
    x3, x3b, xh3, rs3 = _ln_fwd(x2, y3, sm["ln_g"][1, 0][None], sm["ln_b"][1, 0][None], alpha, name="ln10")
    hu1, z1, y4, _, _ = _ffn_fwd(x3b, full["ffn_w_up1"], full["ffn_w_down1"], sm["ffn_conv_w"][1],
                                 sm["ffn_conv_b"][1][None], "ffn1")
    x4, _, xh4, rs4 = _ln_fwd(x3, y4, sm["ln_g"][1, 1][None], sm["ln_b"][1, 1][None], alpha, name="ln11")

    dx4, loss_part = _loss_grad(x4, loss_target[0], name="loss")
    loss = lax.psum(loss_part[0, 0], ("x", "y", "c"))

    def pair_reduce(named, tag):
        g4 = []
        for k, g in named:
            rows, cols = (g.shape[1], g.shape[2]) if BIG_KIND[k] == "col" else (g.shape[0] // N_CHIPS, g.shape[1])
            g4.append(g.reshape(N_CHIPS, 2, rows // 2, cols))
        sib = _pair_exchange(g4, name=f"grad_pair_exchange_{tag}")
        return [_pair_sum(pos, g, r, name=f"grad_pair_sum_{tag}{t}") for t, (g, r) in enumerate(zip(g4, sib))]

    dr4, dr4b, dg11, db11 = _ln_bwd(dx4, xh4, rs4, sm["ln_g"][1, 1][None], name="ln11_bwd")
    dx3, g_up1, g_down1, dcw1, dcb1, _, _ = _ffn_bwd(dr4b, dr4, alpha, x3b, hu1, z1, full["ffn_w_up1"],
                                                     full["ffn_w_down1"], sm["ffn_conv_w"][1],
                                                     sm["ffn_conv_b"][1][None], "ffn1")
    parts_f1 = pair_reduce([("ffn_w_up", g_up1), ("ffn_w_down", g_down1)], "f1")
    dr3, dr3b, dg10, db10 = _ln_bwd(dx3, xh3, rs3, sm["ln_g"][1, 0][None], name="ln10_bwd")
    g_odout = _matmul(mix1, dr3b, mode="tn", out_dtype=BF16, name="od_dwout", tm=512, tn=1024)
    dmix1 = _matmul(dr3b, full["od_w_out0"], mode="nt", b_lead=0, out_dtype=BF16, name="od_dmix", tm=1024, tn=1024)
    dh1, d_odconv, d_pool, d_pscale = _odd_bwd(dmix1, h1, sm["od_conv_w"][0], pool_w_bf, sm["od_pool_scale"],
                                               name="odd_bwd")
    g_odin = _matmul(x2b, dh1, mode="tn", out_split=True, out_dtype=BF16, name="od_dwin", tm=512, tn=1024)
    dx2 = _matmul(dh1, full["od_w_in0"], mode="nt", b_lead=0, b_split=True, out_dtype=F32, add=dr3, add_scale=alpha,
                  name="od_dx", tm=1024, tn=512)
    dr2, dr2b, dg01, db01 = _ln_bwd(dx2, xh2, rs2, sm["ln_g"][0, 1][None], name="ln01_bwd")
    dx1, g_up0, g_down0, dcw0, dcb0, land_up1, land_down1 = _ffn_bwd(
        dr2b, dr2, alpha, x1b, hu0, z0, full["ffn_w_up0"], full["ffn_w_down0"], sm["ffn_conv_w"][0],
        sm["ffn_conv_b"][0][None], "ffn0",
        host_dwup=_host_exchange(parts_f1[:1]), host_dx=_host_exchange(parts_f1[1:]))
    dr1, dr1b, dg00, db00 = _ln_bwd(dx1, xh1, rs1, sm["ln_g"][0, 0][None], name="ln00_bwd")
    g_evout = _matmul(mix0, dr1b, mode="tn", out_dtype=BF16, name="ev_dwout", tm=512, tn=1024)
    dmix0 = _matmul(dr1b, full["ev_w_out0"], mode="nt", b_lead=0, out_dtype=BF16, name="ev_dmix", tm=1024, tn=1024)
    da, dgate, d_dww, d_dwb, d_bng, d_bnb = _evenconv_bwd(dmix0, u1, h0, sm["ev_dw_w"][0], sm["ev_bn_g"],
                                                          sm["ev_bn_b"], name="evconv_bwd")
    parts_b = pair_reduce([("od_w_in", g_odin), ("od_w_out", g_odout), ("ffn_w_up", g_up0),
                           ("ffn_w_down", g_down0), ("ev_w_out", g_evout)], "b")

    d_ln_g = jnp.stack([jnp.stack([dg00[0], dg01[0]]), jnp.stack([dg10[0], dg11[0]])])
    d_ln_b = jnp.stack([jnp.stack([db00[0], db01[0]]), jnp.stack([db10[0], db11[0]])])
    small_partial = {
        "ev_dw_w": d_dww[None], "ev_dw_b": d_dwb, "ev_bn_g": d_bng, "ev_bn_b": d_bnb,
        "od_conv_w": d_odconv[None], "od_pool_w": d_pool[None], "od_pool_scale": d_pscale,
        "ffn_conv_w": jnp.stack([dcw0, dcw1]), "ffn_conv_b": jnp.concatenate([dcb0, dcb1], axis=0),
        "ln_g": d_ln_g, "ln_b": d_ln_b}
    small_names = [k for k in WEIGHTS if k not in BIG]
    packed = _pack([small_partial[k] for k in small_names])
    dq, dk, dv, under_bwd = _attn_bwd(h0, dmix0, tot, n_heads, name="attn_bwd",
                                      hosted=_host_join(_host_exchange(parts_b), _host_all_devices(packed)))
    land_b, all_small = under_bwd[:-1], under_bwd[-1]
    dh0 = jnp.concatenate([dq, dk, dv, da, dgate], axis=1)
    g_evin = _matmul(x0b, dh0, mode="tn", out_split=True, out_dtype=BF16, name="ev_dwin", tm=512, tn=1280)
    parts_e = pair_reduce([("ev_w_in", g_evin)], "e")
    grad_x, land_e = _matmul(dh0, full["ev_w_in0"], mode="nt", b_lead=0, b_split=True, out_dtype=F32, add=dr1,
                             add_scale=alpha, name="ev_dx", tm=1024, tn=512, hosted=_host_exchange(parts_e))

    order = ["ffn_w_up1", "ffn_w_down1", "od_w_in0", "od_w_out0", "ffn_w_up0", "ffn_w_down0", "ev_w_out0", "ev_w_in0"]
    parts = parts_f1 + parts_b + parts_e
    land = list(land_up1) + list(land_down1) + list(land_b) + list(land_e)
    halves = [_chip_sum(pos, p, ld, name=f"grad_chip_sum_{tag}") for tag, p, ld in zip(order, parts, land)]
    reduced = dict(zip(order, _half_swap(halves, name="grad_half_swap")))
    big_grads = {k: [reduced[f"{k}{l}"].reshape(wts[k].shape[1:]) for l in range(wts[k].shape[0])] for k in BIG}

    summed = _sum_slots(all_small, name="sum_small_grads")
    small_full = dict(zip(small_names, _unpack(summed, [small_partial[k].shape for k in small_names])))
    small_grads = {}
    for k in small_names:
        ax = SMALL_AXIS[k]
        if ax is None:
            small_grads[k] = small_full[k]
        else:
            size = wts[k].shape[ax]
            small_grads[k] = lax.dynamic_slice_in_dim(small_full[k], chip * size, size, axis=ax)

    grads, delta, new_m, new_v = {}, {}, {}, {}
    for k in BIG:
        grads[k], delta[k], new_m[k], new_v[k] = _adamw(wts[k], big_grads[k], mom[k], var[k], name=f"adamw_{k}")
    shapes = [wts[k].shape for k in small_names]
    pw, pg, pm, pv = (_pack([d[k] for k in small_names]) for d in (wts, small_grads, mom, var))
    sg, sd, smn, svn = _adamw(pw[None], [pg], pm[None], pv[None], name="adamw_small")
    for dst, buf in ((grads, sg), (delta, sd), (new_m, smn), (new_v, svn)):
        for k, a in zip(small_names, _unpack(buf[0], shapes)):
            dst[k] = a

    return (loss, grad_x[None], *[grads[k] for k in WEIGHTS], *[delta[k] for k in WEIGHTS],
            *[new_m[k] for k in WEIGHTS], *[new_v[k] for k in WEIGHTS])
```

```python
import collections

import jax
import jax.numpy as jnp
from jax import lax
from jax.experimental import pallas as pl
from jax.experimental.pallas import tpu as pltpu

F32 = jnp.float32
BF16 = jnp.bfloat16

HEAD_DIM = 128
POOL_WINDOWS = (2, 4, 8, 16)
LN_EPS = 1e-5
ADAM_LR = 0.001
ADAM_B1 = 0.9
ADAM_B2 = 0.999
ADAM_EPS = 1e-08
ADAM_WD = 0.01
ADAM_STEP = 10
N_CHIPS = 4
N_DEV = 8
MESH = pl.DeviceIdType.MESH
LANES = 128
HALO3 = 16
HALO31 = 32
ROW_CHUNK = 32

ANY = pl.BlockSpec(memory_space=pl.ANY)


def _pick(n, pref, mult=LANES):
    if n <= pref:
        return n
    t = (pref // mult) * mult
    while t >= mult:
        if n % t == 0:
            return t
        t -= mult
    return n


def _params(*sem):
    return pltpu.CompilerParams(dimension_semantics=sem)


def _matmul(a, b, *, mode, out_dtype, name, b_lead=None, b_split=False, out_split=False, add=None,
            add_scale=1.0, tm=512, tn=1024, tk=None, hosted=None):
    halves = isinstance(a, tuple) or isinstance(b, tuple)
    if isinstance(a, tuple):
        assert mode == "nt" and b_split and tk is None
        ash = (a[0].shape[0], 2 * a[0].shape[1])
    else:
        ash = a.shape[-2:]
    if isinstance(b, tuple):
        assert mode == "tn" and tk is None
        bsh = (b[0].shape[0], 2 * b[0].shape[1])
    else:
        bsh = b.shape[-2:]
    if mode == "nn":
        (M, K), (K2, N) = ash, bsh
        if b_split:
            N = N * N_CHIPS
    elif mode == "nt":
        (M, K), (N, K2) = ash, bsh
        if b_split:
            K2 = K2 * N_CHIPS
    else:
        (K, M), (K2, N) = ash, bsh
    assert K == K2, (ash, bsh, mode)
    tm = _pick(M, tm)
    tn = _pick(N // N_CHIPS if (out_split or (b_split and mode == "nn")) else N, tn)
    whole_split_k = b_split and mode == "nt" and tk is None
    if tk is None:
        tk = K
    else:
        tk = _pick(K // N_CHIPS if (b_split and mode == "nt") else K, tk)
    nk = K // tk
    kq = K // N_CHIPS
    n_per = (N // N_CHIPS) // tn
    k_per = (K // N_CHIPS) // tk

    def lead(shape, idx):
        if b_lead is None:
            return pl.BlockSpec(shape, idx)
        return pl.BlockSpec((None,) + shape, lambda i, j, k: (b_lead,) + idx(i, j, k))

    if mode == "nn":
        a_spec = pl.BlockSpec((tm, tk), lambda i, j, k: (i, k))
        if b_split:
            b_spec = lead((None, tk, tn), lambda i, j, k: (lax.div(j, n_per), k, lax.rem(j, n_per)))
        else:
            b_spec = lead((tk, tn), lambda i, j, k: (k, j))
        dims = (((1,), (0,)), ((), ()))
    elif mode == "nt":
        a_spec = pl.BlockSpec((tm, tk), lambda i, j, k: (i, k))
        if whole_split_k:
            b_spec = lead((N_CHIPS, tn, kq), lambda i, j, k: (0, j, 0))
        elif b_split:
            b_spec = lead((None, tn, tk), lambda i, j, k: (lax.div(k, k_per), j, lax.rem(k, k_per)))
        else:
            b_spec = lead((tn, tk), lambda i, j, k: (j, k))
        dims = (((1,), (1,)), ((), ()))
    else:
        a_spec = pl.BlockSpec((tk, tm), lambda i, j, k: (k, i))
        b_spec = pl.BlockSpec((tk, tn), lambda i, j, k: (k, j))
        dims = (((0,), (0,)), ((), ()))
    if out_split:
        out_shape = jax.ShapeDtypeStruct((N_CHIPS, M, N // N_CHIPS), out_dtype)
        out_spec = pl.BlockSpec((None, tm, tn), lambda i, j, k: (lax.div(j, n_per), i, lax.rem(j, n_per)))
    else:
        out_shape = jax.ShapeDtypeStruct((M, N), out_dtype)
        out_spec = pl.BlockSpec((tm, tn), lambda i, j, k: (i, j))
    grid = (M // tm, N // tn, nk)
    nj_half = grid[1] // 2
    if isinstance(a, tuple):
        in_specs = [pl.BlockSpec((tm, K // 2), lambda i, j, k: (i, 0))] * 2 + [b_spec]
        args = [a[0], a[1], b]
    elif isinstance(b, tuple):
        in_specs = [a_spec,
                    pl.BlockSpec((tk, tn), lambda i, j, k: (k, jnp.minimum(j, nj_half - 1))),
                    pl.BlockSpec((tk, tn), lambda i, j, k: (k, jnp.maximum(j - nj_half, 0)))]
        args = [a, b[0], b[1]]
    else:
        in_specs = [a_spec, b_spec]
        args = [a, b]
    n_op = len(args)
    if add is not None:
        in_specs.append(pl.BlockSpec((tm, tn), lambda i, j, k: (i, j)))
        args.append(add)

    n_in = len(args)
    h_in = 0 if hosted is None else len(hosted.ins)
    h_out = 0 if hosted is None else len(hosted.outs)

    def body(*refs):
        ops = refs[:n_op]
        add_ref = refs[n_op] if add is not None else None
        h_ins = refs[n_in:n_in + h_in]
        o_ref = refs[n_in + h_in]
        h_outs = refs[n_in + h_in + 1:n_in + h_in + 1 + h_out]
        scr = refs[n_in + h_in + 1 + h_out:]
        i, j, k = pl.program_id(0), pl.program_id(1), pl.program_id(2)
        if hosted is not None:
            sems = scr[len(scr) - len(hosted.sems):]

            @pl.when((i == 0) & (j == 0) & (k == 0))
            def _():
                hosted.start(h_ins, h_outs, sems)

        def finish(res):
            if add_ref is not None:
                res = res + add_scale * add_ref[...]
            o_ref[...] = res.astype(out_dtype)

        def dot(x, y):
            return lax.dot_general(x, y, dims, preferred_element_type=F32)

        if isinstance(b, tuple):
            @pl.when(j < nj_half)
            def _():
                finish(dot(ops[0][...], ops[1][...]))

            @pl.when(j >= nj_half)
            def _():
                finish(dot(ops[0][...], ops[2][...]))
            part = None
        elif whole_split_k:
            srcs = [(ops[0], s) for s in range(N_CHIPS)] if not isinstance(a, tuple) else \
                   [(ops[s // 2], s % 2) for s in range(N_CHIPS)]
            b_ref = ops[-1]
            part = None
            for s, (src, off) in enumerate(srcs):
                term = dot(src[:, off * kq:(off + 1) * kq], b_ref[s])
                part = term if part is None else part + term
        else:
            part = dot(ops[0][...], ops[1][...])

        if part is None:
            pass
        elif nk == 1:
            finish(part)
        else:
            acc = scr[0]

            @pl.when(k == 0)
            def _():
                acc[...] = part

            @pl.when(k > 0)
            def _():
                acc[...] += part

            @pl.when(k == nk - 1)
            def _():
                finish(acc[...])

        if hosted is not None:
            @pl.when((i == grid[0] - 1) & (j == grid[1] - 1) & (k == nk - 1))
            def _():
                hosted.finish(h_ins, h_outs, sems)

    scratch = [pltpu.VMEM((tm, tn), F32)] if nk > 1 else []
    if hosted is not None:
        res = pl.pallas_call(
            body, name=name,
            out_shape=(out_shape,) + tuple(hosted.outs),
            grid=grid,
            in_specs=in_specs + [ANY] * h_in,
            out_specs=(out_spec,) + (ANY,) * h_out,
            input_output_aliases={n_in + src: 1 + dst for src, dst in hosted.alias.items()},
            scratch_shapes=scratch + [pltpu.SemaphoreType.DMA((n,)) for n in hosted.sems],
            compiler_params=_params("arbitrary", "arbitrary", "arbitrary"),
        )(*args, *hosted.ins)
        return res[0], list(res[1:])
    return pl.pallas_call(
        body, name=name,
        out_shape=out_shape,
        grid=grid,
        in_specs=in_specs,
        out_specs=out_spec,
        scratch_shapes=scratch,
        compiler_params=_params("parallel", "parallel", "arbitrary"),
    )(*args)


def _cast_bf16(w, name):
    L, R, C = w.shape
    tr, tc = _pick(R, 512, 16), _pick(C, 1408)

    def body(w_ref, o_ref):
        o_ref[...] = w_ref[...].astype(BF16)

    return pl.pallas_call(
        body, name=name, out_shape=jax.ShapeDtypeStruct(w.shape, BF16),
        grid=(L, R // tr, C // tc),
        in_specs=[pl.BlockSpec((None, tr, tc), lambda l, i, j: (l, i, j))],
        out_specs=pl.BlockSpec((None, tr, tc), lambda l, i, j: (l, i, j)),
        compiler_params=_params("parallel", "parallel", "parallel"),
    )(w)


def _cast_into_gather(pos, w, layer, name):
    L, R, C = w.shape
    r2 = R // 2
    tr, tc = _pick(r2, 512, 16), _pick(C, 1408)

    def body(p_ref, w_ref, o_ref):
        o_ref[...] = w_ref[...].astype(BF16)

    return pl.pallas_call(
        body, name=name, out_shape=jax.ShapeDtypeStruct((1, N_CHIPS, 2, r2, C), BF16),
        grid_spec=pltpu.PrefetchScalarGridSpec(
            num_scalar_prefetch=1, grid=(2, r2 // tr, C // tc),
            in_specs=[pl.BlockSpec((None, None, tr, tc), lambda h, i, j, p: (layer, h, i, j))],
            out_specs=pl.BlockSpec((None, None, None, tr, tc), lambda h, i, j, p: (0, p[0], h, i, j))),
        compiler_params=_params("parallel", "parallel", "parallel"),
    )(pos, w.reshape(L, 2, r2, C))


def _sigmoid(v):
    return 0.5 * jnp.tanh(0.5 * v) + 0.5


def _ln_fwd(x, y, g, b, alpha, name):
    S, D = x.shape
    tr = _pick(S, 256, 8)

    def body(x_ref, y_ref, g_ref, b_ref, o_ref, ob_ref, xh_ref, rs_ref):
        r = alpha * x_ref[...] + y_ref[...]
        mu = jnp.mean(r, axis=-1, keepdims=True)
        d = r - mu
        var = jnp.mean(d * d, axis=-1, keepdims=True)
        rstd = lax.rsqrt(var + LN_EPS)
        xh = d * rstd
        o = xh * g_ref[...] + b_ref[...]
        o_ref[...] = o
        ob_ref[...] = o.astype(BF16)
        xh_ref[...] = xh
        rs_ref[...] = rstd

    row = pl.BlockSpec((tr, D), lambda i: (i, 0))
    vec = pl.BlockSpec((1, D), lambda i: (0, 0))
    return pl.pallas_call(
        body, name=name,
        out_shape=(jax.ShapeDtypeStruct((S, D), F32), jax.ShapeDtypeStruct((S, D), BF16),
                   jax.ShapeDtypeStruct((S, D), F32), jax.ShapeDtypeStruct((S, 1), F32)),
        grid=(S // tr,),
        in_specs=[row, row, vec, vec],
        out_specs=(row, row, row, pl.BlockSpec((tr, 1), lambda i: (i, 0))),
        compiler_params=_params("parallel"),
    )(x, y, g, b)


def _ln_bwd(dout, xhat, rstd, g, name):
    S, D = dout.shape
    tr = _pick(S, 256, 8)

    def body(do_ref, xh_ref, rs_ref, g_ref, dr_ref, drb_ref, dg_ref, db_ref):
        i = pl.program_id(0)
        do = do_ref[...]
        xh = xh_ref[...]
        dxh = do * g_ref[...]
        m1 = jnp.mean(dxh, axis=-1, keepdims=True)
        m2 = jnp.mean(dxh * xh, axis=-1, keepdims=True)
        dr = rs_ref[...] * (dxh - m1 - xh * m2)
        dr_ref[...] = dr
        drb_ref[...] = dr.astype(BF16)
        pg = jnp.sum(do * xh, axis=0, keepdims=True)
        pb = jnp.sum(do, axis=0, keepdims=True)

        @pl.when(i == 0)
        def _():
            dg_ref[...] = pg
            db_ref[...] = pb

        @pl.when(i > 0)
        def _():
            dg_ref[...] += pg
            db_ref[...] += pb

    row = pl.BlockSpec((tr, D), lambda i: (i, 0))
    vec = pl.BlockSpec((1, D), lambda i: (0, 0))
    return pl.pallas_call(
        body, name=name,
        out_shape=(jax.ShapeDtypeStruct((S, D), F32), jax.ShapeDtypeStruct((S, D), BF16),
                   jax.ShapeDtypeStruct((1, D), F32), jax.ShapeDtypeStruct((1, D), F32)),
        grid=(S // tr,),
        in_specs=[row, row, pl.BlockSpec((tr, 1), lambda i: (i, 0)), vec],
        out_specs=(row, row, vec, vec),
        compiler_params=_params("arbitrary"),
    )(dout, xhat, rstd, g)


def _loss_grad(y, target, name):
    S, D = y.shape
    tr = _pick(S, 256, 8)
    n = S // tr

    def body(y_ref, t_ref, dy_ref, l_ref, acc):
        i = pl.program_id(0)
        d = y_ref[...] - t_ref[...]
        dy_ref[...] = d * (1.0 / D)
        p = jnp.sum(d * d, axis=0, keepdims=True)

        @pl.when(i == 0)
        def _():
            acc[...] = p

        @pl.when(i > 0)
        def _():
            acc[...] += p

        @pl.when(i == n - 1)
        def _():
            l_ref[...] = (0.5 / D) * jnp.sum(acc[...], axis=1, keepdims=True)

    row = pl.BlockSpec((tr, D), lambda i: (i, 0))
    return pl.pallas_call(
        body, name=name,
        out_shape=(jax.ShapeDtypeStruct((S, D), F32), jax.ShapeDtypeStruct((1, 1), F32)),
        grid=(n,),
        in_specs=[row, row],
        out_specs=(row, pl.BlockSpec((1, 1), lambda i: (0, 0))),
        scratch_shapes=[pltpu.VMEM((1, D), F32)],
        compiler_params=_params("arbitrary"),
    )(y, target)


def _prev_spec(tr, halo, width, col):
    return pl.BlockSpec((halo, width), lambda c, i: (jnp.maximum(i * (tr // halo) - 1, 0), col(c)))


def _next_spec(tr, halo, width, col, nrows):
    last = nrows // halo - 1
    return pl.BlockSpec((halo, width), lambda c, i: (jnp.minimum((i + 1) * (tr // halo), last), col(c)))


def _cur_spec(tr, width, col):
    return pl.BlockSpec((tr, width), lambda c, i: (i, col(c)))


def _ffn_act_fwd(hu, conv_w, conv_b, name):
    S, F2 = hu.shape
    F = F2 // 2
    tr, tc, H = _pick(S, 512, 16), _pick(F, 512), HALO3
    nc, nr = F // tc, S // tr
    rc = min(ROW_CHUNK, tr)

    def body(gp_ref, g_ref, u_ref, w_ref, b_ref, z_ref, G):
        i = pl.program_id(1)
        G[0:H, :] = jnp.where(i > 0, gp_ref[...].astype(F32), 0.0)
        G[H:H + tr, :] = g_ref[...].astype(F32)
        w0, w1, w2, b = w_ref[pl.ds(0, 1), :], w_ref[pl.ds(1, 1), :], w_ref[pl.ds(2, 1), :], b_ref[...]
        for r0 in range(0, tr, rc):
            gc = b + w0 * G[pl.ds(H - 2 + r0, rc), :] + w1 * G[pl.ds(H - 1 + r0, rc), :] + w2 * G[pl.ds(H + r0, rc), :]
            z = gc * _sigmoid(gc) * u_ref[pl.ds(r0, rc), :].astype(F32)
            z_ref[pl.ds(r0, rc), :] = z.astype(BF16)

    gcol = lambda c: c
    ucol = lambda c: c + nc
    return pl.pallas_call(
        body, name=name, out_shape=jax.ShapeDtypeStruct((S, F), BF16),
        grid=(nc, nr),
        in_specs=[_prev_spec(tr, H, tc, gcol), _cur_spec(tr, tc, gcol), _cur_spec(tr, tc, ucol),
                  pl.BlockSpec((3, tc), lambda c, i: (0, c)), pl.BlockSpec((1, tc), lambda c, i: (0, c))],
        out_specs=pl.BlockSpec((tr, tc), lambda c, i: (i, c)),
        scratch_shapes=[pltpu.VMEM((H + tr, tc), F32)],
        compiler_params=_params("parallel", "parallel"),
    )(hu, hu, hu, conv_w, conv_b)


def _ffn_act_bwd(dz, hu, conv_w, conv_b, name):
    S, F = dz.shape
    tr, tc, H = _pick(S, 512, 16), _pick(F, 512), HALO3
    nc, nr = F // tc, S // tr
    n = tr + H
    rc = min(ROW_CHUNK, tr)

    def body(dz_ref, dzn_ref, gp_ref, g_ref, gn_ref, u_ref, un_ref, w_ref, b_ref,
             dg_ref, du_ref, dw_ref, db_ref, G, DG):
        i = pl.program_id(1)
        G[0:H, :] = jnp.where(i > 0, gp_ref[...].astype(F32), 0.0)
        G[H:H + tr, :] = g_ref[...].astype(F32)
        G[H + tr:H + tr + H, :] = gn_ref[...].astype(F32)
        w0, w1, w2, b = w_ref[pl.ds(0, 1), :], w_ref[pl.ds(1, 1), :], w_ref[pl.ds(2, 1), :], b_ref[...]

        def fold(v):
            return jnp.sum(v.reshape(v.shape[0] // 8, 8, tc), axis=0)

        def d_gate(r0, rows, dzf, uf):
            taps = [G[pl.ds(H - 2 + k + r0, rows), :] for k in range(3)]
            gc = b + w0 * taps[0] + w1 * taps[1] + w2 * taps[2]
            sg = _sigmoid(gc)
            return dzf * uf * (sg * (1.0 + gc * (1.0 - sg))), gc * sg, taps

        acc_w = [jnp.zeros((8, tc), F32) for _ in range(3)]
        acc_b = jnp.zeros((8, tc), F32)
        for r0 in range(0, tr, rc):
            dzf = dz_ref[pl.ds(r0, rc), :].astype(F32)
            dgc, silu, taps = d_gate(r0, rc, dzf, u_ref[pl.ds(r0, rc), :].astype(F32))
            du_ref[pl.ds(r0, rc), :] = (dzf * silu).astype(BF16)
            DG[pl.ds(r0, rc), :] = dgc
            acc_w = [acc_w[k] + fold(dgc * taps[k]) for k in range(3)]
            acc_b = acc_b + fold(dgc)
        dzn = jnp.where(i < nr - 1, dzn_ref[...].astype(F32), 0.0)
        DG[pl.ds(tr, H), :] = d_gate(tr, H, dzn, un_ref[...].astype(F32))[0]
        for r0 in range(0, tr, rc):
            dg = w2 * DG[pl.ds(r0, rc), :] + w1 * DG[pl.ds(r0 + 1, rc), :] + w0 * DG[pl.ds(r0 + 2, rc), :]
            dg_ref[pl.ds(r0, rc), :] = dg.astype(BF16)
        pw = [jnp.sum(a, axis=0, keepdims=True) for a in acc_w]
        pb = jnp.sum(acc_b, axis=0, keepdims=True)

        @pl.when(i == 0)
        def _():
            for k in range(3):
                dw_ref[pl.ds(k, 1), :] = pw[k]
            db_ref[...] = pb

        @pl.when(i > 0)
        def _():
            for k in range(3):
                dw_ref[pl.ds(k, 1), :] += pw[k]
            db_ref[...] += pb

    gcol = lambda c: c
    ucol = lambda c: c + nc
    blk = pl.BlockSpec((tr, tc), lambda c, i: (i, c))
    return pl.pallas_call(
        body, name=name,
        out_shape=(jax.ShapeDtypeStruct((S, F), BF16), jax.ShapeDtypeStruct((S, F), BF16),
                   jax.ShapeDtypeStruct((3, F), F32), jax.ShapeDtypeStruct((1, F), F32)),
        grid=(nc, nr),
        in_specs=[_cur_spec(tr, tc, gcol), _next_spec(tr, H, tc, gcol, S),
                  _prev_spec(tr, H, tc, gcol), _cur_spec(tr, tc, gcol), _next_spec(tr, H, tc, gcol, S),
                  _cur_spec(tr, tc, ucol), _next_spec(tr, H, tc, ucol, S),
                  pl.BlockSpec((3, tc), lambda c, i: (0, c)), pl.BlockSpec((1, tc), lambda c, i: (0, c))],
        out_specs=(blk, blk, pl.BlockSpec((3, tc), lambda c, i: (0, c)), pl.BlockSpec((1, tc), lambda c, i: (0, c))),
        scratch_shapes=[pltpu.VMEM((H + tr + H, tc), F32), pltpu.VMEM((n, tc), F32)],
        compiler_params=_params("parallel", "arbitrary"),
    )(dz, dz, hu, hu, hu, hu, hu, conv_w, conv_b)


def _softplus_neg(s):
    return jnp.minimum(-s, 0.0) - jnp.log(1.0 + jnp.exp(-jnp.abs(s)))


def _hilo_dot(v, m):
    hi = v.astype(BF16)
    lo = (v - hi.astype(F32)).astype(BF16)
    return (jnp.dot(hi, m, preferred_element_type=F32) + jnp.dot(lo, m, preferred_element_type=F32))


def _attn_fwd(h, n_heads, name, gather=()):
    S = h.shape[0]
    dh = HEAD_DIM
    A = n_heads * dh
    tq = _pick(S, 256)
    nq = S // tq
    scale = 1.0 / float(dh) ** 0.5
    ng = len(gather)
    hp = 2 if n_heads % 2 == 0 else 1
    n_grp, hw = n_heads // hp, hp * dh

    def body(*refs):
        q_ref, k_ref, v_ref = refs[:3]
        o_ref, tot_ref = refs[3 + ng:5 + ng]
        full = refs[5 + ng:5 + 2 * ng]
        hd = pl.program_id(0)
        i = pl.program_id(1)
        if ng:
            ssem, rsem = refs[5 + 2 * ng:]

            @pl.when((hd == 0) & (i == 0))
            def _():
                _gather_start(full, ssem, rsem)

            @pl.when((hd == n_grp - 1) & (i == 0))
            def _():
                _gather_forward(full, ssem, rsem)

        heads = range(hp)
        qs = [q_ref[:, h * dh:(h + 1) * dh] for h in heads]
        r_io = lax.broadcasted_iota(jnp.int32, (tq, tq), 0)
        c_io = lax.broadcasted_iota(jnp.int32, (tq, tq), 1)
        later = (r_io > c_io).astype(BF16)
        causal = c_io < r_io

        def rows(ref, j):
            blk = ref[pl.ds(pl.multiple_of(j * tq, tq), tq), :]
            return [blk[:, h * dh:(h + 1) * dh] for h in heads]

        def qk(kj):
            return [lax.dot_general(qs[h], kj[h], (((1,), (1,)), ((), ())), preferred_element_type=F32) * scale
                    for h in heads]

        def log_weights(s, diag):
            base, tot = [], []
            for h in heads:
                ls = _softplus_neg(s[h])
                if diag:
                    ls = jnp.where(causal, ls, 0.0)
                cs = _hilo_dot(ls, later)
                b = s[h] + ls + cs
                base.append(jnp.where(causal, b, -1e30) if diag else b)
                tot.append(cs[:, 0:1] + ls[:, 0:1])
            return base, tot

        def weigh(vj, acc, run, base):
            out = []
            for h in heads:
                w = jnp.exp(base[h] + run[h])
                out.append(acc[h] + jnp.dot(w.astype(BF16), vj[h], preferred_element_type=F32))
            return out

        def trip(t, carry):
            acc, run, base, tot = carry
            j = i - 1 - t
            s = qk(rows(k_ref, j))
            acc = weigh(rows(v_ref, j + 1), acc, run, base)
            base_n, tot_n = log_weights(s, False)
            return acc, [run[h] + tot[h] for h in heads], base_n, tot_n

        base, tot = log_weights(qk(rows(k_ref, i)), True)
        carry = ([jnp.zeros((tq, dh), F32) for _ in heads], [jnp.zeros((tq, 1), F32) for _ in heads], base, tot)
        acc, run, base, tot = lax.fori_loop(0, i, trip, carry)
        acc = weigh(rows(v_ref, 0), acc, run, base)
        for h in heads:
            o_ref[:, h * dh:(h + 1) * dh] = acc[h].astype(BF16)
            tot_ref[h] = jnp.broadcast_to(run[h] + tot[h], (tq, LANES))
        if ng:
            @pl.when((hd == n_grp - 1) & (i == nq - 1))
            def _():
                _gather_finish(full, ssem, rsem)

    T = _gather_items(gather) if ng else 0
    return pl.pallas_call(
        body, name=name,
        out_shape=(jax.ShapeDtypeStruct((S, A), BF16), jax.ShapeDtypeStruct((n_heads, S, LANES), F32))
        + tuple(jax.ShapeDtypeStruct(b.shape, b.dtype) for b in gather),
        grid=(n_grp, nq),
        in_specs=[pl.BlockSpec((tq, hw), lambda hd, i: (i, hd)),
                  pl.BlockSpec((S, hw), lambda hd, i: (0, n_grp + hd)),
                  pl.BlockSpec((S, hw), lambda hd, i: (0, 2 * n_grp + hd))] + [ANY] * ng,
        out_specs=(pl.BlockSpec((tq, hw), lambda hd, i: (i, hd)),
                   pl.BlockSpec((hp, tq, LANES), lambda hd, i: (hd, i, 0))) + (ANY,) * ng,
        input_output_aliases={3 + a: 2 + a for a in range(ng)},
        scratch_shapes=[pltpu.SemaphoreType.DMA((6 * T,)), pltpu.SemaphoreType.DMA((6 * T,))] if ng else [],
        compiler_params=_params("arbitrary", "arbitrary") if ng else _params("parallel", "parallel"),
    )(h, h, h, *gather)


def _attn_bwd(h, do, tot, n_heads, name, hosted=None):
    S = h.shape[0]
    dh = HEAD_DIM
    A = n_heads * dh
    tq = _pick(S, 256)
    nq = S // tq
    scale = 1.0 / float(dh) ** 0.5
    nt_dims = (((1,), (1,)), ((), ()))
    tn_dims = (((0,), (0,)), ((), ()))
    hp = 2 if n_heads % 2 == 0 else 1
    n_grp, hw = n_heads // hp, hp * dh
    h_in = 0 if hosted is None else len(hosted.ins)
    h_out = 0 if hosted is None else len(hosted.outs)

    def body(*refs):
        q_ref, k_ref, v_ref, do_ref, tot_ref = refs[:5]
        h_ins = refs[5:5 + h_in]
        dq_ref, dk_ref, dv_ref = refs[5 + h_in:8 + h_in]
        h_outs = refs[8 + h_in:8 + h_in + h_out]
        dk_acc, dv_acc = refs[8 + h_in + h_out:10 + h_in + h_out]
        sems = refs[10 + h_in + h_out:]
        hd = pl.program_id(0)
        i = pl.program_id(1)
        if hosted is not None:
            @pl.when((hd == 0) & (i == 0))
            def _():
                hosted.start(h_ins, h_outs, sems)

        @pl.when(i == 0)
        def _():
            dk_acc[...] = jnp.zeros_like(dk_acc)
            dv_acc[...] = jnp.zeros_like(dv_acc)

        heads = range(hp)
        qs = [q_ref[:, h * dh:(h + 1) * dh] for h in heads]
        dos = [do_ref[:, h * dh:(h + 1) * dh] for h in heads]
        total = [tot_ref[h][:, 0:1] for h in heads]
        r_io = lax.broadcasted_iota(jnp.int32, (tq, tq), 0)
        c_io = lax.broadcasted_iota(jnp.int32, (tq, tq), 1)
        upto = (r_io <= c_io).astype(BF16)
        before = (r_io < c_io).astype(BF16)
        causal = c_io < r_io

        def rows(ref, j):
            blk = ref[pl.ds(pl.multiple_of(j * tq, tq), tq), :]
            return [blk[:, h * dh:(h + 1) * dh] for h in heads]

        def qk(kj):
            return [lax.dot_general(qs[h], kj[h], nt_dims, preferred_element_type=F32) * scale for h in heads]

        def weights(base, prun, vj):
            dw = [lax.dot_general(dos[h], vj[h], nt_dims, preferred_element_type=F32) for h in heads]
            w, e, ce = [], [], []
            for h in heads:
                w.append(jnp.exp(base[h] + (total[h] - prun[h])))
                e.append(dw[h] * w[h])
                ce.append(jnp.dot(e[h].astype(BF16), before, preferred_element_type=F32))
            return w, e, ce

        def prefix(s, j):
            keep = jnp.logical_or(causal, j != i)
            ls = [jnp.where(keep, _softplus_neg(s[h]), 0.0) for h in heads]
            return keep, ls, [_hilo_dot(ls[h], upto) for h in heads]

        def grads(j, kj, dq, erun, w, e, ce, sn):
            start = pl.multiple_of(j * tq, tq)
            out = []
            for h in heads:
                ecum = ce[h] + erun[h]
                dz = e[h] * sn[h] - (1.0 - sn[h]) * ecum
                ds = (dz * scale).astype(BF16)
                cols = slice(h * dh, (h + 1) * dh)
                dv_acc[pl.ds(start, tq), cols] += lax.dot_general(w[h].astype(BF16), dos[h], tn_dims,
                                                                  preferred_element_type=F32)
                out.append(dq[h] + jnp.dot(ds, kj[h], preferred_element_type=F32))
                dk_acc[pl.ds(start, tq), cols] += lax.dot_general(ds, qs[h], tn_dims, preferred_element_type=F32)
            return out, [erun[h] + ce[h][:, tq - 1:tq] + e[h][:, tq - 1:tq] for h in heads]

        def carried(s, keep, ls, cs):
            base = [jnp.where(keep, s[h] + ls[h] - cs[h], -1e30) for h in heads]
            return base, [jnp.exp(ls[h]) for h in heads], [cs[h][:, tq - 1:tq] for h in heads]

        def trip(j, carry):
            dq, prun, erun, base, sn, ptot = carry
            s_n = qk(rows(k_ref, j + 1))
            w, e, ce = weights(base, prun, rows(v_ref, j))
            keep, ls_n, cs = prefix(s_n, j + 1)
            dq, erun = grads(j, rows(k_ref, j), dq, erun, w, e, ce, sn)
            return (dq, [prun[h] + ptot[h] for h in heads], erun) + carried(s_n, keep, ls_n, cs)

        zeros = [jnp.zeros((tq, 1), F32) for _ in heads]
        s0 = qk(rows(k_ref, 0))
        first = carried(s0, *prefix(s0, 0))
        carry = lax.fori_loop(0, i, trip, ([jnp.zeros((tq, dh), F32) for _ in heads], zeros, zeros) + first)
        dq, prun, erun, base, sn, _ = carry
        dq, _ = grads(i, rows(k_ref, i), dq, erun, *weights(base, prun, rows(v_ref, i)), sn)
        for h in heads:
            dq_ref[:, h * dh:(h + 1) * dh] = dq[h].astype(BF16)

        @pl.when(i == nq - 1)
        def _():
            dk_ref[...] = dk_acc[...].astype(BF16)
            dv_ref[...] = dv_acc[...].astype(BF16)

        if hosted is not None:
            @pl.when((hd == n_grp - 1) & (i == nq - 1))
            def _():
                hosted.finish(h_ins, h_outs, sems)

    qblk = pl.BlockSpec((tq, hw), lambda hd, i: (i, hd))
    full = pl.BlockSpec((S, hw), lambda hd, i: (0, hd))
    scratch = [pltpu.VMEM((S, hw), F32), pltpu.VMEM((S, hw), F32)]
    if hosted is not None:
        scratch += [pltpu.SemaphoreType.DMA((n,)) for n in hosted.sems]
    res = pl.pallas_call(
        body, name=name,
        out_shape=tuple(jax.ShapeDtypeStruct((S, A), BF16) for _ in range(3))
        + (tuple(hosted.outs) if hosted is not None else ()),
        grid=(n_grp, nq),
        in_specs=[qblk,
                  pl.BlockSpec((S, hw), lambda hd, i: (0, n_grp + hd)),
                  pl.BlockSpec((S, hw), lambda hd, i: (0, 2 * n_grp + hd)),
                  qblk,
                  pl.BlockSpec((hp, tq, LANES), lambda hd, i: (hd, i, 0))] + [ANY] * h_in,
        out_specs=(qblk, full, full) + (ANY,) * h_out,
        input_output_aliases={} if hosted is None else {5 + a: 3 + b for a, b in hosted.alias.items()},
        scratch_shapes=scratch,
        compiler_params=_params("arbitrary", "arbitrary") if hosted is not None else _params("parallel", "arbitrary"),
    )(h, h, h, do, tot, *(hosted.ins if hosted is not None else ()))
    return res[0], res[1], res[2], list(res[3:])


def _evenconv_fwd(h, dw_w, dw_b, bn_g, bn_b, name):
    S = h.shape[0]
    KW, A = dw_w.shape
    H = HALO31
    tr = _pick(S, 256, H)
    first_tap = H - (KW - 1)

    def body(ap_ref, a_ref, gp_ref, g_ref, w_ref, b_ref, bg_ref, bb_ref, u1_ref, u3_ref, U):
        i = pl.program_id(1)
        glu_prev = ap_ref[...].astype(F32) * _sigmoid(gp_ref[...].astype(F32))
        U[0:H, :] = jnp.where(i > 0, glu_prev, 0.0)
        U[H:H + tr, :] = a_ref[...].astype(F32) * _sigmoid(g_ref[...].astype(F32))
        acc = b_ref[...] + w_ref[pl.ds(0, 1), :] * U[pl.ds(first_tap, tr), :]
        for k in range(1, KW):
            acc = acc + w_ref[pl.ds(k, 1), :] * U[pl.ds(first_tap + k, tr), :]
        u1_ref[...] = acc
        mu = jnp.mean(acc, axis=-1, keepdims=True)
        d = acc - mu
        var = jnp.mean(d * d, axis=-1, keepdims=True)
        u2 = d * lax.rsqrt(var + LN_EPS) * bg_ref[...] + bb_ref[...]
        u3_ref[...] = (u2 * _sigmoid(u2)).astype(BF16)

    acol = lambda c: 3
    gcol = lambda c: 4
    vec = pl.BlockSpec((1, A), lambda c, i: (0, 0))
    blk = pl.BlockSpec((tr, A), lambda c, i: (i, 0))
    return pl.pallas_call(
        body, name=name,
        out_shape=(jax.ShapeDtypeStruct((S, A), F32), jax.ShapeDtypeStruct((S, A), BF16)),
        grid=(1, S // tr),
        in_specs=[_prev_spec(tr, H, A, acol), _cur_spec(tr, A, acol),
                  _prev_spec(tr, H, A, gcol), _cur_spec(tr, A, gcol),
                  pl.BlockSpec((KW, A), lambda c, i: (0, 0)), vec, vec, vec],
        out_specs=(blk, blk),
        scratch_shapes=[pltpu.VMEM((H + tr, A), F32)],
        compiler_params=_params("parallel", "parallel"),
    )(h, h, h, h, dw_w, dw_b, bn_g, bn_b)


def _evenconv_bwd(du3, u1, h, dw_w, bn_g, bn_b, name):
    S = h.shape[0]
    KW, A = dw_w.shape
    H = HALO31
    tr = _pick(S, 256, H)
    nr = S // tr
    n = tr + H
    first_tap = H - (KW - 1)

    def body(d3_ref, d3n_ref, u1_ref, u1n_ref, ap_ref, a_ref, gp_ref, g_ref, w_ref, bg_ref, bb_ref,
             da_ref, dg_ref, dww_ref, dwb_ref, dbg_ref, dbb_ref, U0, DU):
        i = pl.program_id(1)
        u1 = jnp.concatenate([u1_ref[...], u1n_ref[...]], axis=0)
        d3 = jnp.concatenate([d3_ref[...], d3n_ref[...]], axis=0).astype(F32)
        rows = lax.broadcasted_iota(jnp.int32, (n, 1), 0)
        d3 = jnp.where((rows < tr) | (i < nr - 1), d3, 0.0)
        mu = jnp.mean(u1, axis=-1, keepdims=True)
        d = u1 - mu
        var = jnp.mean(d * d, axis=-1, keepdims=True)
        rstd = lax.rsqrt(var + LN_EPS)
        xh = d * rstd
        u2 = xh * bg_ref[...] + bb_ref[...]
        sg = _sigmoid(u2)
        du2 = d3 * (sg * (1.0 + u2 * (1.0 - sg)))
        dxh = du2 * bg_ref[...]
        m1 = jnp.mean(dxh, axis=-1, keepdims=True)
        m2 = jnp.mean(dxh * xh, axis=-1, keepdims=True)
        du1 = rstd * (dxh - m1 - xh * m2)
        DU[...] = du1
        pbg = jnp.sum(du2[0:tr] * xh[0:tr], axis=0, keepdims=True)
        pbb = jnp.sum(du2[0:tr], axis=0, keepdims=True)
        pwb = jnp.sum(du1[0:tr], axis=0, keepdims=True)

        glu_prev = ap_ref[...].astype(F32) * _sigmoid(gp_ref[...].astype(F32))
        U0[0:H, :] = jnp.where(i > 0, glu_prev, 0.0)
        a = a_ref[...].astype(F32)
        sgg = _sigmoid(g_ref[...].astype(F32))
        U0[H:H + tr, :] = a * sgg

        @pl.when(i == 0)
        def _():
            dbg_ref[...] = pbg
            dbb_ref[...] = pbb
            dwb_ref[...] = pwb
            dww_ref[...] = jnp.zeros_like(dww_ref)

        @pl.when(i > 0)
        def _():
            dbg_ref[...] += pbg
            dbb_ref[...] += pbb
            dwb_ref[...] += pwb

        du0 = w_ref[pl.ds(0, 1), :] * DU[pl.ds(KW - 1, tr), :]
        for k in range(1, KW):
            du0 = du0 + w_ref[pl.ds(k, 1), :] * DU[pl.ds(KW - 1 - k, tr), :]
        da_ref[...] = (du0 * sgg).astype(BF16)
        dg_ref[...] = (du0 * a * sgg * (1.0 - sgg)).astype(BF16)
        dcur = DU[pl.ds(0, tr), :]
        for k in range(KW):
            dww_ref[pl.ds(k, 1), :] += jnp.sum(dcur * U0[pl.ds(first_tap + k, tr), :], axis=0, keepdims=True)

    acol = lambda c: 3
    gcol = lambda c: 4
    one = lambda c: 1
    zero = lambda c: 0
    vec = pl.BlockSpec((1, A), lambda c, i: (0, 0))
    blk = pl.BlockSpec((tr, A), lambda c, i: (i, 0))
    return pl.pallas_call(
        body, name=name,
        out_shape=(jax.ShapeDtypeStruct((S, A), BF16), jax.ShapeDtypeStruct((S, A), BF16),
                   jax.ShapeDtypeStruct((KW, A), F32), jax.ShapeDtypeStruct((1, A), F32),
                   jax.ShapeDtypeStruct((1, A), F32), jax.ShapeDtypeStruct((1, A), F32)),
        grid=(1, nr),
        in_specs=[_cur_spec(tr, A, one), _next_spec(tr, H, A, one, S),
                  _cur_spec(tr, A, zero), _next_spec(tr, H, A, zero, S),
                  _prev_spec(tr, H, A, acol), _cur_spec(tr, A, acol),
                  _prev_spec(tr, H, A, gcol), _cur_spec(tr, A, gcol),
                  pl.BlockSpec((KW, A), lambda c, i: (0, 0)), vec, vec],
        out_specs=(blk, blk, pl.BlockSpec((KW, A), lambda c, i: (0, 0)), vec, vec, vec),
        scratch_shapes=[pltpu.VMEM((H + tr, A), F32), pltpu.VMEM((n, A), F32)],
        compiler_params=_params("arbitrary", "arbitrary"),
    )(du3, du3, u1, u1, h, h, h, h, dw_w, bn_g, bn_b)


def _pool_inv_count(row0, nrows, window):
    t = row0 + lax.broadcasted_iota(jnp.int32, (nrows, 1), 0)
    return 1.0 / jnp.minimum(t + 1, window).astype(F32)


def _odd_fwd(h, conv_w, pool_w, pool_scale, name):
    S = h.shape[0]
    C = conv_w.shape[1]
    G = len(POOL_WINDOWS)
    Dg = C // G
    H = HALO3
    tr = _pick(S, 256, H)

    def body(cb_ref, ccp_ref, cc_ref, chp_ref, ch_ref, pp_ref, p_ref, w_ref, pw_ref, sc_ref, mix_ref, M, P):
        i = pl.program_id(1)
        M[0:H, :] = jnp.where(i > 0, ccp_ref[...].astype(F32) * chp_ref[...].astype(F32), 0.0)
        M[H:H + tr, :] = cc_ref[...].astype(F32) * ch_ref[...].astype(F32)
        cm = (w_ref[pl.ds(0, 1), :] * M[pl.ds(H - 2, tr), :] + w_ref[pl.ds(1, 1), :] * M[pl.ds(H - 1, tr), :]
              + w_ref[pl.ds(2, 1), :] * M[pl.ds(H, tr), :])
        mix_ref[:, 0:C] = (cb_ref[...].astype(F32) * cm).astype(BF16)
        P[0:H, :] = jnp.where(i > 0, pp_ref[...].astype(F32), 0.0)
        P[H:H + tr, :] = p_ref[...].astype(F32)
        for gi, window in enumerate(POOL_WINDOWS):
            cols = pl.ds(gi * Dg, Dg)
            wsum = P[pl.ds(H, tr), cols]
            for dlt in range(1, window):
                wsum = wsum + P[pl.ds(H - dlt, tr), cols]
            diff = wsum * _pool_inv_count(i * tr, tr, window) - P[pl.ds(H, tr), cols]
            yd = jnp.dot(diff.astype(BF16), pw_ref[gi], preferred_element_type=F32) * sc_ref[:, cols]
            mix_ref[:, pl.ds(C + gi * Dg, Dg)] = yd.astype(BF16)

    col = lambda k: (lambda c: k)
    return pl.pallas_call(
        body, name=name, out_shape=jax.ShapeDtypeStruct((S, 2 * C), BF16),
        grid=(1, S // tr),
        in_specs=[_cur_spec(tr, C, col(0)),
                  _prev_spec(tr, H, C, col(1)), _cur_spec(tr, C, col(1)),
                  _prev_spec(tr, H, C, col(2)), _cur_spec(tr, C, col(2)),
                  _prev_spec(tr, H, C, col(3)), _cur_spec(tr, C, col(3)),
                  pl.BlockSpec((3, C), lambda c, i: (0, 0)),
                  pl.BlockSpec((G, Dg, Dg), lambda c, i: (0, 0, 0)),
                  pl.BlockSpec((1, C), lambda c, i: (0, 0))],
        out_specs=pl.BlockSpec((tr, 2 * C), lambda c, i: (i, 0)),
        scratch_shapes=[pltpu.VMEM((H + tr, C), F32), pltpu.VMEM((H + tr, C), F32)],
        compiler_params=_params("parallel", "parallel"),
    )(h, h, h, h, h, h, h, conv_w, pool_w, pool_scale)


def _odd_bwd(dmix, h, conv_w, pool_w, pool_scale, name):
    S = h.shape[0]
    C = conv_w.shape[1]
    G = len(POOL_WINDOWS)
    Dg = C // G
    H = HALO3
    tr = _pick(S, 256, H)
    nr = S // tr
    n = tr + H
    nt_dims = (((1,), (1,)), ((), ()))
    tn_dims = (((0,), (0,)), ((), ()))

    def body(dyc_ref, dycn_ref, dyd_ref, dydn_ref, cb_ref, cbn_ref, ccp_ref, cc_ref, ccn_ref,
             chp_ref, ch_ref, chn_ref, pp_ref, p_ref, w_ref, pw_ref, sc_ref,
             dh_ref, dw_ref, dpw_ref, dsc_ref, M, DCM, P, Q):
        i = pl.program_id(1)
        rows = lax.broadcasted_iota(jnp.int32, (n, 1), 0)
        valid = (rows < tr) | (i < nr - 1)

        @pl.when(i == 0)
        def _():
            dw_ref[...] = jnp.zeros_like(dw_ref)
            dpw_ref[...] = jnp.zeros_like(dpw_ref)
            dsc_ref[...] = jnp.zeros_like(dsc_ref)

        M[0:H, :] = jnp.where(i > 0, ccp_ref[...].astype(F32) * chp_ref[...].astype(F32), 0.0)
        cc = cc_ref[...].astype(F32)
        ch = ch_ref[...].astype(F32)
        M[H:H + tr, :] = cc * ch
        M[H + tr:H + tr + H, :] = ccn_ref[...].astype(F32) * chn_ref[...].astype(F32)
        w0, w1, w2 = w_ref[pl.ds(0, 1), :], w_ref[pl.ds(1, 1), :], w_ref[pl.ds(2, 1), :]
        cm = w0 * M[pl.ds(H - 2, tr), :] + w1 * M[pl.ds(H - 1, tr), :] + w2 * M[pl.ds(H, tr), :]
        dyc = jnp.concatenate([dyc_ref[...], dycn_ref[...]], axis=0).astype(F32)
        dyc = jnp.where(valid, dyc, 0.0)
        cbf = jnp.concatenate([cb_ref[...], cbn_ref[...]], axis=0).astype(F32)
        dh_ref[:, 0:C] = (dyc[0:tr] * cm).astype(BF16)
        DCM[...] = dyc * cbf
        dm = w2 * DCM[pl.ds(0, tr), :] + w1 * DCM[pl.ds(1, tr), :] + w0 * DCM[pl.ds(2, tr), :]
        dh_ref[:, C:2 * C] = (dm * ch).astype(BF16)
        dh_ref[:, 2 * C:3 * C] = (dm * cc).astype(BF16)
        dcur = DCM[pl.ds(0, tr), :]
        for k in range(3):
            dw_ref[pl.ds(k, 1), :] += jnp.sum(dcur * M[pl.ds(H - 2 + k, tr), :], axis=0, keepdims=True)

        P[0:H, :] = jnp.where(i > 0, pp_ref[...].astype(F32), 0.0)
        P[H:H + tr, :] = p_ref[...].astype(F32)
        dyd = jnp.concatenate([dyd_ref[...], dydn_ref[...]], axis=0).astype(F32)
        dyd = jnp.where(valid, dyd, 0.0)
        for gi, window in enumerate(POOL_WINDOWS):
            cols = pl.ds(gi * Dg, Dg)
            lo = gi * Dg
            wsum = P[pl.ds(H, tr), cols]
            for dlt in range(1, window):
                wsum = wsum + P[pl.ds(H - dlt, tr), cols]
            diff = (wsum * _pool_inv_count(i * tr, tr, window) - P[pl.ds(H, tr), cols]).astype(BF16)
            pw = pw_ref[gi]
            dyd_g = dyd[:, lo:lo + Dg]
            e = (dyd_g * sc_ref[:, cols]).astype(BF16)
            yraw = jnp.dot(diff, pw, preferred_element_type=F32)
            dsc_ref[:, cols] += jnp.sum(dyd_g[0:tr] * yraw, axis=0, keepdims=True)
            dpw_ref[gi] += lax.dot_general(diff, e[0:tr], tn_dims, preferred_element_type=F32)
            ddiff = lax.dot_general(e, pw, nt_dims, preferred_element_type=F32)
            Q[:, cols] = ddiff * _pool_inv_count(i * tr, n, window)
            acc = Q[pl.ds(0, tr), cols]
            for dlt in range(1, window):
                acc = acc + Q[pl.ds(dlt, tr), cols]
            dh_ref[:, pl.ds(3 * C + lo, Dg)] = (acc - ddiff[0:tr]).astype(BF16)

    col = lambda k: (lambda c: k)
    return pl.pallas_call(
        body, name=name,
        out_shape=(jax.ShapeDtypeStruct((S, 4 * C), BF16), jax.ShapeDtypeStruct((3, C), F32),
                   jax.ShapeDtypeStruct((G, Dg, Dg), F32), jax.ShapeDtypeStruct((1, C), F32)),
        grid=(1, nr),
        in_specs=[_cur_spec(tr, C, col(0)), _next_spec(tr, H, C, col(0), S),
                  _cur_spec(tr, C, col(1)), _next_spec(tr, H, C, col(1), S),
                  _cur_spec(tr, C, col(0)), _next_spec(tr, H, C, col(0), S),
                  _prev_spec(tr, H, C, col(1)), _cur_spec(tr, C, col(1)), _next_spec(tr, H, C, col(1), S),
                  _prev_spec(tr, H, C, col(2)), _cur_spec(tr, C, col(2)), _next_spec(tr, H, C, col(2), S),
                  _prev_spec(tr, H, C, col(3)), _cur_spec(tr, C, col(3)),
                  pl.BlockSpec((3, C), lambda c, i: (0, 0)),
                  pl.BlockSpec((G, Dg, Dg), lambda c, i: (0, 0, 0)),
                  pl.BlockSpec((1, C), lambda c, i: (0, 0))],
        out_specs=(pl.BlockSpec((tr, 4 * C), lambda c, i: (i, 0)),
                   pl.BlockSpec((3, C), lambda c, i: (0, 0)),
                   pl.BlockSpec((G, Dg, Dg), lambda c, i: (0, 0, 0)),
                   pl.BlockSpec((1, C), lambda c, i: (0, 0))),
        scratch_shapes=[pltpu.VMEM((H + tr + H, C), F32), pltpu.VMEM((n, C), F32),
                        pltpu.VMEM((H + tr, C), F32), pltpu.VMEM((n, C), F32)],
        compiler_params=_params("arbitrary", "arbitrary"),
    )(dmix, dmix, dmix, dmix, h, h, h, h, h, h, h, h, h, h, conv_w, pool_w, pool_scale)


def _adamw(w, grads, m, v, name):
    L, R, C = w.shape
    assert len(grads) == L
    tr, tc = _pick(R, 256, 8), _pick(C, 1408)
    ni, nj = R // tr, C // tc
    c1 = 1.0 / (1.0 - ADAM_B1 ** ADAM_STEP)
    c2 = 1.0 / (1.0 - ADAM_B2 ** ADAM_STEP)

    def g_spec(layer):
        def idx(l, i, j):
            before, after = l < layer, l > layer
            return (jnp.where(before, 0, jnp.where(after, ni - 1, i)),
                    jnp.where(before, 0, jnp.where(after, nj - 1, j)))
        return pl.BlockSpec((tr, tc), idx)

    def body(w_ref, *rest):
        g_refs = rest[:L]
        m_ref, v_ref, go_ref, d_ref, mo_ref, vo_ref = rest[L:]
        l = pl.program_id(0)
        gg = g_refs[0][...]
        for k in range(1, L):
            gg = jnp.where(l == k, g_refs[k][...], gg)
        mn = ADAM_B1 * m_ref[...] + (1.0 - ADAM_B1) * gg
        vn = ADAM_B2 * v_ref[...] + (1.0 - ADAM_B2) * (gg * gg)
        d_ref[...] = -ADAM_LR * ((mn * c1) / (jnp.sqrt(vn * c2) + ADAM_EPS) + ADAM_WD * w_ref[...])
        go_ref[...] = gg
        mo_ref[...] = mn
        vo_ref[...] = vn

    blk = pl.BlockSpec((None, tr, tc), lambda l, i, j: (l, i, j))
    sds = jax.ShapeDtypeStruct(w.shape, F32)
    return pl.pallas_call(
        body, name=name, out_shape=(sds, sds, sds, sds),
        grid=(L, R // tr, C // tc),
        in_specs=[blk] + [g_spec(k) for k in range(L)] + [blk, blk], out_specs=(blk, blk, blk, blk),
        compiler_params=_params("arbitrary", "arbitrary", "arbitrary"),
    )(w, *grads, m, v)


def _sum_slots(buf, name):
    N, R, C = buf.shape
    tr = _pick(R, 512, 8)

    def body(b_ref, o_ref):
        acc = b_ref[0]
        for k in range(1, N):
            acc = acc + b_ref[k]
        o_ref[...] = acc

    return pl.pallas_call(
        body, name=name, out_shape=jax.ShapeDtypeStruct((R, C), F32),
        grid=(R // tr,),
        in_specs=[pl.BlockSpec((N, tr, C), lambda i: (0, i, 0))],
        out_specs=pl.BlockSpec((tr, C), lambda i: (i, 0)),
        compiler_params=_params("parallel"),
    )(buf)


def _pair_sum(pos, g, rsib, name):
    _, hr, hc = rsib.shape
    tr, tc = _pick(hr, 512, 16), _pick(hc, 2816)

    def body(p_ref, g_ref, r_ref, o_ref):
        o_ref[...] = (g_ref[...].astype(F32) + r_ref[...].astype(F32)).astype(BF16)

    blk = pl.BlockSpec((None, tr, tc), lambda s, i, j, p: (s, i, j))
    return pl.pallas_call(
        body, name=name, out_shape=jax.ShapeDtypeStruct(rsib.shape, BF16),
        grid_spec=pltpu.PrefetchScalarGridSpec(
            num_scalar_prefetch=1, grid=(N_CHIPS, hr // tr, hc // tc),
            in_specs=[pl.BlockSpec((None, None, tr, tc), lambda s, i, j, p: (s, p[1], i, j)), blk],
            out_specs=blk),
        compiler_params=_params("parallel", "parallel", "parallel"),
    )(pos, g, rsib)


def _chip_sum(pos, part, land, name):
    _, sr, sc = land.shape
    tr, tc = _pick(sr, 256, 16), _pick(sc, 2816)

    def body(p_ref, own_ref, l_ref, o_ref):
        acc = own_ref[...].astype(F32)
        for k in range(3):
            acc = acc + l_ref[k].astype(F32)
        o_ref[...] = acc

    return pl.pallas_call(
        body, name=name, out_shape=jax.ShapeDtypeStruct((2, sr, sc), F32),
        grid_spec=pltpu.PrefetchScalarGridSpec(
            num_scalar_prefetch=1, grid=(sr // tr, sc // tc),
            in_specs=[pl.BlockSpec((None, tr, tc), lambda i, j, p: (p[0], i, j)),
                      pl.BlockSpec((3, tr, tc), lambda i, j, p: (0, i, j))],
            out_specs=pl.BlockSpec((None, tr, tc), lambda i, j, p: (p[1], i, j))),
        compiler_params=_params("parallel", "parallel"),
    )(pos, part, land)


def _place():
    x, y, c = lax.axis_index("x"), lax.axis_index("y"), lax.axis_index("c")
    return x, y, c


def _other_chips(x, y):
    return [(1 - x, y), (x, 1 - y), (1 - x, 1 - y)]


def _rcopy(src, dst, ssem, rsem, dev):
    return pltpu.make_async_remote_copy(src_ref=src, dst_ref=dst, send_sem=ssem, recv_sem=rsem,
                                        device_id=dev, device_id_type=MESH)


def _gather_items(bufs):
    return sum(b.shape[0] for b in bufs)


def _gather_walk(full):
    t = 0
    for ref in full:
        for l in range(ref.shape[0]):
            yield t, ref, l
            t += 1


def _gather_start(full, ssem, rsem):
    x, y, c = _place()
    j = 2 * x + y
    for t, ref, l in _gather_walk(full):
        own = ref.at[l, j, c]
        for r, (px, py) in enumerate(_other_chips(x, y)):
            _rcopy(own, own, ssem.at[6 * t + r], rsem.at[6 * t + r], (px, py, c)).start()


def _gather_forward(full, ssem, rsem):
    x, y, c = _place()
    for t, ref, l in _gather_walk(full):
        for r, (px, py) in enumerate(_other_chips(x, y)):
            slab = ref.at[l, 2 * px + py, c]
            _rcopy(slab, slab, ssem.at[6 * t + r], rsem.at[6 * t + r], (px, py, c)).wait_recv()
            _rcopy(slab, slab, ssem.at[6 * t + 3 + r], rsem.at[6 * t + 3 + r], (x, y, 1 - c)).start()


def _gather_finish(full, ssem, rsem):
    x, y, c = _place()
    j = 2 * x + y
    for t, ref, l in _gather_walk(full):
        for r, (px, py) in enumerate(_other_chips(x, y)):
            got = ref.at[l, 2 * px + py, 1 - c]
            _rcopy(got, got, ssem.at[6 * t + 3 + r], rsem.at[6 * t + 3 + r], (x, y, 1 - c)).wait_recv()
    for t, ref, l in _gather_walk(full):
        own = ref.at[l, j, c]
        for r, (px, py) in enumerate(_other_chips(x, y)):
            _rcopy(own, own, ssem.at[6 * t + r], rsem.at[6 * t + r], (px, py, c)).wait_send()
            slab = ref.at[l, 2 * px + py, c]
            _rcopy(slab, slab, ssem.at[6 * t + 3 + r], rsem.at[6 * t + 3 + r], (x, y, 1 - c)).wait_send()


def _land_shape(part):
    return jax.ShapeDtypeStruct((3,) + part.shape[1:], part.dtype)


def _exchange_start(parts, land, ssem, rsem):
    x, y, c = _place()
    for a in range(len(parts)):
        for r, (px, py) in enumerate(_other_chips(x, y)):
            _rcopy(parts[a].at[2 * px + py], land[a].at[r], ssem.at[3 * a + r], rsem.at[3 * a + r],
                   (px, py, c)).start()


def _exchange_finish(parts, land, ssem, rsem):
    x, y, c = _place()
    for a in range(len(parts)):
        for r, (px, py) in enumerate(_other_chips(x, y)):
            _rcopy(parts[a].at[2 * px + py], land[a].at[r], ssem.at[3 * a + r], rsem.at[3 * a + r],
                   (px, py, c)).wait()


_Hosted = collections.namedtuple("_Hosted", "ins outs alias sems start finish")


def _host_join(*hosts):
    ins, outs, alias, sems, spans = [], [], {}, [], []
    for h in hosts:
        spans.append((h, len(ins), len(outs), len(sems)))
        alias.update({len(ins) + a: len(outs) + b for a, b in h.alias.items()})
        ins, outs, sems = ins + list(h.ins), outs + list(h.outs), sems + list(h.sems)

    def each(step):
        def run(i, o, s):
            for h, a, b, c in spans:
                getattr(h, step)(i[a:a + len(h.ins)], o[b:b + len(h.outs)], s[c:c + len(h.sems)])
        return run

    return _Hosted(ins, outs, alias, sems, each("start"), each("finish"))


def _host_exchange(parts):
    n = len(parts)
    return _Hosted(list(parts), [_land_shape(p) for p in parts], {}, [3 * n, 3 * n],
                   lambda ins, outs, sems: _exchange_start(ins, outs, *sems),
                   lambda ins, outs, sems: _exchange_finish(ins, outs, *sems))


def _host_gather(bufs):
    T = _gather_items(bufs)

    def finish(ins, outs, sems):
        _gather_forward(outs, *sems)
        _gather_finish(outs, *sems)

    return _Hosted(list(bufs), [jax.ShapeDtypeStruct(b.shape, b.dtype) for b in bufs],
                   {a: a for a in range(len(bufs))}, [6 * T, 6 * T],
                   lambda ins, outs, sems: _gather_start(outs, *sems), finish)


def _host_all_devices(buf):
    return _Hosted([buf], [jax.ShapeDtypeStruct((N_DEV,) + buf.shape, buf.dtype)], {}, [N_DEV - 1, N_DEV - 1, 1],
                   lambda ins, outs, sems: _all_devices_start(ins[0], outs[0], *sems),
                   lambda ins, outs, sems: _all_devices_finish(ins[0], outs[0], *sems))


def _allgather_big(bufs, name):
    n = len(bufs)
    T = _gather_items(bufs)

    def body(*refs):
        full = refs[n:2 * n]
        ssem, rsem = refs[2 * n:]
        _gather_start(full, ssem, rsem)
        _gather_forward(full, ssem, rsem)
        _gather_finish(full, ssem, rsem)

    return pl.pallas_call(
        body, name=name, out_shape=tuple(jax.ShapeDtypeStruct(b.shape, BF16) for b in bufs),
        in_specs=[ANY] * n, out_specs=tuple([ANY] * n),
        input_output_aliases={a: a for a in range(n)},
        scratch_shapes=[pltpu.SemaphoreType.DMA((6 * T,)), pltpu.SemaphoreType.DMA((6 * T,))],
    )(*bufs)


def _allgather_small(shards, name):
    n = len(shards)
    outs = tuple(jax.ShapeDtypeStruct((N_CHIPS,) + s.shape, s.dtype) for s in shards)

    def body(*refs):
        ins, full = refs[:n], refs[n:2 * n]
        ssem, rsem, lsem = refs[2 * n:]
        x, y, c = _place()
        j = 2 * x + y
        chips = _other_chips(x, y)
        cps, locs = [], []
        for a in range(n):
            loc = pltpu.make_async_copy(ins[a], full[a].at[j], lsem.at[a])
            loc.start()
            locs.append(loc)
            for r, (px, py) in enumerate(chips):
                cp = _rcopy(ins[a], full[a].at[j], ssem.at[3 * a + r], rsem.at[3 * a + r], (px, py, c))
                cp.start()
                cps.append(cp)
        for a in range(n):
            for r, (px, py) in enumerate(chips):
                dst = full[a].at[2 * px + py]
                _rcopy(dst, dst, ssem.at[3 * a + r], rsem.at[3 * a + r], (px, py, c)).wait_recv()
        for cp in cps:
            cp.wait_send()
        for loc in locs:
            loc.wait()

    return pl.pallas_call(
        body, name=name, out_shape=outs,
        in_specs=[ANY] * n, out_specs=tuple([ANY] * n),
        scratch_shapes=[pltpu.SemaphoreType.DMA((3 * n,)), pltpu.SemaphoreType.DMA((3 * n,)),
                        pltpu.SemaphoreType.DMA((n,))],
    )(*shards)


def _pair_copies(ins, got, ssem, rsem):
    x, y, c = _place()
    return [_rcopy(ins[a].at[s, 1 - c], got[a].at[s], ssem.at[N_CHIPS * a + s], rsem.at[N_CHIPS * a + s],
                   (x, y, 1 - c))
            for a in range(len(ins)) for s in range(N_CHIPS)]


def _host_pair_exchange(grads):
    n = len(grads)

    def start(ins, outs, sems):
        for cp in _pair_copies(ins, outs, *sems):
            cp.start()

    def finish(ins, outs, sems):
        for cp in _pair_copies(ins, outs, *sems):
            cp.wait()

    return _Hosted(list(grads), [jax.ShapeDtypeStruct((N_CHIPS,) + g.shape[2:], BF16) for g in grads], {},
                   [N_CHIPS * n, N_CHIPS * n], start, finish)


def _pair_exchange(grads, name):
    n = len(grads)
    host = _host_pair_exchange(grads)

    def body(*refs):
        ins, got, sems = refs[:n], refs[n:2 * n], refs[2 * n:]
        host.start(ins, got, sems)
        host.finish(ins, got, sems)

    return pl.pallas_call(
        body, name=name, out_shape=tuple(host.outs),
        in_specs=[ANY] * n, out_specs=tuple([ANY] * n),
        scratch_shapes=[pltpu.SemaphoreType.DMA((k,)) for k in host.sems],
    )(*grads)


def _chip_exchange(parts, name):
    n = len(parts)

    def body(*refs):
        ins, land = refs[:n], refs[n:2 * n]
        ssem, rsem = refs[2 * n:]
        _exchange_start(ins, land, ssem, rsem)
        _exchange_finish(ins, land, ssem, rsem)

    return pl.pallas_call(
        body, name=name, out_shape=tuple(_land_shape(p) for p in parts),
        in_specs=[ANY] * n, out_specs=tuple([ANY] * n),
        scratch_shapes=[pltpu.SemaphoreType.DMA((3 * n,)), pltpu.SemaphoreType.DMA((3 * n,))],
    )(*parts)


def _half_swap(bufs, name):
    n = len(bufs)

    def body(*refs):
        full = refs[n:2 * n]
        ssem, rsem = refs[2 * n:]
        x, y, c = _place()
        cps = []
        for t in range(n):
            mine = full[t].at[c]
            cp = _rcopy(mine, mine, ssem.at[t], rsem.at[t], (x, y, 1 - c))
            cp.start()
            cps.append(cp)
        for t in range(n):
            got = full[t].at[1 - c]
            _rcopy(got, got, ssem.at[t], rsem.at[t], (x, y, 1 - c)).wait_recv()
        for cp in cps:
            cp.wait_send()

    return pl.pallas_call(
        body, name=name, out_shape=tuple(jax.ShapeDtypeStruct(b.shape, F32) for b in bufs),
        in_specs=[ANY] * n, out_specs=tuple([ANY] * n),
        input_output_aliases={a: a for a in range(n)},
        scratch_shapes=[pltpu.SemaphoreType.DMA((n,)), pltpu.SemaphoreType.DMA((n,))],
    )(*bufs)


def _flipped(x, y, c, m):
    fx, fy, fc = (m >> 2) & 1, (m >> 1) & 1, m & 1
    return x + fx - 2 * x * fx, y + fy - 2 * y * fy, c + fc - 2 * c * fc


def _all_devices_start(b_ref, o_ref, ssem, rsem, lsem):
    x, y, c = _place()
    me = 4 * x + 2 * y + c
    pltpu.make_async_copy(b_ref, o_ref.at[me], lsem.at[0]).start()
    for m in range(1, N_DEV):
        _rcopy(b_ref, o_ref.at[me], ssem.at[m - 1], rsem.at[m - 1], _flipped(x, y, c, m)).start()


def _all_devices_finish(b_ref, o_ref, ssem, rsem, lsem):
    x, y, c = _place()
    me = 4 * x + 2 * y + c
    for m in range(1, N_DEV):
        px, py, pc = _flipped(x, y, c, m)
        got = o_ref.at[4 * px + 2 * py + pc]
        _rcopy(got, got, ssem.at[m - 1], rsem.at[m - 1], (px, py, pc)).wait_recv()
    for m in range(1, N_DEV):
        _rcopy(b_ref, o_ref.at[me], ssem.at[m - 1], rsem.at[m - 1], _flipped(x, y, c, m)).wait_send()
    pltpu.make_async_copy(b_ref, o_ref.at[me], lsem.at[0]).wait()


def _pack(arrs):
    flat = jnp.concatenate([a.reshape(-1) for a in arrs])
    rows = -(-flat.shape[0] // (8 * LANES)) * 8
    flat = jnp.pad(flat, (0, rows * LANES - flat.shape[0]))
    return flat.reshape(rows, LANES)


def _unpack(buf, shapes):
    flat = buf.reshape(-1)
    out, off = [], 0
    for s in shapes:
        size = 1
        for d in s:
            size *= d
        out.append(flat[off:off + size].reshape(s))
        off += size
    return out


BIG = ("ev_w_in", "ev_w_out", "od_w_in", "od_w_out", "ffn_w_up", "ffn_w_down")
BIG_KIND = {"ev_w_in": "col", "ev_w_out": "row", "od_w_in": "col", "od_w_out": "row",
            "ffn_w_up": "col", "ffn_w_down": "row"}
SMALL_AXIS = {"ev_dw_w": 2, "ev_dw_b": None, "ev_bn_g": None, "ev_bn_b": None, "od_conv_w": 2,
              "od_pool_w": 2, "od_pool_scale": 1, "ffn_conv_w": 2, "ffn_conv_b": None, "ln_g": 2, "ln_b": 2}
WEIGHTS = ("ev_w_in", "ev_dw_w", "ev_dw_b", "ev_bn_g", "ev_bn_b", "ev_w_out", "od_w_in", "od_conv_w",
           "od_pool_w", "od_pool_scale", "od_w_out", "ffn_w_up", "ffn_conv_w", "ffn_conv_b", "ffn_w_down",
           "ln_g", "ln_b")


def _ffn_fwd(xb, w_up, w_down, conv_w, conv_b, tag, host_up=None, host_down=None):
    hu = _matmul(xb, w_up, mode="nn", b_lead=0, b_split=True, out_dtype=BF16, name=f"{tag}_up", tm=1024, tn=1408,
                 hosted=host_up)
    hu, up_outs = hu if host_up is not None else (hu, None)
    z = _ffn_act_fwd(hu, conv_w, conv_b, name=f"{tag}_act")
    y = _matmul(z, w_down, mode="nn", b_lead=0, out_dtype=F32, name=f"{tag}_down", tm=512, tn=1024,
                hosted=host_down)
    y, down_outs = y if host_down is not None else (y, None)
    return hu, z, y, up_outs, down_outs


def _ffn_bwd(drb, dr, alpha, xb, hu, z, w_up, w_down, conv_w, conv_b, tag, host_dwup=None, host_dx=None):
    g_down = _matmul(z, drb, mode="tn", out_dtype=BF16, name=f"{tag}_dwdown", tm=512, tn=1024)
    dz = _matmul(drb, w_down, mode="nt", b_lead=0, out_dtype=BF16, name=f"{tag}_dz", tm=1024, tn=1408)
    dg, du, dcw, dcb = _ffn_act_bwd(dz, hu, conv_w, conv_b, name=f"{tag}_actbwd")
    g_up = _matmul(xb, (dg, du), mode="tn", out_split=True, out_dtype=BF16, name=f"{tag}_dwup", tm=512, tn=1408,
                   hosted=host_dwup)
    g_up, dwup_outs = g_up if host_dwup is not None else (g_up, None)
    dx = _matmul((dg, du), w_up, mode="nt", b_lead=0, b_split=True, out_dtype=F32, add=dr, add_scale=alpha,
                 name=f"{tag}_dx", tm=512, tn=512, hosted=host_dx)
    dx, dx_outs = dx if host_dx is not None else (dx, None)
    return dx, g_up, g_down, dcw, dcb, dwup_outs, dx_outs


def kernel(x, ev_w_in, ev_dw_w, ev_dw_b, ev_bn_g, ev_bn_b, ev_w_out, od_w_in, od_conv_w, od_pool_w, od_pool_scale, od_w_out, ffn_w_up, ffn_conv_w, ffn_conv_b, ffn_w_down, ln_g, ln_b, loss_target, m_ev_w_in, m_ev_dw_w, m_ev_dw_b, m_ev_bn_g, m_ev_bn_b, m_ev_w_out, m_od_w_in, m_od_conv_w, m_od_pool_w, m_od_pool_scale, m_od_w_out, m_ffn_w_up, m_ffn_conv_w, m_ffn_conv_b, m_ffn_w_down, m_ln_g, m_ln_b, v_ev_w_in, v_ev_dw_w, v_ev_dw_b, v_ev_bn_g, v_ev_bn_b, v_ev_w_out, v_od_w_in, v_od_conv_w, v_od_pool_w, v_od_pool_scale, v_od_w_out, v_ffn_w_up, v_ffn_conv_w, v_ffn_conv_b, v_ffn_w_down, v_ln_g, v_ln_b):
    wts = dict(ev_w_in=ev_w_in, ev_dw_w=ev_dw_w, ev_dw_b=ev_dw_b, ev_bn_g=ev_bn_g, ev_bn_b=ev_bn_b,
               ev_w_out=ev_w_out, od_w_in=od_w_in, od_conv_w=od_conv_w, od_pool_w=od_pool_w,
               od_pool_scale=od_pool_scale, od_w_out=od_w_out, ffn_w_up=ffn_w_up, ffn_conv_w=ffn_conv_w,
               ffn_conv_b=ffn_conv_b, ffn_w_down=ffn_w_down, ln_g=ln_g, ln_b=ln_b)
    mom = dict(ev_w_in=m_ev_w_in, ev_dw_w=m_ev_dw_w, ev_dw_b=m_ev_dw_b, ev_bn_g=m_ev_bn_g, ev_bn_b=m_ev_bn_b,
               ev_w_out=m_ev_w_out, od_w_in=m_od_w_in, od_conv_w=m_od_conv_w, od_pool_w=m_od_pool_w,
               od_pool_scale=m_od_pool_scale, od_w_out=m_od_w_out, ffn_w_up=m_ffn_w_up, ffn_conv_w=m_ffn_conv_w,
               ffn_conv_b=m_ffn_conv_b, ffn_w_down=m_ffn_w_down, ln_g=m_ln_g, ln_b=m_ln_b)
    var = dict(ev_w_in=v_ev_w_in, ev_dw_w=v_ev_dw_w, ev_dw_b=v_ev_dw_b, ev_bn_g=v_ev_bn_g, ev_bn_b=v_ev_bn_b,
               ev_w_out=v_ev_w_out, od_w_in=v_od_w_in, od_conv_w=v_od_conv_w, od_pool_w=v_od_pool_w,
               od_pool_scale=v_od_pool_scale, od_w_out=v_od_w_out, ffn_w_up=v_ffn_w_up, ffn_conv_w=v_ffn_conv_w,
               ffn_conv_b=v_ffn_conv_b, ffn_w_down=v_ffn_w_down, ln_g=v_ln_g, ln_b=v_ln_b)

    S, D = x.shape[1], x.shape[2]
    depth = ln_g.shape[0]
    alpha = (2.0 * depth) ** 0.25
    A = ev_dw_b.shape[-1]
    n_heads = A // HEAD_DIM
    xi, yi, ci = _place()
    chip = 2 * xi + yi
    pos = jnp.stack([chip, ci]).astype(jnp.int32)

    bufs = {f"{k}{l}": _cast_into_gather(pos, wts[k], l, name=f"cast_{k}{l}")
            for k in BIG for l in range(wts[k].shape[0])}

    def whole(key):
        _, r, c = wts[key[:-1]].shape
        col = BIG_KIND[key[:-1]] == "col"
        return bufs[key].reshape(1, N_CHIPS, r, c) if col else bufs[key].reshape(1, N_CHIPS * r, c)

    full = {}

    def gathered_now(keys, arrays):
        bufs.update(zip(keys, arrays))
        full.update({key: whole(key) for key in keys})

    under_attn = ("ffn_w_up0", "ffn_w_down0", "od_w_in0", "od_w_out0")
    gathered_now(("ev_w_in0",), _allgather_big([bufs["ev_w_in0"]], name="gather_first"))
    small_sharded = [k for k in WEIGHTS if k not in BIG and SMALL_AXIS[k] is not None]
    gathered = _allgather_small([wts[k] for k in small_sharded], name="gather_small")
    sm = {k: wts[k] for k in WEIGHTS if k not in BIG and SMALL_AXIS[k] is None}
    for k, g4 in zip(small_sharded, gathered):
        sm[k] = jnp.concatenate([g4[t] for t in range(N_CHIPS)], axis=SMALL_AXIS[k])
    pool_w_bf = sm["od_pool_w"][0].astype(BF16)

    x0 = x[0]
    x0b = _cast_bf16(x, name="cast_x")[0]
    h0, got = _matmul(x0b, full["ev_w_in0"], mode="nn", b_lead=0, b_split=True, out_dtype=BF16, name="ev_in",
                      tm=1024, tn=1280, hosted=_host_gather([bufs["ev_w_out0"]]))
    gathered_now(("ev_w_out0",), got)
    o_a, tot, *rest = _attn_fwd(h0, n_heads, name="attn_fwd", gather=[bufs[k] for k in under_attn])
    gathered_now(under_attn, rest)
    u1, u3 = _evenconv_fwd(h0, sm["ev_dw_w"][0], sm["ev_dw_b"], sm["ev_bn_g"], sm["ev_bn_b"], name="evconv_fwd")
    mix0 = jnp.concatenate([o_a, u3], axis=1)
    y1 = _matmul(mix0, full["ev_w_out0"], mode="nn", b_lead=0, out_dtype=F32, name="ev_out", tm=1024, tn=1024)
    x1, x1b, xh1, rs1 = _ln_fwd(x0, y1, sm["ln_g"][0, 0][None], sm["ln_b"][0, 0][None], alpha, name="ln00")
    hu0, z0, y2, got_up, got_down = _ffn_fwd(
        x1b, full["ffn_w_up0"], full["ffn_w_down0"], sm["ffn_conv_w"][0], sm["ffn_conv_b"][0][None], "ffn0",
        host_up=_host_gather([bufs["ffn_w_up1"]]), host_down=_host_gather([bufs["ffn_w_down1"]]))
    gathered_now(("ffn_w_up1",), got_up)
    gathered_now(("ffn_w_down1",), got_down)
    x2, x2b, xh2, rs2 = _ln_fwd(x1, y2, sm["ln_g"][0, 1][None], sm["ln_b"][0, 1][None], alpha, name="ln01")
    h1 = _matmul(x2b, full["od_w_in0"], mode="nn", b_lead=0, b_split=True, out_dtype=BF16, name="od_in",
                 tm=1024, tn=1024)
    mix1 = _odd_fwd(h1, sm["od_conv_w"][0], pool_w_bf, sm["od_pool_scale"], name="odd_fwd")
    y3 = _matmul(mix1, full["od_w_out0"], mode="nn", b_lead=0, out_dtype=F32, name="od_out", tm=1024, tn=1024)
    x3, x3b, xh3, rs3 = _ln_fwd(x2, y3, sm["ln_g"][1, 0][None], sm["ln_b"][1, 0][None], alpha, name="ln10")
    hu1, z1, y4, _, _ = _ffn_fwd(x3b, full["ffn_w_up1"], full["ffn_w_down1"], sm["ffn_conv_w"][1],
                                 sm["ffn_conv_b"][1][None], "ffn1")
    x4, _, xh4, rs4 = _ln_fwd(x3, y4, sm["ln_g"][1, 1][None], sm["ln_b"][1, 1][None], alpha, name="ln11")

    dx4, loss_part = _loss_grad(x4, loss_target[0], name="loss")
    loss = lax.psum(loss_part[0, 0], ("x", "y", "c"))

    def halves_view(named):
        g4 = []
        for k, g in named:
            rows, cols = (g.shape[1], g.shape[2]) if BIG_KIND[k] == "col" else (g.shape[0] // N_CHIPS, g.shape[1])
            g4.append(g.reshape(N_CHIPS, 2, rows // 2, cols))
        return g4

    def pair_sums(g4, sib, tag):
        return [_pair_sum(pos, g, r, name=f"grad_pair_sum_{tag}{t}") for t, (g, r) in enumerate(zip(g4, sib))]

    def pair_reduce(named, tag):
        g4 = halves_view(named)
        return pair_sums(g4, _pair_exchange(g4, name=f"grad_pair_exchange_{tag}"), tag)

    dr4, dr4b, dg11, db11 = _ln_bwd(dx4, xh4, rs4, sm["ln_g"][1, 1][None], name="ln11_bwd")
    dx3, g_up1, g_down1, dcw1, dcb1, _, _ = _ffn_bwd(dr4b, dr4, alpha, x3b, hu1, z1, full["ffn_w_up1"],
                                                     full["ffn_w_down1"], sm["ffn_conv_w"][1],
                                                     sm["ffn_conv_b"][1][None], "ffn1")
    dr3, dr3b, dg10, db10 = _ln_bwd(dx3, xh3, rs3, sm["ln_g"][1, 0][None], name="ln10_bwd")
    g4_f1 = halves_view([("ffn_w_up", g_up1), ("ffn_w_down", g_down1)])
    g_odout, sib_f1 = _matmul(mix1, dr3b, mode="tn", out_dtype=BF16, name="od_dwout", tm=512, tn=1024,
                              hosted=_host_pair_exchange(g4_f1))
    parts_f1 = pair_sums(g4_f1, sib_f1, "f1")
    dmix1 = _matmul(dr3b, full["od_w_out0"], mode="nt", b_lead=0, out_dtype=BF16, name="od_dmix", tm=1024, tn=1024)
    dh1, d_odconv, d_pool, d_pscale = _odd_bwd(dmix1, h1, sm["od_conv_w"][0], pool_w_bf, sm["od_pool_scale"],
                                               name="odd_bwd")
    g_odin = _matmul(x2b, dh1, mode="tn", out_split=True, out_dtype=BF16, name="od_dwin", tm=512, tn=1024)
    dx2 = _matmul(dh1, full["od_w_in0"], mode="nt", b_lead=0, b_split=True, out_dtype=F32, add=dr3, add_scale=alpha,
                  name="od_dx", tm=1024, tn=512)
    dr2, dr2b, dg01, db01 = _ln_bwd(dx2, xh2, rs2, sm["ln_g"][0, 1][None], name="ln01_bwd")
    dx1, g_up0, g_down0, dcw0, dcb0, land_up1, land_down1 = _ffn_bwd(
        dr2b, dr2, alpha, x1b, hu0, z0, full["ffn_w_up0"], full["ffn_w_down0"], sm["ffn_conv_w"][0],
        sm["ffn_conv_b"][0][None], "ffn0",
        host_dwup=_host_exchange(parts_f1[:1]), host_dx=_host_exchange(parts_f1[1:]))
    dr1, dr1b, dg00, db00 = _ln_bwd(dx1, xh1, rs1, sm["ln_g"][0, 0][None], name="ln00_bwd")
    g4_b = halves_view([("od_w_in", g_odin), ("od_w_out", g_odout), ("ffn_w_up", g_up0), ("ffn_w_down", g_down0)])
    g_evout, sib_b = _matmul(mix0, dr1b, mode="tn", out_dtype=BF16, name="ev_dwout", tm=512, tn=1024,
                             hosted=_host_pair_exchange(g4_b))
    g4_o = halves_view([("ev_w_out", g_evout)])
    dmix0, sib_o = _matmul(dr1b, full["ev_w_out0"], mode="nt", b_lead=0, out_dtype=BF16, name="ev_dmix",
                           tm=1024, tn=1024, hosted=_host_pair_exchange(g4_o))
    da, dgate, d_dww, d_dwb, d_bng, d_bnb = _evenconv_bwd(dmix0, u1, h0, sm["ev_dw_w"][0], sm["ev_bn_g"],
                                                          sm["ev_bn_b"], name="evconv_bwd")
    parts_b = pair_sums(g4_b + g4_o, list(sib_b) + list(sib_o), "b")

    d_ln_g = jnp.stack([jnp.stack([dg00[0], dg01[0]]), jnp.stack([dg10[0], dg11[0]])])
    d_ln_b = jnp.stack([jnp.stack([db00[0], db01[0]]), jnp.stack([db10[0], db11[0]])])
    small_partial = {
        "ev_dw_w": d_dww[None], "ev_dw_b": d_dwb, "ev_bn_g": d_bng, "ev_bn_b": d_bnb,
        "od_conv_w": d_odconv[None], "od_pool_w": d_pool[None], "od_pool_scale": d_pscale,
        "ffn_conv_w": jnp.stack([dcw0, dcw1]), "ffn_conv_b": jnp.concatenate([dcb0, dcb1], axis=0),
        "ln_g": d_ln_g, "ln_b": d_ln_b}
    small_names = [k for k in WEIGHTS if k not in BIG]
    packed = _pack([small_partial[k] for k in small_names])
    dq, dk, dv, under_bwd = _attn_bwd(h0, dmix0, tot, n_heads, name="attn_bwd",
                                      hosted=_host_join(_host_exchange(parts_b), _host_all_devices(packed)))
    land_b, all_small = under_bwd[:-1], under_bwd[-1]
    dh0 = jnp.concatenate([dq, dk, dv, da, dgate], axis=1)
    g_evin = _matmul(x0b, dh0, mode="tn", out_split=True, out_dtype=BF16, name="ev_dwin", tm=512, tn=1280)
    parts_e = pair_reduce([("ev_w_in", g_evin)], "e")
    grad_x, land_e = _matmul(dh0, full["ev_w_in0"], mode="nt", b_lead=0, b_split=True, out_dtype=F32, add=dr1,
                             add_scale=alpha, name="ev_dx", tm=1024, tn=512, hosted=_host_exchange(parts_e))

    order = ["ffn_w_up1", "ffn_w_down1", "od_w_in0", "od_w_out0", "ffn_w_up0", "ffn_w_down0", "ev_w_out0", "ev_w_in0"]
    parts = parts_f1 + parts_b + parts_e
    land = list(land_up1) + list(land_down1) + list(land_b) + list(land_e)
    halves = [_chip_sum(pos, p, ld, name=f"grad_chip_sum_{tag}") for tag, p, ld in zip(order, parts, land)]
    reduced = dict(zip(order, _half_swap(halves, name="grad_half_swap")))
    big_grads = {k: [reduced[f"{k}{l}"].reshape(wts[k].shape[1:]) for l in range(wts[k].shape[0])] for k in BIG}

    summed = _sum_slots(all_small, name="sum_small_grads")
    small_full = dict(zip(small_names, _unpack(summed, [small_partial[k].shape for k in small_names])))
    small_grads = {}
    for k in small_names:
        ax = SMALL_AXIS[k]
        if ax is None:
            small_grads[k] = small_full[k]
        else:
            size = wts[k].shape[ax]
            small_grads[k] = lax.dynamic_slice_in_dim(small_full[k], chip * size, size, axis=ax)

    grads, delta, new_m, new_v = {}, {}, {}, {}
    for k in BIG:
        grads[k], delta[k], new_m[k], new_v[k] = _adamw(wts[k], big_grads[k], mom[k], var[k], name=f"adamw_{k}")
    shapes = [wts[k].shape for k in small_names]
    pw, pg, pm, pv = (_pack([d[k] for k in small_names]) for d in (wts, small_grads, mom, var))
    sg, sd, smn, svn = _adamw(pw[None], [pg], pm[None], pv[None], name="adamw_small")
    for dst, buf in ((grads, sg), (delta, sd), (new_m, smn), (new_v, svn)):
        for k, a in zip(small_names, _unpack(buf[0], shapes)):
            dst[k] = a

    return (loss, grad_x[None], *[grads[k] for k in WEIGHTS], *[delta[k] for k in WEIGHTS],
            *[new_m[k] for k in WEIGHTS], *[new_v[k] for k in WEIGHTS])
```

```python
import collections

import jax
import jax.numpy as jnp
from jax import lax
from jax.experimental import pallas as pl
from jax.experimental.pallas import tpu as pltpu

F32 = jnp.float32
BF16 = jnp.bfloat16

HEAD_DIM = 128
POOL_WINDOWS = (2, 4, 8, 16)
LN_EPS = 1e-5
ADAM_LR = 0.001
ADAM_B1 = 0.9
ADAM_B2 = 0.999
ADAM_EPS = 1e-08
ADAM_WD = 0.01
ADAM_STEP = 10
N_CHIPS = 4
N_DEV = 8
MESH = pl.DeviceIdType.MESH
LANES = 128
HALO3 = 16
HALO31 = 32
ROW_CHUNK = 32

ANY = pl.BlockSpec(memory_space=pl.ANY)


def _pick(n, pref, mult=LANES):
    if n <= pref:
        return n
    t = (pref // mult) * mult
    while t >= mult:
        if n % t == 0:
            return t
        t -= mult
    return n


def _params(*sem):
    return pltpu.CompilerParams(dimension_semantics=sem)


def _matmul(a, b, *, mode, out_dtype, name, b_lead=None, b_split=False, out_split=False, add=None,
            add_scale=1.0, tm=512, tn=1024, tk=None, hosted=None):
    halves = isinstance(a, tuple) or isinstance(b, tuple)
    if isinstance(a, tuple):
        assert mode == "nt" and b_split and tk is None
        ash = (a[0].shape[0], 2 * a[0].shape[1])
    else:
        ash = a.shape[-2:]
    if isinstance(b, tuple):
        assert mode == "tn" and tk is None
        bsh = (b[0].shape[0], 2 * b[0].shape[1])
    else:
        bsh = b.shape[-2:]
    if mode == "nn":
        (M, K), (K2, N) = ash, bsh
        if b_split:
            N = N * N_CHIPS
    elif mode == "nt":
        (M, K), (N, K2) = ash, bsh
        if b_split:
            K2 = K2 * N_CHIPS
    else:
        (K, M), (K2, N) = ash, bsh
    assert K == K2, (ash, bsh, mode)
    tm = _pick(M, tm)
    tn = _pick(N // N_CHIPS if (out_split or (b_split and mode == "nn")) else N, tn)
    whole_split_k = b_split and mode == "nt" and tk is None
    if tk is None:
        tk = K
    else:
        tk = _pick(K // N_CHIPS if (b_split and mode == "nt") else K, tk)
    nk = K // tk
    kq = K // N_CHIPS
    n_per = (N // N_CHIPS) // tn
    k_per = (K // N_CHIPS) // tk

    def lead(shape, idx):
        if b_lead is None:
            return pl.BlockSpec(shape, idx)
        return pl.BlockSpec((None,) + shape, lambda i, j, k: (b_lead,) + idx(i, j, k))

    if mode == "nn":
        a_spec = pl.BlockSpec((tm, tk), lambda i, j, k: (i, k))
        if b_split:
            b_spec = lead((None, tk, tn), lambda i, j, k: (lax.div(j, n_per), k, lax.rem(j, n_per)))
        else:
            b_spec = lead((tk, tn), lambda i, j, k: (k, j))
        dims = (((1,), (0,)), ((), ()))
    elif mode == "nt":
        a_spec = pl.BlockSpec((tm, tk), lambda i, j, k: (i, k))
        if whole_split_k:
            b_spec = lead((N_CHIPS, tn, kq), lambda i, j, k: (0, j, 0))
        elif b_split:
            b_spec = lead((None, tn, tk), lambda i, j, k: (lax.div(k, k_per), j, lax.rem(k, k_per)))
        else:
            b_spec = lead((tn, tk), lambda i, j, k: (j, k))
        dims = (((1,), (1,)), ((), ()))
    else:
        a_spec = pl.BlockSpec((tk, tm), lambda i, j, k: (k, i))
        b_spec = pl.BlockSpec((tk, tn), lambda i, j, k: (k, j))
        dims = (((0,), (0,)), ((), ()))
    if out_split:
        out_shape = jax.ShapeDtypeStruct((N_CHIPS, M, N // N_CHIPS), out_dtype)
        out_spec = pl.BlockSpec((None, tm, tn), lambda i, j, k: (lax.div(j, n_per), i, lax.rem(j, n_per)))
    else:
        out_shape = jax.ShapeDtypeStruct((M, N), out_dtype)
        out_spec = pl.BlockSpec((tm, tn), lambda i, j, k: (i, j))
    grid = (M // tm, N // tn, nk)
    nj_half = grid[1] // 2
    if isinstance(a, tuple):
        in_specs = [pl.BlockSpec((tm, K // 2), lambda i, j, k: (i, 0))] * 2 + [b_spec]
        args = [a[0], a[1], b]
    elif isinstance(b, tuple):
        in_specs = [a_spec,
                    pl.BlockSpec((tk, tn), lambda i, j, k: (k, jnp.minimum(j, nj_half - 1))),
                    pl.BlockSpec((tk, tn), lambda i, j, k: (k, jnp.maximum(j - nj_half, 0)))]
        args = [a, b[0], b[1]]
    else:
        in_specs = [a_spec, b_spec]
        args = [a, b]
    n_op = len(args)
    if add is not None:
        in_specs.append(pl.BlockSpec((tm, tn), lambda i, j, k: (i, j)))
        args.append(add)

    n_in = len(args)
    h_in = 0 if hosted is None else len(hosted.ins)
    h_out = 0 if hosted is None else len(hosted.outs)

    def body(*refs):
        ops = refs[:n_op]
        add_ref = refs[n_op] if add is not None else None
        h_ins = refs[n_in:n_in + h_in]
        o_ref = refs[n_in + h_in]
        h_outs = refs[n_in + h_in + 1:n_in + h_in + 1 + h_out]
        scr = refs[n_in + h_in + 1 + h_out:]
        i, j, k = pl.program_id(0), pl.program_id(1), pl.program_id(2)
        if hosted is not None:
            sems = scr[len(scr) - len(hosted.sems):]

            @pl.when((i == 0) & (j == 0) & (k == 0))
            def _():
                hosted.start(h_ins, h_outs, sems)

        def finish(res):
            if add_ref is not None:
                res = res + add_scale * add_ref[...]
            o_ref[...] = res.astype(out_dtype)

        def dot(x, y):
            return lax.dot_general(x, y, dims, preferred_element_type=F32)

        if isinstance(b, tuple):
            @pl.when(j < nj_half)
            def _():
                finish(dot(ops[0][...], ops[1][...]))

            @pl.when(j >= nj_half)
            def _():
                finish(dot(ops[0][...], ops[2][...]))
            part = None
        elif whole_split_k:
            srcs = [(ops[0], s) for s in range(N_CHIPS)] if not isinstance(a, tuple) else \
                   [(ops[s // 2], s % 2) for s in range(N_CHIPS)]
            b_ref = ops[-1]
            part = None
            for s, (src, off) in enumerate(srcs):
                term = dot(src[:, off * kq:(off + 1) * kq], b_ref[s])
                part = term if part is None else part + term
        else:
            part = dot(ops[0][...], ops[1][...])

        if part is None:
            pass
        elif nk == 1:
            finish(part)
        else:
            acc = scr[0]

            @pl.when(k == 0)
            def _():
                acc[...] = part

            @pl.when(k > 0)
            def _():
                acc[...] += part

            @pl.when(k == nk - 1)
            def _():
                finish(acc[...])

        if hosted is not None:
            @pl.when((i == grid[0] - 1) & (j == grid[1] - 1) & (k == nk - 1))
            def _():
                hosted.finish(h_ins, h_outs, sems)

    scratch = [pltpu.VMEM((tm, tn), F32)] if nk > 1 else []
    if hosted is not None:
        res = pl.pallas_call(
            body, name=name,
            out_shape=(out_shape,) + tuple(hosted.outs),
            grid=grid,
            in_specs=in_specs + [ANY] * h_in,
            out_specs=(out_spec,) + (ANY,) * h_out,
            input_output_aliases={n_in + src: 1 + dst for src, dst in hosted.alias.items()},
            scratch_shapes=scratch + [pltpu.SemaphoreType.DMA((n,)) for n in hosted.sems],
            compiler_params=_params("arbitrary", "arbitrary", "arbitrary"),
        )(*args, *hosted.ins)
        return res[0], list(res[1:])
    return pl.pallas_call(
        body, name=name,
        out_shape=out_shape,
        grid=grid,
        in_specs=in_specs,
        out_specs=out_spec,
        scratch_shapes=scratch,
        compiler_params=_params("parallel", "parallel", "arbitrary"),
    )(*args)


def _cast_bf16(w, name):
    L, R, C = w.shape
    tr, tc = _pick(R, 512, 16), _pick(C, 1408)

    def body(w_ref, o_ref):
        o_ref[...] = w_ref[...].astype(BF16)

    return pl.pallas_call(
        body, name=name, out_shape=jax.ShapeDtypeStruct(w.shape, BF16),
        grid=(L, R // tr, C // tc),
        in_specs=[pl.BlockSpec((None, tr, tc), lambda l, i, j: (l, i, j))],
        out_specs=pl.BlockSpec((None, tr, tc), lambda l, i, j: (l, i, j)),
        compiler_params=_params("parallel", "parallel", "parallel"),
    )(w)


def _cast_into_gather(pos, w, layer, name):
    L, R, C = w.shape
    r2 = R // 2
    tr, tc = _pick(r2, 512, 16), _pick(C, 1408)

    def body(p_ref, w_ref, o_ref):
        o_ref[...] = w_ref[...].astype(BF16)

    return pl.pallas_call(
        body, name=name, out_shape=jax.ShapeDtypeStruct((1, N_CHIPS, 2, r2, C), BF16),
        grid_spec=pltpu.PrefetchScalarGridSpec(
            num_scalar_prefetch=1, grid=(2, r2 // tr, C // tc),
            in_specs=[pl.BlockSpec((None, None, tr, tc), lambda h, i, j, p: (layer, h, i, j))],
            out_specs=pl.BlockSpec((None, None, None, tr, tc), lambda h, i, j, p: (0, p[0], h, i, j))),
        compiler_params=_params("parallel", "parallel", "parallel"),
    )(pos, w.reshape(L, 2, r2, C))


def _sigmoid(v):
    return 0.5 * jnp.tanh(0.5 * v) + 0.5


def _ln_fwd(x, y, g, b, alpha, name):
    S, D = x.shape
    tr = _pick(S, 256, 8)

    def body(x_ref, y_ref, g_ref, b_ref, o_ref, ob_ref, xh_ref, rs_ref):
        r = alpha * x_ref[...] + y_ref[...]
        mu = jnp.mean(r, axis=-1, keepdims=True)
        d = r - mu
        var = jnp.mean(d * d, axis=-1, keepdims=True)
        rstd = lax.rsqrt(var + LN_EPS)
        xh = d * rstd
        o = xh * g_ref[...] + b_ref[...]
        o_ref[...] = o
        ob_ref[...] = o.astype(BF16)
        xh_ref[...] = xh
        rs_ref[...] = rstd

    row = pl.BlockSpec((tr, D), lambda i: (i, 0))
    vec = pl.BlockSpec((1, D), lambda i: (0, 0))
    return pl.pallas_call(
        body, name=name,
        out_shape=(jax.ShapeDtypeStruct((S, D), F32), jax.ShapeDtypeStruct((S, D), BF16),
                   jax.ShapeDtypeStruct((S, D), F32), jax.ShapeDtypeStruct((S, 1), F32)),
        grid=(S // tr,),
        in_specs=[row, row, vec, vec],
        out_specs=(row, row, row, pl.BlockSpec((tr, 1), lambda i: (i, 0))),
        compiler_params=_params("parallel"),
    )(x, y, g, b)


def _ln_bwd(dout, xhat, rstd, g, name):
    S, D = dout.shape
    tr = _pick(S, 256, 8)

    def body(do_ref, xh_ref, rs_ref, g_ref, dr_ref, drb_ref, dg_ref, db_ref):
        i = pl.program_id(0)
        do = do_ref[...]
        xh = xh_ref[...]
        dxh = do * g_ref[...]
        m1 = jnp.mean(dxh, axis=-1, keepdims=True)
        m2 = jnp.mean(dxh * xh, axis=-1, keepdims=True)
        dr = rs_ref[...] * (dxh - m1 - xh * m2)
        dr_ref[...] = dr
        drb_ref[...] = dr.astype(BF16)
        pg = jnp.sum(do * xh, axis=0, keepdims=True)
        pb = jnp.sum(do, axis=0, keepdims=True)

        @pl.when(i == 0)
        def _():
            dg_ref[...] = pg
            db_ref[...] = pb

        @pl.when(i > 0)
        def _():
            dg_ref[...] += pg
            db_ref[...] += pb

    row = pl.BlockSpec((tr, D), lambda i: (i, 0))
    vec = pl.BlockSpec((1, D), lambda i: (0, 0))
    return pl.pallas_call(
        body, name=name,
        out_shape=(jax.ShapeDtypeStruct((S, D), F32), jax.ShapeDtypeStruct((S, D), BF16),
                   jax.ShapeDtypeStruct((1, D), F32), jax.ShapeDtypeStruct((1, D), F32)),
        grid=(S // tr,),
        in_specs=[row, row, pl.BlockSpec((tr, 1), lambda i: (i, 0)), vec],
        out_specs=(row, row, vec, vec),
        compiler_params=_params("arbitrary"),
    )(dout, xhat, rstd, g)


def _ln_loss_bwd(x, y, g, b, target, alpha, name):
    S, D = x.shape
    tr = _pick(S, 256, 8)
    n = S // tr

    def body(x_ref, y_ref, g_ref, b_ref, t_ref, dr_ref, drb_ref, dg_ref, db_ref, l_ref, acc):
        i = pl.program_id(0)
        r = alpha * x_ref[...] + y_ref[...]
        mu = jnp.mean(r, axis=-1, keepdims=True)
        d = r - mu
        var = jnp.mean(d * d, axis=-1, keepdims=True)
        rstd = lax.rsqrt(var + LN_EPS)
        xh = d * rstd
        err = xh * g_ref[...] + b_ref[...] - t_ref[...]
        do = err * (1.0 / D)
        dxh = do * g_ref[...]
        m1 = jnp.mean(dxh, axis=-1, keepdims=True)
        m2 = jnp.mean(dxh * xh, axis=-1, keepdims=True)
        dr = rstd * (dxh - m1 - xh * m2)
        dr_ref[...] = dr
        drb_ref[...] = dr.astype(BF16)
        pg = jnp.sum(do * xh, axis=0, keepdims=True)
        pb = jnp.sum(do, axis=0, keepdims=True)
        pe = jnp.sum(err * err, axis=0, keepdims=True)

        @pl.when(i == 0)
        def _():
            dg_ref[...] = pg
            db_ref[...] = pb
            acc[...] = pe

        @pl.when(i > 0)
        def _():
            dg_ref[...] += pg
            db_ref[...] += pb
            acc[...] += pe

        @pl.when(i == n - 1)
        def _():
            l_ref[...] = (0.5 / D) * jnp.sum(acc[...], axis=1, keepdims=True)

    row = pl.BlockSpec((tr, D), lambda i: (i, 0))
    vec = pl.BlockSpec((1, D), lambda i: (0, 0))
    return pl.pallas_call(
        body, name=name,
        out_shape=(jax.ShapeDtypeStruct((S, D), F32), jax.ShapeDtypeStruct((S, D), BF16),
                   jax.ShapeDtypeStruct((1, D), F32), jax.ShapeDtypeStruct((1, D), F32),
                   jax.ShapeDtypeStruct((1, 1), F32)),
        grid=(n,),
        in_specs=[row, row, vec, vec, row],
        out_specs=(row, row, vec, vec, pl.BlockSpec((1, 1), lambda i: (0, 0))),
        scratch_shapes=[pltpu.VMEM((1, D), F32)],
        compiler_params=_params("arbitrary"),
    )(x, y, g, b, target)


def _prev_spec(tr, halo, width, col):
    return pl.BlockSpec((halo, width), lambda c, i: (jnp.maximum(i * (tr // halo) - 1, 0), col(c)))


def _next_spec(tr, halo, width, col, nrows):
    last = nrows // halo - 1
    return pl.BlockSpec((halo, width), lambda c, i: (jnp.minimum((i + 1) * (tr // halo), last), col(c)))


def _cur_spec(tr, width, col):
    return pl.BlockSpec((tr, width), lambda c, i: (i, col(c)))


def _ffn_act_fwd(hu, conv_w, conv_b, name):
    S, F2 = hu.shape
    F = F2 // 2
    tr, tc, H = _pick(S, 512, 16), _pick(F, 512), HALO3
    nc, nr = F // tc, S // tr
    rc = min(ROW_CHUNK, tr)

    def body(gp_ref, g_ref, u_ref, w_ref, b_ref, z_ref, G):
        i = pl.program_id(1)
        G[0:H, :] = jnp.where(i > 0, gp_ref[...].astype(F32), 0.0)
        G[H:H + tr, :] = g_ref[...].astype(F32)
        w0, w1, w2, b = w_ref[pl.ds(0, 1), :], w_ref[pl.ds(1, 1), :], w_ref[pl.ds(2, 1), :], b_ref[...]
        for r0 in range(0, tr, rc):
            gc = b + w0 * G[pl.ds(H - 2 + r0, rc), :] + w1 * G[pl.ds(H - 1 + r0, rc), :] + w2 * G[pl.ds(H + r0, rc), :]
            z = gc * _sigmoid(gc) * u_ref[pl.ds(r0, rc), :].astype(F32)
            z_ref[pl.ds(r0, rc), :] = z.astype(BF16)

    gcol = lambda c: c
    ucol = lambda c: c + nc
    return pl.pallas_call(
        body, name=name, out_shape=jax.ShapeDtypeStruct((S, F), BF16),
        grid=(nc, nr),
        in_specs=[_prev_spec(tr, H, tc, gcol), _cur_spec(tr, tc, gcol), _cur_spec(tr, tc, ucol),
                  pl.BlockSpec((3, tc), lambda c, i: (0, c)), pl.BlockSpec((1, tc), lambda c, i: (0, c))],
        out_specs=pl.BlockSpec((tr, tc), lambda c, i: (i, c)),
        scratch_shapes=[pltpu.VMEM((H + tr, tc), F32)],
        compiler_params=_params("parallel", "parallel"),
    )(hu, hu, hu, conv_w, conv_b)


def _ffn_act_bwd(dz, hu, conv_w, conv_b, name):
    S, F = dz.shape
    tr, tc, H = _pick(S, 512, 16), _pick(F, 512), HALO3
    nc, nr = F // tc, S // tr
    n = tr + H
    rc = min(ROW_CHUNK, tr)

    def body(dz_ref, dzn_ref, gp_ref, g_ref, gn_ref, u_ref, un_ref, w_ref, b_ref,
             dg_ref, du_ref, dw_ref, db_ref, G, DG):
        i = pl.program_id(1)
        G[0:H, :] = jnp.where(i > 0, gp_ref[...].astype(F32), 0.0)
        G[H:H + tr, :] = g_ref[...].astype(F32)
        G[H + tr:H + tr + H, :] = gn_ref[...].astype(F32)
        w0, w1, w2, b = w_ref[pl.ds(0, 1), :], w_ref[pl.ds(1, 1), :], w_ref[pl.ds(2, 1), :], b_ref[...]

        def fold(v):
            return jnp.sum(v.reshape(v.shape[0] // 8, 8, tc), axis=0)

        def d_gate(r0, rows, dzf, uf):
            taps = [G[pl.ds(H - 2 + k + r0, rows), :] for k in range(3)]
            gc = b + w0 * taps[0] + w1 * taps[1] + w2 * taps[2]
            sg = _sigmoid(gc)
            return dzf * uf * (sg * (1.0 + gc * (1.0 - sg))), gc * sg, taps

        acc_w = [jnp.zeros((8, tc), F32) for _ in range(3)]
        acc_b = jnp.zeros((8, tc), F32)
        for r0 in range(0, tr, rc):
            dzf = dz_ref[pl.ds(r0, rc), :].astype(F32)
            dgc, silu, taps = d_gate(r0, rc, dzf, u_ref[pl.ds(r0, rc), :].astype(F32))
            du_ref[pl.ds(r0, rc), :] = (dzf * silu).astype(BF16)
            DG[pl.ds(r0, rc), :] = dgc
            acc_w = [acc_w[k] + fold(dgc * taps[k]) for k in range(3)]
            acc_b = acc_b + fold(dgc)
        dzn = jnp.where(i < nr - 1, dzn_ref[...].astype(F32), 0.0)
        DG[pl.ds(tr, H), :] = d_gate(tr, H, dzn, un_ref[...].astype(F32))[0]
        for r0 in range(0, tr, rc):
            dg = w2 * DG[pl.ds(r0, rc), :] + w1 * DG[pl.ds(r0 + 1, rc), :] + w0 * DG[pl.ds(r0 + 2, rc), :]
            dg_ref[pl.ds(r0, rc), :] = dg.astype(BF16)
        pw = [jnp.sum(a, axis=0, keepdims=True) for a in acc_w]
        pb = jnp.sum(acc_b, axis=0, keepdims=True)

        @pl.when(i == 0)
        def _():
            for k in range(3):
                dw_ref[pl.ds(k, 1), :] = pw[k]
            db_ref[...] = pb

        @pl.when(i > 0)
        def _():
            for k in range(3):
                dw_ref[pl.ds(k, 1), :] += pw[k]
            db_ref[...] += pb

    gcol = lambda c: c
    ucol = lambda c: c + nc
    blk = pl.BlockSpec((tr, tc), lambda c, i: (i, c))
    return pl.pallas_call(
        body, name=name,
        out_shape=(jax.ShapeDtypeStruct((S, F), BF16), jax.ShapeDtypeStruct((S, F), BF16),
                   jax.ShapeDtypeStruct((3, F), F32), jax.ShapeDtypeStruct((1, F), F32)),
        grid=(nc, nr),
        in_specs=[_cur_spec(tr, tc, gcol), _next_spec(tr, H, tc, gcol, S),
                  _prev_spec(tr, H, tc, gcol), _cur_spec(tr, tc, gcol), _next_spec(tr, H, tc, gcol, S),
                  _cur_spec(tr, tc, ucol), _next_spec(tr, H, tc, ucol, S),
                  pl.BlockSpec((3, tc), lambda c, i: (0, c)), pl.BlockSpec((1, tc), lambda c, i: (0, c))],
        out_specs=(blk, blk, pl.BlockSpec((3, tc), lambda c, i: (0, c)), pl.BlockSpec((1, tc), lambda c, i: (0, c))),
        scratch_shapes=[pltpu.VMEM((H + tr + H, tc), F32), pltpu.VMEM((n, tc), F32)],
        compiler_params=_params("parallel", "arbitrary"),
    )(dz, dz, hu, hu, hu, hu, hu, conv_w, conv_b)


def _softplus_neg(s):
    return jnp.minimum(-s, 0.0) - jnp.log(1.0 + jnp.exp(-jnp.abs(s)))


def _hilo_dot(v, m):
    hi = v.astype(BF16)
    lo = (v - hi.astype(F32)).astype(BF16)
    return (jnp.dot(hi, m, preferred_element_type=F32) + jnp.dot(lo, m, preferred_element_type=F32))


def _attn_fwd(h, n_heads, name, gather=()):
    S = h.shape[0]
    dh = HEAD_DIM
    A = n_heads * dh
    tq = _pick(S, 256)
    nq = S // tq
    scale = 1.0 / float(dh) ** 0.5
    ng = len(gather)
    hp = 2 if n_heads % 2 == 0 else 1
    n_grp, hw = n_heads // hp, hp * dh

    def body(*refs):
        q_ref, k_ref, v_ref = refs[:3]
        o_ref, tot_ref = refs[3 + ng:5 + ng]
        full = refs[5 + ng:5 + 2 * ng]
        hd = pl.program_id(0)
        i = pl.program_id(1)
        if ng:
            ssem, rsem = refs[5 + 2 * ng:]

            @pl.when((hd == 0) & (i == 0))
            def _():
                _gather_start(full, ssem, rsem)

            @pl.when((hd == n_grp - 1) & (i == 0))
            def _():
                _gather_forward(full, ssem, rsem)

        heads = range(hp)
        qs = [q_ref[:, h * dh:(h + 1) * dh] for h in heads]
        r_io = lax.broadcasted_iota(jnp.int32, (tq, tq), 0)
        c_io = lax.broadcasted_iota(jnp.int32, (tq, tq), 1)
        later = (r_io > c_io).astype(BF16)
        causal = c_io < r_io

        def rows(ref, j):
            blk = ref[pl.ds(pl.multiple_of(j * tq, tq), tq), :]
            return [blk[:, h * dh:(h + 1) * dh] for h in heads]

        def qk(kj):
            return [lax.dot_general(qs[h], kj[h], (((1,), (1,)), ((), ())), preferred_element_type=F32) * scale
                    for h in heads]

        def log_weights(s, diag):
            base, tot = [], []
            for h in heads:
                ls = _softplus_neg(s[h])
                if diag:
                    ls = jnp.where(causal, ls, 0.0)
                cs = _hilo_dot(ls, later)
                b = s[h] + ls + cs
                base.append(jnp.where(causal, b, -1e30) if diag else b)
                tot.append(cs[:, 0:1] + ls[:, 0:1])
            return base, tot

        def weigh(vj, acc, run, base):
            out = []
            for h in heads:
                w = jnp.exp(base[h] + run[h])
                out.append(acc[h] + jnp.dot(w.astype(BF16), vj[h], preferred_element_type=F32))
            return out

        def trip(t, carry):
            acc, run, base, tot = carry
            j = i - 1 - t
            s = qk(rows(k_ref, j))
            acc = weigh(rows(v_ref, j + 1), acc, run, base)
            base_n, tot_n = log_weights(s, False)
            return acc, [run[h] + tot[h] for h in heads], base_n, tot_n

        base, tot = log_weights(qk(rows(k_ref, i)), True)
        carry = ([jnp.zeros((tq, dh), F32) for _ in heads], [jnp.zeros((tq, 1), F32) for _ in heads], base, tot)
        acc, run, base, tot = lax.fori_loop(0, i, trip, carry)
        acc = weigh(rows(v_ref, 0), acc, run, base)
        for h in heads:
            o_ref[:, h * dh:(h + 1) * dh] = acc[h].astype(BF16)
            tot_ref[h] = jnp.broadcast_to(run[h] + tot[h], (tq, LANES))
        if ng:
            @pl.when((hd == n_grp - 1) & (i == nq - 1))
            def _():
                _gather_finish(full, ssem, rsem)

    T = _gather_items(gather) if ng else 0
    return pl.pallas_call(
        body, name=name,
        out_shape=(jax.ShapeDtypeStruct((S, A), BF16), jax.ShapeDtypeStruct((n_heads, S, LANES), F32))
        + tuple(jax.ShapeDtypeStruct(b.shape, b.dtype) for b in gather),
        grid=(n_grp, nq),
        in_specs=[pl.BlockSpec((tq, hw), lambda hd, i: (i, hd)),
                  pl.BlockSpec((S, hw), lambda hd, i: (0, n_grp + hd)),
                  pl.BlockSpec((S, hw), lambda hd, i: (0, 2 * n_grp + hd))] + [ANY] * ng,
        out_specs=(pl.BlockSpec((tq, hw), lambda hd, i: (i, hd)),
                   pl.BlockSpec((hp, tq, LANES), lambda hd, i: (hd, i, 0))) + (ANY,) * ng,
        input_output_aliases={3 + a: 2 + a for a in range(ng)},
        scratch_shapes=[pltpu.SemaphoreType.DMA((6 * T,)), pltpu.SemaphoreType.DMA((6 * T,))] if ng else [],
        compiler_params=_params("arbitrary", "arbitrary") if ng else _params("parallel", "parallel"),
    )(h, h, h, *gather)


def _attn_bwd(h, do, tot, n_heads, name, hosted=None):
    S = h.shape[0]
    dh = HEAD_DIM
    A = n_heads * dh
    tq = _pick(S, 256)
    nq = S // tq
    scale = 1.0 / float(dh) ** 0.5
    nt_dims = (((1,), (1,)), ((), ()))
    tn_dims = (((0,), (0,)), ((), ()))
    hp = 2 if n_heads % 2 == 0 else 1
    n_grp, hw = n_heads // hp, hp * dh
    h_in = 0 if hosted is None else len(hosted.ins)
    h_out = 0 if hosted is None else len(hosted.outs)

    def body(*refs):
        q_ref, k_ref, v_ref, do_ref, tot_ref = refs[:5]
        h_ins = refs[5:5 + h_in]
        dq_ref, dk_ref, dv_ref = refs[5 + h_in:8 + h_in]
        h_outs = refs[8 + h_in:8 + h_in + h_out]
        dk_acc, dv_acc = refs[8 + h_in + h_out:10 + h_in + h_out]
        sems = refs[10 + h_in + h_out:]
        hd = pl.program_id(0)
        i = pl.program_id(1)
        if hosted is not None:
            @pl.when((hd == 0) & (i == 0))
            def _():
                hosted.start(h_ins, h_outs, sems)

        @pl.when(i == 0)
        def _():
            dk_acc[...] = jnp.zeros_like(dk_acc)
            dv_acc[...] = jnp.zeros_like(dv_acc)

        heads = range(hp)
        qs = [q_ref[:, h * dh:(h + 1) * dh] for h in heads]
        dos = [do_ref[:, h * dh:(h + 1) * dh] for h in heads]
        total = [tot_ref[h][:, 0:1] for h in heads]
        r_io = lax.broadcasted_iota(jnp.int32, (tq, tq), 0)
        c_io = lax.broadcasted_iota(jnp.int32, (tq, tq), 1)
        upto = (r_io <= c_io).astype(BF16)
        before = (r_io < c_io).astype(BF16)
        causal = c_io < r_io

        def rows(ref, j):
            blk = ref[pl.ds(pl.multiple_of(j * tq, tq), tq), :]
            return [blk[:, h * dh:(h + 1) * dh] for h in heads]

        def qk(kj):
            return [lax.dot_general(qs[h], kj[h], nt_dims, preferred_element_type=F32) * scale for h in heads]

        def weights(base, prun, vj):
            dw = [lax.dot_general(dos[h], vj[h], nt_dims, preferred_element_type=F32) for h in heads]
            w, e, ce = [], [], []
            for h in heads:
                w.append(jnp.exp(base[h] + (total[h] - prun[h])))
                e.append(dw[h] * w[h])
                ce.append(jnp.dot(e[h].astype(BF16), before, preferred_element_type=F32))
            return w, e, ce

        def prefix(s, j):
            keep = jnp.logical_or(causal, j != i)
            ls = [jnp.where(keep, _softplus_neg(s[h]), 0.0) for h in heads]
            return keep, ls, [_hilo_dot(ls[h], upto) for h in heads]

        def grads(j, kj, dq, erun, w, e, ce, sn):
            start = pl.multiple_of(j * tq, tq)
            out = []
            for h in heads:
                ecum = ce[h] + erun[h]
                dz = e[h] * sn[h] - (1.0 - sn[h]) * ecum
                ds = (dz * scale).astype(BF16)
                cols = slice(h * dh, (h + 1) * dh)
                dv_acc[pl.ds(start, tq), cols] += lax.dot_general(w[h].astype(BF16), dos[h], tn_dims,
                                                                  preferred_element_type=F32)
                out.append(dq[h] + jnp.dot(ds, kj[h], preferred_element_type=F32))
                dk_acc[pl.ds(start, tq), cols] += lax.dot_general(ds, qs[h], tn_dims, preferred_element_type=F32)
            return out, [erun[h] + ce[h][:, tq - 1:tq] + e[h][:, tq - 1:tq] for h in heads]

        def carried(s, keep, ls, cs):
            base = [jnp.where(keep, s[h] + ls[h] - cs[h], -1e30) for h in heads]
            return base, [jnp.exp(ls[h]) for h in heads], [cs[h][:, tq - 1:tq] for h in heads]

        def trip(j, carry):
            dq, prun, erun, base, sn, ptot = carry
            s_n = qk(rows(k_ref, j + 1))
            w, e, ce = weights(base, prun, rows(v_ref, j))
            keep, ls_n, cs = prefix(s_n, j + 1)
            dq, erun = grads(j, rows(k_ref, j), dq, erun, w, e, ce, sn)
            return (dq, [prun[h] + ptot[h] for h in heads], erun) + carried(s_n, keep, ls_n, cs)

        zeros = [jnp.zeros((tq, 1), F32) for _ in heads]
        s0 = qk(rows(k_ref, 0))
        first = carried(s0, *prefix(s0, 0))
        carry = lax.fori_loop(0, i, trip, ([jnp.zeros((tq, dh), F32) for _ in heads], zeros, zeros) + first)
        dq, prun, erun, base, sn, _ = carry
        dq, _ = grads(i, rows(k_ref, i), dq, erun, *weights(base, prun, rows(v_ref, i)), sn)
        for h in heads:
            dq_ref[:, h * dh:(h + 1) * dh] = dq[h].astype(BF16)

        @pl.when(i == nq - 1)
        def _():
            dk_ref[...] = dk_acc[...].astype(BF16)
            dv_ref[...] = dv_acc[...].astype(BF16)

        if hosted is not None:
            @pl.when((hd == n_grp - 1) & (i == nq - 1))
            def _():
                hosted.finish(h_ins, h_outs, sems)

    qblk = pl.BlockSpec((tq, hw), lambda hd, i: (i, hd))
    full = pl.BlockSpec((S, hw), lambda hd, i: (0, hd))
    scratch = [pltpu.VMEM((S, hw), F32), pltpu.VMEM((S, hw), F32)]
    if hosted is not None:
        scratch += [pltpu.SemaphoreType.DMA((n,)) for n in hosted.sems]
    res = pl.pallas_call(
        body, name=name,
        out_shape=tuple(jax.ShapeDtypeStruct((S, A), BF16) for _ in range(3))
        + (tuple(hosted.outs) if hosted is not None else ()),
        grid=(n_grp, nq),
        in_specs=[qblk,
                  pl.BlockSpec((S, hw), lambda hd, i: (0, n_grp + hd)),
                  pl.BlockSpec((S, hw), lambda hd, i: (0, 2 * n_grp + hd)),
                  qblk,
                  pl.BlockSpec((hp, tq, LANES), lambda hd, i: (hd, i, 0))] + [ANY] * h_in,
        out_specs=(qblk, full, full) + (ANY,) * h_out,
        input_output_aliases={} if hosted is None else {5 + a: 3 + b for a, b in hosted.alias.items()},
        scratch_shapes=scratch,
        compiler_params=_params("arbitrary", "arbitrary") if hosted is not None else _params("parallel", "arbitrary"),
    )(h, h, h, do, tot, *(hosted.ins if hosted is not None else ()))
    return res[0], res[1], res[2], list(res[3:])


def _evenconv_fwd(h, dw_w, dw_b, bn_g, bn_b, name):
    S = h.shape[0]
    KW, A = dw_w.shape
    H = HALO31
    tr = _pick(S, 256, H)
    first_tap = H - (KW - 1)

    def body(ap_ref, a_ref, gp_ref, g_ref, w_ref, b_ref, bg_ref, bb_ref, u1_ref, u3_ref, U):
        i = pl.program_id(1)
        glu_prev = ap_ref[...].astype(F32) * _sigmoid(gp_ref[...].astype(F32))
        U[0:H, :] = jnp.where(i > 0, glu_prev, 0.0)
        U[H:H + tr, :] = a_ref[...].astype(F32) * _sigmoid(g_ref[...].astype(F32))
        acc = b_ref[...] + w_ref[pl.ds(0, 1), :] * U[pl.ds(first_tap, tr), :]
        for k in range(1, KW):
            acc = acc + w_ref[pl.ds(k, 1), :] * U[pl.ds(first_tap + k, tr), :]
        u1_ref[...] = acc
        mu = jnp.mean(acc, axis=-1, keepdims=True)
        d = acc - mu
        var = jnp.mean(d * d, axis=-1, keepdims=True)
        u2 = d * lax.rsqrt(var + LN_EPS) * bg_ref[...] + bb_ref[...]
        u3_ref[...] = (u2 * _sigmoid(u2)).astype(BF16)

    acol = lambda c: 3
    gcol = lambda c: 4
    vec = pl.BlockSpec((1, A), lambda c, i: (0, 0))
    blk = pl.BlockSpec((tr, A), lambda c, i: (i, 0))
    return pl.pallas_call(
        body, name=name,
        out_shape=(jax.ShapeDtypeStruct((S, A), F32), jax.ShapeDtypeStruct((S, A), BF16)),
        grid=(1, S // tr),
        in_specs=[_prev_spec(tr, H, A, acol), _cur_spec(tr, A, acol),
                  _prev_spec(tr, H, A, gcol), _cur_spec(tr, A, gcol),
                  pl.BlockSpec((KW, A), lambda c, i: (0, 0)), vec, vec, vec],
        out_specs=(blk, blk),
        scratch_shapes=[pltpu.VMEM((H + tr, A), F32)],
        compiler_params=_params("parallel", "parallel"),
    )(h, h, h, h, dw_w, dw_b, bn_g, bn_b)


def _evenconv_bwd(du3, u1, h, dw_w, bn_g, bn_b, name):
    S = h.shape[0]
    KW, A = dw_w.shape
    H = HALO31
    tr = _pick(S, 256, H)
    nr = S // tr
    n = tr + H
    first_tap = H - (KW - 1)

    def body(d3_ref, d3n_ref, u1_ref, u1n_ref, ap_ref, a_ref, gp_ref, g_ref, w_ref, bg_ref, bb_ref,
             da_ref, dg_ref, dww_ref, dwb_ref, dbg_ref, dbb_ref, U0, DU):
        i = pl.program_id(1)
        u1 = jnp.concatenate([u1_ref[...], u1n_ref[...]], axis=0)
        d3 = jnp.concatenate([d3_ref[...], d3n_ref[...]], axis=0).astype(F32)
        rows = lax.broadcasted_iota(jnp.int32, (n, 1), 0)
        d3 = jnp.where((rows < tr) | (i < nr - 1), d3, 0.0)
        mu = jnp.mean(u1, axis=-1, keepdims=True)
        d = u1 - mu
        var = jnp.mean(d * d, axis=-1, keepdims=True)
        rstd = lax.rsqrt(var + LN_EPS)
        xh = d * rstd
        u2 = xh * bg_ref[...] + bb_ref[...]
        sg = _sigmoid(u2)
        du2 = d3 * (sg * (1.0 + u2 * (1.0 - sg)))
        dxh = du2 * bg_ref[...]
        m1 = jnp.mean(dxh, axis=-1, keepdims=True)
        m2 = jnp.mean(dxh * xh, axis=-1, keepdims=True)
        du1 = rstd * (dxh - m1 - xh * m2)
        DU[...] = du1
        pbg = jnp.sum(du2[0:tr] * xh[0:tr], axis=0, keepdims=True)
        pbb = jnp.sum(du2[0:tr], axis=0, keepdims=True)
        pwb = jnp.sum(du1[0:tr], axis=0, keepdims=True)

        glu_prev = ap_ref[...].astype(F32) * _sigmoid(gp_ref[...].astype(F32))
        U0[0:H, :] = jnp.where(i > 0, glu_prev, 0.0)
        a = a_ref[...].astype(F32)
        sgg = _sigmoid(g_ref[...].astype(F32))
        U0[H:H + tr, :] = a * sgg

        @pl.when(i == 0)
        def _():
            dbg_ref[...] = pbg
            dbb_ref[...] = pbb
            dwb_ref[...] = pwb
            dww_ref[...] = jnp.zeros_like(dww_ref)

        @pl.when(i > 0)
        def _():
            dbg_ref[...] += pbg
            dbb_ref[...] += pbb
            dwb_ref[...] += pwb

        du0 = w_ref[pl.ds(0, 1), :] * DU[pl.ds(KW - 1, tr), :]
        for k in range(1, KW):
            du0 = du0 + w_ref[pl.ds(k, 1), :] * DU[pl.ds(KW - 1 - k, tr), :]
        da_ref[...] = (du0 * sgg).astype(BF16)
        dg_ref[...] = (du0 * a * sgg * (1.0 - sgg)).astype(BF16)
        dcur = DU[pl.ds(0, tr), :]
        for k in range(KW):
            dww_ref[pl.ds(k, 1), :] += jnp.sum(dcur * U0[pl.ds(first_tap + k, tr), :], axis=0, keepdims=True)

    acol = lambda c: 3
    gcol = lambda c: 4
    one = lambda c: 1
    zero = lambda c: 0
    vec = pl.BlockSpec((1, A), lambda c, i: (0, 0))
    blk = pl.BlockSpec((tr, A), lambda c, i: (i, 0))
    return pl.pallas_call(
        body, name=name,
        out_shape=(jax.ShapeDtypeStruct((S, A), BF16), jax.ShapeDtypeStruct((S, A), BF16),
                   jax.ShapeDtypeStruct((KW, A), F32), jax.ShapeDtypeStruct((1, A), F32),
                   jax.ShapeDtypeStruct((1, A), F32), jax.ShapeDtypeStruct((1, A), F32)),
        grid=(1, nr),
        in_specs=[_cur_spec(tr, A, one), _next_spec(tr, H, A, one, S),
                  _cur_spec(tr, A, zero), _next_spec(tr, H, A, zero, S),
                  _prev_spec(tr, H, A, acol), _cur_spec(tr, A, acol),
                  _prev_spec(tr, H, A, gcol), _cur_spec(tr, A, gcol),
                  pl.BlockSpec((KW, A), lambda c, i: (0, 0)), vec, vec],
        out_specs=(blk, blk, pl.BlockSpec((KW, A), lambda c, i: (0, 0)), vec, vec, vec),
        scratch_shapes=[pltpu.VMEM((H + tr, A), F32), pltpu.VMEM((n, A), F32)],
        compiler_params=_params("arbitrary", "arbitrary"),
    )(du3, du3, u1, u1, h, h, h, h, dw_w, bn_g, bn_b)


def _pool_inv_count(row0, nrows, window):
    t = row0 + lax.broadcasted_iota(jnp.int32, (nrows, 1), 0)
    return 1.0 / jnp.minimum(t + 1, window).astype(F32)


def _odd_fwd(h, conv_w, pool_w, pool_scale, name):
    S = h.shape[0]
    C = conv_w.shape[1]
    G = len(POOL_WINDOWS)
    Dg = C // G
    H = HALO3
    tr = _pick(S, 256, H)

    def body(cb_ref, ccp_ref, cc_ref, chp_ref, ch_ref, pp_ref, p_ref, w_ref, pw_ref, sc_ref, mix_ref, M, P):
        i = pl.program_id(1)
        M[0:H, :] = jnp.where(i > 0, ccp_ref[...].astype(F32) * chp_ref[...].astype(F32), 0.0)
        M[H:H + tr, :] = cc_ref[...].astype(F32) * ch_ref[...].astype(F32)
        cm = (w_ref[pl.ds(0, 1), :] * M[pl.ds(H - 2, tr), :] + w_ref[pl.ds(1, 1), :] * M[pl.ds(H - 1, tr), :]
              + w_ref[pl.ds(2, 1), :] * M[pl.ds(H, tr), :])
        mix_ref[:, 0:C] = (cb_ref[...].astype(F32) * cm).astype(BF16)
        P[0:H, :] = jnp.where(i > 0, pp_ref[...].astype(F32), 0.0)
        P[H:H + tr, :] = p_ref[...].astype(F32)
        for gi, window in enumerate(POOL_WINDOWS):
            cols = pl.ds(gi * Dg, Dg)
            wsum = P[pl.ds(H, tr), cols]
            for dlt in range(1, window):
                wsum = wsum + P[pl.ds(H - dlt, tr), cols]
            diff = wsum * _pool_inv_count(i * tr, tr, window) - P[pl.ds(H, tr), cols]
            yd = jnp.dot(diff.astype(BF16), pw_ref[gi], preferred_element_type=F32) * sc_ref[:, cols]
            mix_ref[:, pl.ds(C + gi * Dg, Dg)] = yd.astype(BF16)

    col = lambda k: (lambda c: k)
    return pl.pallas_call(
        body, name=name, out_shape=jax.ShapeDtypeStruct((S, 2 * C), BF16),
        grid=(1, S // tr),
        in_specs=[_cur_spec(tr, C, col(0)),
                  _prev_spec(tr, H, C, col(1)), _cur_spec(tr, C, col(1)),
                  _prev_spec(tr, H, C, col(2)), _cur_spec(tr, C, col(2)),
                  _prev_spec(tr, H, C, col(3)), _cur_spec(tr, C, col(3)),
                  pl.BlockSpec((3, C), lambda c, i: (0, 0)),
                  pl.BlockSpec((G, Dg, Dg), lambda c, i: (0, 0, 0)),
                  pl.BlockSpec((1, C), lambda c, i: (0, 0))],
        out_specs=pl.BlockSpec((tr, 2 * C), lambda c, i: (i, 0)),
        scratch_shapes=[pltpu.VMEM((H + tr, C), F32), pltpu.VMEM((H + tr, C), F32)],
        compiler_params=_params("parallel", "parallel"),
    )(h, h, h, h, h, h, h, conv_w, pool_w, pool_scale)


def _odd_bwd(dmix, h, conv_w, pool_w, pool_scale, name):
    S = h.shape[0]
    C = conv_w.shape[1]
    G = len(POOL_WINDOWS)
    Dg = C // G
    H = HALO3
    tr = _pick(S, 256, H)
    nr = S // tr
    n = tr + H
    nt_dims = (((1,), (1,)), ((), ()))
    tn_dims = (((0,), (0,)), ((), ()))

    def body(dyc_ref, dycn_ref, dyd_ref, dydn_ref, cb_ref, cbn_ref, ccp_ref, cc_ref, ccn_ref,
             chp_ref, ch_ref, chn_ref, pp_ref, p_ref, w_ref, pw_ref, sc_ref,
             dh_ref, dw_ref, dpw_ref, dsc_ref, M, DCM, P, Q):
        i = pl.program_id(1)
        rows = lax.broadcasted_iota(jnp.int32, (n, 1), 0)
        valid = (rows < tr) | (i < nr - 1)

        @pl.when(i == 0)
        def _():
            dw_ref[...] = jnp.zeros_like(dw_ref)
            dpw_ref[...] = jnp.zeros_like(dpw_ref)
            dsc_ref[...] = jnp.zeros_like(dsc_ref)

        M[0:H, :] = jnp.where(i > 0, ccp_ref[...].astype(F32) * chp_ref[...].astype(F32), 0.0)
        cc = cc_ref[...].astype(F32)
        ch = ch_ref[...].astype(F32)
        M[H:H + tr, :] = cc * ch
        M[H + tr:H + tr + H, :] = ccn_ref[...].astype(F32) * chn_ref[...].astype(F32)
        w0, w1, w2 = w_ref[pl.ds(0, 1), :], w_ref[pl.ds(1, 1), :], w_ref[pl.ds(2, 1), :]
        cm = w0 * M[pl.ds(H - 2, tr), :] + w1 * M[pl.ds(H - 1, tr), :] + w2 * M[pl.ds(H, tr), :]
        dyc = jnp.concatenate([dyc_ref[...], dycn_ref[...]], axis=0).astype(F32)
        dyc = jnp.where(valid, dyc, 0.0)
        cbf = jnp.concatenate([cb_ref[...], cbn_ref[...]], axis=0).astype(F32)
        dh_ref[:, 0:C] = (dyc[0:tr] * cm).astype(BF16)
        DCM[...] = dyc * cbf
        dm = w2 * DCM[pl.ds(0, tr), :] + w1 * DCM[pl.ds(1, tr), :] + w0 * DCM[pl.ds(2, tr), :]
        dh_ref[:, C:2 * C] = (dm * ch).astype(BF16)
        dh_ref[:, 2 * C:3 * C] = (dm * cc).astype(BF16)
        dcur = DCM[pl.ds(0, tr), :]
        for k in range(3):
            dw_ref[pl.ds(k, 1), :] += jnp.sum(dcur * M[pl.ds(H - 2 + k, tr), :], axis=0, keepdims=True)

        P[0:H, :] = jnp.where(i > 0, pp_ref[...].astype(F32), 0.0)
        P[H:H + tr, :] = p_ref[...].astype(F32)
        dyd = jnp.concatenate([dyd_ref[...], dydn_ref[...]], axis=0).astype(F32)
        dyd = jnp.where(valid, dyd, 0.0)
        for gi, window in enumerate(POOL_WINDOWS):
            cols = pl.ds(gi * Dg, Dg)
            lo = gi * Dg
            wsum = P[pl.ds(H, tr), cols]
            for dlt in range(1, window):
                wsum = wsum + P[pl.ds(H - dlt, tr), cols]
            diff = (wsum * _pool_inv_count(i * tr, tr, window) - P[pl.ds(H, tr), cols]).astype(BF16)
            pw = pw_ref[gi]
            dyd_g = dyd[:, lo:lo + Dg]
            e = (dyd_g * sc_ref[:, cols]).astype(BF16)
            yraw = jnp.dot(diff, pw, preferred_element_type=F32)
            dsc_ref[:, cols] += jnp.sum(dyd_g[0:tr] * yraw, axis=0, keepdims=True)
            dpw_ref[gi] += lax.dot_general(diff, e[0:tr], tn_dims, preferred_element_type=F32)
            ddiff = lax.dot_general(e, pw, nt_dims, preferred_element_type=F32)
            Q[:, cols] = ddiff * _pool_inv_count(i * tr, n, window)
            acc = Q[pl.ds(0, tr), cols]
            for dlt in range(1, window):
                acc = acc + Q[pl.ds(dlt, tr), cols]
            dh_ref[:, pl.ds(3 * C + lo, Dg)] = (acc - ddiff[0:tr]).astype(BF16)

    col = lambda k: (lambda c: k)
    return pl.pallas_call(
        body, name=name,
        out_shape=(jax.ShapeDtypeStruct((S, 4 * C), BF16), jax.ShapeDtypeStruct((3, C), F32),
                   jax.ShapeDtypeStruct((G, Dg, Dg), F32), jax.ShapeDtypeStruct((1, C), F32)),
        grid=(1, nr),
        in_specs=[_cur_spec(tr, C, col(0)), _next_spec(tr, H, C, col(0), S),
                  _cur_spec(tr, C, col(1)), _next_spec(tr, H, C, col(1), S),
                  _cur_spec(tr, C, col(0)), _next_spec(tr, H, C, col(0), S),
                  _prev_spec(tr, H, C, col(1)), _cur_spec(tr, C, col(1)), _next_spec(tr, H, C, col(1), S),
                  _prev_spec(tr, H, C, col(2)), _cur_spec(tr, C, col(2)), _next_spec(tr, H, C, col(2), S),
                  _prev_spec(tr, H, C, col(3)), _cur_spec(tr, C, col(3)),
                  pl.BlockSpec((3, C), lambda c, i: (0, 0)),
                  pl.BlockSpec((G, Dg, Dg), lambda c, i: (0, 0, 0)),
                  pl.BlockSpec((1, C), lambda c, i: (0, 0))],
        out_specs=(pl.BlockSpec((tr, 4 * C), lambda c, i: (i, 0)),
                   pl.BlockSpec((3, C), lambda c, i: (0, 0)),
                   pl.BlockSpec((G, Dg, Dg), lambda c, i: (0, 0, 0)),
                   pl.BlockSpec((1, C), lambda c, i: (0, 0))),
        scratch_shapes=[pltpu.VMEM((H + tr + H, C), F32), pltpu.VMEM((n, C), F32),
                        pltpu.VMEM((H + tr, C), F32), pltpu.VMEM((n, C), F32)],
        compiler_params=_params("arbitrary", "arbitrary"),
    )(dmix, dmix, dmix, dmix, h, h, h, h, h, h, h, h, h, h, conv_w, pool_w, pool_scale)


def _adamw(w, grads, m, v, name):
    L, R, C = w.shape
    assert len(grads) == L
    tr, tc = _pick(R, 256, 8), _pick(C, 1408)
    ni, nj = R // tr, C // tc
    c1 = 1.0 / (1.0 - ADAM_B1 ** ADAM_STEP)
    c2 = 1.0 / (1.0 - ADAM_B2 ** ADAM_STEP)

    def g_spec(layer):
        def idx(l, i, j):
            before, after = l < layer, l > layer
            return (jnp.where(before, 0, jnp.where(after, ni - 1, i)),
                    jnp.where(before, 0, jnp.where(after, nj - 1, j)))
        return pl.BlockSpec((tr, tc), idx)

    def body(w_ref, *rest):
        g_refs = rest[:L]
        m_ref, v_ref, go_ref, d_ref, mo_ref, vo_ref = rest[L:]
        l = pl.program_id(0)
        gg = g_refs[0][...]
        for k in range(1, L):
            gg = jnp.where(l == k, g_refs[k][...], gg)
        mn = ADAM_B1 * m_ref[...] + (1.0 - ADAM_B1) * gg
        vn = ADAM_B2 * v_ref[...] + (1.0 - ADAM_B2) * (gg * gg)
        d_ref[...] = -ADAM_LR * ((mn * c1) / (jnp.sqrt(vn * c2) + ADAM_EPS) + ADAM_WD * w_ref[...])
        go_ref[...] = gg
        mo_ref[...] = mn
        vo_ref[...] = vn

    blk = pl.BlockSpec((None, tr, tc), lambda l, i, j: (l, i, j))
    sds = jax.ShapeDtypeStruct(w.shape, F32)
    return pl.pallas_call(
        body, name=name, out_shape=(sds, sds, sds, sds),
        grid=(L, R // tr, C // tc),
        in_specs=[blk] + [g_spec(k) for k in range(L)] + [blk, blk], out_specs=(blk, blk, blk, blk),
        compiler_params=_params("arbitrary", "arbitrary", "arbitrary"),
    )(w, *grads, m, v)


def _sum_slots(buf, name):
    N, R, C = buf.shape
    tr = _pick(R, 512, 8)

    def body(b_ref, o_ref):
        acc = b_ref[0]
        for k in range(1, N):
            acc = acc + b_ref[k]
        o_ref[...] = acc

    return pl.pallas_call(
        body, name=name, out_shape=jax.ShapeDtypeStruct((R, C), F32),
        grid=(R // tr,),
        in_specs=[pl.BlockSpec((N, tr, C), lambda i: (0, i, 0))],
        out_specs=pl.BlockSpec((tr, C), lambda i: (i, 0)),
        compiler_params=_params("parallel"),
    )(buf)


def _pair_sum(pos, g, rsib, name):
    _, hr, hc = rsib.shape
    tr, tc = _pick(hr, 512, 16), _pick(hc, 2816)

    def body(p_ref, g_ref, r_ref, o_ref):
        o_ref[...] = (g_ref[...].astype(F32) + r_ref[...].astype(F32)).astype(BF16)

    blk = pl.BlockSpec((None, tr, tc), lambda s, i, j, p: (s, i, j))
    return pl.pallas_call(
        body, name=name, out_shape=jax.ShapeDtypeStruct(rsib.shape, BF16),
        grid_spec=pltpu.PrefetchScalarGridSpec(
            num_scalar_prefetch=1, grid=(N_CHIPS, hr // tr, hc // tc),
            in_specs=[pl.BlockSpec((None, None, tr, tc), lambda s, i, j, p: (s, p[1], i, j)), blk],
            out_specs=blk),
        compiler_params=_params("parallel", "parallel", "parallel"),
    )(pos, g, rsib)


def _chip_sum(pos, part, land, name, hosted=None):
    _, sr, sc = land.shape
    tr, tc = _pick(sr, 256, 16), _pick(sc, 2816)
    grid = (sr // tr, sc // tc)
    h_in = 0 if hosted is None else len(hosted.ins)
    h_out = 0 if hosted is None else len(hosted.outs)

    def body(p_ref, own_ref, l_ref, *rest):
        h_ins, o_ref = rest[:h_in], rest[h_in]
        h_outs, sems = rest[h_in + 1:h_in + 1 + h_out], rest[h_in + 1 + h_out:]
        i, j = pl.program_id(0), pl.program_id(1)
        if hosted is not None:
            @pl.when((i == 0) & (j == 0))
            def _():
                hosted.start(h_ins, h_outs, sems)

        acc = own_ref[...].astype(F32)
        for k in range(3):
            acc = acc + l_ref[k].astype(F32)
        o_ref[...] = acc
        if hosted is not None:
            @pl.when((i == grid[0] - 1) & (j == grid[1] - 1))
            def _():
                hosted.finish(h_ins, h_outs, sems)

    res = pl.pallas_call(
        body, name=name,
        out_shape=(jax.ShapeDtypeStruct((2, sr, sc), F32),) + (tuple(hosted.outs) if hosted is not None else ()),
        grid_spec=pltpu.PrefetchScalarGridSpec(
            num_scalar_prefetch=1, grid=grid,
            in_specs=[pl.BlockSpec((None, tr, tc), lambda i, j, p: (p[0], i, j)),
                      pl.BlockSpec((3, tr, tc), lambda i, j, p: (0, i, j))] + [ANY] * h_in,
            out_specs=(pl.BlockSpec((None, tr, tc), lambda i, j, p: (p[1], i, j)),) + (ANY,) * h_out,
            scratch_shapes=[] if hosted is None else [pltpu.SemaphoreType.DMA((k,)) for k in hosted.sems]),
        input_output_aliases={} if hosted is None else {3 + a: 1 + b for a, b in hosted.alias.items()},
        compiler_params=_params("arbitrary", "arbitrary") if hosted is not None else _params("parallel", "parallel"),
    )(pos, part, land, *(hosted.ins if hosted is not None else ()))
    return res[0], list(res[1:])


def _place():
    x, y, c = lax.axis_index("x"), lax.axis_index("y"), lax.axis_index("c")
    return x, y, c


def _other_chips(x, y):
    return [(1 - x, y), (x, 1 - y), (1 - x, 1 - y)]


def _rcopy(src, dst, ssem, rsem, dev):
    return pltpu.make_async_remote_copy(src_ref=src, dst_ref=dst, send_sem=ssem, recv_sem=rsem,
                                        device_id=dev, device_id_type=MESH)


def _gather_items(bufs):
    return sum(b.shape[0] for b in bufs)


def _gather_walk(full):
    t = 0
    for ref in full:
        for l in range(ref.shape[0]):
            yield t, ref, l
            t += 1


def _gather_start(full, ssem, rsem):
    x, y, c = _place()
    j = 2 * x + y
    for t, ref, l in _gather_walk(full):
        own = ref.at[l, j, c]
        for r, (px, py) in enumerate(_other_chips(x, y)):
            _rcopy(own, own, ssem.at[6 * t + r], rsem.at[6 * t + r], (px, py, c)).start()


def _gather_forward(full, ssem, rsem):
    x, y, c = _place()
    for t, ref, l in _gather_walk(full):
        for r, (px, py) in enumerate(_other_chips(x, y)):
            slab = ref.at[l, 2 * px + py, c]
            _rcopy(slab, slab, ssem.at[6 * t + r], rsem.at[6 * t + r], (px, py, c)).wait_recv()
            _rcopy(slab, slab, ssem.at[6 * t + 3 + r], rsem.at[6 * t + 3 + r], (x, y, 1 - c)).start()


def _gather_finish(full, ssem, rsem):
    x, y, c = _place()
    j = 2 * x + y
    for t, ref, l in _gather_walk(full):
        for r, (px, py) in enumerate(_other_chips(x, y)):
            got = ref.at[l, 2 * px + py, 1 - c]
            _rcopy(got, got, ssem.at[6 * t + 3 + r], rsem.at[6 * t + 3 + r], (x, y, 1 - c)).wait_recv()
    for t, ref, l in _gather_walk(full):
        own = ref.at[l, j, c]
        for r, (px, py) in enumerate(_other_chips(x, y)):
            _rcopy(own, own, ssem.at[6 * t + r], rsem.at[6 * t + r], (px, py, c)).wait_send()
            slab = ref.at[l, 2 * px + py, c]
            _rcopy(slab, slab, ssem.at[6 * t + 3 + r], rsem.at[6 * t + 3 + r], (x, y, 1 - c)).wait_send()


def _land_shape(part):
    return jax.ShapeDtypeStruct((3,) + part.shape[1:], part.dtype)


def _exchange_start(parts, land, ssem, rsem):
    x, y, c = _place()
    for a in range(len(parts)):
        for r, (px, py) in enumerate(_other_chips(x, y)):
            _rcopy(parts[a].at[2 * px + py], land[a].at[r], ssem.at[3 * a + r], rsem.at[3 * a + r],
                   (px, py, c)).start()


def _exchange_finish(parts, land, ssem, rsem):
    x, y, c = _place()
    for a in range(len(parts)):
        for r, (px, py) in enumerate(_other_chips(x, y)):
            _rcopy(parts[a].at[2 * px + py], land[a].at[r], ssem.at[3 * a + r], rsem.at[3 * a + r],
                   (px, py, c)).wait()


_Hosted = collections.namedtuple("_Hosted", "ins outs alias sems start finish")


def _host_join(*hosts):
    ins, outs, alias, sems, spans = [], [], {}, [], []
    for h in hosts:
        spans.append((h, len(ins), len(outs), len(sems)))
        alias.update({len(ins) + a: len(outs) + b for a, b in h.alias.items()})
        ins, outs, sems = ins + list(h.ins), outs + list(h.outs), sems + list(h.sems)

    def each(step):
        def run(i, o, s):
            for h, a, b, c in spans:
                getattr(h, step)(i[a:a + len(h.ins)], o[b:b + len(h.outs)], s[c:c + len(h.sems)])
        return run

    return _Hosted(ins, outs, alias, sems, each("start"), each("finish"))


def _host_exchange(parts):
    n = len(parts)
    return _Hosted(list(parts), [_land_shape(p) for p in parts], {}, [3 * n, 3 * n],
                   lambda ins, outs, sems: _exchange_start(ins, outs, *sems),
                   lambda ins, outs, sems: _exchange_finish(ins, outs, *sems))


def _host_gather(bufs):
    T = _gather_items(bufs)

    def finish(ins, outs, sems):
        _gather_forward(outs, *sems)
        _gather_finish(outs, *sems)

    return _Hosted(list(bufs), [jax.ShapeDtypeStruct(b.shape, b.dtype) for b in bufs],
                   {a: a for a in range(len(bufs))}, [6 * T, 6 * T],
                   lambda ins, outs, sems: _gather_start(outs, *sems), finish)


def _host_all_devices(buf):
    return _Hosted([buf], [jax.ShapeDtypeStruct((N_DEV,) + buf.shape, buf.dtype)], {}, [N_DEV - 1, N_DEV - 1, 1],
                   lambda ins, outs, sems: _all_devices_start(ins[0], outs[0], *sems),
                   lambda ins, outs, sems: _all_devices_finish(ins[0], outs[0], *sems))


def _allgather_big(bufs, name):
    n = len(bufs)
    T = _gather_items(bufs)

    def body(*refs):
        full = refs[n:2 * n]
        ssem, rsem = refs[2 * n:]
        _gather_start(full, ssem, rsem)
        _gather_forward(full, ssem, rsem)
        _gather_finish(full, ssem, rsem)

    return pl.pallas_call(
        body, name=name, out_shape=tuple(jax.ShapeDtypeStruct(b.shape, BF16) for b in bufs),
        in_specs=[ANY] * n, out_specs=tuple([ANY] * n),
        input_output_aliases={a: a for a in range(n)},
        scratch_shapes=[pltpu.SemaphoreType.DMA((6 * T,)), pltpu.SemaphoreType.DMA((6 * T,))],
    )(*bufs)


def _allgather_small(shards, name):
    n = len(shards)
    outs = tuple(jax.ShapeDtypeStruct((N_CHIPS,) + s.shape, s.dtype) for s in shards)

    def body(*refs):
        ins, full = refs[:n], refs[n:2 * n]
        ssem, rsem, lsem = refs[2 * n:]
        x, y, c = _place()
        j = 2 * x + y
        chips = _other_chips(x, y)
        cps, locs = [], []
        for a in range(n):
            loc = pltpu.make_async_copy(ins[a], full[a].at[j], lsem.at[a])
            loc.start()
            locs.append(loc)
            for r, (px, py) in enumerate(chips):
                cp = _rcopy(ins[a], full[a].at[j], ssem.at[3 * a + r], rsem.at[3 * a + r], (px, py, c))
                cp.start()
                cps.append(cp)
        for a in range(n):
            for r, (px, py) in enumerate(chips):
                dst = full[a].at[2 * px + py]
                _rcopy(dst, dst, ssem.at[3 * a + r], rsem.at[3 * a + r], (px, py, c)).wait_recv()
        for cp in cps:
            cp.wait_send()
        for loc in locs:
            loc.wait()

    return pl.pallas_call(
        body, name=name, out_shape=outs,
        in_specs=[ANY] * n, out_specs=tuple([ANY] * n),
        scratch_shapes=[pltpu.SemaphoreType.DMA((3 * n,)), pltpu.SemaphoreType.DMA((3 * n,)),
                        pltpu.SemaphoreType.DMA((n,))],
    )(*shards)


def _pair_copies(ins, got, ssem, rsem):
    x, y, c = _place()
    return [_rcopy(ins[a].at[s, 1 - c], got[a].at[s], ssem.at[N_CHIPS * a + s], rsem.at[N_CHIPS * a + s],
                   (x, y, 1 - c))
            for a in range(len(ins)) for s in range(N_CHIPS)]


def _host_pair_exchange(grads):
    n = len(grads)

    def start(ins, outs, sems):
        for cp in _pair_copies(ins, outs, *sems):
            cp.start()

    def finish(ins, outs, sems):
        for cp in _pair_copies(ins, outs, *sems):
            cp.wait()

    return _Hosted(list(grads), [jax.ShapeDtypeStruct((N_CHIPS,) + g.shape[2:], BF16) for g in grads], {},
                   [N_CHIPS * n, N_CHIPS * n], start, finish)


def _pair_exchange(grads, name):
    n = len(grads)
    host = _host_pair_exchange(grads)

    def body(*refs):
        ins, got, sems = refs[:n], refs[n:2 * n], refs[2 * n:]
        host.start(ins, got, sems)
        host.finish(ins, got, sems)

    return pl.pallas_call(
        body, name=name, out_shape=tuple(host.outs),
        in_specs=[ANY] * n, out_specs=tuple([ANY] * n),
        scratch_shapes=[pltpu.SemaphoreType.DMA((k,)) for k in host.sems],
    )(*grads)


def _chip_exchange(parts, name):
    n = len(parts)

    def body(*refs):
        ins, land = refs[:n], refs[n:2 * n]
        ssem, rsem = refs[2 * n:]
        _exchange_start(ins, land, ssem, rsem)
        _exchange_finish(ins, land, ssem, rsem)

    return pl.pallas_call(
        body, name=name, out_shape=tuple(_land_shape(p) for p in parts),
        in_specs=[ANY] * n, out_specs=tuple([ANY] * n),
        scratch_shapes=[pltpu.SemaphoreType.DMA((3 * n,)), pltpu.SemaphoreType.DMA((3 * n,))],
    )(*parts)


def _host_half_swap(bufs):
    n = len(bufs)

    def start(ins, full, sems):
        x, y, c = _place()
        for t in range(n):
            mine = full[t].at[c]
            _rcopy(mine, mine, sems[0].at[t], sems[1].at[t], (x, y, 1 - c)).start()

    def finish(ins, full, sems):
        x, y, c = _place()
        for t in range(n):
            got = full[t].at[1 - c]
            _rcopy(got, got, sems[0].at[t], sems[1].at[t], (x, y, 1 - c)).wait_recv()
        for t in range(n):
            mine = full[t].at[c]
            _rcopy(mine, mine, sems[0].at[t], sems[1].at[t], (x, y, 1 - c)).wait_send()

    return _Hosted(list(bufs), [jax.ShapeDtypeStruct(b.shape, F32) for b in bufs], {a: a for a in range(n)},
                   [n, n], start, finish)


def _half_swap(bufs, name):
    n = len(bufs)
    host = _host_half_swap(bufs)

    def body(*refs):
        ins, full, sems = refs[:n], refs[n:2 * n], refs[2 * n:]
        host.start(ins, full, sems)
        host.finish(ins, full, sems)

    return pl.pallas_call(
        body, name=name, out_shape=tuple(host.outs),
        in_specs=[ANY] * n, out_specs=tuple([ANY] * n),
        input_output_aliases=host.alias,
        scratch_shapes=[pltpu.SemaphoreType.DMA((k,)) for k in host.sems],
    )(*bufs)


def _flipped(x, y, c, m):
    fx, fy, fc = (m >> 2) & 1, (m >> 1) & 1, m & 1
    return x + fx - 2 * x * fx, y + fy - 2 * y * fy, c + fc - 2 * c * fc


def _all_devices_start(b_ref, o_ref, ssem, rsem, lsem):
    x, y, c = _place()
    me = 4 * x + 2 * y + c
    pltpu.make_async_copy(b_ref, o_ref.at[me], lsem.at[0]).start()
    for m in range(1, N_DEV):
        _rcopy(b_ref, o_ref.at[me], ssem.at[m - 1], rsem.at[m - 1], _flipped(x, y, c, m)).start()


def _all_devices_finish(b_ref, o_ref, ssem, rsem, lsem):
    x, y, c = _place()
    me = 4 * x + 2 * y + c
    for m in range(1, N_DEV):
        px, py, pc = _flipped(x, y, c, m)
        got = o_ref.at[4 * px + 2 * py + pc]
        _rcopy(got, got, ssem.at[m - 1], rsem.at[m - 1], (px, py, pc)).wait_recv()
    for m in range(1, N_DEV):
        _rcopy(b_ref, o_ref.at[me], ssem.at[m - 1], rsem.at[m - 1], _flipped(x, y, c, m)).wait_send()
    pltpu.make_async_copy(b_ref, o_ref.at[me], lsem.at[0]).wait()


def _pack(arrs):
    flat = jnp.concatenate([a.reshape(-1) for a in arrs])
    rows = -(-flat.shape[0] // (8 * LANES)) * 8
    flat = jnp.pad(flat, (0, rows * LANES - flat.shape[0]))
    return flat.reshape(rows, LANES)


def _unpack(buf, shapes):
    flat = buf.reshape(-1)
    out, off = [], 0
    for s in shapes:
        size = 1
        for d in s:
            size *= d
        out.append(flat[off:off + size].reshape(s))
        off += size
    return out


BIG = ("ev_w_in", "ev_w_out", "od_w_in", "od_w_out", "ffn_w_up", "ffn_w_down")
BIG_KIND = {"ev_w_in": "col", "ev_w_out": "row", "od_w_in": "col", "od_w_out": "row",
            "ffn_w_up": "col", "ffn_w_down": "row"}
SMALL_AXIS = {"ev_dw_w": 2, "ev_dw_b": None, "ev_bn_g": None, "ev_bn_b": None, "od_conv_w": 2,
              "od_pool_w": 2, "od_pool_scale": 1, "ffn_conv_w": 2, "ffn_conv_b": None, "ln_g": 2, "ln_b": 2}
WEIGHTS = ("ev_w_in", "ev_dw_w", "ev_dw_b", "ev_bn_g", "ev_bn_b", "ev_w_out", "od_w_in", "od_conv_w",
           "od_pool_w", "od_pool_scale", "od_w_out", "ffn_w_up", "ffn_conv_w", "ffn_conv_b", "ffn_w_down",
           "ln_g", "ln_b")


def _ffn_fwd(xb, w_up, w_down, conv_w, conv_b, tag, host_up=None, host_down=None):
    hu = _matmul(xb, w_up, mode="nn", b_lead=0, b_split=True, out_dtype=BF16, name=f"{tag}_up", tm=1024, tn=1408,
                 hosted=host_up)
    hu, up_outs = hu if host_up is not None else (hu, None)
    z = _ffn_act_fwd(hu, conv_w, conv_b, name=f"{tag}_act")
    y = _matmul(z, w_down, mode="nn", b_lead=0, out_dtype=F32, name=f"{tag}_down", tm=512, tn=1024,
                hosted=host_down)
    y, down_outs = y if host_down is not None else (y, None)
    return hu, z, y, up_outs, down_outs


def _ffn_bwd(drb, dr, alpha, xb, hu, z, w_up, w_down, conv_w, conv_b, tag, host_dwup=None, host_dx=None):
    g_down = _matmul(z, drb, mode="tn", out_dtype=BF16, name=f"{tag}_dwdown", tm=512, tn=1024)
    dz = _matmul(drb, w_down, mode="nt", b_lead=0, out_dtype=BF16, name=f"{tag}_dz", tm=1024, tn=1408)
    dg, du, dcw, dcb = _ffn_act_bwd(dz, hu, conv_w, conv_b, name=f"{tag}_actbwd")
    g_up = _matmul(xb, (dg, du), mode="tn", out_split=True, out_dtype=BF16, name=f"{tag}_dwup", tm=512, tn=1408,
                   hosted=host_dwup)
    g_up, dwup_outs = g_up if host_dwup is not None else (g_up, None)
    dx = _matmul((dg, du), w_up, mode="nt", b_lead=0, b_split=True, out_dtype=F32, add=dr, add_scale=alpha,
                 name=f"{tag}_dx", tm=512, tn=512, hosted=host_dx)
    dx, dx_outs = dx if host_dx is not None else (dx, None)
    return dx, g_up, g_down, dcw, dcb, dwup_outs, dx_outs


def kernel(x, ev_w_in, ev_dw_w, ev_dw_b, ev_bn_g, ev_bn_b, ev_w_out, od_w_in, od_conv_w, od_pool_w, od_pool_scale, od_w_out, ffn_w_up, ffn_conv_w, ffn_conv_b, ffn_w_down, ln_g, ln_b, loss_target, m_ev_w_in, m_ev_dw_w, m_ev_dw_b, m_ev_bn_g, m_ev_bn_b, m_ev_w_out, m_od_w_in, m_od_conv_w, m_od_pool_w, m_od_pool_scale, m_od_w_out, m_ffn_w_up, m_ffn_conv_w, m_ffn_conv_b, m_ffn_w_down, m_ln_g, m_ln_b, v_ev_w_in, v_ev_dw_w, v_ev_dw_b, v_ev_bn_g, v_ev_bn_b, v_ev_w_out, v_od_w_in, v_od_conv_w, v_od_pool_w, v_od_pool_scale, v_od_w_out, v_ffn_w_up, v_ffn_conv_w, v_ffn_conv_b, v_ffn_w_down, v_ln_g, v_ln_b):
    wts = dict(ev_w_in=ev_w_in, ev_dw_w=ev_dw_w, ev_dw_b=ev_dw_b, ev_bn_g=ev_bn_g, ev_bn_b=ev_bn_b,
               ev_w_out=ev_w_out, od_w_in=od_w_in, od_conv_w=od_conv_w, od_pool_w=od_pool_w,
               od_pool_scale=od_pool_scale, od_w_out=od_w_out, ffn_w_up=ffn_w_up, ffn_conv_w=ffn_conv_w,
               ffn_conv_b=ffn_conv_b, ffn_w_down=ffn_w_down, ln_g=ln_g, ln_b=ln_b)
    mom = dict(ev_w_in=m_ev_w_in, ev_dw_w=m_ev_dw_w, ev_dw_b=m_ev_dw_b, ev_bn_g=m_ev_bn_g, ev_bn_b=m_ev_bn_b,
               ev_w_out=m_ev_w_out, od_w_in=m_od_w_in, od_conv_w=m_od_conv_w, od_pool_w=m_od_pool_w,
               od_pool_scale=m_od_pool_scale, od_w_out=m_od_w_out, ffn_w_up=m_ffn_w_up, ffn_conv_w=m_ffn_conv_w,
               ffn_conv_b=m_ffn_conv_b, ffn_w_down=m_ffn_w_down, ln_g=m_ln_g, ln_b=m_ln_b)
    var = dict(ev_w_in=v_ev_w_in, ev_dw_w=v_ev_dw_w, ev_dw_b=v_ev_dw_b, ev_bn_g=v_ev_bn_g, ev_bn_b=v_ev_bn_b,
               ev_w_out=v_ev_w_out, od_w_in=v_od_w_in, od_conv_w=v_od_conv_w, od_pool_w=v_od_pool_w,
               od_pool_scale=v_od_pool_scale, od_w_out=v_od_w_out, ffn_w_up=v_ffn_w_up, ffn_conv_w=v_ffn_conv_w,
               ffn_conv_b=v_ffn_conv_b, ffn_w_down=v_ffn_w_down, ln_g=v_ln_g, ln_b=v_ln_b)

    S, D = x.shape[1], x.shape[2]
    depth = ln_g.shape[0]
    alpha = (2.0 * depth) ** 0.25
    A = ev_dw_b.shape[-1]
    n_heads = A // HEAD_DIM
    xi, yi, ci = _place()
    chip = 2 * xi + yi
    pos = jnp.stack([chip, ci]).astype(jnp.int32)

    bufs = {f"{k}{l}": _cast_into_gather(pos, wts[k], l, name=f"cast_{k}{l}")
            for k in BIG for l in range(wts[k].shape[0])}

    def whole(key):
        _, r, c = wts[key[:-1]].shape
        col = BIG_KIND[key[:-1]] == "col"
        return bufs[key].reshape(1, N_CHIPS, r, c) if col else bufs[key].reshape(1, N_CHIPS * r, c)

    full = {}

    def gathered_now(keys, arrays):
        bufs.update(zip(keys, arrays))
        full.update({key: whole(key) for key in keys})

    under_attn = ("ffn_w_up0", "ffn_w_down0", "od_w_in0", "od_w_out0")
    gathered_now(("ev_w_in0",), _allgather_big([bufs["ev_w_in0"]], name="gather_first"))
    small_sharded = [k for k in WEIGHTS if k not in BIG and SMALL_AXIS[k] is not None]
    gathered = _allgather_small([wts[k] for k in small_sharded], name="gather_small")
    sm = {k: wts[k] for k in WEIGHTS if k not in BIG and SMALL_AXIS[k] is None}
    for k, g4 in zip(small_sharded, gathered):
        sm[k] = jnp.concatenate([g4[t] for t in range(N_CHIPS)], axis=SMALL_AXIS[k])
    pool_w_bf = sm["od_pool_w"][0].astype(BF16)

    x0 = x[0]
    x0b = _cast_bf16(x, name="cast_x")[0]
    h0, got = _matmul(x0b, full["ev_w_in0"], mode="nn", b_lead=0, b_split=True, out_dtype=BF16, name="ev_in",
                      tm=1024, tn=1280, hosted=_host_gather([bufs["ev_w_out0"]]))
    gathered_now(("ev_w_out0",), got)
    o_a, tot, *rest = _attn_fwd(h0, n_heads, name="attn_fwd", gather=[bufs[k] for k in under_attn])
    gathered_now(under_attn, rest)
    u1, u3 = _evenconv_fwd(h0, sm["ev_dw_w"][0], sm["ev_dw_b"], sm["ev_bn_g"], sm["ev_bn_b"], name="evconv_fwd")
    mix0 = jnp.concatenate([o_a, u3], axis=1)
    y1 = _matmul(mix0, full["ev_w_out0"], mode="nn", b_lead=0, out_dtype=F32, name="ev_out", tm=1024, tn=1024)
    x1, x1b, xh1, rs1 = _ln_fwd(x0, y1, sm["ln_g"][0, 0][None], sm["ln_b"][0, 0][None], alpha, name="ln00")
    hu0, z0, y2, got_up, got_down = _ffn_fwd(
        x1b, full["ffn_w_up0"], full["ffn_w_down0"], sm["ffn_conv_w"][0], sm["ffn_conv_b"][0][None], "ffn0",
        host_up=_host_gather([bufs["ffn_w_up1"]]), host_down=_host_gather([bufs["ffn_w_down1"]]))
    gathered_now(("ffn_w_up1",), got_up)
    gathered_now(("ffn_w_down1",), got_down)
    x2, x2b, xh2, rs2 = _ln_fwd(x1, y2, sm["ln_g"][0, 1][None], sm["ln_b"][0, 1][None], alpha, name="ln01")
    h1 = _matmul(x2b, full["od_w_in0"], mode="nn", b_lead=0, b_split=True, out_dtype=BF16, name="od_in",
                 tm=1024, tn=1024)
    mix1 = _odd_fwd(h1, sm["od_conv_w"][0], pool_w_bf, sm["od_pool_scale"], name="odd_fwd")
    y3 = _matmul(mix1, full["od_w_out0"], mode="nn", b_lead=0, out_dtype=F32, name="od_out", tm=1024, tn=1024)
    x3, x3b, xh3, rs3 = _ln_fwd(x2, y3, sm["ln_g"][1, 0][None], sm["ln_b"][1, 0][None], alpha, name="ln10")
    hu1, z1, y4, _, _ = _ffn_fwd(x3b, full["ffn_w_up1"], full["ffn_w_down1"], sm["ffn_conv_w"][1],
                                 sm["ffn_conv_b"][1][None], "ffn1")

    dr4, dr4b, dg11, db11, loss_part = _ln_loss_bwd(x3, y4, sm["ln_g"][1, 1][None], sm["ln_b"][1, 1][None],
                                                    loss_target[0], alpha, name="ln11_loss")
    loss = lax.psum(loss_part[0, 0], ("x", "y", "c"))

    def halves_view(named):
        g4 = []
        for k, g in named:
            rows, cols = (g.shape[1], g.shape[2]) if BIG_KIND[k] == "col" else (g.shape[0] // N_CHIPS, g.shape[1])
            g4.append(g.reshape(N_CHIPS, 2, rows // 2, cols))
        return g4

    def pair_sums(g4, sib, tag):
        return [_pair_sum(pos, g, r, name=f"grad_pair_sum_{tag}{t}") for t, (g, r) in enumerate(zip(g4, sib))]

    def pair_reduce(named, tag):
        g4 = halves_view(named)
        return pair_sums(g4, _pair_exchange(g4, name=f"grad_pair_exchange_{tag}"), tag)

    dx3, g_up1, g_down1, dcw1, dcb1, _, _ = _ffn_bwd(dr4b, dr4, alpha, x3b, hu1, z1, full["ffn_w_up1"],
                                                     full["ffn_w_down1"], sm["ffn_conv_w"][1],
                                                     sm["ffn_conv_b"][1][None], "ffn1")
    dr3, dr3b, dg10, db10 = _ln_bwd(dx3, xh3, rs3, sm["ln_g"][1, 0][None], name="ln10_bwd")
    g4_f1 = halves_view([("ffn_w_up", g_up1), ("ffn_w_down", g_down1)])
    g_odout, sib_f1 = _matmul(mix1, dr3b, mode="tn", out_dtype=BF16, name="od_dwout", tm=512, tn=1024,
                              hosted=_host_pair_exchange(g4_f1))
    parts_f1 = pair_sums(g4_f1, sib_f1, "f1")
    dmix1 = _matmul(dr3b, full["od_w_out0"], mode="nt", b_lead=0, out_dtype=BF16, name="od_dmix", tm=1024, tn=1024)
    dh1, d_odconv, d_pool, d_pscale = _odd_bwd(dmix1, h1, sm["od_conv_w"][0], pool_w_bf, sm["od_pool_scale"],
                                               name="odd_bwd")
    g_odin = _matmul(x2b, dh1, mode="tn", out_split=True, out_dtype=BF16, name="od_dwin", tm=512, tn=1024)
    dx2 = _matmul(dh1, full["od_w_in0"], mode="nt", b_lead=0, b_split=True, out_dtype=F32, add=dr3, add_scale=alpha,
                  name="od_dx", tm=1024, tn=512)
    dr2, dr2b, dg01, db01 = _ln_bwd(dx2, xh2, rs2, sm["ln_g"][0, 1][None], name="ln01_bwd")
    dx1, g_up0, g_down0, dcw0, dcb0, land_up1, land_down1 = _ffn_bwd(
        dr2b, dr2, alpha, x1b, hu0, z0, full["ffn_w_up0"], full["ffn_w_down0"], sm["ffn_conv_w"][0],
        sm["ffn_conv_b"][0][None], "ffn0",
        host_dwup=_host_exchange(parts_f1[:1]), host_dx=_host_exchange(parts_f1[1:]))
    dr1, dr1b, dg00, db00 = _ln_bwd(dx1, xh1, rs1, sm["ln_g"][0, 0][None], name="ln00_bwd")
    g4_b = halves_view([("od_w_in", g_odin), ("od_w_out", g_odout), ("ffn_w_up", g_up0), ("ffn_w_down", g_down0)])
    g_evout, sib_b = _matmul(mix0, dr1b, mode="tn", out_dtype=BF16, name="ev_dwout", tm=512, tn=1024,
                             hosted=_host_pair_exchange(g4_b))
    g4_o = halves_view([("ev_w_out", g_evout)])
    dmix0, sib_o = _matmul(dr1b, full["ev_w_out0"], mode="nt", b_lead=0, out_dtype=BF16, name="ev_dmix",
                           tm=1024, tn=1024, hosted=_host_pair_exchange(g4_o))
    da, dgate, d_dww, d_dwb, d_bng, d_bnb = _evenconv_bwd(dmix0, u1, h0, sm["ev_dw_w"][0], sm["ev_bn_g"],
                                                          sm["ev_bn_b"], name="evconv_bwd")
    parts_b = pair_sums(g4_b + g4_o, list(sib_b) + list(sib_o), "b")

    d_ln_g = jnp.stack([jnp.stack([dg00[0], dg01[0]]), jnp.stack([dg10[0], dg11[0]])])
    d_ln_b = jnp.stack([jnp.stack([db00[0], db01[0]]), jnp.stack([db10[0], db11[0]])])
    small_partial = {
        "ev_dw_w": d_dww[None], "ev_dw_b": d_dwb, "ev_bn_g": d_bng, "ev_bn_b": d_bnb,
        "od_conv_w": d_odconv[None], "od_pool_w": d_pool[None], "od_pool_scale": d_pscale,
        "ffn_conv_w": jnp.stack([dcw0, dcw1]), "ffn_conv_b": jnp.concatenate([dcb0, dcb1], axis=0),
        "ln_g": d_ln_g, "ln_b": d_ln_b}
    small_names = [k for k in WEIGHTS if k not in BIG]
    packed = _pack([small_partial[k] for k in small_names])
    dq, dk, dv, under_bwd = _attn_bwd(h0, dmix0, tot, n_heads, name="attn_bwd",
                                      hosted=_host_join(_host_exchange(parts_b), _host_all_devices(packed)))
    land_b, all_small = under_bwd[:-1], under_bwd[-1]
    dh0 = jnp.concatenate([dq, dk, dv, da, dgate], axis=1)
    g_evin = _matmul(x0b, dh0, mode="tn", out_split=True, out_dtype=BF16, name="ev_dwin", tm=512, tn=1280)
    parts_e = pair_reduce([("ev_w_in", g_evin)], "e")
    grad_x, land_e = _matmul(dh0, full["ev_w_in0"], mode="nt", b_lead=0, b_split=True, out_dtype=F32, add=dr1,
                             add_scale=alpha, name="ev_dx", tm=1024, tn=512, hosted=_host_exchange(parts_e))

    order = ["ffn_w_up1", "ffn_w_down1", "od_w_in0", "od_w_out0", "ffn_w_up0", "ffn_w_down0", "ev_w_out0", "ev_w_in0"]
    parts = parts_f1 + parts_b + parts_e
    land = list(land_up1) + list(land_down1) + list(land_b) + list(land_e)
    by_tag = {tag: (p, ld) for tag, p, ld in zip(order, parts, land)}
    chain = ["ffn_w_up1", "ffn_w_up0", "ffn_w_down1", "ffn_w_down0", "ev_w_in0", "od_w_in0", "od_w_out0", "ev_w_out0"]
    reduced, waiting = {}, None
    for tag in chain:
        p, ld = by_tag[tag]
        host = None if waiting is None else _host_half_swap([waiting[1]])
        half, swapped = _chip_sum(pos, p, ld, name=f"grad_chip_sum_{tag}", hosted=host)
        if waiting is not None:
            reduced[waiting[0]] = swapped[0]
        waiting = (tag, half)
    reduced[waiting[0]] = _half_swap([waiting[1]], name="grad_half_swap")[0]
    big_grads = {k: [reduced[f"{k}{l}"].reshape(wts[k].shape[1:]) for l in range(wts[k].shape[0])] for k in BIG}

    summed = _sum_slots(all_small, name="sum_small_grads")
    small_full = dict(zip(small_names, _unpack(summed, [small_partial[k].shape for k in small_names])))
    small_grads = {}
    for k in small_names:
        ax = SMALL_AXIS[k]
        if ax is None:
            small_grads[k] = small_full[k]
        else:
            size = wts[k].shape[ax]
            small_grads[k] = lax.dynamic_slice_in_dim(small_full[k], chip * size, size, axis=ax)

    grads, delta, new_m, new_v = {}, {}, {}, {}
    for k in BIG:
        grads[k], delta[k], new_m[k], new_v[k] = _adamw(wts[k], big_grads[k], mom[k], var[k], name=f"adamw_{k}")
    shapes = [wts[k].shape for k in small_names]
    pw, pg, pm, pv = (_pack([d[k] for k in small_names]) for d in (wts, small_grads, mom, var))
    sg, sd, smn, svn = _adamw(pw[None], [pg], pm[None], pv[None], name="adamw_small")
    for dst, buf in ((grads, sg), (delta, sd), (new_m, smn), (new_v, svn)):
        for k, a in zip(small_names, _unpack(buf[0], shapes)):
            dst[k] = a

    return (loss, grad_x[None], *[grads[k] for k in WEIGHTS], *[delta[k] for k in WEIGHTS],
            *[new_m[k] for k in WEIGHTS], *[new_v[k] for k in WEIGHTS])
```

```python
import collections

import jax
import jax.numpy as jnp
from jax import lax
from jax.experimental import pallas as pl
from jax.experimental.pallas import tpu as pltpu

F32 = jnp.float32
BF16 = jnp.bfloat16

HEAD_DIM = 128
POOL_WINDOWS = (2, 4, 8, 16)
LN_EPS = 1e-5
ADAM_LR = 0.001
ADAM_B1 = 0.9
ADAM_B2 = 0.999
ADAM_EPS = 1e-08
ADAM_WD = 0.01
ADAM_STEP = 10
N_CHIPS = 4
N_DEV = 8
MESH = pl.DeviceIdType.MESH
LANES = 128
HALO3 = 16
HALO31 = 32
ROW_CHUNK = 32

ANY = pl.BlockSpec(memory_space=pl.ANY)


def _pick(n, pref, mult=LANES):
    if n <= pref:
        return n
    t = (pref // mult) * mult
    while t >= mult:
        if n % t == 0:
            return t
        t -= mult
    return n


def _params(*sem):
    return pltpu.CompilerParams(dimension_semantics=sem)


def _matmul(a, b, *, mode, out_dtype, name, b_lead=None, b_split=False, out_split=False, add=None,
            add_scale=1.0, tm=512, tn=1024, tk=None, hosted=None):
    halves = isinstance(a, tuple) or isinstance(b, tuple)
    if isinstance(a, tuple):
        assert mode == "nt" and b_split and tk is None
        ash = (a[0].shape[0], 2 * a[0].shape[1])
    else:
        ash = a.shape[-2:]
    if isinstance(b, tuple):
        assert mode == "tn" and tk is None
        bsh = (b[0].shape[0], 2 * b[0].shape[1])
    else:
        bsh = b.shape[-2:]
    if mode == "nn":
        (M, K), (K2, N) = ash, bsh
        if b_split:
            N = N * N_CHIPS
    elif mode == "nt":
        (M, K), (N, K2) = ash, bsh
        if b_split:
            K2 = K2 * N_CHIPS
    else:
        (K, M), (K2, N) = ash, bsh
    assert K == K2, (ash, bsh, mode)
    tm = _pick(M, tm)
    tn = _pick(N // N_CHIPS if (out_split or (b_split and mode == "nn")) else N, tn)
    whole_split_k = b_split and mode == "nt" and tk is None
    if tk is None:
        tk = K
    else:
        tk = _pick(K // N_CHIPS if (b_split and mode == "nt") else K, tk)
    nk = K // tk
    kq = K // N_CHIPS
    n_per = (N // N_CHIPS) // tn
    k_per = (K // N_CHIPS) // tk

    def lead(shape, idx):
        if b_lead is None:
            return pl.BlockSpec(shape, idx)
        return pl.BlockSpec((None,) + shape, lambda i, j, k: (b_lead,) + idx(i, j, k))

    if mode == "nn":
        a_spec = pl.BlockSpec((tm, tk), lambda i, j, k: (i, k))
        if b_split:
            b_spec = lead((None, tk, tn), lambda i, j, k: (lax.div(j, n_per), k, lax.rem(j, n_per)))
        else:
            b_spec = lead((tk, tn), lambda i, j, k: (k, j))
        dims = (((1,), (0,)), ((), ()))
    elif mode == "nt":
        a_spec = pl.BlockSpec((tm, tk), lambda i, j, k: (i, k))
        if whole_split_k:
            b_spec = lead((N_CHIPS, tn, kq), lambda i, j, k: (0, j, 0))
        elif b_split:
            b_spec = lead((None, tn, tk), lambda i, j, k: (lax.div(k, k_per), j, lax.rem(k, k_per)))
        else:
            b_spec = lead((tn, tk), lambda i, j, k: (j, k))
        dims = (((1,), (1,)), ((), ()))
    else:
        a_spec = pl.BlockSpec((tk, tm), lambda i, j, k: (k, i))
        b_spec = pl.BlockSpec((tk, tn), lambda i, j, k: (k, j))
        dims = (((0,), (0,)), ((), ()))
    if out_split:
        out_shape = jax.ShapeDtypeStruct((N_CHIPS, M, N // N_CHIPS), out_dtype)
        out_spec = pl.BlockSpec((None, tm, tn), lambda i, j, k: (lax.div(j, n_per), i, lax.rem(j, n_per)))
    else:
        out_shape = jax.ShapeDtypeStruct((M, N), out_dtype)
        out_spec = pl.BlockSpec((tm, tn), lambda i, j, k: (i, j))
    grid = (M // tm, N // tn, nk)
    nj_half = grid[1] // 2
    if isinstance(a, tuple):
        in_specs = [pl.BlockSpec((tm, K // 2), lambda i, j, k: (i, 0))] * 2 + [b_spec]
        args = [a[0], a[1], b]
    elif isinstance(b, tuple):
        in_specs = [a_spec,
                    pl.BlockSpec((tk, tn), lambda i, j, k: (k, jnp.minimum(j, nj_half - 1))),
                    pl.BlockSpec((tk, tn), lambda i, j, k: (k, jnp.maximum(j - nj_half, 0)))]
        args = [a, b[0], b[1]]
    else:
        in_specs = [a_spec, b_spec]
        args = [a, b]
    n_op = len(args)
    if add is not None:
        in_specs.append(pl.BlockSpec((tm, tn), lambda i, j, k: (i, j)))
        args.append(add)

    n_in = len(args)
    h_in = 0 if hosted is None else len(hosted.ins)
    h_out = 0 if hosted is None else len(hosted.outs)

    def body(*refs):
        ops = refs[:n_op]
        add_ref = refs[n_op] if add is not None else None
        h_ins = refs[n_in:n_in + h_in]
        o_ref = refs[n_in + h_in]
        h_outs = refs[n_in + h_in + 1:n_in + h_in + 1 + h_out]
        scr = refs[n_in + h_in + 1 + h_out:]
        i, j, k = pl.program_id(0), pl.program_id(1), pl.program_id(2)
        if hosted is not None:
            sems = scr[len(scr) - len(hosted.sems):]

            @pl.when((i == 0) & (j == 0) & (k == 0))
            def _():
                hosted.start(h_ins, h_outs, sems)

        def finish(res):
            if add_ref is not None:
                res = res + add_scale * add_ref[...]
            o_ref[...] = res.astype(out_dtype)

        def dot(x, y):
            return lax.dot_general(x, y, dims, preferred_element_type=F32)

        if isinstance(b, tuple):
            @pl.when(j < nj_half)
            def _():
                finish(dot(ops[0][...], ops[1][...]))

            @pl.when(j >= nj_half)
            def _():
                finish(dot(ops[0][...], ops[2][...]))
            part = None
        elif whole_split_k:
            srcs = [(ops[0], s) for s in range(N_CHIPS)] if not isinstance(a, tuple) else \
                   [(ops[s // 2], s % 2) for s in range(N_CHIPS)]
            b_ref = ops[-1]
            part = None
            for s, (src, off) in enumerate(srcs):
                term = dot(src[:, off * kq:(off + 1) * kq], b_ref[s])
                part = term if part is None else part + term
        else:
            part = dot(ops[0][...], ops[1][...])

        if part is None:
            pass
        elif nk == 1:
            finish(part)
        else:
            acc = scr[0]

            @pl.when(k == 0)
            def _():
                acc[...] = part

            @pl.when(k > 0)
            def _():
                acc[...] += part

            @pl.when(k == nk - 1)
            def _():
                finish(acc[...])

        if hosted is not None:
            @pl.when((i == grid[0] - 1) & (j == grid[1] - 1) & (k == nk - 1))
            def _():
                hosted.finish(h_ins, h_outs, sems)

    scratch = [pltpu.VMEM((tm, tn), F32)] if nk > 1 else []
    if hosted is not None:
        res = pl.pallas_call(
            body, name=name,
            out_shape=(out_shape,) + tuple(hosted.outs),
            grid=grid,
            in_specs=in_specs + [ANY] * h_in,
            out_specs=(out_spec,) + (ANY,) * h_out,
            input_output_aliases={n_in + src: 1 + dst for src, dst in hosted.alias.items()},
            scratch_shapes=scratch + [pltpu.SemaphoreType.DMA((n,)) for n in hosted.sems],
            compiler_params=_params("arbitrary", "arbitrary", "arbitrary"),
        )(*args, *hosted.ins)
        return res[0], list(res[1:])
    return pl.pallas_call(
        body, name=name,
        out_shape=out_shape,
        grid=grid,
        in_specs=in_specs,
        out_specs=out_spec,
        scratch_shapes=scratch,
        compiler_params=_params("parallel", "parallel", "arbitrary"),
    )(*args)


def _cast_bf16(w, name):
    L, R, C = w.shape
    tr, tc = _pick(R, 512, 16), _pick(C, 1408)

    def body(w_ref, o_ref):
        o_ref[...] = w_ref[...].astype(BF16)

    return pl.pallas_call(
        body, name=name, out_shape=jax.ShapeDtypeStruct(w.shape, BF16),
        grid=(L, R // tr, C // tc),
        in_specs=[pl.BlockSpec((None, tr, tc), lambda l, i, j: (l, i, j))],
        out_specs=pl.BlockSpec((None, tr, tc), lambda l, i, j: (l, i, j)),
        compiler_params=_params("parallel", "parallel", "parallel"),
    )(w)


def _cast_into_gather(pos, w, layer, name):
    L, R, C = w.shape
    r2 = R // 2
    tr, tc = _pick(r2, 512, 16), _pick(C, 1408)

    def body(p_ref, w_ref, o_ref):
        o_ref[...] = w_ref[...].astype(BF16)

    return pl.pallas_call(
        body, name=name, out_shape=jax.ShapeDtypeStruct((1, N_CHIPS, 2, r2, C), BF16),
        grid_spec=pltpu.PrefetchScalarGridSpec(
            num_scalar_prefetch=1, grid=(2, r2 // tr, C // tc),
            in_specs=[pl.BlockSpec((None, None, tr, tc), lambda h, i, j, p: (layer, h, i, j))],
            out_specs=pl.BlockSpec((None, None, None, tr, tc), lambda h, i, j, p: (0, p[0], h, i, j))),
        compiler_params=_params("parallel", "parallel", "parallel"),
    )(pos, w.reshape(L, 2, r2, C))


def _sigmoid(v):
    return 0.5 * jnp.tanh(0.5 * v) + 0.5


def _ln_fwd(x, y, g, b, alpha, name):
    S, D = x.shape
    tr = _pick(S, 256, 8)

    def body(x_ref, y_ref, g_ref, b_ref, o_ref, ob_ref, xh_ref, rs_ref):
        r = alpha * x_ref[...] + y_ref[...]
        mu = jnp.mean(r, axis=-1, keepdims=True)
        d = r - mu
        var = jnp.mean(d * d, axis=-1, keepdims=True)
        rstd = lax.rsqrt(var + LN_EPS)
        xh = d * rstd
        o = xh * g_ref[...] + b_ref[...]
        o_ref[...] = o
        ob_ref[...] = o.astype(BF16)
        xh_ref[...] = xh
        rs_ref[...] = rstd

    row = pl.BlockSpec((tr, D), lambda i: (i, 0))
    vec = pl.BlockSpec((1, D), lambda i: (0, 0))
    return pl.pallas_call(
        body, name=name,
        out_shape=(jax.ShapeDtypeStruct((S, D), F32), jax.ShapeDtypeStruct((S, D), BF16),
                   jax.ShapeDtypeStruct((S, D), F32), jax.ShapeDtypeStruct((S, 1), F32)),
        grid=(S // tr,),
        in_specs=[row, row, vec, vec],
        out_specs=(row, row, row, pl.BlockSpec((tr, 1), lambda i: (i, 0))),
        compiler_params=_params("parallel"),
    )(x, y, g, b)


def _ln_bwd(dout, xhat, rstd, g, name):
    S, D = dout.shape
    tr = _pick(S, 256, 8)

    def body(do_ref, xh_ref, rs_ref, g_ref, dr_ref, drb_ref, dg_ref, db_ref):
        i = pl.program_id(0)
        do = do_ref[...]
        xh = xh_ref[...]
        dxh = do * g_ref[...]
        m1 = jnp.mean(dxh, axis=-1, keepdims=True)
        m2 = jnp.mean(dxh * xh, axis=-1, keepdims=True)
        dr = rs_ref[...] * (dxh - m1 - xh * m2)
        dr_ref[...] = dr
        drb_ref[...] = dr.astype(BF16)
        pg = jnp.sum(do * xh, axis=0, keepdims=True)
        pb = jnp.sum(do, axis=0, keepdims=True)

        @pl.when(i == 0)
        def _():
            dg_ref[...] = pg
            db_ref[...] = pb

        @pl.when(i > 0)
        def _():
            dg_ref[...] += pg
            db_ref[...] += pb

    row = pl.BlockSpec((tr, D), lambda i: (i, 0))
    vec = pl.BlockSpec((1, D), lambda i: (0, 0))
    return pl.pallas_call(
        body, name=name,
        out_shape=(jax.ShapeDtypeStruct((S, D), F32), jax.ShapeDtypeStruct((S, D), BF16),
                   jax.ShapeDtypeStruct((1, D), F32), jax.ShapeDtypeStruct((1, D), F32)),
        grid=(S // tr,),
        in_specs=[row, row, pl.BlockSpec((tr, 1), lambda i: (i, 0)), vec],
        out_specs=(row, row, vec, vec),
        compiler_params=_params("arbitrary"),
    )(dout, xhat, rstd, g)


def _ln_loss_bwd(x, y, g, b, target, alpha, name):
    S, D = x.shape
    tr = _pick(S, 256, 8)
    n = S // tr

    def body(x_ref, y_ref, g_ref, b_ref, t_ref, dr_ref, drb_ref, dg_ref, db_ref, l_ref, acc):
        i = pl.program_id(0)
        r = alpha * x_ref[...] + y_ref[...]
        mu = jnp.mean(r, axis=-1, keepdims=True)
        d = r - mu
        var = jnp.mean(d * d, axis=-1, keepdims=True)
        rstd = lax.rsqrt(var + LN_EPS)
        xh = d * rstd
        err = xh * g_ref[...] + b_ref[...] - t_ref[...]
        do = err * (1.0 / D)
        dxh = do * g_ref[...]
        m1 = jnp.mean(dxh, axis=-1, keepdims=True)
        m2 = jnp.mean(dxh * xh, axis=-1, keepdims=True)
        dr = rstd * (dxh - m1 - xh * m2)
        dr_ref[...] = dr
        drb_ref[...] = dr.astype(BF16)
        pg = jnp.sum(do * xh, axis=0, keepdims=True)
        pb = jnp.sum(do, axis=0, keepdims=True)
        pe = jnp.sum(err * err, axis=0, keepdims=True)

        @pl.when(i == 0)
        def _():
            dg_ref[...] = pg
            db_ref[...] = pb
            acc[...] = pe

        @pl.when(i > 0)
        def _():
            dg_ref[...] += pg
            db_ref[...] += pb
            acc[...] += pe

        @pl.when(i == n - 1)
        def _():
            l_ref[...] = (0.5 / D) * jnp.sum(acc[...], axis=1, keepdims=True)

    row = pl.BlockSpec((tr, D), lambda i: (i, 0))
    vec = pl.BlockSpec((1, D), lambda i: (0, 0))
    return pl.pallas_call(
        body, name=name,
        out_shape=(jax.ShapeDtypeStruct((S, D), F32), jax.ShapeDtypeStruct((S, D), BF16),
                   jax.ShapeDtypeStruct((1, D), F32), jax.ShapeDtypeStruct((1, D), F32),
                   jax.ShapeDtypeStruct((1, 1), F32)),
        grid=(n,),
        in_specs=[row, row, vec, vec, row],
        out_specs=(row, row, vec, vec, pl.BlockSpec((1, 1), lambda i: (0, 0))),
        scratch_shapes=[pltpu.VMEM((1, D), F32)],
        compiler_params=_params("arbitrary"),
    )(x, y, g, b, target)


def _prev_spec(tr, halo, width, col):
    return pl.BlockSpec((halo, width), lambda c, i: (jnp.maximum(i * (tr // halo) - 1, 0), col(c)))


def _next_spec(tr, halo, width, col, nrows):
    last = nrows // halo - 1
    return pl.BlockSpec((halo, width), lambda c, i: (jnp.minimum((i + 1) * (tr // halo), last), col(c)))


def _cur_spec(tr, width, col):
    return pl.BlockSpec((tr, width), lambda c, i: (i, col(c)))


def _ffn_act_fwd(hu, conv_w, conv_b, name):
    S, F2 = hu.shape
    F = F2 // 2
    tr, tc, H = _pick(S, 512, 16), _pick(F, 512), HALO3
    nc, nr = F // tc, S // tr
    rc = min(ROW_CHUNK, tr)

    def body(gp_ref, g_ref, u_ref, w_ref, b_ref, z_ref, G):
        i = pl.program_id(1)
        G[0:H, :] = jnp.where(i > 0, gp_ref[...].astype(F32), 0.0)
        G[H:H + tr, :] = g_ref[...].astype(F32)
        w0, w1, w2, b = w_ref[pl.ds(0, 1), :], w_ref[pl.ds(1, 1), :], w_ref[pl.ds(2, 1), :], b_ref[...]
        for r0 in range(0, tr, rc):
            gc = b + w0 * G[pl.ds(H - 2 + r0, rc), :] + w1 * G[pl.ds(H - 1 + r0, rc), :] + w2 * G[pl.ds(H + r0, rc), :]
            z = gc * _sigmoid(gc) * u_ref[pl.ds(r0, rc), :].astype(F32)
            z_ref[pl.ds(r0, rc), :] = z.astype(BF16)

    gcol = lambda c: c
    ucol = lambda c: c + nc
    return pl.pallas_call(
        body, name=name, out_shape=jax.ShapeDtypeStruct((S, F), BF16),
        grid=(nc, nr),
        in_specs=[_prev_spec(tr, H, tc, gcol), _cur_spec(tr, tc, gcol), _cur_spec(tr, tc, ucol),
                  pl.BlockSpec((3, tc), lambda c, i: (0, c)), pl.BlockSpec((1, tc), lambda c, i: (0, c))],
        out_specs=pl.BlockSpec((tr, tc), lambda c, i: (i, c)),
        scratch_shapes=[pltpu.VMEM((H + tr, tc), F32)],
        compiler_params=_params("parallel", "parallel"),
    )(hu, hu, hu, conv_w, conv_b)


def _ffn_act_bwd(dz, hu, conv_w, conv_b, name):
    S, F = dz.shape
    tr, tc, H = _pick(S, 512, 16), _pick(F, 512), HALO3
    nc, nr = F // tc, S // tr
    n = tr + H
    rc = min(ROW_CHUNK, tr)

    def body(dz_ref, dzn_ref, gp_ref, g_ref, gn_ref, u_ref, un_ref, w_ref, b_ref,
             dg_ref, du_ref, dw_ref, db_ref, G, DG):
        i = pl.program_id(1)
        G[0:H, :] = jnp.where(i > 0, gp_ref[...].astype(F32), 0.0)
        G[H:H + tr, :] = g_ref[...].astype(F32)
        G[H + tr:H + tr + H, :] = gn_ref[...].astype(F32)
        w0, w1, w2, b = w_ref[pl.ds(0, 1), :], w_ref[pl.ds(1, 1), :], w_ref[pl.ds(2, 1), :], b_ref[...]

        def fold(v):
            return jnp.sum(v.reshape(v.shape[0] // 8, 8, tc), axis=0)

        def d_gate(r0, rows, dzf, uf):
            taps = [G[pl.ds(H - 2 + k + r0, rows), :] for k in range(3)]
            gc = b + w0 * taps[0] + w1 * taps[1] + w2 * taps[2]
            sg = _sigmoid(gc)
            return dzf * uf * (sg * (1.0 + gc * (1.0 - sg))), gc * sg, taps

        acc_w = [jnp.zeros((8, tc), F32) for _ in range(3)]
        acc_b = jnp.zeros((8, tc), F32)
        for r0 in range(0, tr, rc):
            dzf = dz_ref[pl.ds(r0, rc), :].astype(F32)
            dgc, silu, taps = d_gate(r0, rc, dzf, u_ref[pl.ds(r0, rc), :].astype(F32))
            du_ref[pl.ds(r0, rc), :] = (dzf * silu).astype(BF16)
            DG[pl.ds(r0, rc), :] = dgc
            acc_w = [acc_w[k] + fold(dgc * taps[k]) for k in range(3)]
            acc_b = acc_b + fold(dgc)
        dzn = jnp.where(i < nr - 1, dzn_ref[...].astype(F32), 0.0)
        DG[pl.ds(tr, H), :] = d_gate(tr, H, dzn, un_ref[...].astype(F32))[0]
        for r0 in range(0, tr, rc):
            dg = w2 * DG[pl.ds(r0, rc), :] + w1 * DG[pl.ds(r0 + 1, rc), :] + w0 * DG[pl.ds(r0 + 2, rc), :]
            dg_ref[pl.ds(r0, rc), :] = dg.astype(BF16)
        pw = [jnp.sum(a, axis=0, keepdims=True) for a in acc_w]
        pb = jnp.sum(acc_b, axis=0, keepdims=True)

        @pl.when(i == 0)
        def _():
            for k in range(3):
                dw_ref[pl.ds(k, 1), :] = pw[k]
            db_ref[...] = pb

        @pl.when(i > 0)
        def _():
            for k in range(3):
                dw_ref[pl.ds(k, 1), :] += pw[k]
            db_ref[...] += pb

    gcol = lambda c: c
    ucol = lambda c: c + nc
    blk = pl.BlockSpec((tr, tc), lambda c, i: (i, c))
    return pl.pallas_call(
        body, name=name,
        out_shape=(jax.ShapeDtypeStruct((S, F), BF16), jax.ShapeDtypeStruct((S, F), BF16),
                   jax.ShapeDtypeStruct((3, F), F32), jax.ShapeDtypeStruct((1, F), F32)),
        grid=(nc, nr),
        in_specs=[_cur_spec(tr, tc, gcol), _next_spec(tr, H, tc, gcol, S),
                  _prev_spec(tr, H, tc, gcol), _cur_spec(tr, tc, gcol), _next_spec(tr, H, tc, gcol, S),
                  _cur_spec(tr, tc, ucol), _next_spec(tr, H, tc, ucol, S),
                  pl.BlockSpec((3, tc), lambda c, i: (0, c)), pl.BlockSpec((1, tc), lambda c, i: (0, c))],
        out_specs=(blk, blk, pl.BlockSpec((3, tc), lambda c, i: (0, c)), pl.BlockSpec((1, tc), lambda c, i: (0, c))),
        scratch_shapes=[pltpu.VMEM((H + tr + H, tc), F32), pltpu.VMEM((n, tc), F32)],
        compiler_params=_params("parallel", "arbitrary"),
    )(dz, dz, hu, hu, hu, hu, hu, conv_w, conv_b)


def _softplus_neg(s):
    return jnp.minimum(-s, 0.0) - jnp.log(1.0 + jnp.exp(-jnp.abs(s)))


def _hilo_dot(v, m):
    hi = v.astype(BF16)
    lo = (v - hi.astype(F32)).astype(BF16)
    return (jnp.dot(hi, m, preferred_element_type=F32) + jnp.dot(lo, m, preferred_element_type=F32))


def _attn_fwd(h, n_heads, name, gather=()):
    S = h.shape[0]
    dh = HEAD_DIM
    A = n_heads * dh
    tq = _pick(S, 256)
    nq = S // tq
    scale = 1.0 / float(dh) ** 0.5
    ng = len(gather)
    hp = 2 if n_heads % 2 == 0 else 1
    n_grp, hw = n_heads // hp, hp * dh

    def body(*refs):
        q_ref, k_ref, v_ref = refs[:3]
        o_ref, tot_ref = refs[3 + ng:5 + ng]
        full = refs[5 + ng:5 + 2 * ng]
        hd = pl.program_id(0)
        i = pl.program_id(1)
        if ng:
            ssem, rsem = refs[5 + 2 * ng:]

            @pl.when((hd == 0) & (i == 0))
            def _():
                _gather_start(full, ssem, rsem)

            @pl.when((hd == n_grp - 1) & (i == 0))
            def _():
                _gather_forward(full, ssem, rsem)

        heads = range(hp)
        qs = [q_ref[:, h * dh:(h + 1) * dh] for h in heads]
        r_io = lax.broadcasted_iota(jnp.int32, (tq, tq), 0)
        c_io = lax.broadcasted_iota(jnp.int32, (tq, tq), 1)
        later = (r_io > c_io).astype(BF16)
        causal = c_io < r_io

        def rows(ref, j):
            blk = ref[pl.ds(pl.multiple_of(j * tq, tq), tq), :]
            return [blk[:, h * dh:(h + 1) * dh] for h in heads]

        def qk(kj):
            return [lax.dot_general(qs[h], kj[h], (((1,), (1,)), ((), ())), preferred_element_type=F32) * scale
                    for h in heads]

        def log_weights(s, diag):
            base, tot = [], []
            for h in heads:
                ls = _softplus_neg(s[h])
                if diag:
                    ls = jnp.where(causal, ls, 0.0)
                cs = _hilo_dot(ls, later)
                b = s[h] + ls + cs
                base.append(jnp.where(causal, b, -1e30) if diag else b)
                tot.append(cs[:, 0:1] + ls[:, 0:1])
            return base, tot

        def weigh(vj, acc, run, base):
            out = []
            for h in heads:
                w = jnp.exp(base[h] + run[h])
                out.append(acc[h] + jnp.dot(w.astype(BF16), vj[h], preferred_element_type=F32))
            return out

        def trip(t, carry):
            acc, run, base, tot = carry
            j = i - 1 - t
            s = qk(rows(k_ref, j))
            acc = weigh(rows(v_ref, j + 1), acc, run, base)
            base_n, tot_n = log_weights(s, False)
            return acc, [run[h] + tot[h] for h in heads], base_n, tot_n

        base, tot = log_weights(qk(rows(k_ref, i)), True)
        carry = ([jnp.zeros((tq, dh), F32) for _ in heads], [jnp.zeros((tq, 1), F32) for _ in heads], base, tot)
        acc, run, base, tot = lax.fori_loop(0, i, trip, carry)
        acc = weigh(rows(v_ref, 0), acc, run, base)
        for h in heads:
            o_ref[:, h * dh:(h + 1) * dh] = acc[h].astype(BF16)
            tot_ref[h] = jnp.broadcast_to(run[h] + tot[h], (tq, LANES))
        if ng:
            @pl.when((hd == n_grp - 1) & (i == nq - 1))
            def _():
                _gather_finish(full, ssem, rsem)

    T = _gather_items(gather) if ng else 0
    return pl.pallas_call(
        body, name=name,
        out_shape=(jax.ShapeDtypeStruct((S, A), BF16), jax.ShapeDtypeStruct((n_heads, S, LANES), F32))
        + tuple(jax.ShapeDtypeStruct(b.shape, b.dtype) for b in gather),
        grid=(n_grp, nq),
        in_specs=[pl.BlockSpec((tq, hw), lambda hd, i: (i, hd)),
                  pl.BlockSpec((S, hw), lambda hd, i: (0, n_grp + hd)),
                  pl.BlockSpec((S, hw), lambda hd, i: (0, 2 * n_grp + hd))] + [ANY] * ng,
        out_specs=(pl.BlockSpec((tq, hw), lambda hd, i: (i, hd)),
                   pl.BlockSpec((hp, tq, LANES), lambda hd, i: (hd, i, 0))) + (ANY,) * ng,
        input_output_aliases={3 + a: 2 + a for a in range(ng)},
        scratch_shapes=[pltpu.SemaphoreType.DMA((6 * T,)), pltpu.SemaphoreType.DMA((6 * T,))] if ng else [],
        compiler_params=_params("arbitrary", "arbitrary") if ng else _params("parallel", "parallel"),
    )(h, h, h, *gather)


def _attn_bwd(h, do, tot, n_heads, name, hosted=None):
    S = h.shape[0]
    dh = HEAD_DIM
    A = n_heads * dh
    tq = _pick(S, 256)
    nq = S // tq
    scale = 1.0 / float(dh) ** 0.5
    nt_dims = (((1,), (1,)), ((), ()))
    tn_dims = (((0,), (0,)), ((), ()))
    hp = 2 if n_heads % 2 == 0 else 1
    n_grp, hw = n_heads // hp, hp * dh
    h_in = 0 if hosted is None else len(hosted.ins)
    h_out = 0 if hosted is None else len(hosted.outs)

    def body(*refs):
        q_ref, k_ref, v_ref, do_ref, tot_ref = refs[:5]
        h_ins = refs[5:5 + h_in]
        dq_ref, dk_ref, dv_ref = refs[5 + h_in:8 + h_in]
        h_outs = refs[8 + h_in:8 + h_in + h_out]
        dk_acc, dv_acc = refs[8 + h_in + h_out:10 + h_in + h_out]
        sems = refs[10 + h_in + h_out:]
        hd = pl.program_id(0)
        i = pl.program_id(1)
        if hosted is not None:
            @pl.when((hd == 0) & (i == 0))
            def _():
                hosted.start(h_ins, h_outs, sems)

        @pl.when(i == 0)
        def _():
            dk_acc[...] = jnp.zeros_like(dk_acc)
            dv_acc[...] = jnp.zeros_like(dv_acc)

        heads = range(hp)
        qs = [q_ref[:, h * dh:(h + 1) * dh] for h in heads]
        dos = [do_ref[:, h * dh:(h + 1) * dh] for h in heads]
        total = [tot_ref[h][:, 0:1] for h in heads]
        r_io = lax.broadcasted_iota(jnp.int32, (tq, tq), 0)
        c_io = lax.broadcasted_iota(jnp.int32, (tq, tq), 1)
        upto = (r_io <= c_io).astype(BF16)
        before = (r_io < c_io).astype(BF16)
        causal = c_io < r_io

        def rows(ref, j):
            blk = ref[pl.ds(pl.multiple_of(j * tq, tq), tq), :]
            return [blk[:, h * dh:(h + 1) * dh] for h in heads]

        def qk(kj):
            return [lax.dot_general(qs[h], kj[h], nt_dims, preferred_element_type=F32) * scale for h in heads]

        def weights(base, prun, vj):
            dw = [lax.dot_general(dos[h], vj[h], nt_dims, preferred_element_type=F32) for h in heads]
            w, e, ce = [], [], []
            for h in heads:
                w.append(jnp.exp(base[h] + (total[h] - prun[h])))
                e.append(dw[h] * w[h])
                ce.append(jnp.dot(e[h].astype(BF16), before, preferred_element_type=F32))
            return w, e, ce

        def prefix(s, j):
            keep = jnp.logical_or(causal, j != i)
            ls = [jnp.where(keep, _softplus_neg(s[h]), 0.0) for h in heads]
            return keep, ls, [_hilo_dot(ls[h], upto) for h in heads]

        def grads(j, kj, dq, erun, w, e, ce, sn):
            start = pl.multiple_of(j * tq, tq)
            out = []
            for h in heads:
                ecum = ce[h] + erun[h]
                dz = e[h] * sn[h] - (1.0 - sn[h]) * ecum
                ds = (dz * scale).astype(BF16)
                cols = slice(h * dh, (h + 1) * dh)
                dv_acc[pl.ds(start, tq), cols] += lax.dot_general(w[h].astype(BF16), dos[h], tn_dims,
                                                                  preferred_element_type=F32)
                out.append(dq[h] + jnp.dot(ds, kj[h], preferred_element_type=F32))
                dk_acc[pl.ds(start, tq), cols] += lax.dot_general(ds, qs[h], tn_dims, preferred_element_type=F32)
            return out, [erun[h] + ce[h][:, tq - 1:tq] + e[h][:, tq - 1:tq] for h in heads]

        def carried(s, keep, ls, cs):
            base = [jnp.where(keep, s[h] + ls[h] - cs[h], -1e30) for h in heads]
            return base, [jnp.exp(ls[h]) for h in heads], [cs[h][:, tq - 1:tq] for h in heads]

        def trip(j, carry):
            dq, prun, erun, base, sn, ptot = carry
            s_n = qk(rows(k_ref, j + 1))
            w, e, ce = weights(base, prun, rows(v_ref, j))
            keep, ls_n, cs = prefix(s_n, j + 1)
            dq, erun = grads(j, rows(k_ref, j), dq, erun, w, e, ce, sn)
            return (dq, [prun[h] + ptot[h] for h in heads], erun) + carried(s_n, keep, ls_n, cs)

        zeros = [jnp.zeros((tq, 1), F32) for _ in heads]
        s0 = qk(rows(k_ref, 0))
        first = carried(s0, *prefix(s0, 0))
        carry = lax.fori_loop(0, i, trip, ([jnp.zeros((tq, dh), F32) for _ in heads], zeros, zeros) + first)
        dq, prun, erun, base, sn, _ = carry
        dq, _ = grads(i, rows(k_ref, i), dq, erun, *weights(base, prun, rows(v_ref, i)), sn)
        for h in heads:
            dq_ref[:, h * dh:(h + 1) * dh] = dq[h].astype(BF16)

        @pl.when(i == nq - 1)
        def _():
            dk_ref[...] = dk_acc[...].astype(BF16)
            dv_ref[...] = dv_acc[...].astype(BF16)

        if hosted is not None:
            @pl.when((hd == n_grp - 1) & (i == nq - 1))
            def _():
                hosted.finish(h_ins, h_outs, sems)

    qblk = pl.BlockSpec((tq, hw), lambda hd, i: (i, hd))
    full = pl.BlockSpec((S, hw), lambda hd, i: (0, hd))
    scratch = [pltpu.VMEM((S, hw), F32), pltpu.VMEM((S, hw), F32)]
    if hosted is not None:
        scratch += [pltpu.SemaphoreType.DMA((n,)) for n in hosted.sems]
    res = pl.pallas_call(
        body, name=name,
        out_shape=tuple(jax.ShapeDtypeStruct((S, A), BF16) for _ in range(3))
        + (tuple(hosted.outs) if hosted is not None else ()),
        grid=(n_grp, nq),
        in_specs=[qblk,
                  pl.BlockSpec((S, hw), lambda hd, i: (0, n_grp + hd)),
                  pl.BlockSpec((S, hw), lambda hd, i: (0, 2 * n_grp + hd)),
                  qblk,
                  pl.BlockSpec((hp, tq, LANES), lambda hd, i: (hd, i, 0))] + [ANY] * h_in,
        out_specs=(qblk, full, full) + (ANY,) * h_out,
        input_output_aliases={} if hosted is None else {5 + a: 3 + b for a, b in hosted.alias.items()},
        scratch_shapes=scratch,
        compiler_params=_params("arbitrary", "arbitrary") if hosted is not None else _params("parallel", "arbitrary"),
    )(h, h, h, do, tot, *(hosted.ins if hosted is not None else ()))
    return res[0], res[1], res[2], list(res[3:])


def _evenconv_fwd(h, dw_w, dw_b, bn_g, bn_b, name):
    S = h.shape[0]
    KW, A = dw_w.shape
    H = HALO31
    tr = _pick(S, 256, H)
    first_tap = H - (KW - 1)

    def body(ap_ref, a_ref, gp_ref, g_ref, w_ref, b_ref, bg_ref, bb_ref, u1_ref, u3_ref, U):
        i = pl.program_id(1)
        glu_prev = ap_ref[...].astype(F32) * _sigmoid(gp_ref[...].astype(F32))
        U[0:H, :] = jnp.where(i > 0, glu_prev, 0.0)
        U[H:H + tr, :] = a_ref[...].astype(F32) * _sigmoid(g_ref[...].astype(F32))
        acc = b_ref[...] + w_ref[pl.ds(0, 1), :] * U[pl.ds(first_tap, tr), :]
        for k in range(1, KW):
            acc = acc + w_ref[pl.ds(k, 1), :] * U[pl.ds(first_tap + k, tr), :]
        u1_ref[...] = acc
        mu = jnp.mean(acc, axis=-1, keepdims=True)
        d = acc - mu
        var = jnp.mean(d * d, axis=-1, keepdims=True)
        u2 = d * lax.rsqrt(var + LN_EPS) * bg_ref[...] + bb_ref[...]
        u3_ref[...] = (u2 * _sigmoid(u2)).astype(BF16)

    acol = lambda c: 3
    gcol = lambda c: 4
    vec = pl.BlockSpec((1, A), lambda c, i: (0, 0))
    blk = pl.BlockSpec((tr, A), lambda c, i: (i, 0))
    return pl.pallas_call(
        body, name=name,
        out_shape=(jax.ShapeDtypeStruct((S, A), F32), jax.ShapeDtypeStruct((S, A), BF16)),
        grid=(1, S // tr),
        in_specs=[_prev_spec(tr, H, A, acol), _cur_spec(tr, A, acol),
                  _prev_spec(tr, H, A, gcol), _cur_spec(tr, A, gcol),
                  pl.BlockSpec((KW, A), lambda c, i: (0, 0)), vec, vec, vec],
        out_specs=(blk, blk),
        scratch_shapes=[pltpu.VMEM((H + tr, A), F32)],
        compiler_params=_params("parallel", "parallel"),
    )(h, h, h, h, dw_w, dw_b, bn_g, bn_b)


def _evenconv_bwd(du3, u1, h, dw_w, bn_g, bn_b, name):
    S = h.shape[0]
    KW, A = dw_w.shape
    H = HALO31
    tr = _pick(S, 256, H)
    nr = S // tr
    n = tr + H
    first_tap = H - (KW - 1)

    def body(d3_ref, d3n_ref, u1_ref, u1n_ref, ap_ref, a_ref, gp_ref, g_ref, w_ref, bg_ref, bb_ref,
             da_ref, dg_ref, dww_ref, dwb_ref, dbg_ref, dbb_ref, U0, DU):
        i = pl.program_id(1)
        u1 = jnp.concatenate([u1_ref[...], u1n_ref[...]], axis=0)
        d3 = jnp.concatenate([d3_ref[...], d3n_ref[...]], axis=0).astype(F32)
        rows = lax.broadcasted_iota(jnp.int32, (n, 1), 0)
        d3 = jnp.where((rows < tr) | (i < nr - 1), d3, 0.0)
        mu = jnp.mean(u1, axis=-1, keepdims=True)
        d = u1 - mu
        var = jnp.mean(d * d, axis=-1, keepdims=True)
        rstd = lax.rsqrt(var + LN_EPS)
        xh = d * rstd
        u2 = xh * bg_ref[...] + bb_ref[...]
        sg = _sigmoid(u2)
        du2 = d3 * (sg * (1.0 + u2 * (1.0 - sg)))
        dxh = du2 * bg_ref[...]
        m1 = jnp.mean(dxh, axis=-1, keepdims=True)
        m2 = jnp.mean(dxh * xh, axis=-1, keepdims=True)
        du1 = rstd * (dxh - m1 - xh * m2)
        DU[...] = du1
        pbg = jnp.sum(du2[0:tr] * xh[0:tr], axis=0, keepdims=True)
        pbb = jnp.sum(du2[0:tr], axis=0, keepdims=True)
        pwb = jnp.sum(du1[0:tr], axis=0, keepdims=True)

        glu_prev = ap_ref[...].astype(F32) * _sigmoid(gp_ref[...].astype(F32))
        U0[0:H, :] = jnp.where(i > 0, glu_prev, 0.0)
        a = a_ref[...].astype(F32)
        sgg = _sigmoid(g_ref[...].astype(F32))
        U0[H:H + tr, :] = a * sgg

        @pl.when(i == 0)
        def _():
            dbg_ref[...] = pbg
            dbb_ref[...] = pbb
            dwb_ref[...] = pwb
            dww_ref[...] = jnp.zeros_like(dww_ref)

        @pl.when(i > 0)
        def _():
            dbg_ref[...] += pbg
            dbb_ref[...] += pbb
            dwb_ref[...] += pwb

        du0 = w_ref[pl.ds(0, 1), :] * DU[pl.ds(KW - 1, tr), :]
        for k in range(1, KW):
            du0 = du0 + w_ref[pl.ds(k, 1), :] * DU[pl.ds(KW - 1 - k, tr), :]
        da_ref[...] = (du0 * sgg).astype(BF16)
        dg_ref[...] = (du0 * a * sgg * (1.0 - sgg)).astype(BF16)
        dcur = DU[pl.ds(0, tr), :]
        for k in range(KW):
            dww_ref[pl.ds(k, 1), :] += jnp.sum(dcur * U0[pl.ds(first_tap + k, tr), :], axis=0, keepdims=True)

    acol = lambda c: 3
    gcol = lambda c: 4
    one = lambda c: 1
    zero = lambda c: 0
    vec = pl.BlockSpec((1, A), lambda c, i: (0, 0))
    blk = pl.BlockSpec((tr, A), lambda c, i: (i, 0))
    return pl.pallas_call(
        body, name=name,
        out_shape=(jax.ShapeDtypeStruct((S, A), BF16), jax.ShapeDtypeStruct((S, A), BF16),
                   jax.ShapeDtypeStruct((KW, A), F32), jax.ShapeDtypeStruct((1, A), F32),
                   jax.ShapeDtypeStruct((1, A), F32), jax.ShapeDtypeStruct((1, A), F32)),
        grid=(1, nr),
        in_specs=[_cur_spec(tr, A, one), _next_spec(tr, H, A, one, S),
                  _cur_spec(tr, A, zero), _next_spec(tr, H, A, zero, S),
                  _prev_spec(tr, H, A, acol), _cur_spec(tr, A, acol),
                  _prev_spec(tr, H, A, gcol), _cur_spec(tr, A, gcol),
                  pl.BlockSpec((KW, A), lambda c, i: (0, 0)), vec, vec],
        out_specs=(blk, blk, pl.BlockSpec((KW, A), lambda c, i: (0, 0)), vec, vec, vec),
        scratch_shapes=[pltpu.VMEM((H + tr, A), F32), pltpu.VMEM((n, A), F32)],
        compiler_params=_params("arbitrary", "arbitrary"),
    )(du3, du3, u1, u1, h, h, h, h, dw_w, bn_g, bn_b)


def _pool_inv_count(row0, nrows, window):
    t = row0 + lax.broadcasted_iota(jnp.int32, (nrows, 1), 0)
    return 1.0 / jnp.minimum(t + 1, window).astype(F32)


def _odd_fwd(h, conv_w, pool_w, pool_scale, name):
    S = h.shape[0]
    C = conv_w.shape[1]
    G = len(POOL_WINDOWS)
    Dg = C // G
    H = HALO3
    tr = _pick(S, 256, H)

    def body(cb_ref, ccp_ref, cc_ref, chp_ref, ch_ref, pp_ref, p_ref, w_ref, pw_ref, sc_ref, mix_ref, M, P):
        i = pl.program_id(1)
        M[0:H, :] = jnp.where(i > 0, ccp_ref[...].astype(F32) * chp_ref[...].astype(F32), 0.0)
        M[H:H + tr, :] = cc_ref[...].astype(F32) * ch_ref[...].astype(F32)
        cm = (w_ref[pl.ds(0, 1), :] * M[pl.ds(H - 2, tr), :] + w_ref[pl.ds(1, 1), :] * M[pl.ds(H - 1, tr), :]
              + w_ref[pl.ds(2, 1), :] * M[pl.ds(H, tr), :])
        mix_ref[:, 0:C] = (cb_ref[...].astype(F32) * cm).astype(BF16)
        P[0:H, :] = jnp.where(i > 0, pp_ref[...].astype(F32), 0.0)
        P[H:H + tr, :] = p_ref[...].astype(F32)
        for gi, window in enumerate(POOL_WINDOWS):
            cols = pl.ds(gi * Dg, Dg)
            wsum = P[pl.ds(H, tr), cols]
            for dlt in range(1, window):
                wsum = wsum + P[pl.ds(H - dlt, tr), cols]
            diff = wsum * _pool_inv_count(i * tr, tr, window) - P[pl.ds(H, tr), cols]
            yd = jnp.dot(diff.astype(BF16), pw_ref[gi], preferred_element_type=F32) * sc_ref[:, cols]
            mix_ref[:, pl.ds(C + gi * Dg, Dg)] = yd.astype(BF16)

    col = lambda k: (lambda c: k)
    return pl.pallas_call(
        body, name=name, out_shape=jax.ShapeDtypeStruct((S, 2 * C), BF16),
        grid=(1, S // tr),
        in_specs=[_cur_spec(tr, C, col(0)),
                  _prev_spec(tr, H, C, col(1)), _cur_spec(tr, C, col(1)),
                  _prev_spec(tr, H, C, col(2)), _cur_spec(tr, C, col(2)),
                  _prev_spec(tr, H, C, col(3)), _cur_spec(tr, C, col(3)),
                  pl.BlockSpec((3, C), lambda c, i: (0, 0)),
                  pl.BlockSpec((G, Dg, Dg), lambda c, i: (0, 0, 0)),
                  pl.BlockSpec((1, C), lambda c, i: (0, 0))],
        out_specs=pl.BlockSpec((tr, 2 * C), lambda c, i: (i, 0)),
        scratch_shapes=[pltpu.VMEM((H + tr, C), F32), pltpu.VMEM((H + tr, C), F32)],
        compiler_params=_params("parallel", "parallel"),
    )(h, h, h, h, h, h, h, conv_w, pool_w, pool_scale)


def _odd_bwd(dmix, h, conv_w, pool_w, pool_scale, name):
    S = h.shape[0]
    C = conv_w.shape[1]
    G = len(POOL_WINDOWS)
    Dg = C // G
    H = HALO3
    tr = _pick(S, 256, H)
    nr = S // tr
    n = tr + H
    nt_dims = (((1,), (1,)), ((), ()))
    tn_dims = (((0,), (0,)), ((), ()))

    def body(dyc_ref, dycn_ref, dyd_ref, dydn_ref, cb_ref, cbn_ref, ccp_ref, cc_ref, ccn_ref,
             chp_ref, ch_ref, chn_ref, pp_ref, p_ref, w_ref, pw_ref, sc_ref,
             dh_ref, dw_ref, dpw_ref, dsc_ref, M, DCM, P, Q):
        i = pl.program_id(1)
        rows = lax.broadcasted_iota(jnp.int32, (n, 1), 0)
        valid = (rows < tr) | (i < nr - 1)

        @pl.when(i == 0)
        def _():
            dw_ref[...] = jnp.zeros_like(dw_ref)
            dpw_ref[...] = jnp.zeros_like(dpw_ref)
            dsc_ref[...] = jnp.zeros_like(dsc_ref)

        M[0:H, :] = jnp.where(i > 0, ccp_ref[...].astype(F32) * chp_ref[...].astype(F32), 0.0)
        cc = cc_ref[...].astype(F32)
        ch = ch_ref[...].astype(F32)
        M[H:H + tr, :] = cc * ch
        M[H + tr:H + tr + H, :] = ccn_ref[...].astype(F32) * chn_ref[...].astype(F32)
        w0, w1, w2 = w_ref[pl.ds(0, 1), :], w_ref[pl.ds(1, 1), :], w_ref[pl.ds(2, 1), :]
        cm = w0 * M[pl.ds(H - 2, tr), :] + w1 * M[pl.ds(H - 1, tr), :] + w2 * M[pl.ds(H, tr), :]
        dyc = jnp.concatenate([dyc_ref[...], dycn_ref[...]], axis=0).astype(F32)
        dyc = jnp.where(valid, dyc, 0.0)
        cbf = jnp.concatenate([cb_ref[...], cbn_ref[...]], axis=0).astype(F32)
        dh_ref[:, 0:C] = (dyc[0:tr] * cm).astype(BF16)
        DCM[...] = dyc * cbf
        dm = w2 * DCM[pl.ds(0, tr), :] + w1 * DCM[pl.ds(1, tr), :] + w0 * DCM[pl.ds(2, tr), :]
        dh_ref[:, C:2 * C] = (dm * ch).astype(BF16)
        dh_ref[:, 2 * C:3 * C] = (dm * cc).astype(BF16)
        dcur = DCM[pl.ds(0, tr), :]
        for k in range(3):
            dw_ref[pl.ds(k, 1), :] += jnp.sum(dcur * M[pl.ds(H - 2 + k, tr), :], axis=0, keepdims=True)

        P[0:H, :] = jnp.where(i > 0, pp_ref[...].astype(F32), 0.0)
        P[H:H + tr, :] = p_ref[...].astype(F32)
        dyd = jnp.concatenate([dyd_ref[...], dydn_ref[...]], axis=0).astype(F32)
        dyd = jnp.where(valid, dyd, 0.0)
        for gi, window in enumerate(POOL_WINDOWS):
            cols = pl.ds(gi * Dg, Dg)
            lo = gi * Dg
            wsum = P[pl.ds(H, tr), cols]
            for dlt in range(1, window):
                wsum = wsum + P[pl.ds(H - dlt, tr), cols]
            diff = (wsum * _pool_inv_count(i * tr, tr, window) - P[pl.ds(H, tr), cols]).astype(BF16)
            pw = pw_ref[gi]
            dyd_g = dyd[:, lo:lo + Dg]
            e = (dyd_g * sc_ref[:, cols]).astype(BF16)
            yraw = jnp.dot(diff, pw, preferred_element_type=F32)
            dsc_ref[:, cols] += jnp.sum(dyd_g[0:tr] * yraw, axis=0, keepdims=True)
            dpw_ref[gi] += lax.dot_general(diff, e[0:tr], tn_dims, preferred_element_type=F32)
            ddiff = lax.dot_general(e, pw, nt_dims, preferred_element_type=F32)
            Q[:, cols] = ddiff * _pool_inv_count(i * tr, n, window)
            acc = Q[pl.ds(0, tr), cols]
            for dlt in range(1, window):
                acc = acc + Q[pl.ds(dlt, tr), cols]
            dh_ref[:, pl.ds(3 * C + lo, Dg)] = (acc - ddiff[0:tr]).astype(BF16)

    col = lambda k: (lambda c: k)
    return pl.pallas_call(
        body, name=name,
        out_shape=(jax.ShapeDtypeStruct((S, 4 * C), BF16), jax.ShapeDtypeStruct((3, C), F32),
                   jax.ShapeDtypeStruct((G, Dg, Dg), F32), jax.ShapeDtypeStruct((1, C), F32)),
        grid=(1, nr),
        in_specs=[_cur_spec(tr, C, col(0)), _next_spec(tr, H, C, col(0), S),
                  _cur_spec(tr, C, col(1)), _next_spec(tr, H, C, col(1), S),
                  _cur_spec(tr, C, col(0)), _next_spec(tr, H, C, col(0), S),
                  _prev_spec(tr, H, C, col(1)), _cur_spec(tr, C, col(1)), _next_spec(tr, H, C, col(1), S),
                  _prev_spec(tr, H, C, col(2)), _cur_spec(tr, C, col(2)), _next_spec(tr, H, C, col(2), S),
                  _prev_spec(tr, H, C, col(3)), _cur_spec(tr, C, col(3)),
                  pl.BlockSpec((3, C), lambda c, i: (0, 0)),
                  pl.BlockSpec((G, Dg, Dg), lambda c, i: (0, 0, 0)),
                  pl.BlockSpec((1, C), lambda c, i: (0, 0))],
        out_specs=(pl.BlockSpec((tr, 4 * C), lambda c, i: (i, 0)),
                   pl.BlockSpec((3, C), lambda c, i: (0, 0)),
                   pl.BlockSpec((G, Dg, Dg), lambda c, i: (0, 0, 0)),
                   pl.BlockSpec((1, C), lambda c, i: (0, 0))),
        scratch_shapes=[pltpu.VMEM((H + tr + H, C), F32), pltpu.VMEM((n, C), F32),
                        pltpu.VMEM((H + tr, C), F32), pltpu.VMEM((n, C), F32)],
        compiler_params=_params("arbitrary", "arbitrary"),
    )(dmix, dmix, dmix, dmix, h, h, h, h, h, h, h, h, h, h, conv_w, pool_w, pool_scale)


def _adamw(w, grads, m, v, name):
    L, R, C = w.shape
    assert len(grads) == L
    tr, tc = _pick(R, 256, 8), _pick(C, 1408)
    ni, nj = R // tr, C // tc
    c1 = 1.0 / (1.0 - ADAM_B1 ** ADAM_STEP)
    c2 = 1.0 / (1.0 - ADAM_B2 ** ADAM_STEP)

    def g_spec(layer):
        def idx(l, i, j):
            before, after = l < layer, l > layer
            return (jnp.where(before, 0, jnp.where(after, ni - 1, i)),
                    jnp.where(before, 0, jnp.where(after, nj - 1, j)))
        return pl.BlockSpec((tr, tc), idx)

    def body(w_ref, *rest):
        g_refs = rest[:L]
        m_ref, v_ref, go_ref, d_ref, mo_ref, vo_ref = rest[L:]
        l = pl.program_id(0)
        gg = g_refs[0][...]
        for k in range(1, L):
            gg = jnp.where(l == k, g_refs[k][...], gg)
        mn = ADAM_B1 * m_ref[...] + (1.0 - ADAM_B1) * gg
        vn = ADAM_B2 * v_ref[...] + (1.0 - ADAM_B2) * (gg * gg)
        d_ref[...] = -ADAM_LR * ((mn * c1) / (jnp.sqrt(vn * c2) + ADAM_EPS) + ADAM_WD * w_ref[...])
        go_ref[...] = gg
        mo_ref[...] = mn
        vo_ref[...] = vn

    blk = pl.BlockSpec((None, tr, tc), lambda l, i, j: (l, i, j))
    sds = jax.ShapeDtypeStruct(w.shape, F32)
    return pl.pallas_call(
        body, name=name, out_shape=(sds, sds, sds, sds),
        grid=(L, R // tr, C // tc),
        in_specs=[blk] + [g_spec(k) for k in range(L)] + [blk, blk], out_specs=(blk, blk, blk, blk),
        compiler_params=_params("arbitrary", "arbitrary", "arbitrary"),
    )(w, *grads, m, v)


def _sum_slots(buf, name):
    N, R, C = buf.shape
    tr = _pick(R, 512, 8)

    def body(b_ref, o_ref):
        acc = b_ref[0]
        for k in range(1, N):
            acc = acc + b_ref[k]
        o_ref[...] = acc

    return pl.pallas_call(
        body, name=name, out_shape=jax.ShapeDtypeStruct((R, C), F32),
        grid=(R // tr,),
        in_specs=[pl.BlockSpec((N, tr, C), lambda i: (0, i, 0))],
        out_specs=pl.BlockSpec((tr, C), lambda i: (i, 0)),
        compiler_params=_params("parallel"),
    )(buf)


def _pair_sum(pos, g, rsib, name):
    _, hr, hc = rsib.shape
    tr, tc = _pick(hr, 512, 16), _pick(hc, 2816)

    def body(p_ref, g_ref, r_ref, o_ref):
        o_ref[...] = (g_ref[...].astype(F32) + r_ref[...].astype(F32)).astype(BF16)

    blk = pl.BlockSpec((None, tr, tc), lambda s, i, j, p: (s, i, j))
    return pl.pallas_call(
        body, name=name, out_shape=jax.ShapeDtypeStruct(rsib.shape, BF16),
        grid_spec=pltpu.PrefetchScalarGridSpec(
            num_scalar_prefetch=1, grid=(N_CHIPS, hr // tr, hc // tc),
            in_specs=[pl.BlockSpec((None, None, tr, tc), lambda s, i, j, p: (s, p[1], i, j)), blk],
            out_specs=blk),
        compiler_params=_params("parallel", "parallel", "parallel"),
    )(pos, g, rsib)


def _chip_sum(pos, part, land, name):
    _, sr, sc = land.shape
    tr, tc = _pick(sr, 256, 16), _pick(sc, 2816)

    def body(p_ref, own_ref, l_ref, o_ref):
        acc = own_ref[...].astype(F32)
        for k in range(3):
            acc = acc + l_ref[k].astype(F32)
        o_ref[...] = acc

    return pl.pallas_call(
        body, name=name, out_shape=jax.ShapeDtypeStruct((2, sr, sc), F32),
        grid_spec=pltpu.PrefetchScalarGridSpec(
            num_scalar_prefetch=1, grid=(sr // tr, sc // tc),
            in_specs=[pl.BlockSpec((None, tr, tc), lambda i, j, p: (p[0], i, j)),
                      pl.BlockSpec((3, tr, tc), lambda i, j, p: (0, i, j))],
            out_specs=pl.BlockSpec((None, tr, tc), lambda i, j, p: (p[1], i, j))),
        compiler_params=_params("parallel", "parallel"),
    )(pos, part, land)


def _place():
    x, y, c = lax.axis_index("x"), lax.axis_index("y"), lax.axis_index("c")
    return x, y, c


def _other_chips(x, y):
    return [(1 - x, y), (x, 1 - y), (1 - x, 1 - y)]


def _rcopy(src, dst, ssem, rsem, dev):
    return pltpu.make_async_remote_copy(src_ref=src, dst_ref=dst, send_sem=ssem, recv_sem=rsem,
                                        device_id=dev, device_id_type=MESH)


def _gather_items(bufs):
    return sum(b.shape[0] for b in bufs)


def _gather_walk(full):
    t = 0
    for ref in full:
        for l in range(ref.shape[0]):
            yield t, ref, l
            t += 1


def _gather_start(full, ssem, rsem):
    x, y, c = _place()
    j = 2 * x + y
    for t, ref, l in _gather_walk(full):
        own = ref.at[l, j, c]
        for r, (px, py) in enumerate(_other_chips(x, y)):
            _rcopy(own, own, ssem.at[6 * t + r], rsem.at[6 * t + r], (px, py, c)).start()


def _gather_forward(full, ssem, rsem):
    x, y, c = _place()
    for t, ref, l in _gather_walk(full):
        for r, (px, py) in enumerate(_other_chips(x, y)):
            slab = ref.at[l, 2 * px + py, c]
            _rcopy(slab, slab, ssem.at[6 * t + r], rsem.at[6 * t + r], (px, py, c)).wait_recv()
            _rcopy(slab, slab, ssem.at[6 * t + 3 + r], rsem.at[6 * t + 3 + r], (x, y, 1 - c)).start()


def _gather_finish(full, ssem, rsem):
    x, y, c = _place()
    j = 2 * x + y
    for t, ref, l in _gather_walk(full):
        for r, (px, py) in enumerate(_other_chips(x, y)):
            got = ref.at[l, 2 * px + py, 1 - c]
            _rcopy(got, got, ssem.at[6 * t + 3 + r], rsem.at[6 * t + 3 + r], (x, y, 1 - c)).wait_recv()
    for t, ref, l in _gather_walk(full):
        own = ref.at[l, j, c]
        for r, (px, py) in enumerate(_other_chips(x, y)):
            _rcopy(own, own, ssem.at[6 * t + r], rsem.at[6 * t + r], (px, py, c)).wait_send()
            slab = ref.at[l, 2 * px + py, c]
            _rcopy(slab, slab, ssem.at[6 * t + 3 + r], rsem.at[6 * t + 3 + r], (x, y, 1 - c)).wait_send()


def _land_shape(part):
    return jax.ShapeDtypeStruct((3,) + part.shape[1:], part.dtype)


def _exchange_start(parts, land, ssem, rsem):
    x, y, c = _place()
    for a in range(len(parts)):
        for r, (px, py) in enumerate(_other_chips(x, y)):
            _rcopy(parts[a].at[2 * px + py], land[a].at[r], ssem.at[3 * a + r], rsem.at[3 * a + r],
                   (px, py, c)).start()


def _exchange_finish(parts, land, ssem, rsem):
    x, y, c = _place()
    for a in range(len(parts)):
        for r, (px, py) in enumerate(_other_chips(x, y)):
            _rcopy(parts[a].at[2 * px + py], land[a].at[r], ssem.at[3 * a + r], rsem.at[3 * a + r],
                   (px, py, c)).wait()


_Hosted = collections.namedtuple("_Hosted", "ins outs alias sems start finish")


def _host_join(*hosts):
    ins, outs, alias, sems, spans = [], [], {}, [], []
    for h in hosts:
        spans.append((h, len(ins), len(outs), len(sems)))
        alias.update({len(ins) + a: len(outs) + b for a, b in h.alias.items()})
        ins, outs, sems = ins + list(h.ins), outs + list(h.outs), sems + list(h.sems)

    def each(step):
        def run(i, o, s):
            for h, a, b, c in spans:
                getattr(h, step)(i[a:a + len(h.ins)], o[b:b + len(h.outs)], s[c:c + len(h.sems)])
        return run

    return _Hosted(ins, outs, alias, sems, each("start"), each("finish"))


def _host_exchange(parts):
    n = len(parts)
    return _Hosted(list(parts), [_land_shape(p) for p in parts], {}, [3 * n, 3 * n],
                   lambda ins, outs, sems: _exchange_start(ins, outs, *sems),
                   lambda ins, outs, sems: _exchange_finish(ins, outs, *sems))


def _host_gather(bufs):
    T = _gather_items(bufs)

    def finish(ins, outs, sems):
        _gather_forward(outs, *sems)
        _gather_finish(outs, *sems)

    return _Hosted(list(bufs), [jax.ShapeDtypeStruct(b.shape, b.dtype) for b in bufs],
                   {a: a for a in range(len(bufs))}, [6 * T, 6 * T],
                   lambda ins, outs, sems: _gather_start(outs, *sems), finish)


def _host_all_devices(buf):
    return _Hosted([buf], [jax.ShapeDtypeStruct((N_DEV,) + buf.shape, buf.dtype)], {}, [N_DEV - 1, N_DEV - 1, 1],
                   lambda ins, outs, sems: _all_devices_start(ins[0], outs[0], *sems),
                   lambda ins, outs, sems: _all_devices_finish(ins[0], outs[0], *sems))


def _allgather_big(bufs, name):
    n = len(bufs)
    T = _gather_items(bufs)

    def body(*refs):
        full = refs[n:2 * n]
        ssem, rsem = refs[2 * n:]
        _gather_start(full, ssem, rsem)
        _gather_forward(full, ssem, rsem)
        _gather_finish(full, ssem, rsem)

    return pl.pallas_call(
        body, name=name, out_shape=tuple(jax.ShapeDtypeStruct(b.shape, BF16) for b in bufs),
        in_specs=[ANY] * n, out_specs=tuple([ANY] * n),
        input_output_aliases={a: a for a in range(n)},
        scratch_shapes=[pltpu.SemaphoreType.DMA((6 * T,)), pltpu.SemaphoreType.DMA((6 * T,))],
    )(*bufs)


def _allgather_small(shards, name):
    n = len(shards)
    outs = tuple(jax.ShapeDtypeStruct((N_CHIPS,) + s.shape, s.dtype) for s in shards)

    def body(*refs):
        ins, full = refs[:n], refs[n:2 * n]
        ssem, rsem, lsem = refs[2 * n:]
        x, y, c = _place()
        j = 2 * x + y
        chips = _other_chips(x, y)
        cps, locs = [], []
        for a in range(n):
            loc = pltpu.make_async_copy(ins[a], full[a].at[j], lsem.at[a])
            loc.start()
            locs.append(loc)
            for r, (px, py) in enumerate(chips):
                cp = _rcopy(ins[a], full[a].at[j], ssem.at[3 * a + r], rsem.at[3 * a + r], (px, py, c))
                cp.start()
                cps.append(cp)
        for a in range(n):
            for r, (px, py) in enumerate(chips):
                dst = full[a].at[2 * px + py]
                _rcopy(dst, dst, ssem.at[3 * a + r], rsem.at[3 * a + r], (px, py, c)).wait_recv()
        for cp in cps:
            cp.wait_send()
        for loc in locs:
            loc.wait()

    return pl.pallas_call(
        body, name=name, out_shape=outs,
        in_specs=[ANY] * n, out_specs=tuple([ANY] * n),
        scratch_shapes=[pltpu.SemaphoreType.DMA((3 * n,)), pltpu.SemaphoreType.DMA((3 * n,)),
                        pltpu.SemaphoreType.DMA((n,))],
    )(*shards)


def _pair_copies(ins, got, ssem, rsem):
    x, y, c = _place()
    return [_rcopy(ins[a].at[s, 1 - c], got[a].at[s], ssem.at[N_CHIPS * a + s], rsem.at[N_CHIPS * a + s],
                   (x, y, 1 - c))
            for a in range(len(ins)) for s in range(N_CHIPS)]


def _host_pair_exchange(grads):
    n = len(grads)

    def start(ins, outs, sems):
        for cp in _pair_copies(ins, outs, *sems):
            cp.start()

    def finish(ins, outs, sems):
        for cp in _pair_copies(ins, outs, *sems):
            cp.wait()

    return _Hosted(list(grads), [jax.ShapeDtypeStruct((N_CHIPS,) + g.shape[2:], BF16) for g in grads], {},
                   [N_CHIPS * n, N_CHIPS * n], start, finish)


def _pair_exchange(grads, name):
    n = len(grads)
    host = _host_pair_exchange(grads)

    def body(*refs):
        ins, got, sems = refs[:n], refs[n:2 * n], refs[2 * n:]
        host.start(ins, got, sems)
        host.finish(ins, got, sems)

    return pl.pallas_call(
        body, name=name, out_shape=tuple(host.outs),
        in_specs=[ANY] * n, out_specs=tuple([ANY] * n),
        scratch_shapes=[pltpu.SemaphoreType.DMA((k,)) for k in host.sems],
    )(*grads)


def _chip_exchange(parts, name):
    n = len(parts)

    def body(*refs):
        ins, land = refs[:n], refs[n:2 * n]
        ssem, rsem = refs[2 * n:]
        _exchange_start(ins, land, ssem, rsem)
        _exchange_finish(ins, land, ssem, rsem)

    return pl.pallas_call(
        body, name=name, out_shape=tuple(_land_shape(p) for p in parts),
        in_specs=[ANY] * n, out_specs=tuple([ANY] * n),
        scratch_shapes=[pltpu.SemaphoreType.DMA((3 * n,)), pltpu.SemaphoreType.DMA((3 * n,))],
    )(*parts)


def _host_half_swap(bufs):
    n = len(bufs)

    def start(ins, full, sems):
        x, y, c = _place()
        for t in range(n):
            mine = full[t].at[c]
            _rcopy(mine, mine, sems[0].at[t], sems[1].at[t], (x, y, 1 - c)).start()

    def finish(ins, full, sems):
        x, y, c = _place()
        for t in range(n):
            got = full[t].at[1 - c]
            _rcopy(got, got, sems[0].at[t], sems[1].at[t], (x, y, 1 - c)).wait_recv()
        for t in range(n):
            mine = full[t].at[c]
            _rcopy(mine, mine, sems[0].at[t], sems[1].at[t], (x, y, 1 - c)).wait_send()

    return _Hosted(list(bufs), [jax.ShapeDtypeStruct(b.shape, F32) for b in bufs], {a: a for a in range(n)},
                   [n, n], start, finish)


def _half_swap(bufs, name):
    n = len(bufs)
    host = _host_half_swap(bufs)

    def body(*refs):
        ins, full, sems = refs[:n], refs[n:2 * n], refs[2 * n:]
        host.start(ins, full, sems)
        host.finish(ins, full, sems)

    return pl.pallas_call(
        body, name=name, out_shape=tuple(host.outs),
        in_specs=[ANY] * n, out_specs=tuple([ANY] * n),
        input_output_aliases=host.alias,
        scratch_shapes=[pltpu.SemaphoreType.DMA((k,)) for k in host.sems],
    )(*bufs)


def _flipped(x, y, c, m):
    fx, fy, fc = (m >> 2) & 1, (m >> 1) & 1, m & 1
    return x + fx - 2 * x * fx, y + fy - 2 * y * fy, c + fc - 2 * c * fc


def _all_devices_start(b_ref, o_ref, ssem, rsem, lsem):
    x, y, c = _place()
    me = 4 * x + 2 * y + c
    pltpu.make_async_copy(b_ref, o_ref.at[me], lsem.at[0]).start()
    for m in range(1, N_DEV):
        _rcopy(b_ref, o_ref.at[me], ssem.at[m - 1], rsem.at[m - 1], _flipped(x, y, c, m)).start()


def _all_devices_finish(b_ref, o_ref, ssem, rsem, lsem):
    x, y, c = _place()
    me = 4 * x + 2 * y + c
    for m in range(1, N_DEV):
        px, py, pc = _flipped(x, y, c, m)
        got = o_ref.at[4 * px + 2 * py + pc]
        _rcopy(got, got, ssem.at[m - 1], rsem.at[m - 1], (px, py, pc)).wait_recv()
    for m in range(1, N_DEV):
        _rcopy(b_ref, o_ref.at[me], ssem.at[m - 1], rsem.at[m - 1], _flipped(x, y, c, m)).wait_send()
    pltpu.make_async_copy(b_ref, o_ref.at[me], lsem.at[0]).wait()


def _pack(arrs):
    flat = jnp.concatenate([a.reshape(-1) for a in arrs])
    rows = -(-flat.shape[0] // (8 * LANES)) * 8
    flat = jnp.pad(flat, (0, rows * LANES - flat.shape[0]))
    return flat.reshape(rows, LANES)


def _unpack(buf, shapes):
    flat = buf.reshape(-1)
    out, off = [], 0
    for s in shapes:
        size = 1
        for d in s:
            size *= d
        out.append(flat[off:off + size].reshape(s))
        off += size
    return out


BIG = ("ev_w_in", "ev_w_out", "od_w_in", "od_w_out", "ffn_w_up", "ffn_w_down")
BIG_KIND = {"ev_w_in": "col", "ev_w_out": "row", "od_w_in": "col", "od_w_out": "row",
            "ffn_w_up": "col", "ffn_w_down": "row"}
SMALL_AXIS = {"ev_dw_w": 2, "ev_dw_b": None, "ev_bn_g": None, "ev_bn_b": None, "od_conv_w": 2,
              "od_pool_w": 2, "od_pool_scale": 1, "ffn_conv_w": 2, "ffn_conv_b": None, "ln_g": 2, "ln_b": 2}
WEIGHTS = ("ev_w_in", "ev_dw_w", "ev_dw_b", "ev_bn_g", "ev_bn_b", "ev_w_out", "od_w_in", "od_conv_w",
           "od_pool_w", "od_pool_scale", "od_w_out", "ffn_w_up", "ffn_conv_w", "ffn_conv_b", "ffn_w_down",
           "ln_g", "ln_b")


def _ffn_fwd(xb, w_up, w_down, conv_w, conv_b, tag, host_up=None):
    hu = _matmul(xb, w_up, mode="nn", b_lead=0, b_split=True, out_dtype=BF16, name=f"{tag}_up", tm=1024, tn=1408,
                 hosted=host_up)
    hu, up_outs = hu if host_up is not None else (hu, None)
    z = _ffn_act_fwd(hu, conv_w, conv_b, name=f"{tag}_act")
    y = _matmul(z, w_down, mode="nn", b_lead=0, out_dtype=F32, name=f"{tag}_down", tm=512, tn=1024)
    return hu, z, y, up_outs


def _ffn_bwd(drb, dr, alpha, xb, hu, z, w_up, w_down, conv_w, conv_b, tag, host_dwup=None, host_dx=None):
    g_down = _matmul(z, drb, mode="tn", out_dtype=BF16, name=f"{tag}_dwdown", tm=512, tn=1024)
    dz = _matmul(drb, w_down, mode="nt", b_lead=0, out_dtype=BF16, name=f"{tag}_dz", tm=1024, tn=1408)
    dg, du, dcw, dcb = _ffn_act_bwd(dz, hu, conv_w, conv_b, name=f"{tag}_actbwd")
    g_up = _matmul(xb, (dg, du), mode="tn", out_split=True, out_dtype=BF16, name=f"{tag}_dwup", tm=512, tn=1408,
                   hosted=host_dwup)
    g_up, dwup_outs = g_up if host_dwup is not None else (g_up, None)
    dx = _matmul((dg, du), w_up, mode="nt", b_lead=0, b_split=True, out_dtype=F32, add=dr, add_scale=alpha,
                 name=f"{tag}_dx", tm=512, tn=512, hosted=host_dx)
    dx, dx_outs = dx if host_dx is not None else (dx, None)
    return dx, g_up, g_down, dcw, dcb, dwup_outs, dx_outs


def kernel(x, ev_w_in, ev_dw_w, ev_dw_b, ev_bn_g, ev_bn_b, ev_w_out, od_w_in, od_conv_w, od_pool_w, od_pool_scale, od_w_out, ffn_w_up, ffn_conv_w, ffn_conv_b, ffn_w_down, ln_g, ln_b, loss_target, m_ev_w_in, m_ev_dw_w, m_ev_dw_b, m_ev_bn_g, m_ev_bn_b, m_ev_w_out, m_od_w_in, m_od_conv_w, m_od_pool_w, m_od_pool_scale, m_od_w_out, m_ffn_w_up, m_ffn_conv_w, m_ffn_conv_b, m_ffn_w_down, m_ln_g, m_ln_b, v_ev_w_in, v_ev_dw_w, v_ev_dw_b, v_ev_bn_g, v_ev_bn_b, v_ev_w_out, v_od_w_in, v_od_conv_w, v_od_pool_w, v_od_pool_scale, v_od_w_out, v_ffn_w_up, v_ffn_conv_w, v_ffn_conv_b, v_ffn_w_down, v_ln_g, v_ln_b):
    wts = dict(ev_w_in=ev_w_in, ev_dw_w=ev_dw_w, ev_dw_b=ev_dw_b, ev_bn_g=ev_bn_g, ev_bn_b=ev_bn_b,
               ev_w_out=ev_w_out, od_w_in=od_w_in, od_conv_w=od_conv_w, od_pool_w=od_pool_w,
               od_pool_scale=od_pool_scale, od_w_out=od_w_out, ffn_w_up=ffn_w_up, ffn_conv_w=ffn_conv_w,
               ffn_conv_b=ffn_conv_b, ffn_w_down=ffn_w_down, ln_g=ln_g, ln_b=ln_b)
    mom = dict(ev_w_in=m_ev_w_in, ev_dw_w=m_ev_dw_w, ev_dw_b=m_ev_dw_b, ev_bn_g=m_ev_bn_g, ev_bn_b=m_ev_bn_b,
               ev_w_out=m_ev_w_out, od_w_in=m_od_w_in, od_conv_w=m_od_conv_w, od_pool_w=m_od_pool_w,
               od_pool_scale=m_od_pool_scale, od_w_out=m_od_w_out, ffn_w_up=m_ffn_w_up, ffn_conv_w=m_ffn_conv_w,
               ffn_conv_b=m_ffn_conv_b, ffn_w_down=m_ffn_w_down, ln_g=m_ln_g, ln_b=m_ln_b)
    var = dict(ev_w_in=v_ev_w_in, ev_dw_w=v_ev_dw_w, ev_dw_b=v_ev_dw_b, ev_bn_g=v_ev_bn_g, ev_bn_b=v_ev_bn_b,
               ev_w_out=v_ev_w_out, od_w_in=v_od_w_in, od_conv_w=v_od_conv_w, od_pool_w=v_od_pool_w,
               od_pool_scale=v_od_pool_scale, od_w_out=v_od_w_out, ffn_w_up=v_ffn_w_up, ffn_conv_w=v_ffn_conv_w,
               ffn_conv_b=v_ffn_conv_b, ffn_w_down=v_ffn_w_down, ln_g=v_ln_g, ln_b=v_ln_b)

    S, D = x.shape[1], x.shape[2]
    depth = ln_g.shape[0]
    alpha = (2.0 * depth) ** 0.25
    A = ev_dw_b.shape[-1]
    n_heads = A // HEAD_DIM
    xi, yi, ci = _place()
    chip = 2 * xi + yi
    pos = jnp.stack([chip, ci]).astype(jnp.int32)

    bufs = {f"{k}{l}": _cast_into_gather(pos, wts[k], l, name=f"cast_{k}{l}")
            for k in BIG for l in range(wts[k].shape[0])}

    def whole(key):
        _, r, c = wts[key[:-1]].shape
        col = BIG_KIND[key[:-1]] == "col"
        return bufs[key].reshape(1, N_CHIPS, r, c) if col else bufs[key].reshape(1, N_CHIPS * r, c)

    full = {}

    def gathered_now(keys, arrays):
        bufs.update(zip(keys, arrays))
        full.update({key: whole(key) for key in keys})

    under_attn = ("ffn_w_up0", "ffn_w_down0", "od_w_in0", "od_w_out0", "ffn_w_down1")
    gathered_now(("ev_w_in0",), _allgather_big([bufs["ev_w_in0"]], name="gather_first"))
    small_sharded = [k for k in WEIGHTS if k not in BIG and SMALL_AXIS[k] is not None]
    gathered = _allgather_small([wts[k] for k in small_sharded], name="gather_small")
    sm = {k: wts[k] for k in WEIGHTS if k not in BIG and SMALL_AXIS[k] is None}
    for k, g4 in zip(small_sharded, gathered):
        sm[k] = jnp.concatenate([g4[t] for t in range(N_CHIPS)], axis=SMALL_AXIS[k])
    pool_w_bf = sm["od_pool_w"][0].astype(BF16)

    x0 = x[0]
    x0b = _cast_bf16(x, name="cast_x")[0]
    h0, got = _matmul(x0b, full["ev_w_in0"], mode="nn", b_lead=0, b_split=True, out_dtype=BF16, name="ev_in",
                      tm=1024, tn=1280, hosted=_host_gather([bufs["ev_w_out0"]]))
    gathered_now(("ev_w_out0",), got)
    o_a, tot, *rest = _attn_fwd(h0, n_heads, name="attn_fwd", gather=[bufs[k] for k in under_attn])
    gathered_now(under_attn, rest)
    u1, u3 = _evenconv_fwd(h0, sm["ev_dw_w"][0], sm["ev_dw_b"], sm["ev_bn_g"], sm["ev_bn_b"], name="evconv_fwd")
    mix0 = jnp.concatenate([o_a, u3], axis=1)
    y1 = _matmul(mix0, full["ev_w_out0"], mode="nn", b_lead=0, out_dtype=F32, name="ev_out", tm=1024, tn=1024)
    x1, x1b, xh1, rs1 = _ln_fwd(x0, y1, sm["ln_g"][0, 0][None], sm["ln_b"][0, 0][None], alpha, name="ln00")
    hu0, z0, y2, got_up = _ffn_fwd(
        x1b, full["ffn_w_up0"], full["ffn_w_down0"], sm["ffn_conv_w"][0], sm["ffn_conv_b"][0][None], "ffn0",
        host_up=_host_gather([bufs["ffn_w_up1"]]))
    gathered_now(("ffn_w_up1",), got_up)
    x2, x2b, xh2, rs2 = _ln_fwd(x1, y2, sm["ln_g"][0, 1][None], sm["ln_b"][0, 1][None], alpha, name="ln01")
    h1 = _matmul(x2b, full["od_w_in0"], mode="nn", b_lead=0, b_split=True, out_dtype=BF16, name="od_in",
                 tm=1024, tn=1024)
    mix1 = _odd_fwd(h1, sm["od_conv_w"][0], pool_w_bf, sm["od_pool_scale"], name="odd_fwd")
    y3 = _matmul(mix1, full["od_w_out0"], mode="nn", b_lead=0, out_dtype=F32, name="od_out", tm=1024, tn=1024)
    x3, x3b, xh3, rs3 = _ln_fwd(x2, y3, sm["ln_g"][1, 0][None], sm["ln_b"][1, 0][None], alpha, name="ln10")
    hu1, z1, y4, _ = _ffn_fwd(x3b, full["ffn_w_up1"], full["ffn_w_down1"], sm["ffn_conv_w"][1],
                              sm["ffn_conv_b"][1][None], "ffn1")

    dr4, dr4b, dg11, db11, loss_part = _ln_loss_bwd(x3, y4, sm["ln_g"][1, 1][None], sm["ln_b"][1, 1][None],
                                                    loss_target[0], alpha, name="ln11_loss")
    loss = lax.psum(loss_part[0, 0], ("x", "y", "c"))

    def halves_view(named):
        g4 = []
        for k, g in named:
            rows, cols = (g.shape[1], g.shape[2]) if BIG_KIND[k] == "col" else (g.shape[0] // N_CHIPS, g.shape[1])
            g4.append(g.reshape(N_CHIPS, 2, rows // 2, cols))
        return g4

    def pair_sums(g4, sib, tag):
        return [_pair_sum(pos, g, r, name=f"grad_pair_sum_{tag}{t}") for t, (g, r) in enumerate(zip(g4, sib))]

    def pair_reduce(named, tag):
        g4 = halves_view(named)
        return pair_sums(g4, _pair_exchange(g4, name=f"grad_pair_exchange_{tag}"), tag)

    dx3, g_up1, g_down1, dcw1, dcb1, _, _ = _ffn_bwd(dr4b, dr4, alpha, x3b, hu1, z1, full["ffn_w_up1"],
                                                     full["ffn_w_down1"], sm["ffn_conv_w"][1],
                                                     sm["ffn_conv_b"][1][None], "ffn1")
    dr3, dr3b, dg10, db10 = _ln_bwd(dx3, xh3, rs3, sm["ln_g"][1, 0][None], name="ln10_bwd")
    g4_f1 = halves_view([("ffn_w_up", g_up1), ("ffn_w_down", g_down1)])
    g_odout, sib_f1 = _matmul(mix1, dr3b, mode="tn", out_dtype=BF16, name="od_dwout", tm=512, tn=1024,
                              hosted=_host_pair_exchange(g4_f1))
    parts_f1 = pair_sums(g4_f1, sib_f1, "f1")
    dmix1 = _matmul(dr3b, full["od_w_out0"], mode="nt", b_lead=0, out_dtype=BF16, name="od_dmix", tm=1024, tn=1024)
    dh1, d_odconv, d_pool, d_pscale = _odd_bwd(dmix1, h1, sm["od_conv_w"][0], pool_w_bf, sm["od_pool_scale"],
                                               name="odd_bwd")
    g_odin = _matmul(x2b, dh1, mode="tn", out_split=True, out_dtype=BF16, name="od_dwin", tm=512, tn=1024)
    dx2 = _matmul(dh1, full["od_w_in0"], mode="nt", b_lead=0, b_split=True, out_dtype=F32, add=dr3, add_scale=alpha,
                  name="od_dx", tm=1024, tn=512)
    dr2, dr2b, dg01, db01 = _ln_bwd(dx2, xh2, rs2, sm["ln_g"][0, 1][None], name="ln01_bwd")
    dx1, g_up0, g_down0, dcw0, dcb0, land_up1, land_down1 = _ffn_bwd(
        dr2b, dr2, alpha, x1b, hu0, z0, full["ffn_w_up0"], full["ffn_w_down0"], sm["ffn_conv_w"][0],
        sm["ffn_conv_b"][0][None], "ffn0",
        host_dwup=_host_exchange(parts_f1[:1]), host_dx=_host_exchange(parts_f1[1:]))
    dr1, dr1b, dg00, db00 = _ln_bwd(dx1, xh1, rs1, sm["ln_g"][0, 0][None], name="ln00_bwd")
    g4_b = halves_view([("od_w_in", g_odin), ("od_w_out", g_odout), ("ffn_w_up", g_up0), ("ffn_w_down", g_down0)])
    g_evout, sib_b = _matmul(mix0, dr1b, mode="tn", out_dtype=BF16, name="ev_dwout", tm=512, tn=1024,
                             hosted=_host_pair_exchange(g4_b))
    g4_o = halves_view([("ev_w_out", g_evout)])
    dmix0, sib_o = _matmul(dr1b, full["ev_w_out0"], mode="nt", b_lead=0, out_dtype=BF16, name="ev_dmix",
                           tm=1024, tn=1024, hosted=_host_pair_exchange(g4_o))
    da, dgate, d_dww, d_dwb, d_bng, d_bnb = _evenconv_bwd(dmix0, u1, h0, sm["ev_dw_w"][0], sm["ev_bn_g"],
                                                          sm["ev_bn_b"], name="evconv_bwd")
    parts_b = pair_sums(g4_b + g4_o, list(sib_b) + list(sib_o), "b")

    d_ln_g = jnp.stack([jnp.stack([dg00[0], dg01[0]]), jnp.stack([dg10[0], dg11[0]])])
    d_ln_b = jnp.stack([jnp.stack([db00[0], db01[0]]), jnp.stack([db10[0], db11[0]])])
    small_partial = {
        "ev_dw_w": d_dww[None], "ev_dw_b": d_dwb, "ev_bn_g": d_bng, "ev_bn_b": d_bnb,
        "od_conv_w": d_odconv[None], "od_pool_w": d_pool[None], "od_pool_scale": d_pscale,
        "ffn_conv_w": jnp.stack([dcw0, dcw1]), "ffn_conv_b": jnp.concatenate([dcb0, dcb1], axis=0),
        "ln_g": d_ln_g, "ln_b": d_ln_b}
    small_names = [k for k in WEIGHTS if k not in BIG]
    packed = _pack([small_partial[k] for k in small_names])
    dq, dk, dv, under_bwd = _attn_bwd(h0, dmix0, tot, n_heads, name="attn_bwd",
                                      hosted=_host_join(_host_exchange(parts_b), _host_all_devices(packed)))
    land_b, all_small = under_bwd[:-1], under_bwd[-1]
    dh0 = jnp.concatenate([dq, dk, dv, da, dgate], axis=1)
    g_evin = _matmul(x0b, dh0, mode="tn", out_split=True, out_dtype=BF16, name="ev_dwin", tm=512, tn=1280)
    parts_e = pair_reduce([("ev_w_in", g_evin)], "e")
    grad_x, land_e = _matmul(dh0, full["ev_w_in0"], mode="nt", b_lead=0, b_split=True, out_dtype=F32, add=dr1,
                             add_scale=alpha, name="ev_dx", tm=1024, tn=512, hosted=_host_exchange(parts_e))

    order = ["ffn_w_up1", "ffn_w_down1", "od_w_in0", "od_w_out0", "ffn_w_up0", "ffn_w_down0", "ev_w_out0", "ev_w_in0"]
    parts = parts_f1 + parts_b + parts_e
    land = list(land_up1) + list(land_down1) + list(land_b) + list(land_e)
    halves = [_chip_sum(pos, p, ld, name=f"grad_chip_sum_{tag}") for tag, p, ld in zip(order, parts, land)]
    reduced = dict(zip(order, _half_swap(halves, name="grad_half_swap")))
    big_grads = {k: [reduced[f"{k}{l}"].reshape(wts[k].shape[1:]) for l in range(wts[k].shape[0])] for k in BIG}

    summed = _sum_slots(all_small, name="sum_small_grads")
    small_full = dict(zip(small_names, _unpack(summed, [small_partial[k].shape for k in small_names])))
    small_grads = {}
    for k in small_names:
        ax = SMALL_AXIS[k]
        if ax is None:
            small_grads[k] = small_full[k]
        else:
            size = wts[k].shape[ax]
            small_grads[k] = lax.dynamic_slice_in_dim(small_full[k], chip * size, size, axis=ax)

    grads, delta, new_m, new_v = {}, {}, {}, {}
    for k in BIG:
        grads[k], delta[k], new_m[k], new_v[k] = _adamw(wts[k], big_grads[k], mom[k], var[k], name=f"adamw_{k}")
    shapes = [wts[k].shape for k in small_names]
    pw, pg, pm, pv = (_pack([d[k] for k in small_names]) for d in (wts, small_grads, mom, var))
    sg, sd, smn, svn = _adamw(pw[None], [pg], pm[None], pv[None], name="adamw_small")
    for dst, buf in ((grads, sg), (delta, sd), (new_m, smn), (new_v, svn)):
        for k, a in zip(small_names, _unpack(buf[0], shapes)):
            dst[k] = a

    return (loss, grad_x[None], *[grads[k] for k in WEIGHTS], *[delta[k] for k in WEIGHTS],
            *[new_m[k] for k in WEIGHTS], *[new_v[k] for k in WEIGHTS])
```

```python
import collections

import jax
import jax.numpy as jnp
from jax import lax
from jax.experimental import pallas as pl
from jax.experimental.pallas import tpu as pltpu

F32 = jnp.float32
BF16 = jnp.bfloat16

HEAD_DIM = 128
POOL_WINDOWS = (2, 4, 8, 16)
LN_EPS = 1e-5
ADAM_LR = 0.001
ADAM_B1 = 0.9
ADAM_B2 = 0.999
ADAM_EPS = 1e-08
ADAM_WD = 0.01
ADAM_STEP = 10
N_CHIPS = 4
N_DEV = 8
MESH = pl.DeviceIdType.MESH
LANES = 128
HALO3 = 16
HALO31 = 32
ROW_CHUNK = 32

ANY = pl.BlockSpec(memory_space=pl.ANY)


def _pick(n, pref, mult=LANES):
    if n <= pref:
        return n
    t = (pref // mult) * mult
    while t >= mult:
        if n % t == 0:
            return t
        t -= mult
    return n


def _params(*sem):
    return pltpu.CompilerParams(dimension_semantics=sem)


def _matmul(a, b, *, mode, out_dtype, name, b_lead=None, b_split=False, out_split=False, add=None,
            add_scale=1.0, tm=512, tn=1024, tk=None, hosted=None):
    halves = isinstance(a, tuple) or isinstance(b, tuple)
    if isinstance(a, tuple):
        assert mode == "nt" and b_split and tk is None
        ash = (a[0].shape[0], 2 * a[0].shape[1])
    else:
        ash = a.shape[-2:]
    if isinstance(b, tuple):
        assert mode == "tn" and tk is None
        bsh = (b[0].shape[0], 2 * b[0].shape[1])
    else:
        bsh = b.shape[-2:]
    if mode == "nn":
        (M, K), (K2, N) = ash, bsh
        if b_split:
            N = N * N_CHIPS
    elif mode == "nt":
        (M, K), (N, K2) = ash, bsh
        if b_split:
            K2 = K2 * N_CHIPS
    else:
        (K, M), (K2, N) = ash, bsh
    assert K == K2, (ash, bsh, mode)
    tm = _pick(M, tm)
    tn = _pick(N // N_CHIPS if (out_split or (b_split and mode == "nn")) else N, tn)
    whole_split_k = b_split and mode == "nt" and tk is None
    if tk is None:
        tk = K
    else:
        tk = _pick(K // N_CHIPS if (b_split and mode == "nt") else K, tk)
    nk = K // tk
    kq = K // N_CHIPS
    n_per = (N // N_CHIPS) // tn
    k_per = (K // N_CHIPS) // tk

    def lead(shape, idx):
        if b_lead is None:
            return pl.BlockSpec(shape, idx)
        return pl.BlockSpec((None,) + shape, lambda i, j, k: (b_lead,) + idx(i, j, k))

    if mode == "nn":
        a_spec = pl.BlockSpec((tm, tk), lambda i, j, k: (i, k))
        if b_split:
            b_spec = lead((None, tk, tn), lambda i, j, k: (lax.div(j, n_per), k, lax.rem(j, n_per)))
        else:
            b_spec = lead((tk, tn), lambda i, j, k: (k, j))
        dims = (((1,), (0,)), ((), ()))
    elif mode == "nt":
        a_spec = pl.BlockSpec((tm, tk), lambda i, j, k: (i, k))
        if whole_split_k:
            b_spec = lead((N_CHIPS, tn, kq), lambda i, j, k: (0, j, 0))
        elif b_split:
            b_spec = lead((None, tn, tk), lambda i, j, k: (lax.div(k, k_per), j, lax.rem(k, k_per)))
        else:
            b_spec = lead((tn, tk), lambda i, j, k: (j, k))
        dims = (((1,), (1,)), ((), ()))
    else:
        a_spec = pl.BlockSpec((tk, tm), lambda i, j, k: (k, i))
        b_spec = pl.BlockSpec((tk, tn), lambda i, j, k: (k, j))
        dims = (((0,), (0,)), ((), ()))
    if out_split:
        out_shape = jax.ShapeDtypeStruct((N_CHIPS, M, N // N_CHIPS), out_dtype)
        out_spec = pl.BlockSpec((None, tm, tn), lambda i, j, k: (lax.div(j, n_per), i, lax.rem(j, n_per)))
    else:
        out_shape = jax.ShapeDtypeStruct((M, N), out_dtype)
        out_spec = pl.BlockSpec((tm, tn), lambda i, j, k: (i, j))
    grid = (M // tm, N // tn, nk)
    nj_half = grid[1] // 2
    if isinstance(a, tuple):
        in_specs = [pl.BlockSpec((tm, K // 2), lambda i, j, k: (i, 0))] * 2 + [b_spec]
        args = [a[0], a[1], b]
    elif isinstance(b, tuple):
        in_specs = [a_spec,
                    pl.BlockSpec((tk, tn), lambda i, j, k: (k, jnp.minimum(j, nj_half - 1))),
                    pl.BlockSpec((tk, tn), lambda i, j, k: (k, jnp.maximum(j - nj_half, 0)))]
        args = [a, b[0], b[1]]
    else:
        in_specs = [a_spec, b_spec]
        args = [a, b]
    n_op = len(args)
    if add is not None:
        in_specs.append(pl.BlockSpec((tm, tn), lambda i, j, k: (i, j)))
        args.append(add)

    n_in = len(args)
    h_in = 0 if hosted is None else len(hosted.ins)
    h_out = 0 if hosted is None else len(hosted.outs)

    def body(*refs):
        ops = refs[:n_op]
        add_ref = refs[n_op] if add is not None else None
        h_ins = refs[n_in:n_in + h_in]
        o_ref = refs[n_in + h_in]
        h_outs = refs[n_in + h_in + 1:n_in + h_in + 1 + h_out]
        scr = refs[n_in + h_in + 1 + h_out:]
        i, j, k = pl.program_id(0), pl.program_id(1), pl.program_id(2)
        if hosted is not None:
            sems = scr[len(scr) - len(hosted.sems):]

            @pl.when((i == 0) & (j == 0) & (k == 0))
            def _():
                hosted.start(h_ins, h_outs, sems)

        def finish(res):
            if add_ref is not None:
                res = res + add_scale * add_ref[...]
            o_ref[...] = res.astype(out_dtype)

        def dot(x, y):
            return lax.dot_general(x, y, dims, preferred_element_type=F32)

        if isinstance(b, tuple):
            @pl.when(j < nj_half)
            def _():
                finish(dot(ops[0][...], ops[1][...]))

            @pl.when(j >= nj_half)
            def _():
                finish(dot(ops[0][...], ops[2][...]))
            part = None
        elif whole_split_k:
            srcs = [(ops[0], s) for s in range(N_CHIPS)] if not isinstance(a, tuple) else \
                   [(ops[s // 2], s % 2) for s in range(N_CHIPS)]
            b_ref = ops[-1]
            part = None
            for s, (src, off) in enumerate(srcs):
                term = dot(src[:, off * kq:(off + 1) * kq], b_ref[s])
                part = term if part is None else part + term
        else:
            part = dot(ops[0][...], ops[1][...])

        if part is None:
            pass
        elif nk == 1:
            finish(part)
        else:
            acc = scr[0]

            @pl.when(k == 0)
            def _():
                acc[...] = part

            @pl.when(k > 0)
            def _():
                acc[...] += part

            @pl.when(k == nk - 1)
            def _():
                finish(acc[...])

        if hosted is not None:
            @pl.when((i == grid[0] - 1) & (j == grid[1] - 1) & (k == nk - 1))
            def _():
                hosted.finish(h_ins, h_outs, sems)

    scratch = [pltpu.VMEM((tm, tn), F32)] if nk > 1 else []
    if hosted is not None:
        res = pl.pallas_call(
            body, name=name,
            out_shape=(out_shape,) + tuple(hosted.outs),
            grid=grid,
            in_specs=in_specs + [ANY] * h_in,
            out_specs=(out_spec,) + (ANY,) * h_out,
            input_output_aliases={n_in + src: 1 + dst for src, dst in hosted.alias.items()},
            scratch_shapes=scratch + [pltpu.SemaphoreType.DMA((n,)) for n in hosted.sems],
            compiler_params=_params("arbitrary", "arbitrary", "arbitrary"),
        )(*args, *hosted.ins)
        return res[0], list(res[1:])
    return pl.pallas_call(
        body, name=name,
        out_shape=out_shape,
        grid=grid,
        in_specs=in_specs,
        out_specs=out_spec,
        scratch_shapes=scratch,
        compiler_params=_params("parallel", "parallel", "arbitrary"),
    )(*args)


def _cast_bf16(w, name):
    L, R, C = w.shape
    tr, tc = _pick(R, 512, 16), _pick(C, 1408)

    def body(w_ref, o_ref):
        o_ref[...] = w_ref[...].astype(BF16)

    return pl.pallas_call(
        body, name=name, out_shape=jax.ShapeDtypeStruct(w.shape, BF16),
        grid=(L, R // tr, C // tc),
        in_specs=[pl.BlockSpec((None, tr, tc), lambda l, i, j: (l, i, j))],
        out_specs=pl.BlockSpec((None, tr, tc), lambda l, i, j: (l, i, j)),
        compiler_params=_params("parallel", "parallel", "parallel"),
    )(w)


def _cast_into_gather(pos, w, layer, name):
    L, R, C = w.shape
    r2 = R // 2
    tr, tc = _pick(r2, 512, 16), _pick(C, 1408)

    def body(p_ref, w_ref, o_ref):
        o_ref[...] = w_ref[...].astype(BF16)

    return pl.pallas_call(
        body, name=name, out_shape=jax.ShapeDtypeStruct((1, N_CHIPS, 2, r2, C), BF16),
        grid_spec=pltpu.PrefetchScalarGridSpec(
            num_scalar_prefetch=1, grid=(2, r2 // tr, C // tc),
            in_specs=[pl.BlockSpec((None, None, tr, tc), lambda h, i, j, p: (layer, h, i, j))],
            out_specs=pl.BlockSpec((None, None, None, tr, tc), lambda h, i, j, p: (0, p[0], h, i, j))),
        compiler_params=_params("parallel", "parallel", "parallel"),
    )(pos, w.reshape(L, 2, r2, C))


def _sigmoid(v):
    return 0.5 * jnp.tanh(0.5 * v) + 0.5


def _ln_fwd(x, y, g, b, alpha, name):
    S, D = x.shape
    tr = _pick(S, 256, 8)

    def body(x_ref, y_ref, g_ref, b_ref, o_ref, ob_ref, xh_ref, rs_ref):
        r = alpha * x_ref[...] + y_ref[...]
        mu = jnp.mean(r, axis=-1, keepdims=True)
        d = r - mu
        var = jnp.mean(d * d, axis=-1, keepdims=True)
        rstd = lax.rsqrt(var + LN_EPS)
        xh = d * rstd
        o = xh * g_ref[...] + b_ref[...]
        o_ref[...] = o
        ob_ref[...] = o.astype(BF16)
        xh_ref[...] = xh
        rs_ref[...] = rstd

    row = pl.BlockSpec((tr, D), lambda i: (i, 0))
    vec = pl.BlockSpec((1, D), lambda i: (0, 0))
    return pl.pallas_call(
        body, name=name,
        out_shape=(jax.ShapeDtypeStruct((S, D), F32), jax.ShapeDtypeStruct((S, D), BF16),
                   jax.ShapeDtypeStruct((S, D), F32), jax.ShapeDtypeStruct((S, 1), F32)),
        grid=(S // tr,),
        in_specs=[row, row, vec, vec],
        out_specs=(row, row, row, pl.BlockSpec((tr, 1), lambda i: (i, 0))),
        compiler_params=_params("parallel"),
    )(x, y, g, b)


def _ln_bwd(dout, xhat, rstd, g, name):
    S, D = dout.shape
    tr = _pick(S, 256, 8)

    def body(do_ref, xh_ref, rs_ref, g_ref, dr_ref, drb_ref, dg_ref, db_ref):
        i = pl.program_id(0)
        do = do_ref[...]
        xh = xh_ref[...]
        dxh = do * g_ref[...]
        m1 = jnp.mean(dxh, axis=-1, keepdims=True)
        m2 = jnp.mean(dxh * xh, axis=-1, keepdims=True)
        dr = rs_ref[...] * (dxh - m1 - xh * m2)
        dr_ref[...] = dr
        drb_ref[...] = dr.astype(BF16)
        pg = jnp.sum(do * xh, axis=0, keepdims=True)
        pb = jnp.sum(do, axis=0, keepdims=True)

        @pl.when(i == 0)
        def _():
            dg_ref[...] = pg
            db_ref[...] = pb

        @pl.when(i > 0)
        def _():
            dg_ref[...] += pg
            db_ref[...] += pb

    row = pl.BlockSpec((tr, D), lambda i: (i, 0))
    vec = pl.BlockSpec((1, D), lambda i: (0, 0))
    return pl.pallas_call(
        body, name=name,
        out_shape=(jax.ShapeDtypeStruct((S, D), F32), jax.ShapeDtypeStruct((S, D), BF16),
                   jax.ShapeDtypeStruct((1, D), F32), jax.ShapeDtypeStruct((1, D), F32)),
        grid=(S // tr,),
        in_specs=[row, row, pl.BlockSpec((tr, 1), lambda i: (i, 0)), vec],
        out_specs=(row, row, vec, vec),
        compiler_params=_params("arbitrary"),
    )(dout, xhat, rstd, g)


def _ln_loss_bwd(x, y, g, b, target, alpha, name):
    S, D = x.shape
    tr = _pick(S, 256, 8)
    n = S // tr

    def body(x_ref, y_ref, g_ref, b_ref, t_ref, dr_ref, drb_ref, dg_ref, db_ref, l_ref, acc):
        i = pl.program_id(0)
        r = alpha * x_ref[...] + y_ref[...]
        mu = jnp.mean(r, axis=-1, keepdims=True)
        d = r - mu
        var = jnp.mean(d * d, axis=-1, keepdims=True)
        rstd = lax.rsqrt(var + LN_EPS)
        xh = d * rstd
        err = xh * g_ref[...] + b_ref[...] - t_ref[...]
        do = err * (1.0 / D)
        dxh = do * g_ref[...]
        m1 = jnp.mean(dxh, axis=-1, keepdims=True)
        m2 = jnp.mean(dxh * xh, axis=-1, keepdims=True)
        dr = rstd * (dxh - m1 - xh * m2)
        dr_ref[...] = dr
        drb_ref[...] = dr.astype(BF16)
        pg = jnp.sum(do * xh, axis=0, keepdims=True)
        pb = jnp.sum(do, axis=0, keepdims=True)
        pe = jnp.sum(err * err, axis=0, keepdims=True)

        @pl.when(i == 0)
        def _():
            dg_ref[...] = pg
            db_ref[...] = pb
            acc[...] = pe

        @pl.when(i > 0)
        def _():
            dg_ref[...] += pg
            db_ref[...] += pb
            acc[...] += pe

        @pl.when(i == n - 1)
        def _():
            l_ref[...] = (0.5 / D) * jnp.sum(acc[...], axis=1, keepdims=True)

    row = pl.BlockSpec((tr, D), lambda i: (i, 0))
    vec = pl.BlockSpec((1, D), lambda i: (0, 0))
    return pl.pallas_call(
        body, name=name,
        out_shape=(jax.ShapeDtypeStruct((S, D), F32), jax.ShapeDtypeStruct((S, D), BF16),
                   jax.ShapeDtypeStruct((1, D), F32), jax.ShapeDtypeStruct((1, D), F32),
                   jax.ShapeDtypeStruct((1, 1), F32)),
        grid=(n,),
        in_specs=[row, row, vec, vec, row],
        out_specs=(row, row, vec, vec, pl.BlockSpec((1, 1), lambda i: (0, 0))),
        scratch_shapes=[pltpu.VMEM((1, D), F32)],
        compiler_params=_params("arbitrary"),
    )(x, y, g, b, target)


def _prev_spec(tr, halo, width, col):
    return pl.BlockSpec((halo, width), lambda c, i: (jnp.maximum(i * (tr // halo) - 1, 0), col(c)))


def _next_spec(tr, halo, width, col, nrows):
    last = nrows // halo - 1
    return pl.BlockSpec((halo, width), lambda c, i: (jnp.minimum((i + 1) * (tr // halo), last), col(c)))


def _cur_spec(tr, width, col):
    return pl.BlockSpec((tr, width), lambda c, i: (i, col(c)))


def _ffn_act_fwd(hu, conv_w, conv_b, name):
    S, F2 = hu.shape
    F = F2 // 2
    tr, tc, H = _pick(S, 512, 16), _pick(F, 512), HALO3
    nc, nr = F // tc, S // tr
    rc = min(ROW_CHUNK, tr)

    def body(gp_ref, g_ref, u_ref, w_ref, b_ref, z_ref, G):
        i = pl.program_id(1)
        G[0:H, :] = jnp.where(i > 0, gp_ref[...].astype(F32), 0.0)
        G[H:H + tr, :] = g_ref[...].astype(F32)
        w0, w1, w2, b = w_ref[pl.ds(0, 1), :], w_ref[pl.ds(1, 1), :], w_ref[pl.ds(2, 1), :], b_ref[...]
        for r0 in range(0, tr, rc):
            gc = b + w0 * G[pl.ds(H - 2 + r0, rc), :] + w1 * G[pl.ds(H - 1 + r0, rc), :] + w2 * G[pl.ds(H + r0, rc), :]
            z = gc * _sigmoid(gc) * u_ref[pl.ds(r0, rc), :].astype(F32)
            z_ref[pl.ds(r0, rc), :] = z.astype(BF16)

    gcol = lambda c: c
    ucol = lambda c: c + nc
    return pl.pallas_call(
        body, name=name, out_shape=jax.ShapeDtypeStruct((S, F), BF16),
        grid=(nc, nr),
        in_specs=[_prev_spec(tr, H, tc, gcol), _cur_spec(tr, tc, gcol), _cur_spec(tr, tc, ucol),
                  pl.BlockSpec((3, tc), lambda c, i: (0, c)), pl.BlockSpec((1, tc), lambda c, i: (0, c))],
        out_specs=pl.BlockSpec((tr, tc), lambda c, i: (i, c)),
        scratch_shapes=[pltpu.VMEM((H + tr, tc), F32)],
        compiler_params=_params("parallel", "parallel"),
    )(hu, hu, hu, conv_w, conv_b)


def _ffn_act_bwd(dz, hu, conv_w, conv_b, name):
    S, F = dz.shape
    tr, tc, H = _pick(S, 512, 16), _pick(F, 512), HALO3
    nc, nr = F // tc, S // tr
    n = tr + H
    rc = min(ROW_CHUNK, tr)

    def body(dz_ref, dzn_ref, gp_ref, g_ref, gn_ref, u_ref, un_ref, w_ref, b_ref,
             dg_ref, du_ref, dw_ref, db_ref, G, DG):
        i = pl.program_id(1)
        G[0:H, :] = jnp.where(i > 0, gp_ref[...].astype(F32), 0.0)
        G[H:H + tr, :] = g_ref[...].astype(F32)
        G[H + tr:H + tr + H, :] = gn_ref[...].astype(F32)
        w0, w1, w2, b = w_ref[pl.ds(0, 1), :], w_ref[pl.ds(1, 1), :], w_ref[pl.ds(2, 1), :], b_ref[...]

        def fold(v):
            return jnp.sum(v.reshape(v.shape[0] // 8, 8, tc), axis=0)

        def d_gate(r0, rows, dzf, uf):
            taps = [G[pl.ds(H - 2 + k + r0, rows), :] for k in range(3)]
            gc = b + w0 * taps[0] + w1 * taps[1] + w2 * taps[2]
            sg = _sigmoid(gc)
            return dzf * uf * (sg * (1.0 + gc * (1.0 - sg))), gc * sg, taps

        acc_w = [jnp.zeros((8, tc), F32) for _ in range(3)]
        acc_b = jnp.zeros((8, tc), F32)
        for r0 in range(0, tr, rc):
            dzf = dz_ref[pl.ds(r0, rc), :].astype(F32)
            dgc, silu, taps = d_gate(r0, rc, dzf, u_ref[pl.ds(r0, rc), :].astype(F32))
            du_ref[pl.ds(r0, rc), :] = (dzf * silu).astype(BF16)
            DG[pl.ds(r0, rc), :] = dgc
            acc_w = [acc_w[k] + fold(dgc * taps[k]) for k in range(3)]
            acc_b = acc_b + fold(dgc)
        dzn = jnp.where(i < nr - 1, dzn_ref[...].astype(F32), 0.0)
        DG[pl.ds(tr, H), :] = d_gate(tr, H, dzn, un_ref[...].astype(F32))[0]
        for r0 in range(0, tr, rc):
            dg = w2 * DG[pl.ds(r0, rc), :] + w1 * DG[pl.ds(r0 + 1, rc), :] + w0 * DG[pl.ds(r0 + 2, rc), :]
            dg_ref[pl.ds(r0, rc), :] = dg.astype(BF16)
        pw = [jnp.sum(a, axis=0, keepdims=True) for a in acc_w]
        pb = jnp.sum(acc_b, axis=0, keepdims=True)

        @pl.when(i == 0)
        def _():
            for k in range(3):
                dw_ref[pl.ds(k, 1), :] = pw[k]
            db_ref[...] = pb

        @pl.when(i > 0)
        def _():
            for k in range(3):
                dw_ref[pl.ds(k, 1), :] += pw[k]
            db_ref[...] += pb

    gcol = lambda c: c
    ucol = lambda c: c + nc
    blk = pl.BlockSpec((tr, tc), lambda c, i: (i, c))
    return pl.pallas_call(
        body, name=name,
        out_shape=(jax.ShapeDtypeStruct((S, F), BF16), jax.ShapeDtypeStruct((S, F), BF16),
                   jax.ShapeDtypeStruct((3, F), F32), jax.ShapeDtypeStruct((1, F), F32)),
        grid=(nc, nr),
        in_specs=[_cur_spec(tr, tc, gcol), _next_spec(tr, H, tc, gcol, S),
                  _prev_spec(tr, H, tc, gcol), _cur_spec(tr, tc, gcol), _next_spec(tr, H, tc, gcol, S),
                  _cur_spec(tr, tc, ucol), _next_spec(tr, H, tc, ucol, S),
                  pl.BlockSpec((3, tc), lambda c, i: (0, c)), pl.BlockSpec((1, tc), lambda c, i: (0, c))],
        out_specs=(blk, blk, pl.BlockSpec((3, tc), lambda c, i: (0, c)), pl.BlockSpec((1, tc), lambda c, i: (0, c))),
        scratch_shapes=[pltpu.VMEM((H + tr + H, tc), F32), pltpu.VMEM((n, tc), F32)],
        compiler_params=_params("parallel", "arbitrary"),
    )(dz, dz, hu, hu, hu, hu, hu, conv_w, conv_b)


def _softplus_neg(s):
    return jnp.minimum(-s, 0.0) - jnp.log(1.0 + jnp.exp(-jnp.abs(s)))


def _hilo_dot(v, m):
    hi = v.astype(BF16)
    lo = (v - hi.astype(F32)).astype(BF16)
    return (jnp.dot(hi, m, preferred_element_type=F32) + jnp.dot(lo, m, preferred_element_type=F32))


def _attn_fwd(h, n_heads, name, gather=()):
    S = h.shape[0]
    dh = HEAD_DIM
    A = n_heads * dh
    tq = _pick(S, 256)
    nq = S // tq
    scale = 1.0 / float(dh) ** 0.5
    ng = len(gather)
    hp = 2 if n_heads % 2 == 0 else 1
    n_grp, hw = n_heads // hp, hp * dh

    def body(*refs):
        q_ref, k_ref, v_ref = refs[:3]
        o_ref, tot_ref = refs[3 + ng:5 + ng]
        full = refs[5 + ng:5 + 2 * ng]
        hd = pl.program_id(0)
        i = pl.program_id(1)
        if ng:
            ssem, rsem = refs[5 + 2 * ng:]

            @pl.when((hd == 0) & (i == 0))
            def _():
                _gather_start(full, ssem, rsem)

            @pl.when((hd == n_grp - 1) & (i == 0))
            def _():
                _gather_forward(full, ssem, rsem)

        heads = range(hp)
        qs = [q_ref[:, h * dh:(h + 1) * dh] for h in heads]
        r_io = lax.broadcasted_iota(jnp.int32, (tq, tq), 0)
        c_io = lax.broadcasted_iota(jnp.int32, (tq, tq), 1)
        later = (r_io > c_io).astype(BF16)
        causal = c_io < r_io

        def rows(ref, j):
            blk = ref[pl.ds(pl.multiple_of(j * tq, tq), tq), :]
            return [blk[:, h * dh:(h + 1) * dh] for h in heads]

        def qk(kj):
            return [lax.dot_general(qs[h], kj[h], (((1,), (1,)), ((), ())), preferred_element_type=F32) * scale
                    for h in heads]

        def log_weights(s, diag):
            base, tot = [], []
            for h in heads:
                ls = _softplus_neg(s[h])
                if diag:
                    ls = jnp.where(causal, ls, 0.0)
                cs = _hilo_dot(ls, later)
                b = s[h] + ls + cs
                base.append(jnp.where(causal, b, -1e30) if diag else b)
                tot.append(cs[:, 0:1] + ls[:, 0:1])
            return base, tot

        def weigh(vj, acc, run, base):
            out = []
            for h in heads:
                w = jnp.exp(base[h] + run[h])
                out.append(acc[h] + jnp.dot(w.astype(BF16), vj[h], preferred_element_type=F32))
            return out

        def trip(t, carry):
            acc, run, base, tot = carry
            j = i - 1 - t
            s = qk(rows(k_ref, j))
            acc = weigh(rows(v_ref, j + 1), acc, run, base)
            base_n, tot_n = log_weights(s, False)
            return acc, [run[h] + tot[h] for h in heads], base_n, tot_n

        base, tot = log_weights(qk(rows(k_ref, i)), True)
        carry = ([jnp.zeros((tq, dh), F32) for _ in heads], [jnp.zeros((tq, 1), F32) for _ in heads], base, tot)
        acc, run, base, tot = lax.fori_loop(0, i, trip, carry)
        acc = weigh(rows(v_ref, 0), acc, run, base)
        for h in heads:
            o_ref[:, h * dh:(h + 1) * dh] = acc[h].astype(BF16)
            tot_ref[h] = jnp.broadcast_to(run[h] + tot[h], (tq, LANES))
        if ng:
            @pl.when((hd == n_grp - 1) & (i == nq - 1))
            def _():
                _gather_finish(full, ssem, rsem)

    T = _gather_items(gather) if ng else 0
    return pl.pallas_call(
        body, name=name,
        out_shape=(jax.ShapeDtypeStruct((S, A), BF16), jax.ShapeDtypeStruct((n_heads, S, LANES), F32))
        + tuple(jax.ShapeDtypeStruct(b.shape, b.dtype) for b in gather),
        grid=(n_grp, nq),
        in_specs=[pl.BlockSpec((tq, hw), lambda hd, i: (i, hd)),
                  pl.BlockSpec((S, hw), lambda hd, i: (0, n_grp + hd)),
                  pl.BlockSpec((S, hw), lambda hd, i: (0, 2 * n_grp + hd))] + [ANY] * ng,
        out_specs=(pl.BlockSpec((tq, hw), lambda hd, i: (i, hd)),
                   pl.BlockSpec((hp, tq, LANES), lambda hd, i: (hd, i, 0))) + (ANY,) * ng,
        input_output_aliases={3 + a: 2 + a for a in range(ng)},
        scratch_shapes=[pltpu.SemaphoreType.DMA((6 * T,)), pltpu.SemaphoreType.DMA((6 * T,))] if ng else [],
        compiler_params=_params("arbitrary", "arbitrary") if ng else _params("parallel", "parallel"),
    )(h, h, h, *gather)


def _attn_bwd(h, do, tot, n_heads, name, hosted=None):
    S = h.shape[0]
    dh = HEAD_DIM
    A = n_heads * dh
    tq = _pick(S, 256)
    nq = S // tq
    scale = 1.0 / float(dh) ** 0.5
    nt_dims = (((1,), (1,)), ((), ()))
    tn_dims = (((0,), (0,)), ((), ()))
    hp = 2 if n_heads % 2 == 0 else 1
    n_grp, hw = n_heads // hp, hp * dh
    h_in = 0 if hosted is None else len(hosted.ins)
    h_out = 0 if hosted is None else len(hosted.outs)

    def body(*refs):
        q_ref, k_ref, v_ref, do_ref, tot_ref = refs[:5]
        h_ins = refs[5:5 + h_in]
        dq_ref, dk_ref, dv_ref = refs[5 + h_in:8 + h_in]
        h_outs = refs[8 + h_in:8 + h_in + h_out]
        dk_acc, dv_acc = refs[8 + h_in + h_out:10 + h_in + h_out]
        sems = refs[10 + h_in + h_out:]
        hd = pl.program_id(0)
        i = pl.program_id(1)
        if hosted is not None:
            @pl.when((hd == 0) & (i == 0))
            def _():
                hosted.start(h_ins, h_outs, sems)

        @pl.when(i == 0)
        def _():
            dk_acc[...] = jnp.zeros_like(dk_acc)
            dv_acc[...] = jnp.zeros_like(dv_acc)

        heads = range(hp)
        qs = [q_ref[:, h * dh:(h + 1) * dh] for h in heads]
        dos = [do_ref[:, h * dh:(h + 1) * dh] for h in heads]
        total = [tot_ref[h][:, 0:1] for h in heads]
        r_io = lax.broadcasted_iota(jnp.int32, (tq, tq), 0)
        c_io = lax.broadcasted_iota(jnp.int32, (tq, tq), 1)
        upto = (r_io <= c_io).astype(BF16)
        before = (r_io < c_io).astype(BF16)
        causal = c_io < r_io

        def rows(ref, j):
            blk = ref[pl.ds(pl.multiple_of(j * tq, tq), tq), :]
            return [blk[:, h * dh:(h + 1) * dh] for h in heads]

        def qk(kj):
            return [lax.dot_general(qs[h], kj[h], nt_dims, preferred_element_type=F32) * scale for h in heads]

        def weights(base, prun, vj):
            dw = [lax.dot_general(dos[h], vj[h], nt_dims, preferred_element_type=F32) for h in heads]
            w, e, ce = [], [], []
            for h in heads:
                w.append(jnp.exp(base[h] + (total[h] - prun[h])))
                e.append(dw[h] * w[h])
                ce.append(jnp.dot(e[h].astype(BF16), before, preferred_element_type=F32))
            return w, e, ce

        def prefix(s, j):
            keep = jnp.logical_or(causal, j != i)
            ls = [jnp.where(keep, _softplus_neg(s[h]), 0.0) for h in heads]
            return keep, ls, [_hilo_dot(ls[h], upto) for h in heads]

        def grads(j, kj, dq, erun, w, e, ce, sn):
            start = pl.multiple_of(j * tq, tq)
            out = []
            for h in heads:
                ecum = ce[h] + erun[h]
                dz = e[h] * sn[h] - (1.0 - sn[h]) * ecum
                ds = (dz * scale).astype(BF16)
                cols = slice(h * dh, (h + 1) * dh)
                dv_acc[pl.ds(start, tq), cols] += lax.dot_general(w[h].astype(BF16), dos[h], tn_dims,
                                                                  preferred_element_type=F32)
                out.append(dq[h] + jnp.dot(ds, kj[h], preferred_element_type=F32))
                dk_acc[pl.ds(start, tq), cols] += lax.dot_general(ds, qs[h], tn_dims, preferred_element_type=F32)
            return out, [erun[h] + ce[h][:, tq - 1:tq] + e[h][:, tq - 1:tq] for h in heads]

        def carried(s, keep, ls, cs):
            base = [jnp.where(keep, s[h] + ls[h] - cs[h], -1e30) for h in heads]
            return base, [jnp.exp(ls[h]) for h in heads], [cs[h][:, tq - 1:tq] for h in heads]

        def trip(j, carry):
            dq, prun, erun, base, sn, ptot = carry
            s_n = qk(rows(k_ref, j + 1))
            w, e, ce = weights(base, prun, rows(v_ref, j))
            keep, ls_n, cs = prefix(s_n, j + 1)
            dq, erun = grads(j, rows(k_ref, j), dq, erun, w, e, ce, sn)
            return (dq, [prun[h] + ptot[h] for h in heads], erun) + carried(s_n, keep, ls_n, cs)

        zeros = [jnp.zeros((tq, 1), F32) for _ in heads]
        s0 = qk(rows(k_ref, 0))
        first = carried(s0, *prefix(s0, 0))
        carry = lax.fori_loop(0, i, trip, ([jnp.zeros((tq, dh), F32) for _ in heads], zeros, zeros) + first)
        dq, prun, erun, base, sn, _ = carry
        dq, _ = grads(i, rows(k_ref, i), dq, erun, *weights(base, prun, rows(v_ref, i)), sn)
        for h in heads:
            dq_ref[:, h * dh:(h + 1) * dh] = dq[h].astype(BF16)

        @pl.when(i == nq - 1)
        def _():
            dk_ref[...] = dk_acc[...].astype(BF16)
            dv_ref[...] = dv_acc[...].astype(BF16)

        if hosted is not None:
            @pl.when((hd == n_grp - 1) & (i == nq - 1))
            def _():
                hosted.finish(h_ins, h_outs, sems)

    qblk = pl.BlockSpec((tq, hw), lambda hd, i: (i, hd))
    full = pl.BlockSpec((S, hw), lambda hd, i: (0, hd))
    scratch = [pltpu.VMEM((S, hw), F32), pltpu.VMEM((S, hw), F32)]
    if hosted is not None:
        scratch += [pltpu.SemaphoreType.DMA((n,)) for n in hosted.sems]
    res = pl.pallas_call(
        body, name=name,
        out_shape=tuple(jax.ShapeDtypeStruct((S, A), BF16) for _ in range(3))
        + (tuple(hosted.outs) if hosted is not None else ()),
        grid=(n_grp, nq),
        in_specs=[qblk,
                  pl.BlockSpec((S, hw), lambda hd, i: (0, n_grp + hd)),
                  pl.BlockSpec((S, hw), lambda hd, i: (0, 2 * n_grp + hd)),
                  qblk,
                  pl.BlockSpec((hp, tq, LANES), lambda hd, i: (hd, i, 0))] + [ANY] * h_in,
        out_specs=(qblk, full, full) + (ANY,) * h_out,
        input_output_aliases={} if hosted is None else {5 + a: 3 + b for a, b in hosted.alias.items()},
        scratch_shapes=scratch,
        compiler_params=_params("arbitrary", "arbitrary") if hosted is not None else _params("parallel", "arbitrary"),
    )(h, h, h, do, tot, *(hosted.ins if hosted is not None else ()))
    return res[0], res[1], res[2], list(res[3:])


def _evenconv_fwd(h, dw_w, dw_b, bn_g, bn_b, name):
    S = h.shape[0]
    KW, A = dw_w.shape
    H = HALO31
    tr = _pick(S, 256, H)
    first_tap = H - (KW - 1)

    def body(ap_ref, a_ref, gp_ref, g_ref, w_ref, b_ref, bg_ref, bb_ref, u1_ref, u3_ref, U):
        i = pl.program_id(1)
        glu_prev = ap_ref[...].astype(F32) * _sigmoid(gp_ref[...].astype(F32))
        U[0:H, :] = jnp.where(i > 0, glu_prev, 0.0)
        U[H:H + tr, :] = a_ref[...].astype(F32) * _sigmoid(g_ref[...].astype(F32))
        acc = b_ref[...] + w_ref[pl.ds(0, 1), :] * U[pl.ds(first_tap, tr), :]
        for k in range(1, KW):
            acc = acc + w_ref[pl.ds(k, 1), :] * U[pl.ds(first_tap + k, tr), :]
        u1_ref[...] = acc
        mu = jnp.mean(acc, axis=-1, keepdims=True)
        d = acc - mu
        var = jnp.mean(d * d, axis=-1, keepdims=True)
        u2 = d * lax.rsqrt(var + LN_EPS) * bg_ref[...] + bb_ref[...]
        u3_ref[...] = (u2 * _sigmoid(u2)).astype(BF16)

    acol = lambda c: 3
    gcol = lambda c: 4
    vec = pl.BlockSpec((1, A), lambda c, i: (0, 0))
    blk = pl.BlockSpec((tr, A), lambda c, i: (i, 0))
    return pl.pallas_call(
        body, name=name,
        out_shape=(jax.ShapeDtypeStruct((S, A), F32), jax.ShapeDtypeStruct((S, A), BF16)),
        grid=(1, S // tr),
        in_specs=[_prev_spec(tr, H, A, acol), _cur_spec(tr, A, acol),
                  _prev_spec(tr, H, A, gcol), _cur_spec(tr, A, gcol),
                  pl.BlockSpec((KW, A), lambda c, i: (0, 0)), vec, vec, vec],
        out_specs=(blk, blk),
        scratch_shapes=[pltpu.VMEM((H + tr, A), F32)],
        compiler_params=_params("parallel", "parallel"),
    )(h, h, h, h, dw_w, dw_b, bn_g, bn_b)


def _evenconv_bwd(du3, u1, h, dw_w, bn_g, bn_b, name):
    S = h.shape[0]
    KW, A = dw_w.shape
    H = HALO31
    tr = _pick(S, 256, H)
    nr = S // tr
    n = tr + H
    first_tap = H - (KW - 1)

    def body(d3_ref, d3n_ref, u1_ref, u1n_ref, ap_ref, a_ref, gp_ref, g_ref, w_ref, bg_ref, bb_ref,
             da_ref, dg_ref, dww_ref, dwb_ref, dbg_ref, dbb_ref, U0, DU):
        i = pl.program_id(1)
        u1 = jnp.concatenate([u1_ref[...], u1n_ref[...]], axis=0)
        d3 = jnp.concatenate([d3_ref[...], d3n_ref[...]], axis=0).astype(F32)
        rows = lax.broadcasted_iota(jnp.int32, (n, 1), 0)
        d3 = jnp.where((rows < tr) | (i < nr - 1), d3, 0.0)
        mu = jnp.mean(u1, axis=-1, keepdims=True)
        d = u1 - mu
        var = jnp.mean(d * d, axis=-1, keepdims=True)
        rstd = lax.rsqrt(var + LN_EPS)
        xh = d * rstd
        u2 = xh * bg_ref[...] + bb_ref[...]
        sg = _sigmoid(u2)
        du2 = d3 * (sg * (1.0 + u2 * (1.0 - sg)))
        dxh = du2 * bg_ref[...]
        m1 = jnp.mean(dxh, axis=-1, keepdims=True)
        m2 = jnp.mean(dxh * xh, axis=-1, keepdims=True)
        du1 = rstd * (dxh - m1 - xh * m2)
        DU[...] = du1
        pbg = jnp.sum(du2[0:tr] * xh[0:tr], axis=0, keepdims=True)
        pbb = jnp.sum(du2[0:tr], axis=0, keepdims=True)
        pwb = jnp.sum(du1[0:tr], axis=0, keepdims=True)

        glu_prev = ap_ref[...].astype(F32) * _sigmoid(gp_ref[...].astype(F32))
        U0[0:H, :] = jnp.where(i > 0, glu_prev, 0.0)
        a = a_ref[...].astype(F32)
        sgg = _sigmoid(g_ref[...].astype(F32))
        U0[H:H + tr, :] = a * sgg

        @pl.when(i == 0)
        def _():
            dbg_ref[...] = pbg
            dbb_ref[...] = pbb
            dwb_ref[...] = pwb
            dww_ref[...] = jnp.zeros_like(dww_ref)

        @pl.when(i > 0)
        def _():
            dbg_ref[...] += pbg
            dbb_ref[...] += pbb
            dwb_ref[...] += pwb

        du0 = w_ref[pl.ds(0, 1), :] * DU[pl.ds(KW - 1, tr), :]
        for k in range(1, KW):
            du0 = du0 + w_ref[pl.ds(k, 1), :] * DU[pl.ds(KW - 1 - k, tr), :]
        da_ref[...] = (du0 * sgg).astype(BF16)
        dg_ref[...] = (du0 * a * sgg * (1.0 - sgg)).astype(BF16)
        dcur = DU[pl.ds(0, tr), :]
        for k in range(KW):
            dww_ref[pl.ds(k, 1), :] += jnp.sum(dcur * U0[pl.ds(first_tap + k, tr), :], axis=0, keepdims=True)

    acol = lambda c: 3
    gcol = lambda c: 4
    one = lambda c: 1
    zero = lambda c: 0
    vec = pl.BlockSpec((1, A), lambda c, i: (0, 0))
    blk = pl.BlockSpec((tr, A), lambda c, i: (i, 0))
    return pl.pallas_call(
        body, name=name,
        out_shape=(jax.ShapeDtypeStruct((S, A), BF16), jax.ShapeDtypeStruct((S, A), BF16),
                   jax.ShapeDtypeStruct((KW, A), F32), jax.ShapeDtypeStruct((1, A), F32),
                   jax.ShapeDtypeStruct((1, A), F32), jax.ShapeDtypeStruct((1, A), F32)),
        grid=(1, nr),
        in_specs=[_cur_spec(tr, A, one), _next_spec(tr, H, A, one, S),
                  _cur_spec(tr, A, zero), _next_spec(tr, H, A, zero, S),
                  _prev_spec(tr, H, A, acol), _cur_spec(tr, A, acol),
                  _prev_spec(tr, H, A, gcol), _cur_spec(tr, A, gcol),
                  pl.BlockSpec((KW, A), lambda c, i: (0, 0)), vec, vec],
        out_specs=(blk, blk, pl.BlockSpec((KW, A), lambda c, i: (0, 0)), vec, vec, vec),
        scratch_shapes=[pltpu.VMEM((H + tr, A), F32), pltpu.VMEM((n, A), F32)],
        compiler_params=_params("arbitrary", "arbitrary"),
    )(du3, du3, u1, u1, h, h, h, h, dw_w, bn_g, bn_b)


def _pool_inv_count(row0, nrows, window):
    t = row0 + lax.broadcasted_iota(jnp.int32, (nrows, 1), 0)
    return 1.0 / jnp.minimum(t + 1, window).astype(F32)


def _odd_fwd(h, conv_w, pool_w, pool_scale, name):
    S = h.shape[0]
    C = conv_w.shape[1]
    G = len(POOL_WINDOWS)
    Dg = C // G
    H = HALO3
    tr = _pick(S, 256, H)

    def body(cb_ref, ccp_ref, cc_ref, chp_ref, ch_ref, pp_ref, p_ref, w_ref, pw_ref, sc_ref, mix_ref, M, P):
        i = pl.program_id(1)
        M[0:H, :] = jnp.where(i > 0, ccp_ref[...].astype(F32) * chp_ref[...].astype(F32), 0.0)
        M[H:H + tr, :] = cc_ref[...].astype(F32) * ch_ref[...].astype(F32)
        cm = (w_ref[pl.ds(0, 1), :] * M[pl.ds(H - 2, tr), :] + w_ref[pl.ds(1, 1), :] * M[pl.ds(H - 1, tr), :]
              + w_ref[pl.ds(2, 1), :] * M[pl.ds(H, tr), :])
        mix_ref[:, 0:C] = (cb_ref[...].astype(F32) * cm).astype(BF16)
        P[0:H, :] = jnp.where(i > 0, pp_ref[...].astype(F32), 0.0)
        P[H:H + tr, :] = p_ref[...].astype(F32)
        for gi, window in enumerate(POOL_WINDOWS):
            cols = pl.ds(gi * Dg, Dg)
            wsum = P[pl.ds(H, tr), cols]
            for dlt in range(1, window):
                wsum = wsum + P[pl.ds(H - dlt, tr), cols]
            diff = wsum * _pool_inv_count(i * tr, tr, window) - P[pl.ds(H, tr), cols]
            yd = jnp.dot(diff.astype(BF16), pw_ref[gi], preferred_element_type=F32) * sc_ref[:, cols]
            mix_ref[:, pl.ds(C + gi * Dg, Dg)] = yd.astype(BF16)

    col = lambda k: (lambda c: k)
    return pl.pallas_call(
        body, name=name, out_shape=jax.ShapeDtypeStruct((S, 2 * C), BF16),
        grid=(1, S // tr),
        in_specs=[_cur_spec(tr, C, col(0)),
                  _prev_spec(tr, H, C, col(1)), _cur_spec(tr, C, col(1)),
                  _prev_spec(tr, H, C, col(2)), _cur_spec(tr, C, col(2)),
                  _prev_spec(tr, H, C, col(3)), _cur_spec(tr, C, col(3)),
                  pl.BlockSpec((3, C), lambda c, i: (0, 0)),
                  pl.BlockSpec((G, Dg, Dg), lambda c, i: (0, 0, 0)),
                  pl.BlockSpec((1, C), lambda c, i: (0, 0))],
        out_specs=pl.BlockSpec((tr, 2 * C), lambda c, i: (i, 0)),
        scratch_shapes=[pltpu.VMEM((H + tr, C), F32), pltpu.VMEM((H + tr, C), F32)],
        compiler_params=_params("parallel", "parallel"),
    )(h, h, h, h, h, h, h, conv_w, pool_w, pool_scale)


def _odd_bwd(dmix, h, conv_w, pool_w, pool_scale, name):
    S = h.shape[0]
    C = conv_w.shape[1]
    G = len(POOL_WINDOWS)
    Dg = C // G
    H = HALO3
    tr = _pick(S, 256, H)
    nr = S // tr
    n = tr + H
    nt_dims = (((1,), (1,)), ((), ()))
    tn_dims = (((0,), (0,)), ((), ()))

    def body(dyc_ref, dycn_ref, dyd_ref, dydn_ref, cb_ref, cbn_ref, ccp_ref, cc_ref, ccn_ref,
             chp_ref, ch_ref, chn_ref, pp_ref, p_ref, w_ref, pw_ref, sc_ref,
             dh_ref, dw_ref, dpw_ref, dsc_ref, M, DCM, P, Q):
        i = pl.program_id(1)
        rows = lax.broadcasted_iota(jnp.int32, (n, 1), 0)
        valid = (rows < tr) | (i < nr - 1)

        @pl.when(i == 0)
        def _():
            dw_ref[...] = jnp.zeros_like(dw_ref)
            dpw_ref[...] = jnp.zeros_like(dpw_ref)
            dsc_ref[...] = jnp.zeros_like(dsc_ref)

        M[0:H, :] = jnp.where(i > 0, ccp_ref[...].astype(F32) * chp_ref[...].astype(F32), 0.0)
        cc = cc_ref[...].astype(F32)
        ch = ch_ref[...].astype(F32)
        M[H:H + tr, :] = cc * ch
        M[H + tr:H + tr + H, :] = ccn_ref[...].astype(F32) * chn_ref[...].astype(F32)
        w0, w1, w2 = w_ref[pl.ds(0, 1), :], w_ref[pl.ds(1, 1), :], w_ref[pl.ds(2, 1), :]
        cm = w0 * M[pl.ds(H - 2, tr), :] + w1 * M[pl.ds(H - 1, tr), :] + w2 * M[pl.ds(H, tr), :]
        dyc = jnp.concatenate([dyc_ref[...], dycn_ref[...]], axis=0).astype(F32)
        dyc = jnp.where(valid, dyc, 0.0)
        cbf = jnp.concatenate([cb_ref[...], cbn_ref[...]], axis=0).astype(F32)
        dh_ref[:, 0:C] = (dyc[0:tr] * cm).astype(BF16)
        DCM[...] = dyc * cbf
        dm = w2 * DCM[pl.ds(0, tr), :] + w1 * DCM[pl.ds(1, tr), :] + w0 * DCM[pl.ds(2, tr), :]
        dh_ref[:, C:2 * C] = (dm * ch).astype(BF16)
        dh_ref[:, 2 * C:3 * C] = (dm * cc).astype(BF16)
        dcur = DCM[pl.ds(0, tr), :]
        for k in range(3):
            dw_ref[pl.ds(k, 1), :] += jnp.sum(dcur * M[pl.ds(H - 2 + k, tr), :], axis=0, keepdims=True)

        P[0:H, :] = jnp.where(i > 0, pp_ref[...].astype(F32), 0.0)
        P[H:H + tr, :] = p_ref[...].astype(F32)
        dyd = jnp.concatenate([dyd_ref[...], dydn_ref[...]], axis=0).astype(F32)
        dyd = jnp.where(valid, dyd, 0.0)
        for gi, window in enumerate(POOL_WINDOWS):
            cols = pl.ds(gi * Dg, Dg)
            lo = gi * Dg
            wsum = P[pl.ds(H, tr), cols]
            for dlt in range(1, window):
                wsum = wsum + P[pl.ds(H - dlt, tr), cols]
            diff = (wsum * _pool_inv_count(i * tr, tr, window) - P[pl.ds(H, tr), cols]).astype(BF16)
            pw = pw_ref[gi]
            dyd_g = dyd[:, lo:lo + Dg]
            e = (dyd_g * sc_ref[:, cols]).astype(BF16)
            yraw = jnp.dot(diff, pw, preferred_element_type=F32)
            dsc_ref[:, cols] += jnp.sum(dyd_g[0:tr] * yraw, axis=0, keepdims=True)
            dpw_ref[gi] += lax.dot_general(diff, e[0:tr], tn_dims, preferred_element_type=F32)
            ddiff = lax.dot_general(e, pw, nt_dims, preferred_element_type=F32)
            Q[:, cols] = ddiff * _pool_inv_count(i * tr, n, window)
            acc = Q[pl.ds(0, tr), cols]
            for dlt in range(1, window):
                acc = acc + Q[pl.ds(dlt, tr), cols]
            dh_ref[:, pl.ds(3 * C + lo, Dg)] = (acc - ddiff[0:tr]).astype(BF16)

    col = lambda k: (lambda c: k)
    return pl.pallas_call(
        body, name=name,
        out_shape=(jax.ShapeDtypeStruct((S, 4 * C), BF16), jax.ShapeDtypeStruct((3, C), F32),
                   jax.ShapeDtypeStruct((G, Dg, Dg), F32), jax.ShapeDtypeStruct((1, C), F32)),
        grid=(1, nr),
        in_specs=[_cur_spec(tr, C, col(0)), _next_spec(tr, H, C, col(0), S),
                  _cur_spec(tr, C, col(1)), _next_spec(tr, H, C, col(1), S),
                  _cur_spec(tr, C, col(0)), _next_spec(tr, H, C, col(0), S),
                  _prev_spec(tr, H, C, col(1)), _cur_spec(tr, C, col(1)), _next_spec(tr, H, C, col(1), S),
                  _prev_spec(tr, H, C, col(2)), _cur_spec(tr, C, col(2)), _next_spec(tr, H, C, col(2), S),
                  _prev_spec(tr, H, C, col(3)), _cur_spec(tr, C, col(3)),
                  pl.BlockSpec((3, C), lambda c, i: (0, 0)),
                  pl.BlockSpec((G, Dg, Dg), lambda c, i: (0, 0, 0)),
                  pl.BlockSpec((1, C), lambda c, i: (0, 0))],
        out_specs=(pl.BlockSpec((tr, 4 * C), lambda c, i: (i, 0)),
                   pl.BlockSpec((3, C), lambda c, i: (0, 0)),
                   pl.BlockSpec((G, Dg, Dg), lambda c, i: (0, 0, 0)),
                   pl.BlockSpec((1, C), lambda c, i: (0, 0))),
        scratch_shapes=[pltpu.VMEM((H + tr + H, C), F32), pltpu.VMEM((n, C), F32),
                        pltpu.VMEM((H + tr, C), F32), pltpu.VMEM((n, C), F32)],
        compiler_params=_params("arbitrary", "arbitrary"),
    )(dmix, dmix, dmix, dmix, h, h, h, h, h, h, h, h, h, h, conv_w, pool_w, pool_scale)


def _adamw(w, grads, m, v, name):
    L, R, C = w.shape
    assert len(grads) == L
    tr, tc = _pick(R, 256, 8), _pick(C, 1408)
    ni, nj = R // tr, C // tc
    c1 = 1.0 / (1.0 - ADAM_B1 ** ADAM_STEP)
    c2 = 1.0 / (1.0 - ADAM_B2 ** ADAM_STEP)

    def g_spec(layer):
        def idx(l, i, j):
            before, after = l < layer, l > layer
            return (jnp.where(before, 0, jnp.where(after, ni - 1, i)),
                    jnp.where(before, 0, jnp.where(after, nj - 1, j)))
        return pl.BlockSpec((tr, tc), idx)

    def body(w_ref, *rest):
        g_refs = rest[:L]
        m_ref, v_ref, go_ref, d_ref, mo_ref, vo_ref = rest[L:]
        l = pl.program_id(0)
        gg = g_refs[0][...]
        for k in range(1, L):
            gg = jnp.where(l == k, g_refs[k][...], gg)
        mn = ADAM_B1 * m_ref[...] + (1.0 - ADAM_B1) * gg
        vn = ADAM_B2 * v_ref[...] + (1.0 - ADAM_B2) * (gg * gg)
        d_ref[...] = -ADAM_LR * ((mn * c1) / (jnp.sqrt(vn * c2) + ADAM_EPS) + ADAM_WD * w_ref[...])
        go_ref[...] = gg
        mo_ref[...] = mn
        vo_ref[...] = vn

    blk = pl.BlockSpec((None, tr, tc), lambda l, i, j: (l, i, j))
    sds = jax.ShapeDtypeStruct(w.shape, F32)
    return pl.pallas_call(
        body, name=name, out_shape=(sds, sds, sds, sds),
        grid=(L, R // tr, C // tc),
        in_specs=[blk] + [g_spec(k) for k in range(L)] + [blk, blk], out_specs=(blk, blk, blk, blk),
        compiler_params=_params("arbitrary", "arbitrary", "arbitrary"),
    )(w, *grads, m, v)


def _sum_slots(buf, name):
    N, R, C = buf.shape
    tr = _pick(R, 512, 8)

    def body(b_ref, o_ref):
        acc = b_ref[0]
        for k in range(1, N):
            acc = acc + b_ref[k]
        o_ref[...] = acc

    return pl.pallas_call(
        body, name=name, out_shape=jax.ShapeDtypeStruct((R, C), F32),
        grid=(R // tr,),
        in_specs=[pl.BlockSpec((N, tr, C), lambda i: (0, i, 0))],
        out_specs=pl.BlockSpec((tr, C), lambda i: (i, 0)),
        compiler_params=_params("parallel"),
    )(buf)


def _pair_sum(pos, g, rsib, name):
    _, hr, hc = rsib.shape
    tr, tc = _pick(hr, 512, 16), _pick(hc, 2816)

    def body(p_ref, g_ref, r_ref, o_ref):
        o_ref[...] = (g_ref[...].astype(F32) + r_ref[...].astype(F32)).astype(BF16)

    blk = pl.BlockSpec((None, tr, tc), lambda s, i, j, p: (s, i, j))
    return pl.pallas_call(
        body, name=name, out_shape=jax.ShapeDtypeStruct(rsib.shape, BF16),
        grid_spec=pltpu.PrefetchScalarGridSpec(
            num_scalar_prefetch=1, grid=(N_CHIPS, hr // tr, hc // tc),
            in_specs=[pl.BlockSpec((None, None, tr, tc), lambda s, i, j, p: (s, p[1], i, j)), blk],
            out_specs=blk),
        compiler_params=_params("parallel", "parallel", "parallel"),
    )(pos, g, rsib)


def _chip_sum(pos, part, land, name):
    _, sr, sc = land.shape
    tr, tc = _pick(sr, 256, 16), _pick(sc, 2816)

    def body(p_ref, own_ref, l_ref, o_ref):
        acc = own_ref[...].astype(F32)
        for k in range(3):
            acc = acc + l_ref[k].astype(F32)
        o_ref[...] = acc

    return pl.pallas_call(
        body, name=name, out_shape=jax.ShapeDtypeStruct((2, sr, sc), F32),
        grid_spec=pltpu.PrefetchScalarGridSpec(
            num_scalar_prefetch=1, grid=(sr // tr, sc // tc),
            in_specs=[pl.BlockSpec((None, tr, tc), lambda i, j, p: (p[0], i, j)),
                      pl.BlockSpec((3, tr, tc), lambda i, j, p: (0, i, j))],
            out_specs=pl.BlockSpec((None, tr, tc), lambda i, j, p: (p[1], i, j))),
        compiler_params=_params("parallel", "parallel"),
    )(pos, part, land)


def _place():
    x, y, c = lax.axis_index("x"), lax.axis_index("y"), lax.axis_index("c")
    return x, y, c


def _other_chips(x, y):
    return [(1 - x, y), (x, 1 - y), (1 - x, 1 - y)]


def _rcopy(src, dst, ssem, rsem, dev):
    return pltpu.make_async_remote_copy(src_ref=src, dst_ref=dst, send_sem=ssem, recv_sem=rsem,
                                        device_id=dev, device_id_type=MESH)


def _gather_items(bufs):
    return sum(b.shape[0] for b in bufs)


def _gather_walk(full):
    t = 0
    for ref in full:
        for l in range(ref.shape[0]):
            yield t, ref, l
            t += 1


def _gather_start(full, ssem, rsem):
    x, y, c = _place()
    j = 2 * x + y
    for t, ref, l in _gather_walk(full):
        own = ref.at[l, j, c]
        for r, (px, py) in enumerate(_other_chips(x, y)):
            _rcopy(own, own, ssem.at[6 * t + r], rsem.at[6 * t + r], (px, py, c)).start()


def _gather_forward(full, ssem, rsem):
    x, y, c = _place()
    for t, ref, l in _gather_walk(full):
        for r, (px, py) in enumerate(_other_chips(x, y)):
            slab = ref.at[l, 2 * px + py, c]
            _rcopy(slab, slab, ssem.at[6 * t + r], rsem.at[6 * t + r], (px, py, c)).wait_recv()
            _rcopy(slab, slab, ssem.at[6 * t + 3 + r], rsem.at[6 * t + 3 + r], (x, y, 1 - c)).start()


def _gather_finish(full, ssem, rsem):
    x, y, c = _place()
    j = 2 * x + y
    for t, ref, l in _gather_walk(full):
        for r, (px, py) in enumerate(_other_chips(x, y)):
            got = ref.at[l, 2 * px + py, 1 - c]
            _rcopy(got, got, ssem.at[6 * t + 3 + r], rsem.at[6 * t + 3 + r], (x, y, 1 - c)).wait_recv()
    for t, ref, l in _gather_walk(full):
        own = ref.at[l, j, c]
        for r, (px, py) in enumerate(_other_chips(x, y)):
            _rcopy(own, own, ssem.at[6 * t + r], rsem.at[6 * t + r], (px, py, c)).wait_send()
            slab = ref.at[l, 2 * px + py, c]
            _rcopy(slab, slab, ssem.at[6 * t + 3 + r], rsem.at[6 * t + 3 + r], (x, y, 1 - c)).wait_send()


def _land_shape(part):
    return jax.ShapeDtypeStruct((3,) + part.shape[1:], part.dtype)


def _exchange_start(parts, land, ssem, rsem):
    x, y, c = _place()
    for a in range(len(parts)):
        for r, (px, py) in enumerate(_other_chips(x, y)):
            _rcopy(parts[a].at[2 * px + py], land[a].at[r], ssem.at[3 * a + r], rsem.at[3 * a + r],
                   (px, py, c)).start()


def _exchange_finish(parts, land, ssem, rsem):
    x, y, c = _place()
    for a in range(len(parts)):
        for r, (px, py) in enumerate(_other_chips(x, y)):
            _rcopy(parts[a].at[2 * px + py], land[a].at[r], ssem.at[3 * a + r], rsem.at[3 * a + r],
                   (px, py, c)).wait()


_Hosted = collections.namedtuple("_Hosted", "ins outs alias sems start finish")


def _host_join(*hosts):
    ins, outs, alias, sems, spans = [], [], {}, [], []
    for h in hosts:
        spans.append((h, len(ins), len(outs), len(sems)))
        alias.update({len(ins) + a: len(outs) + b for a, b in h.alias.items()})
        ins, outs, sems = ins + list(h.ins), outs + list(h.outs), sems + list(h.sems)

    def each(step):
        def run(i, o, s):
            for h, a, b, c in spans:
                getattr(h, step)(i[a:a + len(h.ins)], o[b:b + len(h.outs)], s[c:c + len(h.sems)])
        return run

    return _Hosted(ins, outs, alias, sems, each("start"), each("finish"))


def _host_exchange(parts):
    n = len(parts)
    return _Hosted(list(parts), [_land_shape(p) for p in parts], {}, [3 * n, 3 * n],
                   lambda ins, outs, sems: _exchange_start(ins, outs, *sems),
                   lambda ins, outs, sems: _exchange_finish(ins, outs, *sems))


def _host_gather(bufs):
    T = _gather_items(bufs)

    def finish(ins, outs, sems):
        _gather_forward(outs, *sems)
        _gather_finish(outs, *sems)

    return _Hosted(list(bufs), [jax.ShapeDtypeStruct(b.shape, b.dtype) for b in bufs],
                   {a: a for a in range(len(bufs))}, [6 * T, 6 * T],
                   lambda ins, outs, sems: _gather_start(outs, *sems), finish)


def _host_all_devices(buf):
    return _Hosted([buf], [jax.ShapeDtypeStruct((N_DEV,) + buf.shape, buf.dtype)], {}, [N_DEV - 1, N_DEV - 1, 1],
                   lambda ins, outs, sems: _all_devices_start(ins[0], outs[0], *sems),
                   lambda ins, outs, sems: _all_devices_finish(ins[0], outs[0], *sems))


def _allgather_big(bufs, name):
    n = len(bufs)
    T = _gather_items(bufs)

    def body(*refs):
        full = refs[n:2 * n]
        ssem, rsem = refs[2 * n:]
        _gather_start(full, ssem, rsem)
        _gather_forward(full, ssem, rsem)
        _gather_finish(full, ssem, rsem)

    return pl.pallas_call(
        body, name=name, out_shape=tuple(jax.ShapeDtypeStruct(b.shape, BF16) for b in bufs),
        in_specs=[ANY] * n, out_specs=tuple([ANY] * n),
        input_output_aliases={a: a for a in range(n)},
        scratch_shapes=[pltpu.SemaphoreType.DMA((6 * T,)), pltpu.SemaphoreType.DMA((6 * T,))],
    )(*bufs)


def _allgather_small(shards, name):
    n = len(shards)
    outs = tuple(jax.ShapeDtypeStruct((N_CHIPS,) + s.shape, s.dtype) for s in shards)

    def body(*refs):
        ins, full = refs[:n], refs[n:2 * n]
        ssem, rsem, lsem = refs[2 * n:]
        x, y, c = _place()
        j = 2 * x + y
        chips = _other_chips(x, y)
        cps, locs = [], []
        for a in range(n):
            loc = pltpu.make_async_copy(ins[a], full[a].at[j], lsem.at[a])
            loc.start()
            locs.append(loc)
            for r, (px, py) in enumerate(chips):
                cp = _rcopy(ins[a], full[a].at[j], ssem.at[3 * a + r], rsem.at[3 * a + r], (px, py, c))
                cp.start()
                cps.append(cp)
        for a in range(n):
            for r, (px, py) in enumerate(chips):
                dst = full[a].at[2 * px + py]
                _rcopy(dst, dst, ssem.at[3 * a + r], rsem.at[3 * a + r], (px, py, c)).wait_recv()
        for cp in cps:
            cp.wait_send()
        for loc in locs:
            loc.wait()

    return pl.pallas_call(
        body, name=name, out_shape=outs,
        in_specs=[ANY] * n, out_specs=tuple([ANY] * n),
        scratch_shapes=[pltpu.SemaphoreType.DMA((3 * n,)), pltpu.SemaphoreType.DMA((3 * n,)),
                        pltpu.SemaphoreType.DMA((n,))],
    )(*shards)


def _pair_copies(ins, got, ssem, rsem):
    x, y, c = _place()
    return [_rcopy(ins[a].at[s, 1 - c], got[a].at[s], ssem.at[N_CHIPS * a + s], rsem.at[N_CHIPS * a + s],
                   (x, y, 1 - c))
            for a in range(len(ins)) for s in range(N_CHIPS)]


def _host_pair_exchange(grads):
    n = len(grads)

    def start(ins, outs, sems):
        for cp in _pair_copies(ins, outs, *sems):
            cp.start()

    def finish(ins, outs, sems):
        for cp in _pair_copies(ins, outs, *sems):
            cp.wait()

    return _Hosted(list(grads), [jax.ShapeDtypeStruct((N_CHIPS,) + g.shape[2:], BF16) for g in grads], {},
                   [N_CHIPS * n, N_CHIPS * n], start, finish)


def _pair_exchange(grads, name):
    n = len(grads)
    host = _host_pair_exchange(grads)

    def body(*refs):
        ins, got, sems = refs[:n], refs[n:2 * n], refs[2 * n:]
        host.start(ins, got, sems)
        host.finish(ins, got, sems)

    return pl.pallas_call(
        body, name=name, out_shape=tuple(host.outs),
        in_specs=[ANY] * n, out_specs=tuple([ANY] * n),
        scratch_shapes=[pltpu.SemaphoreType.DMA((k,)) for k in host.sems],
    )(*grads)


def _chip_exchange(parts, name):
    n = len(parts)

    def body(*refs):
        ins, land = refs[:n], refs[n:2 * n]
        ssem, rsem = refs[2 * n:]
        _exchange_start(ins, land, ssem, rsem)
        _exchange_finish(ins, land, ssem, rsem)

    return pl.pallas_call(
        body, name=name, out_shape=tuple(_land_shape(p) for p in parts),
        in_specs=[ANY] * n, out_specs=tuple([ANY] * n),
        scratch_shapes=[pltpu.SemaphoreType.DMA((3 * n,)), pltpu.SemaphoreType.DMA((3 * n,))],
    )(*parts)


def _host_half_swap(bufs):
    n = len(bufs)

    def start(ins, full, sems):
        x, y, c = _place()
        for t in range(n):
            mine = full[t].at[c]
            _rcopy(mine, mine, sems[0].at[t], sems[1].at[t], (x, y, 1 - c)).start()

    def finish(ins, full, sems):
        x, y, c = _place()
        for t in range(n):
            got = full[t].at[1 - c]
            _rcopy(got, got, sems[0].at[t], sems[1].at[t], (x, y, 1 - c)).wait_recv()
        for t in range(n):
            mine = full[t].at[c]
            _rcopy(mine, mine, sems[0].at[t], sems[1].at[t], (x, y, 1 - c)).wait_send()

    return _Hosted(list(bufs), [jax.ShapeDtypeStruct(b.shape, F32) for b in bufs], {a: a for a in range(n)},
                   [n, n], start, finish)


def _half_swap(bufs, name):
    n = len(bufs)
    host = _host_half_swap(bufs)

    def body(*refs):
        ins, full, sems = refs[:n], refs[n:2 * n], refs[2 * n:]
        host.start(ins, full, sems)
        host.finish(ins, full, sems)

    return pl.pallas_call(
        body, name=name, out_shape=tuple(host.outs),
        in_specs=[ANY] * n, out_specs=tuple([ANY] * n),
        input_output_aliases=host.alias,
        scratch_shapes=[pltpu.SemaphoreType.DMA((k,)) for k in host.sems],
    )(*bufs)


def _flipped(x, y, c, m):
    fx, fy, fc = (m >> 2) & 1, (m >> 1) & 1, m & 1
    return x + fx - 2 * x * fx, y + fy - 2 * y * fy, c + fc - 2 * c * fc


def _all_devices_start(b_ref, o_ref, ssem, rsem, lsem):
    x, y, c = _place()
    me = 4 * x + 2 * y + c
    pltpu.make_async_copy(b_ref, o_ref.at[me], lsem.at[0]).start()
    for m in range(1, N_DEV):
        _rcopy(b_ref, o_ref.at[me], ssem.at[m - 1], rsem.at[m - 1], _flipped(x, y, c, m)).start()


def _all_devices_finish(b_ref, o_ref, ssem, rsem, lsem):
    x, y, c = _place()
    me = 4 * x + 2 * y + c
    for m in range(1, N_DEV):
        px, py, pc = _flipped(x, y, c, m)
        got = o_ref.at[4 * px + 2 * py + pc]
        _rcopy(got, got, ssem.at[m - 1], rsem.at[m - 1], (px, py, pc)).wait_recv()
    for m in range(1, N_DEV):
        _rcopy(b_ref, o_ref.at[me], ssem.at[m - 1], rsem.at[m - 1], _flipped(x, y, c, m)).wait_send()
    pltpu.make_async_copy(b_ref, o_ref.at[me], lsem.at[0]).wait()


def _pack(arrs):
    flat = jnp.concatenate([a.reshape(-1) for a in arrs])
    rows = -(-flat.shape[0] // (8 * LANES)) * 8
    flat = jnp.pad(flat, (0, rows * LANES - flat.shape[0]))
    return flat.reshape(rows, LANES)


def _unpack(buf, shapes):
    flat = buf.reshape(-1)
    out, off = [], 0
    for s in shapes:
        size = 1
        for d in s:
            size *= d
        out.append(flat[off:off + size].reshape(s))
        off += size
    return out


BIG = ("ev_w_in", "ev_w_out", "od_w_in", "od_w_out", "ffn_w_up", "ffn_w_down")
BIG_KIND = {"ev_w_in": "col", "ev_w_out": "row", "od_w_in": "col", "od_w_out": "row",
            "ffn_w_up": "col", "ffn_w_down": "row"}
SMALL_AXIS = {"ev_dw_w": 2, "ev_dw_b": None, "ev_bn_g": None, "ev_bn_b": None, "od_conv_w": 2,
              "od_pool_w": 2, "od_pool_scale": 1, "ffn_conv_w": 2, "ffn_conv_b": None, "ln_g": 2, "ln_b": 2}
WEIGHTS = ("ev_w_in", "ev_dw_w", "ev_dw_b", "ev_bn_g", "ev_bn_b", "ev_w_out", "od_w_in", "od_conv_w",
           "od_pool_w", "od_pool_scale", "od_w_out", "ffn_w_up", "ffn_conv_w", "ffn_conv_b", "ffn_w_down",
           "ln_g", "ln_b")


def _ffn_fwd(xb, w_up, w_down, conv_w, conv_b, tag, host_up=None, host_down=None):
    hu = _matmul(xb, w_up, mode="nn", b_lead=0, b_split=True, out_dtype=BF16, name=f"{tag}_up", tm=1024, tn=1408,
                 hosted=host_up)
    hu, up_outs = hu if host_up is not None else (hu, None)
    z = _ffn_act_fwd(hu, conv_w, conv_b, name=f"{tag}_act")
    y = _matmul(z, w_down, mode="nn", b_lead=0, out_dtype=F32, name=f"{tag}_down", tm=512, tn=1024,
                hosted=host_down)
    y, down_outs = y if host_down is not None else (y, None)
    return hu, z, y, up_outs, down_outs


def _ffn_bwd(drb, dr, alpha, xb, hu, z, w_up, w_down, conv_w, conv_b, tag, host_dwup=None, host_dx=None):
    g_down = _matmul(z, drb, mode="tn", out_dtype=BF16, name=f"{tag}_dwdown", tm=512, tn=1024)
    dz = _matmul(drb, w_down, mode="nt", b_lead=0, out_dtype=BF16, name=f"{tag}_dz", tm=1024, tn=1408)
    dg, du, dcw, dcb = _ffn_act_bwd(dz, hu, conv_w, conv_b, name=f"{tag}_actbwd")
    g_up = _matmul(xb, (dg, du), mode="tn", out_split=True, out_dtype=BF16, name=f"{tag}_dwup", tm=512, tn=1408,
                   hosted=host_dwup)
    g_up, dwup_outs = g_up if host_dwup is not None else (g_up, None)
    dx = _matmul((dg, du), w_up, mode="nt", b_lead=0, b_split=True, out_dtype=F32, add=dr, add_scale=alpha,
                 name=f"{tag}_dx", tm=512, tn=512, hosted=host_dx)
    dx, dx_outs = dx if host_dx is not None else (dx, None)
    return dx, g_up, g_down, dcw, dcb, dwup_outs, dx_outs


def kernel(x, ev_w_in, ev_dw_w, ev_dw_b, ev_bn_g, ev_bn_b, ev_w_out, od_w_in, od_conv_w, od_pool_w, od_pool_scale, od_w_out, ffn_w_up, ffn_conv_w, ffn_conv_b, ffn_w_down, ln_g, ln_b, loss_target, m_ev_w_in, m_ev_dw_w, m_ev_dw_b, m_ev_bn_g, m_ev_bn_b, m_ev_w_out, m_od_w_in, m_od_conv_w, m_od_pool_w, m_od_pool_scale, m_od_w_out, m_ffn_w_up, m_ffn_conv_w, m_ffn_conv_b, m_ffn_w_down, m_ln_g, m_ln_b, v_ev_w_in, v_ev_dw_w, v_ev_dw_b, v_ev_bn_g, v_ev_bn_b, v_ev_w_out, v_od_w_in, v_od_conv_w, v_od_pool_w, v_od_pool_scale, v_od_w_out, v_ffn_w_up, v_ffn_conv_w, v_ffn_conv_b, v_ffn_w_down, v_ln_g, v_ln_b):
    wts = dict(ev_w_in=ev_w_in, ev_dw_w=ev_dw_w, ev_dw_b=ev_dw_b, ev_bn_g=ev_bn_g, ev_bn_b=ev_bn_b,
               ev_w_out=ev_w_out, od_w_in=od_w_in, od_conv_w=od_conv_w, od_pool_w=od_pool_w,
               od_pool_scale=od_pool_scale, od_w_out=od_w_out, ffn_w_up=ffn_w_up, ffn_conv_w=ffn_conv_w,
               ffn_conv_b=ffn_conv_b, ffn_w_down=ffn_w_down, ln_g=ln_g, ln_b=ln_b)
    mom = dict(ev_w_in=m_ev_w_in, ev_dw_w=m_ev_dw_w, ev_dw_b=m_ev_dw_b, ev_bn_g=m_ev_bn_g, ev_bn_b=m_ev_bn_b,
               ev_w_out=m_ev_w_out, od_w_in=m_od_w_in, od_conv_w=m_od_conv_w, od_pool_w=m_od_pool_w,
               od_pool_scale=m_od_pool_scale, od_w_out=m_od_w_out, ffn_w_up=m_ffn_w_up, ffn_conv_w=m_ffn_conv_w,
               ffn_conv_b=m_ffn_conv_b, ffn_w_down=m_ffn_w_down, ln_g=m_ln_g, ln_b=m_ln_b)
    var = dict(ev_w_in=v_ev_w_in, ev_dw_w=v_ev_dw_w, ev_dw_b=v_ev_dw_b, ev_bn_g=v_ev_bn_g, ev_bn_b=v_ev_bn_b,
               ev_w_out=v_ev_w_out, od_w_in=v_od_w_in, od_conv_w=v_od_conv_w, od_pool_w=v_od_pool_w,
               od_pool_scale=v_od_pool_scale, od_w_out=v_od_w_out, ffn_w_up=v_ffn_w_up, ffn_conv_w=v_ffn_conv_w,
               ffn_conv_b=v_ffn_conv_b, ffn_w_down=v_ffn_w_down, ln_g=v_ln_g, ln_b=v_ln_b)

    S, D = x.shape[1], x.shape[2]
    depth = ln_g.shape[0]
    alpha = (2.0 * depth) ** 0.25
    A = ev_dw_b.shape[-1]
    n_heads = A // HEAD_DIM
    xi, yi, ci = _place()
    chip = 2 * xi + yi
    pos = jnp.stack([chip, ci]).astype(jnp.int32)

    bufs = {f"{k}{l}": _cast_into_gather(pos, wts[k], l, name=f"cast_{k}{l}")
            for k in BIG for l in range(wts[k].shape[0])}

    def whole(key):
        _, r, c = wts[key[:-1]].shape
        col = BIG_KIND[key[:-1]] == "col"
        return bufs[key].reshape(1, N_CHIPS, r, c) if col else bufs[key].reshape(1, N_CHIPS * r, c)

    full = {}

    def gathered_now(keys, arrays):
        bufs.update(zip(keys, arrays))
        full.update({key: whole(key) for key in keys})

    under_attn = ("ffn_w_up0", "ffn_w_down0", "od_w_in0", "od_w_out0")
    gathered_now(("ev_w_in0",), _allgather_big([bufs["ev_w_in0"]], name="gather_first"))
    small_sharded = [k for k in WEIGHTS if k not in BIG and SMALL_AXIS[k] is not None]
    gathered = _allgather_small([wts[k] for k in small_sharded], name="gather_small")
    sm = {k: wts[k] for k in WEIGHTS if k not in BIG and SMALL_AXIS[k] is None}
    for k, g4 in zip(small_sharded, gathered):
        sm[k] = jnp.concatenate([g4[t] for t in range(N_CHIPS)], axis=SMALL_AXIS[k])
    pool_w_bf = sm["od_pool_w"][0].astype(BF16)

    x0 = x[0]
    x0b = _cast_bf16(x, name="cast_x")[0]
    h0, got = _matmul(x0b, full["ev_w_in0"], mode="nn", b_lead=0, b_split=True, out_dtype=BF16, name="ev_in",
                      tm=1024, tn=1280, hosted=_host_gather([bufs["ev_w_out0"]]))
    gathered_now(("ev_w_out0",), got)
    o_a, tot, *rest = _attn_fwd(h0, n_heads, name="attn_fwd", gather=[bufs[k] for k in under_attn])
    gathered_now(under_attn, rest)
    u1, u3 = _evenconv_fwd(h0, sm["ev_dw_w"][0], sm["ev_dw_b"], sm["ev_bn_g"], sm["ev_bn_b"], name="evconv_fwd")
    mix0 = jnp.concatenate([o_a, u3], axis=1)
    y1 = _matmul(mix0, full["ev_w_out0"], mode="nn", b_lead=0, out_dtype=F32, name="ev_out", tm=1024, tn=1024)
    x1, x1b, xh1, rs1 = _ln_fwd(x0, y1, sm["ln_g"][0, 0][None], sm["ln_b"][0, 0][None], alpha, name="ln00")
    hu0, z0, y2, got_up, got_down = _ffn_fwd(
        x1b, full["ffn_w_up0"], full["ffn_w_down0"], sm["ffn_conv_w"][0], sm["ffn_conv_b"][0][None], "ffn0",
        host_up=_host_gather([bufs["ffn_w_up1"]]), host_down=_host_gather([bufs["ffn_w_down1"]]))
    gathered_now(("ffn_w_up1",), got_up)
    gathered_now(("ffn_w_down1",), got_down)
    x2, x2b, xh2, rs2 = _ln_fwd(x1, y2, sm["ln_g"][0, 1][None], sm["ln_b"][0, 1][None], alpha, name="ln01")
    h1 = _matmul(x2b, full["od_w_in0"], mode="nn", b_lead=0, b_split=True, out_dtype=BF16, name="od_in",
                 tm=1024, tn=1024)
    mix1 = _odd_fwd(h1, sm["od_conv_w"][0], pool_w_bf, sm["od_pool_scale"], name="odd_fwd")
    y3 = _matmul(mix1, full["od_w_out0"], mode="nn", b_lead=0, out_dtype=F32, name="od_out", tm=1024, tn=1024)
    x3, x3b, xh3, rs3 = _ln_fwd(x2, y3, sm["ln_g"][1, 0][None], sm["ln_b"][1, 0][None], alpha, name="ln10")
    hu1, z1, y4, _, _ = _ffn_fwd(x3b, full["ffn_w_up1"], full["ffn_w_down1"], sm["ffn_conv_w"][1],
                                 sm["ffn_conv_b"][1][None], "ffn1")

    dr4, dr4b, dg11, db11, loss_part = _ln_loss_bwd(x3, y4, sm["ln_g"][1, 1][None], sm["ln_b"][1, 1][None],
                                                    loss_target[0], alpha, name="ln11_loss")
    loss = lax.psum(loss_part[0, 0], ("x", "y", "c"))

    def halves_view(named):
        g4 = []
        for k, g in named:
            rows, cols = (g.shape[1], g.shape[2]) if BIG_KIND[k] == "col" else (g.shape[0] // N_CHIPS, g.shape[1])
            g4.append(g.reshape(N_CHIPS, 2, rows // 2, cols))
        return g4

    def pair_sums(g4, sib, tag):
        return [_pair_sum(pos, g, r, name=f"grad_pair_sum_{tag}{t}") for t, (g, r) in enumerate(zip(g4, sib))]

    def pair_reduce(named, tag):
        g4 = halves_view(named)
        return pair_sums(g4, _pair_exchange(g4, name=f"grad_pair_exchange_{tag}"), tag)

    dx3, g_up1, g_down1, dcw1, dcb1, _, _ = _ffn_bwd(dr4b, dr4, alpha, x3b, hu1, z1, full["ffn_w_up1"],
                                                     full["ffn_w_down1"], sm["ffn_conv_w"][1],
                                                     sm["ffn_conv_b"][1][None], "ffn1")
    dr3, dr3b, dg10, db10 = _ln_bwd(dx3, xh3, rs3, sm["ln_g"][1, 0][None], name="ln10_bwd")
    g4_f1 = halves_view([("ffn_w_up", g_up1), ("ffn_w_down", g_down1)])
    g_odout, sib_f1 = _matmul(mix1, dr3b, mode="tn", out_dtype=BF16, name="od_dwout", tm=512, tn=1024,
                              hosted=_host_pair_exchange(g4_f1))
    parts_f1 = pair_sums(g4_f1, sib_f1, "f1")
    dmix1 = _matmul(dr3b, full["od_w_out0"], mode="nt", b_lead=0, out_dtype=BF16, name="od_dmix", tm=1024, tn=1024)
    dh1, d_odconv, d_pool, d_pscale = _odd_bwd(dmix1, h1, sm["od_conv_w"][0], pool_w_bf, sm["od_pool_scale"],
                                               name="odd_bwd")
    g_odin = _matmul(x2b, dh1, mode="tn", out_split=True, out_dtype=BF16, name="od_dwin", tm=512, tn=1024)
    dx2 = _matmul(dh1, full["od_w_in0"], mode="nt", b_lead=0, b_split=True, out_dtype=F32, add=dr3, add_scale=alpha,
                  name="od_dx", tm=1024, tn=512)
    dr2, dr2b, dg01, db01 = _ln_bwd(dx2, xh2, rs2, sm["ln_g"][0, 1][None], name="ln01_bwd")
    dx1, g_up0, g_down0, dcw0, dcb0, land_up1, land_down1 = _ffn_bwd(
        dr2b, dr2, alpha, x1b, hu0, z0, full["ffn_w_up0"], full["ffn_w_down0"], sm["ffn_conv_w"][0],
        sm["ffn_conv_b"][0][None], "ffn0",
        host_dwup=_host_exchange(parts_f1[:1]), host_dx=_host_exchange(parts_f1[1:]))
    dr1, dr1b, dg00, db00 = _ln_bwd(dx1, xh1, rs1, sm["ln_g"][0, 0][None], name="ln00_bwd")
    g4_b = halves_view([("od_w_in", g_odin), ("od_w_out", g_odout), ("ffn_w_up", g_up0), ("ffn_w_down", g_down0)])
    g_evout, sib_b = _matmul(mix0, dr1b, mode="tn", out_dtype=BF16, name="ev_dwout", tm=512, tn=1024,
                             hosted=_host_pair_exchange(g4_b))
    g4_o = halves_view([("ev_w_out", g_evout)])
    dmix0, sib_o = _matmul(dr1b, full["ev_w_out0"], mode="nt", b_lead=0, out_dtype=BF16, name="ev_dmix",
                           tm=1024, tn=1024, hosted=_host_pair_exchange(g4_o))
    da, dgate, d_dww, d_dwb, d_bng, d_bnb = _evenconv_bwd(dmix0, u1, h0, sm["ev_dw_w"][0], sm["ev_bn_g"],
                                                          sm["ev_bn_b"], name="evconv_bwd")
    parts_b = pair_sums(g4_b + g4_o, list(sib_b) + list(sib_o), "b")

    d_ln_g = jnp.stack([jnp.stack([dg00[0], dg01[0]]), jnp.stack([dg10[0], dg11[0]])])
    d_ln_b = jnp.stack([jnp.stack([db00[0], db01[0]]), jnp.stack([db10[0], db11[0]])])
    small_partial = {
        "ev_dw_w": d_dww[None], "ev_dw_b": d_dwb, "ev_bn_g": d_bng, "ev_bn_b": d_bnb,
        "od_conv_w": d_odconv[None], "od_pool_w": d_pool[None], "od_pool_scale": d_pscale,
        "ffn_conv_w": jnp.stack([dcw0, dcw1]), "ffn_conv_b": jnp.concatenate([dcb0, dcb1], axis=0),
        "ln_g": d_ln_g, "ln_b": d_ln_b}
    small_names = [k for k in WEIGHTS if k not in BIG]
    packed = _pack([small_partial[k] for k in small_names])
    dq, dk, dv, under_bwd = _attn_bwd(h0, dmix0, tot, n_heads, name="attn_bwd",
                                      hosted=_host_join(_host_exchange(parts_b), _host_all_devices(packed)))
    land_b, all_small = under_bwd[:-1], under_bwd[-1]
    dh0 = jnp.concatenate([dq, dk, dv, da, dgate], axis=1)
    g_evin = _matmul(x0b, dh0, mode="tn", out_split=True, out_dtype=BF16, name="ev_dwin", tm=512, tn=1280)
    parts_e = pair_reduce([("ev_w_in", g_evin)], "e")
    grad_x, land_e = _matmul(dh0, full["ev_w_in0"], mode="nt", b_lead=0, b_split=True, out_dtype=F32, add=dr1,
                             add_scale=alpha, name="ev_dx", tm=1024, tn=512, hosted=_host_exchange(parts_e))

    order = ["ffn_w_up1", "ffn_w_down1", "od_w_in0", "od_w_out0", "ffn_w_up0", "ffn_w_down0", "ev_w_out0", "ev_w_in0"]
    parts = parts_f1 + parts_b + parts_e
    land = list(land_up1) + list(land_down1) + list(land_b) + list(land_e)
    halves = [_chip_sum(pos, p, ld, name=f"grad_chip_sum_{tag}") for tag, p, ld in zip(order, parts, land)]
    reduced = dict(zip(order, _half_swap(halves, name="grad_half_swap")))
    big_grads = {k: [reduced[f"{k}{l}"].reshape(wts[k].shape[1:]) for l in range(wts[k].shape[0])] for k in BIG}

    summed = _sum_slots(all_small, name="sum_small_grads")
    small_full = dict(zip(small_names, _unpack(summed, [small_partial[k].shape for k in small_names])))
    small_grads = {}
    for k in small_names:
        ax = SMALL_AXIS[k]
        if ax is None:
            small_grads[k] = small_full[k]
        else:
            size = wts[k].shape[ax]
            small_grads[k] = lax.dynamic_slice_in_dim(small_full[k], chip * size, size, axis=ax)

    grads, delta, new_m, new_v = {}, {}, {}, {}
    for k in BIG:
        grads[k], delta[k], new_m[k], new_v[k] = _adamw(wts[k], big_grads[k], mom[k], var[k], name=f"adamw_{k}")
    shapes = [wts[k].shape for k in small_names]
    pw, pg, pm, pv = (_pack([d[k] for k in small_names]) for d in (wts, small_grads, mom, var))
    sg, sd, smn, svn = _adamw(pw[None], [pg], pm[None], pv[None], name="adamw_small")
    for dst, buf in ((grads, sg), (delta, sd), (new_m, smn), (new_v, svn)):
        for k, a in zip(small_names, _unpack(buf[0], shapes)):
            dst[k] = a

    return (loss, grad_x[None], *[grads[k] for k in WEIGHTS], *[delta[k] for k in WEIGHTS],
            *[new_m[k] for k in WEIGHTS], *[new_v[k] for k in WEIGHTS])
```
